```python
import jax, jax.numpy as jnp
from jax import lax
import numpy as np

D_MODEL = 1024
BATCH = 8
SEQ = 2048
DEPTH = 2
DEC_BATCH = 128
DEC_SEQ = 1
PAST_LEN = 16384
PAGE_SIZE = 128

N_EVEN = (DEPTH + 1) // 2
N_ODD = DEPTH // 2

GLA_HEADS = 4
GLA_DV = D_MODEL // 2 // GLA_HEADS
GLA_DK = GLA_DV // 2
GLA_KW = GLA_HEADS * GLA_DK
GLA_VW = GLA_HEADS * GLA_DV
GLA_GATE_RANK = 16
GLA_TAU = 16.0
GLA_CHUNK = 16
CONV_CH = D_MODEL // 2
CONV_WIDTH = 31
IN0_SIZES = (GLA_KW, GLA_KW, GLA_VW, GLA_VW, GLA_GATE_RANK, 2 * CONV_CH)
IN0_WIDTH = sum(IN0_SIZES)
MIX0_WIDTH = GLA_VW + CONV_CH
POOL_WINDOWS = (2, 4, 8, 16)
POOL_GROUPS = len(POOL_WINDOWS)
POOL_GC = D_MODEL // POOL_GROUPS
POOL_BUF = max(POOL_WINDOWS) - 1
MOE_GROUPS = 4
MOE_EXPERTS = 4
MOE_TOPK = 2
MOE_FF = 512
ALPHA = (2 * DEPTH) ** 0.25
BETA = (8 * DEPTH) ** -0.25
LN_EPS = 1e-5

kernel_name = "hybrid_gla_conformer_pool_hmoe_step"


def _standardize(x):
    xf = x.astype(jnp.float32)
    mu = jnp.mean(xf, -1, keepdims=True)
    var = jnp.mean(jnp.square(xf - mu), -1, keepdims=True)
    return ((xf - mu) * lax.rsqrt(var + LN_EPS)).astype(x.dtype)


def layer_norm(x, gain, bias):
    return _standardize(x) * gain + bias


def ada_mod(c, w, b):
    m = jax.nn.silu(c) @ w + b
    shift, scale, gate = jnp.split(m, 3, axis=-1)
    return shift[:, None], scale[:, None], gate[:, None]


def gla_recurrence(q, k, v, log_a, s0):
    B, L = q.shape[:2]
    C = min(GLA_CHUNK, L)
    pad = (-L) % C
    f32 = jnp.float32
    q, k, v, log_a = (t.astype(f32) for t in (q, k, v, log_a))
    if pad:
        pw = ((0, 0), (0, pad), (0, 0), (0, 0))
        q, k, v, log_a = (jnp.pad(t, pw) for t in (q, k, v, log_a))
    n = (L + pad) // C

    def to_chunks(t):
        return t.reshape(B, n, C, GLA_HEADS, t.shape[-1]).transpose(1, 0, 3, 2, 4)

    mask = jnp.tril(jnp.ones((C, C), dtype=bool))[:, :, None]

    def step(s, inp):
        qc, kc, vc, ac = inp
        b = jnp.cumsum(ac, axis=2)
        o_inter = jnp.einsum('bhck,bhkv->bhcv', qc * jnp.exp(b), s)
        diff = b[:, :, :, None, :] - b[:, :, None, :, :]
        dec = jnp.where(mask, jnp.exp(jnp.where(mask, diff, 0.0)), 0.0)
        att = jnp.einsum('bhtk,bhsk,bhtsk->bhts', qc, kc, dec)
        o_intra = jnp.einsum('bhts,bhsv->bhtv', att, vc)
        b_last = b[:, :, -1]
        s_new = jnp.exp(b_last)[..., None] * s + jnp.einsum(
            'bhsk,bhsv->bhkv', kc * jnp.exp(b_last[:, :, None] - b), vc)
        return s_new, o_inter + o_intra

    s_fin, o = lax.scan(step, s0.astype(f32),
                        (to_chunks(q), to_chunks(k), to_chunks(v), to_chunks(log_a)))
    o = o.transpose(1, 0, 3, 2, 4).reshape(B, n * C, GLA_HEADS, GLA_DV)[:, :L]
    return o, s_fin.astype(s0.dtype)


def mixer_even(h, s_gla, conv_buf, w_in, w_a2, b_a, gla_norm_g, conv_w, conv_b,
               conv_ln_g, conv_ln_b, w_out):
    B, L, _ = h.shape
    z = h @ w_in
    q, k, v, g, a_lr, u = jnp.split(z, np.cumsum(IN0_SIZES)[:-1].tolist(), axis=-1)
    q = q.reshape(B, L, GLA_HEADS, GLA_DK) * (GLA_DK ** -0.5)
    k = k.reshape(B, L, GLA_HEADS, GLA_DK)
    v = v.reshape(B, L, GLA_HEADS, GLA_DV)
    log_a = jax.nn.log_sigmoid((a_lr @ w_a2 + b_a).astype(jnp.float32)) / GLA_TAU
    log_a = log_a.reshape(B, L, GLA_HEADS, GLA_DK)
    o, s_new = gla_recurrence(q, k, v, log_a, s_gla)
    o = _standardize(o.astype(h.dtype)).reshape(B, L, GLA_VW) * gla_norm_g
    o = o * jax.nn.silu(g)
    ua, ug = jnp.split(u, 2, axis=-1)
    glu = ua * jax.nn.sigmoid(ug)
    full = jnp.concatenate([conv_buf.astype(glu.dtype), glu], axis=1)
    y = lax.conv_general_dilated(full, conv_w[:, None, :].astype(full.dtype), window_strides=(1,),
                                 padding='VALID', dimension_numbers=('NWC', 'WIO', 'NWC'),
                                 feature_group_count=CONV_CH) + conv_b
    y = jax.nn.silu(layer_norm(y, conv_ln_g, conv_ln_b))
    new_buf = full[:, -(CONV_WIDTH - 1):]
    return jnp.concatenate([o, y], axis=-1) @ w_out, s_new, new_buf


def mixer_odd(h, pool_buf, start_pos, w_grp, pool_scale, w_out):
    B, L, D = h.shape
    full = jnp.concatenate([pool_buf.astype(h.dtype), h], axis=1)
    cs = jnp.cumsum(full.astype(jnp.float32), axis=1)
    cs = jnp.concatenate([jnp.zeros((B, 1, D), jnp.float32), cs], axis=1)
    end = cs[:, POOL_BUF + 1:]
    pos = start_pos + jnp.arange(L)
    outs = []
    for gi, w in enumerate(POOL_WINDOWS):
        sl = slice(gi * POOL_GC, (gi + 1) * POOL_GC)
        start = cs[:, POOL_BUF + 1 - w: POOL_BUF + 1 - w + L, sl]
        cnt = jnp.minimum(w, pos + 1).astype(jnp.float32)[None, :, None]
        outs.append((end[..., sl] - start) / cnt)
    pooled = jnp.concatenate(outs, axis=-1).astype(h.dtype) - h
    mixed = jnp.einsum('blgc,gcd->blgd', pooled.reshape(B, L, POOL_GROUPS, POOL_GC), w_grp)
    mixed = mixed.reshape(B, L, D) * pool_scale
    return mixed @ w_out, full[:, -POOL_BUF:]


def hier_moe(h, w_coarse, b_coarse, w_fine, b_fine, w_gate, w_up, w_down):
    B, L, D = h.shape
    t = h.reshape(-1, D)
    pc = jax.nn.softmax((t @ w_coarse + b_coarse).astype(jnp.float32), axis=-1)
    pg, gidx = lax.top_k(pc, 1)
    fine_logits = jnp.einsum('td,gde->tge', t, w_fine) + b_fine
    fine_sel = jnp.take_along_axis(fine_logits, gidx[:, :, None], axis=1)[:, 0]
    pf = jax.nn.softmax(fine_sel.astype(jnp.float32), axis=-1)
    vals, eidx = lax.top_k(pf, MOE_TOPK)
    wts = pg * (vals / jnp.sum(vals, -1, keepdims=True))
    fine_w = jnp.sum(jax.nn.one_hot(eidx, MOE_EXPERTS) * wts[..., None], axis=1)
    comb = (jax.nn.one_hot(gidx[:, 0], MOE_GROUPS)[:, :, None] * fine_w[:, None, :]).astype(t.dtype)
    out = jnp.zeros_like(t)
    for g in range(MOE_GROUPS):
        hid = jax.nn.silu(jnp.einsum('td,edf->tef', t, w_gate[g])) * jnp.einsum('td,edf->tef', t, w_up[g])
        out = out + jnp.einsum('tef,efd->td', hid * comb[:, g, :, None], w_down[g])
    return out.reshape(B, L, D)


def trunk(x, c, s_gla, s_conv, s_pool, start_pos, p):
    new_gla, new_conv, new_pool = [], [], []
    for layer in range(DEPTH):
        i = layer // 2
        shift, scale, gate = ada_mod(c, p['w_ada'][layer, 0], p['b_ada'][layer, 0])
        h = x * (1 + scale) + shift
        if layer % 2 == 0:
            m, sg, sc = mixer_even(h, s_gla[i], s_conv[i], p['w_in_even'][i], p['w_a2'][i], p['b_a'][i],
                                   p['gla_norm_g'][i], p['conv_w'][i], p['conv_b'][i], p['conv_ln_g'][i],
                                   p['conv_ln_b'][i], p['w_out_even'][i])
            new_gla.append(sg)
            new_conv.append(sc)
        else:
            m, sp = mixer_odd(h, s_pool[i], start_pos, p['w_grp_pool'][i], p['pool_scale'][i],
                              p['w_out_odd'][i])
            new_pool.append(sp)
        x = layer_norm(ALPHA * x + (1 + gate) * m, p['ln_g'][layer, 0], p['ln_b'][layer, 0])
        shift, scale, gate = ada_mod(c, p['w_ada'][layer, 1], p['b_ada'][layer, 1])
        h = x * (1 + scale) + shift
        m = hier_moe(h, p['w_coarse'][layer], p['b_coarse'][layer], p['w_fine'][layer], p['b_fine'][layer],
                     p['w_gate'][layer], p['w_up'][layer], p['w_down'][layer])
        x = layer_norm(ALPHA * x + (1 + gate) * m, p['ln_g'][layer, 1], p['ln_b'][layer, 1])
    return x, jnp.stack(new_gla), jnp.stack(new_conv), jnp.stack(new_pool)


def setup_inputs(seed: int = 0) -> dict:
    key = jax.random.key(seed)
    ks = iter(jax.random.split(key, 40))
    nrm = lambda shape, s: jax.random.normal(next(ks), shape, jnp.float32) * s
    D = D_MODEL
    return {
        "x_prompt": nrm((BATCH, SEQ, D), 1.0),
        "x_sample": nrm((DEC_BATCH, DEC_SEQ, D), 1.0),
        "state_gla": nrm((N_EVEN, DEC_BATCH, GLA_HEADS, GLA_DK, GLA_DV), 1.0),
        "state_conv": nrm((N_EVEN, DEC_BATCH, CONV_WIDTH - 1, CONV_CH), 0.5),
        "state_pool": nrm((N_ODD, DEC_BATCH, POOL_BUF, D), 1.0),
        "c_prompt": nrm((BATCH, D), 1.0),
        "c_sample": nrm((DEC_BATCH, D), 1.0),
        "w_ada": nrm((DEPTH, 2, D, 3 * D), 0.2 * D ** -0.5),
        "b_ada": nrm((DEPTH, 2, 3 * D), 0.02),
        "ln_g": 1.0 + nrm((DEPTH, 2, D), 0.02),
        "ln_b": nrm((DEPTH, 2, D), 0.02),
        "w_in_even": nrm((N_EVEN, D, IN0_WIDTH), D ** -0.5),
        "w_a2": nrm((N_EVEN, GLA_GATE_RANK, GLA_KW), GLA_GATE_RANK ** -0.5),
        "b_a": nrm((N_EVEN, GLA_KW), 0.1),
        "gla_norm_g": 1.0 + nrm((N_EVEN, GLA_VW), 0.02),
        "conv_w": nrm((N_EVEN, CONV_WIDTH, CONV_CH), CONV_WIDTH ** -0.5),
        "conv_b": nrm((N_EVEN, CONV_CH), 0.02),
        "conv_ln_g": 1.0 + nrm((N_EVEN, CONV_CH), 0.02),
        "conv_ln_b": nrm((N_EVEN, CONV_CH), 0.02),
        "w_out_even": nrm((N_EVEN, MIX0_WIDTH, D), BETA * MIX0_WIDTH ** -0.5),
        "w_grp_pool": nrm((N_ODD, POOL_GROUPS, POOL_GC, POOL_GC), POOL_GC ** -0.5),
        "pool_scale": 1.0 + nrm((N_ODD, D), 0.1),
        "w_out_odd": nrm((N_ODD, D, D), BETA * D ** -0.5),
        "w_coarse": nrm((DEPTH, D, MOE_GROUPS), D ** -0.5),
        "b_coarse": nrm((DEPTH, MOE_GROUPS), 0.01),
        "w_fine": nrm((DEPTH, MOE_GROUPS, D, MOE_EXPERTS), D ** -0.5),
        "b_fine": nrm((DEPTH, MOE_GROUPS, MOE_EXPERTS), 0.01),
        "w_gate": nrm((DEPTH, MOE_GROUPS, MOE_EXPERTS, D, MOE_FF), D ** -0.5),
        "w_up": nrm((DEPTH, MOE_GROUPS, MOE_EXPERTS, D, MOE_FF), D ** -0.5),
        "w_down": nrm((DEPTH, MOE_GROUPS, MOE_EXPERTS, MOE_FF, D), BETA * MOE_FF ** -0.5),
    }


def reference(x_prompt, x_sample, state_gla, state_conv, state_pool, c_prompt, c_sample,
              w_ada, b_ada, ln_g, ln_b, w_in_even, w_a2, b_a, gla_norm_g, conv_w, conv_b,
              conv_ln_g, conv_ln_b, w_out_even, w_grp_pool, pool_scale, w_out_odd,
              w_coarse, b_coarse, w_fine, b_fine, w_gate, w_up, w_down):
    p = dict(w_ada=w_ada, b_ada=b_ada, ln_g=ln_g, ln_b=ln_b, w_in_even=w_in_even, w_a2=w_a2, b_a=b_a,
             gla_norm_g=gla_norm_g, conv_w=conv_w, conv_b=conv_b, conv_ln_g=conv_ln_g,
             conv_ln_b=conv_ln_b, w_out_even=w_out_even, w_grp_pool=w_grp_pool, pool_scale=pool_scale,
             w_out_odd=w_out_odd, w_coarse=w_coarse, b_coarse=b_coarse, w_fine=w_fine, b_fine=b_fine,
             w_gate=w_gate, w_up=w_up, w_down=w_down)
    z_gla = jnp.zeros((N_EVEN, BATCH) + state_gla.shape[2:], state_gla.dtype)
    z_conv = jnp.zeros((N_EVEN, BATCH) + state_conv.shape[2:], state_conv.dtype)
    z_pool = jnp.zeros((N_ODD, BATCH) + state_pool.shape[2:], state_pool.dtype)
    y_prompt, g_p, cv_p, pl_p = trunk(x_prompt, c_prompt, z_gla, z_conv, z_pool, 0, p)
    y_sample, g_s, cv_s, pl_s = trunk(x_sample, c_sample, state_gla, state_conv, state_pool, PAST_LEN, p)
    return (y_prompt, y_sample, g_p, cv_p, pl_p, g_s, cv_s, pl_s)
```

```python
import functools

import jax
import jax.numpy as jnp
from jax import lax
from jax.experimental import pallas as pl
from jax.experimental.pallas import tpu as pltpu

F32 = jnp.float32
BF16 = jnp.bfloat16

D_MODEL = 1024
GLA_HEADS = 4
GLA_DK = 64
GLA_DV = 128
GLA_KW = GLA_HEADS * GLA_DK
GLA_VW = GLA_HEADS * GLA_DV
GLA_RANK = 16
GLA_TAU = 16.0
CONV_CH = 512
CONV_WIDTH = 31
CONV_BUF = CONV_WIDTH - 1
POOL_WINDOWS = (2, 4, 8, 16)
POOL_GC = D_MODEL // len(POOL_WINDOWS)
POOL_BUF = max(POOL_WINDOWS) - 1
MOE_GROUPS = 4
MOE_EXPERTS = 4
N_EXPERTS = MOE_GROUPS * MOE_EXPERTS
MOE_FF = 512
DEPTH = 2
ALPHA = (2 * DEPTH) ** 0.25
LN_EPS = 1e-5
PAST_LEN = 16384

LANES = 128
SUBLANES = 8
A_PAD = LANES
Z_WIDTH = 2 * GLA_KW + 2 * GLA_VW + 2 * CONV_CH + A_PAD
NEG_BIG = -1e30
VMEM_LIMIT = 56 * 1024 * 1024

TILE_L = 256
CONV_HALO = 32
POOL_HALO = 16
SAMPLE_BLK = 16
MOE_TILE = 512


def _dot(a, b):
    return jnp.dot(a, b, preferred_element_type=F32)


def _dot_nt(a, b):
    return lax.dot_general(a, b, (((1,), (1,)), ((), ())), preferred_element_type=F32)


def _split3(x):
    hi = x.astype(BF16)
    r1 = x - hi.astype(F32)
    mid = r1.astype(BF16)
    lo = (r1 - mid.astype(F32)).astype(BF16)
    return hi, mid, lo


def _split2(x):
    hi = x.astype(BF16)
    lo = (x - hi.astype(F32)).astype(BF16)
    return hi, lo


def _layer_norm(x, g, b):
    mu = jnp.mean(x, axis=-1, keepdims=True)
    xc = x - mu
    var = jnp.mean(xc * xc, axis=-1, keepdims=True)
    return xc * lax.rsqrt(var + LN_EPS) * g + b


def _standardize(x):
    mu = jnp.mean(x, axis=-1, keepdims=True)
    xc = x - mu
    var = jnp.mean(xc * xc, axis=-1, keepdims=True)
    return xc * lax.rsqrt(var + LN_EPS)


def _sigmoid(x):
    return 1.0 / (1.0 + jnp.exp(-x))


def _silu(x):
    return x * _sigmoid(x)


def _log_sigmoid(x):
    return jnp.minimum(x, 0.0) - jnp.log1p(jnp.exp(-jnp.abs(x)))


def _mod3(mod):
    return mod[:, 0:D_MODEL], mod[:, D_MODEL:2 * D_MODEL], mod[:, 2 * D_MODEL:3 * D_MODEL]


def _ada_kernel(c_ref, w_ref, b_ref, o_ref):
    c = c_ref[...]
    o_ref[0] = _dot(_silu(c).astype(BF16), w_ref[0].astype(BF16)) + b_ref[0]


def _ada_call(c_all, w_ada, b_ada):
    n = c_all.shape[0]
    ncomb = w_ada.shape[0]
    tn = D_MODEL
    return pl.pallas_call(
        _ada_kernel,
        grid=(ncomb, 3 * D_MODEL // tn),
        in_specs=[
            pl.BlockSpec((n, D_MODEL), lambda i, j: (0, 0)),
            pl.BlockSpec((1, D_MODEL, tn), lambda i, j: (i, 0, j)),
            pl.BlockSpec((1, 1, tn), lambda i, j: (i, 0, j)),
        ],
        out_specs=pl.BlockSpec((1, n, tn), lambda i, j: (i, 0, j)),
        out_shape=jax.ShapeDtypeStruct((ncomb, n, 3 * D_MODEL), F32),
        compiler_params=pltpu.CompilerParams(
            dimension_semantics=("arbitrary", "arbitrary"), vmem_limit_bytes=VMEM_LIMIT),
        name="ada",
    )(c_all, w_ada, b_ada)


def _route(logits):
    rows = logits.shape[0]
    lane = lax.broadcasted_iota(jnp.int32, (rows, LANES), 1)
    lanef = lane.astype(F32)
    big = float(LANES)
    lc = jnp.where(lane < MOE_GROUPS, logits, NEG_BIG)
    mc = jnp.max(lc, axis=-1, keepdims=True)
    pg = 1.0 / jnp.sum(jnp.exp(lc - mc), axis=-1, keepdims=True)
    gidx = jnp.min(jnp.where(lc == mc, lanef, big), axis=-1, keepdims=True)
    lo = float(MOE_GROUPS) + gidx * float(MOE_EXPERTS)
    in_grp = (lanef >= lo) & (lanef < lo + float(MOE_EXPERTS))
    lf = jnp.where(in_grp, logits, NEG_BIG)
    m1 = jnp.max(lf, axis=-1, keepdims=True)
    i1 = jnp.min(jnp.where(lf == m1, lanef, big), axis=-1, keepdims=True)
    lf2 = jnp.where(lanef == i1, NEG_BIG, lf)
    m2 = jnp.max(lf2, axis=-1, keepdims=True)
    i2 = jnp.min(jnp.where(lf2 == m2, lanef, big), axis=-1, keepdims=True)
    e2 = jnp.exp(m2 - m1)
    w1 = pg / (1.0 + e2)
    w2 = pg * e2 / (1.0 + e2)
    off = float(MOE_GROUPS)
    return jnp.where(lanef == i1 - off, w1, 0.0) + jnp.where(lanef == i2 - off, w2, 0.0)


def _post_mixer(x, m, gate, lng, lnb, mod_b, wr_hi, wr_lo, br):
    x1 = _layer_norm(ALPHA * x + (1.0 + gate) * m, lng, lnb)
    shift2, scale2, _ = _mod3(mod_b)
    h2 = x1 * (1.0 + scale2) + shift2
    h_hi, h_lo = _split2(h2)
    logits = _dot(h_hi, wr_hi) + _dot(h_lo, wr_hi) + _dot(h_hi, wr_lo) + br
    return x1, _route(logits)


def _hold_rows(b, s):
    n, w = b.shape
    if s >= SUBLANES:
        pieces = []
        for p in range(n // (2 * s)):
            r = p * 2 * s + s - 1
            pieces.append(jnp.broadcast_to(b[r:r + 1, :], (2 * s, w)))
        return pieces[0] if len(pieces) == 1 else jnp.concatenate(pieces, axis=0)
    b3 = b.reshape(n // SUBLANES, SUBLANES, w)
    sub = lax.broadcasted_iota(jnp.int32, b3.shape, 1)

    def bc(r):
        return jnp.broadcast_to(b3[:, r:r + 1, :], b3.shape)

    out = bc(s - 1)
    for p in range(1, SUBLANES // (2 * s)):
        out = jnp.where(sub >= p * 2 * s, bc(p * 2 * s + s - 1), out)
    return out.reshape(n, w)


def _gla_tile(q, k, v, la, s_prev):
    n = q.shape[0]
    row = lax.broadcasted_iota(jnp.int32, (n, n), 0)
    col = lax.broadcasted_iota(jnp.int32, (n, n), 1)
    tri = jnp.where(row >= col, 1.0, 0.0).astype(BF16)
    hi, mid, lo = _split3(la)
    b = _dot(tri, hi) + _dot(tri, mid) + _dot(tri, lo)
    b_last = b[n - 1:n, :]
    qe = (q * jnp.exp(b)).astype(BF16)
    kl = k * jnp.exp(b_last - b)
    qb = q.astype(BF16)
    kb = k.astype(BF16)
    vb = v.astype(BF16)
    sb = s_prev.astype(BF16)

    rowi = lax.broadcasted_iota(jnp.int32, (n, GLA_KW), 0)
    levels = []
    s = n // 2
    while s >= 1:
        m = _hold_rows(b, s)
        second = (rowi & s) != 0
        qs = (q * jnp.exp(jnp.where(second, b - m, NEG_BIG))).astype(BF16)
        ks = (k * jnp.exp(jnp.where(second, NEG_BIG, m - b))).astype(BF16)
        levels.append((s, qs, ks))
        s //= 2

    lane = lax.broadcasted_iota(jnp.int32, (n, LANES), 1)
    xor = row ^ col
    zero = jnp.zeros((), BF16)
    outs = []
    for h in range(GLA_HEADS):
        p = h // 2
        sl = slice(p * LANES, (p + 1) * LANES)
        hm = (lane < GLA_DK) if h % 2 == 0 else (lane >= GLA_DK)
        att = None
        for s, qs, ks in levels:
            a_s = _dot_nt(jnp.where(hm, qs[:, sl], zero), ks[:, sl])
            att = a_s if att is None else jnp.where(xor < 2 * s, a_s, att)
        a_0 = _dot_nt(jnp.where(hm, qb[:, sl], zero), kb[:, sl])
        att = jnp.where(row == col, a_0, att)
        o_h = _dot(att.astype(BF16), vb[:, h * GLA_DV:(h + 1) * GLA_DV])
        o_h = o_h + _dot(jnp.where(hm, qe[:, sl], zero), sb[sl, :])
        outs.append(o_h)

    decay = jnp.exp(b_last)
    upper = lax.broadcasted_iota(jnp.int32, (LANES, LANES), 0) < GLA_DK
    s_new = []
    for p in range(GLA_HEADS // 2):
        sl = slice(p * LANES, (p + 1) * LANES)
        kl_t = jnp.transpose(kl[:, sl]).astype(BF16)
        u = _dot(kl_t, vb[:, p * 2 * GLA_DV:(p + 1) * 2 * GLA_DV])
        upd = jnp.where(upper, u[:, 0:GLA_DV], u[:, GLA_DV:2 * GLA_DV])
        dcol = jnp.transpose(jnp.broadcast_to(decay[:, sl], (LANES, LANES)))
        s_new.append(dcol * s_prev[sl, :] + upd)
    return outs, jnp.concatenate(s_new, axis=0)


def _mixer0_prompt_kernel(x_ref, moda_ref, modb_ref, w_in_ref, w_a2_ref, b_a_ref, gng_ref,
                          cw_ref, cb_ref, clg_ref, clb_ref, w_out_ref, lng_ref, lnb_ref,
                          wr_hi_ref, wr_lo_ref, br_ref,
                          x1_ref, comb_ref, nconv_ref, ngla_ref,
                          s_ref, cbuf_ref):
    t = pl.program_id(1)
    n = x_ref.shape[1]

    @pl.when(t == 0)
    def _():
        s_ref[...] = jnp.zeros_like(s_ref)
        cbuf_ref[0:CONV_HALO, :] = jnp.zeros((CONV_HALO, CONV_CH), F32)

    x = x_ref[0]
    shift, scale, gate = _mod3(moda_ref[0])
    h = x * (1.0 + scale) + shift
    z = _dot(h.astype(BF16), w_in_ref[...])
    c0 = 0
    q = z[:, c0:c0 + GLA_KW] * (GLA_DK ** -0.5); c0 += GLA_KW
    k = z[:, c0:c0 + GLA_KW]; c0 += GLA_KW
    v = z[:, c0:c0 + GLA_VW]; c0 += GLA_VW
    g = z[:, c0:c0 + GLA_VW]; c0 += GLA_VW
    ua = z[:, c0:c0 + CONV_CH]; c0 += CONV_CH
    ug = z[:, c0:c0 + CONV_CH]; c0 += CONV_CH
    a_lr = z[:, c0:c0 + A_PAD]

    la = _log_sigmoid(_dot(a_lr.astype(BF16), w_a2_ref[...]) + b_a_ref[...]) * (1.0 / GLA_TAU)
    o_heads, s_new = _gla_tile(q, k, v, la, s_ref[...])
    s_ref[...] = s_new

    glu = ua * _sigmoid(ug)
    cbuf_ref[CONV_HALO:CONV_HALO + n, :] = glu
    acc = jnp.broadcast_to(cb_ref[...], (n, CONV_CH))
    base = CONV_HALO - CONV_BUF
    for j in range(CONV_WIDTH):
        acc = acc + cw_ref[j:j + 1, :] * cbuf_ref[base + j:base + j + n, :]
    y = _silu(_layer_norm(acc, clg_ref[...], clb_ref[...]))

    @pl.when(t == pl.num_programs(1) - 1)
    def _():
        nconv_ref[0, 0] = cbuf_ref[CONV_HALO + n - CONV_BUF:CONV_HALO + n, :]
        ngla_ref[0, 0] = s_new.reshape(GLA_HEADS, GLA_DK, GLA_DV)

    cbuf_ref[0:CONV_HALO, :] = cbuf_ref[n:n + CONV_HALO, :]

    m = _dot(y.astype(BF16), w_out_ref[GLA_VW:GLA_VW + CONV_CH, :])
    for hd in range(GLA_HEADS):
        sl = slice(hd * GLA_DV, (hd + 1) * GLA_DV)
        o_h = _standardize(o_heads[hd]) * gng_ref[:, sl] * _silu(g[:, sl])
        m = m + _dot(o_h.astype(BF16), w_out_ref[sl, :])

    x1, comb = _post_mixer(x, m, gate, lng_ref[...], lnb_ref[...], modb_ref[0],
                           wr_hi_ref[...], wr_lo_ref[...], br_ref[...])
    x1_ref[0] = x1
    comb_ref[0] = comb


def _full(shape):
    nd = len(shape)
    return pl.BlockSpec(shape, lambda *_: (0,) * nd)


def _mixer0_prompt_call(x, mod_a, mod_b, p):
    bsz, seq, _ = x.shape
    tl = TILE_L
    row3 = lambda b, t: (b, 0, 0)
    return pl.pallas_call(
        _mixer0_prompt_kernel,
        grid=(bsz, seq // tl),
        in_specs=[
            pl.BlockSpec((1, tl, D_MODEL), lambda b, t: (b, t, 0)),
            pl.BlockSpec((1, 1, 3 * D_MODEL), row3),
            pl.BlockSpec((1, 1, 3 * D_MODEL), row3),
            _full((D_MODEL, Z_WIDTH)),
            _full((A_PAD, GLA_KW)),
            _full((1, GLA_KW)),
            _full((1, GLA_VW)),
            _full((CONV_HALO, CONV_CH)),
            _full((1, CONV_CH)),
            _full((1, CONV_CH)),
            _full((1, CONV_CH)),
            _full((GLA_VW + CONV_CH, D_MODEL)),
            _full((1, D_MODEL)),
            _full((1, D_MODEL)),
            _full((D_MODEL, LANES)),
            _full((D_MODEL, LANES)),
            _full((1, LANES)),
        ],
        out_specs=[
            pl.BlockSpec((1, tl, D_MODEL), lambda b, t: (b, t, 0)),
            pl.BlockSpec((1, tl, LANES), lambda b, t: (b, t, 0)),
            pl.BlockSpec((1, 1, CONV_BUF, CONV_CH), lambda b, t: (0, b, 0, 0)),
            pl.BlockSpec((1, 1, GLA_HEADS, GLA_DK, GLA_DV), lambda b, t: (0, b, 0, 0, 0)),
        ],
        out_shape=[
            jax.ShapeDtypeStruct((bsz, seq, D_MODEL), F32),
            jax.ShapeDtypeStruct((bsz, seq, LANES), F32),
            jax.ShapeDtypeStruct((1, bsz, CONV_BUF, CONV_CH), F32),
            jax.ShapeDtypeStruct((1, bsz, GLA_HEADS, GLA_DK, GLA_DV), F32),
        ],
        scratch_shapes=[
            pltpu.VMEM((GLA_KW, GLA_DV), F32),
            pltpu.VMEM((CONV_HALO + tl, CONV_CH), F32),
        ],
        compiler_params=pltpu.CompilerParams(
            dimension_semantics=("arbitrary", "arbitrary"), vmem_limit_bytes=VMEM_LIMIT),
        name="mixer0_prompt",
    )(x, mod_a, mod_b, p["w_in"], p["w_a2"], p["b_a"], p["gng"], p["conv_w"], p["conv_b"],
      p["conv_ln_g"], p["conv_ln_b"], p["w_out0"], p["ln_g00"], p["ln_b00"],
      p["wr_hi0"], p["wr_lo0"], p["br0"])


def _mixer0_sample_kernel(x_ref, moda_ref, modb_ref, sgla_ref, sconv_ref, w_in_ref, w_a2_ref,
                          b_a_ref, gng_ref, cw_ref, cb_ref, clg_ref, clb_ref, w_out_ref, lng_ref,
                          lnb_ref, wr_hi_ref, wr_lo_ref, br_ref,
                          x1_ref, comb_ref, ngla_ref, nconv_ref,
                          zt_ref, v_ref, g_ref, glu_ref, o_ref, y_ref):
    i = pl.program_id(0)
    nb = sgla_ref.shape[1]
    ntok = x_ref.shape[0]

    @pl.when(i == 0)
    def _():
        shift, scale, _ = _mod3(moda_ref[...])
        h = x_ref[...] * (1.0 + scale) + shift
        z = _dot(h.astype(BF16), w_in_ref[...])
        c0 = 0
        q = z[:, c0:c0 + GLA_KW] * (GLA_DK ** -0.5); c0 += GLA_KW
        k = z[:, c0:c0 + GLA_KW]; c0 += GLA_KW
        v_ref[...] = z[:, c0:c0 + GLA_VW]; c0 += GLA_VW
        g_ref[...] = z[:, c0:c0 + GLA_VW]; c0 += GLA_VW
        ua = z[:, c0:c0 + CONV_CH]; c0 += CONV_CH
        ug = z[:, c0:c0 + CONV_CH]; c0 += CONV_CH
        a_lr = z[:, c0:c0 + A_PAD]
        glu_ref[...] = ua * _sigmoid(ug)
        la = _log_sigmoid(_dot(a_lr.astype(BF16), w_a2_ref[...]) + b_a_ref[...]) * (1.0 / GLA_TAU)
        a_hi, a_lo = _split2(jnp.transpose(jnp.exp(la)))
        zt_ref[0 * GLA_KW:1 * GLA_KW, :] = a_hi
        zt_ref[1 * GLA_KW:2 * GLA_KW, :] = a_lo
        zt_ref[2 * GLA_KW:3 * GLA_KW, :] = jnp.transpose(k).astype(BF16)
        zt_ref[3 * GLA_KW:4 * GLA_KW, :] = jnp.transpose(q).astype(BF16)

    tok_row = lax.broadcasted_iota(jnp.int32, (ntok, LANES), 0)
    blk = pl.ds(pl.multiple_of(i * nb, nb), nb)
    v_blk = v_ref[blk, :]
    glu_blk = glu_ref[blk, :]
    o_rows, y_rows = [], []
    for n in range(nb):
        onehot = jnp.where(tok_row == i * nb + n, 1.0, 0.0).astype(BF16)
        cols = _dot(zt_ref[...], onehot)
        a_col = cols[0:GLA_KW] + cols[GLA_KW:2 * GLA_KW]
        k_col = cols[2 * GLA_KW:3 * GLA_KW]
        q_col = cols[3 * GLA_KW:4 * GLA_KW]
        v_row = v_blk[n:n + 1, :]
        v_b = jnp.concatenate(
            [jnp.broadcast_to(v_row[:, h * GLA_DV:(h + 1) * GLA_DV], (GLA_DK, GLA_DV))
             for h in range(GLA_HEADS)], axis=0)
        s_old = sgla_ref[0, n].reshape(GLA_KW, GLA_DV)
        s_new = a_col * s_old + k_col * v_b
        ngla_ref[0, n] = s_new.reshape(GLA_HEADS, GLA_DK, GLA_DV)
        o4 = jnp.sum((q_col * s_new).reshape(GLA_HEADS, GLA_DK, GLA_DV), axis=1)
        o_rows.append(jnp.concatenate([o4[h:h + 1, :] for h in range(GLA_HEADS)], axis=1))
        glu_row = glu_blk[n:n + 1, :]
        past = sconv_ref[0, n]
        y_rows.append(jnp.sum(past * cw_ref[0:CONV_BUF, :], axis=0, keepdims=True)
                      + glu_row * cw_ref[CONV_BUF:CONV_WIDTH, :] + cb_ref[...])
        nconv_ref[0, n, 0:CONV_BUF - 1, :] = sconv_ref[0, n, 1:CONV_BUF, :]
        nconv_ref[0, n, CONV_BUF - 1:CONV_BUF, :] = glu_row
    o_ref[blk, :] = jnp.concatenate(o_rows, axis=0)
    y_ref[blk, :] = jnp.concatenate(y_rows, axis=0)

    @pl.when(i == pl.num_programs(0) - 1)
    def _():
        y = _silu(_layer_norm(y_ref[...], clg_ref[...], clb_ref[...]))
        m = _dot(y.astype(BF16), w_out_ref[GLA_VW:GLA_VW + CONV_CH, :])
        for hd in range(GLA_HEADS):
            sl = slice(hd * GLA_DV, (hd + 1) * GLA_DV)
            o_h = _standardize(o_ref[:, sl]) * gng_ref[:, sl] * _silu(g_ref[:, sl])
            m = m + _dot(o_h.astype(BF16), w_out_ref[sl, :])
        _, _, gate = _mod3(moda_ref[...])
        x1, comb = _post_mixer(x_ref[...], m, gate, lng_ref[...], lnb_ref[...], modb_ref[...],
                               wr_hi_ref[...], wr_lo_ref[...], br_ref[...])
        x1_ref[...] = x1
        comb_ref[...] = comb


def _mixer0_sample_call(x, mod_a, mod_b, state_gla, state_conv, p):
    ntok = x.shape[0]
    nb = SAMPLE_BLK
    return pl.pallas_call(
        _mixer0_sample_kernel,
        grid=(ntok // nb,),
        in_specs=[
            _full((ntok, D_MODEL)),
            _full((ntok, 3 * D_MODEL)),
            _full((ntok, 3 * D_MODEL)),
            pl.BlockSpec((1, nb, GLA_HEADS, GLA_DK, GLA_DV), lambda i: (0, i, 0, 0, 0)),
            pl.BlockSpec((1, nb, CONV_BUF, CONV_CH), lambda i: (0, i, 0, 0)),
            _full((D_MODEL, Z_WIDTH)),
            _full((A_PAD, GLA_KW)),
            _full((1, GLA_KW)),
            _full((1, GLA_VW)),
            _full((CONV_HALO, CONV_CH)),
            _full((1, CONV_CH)),
            _full((1, CONV_CH)),
            _full((1, CONV_CH)),
            _full((GLA_VW + CONV_CH, D_MODEL)),
            _full((1, D_MODEL)),
            _full((1, D_MODEL)),
            _full((D_MODEL, LANES)),
            _full((D_MODEL, LANES)),
            _full((1, LANES)),
        ],
        out_specs=[
            _full((ntok, D_MODEL)),
            _full((ntok, LANES)),
            pl.BlockSpec((1, nb, GLA_HEADS, GLA_DK, GLA_DV), lambda i: (0, i, 0, 0, 0)),
            pl.BlockSpec((1, nb, CONV_BUF, CONV_CH), lambda i: (0, i, 0, 0)),
        ],
        out_shape=[
            jax.ShapeDtypeStruct((ntok, D_MODEL), F32),
            jax.ShapeDtypeStruct((ntok, LANES), F32),
            jax.ShapeDtypeStruct(state_gla.shape, F32),
            jax.ShapeDtypeStruct(state_conv.shape, F32),
        ],
        scratch_shapes=[
            pltpu.VMEM((4 * GLA_KW, ntok), BF16),
            pltpu.VMEM((ntok, GLA_VW), F32),
            pltpu.VMEM((ntok, GLA_VW), F32),
            pltpu.VMEM((ntok, CONV_CH), F32),
            pltpu.VMEM((ntok, GLA_VW), F32),
            pltpu.VMEM((ntok, CONV_CH), F32),
        ],
        compiler_params=pltpu.CompilerParams(
            dimension_semantics=("arbitrary",), vmem_limit_bytes=VMEM_LIMIT),
        name="mixer0_sample",
    )(x, mod_a, mod_b, state_gla, state_conv, p["w_in"], p["w_a2"], p["b_a"], p["gng"],
      p["conv_w"], p["conv_b"], p["conv_ln_g"], p["conv_ln_b"], p["w_out0"], p["ln_g00"],
      p["ln_b00"], p["wr_hi0"], p["wr_lo0"], p["br0"])


def _pool_project(pooled, h, wg_ref, ps_ref, w_out_ref):
    m = None
    for gi in range(len(POOL_WINDOWS)):
        sl = slice(gi * POOL_GC, (gi + 1) * POOL_GC)
        mixed = _dot((pooled[:, sl] - h[:, sl]).astype(BF16), wg_ref[gi]) * ps_ref[:, sl]
        part = _dot(mixed.astype(BF16), w_out_ref[sl, :])
        m = part if m is None else m + part
    return m


def _mixer1_prompt_kernel(x_ref, moda_ref, modb_ref, wg_ref, ps_ref, w_out_ref, lng_ref, lnb_ref,
                          wr_hi_ref, wr_lo_ref, br_ref,
                          x1_ref, comb_ref, npool_ref, pbuf_ref):
    t = pl.program_id(1)
    n = x_ref.shape[1]

    @pl.when(t == 0)
    def _():
        pbuf_ref[0:POOL_HALO, :] = jnp.zeros((POOL_HALO, D_MODEL), F32)

    x = x_ref[0]
    shift, scale, gate = _mod3(moda_ref[0])
    h = x * (1.0 + scale) + shift
    pbuf_ref[POOL_HALO:POOL_HALO + n, :] = h

    ext = pbuf_ref[...]
    sums = []
    cur = ext
    for gi, w in enumerate(POOL_WINDOWS):
        cur = cur + pltpu.roll(cur, w // 2, axis=0)
        sums.append(cur[POOL_HALO:, 0:POOL_GC])
        if gi + 1 < len(POOL_WINDOWS):
            cur = cur[:, POOL_GC:]
    pos = lax.broadcasted_iota(jnp.int32, (n, POOL_GC), 0) + t * n
    pooled = jnp.concatenate(
        [s / jnp.minimum(w, pos + 1).astype(F32) for s, w in zip(sums, POOL_WINDOWS)], axis=1)

    @pl.when(t == pl.num_programs(1) - 1)
    def _():
        npool_ref[0, 0] = pbuf_ref[POOL_HALO + n - POOL_BUF:POOL_HALO + n, :]

    pbuf_ref[0:POOL_HALO, :] = pbuf_ref[n:n + POOL_HALO, :]

    m = _pool_project(pooled, h, wg_ref, ps_ref, w_out_ref)
    x1, comb = _post_mixer(x, m, gate, lng_ref[...], lnb_ref[...], modb_ref[0],
                           wr_hi_ref[...], wr_lo_ref[...], br_ref[...])
    x1_ref[0] = x1
    comb_ref[0] = comb


def _mixer1_prompt_call(x, mod_a, mod_b, p):
    bsz, seq, _ = x.shape
    tl = TILE_L
    row3 = lambda b, t: (b, 0, 0)
    ng = len(POOL_WINDOWS)
    return pl.pallas_call(
        _mixer1_prompt_kernel,
        grid=(bsz, seq // tl),
        in_specs=[
            pl.BlockSpec((1, tl, D_MODEL), lambda b, t: (b, t, 0)),
            pl.BlockSpec((1, 1, 3 * D_MODEL), row3),
            pl.BlockSpec((1, 1, 3 * D_MODEL), row3),
            _full((ng, POOL_GC, POOL_GC)),
            _full((1, D_MODEL)),
            _full((D_MODEL, D_MODEL)),
            _full((1, D_MODEL)),
            _full((1, D_MODEL)),
            _full((D_MODEL, LANES)),
            _full((D_MODEL, LANES)),
            _full((1, LANES)),
        ],
        out_specs=[
            pl.BlockSpec((1, tl, D_MODEL), lambda b, t: (b, t, 0)),
            pl.BlockSpec((1, tl, LANES), lambda b, t: (b, t, 0)),
            pl.BlockSpec((1, 1, POOL_BUF, D_MODEL), lambda b, t: (0, b, 0, 0)),
        ],
        out_shape=[
            jax.ShapeDtypeStruct((bsz, seq, D_MODEL), F32),
            jax.ShapeDtypeStruct((bsz, seq, LANES), F32),
            jax.ShapeDtypeStruct((1, bsz, POOL_BUF, D_MODEL), F32),
        ],
        scratch_shapes=[pltpu.VMEM((POOL_HALO + tl, D_MODEL), F32)],
        compiler_params=pltpu.CompilerParams(
            dimension_semantics=("arbitrary", "arbitrary"), vmem_limit_bytes=VMEM_LIMIT),
        name="mixer1_prompt",
    )(x, mod_a, mod_b, p["w_grp"], p["pool_scale"], p["w_out1"], p["ln_g10"], p["ln_b10"],
      p["wr_hi1"], p["wr_lo1"], p["br1"])


def _mixer1_sample_kernel(x_ref, moda_ref, modb_ref, spool_ref, wg_ref, ps_ref, w_out_ref,
                          lng_ref, lnb_ref, wr_hi_ref, wr_lo_ref, br_ref,
                          x1_ref, comb_ref, npool_ref, h_ref, pooled_ref):
    i = pl.program_id(0)
    nb = spool_ref.shape[1]

    @pl.when(i == 0)
    def _():
        shift, scale, _ = _mod3(moda_ref[...])
        h_ref[...] = x_ref[...] * (1.0 + scale) + shift

    lane = lax.broadcasted_iota(jnp.int32, (POOL_BUF, D_MODEL), 1)
    rowi = lax.broadcasted_iota(jnp.int32, (POOL_BUF, D_MODEL), 0)
    first = jnp.zeros((POOL_BUF, D_MODEL), jnp.int32)
    lane1 = lax.broadcasted_iota(jnp.int32, (1, D_MODEL), 1)
    inv_w = jnp.zeros((1, D_MODEL), F32)
    for gi, w in enumerate(POOL_WINDOWS):
        in_g = (lane >= gi * POOL_GC) & (lane < (gi + 1) * POOL_GC)
        first = jnp.where(in_g, POOL_BUF - (w - 1), first)
        in_g1 = (lane1 >= gi * POOL_GC) & (lane1 < (gi + 1) * POOL_GC)
        inv_w = jnp.where(in_g1, 1.0 / w, inv_w)
    keep = rowi >= first

    blk = pl.ds(pl.multiple_of(i * nb, nb), nb)
    h_blk = h_ref[blk, :]
    rows = []
    for n in range(nb):
        h_row = h_blk[n:n + 1, :]
        past = spool_ref[0, n]
        total = jnp.sum(jnp.where(keep, past, 0.0), axis=0, keepdims=True) + h_row
        rows.append(total * inv_w)
        npool_ref[0, n, 0:POOL_BUF - 1, :] = spool_ref[0, n, 1:POOL_BUF, :]
        npool_ref[0, n, POOL_BUF - 1:POOL_BUF, :] = h_row
    pooled_ref[blk, :] = jnp.concatenate(rows, axis=0)

    @pl.when(i == pl.num_programs(0) - 1)
    def _():
        m = _pool_project(pooled_ref[...], h_ref[...], wg_ref, ps_ref, w_out_ref)
        _, _, gate = _mod3(moda_ref[...])
        x1, comb = _post_mixer(x_ref[...], m, gate, lng_ref[...], lnb_ref[...], modb_ref[...],
                               wr_hi_ref[...], wr_lo_ref[...], br_ref[...])
        x1_ref[...] = x1
        comb_ref[...] = comb


def _mixer1_sample_call(x, mod_a, mod_b, state_pool, p):
    ntok = x.shape[0]
    nb = SAMPLE_BLK
    ng = len(POOL_WINDOWS)
    return pl.pallas_call(
        _mixer1_sample_kernel,
        grid=(ntok // nb,),
        in_specs=[
            _full((ntok, D_MODEL)),
            _full((ntok, 3 * D_MODEL)),
            _full((ntok, 3 * D_MODEL)),
            pl.BlockSpec((1, nb, POOL_BUF, D_MODEL), lambda i: (0, i, 0, 0)),
            _full((ng, POOL_GC, POOL_GC)),
            _full((1, D_MODEL)),
            _full((D_MODEL, D_MODEL)),
            _full((1, D_MODEL)),
            _full((1, D_MODEL)),
            _full((D_MODEL, LANES)),
            _full((D_MODEL, LANES)),
            _full((1, LANES)),
        ],
        out_specs=[
            _full((ntok, D_MODEL)),
            _full((ntok, LANES)),
            pl.BlockSpec((1, nb, POOL_BUF, D_MODEL), lambda i: (0, i, 0, 0)),
        ],
        out_shape=[
            jax.ShapeDtypeStruct((ntok, D_MODEL), F32),
            jax.ShapeDtypeStruct((ntok, LANES), F32),
            jax.ShapeDtypeStruct(state_pool.shape, F32),
        ],
        scratch_shapes=[pltpu.VMEM((ntok, D_MODEL), F32), pltpu.VMEM((ntok, D_MODEL), F32)],
        compiler_params=pltpu.CompilerParams(
            dimension_semantics=("arbitrary",), vmem_limit_bytes=VMEM_LIMIT),
        name="mixer1_sample",
    )(x, mod_a, mod_b, state_pool, p["w_grp"], p["pool_scale"], p["w_out1"], p["ln_g10"],
      p["ln_b10"], p["wr_hi1"], p["wr_lo1"], p["br1"])


def _moe_kernel(x_ref, mod_ref, comb_ref, wg_ref, wu_ref, wd_ref, lng_ref, lnb_ref,
                out_ref, h_ref, acc_ref):
    e = pl.program_id(1)

    @pl.when(e == 0)
    def _():
        shift, scale, _ = _mod3(mod_ref[0])
        h_ref[...] = (x_ref[...] * (1.0 + scale) + shift).astype(BF16)
        acc_ref[...] = jnp.zeros_like(acc_ref)

    hb = h_ref[...]
    hid = _silu(_dot(hb, wg_ref[0])) * _dot(hb, wu_ref[0])
    lane = lax.broadcasted_iota(jnp.int32, comb_ref.shape, 1)
    c = jnp.sum(jnp.where(lane == e, comb_ref[...], 0.0), axis=-1, keepdims=True)
    acc_ref[...] += _dot((hid * c).astype(BF16), wd_ref[0])

    @pl.when(e == pl.num_programs(1) - 1)
    def _():
        _, _, gate = _mod3(mod_ref[0])
        out_ref[...] = _layer_norm(ALPHA * x_ref[...] + (1.0 + gate) * acc_ref[...],
                                   lng_ref[...], lnb_ref[...])


def _moe_call(x, mod, comb, wg, wu, wd, lng, lnb, tile, tiles_per_mod):
    ntok = x.shape[0]
    mrows = mod.shape[1]
    return pl.pallas_call(
        _moe_kernel,
        grid=(ntok // tile, N_EXPERTS),
        in_specs=[
            pl.BlockSpec((tile, D_MODEL), lambda i, e: (i, 0)),
            pl.BlockSpec((1, mrows, 3 * D_MODEL), lambda i, e: (i // tiles_per_mod, 0, 0)),
            pl.BlockSpec((tile, LANES), lambda i, e: (i, 0)),
            pl.BlockSpec((1, D_MODEL, MOE_FF), lambda i, e: (e, 0, 0)),
            pl.BlockSpec((1, D_MODEL, MOE_FF), lambda i, e: (e, 0, 0)),
            pl.BlockSpec((1, MOE_FF, D_MODEL), lambda i, e: (e, 0, 0)),
            pl.BlockSpec((1, D_MODEL), lambda i, e: (0, 0)),
            pl.BlockSpec((1, D_MODEL), lambda i, e: (0, 0)),
        ],
        out_specs=pl.BlockSpec((tile, D_MODEL), lambda i, e: (i, 0)),
        out_shape=jax.ShapeDtypeStruct((ntok, D_MODEL), F32),
        scratch_shapes=[pltpu.VMEM((tile, D_MODEL), BF16), pltpu.VMEM((tile, D_MODEL), F32)],
        compiler_params=pltpu.CompilerParams(
            dimension_semantics=("arbitrary", "arbitrary"), vmem_limit_bytes=VMEM_LIMIT),
        name="moe",
    )(x, mod, comb, wg, wu, wd, lng, lnb)


def _router_weights(w_coarse, b_coarse, w_fine, b_fine):
    wf = jnp.transpose(w_fine, (1, 0, 2)).reshape(D_MODEL, N_EXPERTS)
    w = jnp.concatenate([w_coarse, wf], axis=1)
    w = jnp.pad(w, ((0, 0), (0, LANES - w.shape[1])))
    b = jnp.concatenate([b_coarse, b_fine.reshape(N_EXPERTS)])
    b = jnp.pad(b, (0, LANES - b.shape[0])).reshape(1, LANES)
    hi = w.astype(BF16)
    lo = (w - hi.astype(F32)).astype(BF16)
    return hi, lo, b


def _prep_params(ln_g, ln_b, w_in_even, w_a2, b_a, gla_norm_g, conv_w, conv_b, conv_ln_g,
                 conv_ln_b, w_out_even, w_grp_pool, pool_scale, w_out_odd, w_coarse, b_coarse,
                 w_fine, b_fine):
    p = {}
    w_in = w_in_even[0]
    o_q, o_k, o_v, o_g = 0, GLA_KW, 2 * GLA_KW, 2 * GLA_KW + GLA_VW
    o_a = o_g + GLA_VW
    o_u = o_a + GLA_RANK
    w_in_r = jnp.concatenate(
        [w_in[:, o_q:o_a], w_in[:, o_u:o_u + 2 * CONV_CH], w_in[:, o_a:o_u],
         jnp.zeros((D_MODEL, A_PAD - GLA_RANK), F32)], axis=1)
    p["w_in"] = w_in_r.astype(BF16)
    p["w_a2"] = jnp.pad(w_a2[0], ((0, A_PAD - GLA_RANK), (0, 0))).astype(BF16)
    p["b_a"] = b_a[0].reshape(1, GLA_KW)
    p["gng"] = gla_norm_g[0].reshape(1, GLA_VW)
    p["conv_w"] = jnp.pad(conv_w[0], ((0, CONV_HALO - CONV_WIDTH), (0, 0)))
    p["conv_b"] = conv_b[0].reshape(1, CONV_CH)
    p["conv_ln_g"] = conv_ln_g[0].reshape(1, CONV_CH)
    p["conv_ln_b"] = conv_ln_b[0].reshape(1, CONV_CH)
    p["w_out0"] = w_out_even[0].astype(BF16)
    p["w_grp"] = w_grp_pool[0].astype(BF16)
    p["pool_scale"] = pool_scale[0].reshape(1, D_MODEL)
    p["w_out1"] = w_out_odd[0].astype(BF16)
    for layer in range(DEPTH):
        for j in range(2):
            p[f"ln_g{layer}{j}"] = ln_g[layer, j].reshape(1, D_MODEL)
            p[f"ln_b{layer}{j}"] = ln_b[layer, j].reshape(1, D_MODEL)
        hi, lo, b = _router_weights(w_coarse[layer], b_coarse[layer], w_fine[layer], b_fine[layer])
        p[f"wr_hi{layer}"], p[f"wr_lo{layer}"], p[f"br{layer}"] = hi, lo, b
    return p


def kernel(x_prompt, x_sample, state_gla, state_conv, state_pool, c_prompt, c_sample, w_ada, b_ada,
           ln_g, ln_b, w_in_even, w_a2, b_a, gla_norm_g, conv_w, conv_b, conv_ln_g, conv_ln_b,
           w_out_even, w_grp_pool, pool_scale, w_out_odd, w_coarse, b_coarse, w_fine, b_fine,
           w_gate, w_up, w_down):
    bsz, seq, _ = x_prompt.shape
    nsmp = x_sample.shape[0]
    p = _prep_params(ln_g, ln_b, w_in_even, w_a2, b_a, gla_norm_g, conv_w, conv_b, conv_ln_g,
                     conv_ln_b, w_out_even, w_grp_pool, pool_scale, w_out_odd, w_coarse, b_coarse,
                     w_fine, b_fine)
    wg = w_gate.reshape(DEPTH, N_EXPERTS, D_MODEL, MOE_FF).astype(BF16)
    wu = w_up.reshape(DEPTH, N_EXPERTS, D_MODEL, MOE_FF).astype(BF16)
    wd = w_down.reshape(DEPTH, N_EXPERTS, MOE_FF, D_MODEL).astype(BF16)

    c_all = jnp.concatenate([c_prompt, c_sample], axis=0)
    mod = _ada_call(c_all, w_ada.reshape(2 * DEPTH, D_MODEL, 3 * D_MODEL),
                    b_ada.reshape(2 * DEPTH, 1, 3 * D_MODEL))
    mod_p = mod[:, :bsz].reshape(2 * DEPTH, bsz, 1, 3 * D_MODEL)
    mod_s = mod[:, bsz:]

    def moe(xp, xs, comb_p, comb_s, layer):
        tiles_per_seq = seq // MOE_TILE
        lng, lnb = p[f"ln_g{layer}1"], p[f"ln_b{layer}1"]
        yp = _moe_call(xp.reshape(bsz * seq, D_MODEL), mod_p[2 * layer + 1],
                       comb_p.reshape(bsz * seq, LANES), wg[layer], wu[layer], wd[layer],
                       lng, lnb, MOE_TILE, tiles_per_seq)
        ys = _moe_call(xs, mod_s[2 * layer + 1][None], comb_s, wg[layer], wu[layer], wd[layer],
                       lng, lnb, nsmp, 1)
        return yp.reshape(bsz, seq, D_MODEL), ys

    xs0 = x_sample.reshape(nsmp, D_MODEL)
    x1p, comb_p, conv_p, gla_p = _mixer0_prompt_call(x_prompt, mod_p[0], mod_p[1], p)
    x1s, comb_s, gla_s, conv_s = _mixer0_sample_call(xs0, mod_s[0], mod_s[1], state_gla,
                                                     state_conv, p)
    x2p, x2s = moe(x1p, x1s, comb_p, comb_s, 0)
    x3p, comb_p, pool_p = _mixer1_prompt_call(x2p, mod_p[2], mod_p[3], p)
    x3s, comb_s, pool_s = _mixer1_sample_call(x2s, mod_s[2], mod_s[3], state_pool, p)
    x4p, x4s = moe(x3p, x3s, comb_p, comb_s, 1)
    return (x4p, x4s.reshape(nsmp, 1, D_MODEL), gla_p, conv_p, pool_p, gla_s, conv_s, pool_s)
```

```python
import jax
import jax.numpy as jnp
from jax import lax
from jax.experimental import pallas as pl
from jax.experimental.pallas import tpu as pltpu

F32 = jnp.float32
BF16 = jnp.bfloat16
I32 = jnp.int32

D_MODEL = 1024
GLA_HEADS = 4
GLA_DK = 64
GLA_DV = 128
GLA_KW = GLA_HEADS * GLA_DK
GLA_VW = GLA_HEADS * GLA_DV
GLA_RANK = 16
GLA_TAU = 16.0
CONV_CH = 512
CONV_WIDTH = 31
CONV_BUF = CONV_WIDTH - 1
POOL_WINDOWS = (2, 4, 8, 16)
POOL_GC = D_MODEL // len(POOL_WINDOWS)
POOL_BUF = max(POOL_WINDOWS) - 1
MOE_GROUPS = 4
MOE_EXPERTS = 4
N_EXPERTS = MOE_GROUPS * MOE_EXPERTS
N_PAIRS = MOE_EXPERTS * (MOE_EXPERTS - 1) // 2
N_BUCKETS = MOE_GROUPS * N_PAIRS
PAIR_A = (0, 0, 0, 1, 1, 2)
PAIR_B = (1, 2, 3, 2, 3, 3)
MOE_FF = 512
DEPTH = 2
ALPHA = (2 * DEPTH) ** 0.25
LN_EPS = 1e-5

LANES = 128
SUBLANES = 8
CHUNKS = D_MODEL // LANES
A_PAD = LANES
Z_WIDTH = 2 * GLA_KW + 2 * GLA_VW + 2 * CONV_CH + A_PAD
NEG_BIG = -1e30
VMEM_LIMIT = 56 * 1024 * 1024

TILE_L = 256
CONV_HALO = 32
CONV_W_ROWS = 32
POOL_HALO = 16
SAMPLE_BLK = 16
MOE_TILE = 256
ROW_TILE = 128


def _dot(a, b):
    return jnp.dot(a, b, preferred_element_type=F32)


def _dot_nt(a, b):
    return lax.dot_general(a, b, (((1,), (1,)), ((), ())), preferred_element_type=F32)


def _split3(x):
    hi = x.astype(BF16)
    r1 = x - hi.astype(F32)
    mid = r1.astype(BF16)
    lo = (r1 - mid.astype(F32)).astype(BF16)
    return hi, mid, lo


def _split2(x):
    hi = x.astype(BF16)
    lo = (x - hi.astype(F32)).astype(BF16)
    return hi, lo


def _dot_w3(a, w_hi, w_lo):
    a_hi, a_lo = _split2(a)
    return _dot(a_hi, w_hi) + (_dot(a_lo, w_hi) + _dot(a_hi, w_lo))


def _lhs3(parts):
    hi, lo = parts
    return jnp.concatenate([hi, lo, hi], axis=1)


def _rhs3_rows(x):
    hi, lo = _split2(x)
    return jnp.concatenate([hi, hi, lo], axis=0)


def _rhs3_lanes(parts):
    hi, lo = parts
    return jnp.concatenate([hi, hi, lo], axis=1)


def _layer_norm(x, g, b):
    mu = jnp.mean(x, axis=-1, keepdims=True)
    xc = x - mu
    var = jnp.mean(xc * xc, axis=-1, keepdims=True)
    return xc * lax.rsqrt(var + LN_EPS) * g + b


def _standardize(x):
    mu = jnp.mean(x, axis=-1, keepdims=True)
    xc = x - mu
    var = jnp.mean(xc * xc, axis=-1, keepdims=True)
    return xc * lax.rsqrt(var + LN_EPS)


def _sigmoid(x):
    return 1.0 / (1.0 + jnp.exp(-x))


def _silu(x):
    return x * _sigmoid(x)


def _log_sigmoid(x):
    return jnp.minimum(x, 0.0) - jnp.log(1.0 + jnp.exp(-jnp.abs(x)))


def _mod3(mod):
    return mod[:, 0:D_MODEL], mod[:, D_MODEL:2 * D_MODEL], mod[:, 2 * D_MODEL:3 * D_MODEL]


def _full(shape):
    nd = len(shape)
    return pl.BlockSpec(shape, lambda *_: (0,) * nd)


def _const(shape):
    nd = len(shape)
    return pl.BlockSpec(shape, lambda *_: (0,) * nd, pipeline_mode=pl.Buffered(1))


def _hi_lo(w):
    bits = lax.bitcast_convert_type(w, jnp.uint32) & jnp.uint32(0xFFFF0000)
    hi = lax.bitcast_convert_type(bits, F32)
    return jnp.stack([hi.astype(BF16), (w - hi).astype(BF16)])


def _params(*sem):
    return pltpu.CompilerParams(dimension_semantics=sem, vmem_limit_bytes=VMEM_LIMIT)


def _ada_kernel(c_ref, w_ref, b_ref, o_ref):
    w_hi, w_lo = _split2(w_ref[0])
    o_ref[0] = _dot_w3(_silu(c_ref[...]), w_hi, w_lo) + b_ref[0]


def _ada_call(c_all, w_ada, b_ada):
    n = c_all.shape[0]
    ncomb = w_ada.shape[0]
    tn = D_MODEL
    return pl.pallas_call(
        _ada_kernel,
        grid=(ncomb, 3 * D_MODEL // tn),
        in_specs=[
            pl.BlockSpec((n, D_MODEL), lambda i, j: (0, 0)),
            pl.BlockSpec((1, D_MODEL, tn), lambda i, j: (i, 0, j)),
            pl.BlockSpec((1, 1, tn), lambda i, j: (i, 0, j)),
        ],
        out_specs=pl.BlockSpec((1, n, tn), lambda i, j: (i, 0, j)),
        out_shape=jax.ShapeDtypeStruct((ncomb, n, 3 * D_MODEL), F32),
        compiler_params=_params("arbitrary", "arbitrary"),
        name="ada",
    )(c_all, w_ada, b_ada)


def _route(logits, carry):
    rows = logits.shape[0]
    lane = lax.broadcasted_iota(I32, (rows, LANES), 1)
    lanef = lane.astype(F32)
    big = float(LANES)
    lc = jnp.where(lane < MOE_GROUPS, logits, NEG_BIG)
    mc = jnp.max(lc, axis=-1, keepdims=True)
    pg = 1.0 / jnp.sum(jnp.exp(lc - mc), axis=-1, keepdims=True)
    gidx = jnp.min(jnp.where(lc == mc, lanef, big), axis=-1, keepdims=True)
    lo = float(MOE_GROUPS) + gidx * float(MOE_EXPERTS)
    in_grp = (lanef >= lo) & (lanef < lo + float(MOE_EXPERTS))
    lf = jnp.where(in_grp, logits, NEG_BIG)
    m1 = jnp.max(lf, axis=-1, keepdims=True)
    i1 = jnp.min(jnp.where(lf == m1, lanef, big), axis=-1, keepdims=True)
    lf2 = jnp.where(lanef == i1, NEG_BIG, lf)
    m2 = jnp.max(lf2, axis=-1, keepdims=True)
    i2 = jnp.min(jnp.where(lf2 == m2, lanef, big), axis=-1, keepdims=True)
    e2 = jnp.exp(m2 - m1)
    w1 = pg / (1.0 + e2)
    w2 = pg * e2 / (1.0 + e2)
    ea = jnp.minimum(i1, i2) - lo
    eb = jnp.maximum(i1, i2) - lo
    w_a = jnp.where(i1 < i2, w1, w2)
    w_b = jnp.where(i1 < i2, w2, w1)
    bucket = gidx * float(N_PAIRS) + ea * (7.0 - ea) * 0.5 + eb - ea - 1.0
    onehot = jnp.where(lanef == bucket, 1.0, 0.0)
    r = lax.broadcasted_iota(I32, (rows, rows), 0)
    c = lax.broadcasted_iota(I32, (rows, rows), 1)
    before = _dot(jnp.where(r > c, 1.0, 0.0).astype(BF16), onehot.astype(BF16))
    rank = jnp.sum(onehot * (before + carry), axis=-1, keepdims=True)
    new_carry = carry + jnp.sum(onehot, axis=0, keepdims=True)
    route = (jnp.where(lane == 0, w_a, 0.0) + jnp.where(lane == 1, w_b, 0.0)
             + jnp.where(lane == 2, bucket, 0.0) + jnp.where(lane == 3, rank, 0.0))
    return route, new_carry


def _store_chunked(ref, x):
    rows = x.shape[0]
    for c in range(CHUNKS):
        ref[pl.ds(c, rows, stride=CHUNKS), :] = x[:, c * LANES:(c + 1) * LANES]


def _load_chunked(ref, rows, group, first):
    return jnp.concatenate(
        [ref[pl.ds(first + c, rows, stride=group), :] for c in range(CHUNKS)], axis=1)


def _post_mixer(x, m, gate, lng_ref, lnb_ref, mod_b, wr_hi_ref, wr_lo_ref, br_ref, carry,
                x1_ref, h2c_ref, route_ref):
    x1 = _layer_norm(ALPHA * x + (1.0 + gate) * m, lng_ref[...], lnb_ref[...])
    shift2, scale2, _ = _mod3(mod_b)
    h2 = x1 * (1.0 + scale2) + shift2
    h_hi, h_lo = _split2(h2)
    wr_hi = wr_hi_ref[...]
    logits = _dot(h_hi, wr_hi) + _dot(h_lo, wr_hi) + _dot(h_hi, wr_lo_ref[...]) + br_ref[...]
    route, carry = _route(logits, carry)
    x1_ref[...] = x1
    _store_chunked(h2c_ref, h2)
    route_ref[...] = route
    return carry


def _token_out_specs(rows, index_map):
    return [pl.BlockSpec((rows, D_MODEL), index_map),
            pl.BlockSpec((rows * CHUNKS, LANES), index_map),
            pl.BlockSpec((rows, LANES), index_map)]


def _token_out_shapes(ntok):
    return [jax.ShapeDtypeStruct((ntok, D_MODEL), F32),
            jax.ShapeDtypeStruct((ntok * CHUNKS, LANES), F32),
            jax.ShapeDtypeStruct((ntok, LANES), F32)]


def _hold_rows(b, s):
    n, w = b.shape
    if s >= SUBLANES:
        pieces = []
        for p in range(n // (2 * s)):
            r = p * 2 * s + s - 1
            pieces.append(jnp.broadcast_to(b[r:r + 1, :], (2 * s, w)))
        return pieces[0] if len(pieces) == 1 else jnp.concatenate(pieces, axis=0)
    b3 = b.reshape(n // SUBLANES, SUBLANES, w)
    sub = lax.broadcasted_iota(I32, b3.shape, 1)

    def bc(r):
        return jnp.broadcast_to(b3[:, r:r + 1, :], b3.shape)

    out = bc(s - 1)
    for p in range(1, SUBLANES // (2 * s)):
        out = jnp.where(sub >= p * 2 * s, bc(p * 2 * s + s - 1), out)
    return out.reshape(n, w)


def _gla_tile(q, k, v, la, s_prev):
    n = q.shape[0]
    row = lax.broadcasted_iota(I32, (n, n), 0)
    col = lax.broadcasted_iota(I32, (n, n), 1)
    tri = jnp.where(row >= col, 1.0, 0.0).astype(BF16)
    hi, mid, lo = _split3(la)
    b = _dot(tri, hi) + _dot(tri, mid) + _dot(tri, lo)
    b_last = b[n - 1:n, :]
    kl = k * jnp.exp(b_last - b)

    rowi = lax.broadcasted_iota(I32, (n, GLA_KW), 0)
    levels = []
    s = n // 2
    while s >= 1:
        m = _hold_rows(b, s)
        second = (rowi & s) != 0
        qs = _split2(q * jnp.exp(jnp.where(second, b - m, NEG_BIG)))
        ks = _split2(k * jnp.exp(jnp.where(second, NEG_BIG, m - b)))
        levels.append((2 * s, qs, ks))
        s //= 2
    levels.append((1, _split2(q), _split2(k)))
    qe = _split2(q * jnp.exp(b))

    lane = lax.broadcasted_iota(I32, (n, LANES), 1)
    xor = row ^ col
    zero = jnp.zeros((), BF16)

    def head_part(parts, sl, hm):
        return tuple(jnp.where(hm, x[:, sl], zero) for x in parts)

    outs = []
    for h in range(GLA_HEADS):
        p = h // 2
        sl = slice(p * LANES, (p + 1) * LANES)
        hm = (lane < GLA_DK) if h % 2 == 0 else (lane >= GLA_DK)
        att = None
        for span, qs, ks in levels:
            a_s = _dot_nt(_lhs3(head_part(qs, sl, hm)), _rhs3_lanes((ks[0][:, sl], ks[1][:, sl])))
            att = a_s if att is None else jnp.where(xor < span, a_s, att)
        o_h = _dot(_lhs3(_split2(att)), _rhs3_rows(v[:, h * GLA_DV:(h + 1) * GLA_DV]))
        o_h = o_h + _dot(_lhs3(head_part(qe, sl, hm)), _rhs3_rows(s_prev[sl, :]))
        outs.append(o_h)

    decay = jnp.exp(b_last)
    upper = lax.broadcasted_iota(I32, (LANES, LANES), 0) < GLA_DK
    s_new = []
    for p in range(GLA_HEADS // 2):
        sl = slice(p * LANES, (p + 1) * LANES)
        kl_t = jnp.transpose(kl[:, sl])
        u = _dot(_lhs3(_split2(kl_t)), _rhs3_rows(v[:, p * 2 * GLA_DV:(p + 1) * 2 * GLA_DV]))
        upd = jnp.where(upper, u[:, 0:GLA_DV], u[:, GLA_DV:2 * GLA_DV])
        dcol = jnp.transpose(jnp.broadcast_to(decay[:, sl], (LANES, LANES)))
        s_new.append(dcol * s_prev[sl, :] + upd)
    return outs, jnp.concatenate(s_new, axis=0)


def _split_projection(z):
    c0 = 0
    q = z[:, c0:c0 + GLA_KW] * (GLA_DK ** -0.5); c0 += GLA_KW
    k = z[:, c0:c0 + GLA_KW]; c0 += GLA_KW
    v = z[:, c0:c0 + GLA_VW]; c0 += GLA_VW
    g = z[:, c0:c0 + GLA_VW]; c0 += GLA_VW
    ua = z[:, c0:c0 + CONV_CH]; c0 += CONV_CH
    ug = z[:, c0:c0 + CONV_CH]; c0 += CONV_CH
    a_lr = z[:, c0:c0 + A_PAD]
    return q, k, v, g, ua, ug, a_lr


def _mix0_project(o_heads, g, y, gng_ref, w_out_ref):
    sl = slice(GLA_VW, GLA_VW + CONV_CH)
    m = _dot_w3(y, w_out_ref[0, sl, :], w_out_ref[1, sl, :])
    for hd in range(GLA_HEADS):
        sl = slice(hd * GLA_DV, (hd + 1) * GLA_DV)
        o_h = _standardize(o_heads[hd]) * gng_ref[:, sl] * _silu(g[:, sl])
        m = m + _dot_w3(o_h, w_out_ref[0, sl, :], w_out_ref[1, sl, :])
    return m


def _mix0_inputs(h, w_in_ref, w_a2_ref, b_a_ref):
    z = _dot_w3(h, w_in_ref[0], w_in_ref[1])
    q, k, v, g, ua, ug, a_lr = _split_projection(z)
    la = _log_sigmoid(_dot_w3(a_lr, w_a2_ref[0], w_a2_ref[1]) + b_a_ref[...]) * (1.0 / GLA_TAU)
    return q, k, v, g, ua, ug, la


def _mixer0_prompt_kernel(x_ref, moda_ref, modb_ref, w_in_ref, w_a2_ref, b_a_ref, gng_ref,
                          cw_ref, cb_ref, clg_ref, clb_ref, w_out_ref, lng_ref, lnb_ref,
                          wr_hi_ref, wr_lo_ref, br_ref,
                          x1_ref, h2c_ref, route_ref, cnt_ref, nconv_ref, ngla_ref,
                          s_ref, cbuf_ref, carry_ref):
    t = pl.program_id(1)
    n = x_ref.shape[1]

    @pl.when((t == 0) & (pl.program_id(0) == 0))
    def _():
        carry_ref[...] = jnp.zeros_like(carry_ref)

    @pl.when(t == 0)
    def _():
        s_ref[...] = jnp.zeros_like(s_ref)
        cbuf_ref[0:CONV_HALO, :] = jnp.zeros((CONV_HALO, CONV_CH), F32)

    x = x_ref[0]
    shift, scale, gate = _mod3(moda_ref[0])
    h = x * (1.0 + scale) + shift
    q, k, v, g, ua, ug, la = _mix0_inputs(h, w_in_ref, w_a2_ref, b_a_ref)
    o_heads, s_new = _gla_tile(q, k, v, la, s_ref[...])
    s_ref[...] = s_new

    glu = ua * _sigmoid(ug)
    cbuf_ref[CONV_HALO:CONV_HALO + n, :] = glu
    acc = jnp.broadcast_to(cb_ref[...], (n, CONV_CH))
    base = CONV_HALO - CONV_BUF
    for j in range(CONV_WIDTH):
        acc = acc + cw_ref[j:j + 1, :] * cbuf_ref[base + j:base + j + n, :]
    y = _silu(_layer_norm(acc, clg_ref[...], clb_ref[...]))

    @pl.when(t == pl.num_programs(1) - 1)
    def _():
        nconv_ref[0, 0] = cbuf_ref[CONV_HALO + n - CONV_BUF:CONV_HALO + n, :]
        ngla_ref[0, 0] = s_new.reshape(GLA_HEADS, GLA_DK, GLA_DV)

    cbuf_ref[0:CONV_HALO, :] = cbuf_ref[n:n + CONV_HALO, :]

    m = _mix0_project(o_heads, g, y, gng_ref, w_out_ref)
    carry = _post_mixer(x, m, gate, lng_ref, lnb_ref, modb_ref[0], wr_hi_ref, wr_lo_ref, br_ref,
                        carry_ref[...], x1_ref, h2c_ref, route_ref)
    carry_ref[...] = carry
    cnt_ref[...] = carry


def _mixer0_prompt_call(x, mod_a, mod_b, p):
    bsz, seq, _ = x.shape
    tl = TILE_L
    nt = seq // tl
    row3 = lambda b, t: (b, 0, 0)
    tok = lambda b, t: (b * nt + t, 0)
    return pl.pallas_call(
        _mixer0_prompt_kernel,
        grid=(bsz, nt),
        in_specs=[
            pl.BlockSpec((1, tl, D_MODEL), lambda b, t: (b, t, 0)),
            pl.BlockSpec((1, 1, 3 * D_MODEL), row3),
            pl.BlockSpec((1, 1, 3 * D_MODEL), row3),
            _const((2, D_MODEL, Z_WIDTH)),
            _const((2, A_PAD, GLA_KW)),
            _full((1, GLA_KW)),
            _full((1, GLA_VW)),
            _full((CONV_W_ROWS, CONV_CH)),
            _full((1, CONV_CH)),
            _full((1, CONV_CH)),
            _full((1, CONV_CH)),
            _const((2, GLA_VW + CONV_CH, D_MODEL)),
            _full((1, D_MODEL)),
            _full((1, D_MODEL)),
            _full((D_MODEL, LANES)),
            _full((D_MODEL, LANES)),
            _full((1, LANES)),
        ],
        out_specs=_token_out_specs(tl, tok) + [
            _full((1, LANES)),
            pl.BlockSpec((1, 1, CONV_BUF, CONV_CH), lambda b, t: (0, b, 0, 0)),
            pl.BlockSpec((1, 1, GLA_HEADS, GLA_DK, GLA_DV), lambda b, t: (0, b, 0, 0, 0)),
        ],
        out_shape=_token_out_shapes(bsz * seq) + [
            jax.ShapeDtypeStruct((1, LANES), F32),
            jax.ShapeDtypeStruct((1, bsz, CONV_BUF, CONV_CH), F32),
            jax.ShapeDtypeStruct((1, bsz, GLA_HEADS, GLA_DK, GLA_DV), F32),
        ],
        scratch_shapes=[
            pltpu.VMEM((GLA_KW, GLA_DV), F32),
            pltpu.VMEM((CONV_HALO + tl, CONV_CH), F32),
            pltpu.VMEM((1, LANES), F32),
        ],
        compiler_params=_params("arbitrary", "arbitrary"),
        name="mixer0_prompt",
    )(x, mod_a, mod_b, p["w_in"], p["w_a2"], p["b_a"], p["gng"], p["conv_w"], p["conv_b"],
      p["conv_ln_g"], p["conv_ln_b"], p["w_out0"], p["ln_g00"], p["ln_b00"],
      p["wr_hi0"], p["wr_lo0"], p["br0"])


def _mixer0_sample_kernel(x_ref, moda_ref, modb_ref, sgla_ref, sconv_ref, cnt_in_ref,
                          w_in_ref, w_a2_ref,
                          b_a_ref, gng_ref, cw_ref, cb_ref, clg_ref, clb_ref, w_out_ref, lng_ref,
                          lnb_ref, wr_hi_ref, wr_lo_ref, br_ref,
                          x1_ref, h2c_ref, route_ref, cnt_ref, ngla_ref, nconv_ref,
                          zt_ref, v_ref, g_ref, glu_ref, o_ref, y_ref):
    i = pl.program_id(0)
    nb = sgla_ref.shape[1]
    ntok = x_ref.shape[0]

    @pl.when(i == 0)
    def _():
        shift, scale, _ = _mod3(moda_ref[...])
        h = x_ref[...] * (1.0 + scale) + shift
        q, k, v, g, ua, ug, la = _mix0_inputs(h, w_in_ref, w_a2_ref, b_a_ref)
        v_ref[...] = v
        g_ref[...] = g
        glu_ref[...] = ua * _sigmoid(ug)
        for j, val in enumerate((jnp.exp(la), k, q)):
            hi, lo = _split2(jnp.transpose(val))
            zt_ref[(2 * j) * GLA_KW:(2 * j + 1) * GLA_KW, :] = hi
            zt_ref[(2 * j + 1) * GLA_KW:(2 * j + 2) * GLA_KW, :] = lo

    tok_row = lax.broadcasted_iota(I32, (ntok, LANES), 0)
    blk = pl.ds(pl.multiple_of(i * nb, nb), nb)
    v_blk = v_ref[blk, :]
    glu_blk = glu_ref[blk, :]
    o_rows, y_rows = [], []
    for n in range(nb):
        onehot = jnp.where(tok_row == i * nb + n, 1.0, 0.0).astype(BF16)
        cols = _dot(zt_ref[...], onehot)
        a_col = cols[0:GLA_KW] + cols[GLA_KW:2 * GLA_KW]
        k_col = cols[2 * GLA_KW:3 * GLA_KW] + cols[3 * GLA_KW:4 * GLA_KW]
        q_col = cols[4 * GLA_KW:5 * GLA_KW] + cols[5 * GLA_KW:6 * GLA_KW]
        v_row = v_blk[n:n + 1, :]
        v_b = jnp.concatenate(
            [jnp.broadcast_to(v_row[:, h * GLA_DV:(h + 1) * GLA_DV], (GLA_DK, GLA_DV))
             for h in range(GLA_HEADS)], axis=0)
        s_old = sgla_ref[0, n].reshape(GLA_KW, GLA_DV)
        s_new = a_col * s_old + k_col * v_b
        ngla_ref[0, n] = s_new.reshape(GLA_HEADS, GLA_DK, GLA_DV)
        o4 = jnp.sum((q_col * s_new).reshape(GLA_HEADS, GLA_DK, GLA_DV), axis=1)
        o_rows.append(jnp.concatenate([o4[h:h + 1, :] for h in range(GLA_HEADS)], axis=1))
        glu_row = glu_blk[n:n + 1, :]
        past = sconv_ref[0, n]
        y_rows.append(jnp.sum(past * cw_ref[0:CONV_BUF, :], axis=0, keepdims=True)
                      + glu_row * cw_ref[CONV_BUF:CONV_WIDTH, :] + cb_ref[...])
        nconv_ref[0, n, 0:CONV_BUF - 1, :] = sconv_ref[0, n, 1:CONV_BUF, :]
        nconv_ref[0, n, CONV_BUF - 1:CONV_BUF, :] = glu_row
    o_ref[blk, :] = jnp.concatenate(o_rows, axis=0)
    y_ref[blk, :] = jnp.concatenate(y_rows, axis=0)

    @pl.when(i == pl.num_programs(0) - 1)
    def _():
        y = _silu(_layer_norm(y_ref[...], clg_ref[...], clb_ref[...]))
        o_heads = [o_ref[:, hd * GLA_DV:(hd + 1) * GLA_DV] for hd in range(GLA_HEADS)]
        m = _mix0_project(o_heads, g_ref[...], y, gng_ref, w_out_ref)
        _, _, gate = _mod3(moda_ref[...])
        cnt_ref[...] = _post_mixer(x_ref[...], m, gate, lng_ref, lnb_ref, modb_ref[...], wr_hi_ref,
                                   wr_lo_ref, br_ref, cnt_in_ref[...], x1_ref, h2c_ref, route_ref)


def _mixer0_sample_call(x, mod_a, mod_b, state_gla, state_conv, cnt, p):
    ntok = x.shape[0]
    nb = SAMPLE_BLK
    tok = lambda i: (0, 0)
    return pl.pallas_call(
        _mixer0_sample_kernel,
        grid=(ntok // nb,),
        in_specs=[
            _full((ntok, D_MODEL)),
            _full((ntok, 3 * D_MODEL)),
            _full((ntok, 3 * D_MODEL)),
            pl.BlockSpec((1, nb, GLA_HEADS, GLA_DK, GLA_DV), lambda i: (0, i, 0, 0, 0)),
            pl.BlockSpec((1, nb, CONV_BUF, CONV_CH), lambda i: (0, i, 0, 0)),
            _full((1, LANES)),
            _const((2, D_MODEL, Z_WIDTH)),
            _const((2, A_PAD, GLA_KW)),
            _full((1, GLA_KW)),
            _full((1, GLA_VW)),
            _full((CONV_W_ROWS, CONV_CH)),
            _full((1, CONV_CH)),
            _full((1, CONV_CH)),
            _full((1, CONV_CH)),
            _const((2, GLA_VW + CONV_CH, D_MODEL)),
            _full((1, D_MODEL)),
            _full((1, D_MODEL)),
            _full((D_MODEL, LANES)),
            _full((D_MODEL, LANES)),
            _full((1, LANES)),
        ],
        out_specs=_token_out_specs(ntok, tok) + [
            _full((1, LANES)),
            pl.BlockSpec((1, nb, GLA_HEADS, GLA_DK, GLA_DV), lambda i: (0, i, 0, 0, 0)),
            pl.BlockSpec((1, nb, CONV_BUF, CONV_CH), lambda i: (0, i, 0, 0)),
        ],
        out_shape=_token_out_shapes(ntok) + [
            jax.ShapeDtypeStruct((1, LANES), F32),
            jax.ShapeDtypeStruct(state_gla.shape, F32),
            jax.ShapeDtypeStruct(state_conv.shape, F32),
        ],
        scratch_shapes=[
            pltpu.VMEM((6 * GLA_KW, ntok), BF16),
            pltpu.VMEM((ntok, GLA_VW), F32),
            pltpu.VMEM((ntok, GLA_VW), F32),
            pltpu.VMEM((ntok, CONV_CH), F32),
            pltpu.VMEM((ntok, GLA_VW), F32),
            pltpu.VMEM((ntok, CONV_CH), F32),
        ],
        compiler_params=_params("arbitrary"),
        name="mixer0_sample",
    )(x, mod_a, mod_b, state_gla, state_conv, cnt, p["w_in"], p["w_a2"], p["b_a"],
      p["gng"], p["conv_w"], p["conv_b"], p["conv_ln_g"], p["conv_ln_b"], p["w_out0"],
      p["ln_g00"], p["ln_b00"], p["wr_hi0"], p["wr_lo0"], p["br0"])


def _pool_project(pooled, h, wg_ref, ps_ref, w_out_ref):
    m = None
    for gi in range(len(POOL_WINDOWS)):
        sl = slice(gi * POOL_GC, (gi + 1) * POOL_GC)
        mixed = _dot((pooled[:, sl] - h[:, sl]).astype(BF16), wg_ref[gi]) * ps_ref[:, sl]
        part = _dot(mixed.astype(BF16), w_out_ref[sl, :])
        m = part if m is None else m + part
    return m


def _mixer1_prompt_kernel(x_ref, moda_ref, modb_ref, wg_ref, ps_ref, w_out_ref, lng_ref, lnb_ref,
                          wr_hi_ref, wr_lo_ref, br_ref,
                          x1_ref, h2c_ref, route_ref, cnt_ref, npool_ref,
                          pbuf_ref, carry_ref):
    t = pl.program_id(1)
    n = x_ref.shape[1]

    @pl.when((t == 0) & (pl.program_id(0) == 0))
    def _():
        carry_ref[...] = jnp.zeros_like(carry_ref)

    @pl.when(t == 0)
    def _():
        pbuf_ref[0:POOL_HALO, :] = jnp.zeros((POOL_HALO, D_MODEL), F32)

    x = x_ref[0]
    shift, scale, gate = _mod3(moda_ref[0])
    h = x * (1.0 + scale) + shift
    pbuf_ref[POOL_HALO:POOL_HALO + n, :] = h

    cur = pbuf_ref[...]
    sums = []
    for gi, w in enumerate(POOL_WINDOWS):
        cur = cur + pltpu.roll(cur, w // 2, axis=0)
        sums.append(cur[POOL_HALO:, 0:POOL_GC])
        if gi + 1 < len(POOL_WINDOWS):
            cur = cur[:, POOL_GC:]
    pos = lax.broadcasted_iota(I32, (n, POOL_GC), 0) + t * n
    pooled = jnp.concatenate(
        [s / jnp.minimum(w, pos + 1).astype(F32) for s, w in zip(sums, POOL_WINDOWS)], axis=1)

    @pl.when(t == pl.num_programs(1) - 1)
    def _():
        npool_ref[0, 0] = pbuf_ref[POOL_HALO + n - POOL_BUF:POOL_HALO + n, :]

    pbuf_ref[0:POOL_HALO, :] = pbuf_ref[n:n + POOL_HALO, :]

    m = _pool_project(pooled, h, wg_ref, ps_ref, w_out_ref)
    carry = _post_mixer(x, m, gate, lng_ref, lnb_ref, modb_ref[0], wr_hi_ref, wr_lo_ref, br_ref,
                        carry_ref[...], x1_ref, h2c_ref, route_ref)
    carry_ref[...] = carry
    cnt_ref[...] = carry


def _mixer1_prompt_call(x, mod_a, mod_b, p):
    bsz, seq, _ = x.shape
    tl = TILE_L
    nt = seq // tl
    row3 = lambda b, t: (b, 0, 0)
    tok = lambda b, t: (b * nt + t, 0)
    ng = len(POOL_WINDOWS)
    return pl.pallas_call(
        _mixer1_prompt_kernel,
        grid=(bsz, nt),
        in_specs=[
            pl.BlockSpec((1, tl, D_MODEL), lambda b, t: (b, t, 0)),
            pl.BlockSpec((1, 1, 3 * D_MODEL), row3),
            pl.BlockSpec((1, 1, 3 * D_MODEL), row3),
            _full((ng, POOL_GC, POOL_GC)),
            _full((1, D_MODEL)),
            _full((D_MODEL, D_MODEL)),
            _full((1, D_MODEL)),
            _full((1, D_MODEL)),
            _full((D_MODEL, LANES)),
            _full((D_MODEL, LANES)),
            _full((1, LANES)),
        ],
        out_specs=_token_out_specs(tl, tok) + [
            _full((1, LANES)),
            pl.BlockSpec((1, 1, POOL_BUF, D_MODEL), lambda b, t: (0, b, 0, 0)),
        ],
        out_shape=_token_out_shapes(bsz * seq) + [
            jax.ShapeDtypeStruct((1, LANES), F32),
            jax.ShapeDtypeStruct((1, bsz, POOL_BUF, D_MODEL), F32),
        ],
        scratch_shapes=[pltpu.VMEM((POOL_HALO + tl, D_MODEL), F32), pltpu.VMEM((1, LANES), F32)],
        compiler_params=_params("arbitrary", "arbitrary"),
        name="mixer1_prompt",
    )(x, mod_a, mod_b, p["w_grp"], p["pool_scale"], p["w_out1"], p["ln_g10"], p["ln_b10"],
      p["wr_hi1"], p["wr_lo1"], p["br1"])


def _mixer1_sample_kernel(x_ref, moda_ref, modb_ref, spool_ref, cnt_in_ref,
                          wg_ref, ps_ref, w_out_ref,
                          lng_ref, lnb_ref, wr_hi_ref, wr_lo_ref, br_ref,
                          x1_ref, h2c_ref, route_ref, cnt_ref, npool_ref, h_ref, pooled_ref):
    i = pl.program_id(0)
    nb = spool_ref.shape[1]

    @pl.when(i == 0)
    def _():
        shift, scale, _ = _mod3(moda_ref[...])
        h_ref[...] = x_ref[...] * (1.0 + scale) + shift

    lane = lax.broadcasted_iota(I32, (POOL_BUF, D_MODEL), 1)
    rowi = lax.broadcasted_iota(I32, (POOL_BUF, D_MODEL), 0)
    first = jnp.zeros((POOL_BUF, D_MODEL), I32)
    lane1 = lax.broadcasted_iota(I32, (1, D_MODEL), 1)
    inv_w = jnp.zeros((1, D_MODEL), F32)
    for gi, w in enumerate(POOL_WINDOWS):
        in_g = (lane >= gi * POOL_GC) & (lane < (gi + 1) * POOL_GC)
        first = jnp.where(in_g, POOL_BUF - (w - 1), first)
        in_g1 = (lane1 >= gi * POOL_GC) & (lane1 < (gi + 1) * POOL_GC)
        inv_w = jnp.where(in_g1, 1.0 / w, inv_w)
    keep = rowi >= first

    blk = pl.ds(pl.multiple_of(i * nb, nb), nb)
    h_blk = h_ref[blk, :]
    rows = []
    for n in range(nb):
        h_row = h_blk[n:n + 1, :]
        past = spool_ref[0, n]
        total = jnp.sum(jnp.where(keep, past, 0.0), axis=0, keepdims=True) + h_row
        rows.append(total * inv_w)
        npool_ref[0, n, 0:POOL_BUF - 1, :] = spool_ref[0, n, 1:POOL_BUF, :]
        npool_ref[0, n, POOL_BUF - 1:POOL_BUF, :] = h_row
    pooled_ref[blk, :] = jnp.concatenate(rows, axis=0)

    @pl.when(i == pl.num_programs(0) - 1)
    def _():
        m = _pool_project(pooled_ref[...], h_ref[...], wg_ref, ps_ref, w_out_ref)
        _, _, gate = _mod3(moda_ref[...])
        cnt_ref[...] = _post_mixer(x_ref[...], m, gate, lng_ref, lnb_ref, modb_ref[...], wr_hi_ref,
                                   wr_lo_ref, br_ref, cnt_in_ref[...], x1_ref, h2c_ref, route_ref)


def _mixer1_sample_call(x, mod_a, mod_b, state_pool, cnt, p):
    ntok = x.shape[0]
    nb = SAMPLE_BLK
    ng = len(POOL_WINDOWS)
    tok = lambda i: (0, 0)
    return pl.pallas_call(
        _mixer1_sample_kernel,
        grid=(ntok // nb,),
        in_specs=[
            _full((ntok, D_MODEL)),
            _full((ntok, 3 * D_MODEL)),
            _full((ntok, 3 * D_MODEL)),
            pl.BlockSpec((1, nb, POOL_BUF, D_MODEL), lambda i: (0, i, 0, 0)),
            _full((1, LANES)),
            _full((ng, POOL_GC, POOL_GC)),
            _full((1, D_MODEL)),
            _full((D_MODEL, D_MODEL)),
            _full((1, D_MODEL)),
            _full((1, D_MODEL)),
            _full((D_MODEL, LANES)),
            _full((D_MODEL, LANES)),
            _full((1, LANES)),
        ],
        out_specs=_token_out_specs(ntok, tok) + [
            _full((1, LANES)),
            pl.BlockSpec((1, nb, POOL_BUF, D_MODEL), lambda i: (0, i, 0, 0)),
        ],
        out_shape=_token_out_shapes(ntok) + [
            jax.ShapeDtypeStruct((1, LANES), F32),
            jax.ShapeDtypeStruct(state_pool.shape, F32),
        ],
        scratch_shapes=[pltpu.VMEM((ntok, D_MODEL), F32), pltpu.VMEM((ntok, D_MODEL), F32)],
        compiler_params=_params("arbitrary"),
        name="mixer1_sample",
    )(x, mod_a, mod_b, state_pool, cnt, p["w_grp"], p["pool_scale"], p["w_out1"],
      p["ln_g10"], p["ln_b10"], p["wr_hi1"], p["wr_lo1"], p["br1"])


def _dispatch_kernel(dest_ref, srcp_any, srcs_any, init_any, dst_any, sems):
    del init_any
    i = pl.program_id(0)
    n = dest_ref.shape[2]
    nprompt = srcp_any.shape[0] // (n * CHUNKS)
    slot = i % 2

    def start_all(src_any, tile):
        def body(j, c):
            d = dest_ref[0, 0, j]
            pltpu.make_async_copy(
                src_any.at[pl.ds(pl.multiple_of((tile * n + j) * CHUNKS, CHUNKS), CHUNKS)],
                dst_any.at[pl.ds(pl.multiple_of(d * CHUNKS, CHUNKS), CHUNKS)],
                sems.at[slot]).start()
            return c
        lax.fori_loop(0, n, body, 0, unroll=8)

    def drain(sem):
        def body(j, c):
            pltpu.make_async_copy(srcp_any.at[pl.ds(0, CHUNKS)], dst_any.at[pl.ds(0, CHUNKS)],
                                  sem).wait()
            return c
        lax.fori_loop(0, n, body, 0, unroll=8)

    @pl.when(i < nprompt)
    def _():
        start_all(srcp_any, i)

    @pl.when(i >= nprompt)
    def _():
        start_all(srcs_any, i - nprompt)

    @pl.when(i > 0)
    def _():
        drain(sems.at[1 - slot])

    @pl.when(i == pl.num_programs(0) - 1)
    def _():
        drain(sems.at[slot])


def _dispatch_call(dest3, h2c_p, h2c_s, sorted_init):
    ntile = dest3.shape[0]
    anyspec = pl.BlockSpec(memory_space=pl.ANY)
    return pl.pallas_call(
        _dispatch_kernel,
        grid=(ntile,),
        in_specs=[
            pl.BlockSpec((1, 1, ROW_TILE), lambda i: (i, 0, 0), memory_space=pltpu.SMEM),
            anyspec, anyspec, anyspec,
        ],
        out_specs=anyspec,
        out_shape=jax.ShapeDtypeStruct(sorted_init.shape, F32),
        scratch_shapes=[pltpu.SemaphoreType.DMA((2,))],
        input_output_aliases={3: 0},
        compiler_params=_params("arbitrary"),
        name="dispatch",
    )(dest3, h2c_p, h2c_s, sorted_init)


def _moe_kernel(sched_ref, x_ref, wga_ref, wua_ref, wda_ref, wgb_ref, wub_ref, wdb_ref, y_ref):
    i = pl.program_id(0)
    rows = x_ref.shape[0] // CHUNKS

    @pl.when(sched_ref[3, i] == 1)
    def _():
        x = _load_chunked(x_ref, rows, CHUNKS, 0).astype(BF16)
        for first, (wg, wu, wd) in ((0, (wga_ref, wua_ref, wda_ref)),
                                    (CHUNKS, (wgb_ref, wub_ref, wdb_ref))):
            hid = _silu(_dot(x, wg[0])) * _dot(x, wu[0])
            y = _dot(hid.astype(BF16), wd[0])
            for c in range(CHUNKS):
                y_ref[pl.ds(first + c, rows, stride=2 * CHUNKS), :] = y[:, c * LANES:(c + 1) * LANES]

    @pl.when(sched_ref[3, i] == 0)
    def _():
        y_ref[...] = jnp.zeros_like(y_ref)


def _moe_call(sched, sorted_x, wg, wu, wd):
    ntile = sched.shape[1]
    tm = MOE_TILE
    ea = lambda i, s: (s[1, i], 0, 0)
    eb = lambda i, s: (s[2, i], 0, 0)
    grid_spec = pltpu.PrefetchScalarGridSpec(
        num_scalar_prefetch=1,
        grid=(ntile,),
        in_specs=[
            pl.BlockSpec((tm * CHUNKS, LANES), lambda i, s: (s[0, i], 0)),
            pl.BlockSpec((1, D_MODEL, MOE_FF), ea),
            pl.BlockSpec((1, D_MODEL, MOE_FF), ea),
            pl.BlockSpec((1, MOE_FF, D_MODEL), ea),
            pl.BlockSpec((1, D_MODEL, MOE_FF), eb),
            pl.BlockSpec((1, D_MODEL, MOE_FF), eb),
            pl.BlockSpec((1, MOE_FF, D_MODEL), eb),
        ],
        out_specs=pl.BlockSpec((tm * 2 * CHUNKS, LANES), lambda i, s: (i, 0)),
    )
    return pl.pallas_call(
        _moe_kernel,
        grid_spec=grid_spec,
        out_shape=jax.ShapeDtypeStruct((ntile * tm * 2 * CHUNKS, LANES), F32),
        compiler_params=_params("arbitrary"),
        name="moe",
    )(sched, sorted_x, wg, wu, wd, wg, wu, wd)


def _combine_kernel(dest_ref, dnext_ref, x1_ref, mod_ref, route_ref, y_any, lng_ref, lnb_ref,
                    out_ref, buf_ref, sems):
    i = pl.program_id(0)
    nstep = pl.num_programs(0)
    n = x1_ref.shape[0]
    rec = 2 * CHUNKS
    slot = i % 2

    def row_copy(d_ref, j, s):
        d = d_ref[0, 0, j]
        return pltpu.make_async_copy(
            y_any.at[pl.ds(pl.multiple_of(d * rec, rec), rec)],
            buf_ref.at[s, pl.ds(pl.multiple_of(j * rec, rec), rec)], sems.at[s])

    def start_all(d_ref, s):
        def body(j, c):
            row_copy(d_ref, j, s).start()
            return c
        lax.fori_loop(0, n, body, 0, unroll=8)

    @pl.when(i == 0)
    def _():
        start_all(dest_ref, 0)

    @pl.when(i + 1 < nstep)
    def _():
        start_all(dnext_ref, 1 - slot)

    def wait_all(j, c):
        pltpu.make_async_copy(y_any.at[pl.ds(0, rec)], buf_ref.at[slot, pl.ds(0, rec)],
                              sems.at[slot]).wait()
        return c
    lax.fori_loop(0, n, wait_all, 0, unroll=8)

    ya = _load_chunked(buf_ref.at[slot], n, rec, 0)
    yb = _load_chunked(buf_ref.at[slot], n, rec, CHUNKS)
    lane = lax.broadcasted_iota(I32, route_ref.shape, 1)
    route = route_ref[...]
    w_a = jnp.sum(jnp.where(lane == 0, route, 0.0), axis=-1, keepdims=True)
    w_b = jnp.sum(jnp.where(lane == 1, route, 0.0), axis=-1, keepdims=True)
    _, _, gate = _mod3(mod_ref[0])
    out_ref[...] = _layer_norm(ALPHA * x1_ref[...] + (1.0 + gate) * (w_a * ya + w_b * yb),
                               lng_ref[...], lnb_ref[...])


def _combine_call(dest3, x1, mod, route, y_sorted, lng, lnb, tile_off, tiles_per_mod):
    n = ROW_TILE
    ntile = x1.shape[0] // n
    mrows = mod.shape[1]
    tok = lambda i: (i, 0)
    return pl.pallas_call(
        _combine_kernel,
        grid=(ntile,),
        in_specs=[
            pl.BlockSpec((1, 1, n), lambda i: (i + tile_off, 0, 0), memory_space=pltpu.SMEM),
            pl.BlockSpec((1, 1, n), lambda i: (jnp.minimum(i + 1, ntile - 1) + tile_off, 0, 0),
                         memory_space=pltpu.SMEM),
            pl.BlockSpec((n, D_MODEL), tok),
            pl.BlockSpec((1, mrows, 3 * D_MODEL), lambda i: (i // tiles_per_mod, 0, 0)),
            pl.BlockSpec((n, LANES), tok),
            pl.BlockSpec(memory_space=pl.ANY),
            _full((1, D_MODEL)),
            _full((1, D_MODEL)),
        ],
        out_specs=pl.BlockSpec((n, D_MODEL), lambda i: (i, 0)),
        out_shape=jax.ShapeDtypeStruct((ntile * n, D_MODEL), F32),
        scratch_shapes=[pltpu.VMEM((2, n * 2 * CHUNKS, LANES), F32), pltpu.SemaphoreType.DMA((2,))],
        compiler_params=_params("arbitrary"),
        name="combine",
    )(dest3, dest3, x1, mod, route, y_sorted, lng, lnb)


def _moe_plan(route_p, route_s, cnt):
    tm = MOE_TILE
    ntok_all = route_p.shape[0] + route_s.shape[0]
    ntile = -(-ntok_all // tm) + N_BUCKETS
    bucket = jnp.concatenate([route_p[:, 2], route_s[:, 2]]).astype(I32)
    rank = jnp.concatenate([route_p[:, 3], route_s[:, 3]]).astype(I32)
    counts = cnt[0, :N_BUCKETS].astype(I32)
    tiles_b = (counts + tm - 1) // tm
    tile_end = jnp.cumsum(tiles_b)
    row_start = (tile_end - tiles_b) * tm
    dest = row_start[bucket] + rank
    used = tile_end[-1]
    ti = jnp.minimum(jnp.arange(ntile, dtype=I32), used - 1)
    b_of = jnp.searchsorted(tile_end, ti, side="right").astype(I32)
    grp, pair = b_of // N_PAIRS, b_of % N_PAIRS
    ea = grp * MOE_EXPERTS + jnp.asarray(PAIR_A, I32)[pair]
    eb = grp * MOE_EXPERTS + jnp.asarray(PAIR_B, I32)[pair]
    valid = (jnp.arange(ntile, dtype=I32) < used).astype(I32)
    sched = jnp.stack([ti, ea, eb, valid])
    return dest.reshape(ntok_all // ROW_TILE, 1, ROW_TILE), sched


def _router_weights(w_coarse, b_coarse, w_fine, b_fine):
    wf = jnp.transpose(w_fine, (1, 0, 2)).reshape(D_MODEL, N_EXPERTS)
    w = jnp.concatenate([w_coarse, wf], axis=1)
    w = jnp.pad(w, ((0, 0), (0, LANES - w.shape[1])))
    b = jnp.concatenate([b_coarse, b_fine.reshape(N_EXPERTS)])
    b = jnp.pad(b, (0, LANES - b.shape[0])).reshape(1, LANES)
    hi_lo = _hi_lo(w)
    return hi_lo[0], hi_lo[1], b


def _prep_params(ln_g, ln_b, w_in_even, w_a2, b_a, gla_norm_g, conv_w, conv_b, conv_ln_g,
                 conv_ln_b, w_out_even, w_grp_pool, pool_scale, w_out_odd, w_coarse, b_coarse,
                 w_fine, b_fine):
    p = {}
    w_in = w_in_even[0]
    o_q, o_k, o_v, o_g = 0, GLA_KW, 2 * GLA_KW, 2 * GLA_KW + GLA_VW
    o_a = o_g + GLA_VW
    o_u = o_a + GLA_RANK
    w_in_r = jnp.concatenate(
        [w_in[:, o_q:o_a], w_in[:, o_u:o_u + 2 * CONV_CH], w_in[:, o_a:o_u],
         jnp.zeros((D_MODEL, A_PAD - GLA_RANK), F32)], axis=1)
    p["w_in"] = _hi_lo(w_in_r)
    p["w_a2"] = _hi_lo(jnp.pad(w_a2[0], ((0, A_PAD - GLA_RANK), (0, 0))))
    p["b_a"] = b_a[0].reshape(1, GLA_KW)
    p["gng"] = gla_norm_g[0].reshape(1, GLA_VW)
    p["conv_w"] = jnp.pad(conv_w[0], ((0, CONV_W_ROWS - CONV_WIDTH), (0, 0)))
    p["conv_b"] = conv_b[0].reshape(1, CONV_CH)
    p["conv_ln_g"] = conv_ln_g[0].reshape(1, CONV_CH)
    p["conv_ln_b"] = conv_ln_b[0].reshape(1, CONV_CH)
    p["w_out0"] = _hi_lo(w_out_even[0])
    p["w_grp"] = w_grp_pool[0].astype(BF16)
    p["pool_scale"] = pool_scale[0].reshape(1, D_MODEL)
    p["w_out1"] = w_out_odd[0].astype(BF16)
    for layer in range(DEPTH):
        for j in range(2):
            p[f"ln_g{layer}{j}"] = ln_g[layer, j].reshape(1, D_MODEL)
            p[f"ln_b{layer}{j}"] = ln_b[layer, j].reshape(1, D_MODEL)
        hi, lo, b = _router_weights(w_coarse[layer], b_coarse[layer], w_fine[layer], b_fine[layer])
        p[f"wr_hi{layer}"], p[f"wr_lo{layer}"], p[f"br{layer}"] = hi, lo, b
    return p


def kernel(x_prompt, x_sample, state_gla, state_conv, state_pool, c_prompt, c_sample, w_ada, b_ada,
           ln_g, ln_b, w_in_even, w_a2, b_a, gla_norm_g, conv_w, conv_b, conv_ln_g, conv_ln_b,
           w_out_even, w_grp_pool, pool_scale, w_out_odd, w_coarse, b_coarse, w_fine, b_fine,
           w_gate, w_up, w_down):
    bsz, seq, _ = x_prompt.shape
    nsmp = x_sample.shape[0]
    ntok_p = bsz * seq
    ntok_all = ntok_p + nsmp
    assert seq % TILE_L == 0 and seq % ROW_TILE == 0 and nsmp == ROW_TILE and ntok_p % nsmp == 0
    p = _prep_params(ln_g, ln_b, w_in_even, w_a2, b_a, gla_norm_g, conv_w, conv_b, conv_ln_g,
                     conv_ln_b, w_out_even, w_grp_pool, pool_scale, w_out_odd, w_coarse, b_coarse,
                     w_fine, b_fine)
    wg = w_gate.reshape(DEPTH, N_EXPERTS, D_MODEL, MOE_FF).astype(BF16)
    wu = w_up.reshape(DEPTH, N_EXPERTS, D_MODEL, MOE_FF).astype(BF16)
    wd = w_down.reshape(DEPTH, N_EXPERTS, MOE_FF, D_MODEL).astype(BF16)

    c_all = jnp.concatenate([c_prompt, c_sample], axis=0)
    mod = _ada_call(c_all, w_ada.reshape(2 * DEPTH, D_MODEL, 3 * D_MODEL),
                    b_ada.reshape(2 * DEPTH, 1, 3 * D_MODEL))
    mod_p = mod[:, :bsz].reshape(2 * DEPTH, bsz, 1, 3 * D_MODEL)
    mod_s = mod[:, bsz:]

    def moe(tok_p, tok_s, cnt, layer):
        x1_p, h2c_p, route_p = tok_p
        x1_s, h2c_s, route_s = tok_s
        dest3, sched = _moe_plan(route_p, route_s, cnt)
        nrow = sched.shape[1] * MOE_TILE
        sorted_x = _dispatch_call(dest3, h2c_p, h2c_s, jnp.zeros((nrow * CHUNKS, LANES), F32))
        y_sorted = _moe_call(sched, sorted_x, wg[layer], wu[layer], wd[layer])
        lng, lnb = p[f"ln_g{layer}1"], p[f"ln_b{layer}1"]
        yp = _combine_call(dest3, x1_p, mod_p[2 * layer + 1], route_p, y_sorted, lng, lnb,
                           0, seq // ROW_TILE)
        ys = _combine_call(dest3, x1_s, mod_s[2 * layer + 1][None], route_s, y_sorted, lng, lnb,
                           ntok_p // ROW_TILE, 1)
        return yp.reshape(bsz, seq, D_MODEL), ys

    xs0 = x_sample.reshape(nsmp, D_MODEL)
    *tok0p, cnt0, conv_p, gla_p = _mixer0_prompt_call(x_prompt, mod_p[0], mod_p[1], p)
    *tok0s, cnt0, gla_s, conv_s = _mixer0_sample_call(xs0, mod_s[0], mod_s[1], state_gla,
                                                      state_conv, cnt0, p)
    x2p, x2s = moe(tok0p, tok0s, cnt0, 0)
    *tok1p, cnt1, pool_p = _mixer1_prompt_call(x2p, mod_p[2], mod_p[3], p)
    *tok1s, cnt1, pool_s = _mixer1_sample_call(x2s, mod_s[2], mod_s[3], state_pool, cnt1, p)
    x4p, x4s = moe(tok1p, tok1s, cnt1, 1)
    return (x4p, x4s.reshape(nsmp, 1, D_MODEL), gla_p, conv_p, pool_p, gla_s, conv_s, pool_s)
```

```python
import jax
import jax.numpy as jnp
from jax import lax
from jax.experimental import pallas as pl
from jax.experimental.pallas import tpu as pltpu

F32 = jnp.float32
BF16 = jnp.bfloat16
I32 = jnp.int32

D_MODEL = 1024
GLA_HEADS = 4
GLA_DK = 64
GLA_DV = 128
GLA_KW = GLA_HEADS * GLA_DK
GLA_VW = GLA_HEADS * GLA_DV
GLA_RANK = 16
GLA_TAU = 16.0
CONV_CH = 512
CONV_WIDTH = 31
CONV_BUF = CONV_WIDTH - 1
POOL_WINDOWS = (2, 4, 8, 16)
POOL_GC = D_MODEL // len(POOL_WINDOWS)
POOL_BUF = max(POOL_WINDOWS) - 1
MOE_GROUPS = 4
MOE_EXPERTS = 4
N_EXPERTS = MOE_GROUPS * MOE_EXPERTS
N_PAIRS = MOE_EXPERTS * (MOE_EXPERTS - 1) // 2
N_BUCKETS = MOE_GROUPS * N_PAIRS
PAIR_A = (0, 0, 0, 1, 1, 2)
PAIR_B = (1, 2, 3, 2, 3, 3)
MOE_FF = 512
DEPTH = 2
ALPHA = (2 * DEPTH) ** 0.25
LN_EPS = 1e-5

LANES = 128
SUBLANES = 8
CHUNKS = D_MODEL // LANES
A_PAD = LANES
Z_WIDTH = 2 * GLA_KW + 2 * GLA_VW + 2 * CONV_CH + A_PAD
NEG_BIG = -1e30
VMEM_LIMIT = 56 * 1024 * 1024

TILE_L = 256
CONV_HALO = 32
CONV_W_ROWS = 32
POOL_HALO = 16
SAMPLE_BLK = 16
MOE_TILE = 256
ROW_TILE = 128
DISPATCH_ROWS = 512


def _dot(a, b):
    return jnp.dot(a, b, preferred_element_type=F32)


def _dot_nt(a, b):
    return lax.dot_general(a, b, (((1,), (1,)), ((), ())), preferred_element_type=F32)


def _split3(x):
    hi = x.astype(BF16)
    r1 = x - hi.astype(F32)
    mid = r1.astype(BF16)
    lo = (r1 - mid.astype(F32)).astype(BF16)
    return hi, mid, lo


def _split2(x):
    hi = x.astype(BF16)
    lo = (x - hi.astype(F32)).astype(BF16)
    return hi, lo


def _dot_w3(a, w_hi, w_lo):
    a_hi, a_lo = _split2(a)
    return _dot(a_hi, w_hi) + (_dot(a_lo, w_hi) + _dot(a_hi, w_lo))


def _lhs3(parts):
    hi, lo = parts
    return jnp.concatenate([hi, lo, hi], axis=1)


def _rhs3_rows(x):
    hi, lo = _split2(x)
    return jnp.concatenate([hi, hi, lo], axis=0)


def _rhs3_lanes(parts):
    hi, lo = parts
    return jnp.concatenate([hi, hi, lo], axis=1)


def _layer_norm(x, g, b):
    mu = jnp.mean(x, axis=-1, keepdims=True)
    xc = x - mu
    var = jnp.mean(xc * xc, axis=-1, keepdims=True)
    return xc * lax.rsqrt(var + LN_EPS) * g + b


def _standardize(x):
    mu = jnp.mean(x, axis=-1, keepdims=True)
    xc = x - mu
    var = jnp.mean(xc * xc, axis=-1, keepdims=True)
    return xc * lax.rsqrt(var + LN_EPS)


def _sigmoid(x):
    return 1.0 / (1.0 + jnp.exp(-x))


def _silu(x):
    return x * _sigmoid(x)


def _log_sigmoid(x):
    return jnp.minimum(x, 0.0) - jnp.log(1.0 + jnp.exp(-jnp.abs(x)))


def _mod3(mod):
    return mod[:, 0:D_MODEL], mod[:, D_MODEL:2 * D_MODEL], mod[:, 2 * D_MODEL:3 * D_MODEL]


def _full(shape):
    nd = len(shape)
    return pl.BlockSpec(shape, lambda *_: (0,) * nd)


def _const(shape):
    nd = len(shape)
    return pl.BlockSpec(shape, lambda *_: (0,) * nd, pipeline_mode=pl.Buffered(1))


def _hi_lo(w):
    bits = lax.bitcast_convert_type(w, jnp.uint32) & jnp.uint32(0xFFFF0000)
    hi = lax.bitcast_convert_type(bits, F32)
    return jnp.stack([hi.astype(BF16), (w - hi).astype(BF16)])


def _params(*sem):
    return pltpu.CompilerParams(dimension_semantics=sem, vmem_limit_bytes=VMEM_LIMIT)


def _ada_kernel(c_ref, w_ref, b_ref, o_ref):
    w_hi, w_lo = _split2(w_ref[0])
    o_ref[0] = _dot_w3(_silu(c_ref[...]), w_hi, w_lo) + b_ref[0]


def _ada_call(c_all, w_ada, b_ada):
    n = c_all.shape[0]
    ncomb = w_ada.shape[0]
    tn = D_MODEL
    return pl.pallas_call(
        _ada_kernel,
        grid=(ncomb, 3 * D_MODEL // tn),
        in_specs=[
            pl.BlockSpec((n, D_MODEL), lambda i, j: (0, 0)),
            pl.BlockSpec((1, D_MODEL, tn), lambda i, j: (i, 0, j)),
            pl.BlockSpec((1, 1, tn), lambda i, j: (i, 0, j)),
        ],
        out_specs=pl.BlockSpec((1, n, tn), lambda i, j: (i, 0, j)),
        out_shape=jax.ShapeDtypeStruct((ncomb, n, 3 * D_MODEL), F32),
        compiler_params=_params("arbitrary", "arbitrary"),
        name="ada",
    )(c_all, w_ada, b_ada)


def _route(logits, carry):
    rows = logits.shape[0]
    lane = lax.broadcasted_iota(I32, (rows, LANES), 1)
    lanef = lane.astype(F32)
    big = float(LANES)
    lc = jnp.where(lane < MOE_GROUPS, logits, NEG_BIG)
    mc = jnp.max(lc, axis=-1, keepdims=True)
    pg = 1.0 / jnp.sum(jnp.exp(lc - mc), axis=-1, keepdims=True)
    gidx = jnp.min(jnp.where(lc == mc, lanef, big), axis=-1, keepdims=True)
    lo = float(MOE_GROUPS) + gidx * float(MOE_EXPERTS)
    in_grp = (lanef >= lo) & (lanef < lo + float(MOE_EXPERTS))
    lf = jnp.where(in_grp, logits, NEG_BIG)
    m1 = jnp.max(lf, axis=-1, keepdims=True)
    i1 = jnp.min(jnp.where(lf == m1, lanef, big), axis=-1, keepdims=True)
    lf2 = jnp.where(lanef == i1, NEG_BIG, lf)
    m2 = jnp.max(lf2, axis=-1, keepdims=True)
    i2 = jnp.min(jnp.where(lf2 == m2, lanef, big), axis=-1, keepdims=True)
    e2 = jnp.exp(m2 - m1)
    w1 = pg / (1.0 + e2)
    w2 = pg * e2 / (1.0 + e2)
    ea = jnp.minimum(i1, i2) - lo
    eb = jnp.maximum(i1, i2) - lo
    w_a = jnp.where(i1 < i2, w1, w2)
    w_b = jnp.where(i1 < i2, w2, w1)
    bucket = gidx * float(N_PAIRS) + ea * (7.0 - ea) * 0.5 + eb - ea - 1.0
    onehot = jnp.where(lanef == bucket, 1.0, 0.0)
    r = lax.broadcasted_iota(I32, (rows, rows), 0)
    c = lax.broadcasted_iota(I32, (rows, rows), 1)
    before = _dot(jnp.where(r > c, 1.0, 0.0).astype(BF16), onehot.astype(BF16))
    rank = jnp.sum(onehot * (before + carry), axis=-1, keepdims=True)
    new_carry = carry + jnp.sum(onehot, axis=0, keepdims=True)
    route = (jnp.where(lane == 0, w_a, 0.0) + jnp.where(lane == 1, w_b, 0.0)
             + jnp.where(lane == 2, bucket, 0.0) + jnp.where(lane == 3, rank, 0.0))
    return route, new_carry


def _store_chunked(ref, x):
    rows = x.shape[0]
    for c in range(CHUNKS):
        ref[pl.ds(c, rows, stride=CHUNKS), :] = x[:, c * LANES:(c + 1) * LANES]


def _load_chunked(ref, rows, group, first):
    return jnp.concatenate(
        [ref[pl.ds(first + c, rows, stride=group), :] for c in range(CHUNKS)], axis=1)


def _post_mixer(x, m, gate, lng_ref, lnb_ref, mod_b, wr_hi_ref, wr_lo_ref, br_ref, carry,
                x1_ref, h2c_ref, route_ref):
    x1 = _layer_norm(ALPHA * x + (1.0 + gate) * m, lng_ref[...], lnb_ref[...])
    shift2, scale2, _ = _mod3(mod_b)
    h2 = x1 * (1.0 + scale2) + shift2
    h_hi, h_lo = _split2(h2)
    wr_hi = wr_hi_ref[...]
    logits = _dot(h_hi, wr_hi) + _dot(h_lo, wr_hi) + _dot(h_hi, wr_lo_ref[...]) + br_ref[...]
    route, carry = _route(logits, carry)
    x1_ref[...] = x1
    _store_chunked(h2c_ref, h2)
    route_ref[...] = route
    return carry


def _token_out_specs(rows, index_map):
    return [pl.BlockSpec((rows, D_MODEL), index_map),
            pl.BlockSpec((rows * CHUNKS, LANES), index_map),
            pl.BlockSpec((rows, LANES), index_map)]


def _token_out_shapes(ntok):
    return [jax.ShapeDtypeStruct((ntok, D_MODEL), F32),
            jax.ShapeDtypeStruct((ntok * CHUNKS, LANES), F32),
            jax.ShapeDtypeStruct((ntok, LANES), F32)]


def _hold_rows(b, s):
    n, w = b.shape
    if s >= SUBLANES:
        pieces = []
        for p in range(n // (2 * s)):
            r = p * 2 * s + s - 1
            pieces.append(jnp.broadcast_to(b[r:r + 1, :], (2 * s, w)))
        return pieces[0] if len(pieces) == 1 else jnp.concatenate(pieces, axis=0)
    b3 = b.reshape(n // SUBLANES, SUBLANES, w)
    sub = lax.broadcasted_iota(I32, b3.shape, 1)

    def bc(r):
        return jnp.broadcast_to(b3[:, r:r + 1, :], b3.shape)

    out = bc(s - 1)
    for p in range(1, SUBLANES // (2 * s)):
        out = jnp.where(sub >= p * 2 * s, bc(p * 2 * s + s - 1), out)
    return out.reshape(n, w)


def _gla_tile(q, k, v, la, s_prev):
    n = q.shape[0]
    row = lax.broadcasted_iota(I32, (n, n), 0)
    col = lax.broadcasted_iota(I32, (n, n), 1)
    tri = jnp.where(row >= col, 1.0, 0.0).astype(BF16)
    hi, mid, lo = _split3(la)
    b = _dot(tri, hi) + _dot(tri, mid) + _dot(tri, lo)
    b_last = b[n - 1:n, :]
    kl = k * jnp.exp(b_last - b)

    rowi = lax.broadcasted_iota(I32, (n, GLA_KW), 0)
    levels = []
    s = n // 2
    while s >= 1:
        m = _hold_rows(b, s)
        second = (rowi & s) != 0
        qs = _split2(q * jnp.exp(jnp.where(second, b - m, NEG_BIG)))
        ks = _split2(k * jnp.exp(jnp.where(second, NEG_BIG, m - b)))
        levels.append((2 * s, qs, ks))
        s //= 2
    levels.append((1, _split2(q), _split2(k)))
    qe = _split2(q * jnp.exp(b))

    lane = lax.broadcasted_iota(I32, (n, LANES), 1)
    xor = row ^ col
    zero = jnp.zeros((), BF16)

    def head_part(parts, sl, hm):
        return tuple(jnp.where(hm, x[:, sl], zero) for x in parts)

    outs = []
    for h in range(GLA_HEADS):
        p = h // 2
        sl = slice(p * LANES, (p + 1) * LANES)
        hm = (lane < GLA_DK) if h % 2 == 0 else (lane >= GLA_DK)
        att = None
        for span, qs, ks in levels:
            a_s = _dot_nt(_lhs3(head_part(qs, sl, hm)), _rhs3_lanes((ks[0][:, sl], ks[1][:, sl])))
            att = a_s if att is None else jnp.where(xor < span, a_s, att)
        o_h = _dot(_lhs3(_split2(att)), _rhs3_rows(v[:, h * GLA_DV:(h + 1) * GLA_DV]))
        o_h = o_h + _dot(_lhs3(head_part(qe, sl, hm)), _rhs3_rows(s_prev[sl, :]))
        outs.append(o_h)

    decay = jnp.exp(b_last)
    upper = lax.broadcasted_iota(I32, (LANES, LANES), 0) < GLA_DK
    s_new = []
    for p in range(GLA_HEADS // 2):
        sl = slice(p * LANES, (p + 1) * LANES)
        kl_t = jnp.transpose(kl[:, sl])
        u = _dot(_lhs3(_split2(kl_t)), _rhs3_rows(v[:, p * 2 * GLA_DV:(p + 1) * 2 * GLA_DV]))
        upd = jnp.where(upper, u[:, 0:GLA_DV], u[:, GLA_DV:2 * GLA_DV])
        dcol = jnp.transpose(jnp.broadcast_to(decay[:, sl], (LANES, LANES)))
        s_new.append(dcol * s_prev[sl, :] + upd)
    return outs, jnp.concatenate(s_new, axis=0)


def _split_projection(z):
    c0 = 0
    q = z[:, c0:c0 + GLA_KW] * (GLA_DK ** -0.5); c0 += GLA_KW
    k = z[:, c0:c0 + GLA_KW]; c0 += GLA_KW
    v = z[:, c0:c0 + GLA_VW]; c0 += GLA_VW
    g = z[:, c0:c0 + GLA_VW]; c0 += GLA_VW
    ua = z[:, c0:c0 + CONV_CH]; c0 += CONV_CH
    ug = z[:, c0:c0 + CONV_CH]; c0 += CONV_CH
    a_lr = z[:, c0:c0 + A_PAD]
    return q, k, v, g, ua, ug, a_lr


def _mix0_project(o_heads, g, y, gng_ref, w_out_ref):
    sl = slice(GLA_VW, GLA_VW + CONV_CH)
    m = _dot_w3(y, w_out_ref[0, sl, :], w_out_ref[1, sl, :])
    for hd in range(GLA_HEADS):
        sl = slice(hd * GLA_DV, (hd + 1) * GLA_DV)
        o_h = _standardize(o_heads[hd]) * gng_ref[:, sl] * _silu(g[:, sl])
        m = m + _dot_w3(o_h, w_out_ref[0, sl, :], w_out_ref[1, sl, :])
    return m


def _mix0_inputs(h, w_in_ref, w_a2_ref, b_a_ref):
    z = _dot_w3(h, w_in_ref[0], w_in_ref[1])
    q, k, v, g, ua, ug, a_lr = _split_projection(z)
    la = _log_sigmoid(_dot_w3(a_lr, w_a2_ref[0], w_a2_ref[1]) + b_a_ref[...]) * (1.0 / GLA_TAU)
    return q, k, v, g, ua, ug, la


def _mixer0_prompt_kernel(x_ref, moda_ref, modb_ref, w_in_ref, w_a2_ref, b_a_ref, gng_ref,
                          cw_ref, cb_ref, clg_ref, clb_ref, w_out_ref, lng_ref, lnb_ref,
                          wr_hi_ref, wr_lo_ref, br_ref,
                          x1_ref, h2c_ref, route_ref, cnt_ref, nconv_ref, ngla_ref,
                          s_ref, cbuf_ref, shift_ref, carry_ref):
    t = pl.program_id(1)
    n = x_ref.shape[1]

    @pl.when((t == 0) & (pl.program_id(0) == 0))
    def _():
        carry_ref[...] = jnp.zeros_like(carry_ref)

    @pl.when(t == 0)
    def _():
        s_ref[...] = jnp.zeros_like(s_ref)
        cbuf_ref[0:CONV_HALO, :] = jnp.zeros((CONV_HALO, CONV_CH), F32)

    x = x_ref[0]
    shift, scale, gate = _mod3(moda_ref[0])
    h = x * (1.0 + scale) + shift
    q, k, v, g, ua, ug, la = _mix0_inputs(h, w_in_ref, w_a2_ref, b_a_ref)
    o_heads, s_new = _gla_tile(q, k, v, la, s_ref[...])
    s_ref[...] = s_new

    glu = ua * _sigmoid(ug)
    cbuf_ref[CONV_HALO:CONV_HALO + n, :] = glu
    acc = jnp.broadcast_to(cb_ref[...], (n, CONV_CH))
    base = CONV_HALO - CONV_BUF
    for r in range(SUBLANES):
        taps = range(r, CONV_WIDTH, SUBLANES)
        span = n + (len(taps) - 1) * SUBLANES
        if (base + r) % SUBLANES == 0:
            src, off = cbuf_ref, base + r
        else:
            shift_ref[r, 0:span, :] = cbuf_ref[base + r:base + r + span, :]
            src, off = shift_ref.at[r], 0
        for a, j in enumerate(taps):
            lo = off + a * SUBLANES
            acc = acc + cw_ref[j:j + 1, :] * src[lo:lo + n, :]
    y = _silu(_layer_norm(acc, clg_ref[...], clb_ref[...]))

    @pl.when(t == pl.num_programs(1) - 1)
    def _():
        nconv_ref[0, 0] = cbuf_ref[CONV_HALO + n - CONV_BUF:CONV_HALO + n, :]
        ngla_ref[0, 0] = s_new.reshape(GLA_HEADS, GLA_DK, GLA_DV)

    cbuf_ref[0:CONV_HALO, :] = cbuf_ref[n:n + CONV_HALO, :]

    m = _mix0_project(o_heads, g, y, gng_ref, w_out_ref)
    carry = _post_mixer(x, m, gate, lng_ref, lnb_ref, modb_ref[0], wr_hi_ref, wr_lo_ref, br_ref,
                        carry_ref[...], x1_ref, h2c_ref, route_ref)
    carry_ref[...] = carry
    cnt_ref[...] = carry


def _mixer0_prompt_call(x, mod_a, mod_b, p):
    bsz, seq, _ = x.shape
    tl = TILE_L
    nt = seq // tl
    row3 = lambda b, t: (b, 0, 0)
    tok = lambda b, t: (b * nt + t, 0)
    return pl.pallas_call(
        _mixer0_prompt_kernel,
        grid=(bsz, nt),
        in_specs=[
            pl.BlockSpec((1, tl, D_MODEL), lambda b, t: (b, t, 0)),
            pl.BlockSpec((1, 1, 3 * D_MODEL), row3),
            pl.BlockSpec((1, 1, 3 * D_MODEL), row3),
            _const((2, D_MODEL, Z_WIDTH)),
            _const((2, A_PAD, GLA_KW)),
            _full((1, GLA_KW)),
            _full((1, GLA_VW)),
            _full((CONV_W_ROWS, CONV_CH)),
            _full((1, CONV_CH)),
            _full((1, CONV_CH)),
            _full((1, CONV_CH)),
            _const((2, GLA_VW + CONV_CH, D_MODEL)),
            _full((1, D_MODEL)),
            _full((1, D_MODEL)),
            _full((D_MODEL, LANES)),
            _full((D_MODEL, LANES)),
            _full((1, LANES)),
        ],
        out_specs=_token_out_specs(tl, tok) + [
            _full((1, LANES)),
            pl.BlockSpec((1, 1, CONV_BUF, CONV_CH), lambda b, t: (0, b, 0, 0)),
            pl.BlockSpec((1, 1, GLA_HEADS, GLA_DK, GLA_DV), lambda b, t: (0, b, 0, 0, 0)),
        ],
        out_shape=_token_out_shapes(bsz * seq) + [
            jax.ShapeDtypeStruct((1, LANES), F32),
            jax.ShapeDtypeStruct((1, bsz, CONV_BUF, CONV_CH), F32),
            jax.ShapeDtypeStruct((1, bsz, GLA_HEADS, GLA_DK, GLA_DV), F32),
        ],
        scratch_shapes=[
            pltpu.VMEM((GLA_KW, GLA_DV), F32),
            pltpu.VMEM((CONV_HALO + tl, CONV_CH), F32),
            pltpu.VMEM((SUBLANES, CONV_HALO + tl, CONV_CH), F32),
            pltpu.VMEM((1, LANES), F32),
        ],
        compiler_params=_params("arbitrary", "arbitrary"),
        name="mixer0_prompt",
    )(x, mod_a, mod_b, p["w_in"], p["w_a2"], p["b_a"], p["gng"], p["conv_w"], p["conv_b"],
      p["conv_ln_g"], p["conv_ln_b"], p["w_out0"], p["ln_g00"], p["ln_b00"],
      p["wr_hi0"], p["wr_lo0"], p["br0"])


def _mixer0_sample_kernel(x_ref, moda_ref, modb_ref, sgla_ref, sconv_ref, cnt_in_ref,
                          w_in_ref, w_a2_ref,
                          b_a_ref, gng_ref, cw_ref, cb_ref, clg_ref, clb_ref, w_out_ref, lng_ref,
                          lnb_ref, wr_hi_ref, wr_lo_ref, br_ref,
                          x1_ref, h2c_ref, route_ref, cnt_ref, ngla_ref, nconv_ref,
                          zt_ref, v_ref, g_ref, glu_ref, o_ref, y_ref):
    i = pl.program_id(0)
    nb = sgla_ref.shape[1]
    ntok = x_ref.shape[0]

    @pl.when(i == 0)
    def _():
        shift, scale, _ = _mod3(moda_ref[...])
        h = x_ref[...] * (1.0 + scale) + shift
        q, k, v, g, ua, ug, la = _mix0_inputs(h, w_in_ref, w_a2_ref, b_a_ref)
        v_ref[...] = v
        g_ref[...] = g
        glu_ref[...] = ua * _sigmoid(ug)
        for j, val in enumerate((jnp.exp(la), k, q)):
            hi, lo = _split2(jnp.transpose(val))
            zt_ref[(2 * j) * GLA_KW:(2 * j + 1) * GLA_KW, :] = hi
            zt_ref[(2 * j + 1) * GLA_KW:(2 * j + 2) * GLA_KW, :] = lo

    tok_row = lax.broadcasted_iota(I32, (ntok, LANES), 0)
    blk = pl.ds(pl.multiple_of(i * nb, nb), nb)
    v_blk = v_ref[blk, :]
    glu_blk = glu_ref[blk, :]
    o_rows, y_rows = [], []
    for n in range(nb):
        onehot = jnp.where(tok_row == i * nb + n, 1.0, 0.0).astype(BF16)
        cols = _dot(zt_ref[...], onehot)
        a_col = cols[0:GLA_KW] + cols[GLA_KW:2 * GLA_KW]
        k_col = cols[2 * GLA_KW:3 * GLA_KW] + cols[3 * GLA_KW:4 * GLA_KW]
        q_col = cols[4 * GLA_KW:5 * GLA_KW] + cols[5 * GLA_KW:6 * GLA_KW]
        v_row = v_blk[n:n + 1, :]
        v_b = jnp.concatenate(
            [jnp.broadcast_to(v_row[:, h * GLA_DV:(h + 1) * GLA_DV], (GLA_DK, GLA_DV))
             for h in range(GLA_HEADS)], axis=0)
        s_old = sgla_ref[0, n].reshape(GLA_KW, GLA_DV)
        s_new = a_col * s_old + k_col * v_b
        ngla_ref[0, n] = s_new.reshape(GLA_HEADS, GLA_DK, GLA_DV)
        o4 = jnp.sum((q_col * s_new).reshape(GLA_HEADS, GLA_DK, GLA_DV), axis=1)
        o_rows.append(jnp.concatenate([o4[h:h + 1, :] for h in range(GLA_HEADS)], axis=1))
        glu_row = glu_blk[n:n + 1, :]
        past = sconv_ref[0, n]
        y_rows.append(jnp.sum(past * cw_ref[0:CONV_BUF, :], axis=0, keepdims=True)
                      + glu_row * cw_ref[CONV_BUF:CONV_WIDTH, :] + cb_ref[...])
        nconv_ref[0, n, 0:CONV_BUF - 1, :] = sconv_ref[0, n, 1:CONV_BUF, :]
        nconv_ref[0, n, CONV_BUF - 1:CONV_BUF, :] = glu_row
    o_ref[blk, :] = jnp.concatenate(o_rows, axis=0)
    y_ref[blk, :] = jnp.concatenate(y_rows, axis=0)

    @pl.when(i == pl.num_programs(0) - 1)
    def _():
        y = _silu(_layer_norm(y_ref[...], clg_ref[...], clb_ref[...]))
        o_heads = [o_ref[:, hd * GLA_DV:(hd + 1) * GLA_DV] for hd in range(GLA_HEADS)]
        m = _mix0_project(o_heads, g_ref[...], y, gng_ref, w_out_ref)
        _, _, gate = _mod3(moda_ref[...])
        cnt_ref[...] = _post_mixer(x_ref[...], m, gate, lng_ref, lnb_ref, modb_ref[...], wr_hi_ref,
                                   wr_lo_ref, br_ref, cnt_in_ref[...], x1_ref, h2c_ref, route_ref)


def _mixer0_sample_call(x, mod_a, mod_b, state_gla, state_conv, cnt, p):
    ntok = x.shape[0]
    nb = SAMPLE_BLK
    tok = lambda i: (0, 0)
    return pl.pallas_call(
        _mixer0_sample_kernel,
        grid=(ntok // nb,),
        in_specs=[
            _full((ntok, D_MODEL)),
            _full((ntok, 3 * D_MODEL)),
            _full((ntok, 3 * D_MODEL)),
            pl.BlockSpec((1, nb, GLA_HEADS, GLA_DK, GLA_DV), lambda i: (0, i, 0, 0, 0)),
            pl.BlockSpec((1, nb, CONV_BUF, CONV_CH), lambda i: (0, i, 0, 0)),
            _full((1, LANES)),
            _const((2, D_MODEL, Z_WIDTH)),
            _const((2, A_PAD, GLA_KW)),
            _full((1, GLA_KW)),
            _full((1, GLA_VW)),
            _full((CONV_W_ROWS, CONV_CH)),
            _full((1, CONV_CH)),
            _full((1, CONV_CH)),
            _full((1, CONV_CH)),
            _const((2, GLA_VW + CONV_CH, D_MODEL)),
            _full((1, D_MODEL)),
            _full((1, D_MODEL)),
            _full((D_MODEL, LANES)),
            _full((D_MODEL, LANES)),
            _full((1, LANES)),
        ],
        out_specs=_token_out_specs(ntok, tok) + [
            _full((1, LANES)),
            pl.BlockSpec((1, nb, GLA_HEADS, GLA_DK, GLA_DV), lambda i: (0, i, 0, 0, 0)),
            pl.BlockSpec((1, nb, CONV_BUF, CONV_CH), lambda i: (0, i, 0, 0)),
        ],
        out_shape=_token_out_shapes(ntok) + [
            jax.ShapeDtypeStruct((1, LANES), F32),
            jax.ShapeDtypeStruct(state_gla.shape, F32),
            jax.ShapeDtypeStruct(state_conv.shape, F32),
        ],
        scratch_shapes=[
            pltpu.VMEM((6 * GLA_KW, ntok), BF16),
            pltpu.VMEM((ntok, GLA_VW), F32),
            pltpu.VMEM((ntok, GLA_VW), F32),
            pltpu.VMEM((ntok, CONV_CH), F32),
            pltpu.VMEM((ntok, GLA_VW), F32),
            pltpu.VMEM((ntok, CONV_CH), F32),
        ],
        compiler_params=_params("arbitrary"),
        name="mixer0_sample",
    )(x, mod_a, mod_b, state_gla, state_conv, cnt, p["w_in"], p["w_a2"], p["b_a"],
      p["gng"], p["conv_w"], p["conv_b"], p["conv_ln_g"], p["conv_ln_b"], p["w_out0"],
      p["ln_g00"], p["ln_b00"], p["wr_hi0"], p["wr_lo0"], p["br0"])


def _pool_project(pooled, h, wg_ref, ps_ref, w_out_ref):
    m = None
    for gi in range(len(POOL_WINDOWS)):
        sl = slice(gi * POOL_GC, (gi + 1) * POOL_GC)
        mixed = _dot((pooled[:, sl] - h[:, sl]).astype(BF16), wg_ref[gi]) * ps_ref[:, sl]
        part = _dot(mixed.astype(BF16), w_out_ref[sl, :])
        m = part if m is None else m + part
    return m


def _mixer1_prompt_kernel(x_ref, moda_ref, modb_ref, wg_ref, ps_ref, w_out_ref, lng_ref, lnb_ref,
                          wr_hi_ref, wr_lo_ref, br_ref,
                          x1_ref, h2c_ref, route_ref, cnt_ref, npool_ref,
                          pbuf_ref, carry_ref):
    t = pl.program_id(1)
    n = x_ref.shape[1]

    @pl.when((t == 0) & (pl.program_id(0) == 0))
    def _():
        carry_ref[...] = jnp.zeros_like(carry_ref)

    @pl.when(t == 0)
    def _():
        pbuf_ref[0:POOL_HALO, :] = jnp.zeros((POOL_HALO, D_MODEL), F32)

    x = x_ref[0]
    shift, scale, gate = _mod3(moda_ref[0])
    h = x * (1.0 + scale) + shift
    pbuf_ref[POOL_HALO:POOL_HALO + n, :] = h

    cur = pbuf_ref[...]
    sums = []
    for gi, w in enumerate(POOL_WINDOWS):
        cur = cur + pltpu.roll(cur, w // 2, axis=0)
        sums.append(cur[POOL_HALO:, 0:POOL_GC])
        if gi + 1 < len(POOL_WINDOWS):
            cur = cur[:, POOL_GC:]
    pos = lax.broadcasted_iota(I32, (n, POOL_GC), 0) + t * n
    pooled = jnp.concatenate(
        [s / jnp.minimum(w, pos + 1).astype(F32) for s, w in zip(sums, POOL_WINDOWS)], axis=1)

    @pl.when(t == pl.num_programs(1) - 1)
    def _():
        npool_ref[0, 0] = pbuf_ref[POOL_HALO + n - POOL_BUF:POOL_HALO + n, :]

    pbuf_ref[0:POOL_HALO, :] = pbuf_ref[n:n + POOL_HALO, :]

    m = _pool_project(pooled, h, wg_ref, ps_ref, w_out_ref)
    carry = _post_mixer(x, m, gate, lng_ref, lnb_ref, modb_ref[0], wr_hi_ref, wr_lo_ref, br_ref,
                        carry_ref[...], x1_ref, h2c_ref, route_ref)
    carry_ref[...] = carry
    cnt_ref[...] = carry


def _mixer1_prompt_call(x, mod_a, mod_b, p):
    bsz, seq, _ = x.shape
    tl = TILE_L
    nt = seq // tl
    row3 = lambda b, t: (b, 0, 0)
    tok = lambda b, t: (b * nt + t, 0)
    ng = len(POOL_WINDOWS)
    return pl.pallas_call(
        _mixer1_prompt_kernel,
        grid=(bsz, nt),
        in_specs=[
            pl.BlockSpec((1, tl, D_MODEL), lambda b, t: (b, t, 0)),
            pl.BlockSpec((1, 1, 3 * D_MODEL), row3),
            pl.BlockSpec((1, 1, 3 * D_MODEL), row3),
            _full((ng, POOL_GC, POOL_GC)),
            _full((1, D_MODEL)),
            _full((D_MODEL, D_MODEL)),
            _full((1, D_MODEL)),
            _full((1, D_MODEL)),
            _full((D_MODEL, LANES)),
            _full((D_MODEL, LANES)),
            _full((1, LANES)),
        ],
        out_specs=_token_out_specs(tl, tok) + [
            _full((1, LANES)),
            pl.BlockSpec((1, 1, POOL_BUF, D_MODEL), lambda b, t: (0, b, 0, 0)),
        ],
        out_shape=_token_out_shapes(bsz * seq) + [
            jax.ShapeDtypeStruct((1, LANES), F32),
            jax.ShapeDtypeStruct((1, bsz, POOL_BUF, D_MODEL), F32),
        ],
        scratch_shapes=[pltpu.VMEM((POOL_HALO + tl, D_MODEL), F32), pltpu.VMEM((1, LANES), F32)],
        compiler_params=_params("arbitrary", "arbitrary"),
        name="mixer1_prompt",
    )(x, mod_a, mod_b, p["w_grp"], p["pool_scale"], p["w_out1"], p["ln_g10"], p["ln_b10"],
      p["wr_hi1"], p["wr_lo1"], p["br1"])


def _mixer1_sample_kernel(x_ref, moda_ref, modb_ref, spool_ref, cnt_in_ref,
                          wg_ref, ps_ref, w_out_ref,
                          lng_ref, lnb_ref, wr_hi_ref, wr_lo_ref, br_ref,
                          x1_ref, h2c_ref, route_ref, cnt_ref, npool_ref, h_ref, pooled_ref):
    i = pl.program_id(0)
    nb = spool_ref.shape[1]

    @pl.when(i == 0)
    def _():
        shift, scale, _ = _mod3(moda_ref[...])
        h_ref[...] = x_ref[...] * (1.0 + scale) + shift

    lane = lax.broadcasted_iota(I32, (POOL_BUF, D_MODEL), 1)
    rowi = lax.broadcasted_iota(I32, (POOL_BUF, D_MODEL), 0)
    first = jnp.zeros((POOL_BUF, D_MODEL), I32)
    lane1 = lax.broadcasted_iota(I32, (1, D_MODEL), 1)
    inv_w = jnp.zeros((1, D_MODEL), F32)
    for gi, w in enumerate(POOL_WINDOWS):
        in_g = (lane >= gi * POOL_GC) & (lane < (gi + 1) * POOL_GC)
        first = jnp.where(in_g, POOL_BUF - (w - 1), first)
        in_g1 = (lane1 >= gi * POOL_GC) & (lane1 < (gi + 1) * POOL_GC)
        inv_w = jnp.where(in_g1, 1.0 / w, inv_w)
    keep = rowi >= first

    blk = pl.ds(pl.multiple_of(i * nb, nb), nb)
    h_blk = h_ref[blk, :]
    rows = []
    for n in range(nb):
        h_row = h_blk[n:n + 1, :]
        past = spool_ref[0, n]
        total = jnp.sum(jnp.where(keep, past, 0.0), axis=0, keepdims=True) + h_row
        rows.append(total * inv_w)
        npool_ref[0, n, 0:POOL_BUF - 1, :] = spool_ref[0, n, 1:POOL_BUF, :]
        npool_ref[0, n, POOL_BUF - 1:POOL_BUF, :] = h_row
    pooled_ref[blk, :] = jnp.concatenate(rows, axis=0)

    @pl.when(i == pl.num_programs(0) - 1)
    def _():
        m = _pool_project(pooled_ref[...], h_ref[...], wg_ref, ps_ref, w_out_ref)
        _, _, gate = _mod3(moda_ref[...])
        cnt_ref[...] = _post_mixer(x_ref[...], m, gate, lng_ref, lnb_ref, modb_ref[...], wr_hi_ref,
                                   wr_lo_ref, br_ref, cnt_in_ref[...], x1_ref, h2c_ref, route_ref)


def _mixer1_sample_call(x, mod_a, mod_b, state_pool, cnt, p):
    ntok = x.shape[0]
    nb = SAMPLE_BLK
    ng = len(POOL_WINDOWS)
    tok = lambda i: (0, 0)
    return pl.pallas_call(
        _mixer1_sample_kernel,
        grid=(ntok // nb,),
        in_specs=[
            _full((ntok, D_MODEL)),
            _full((ntok, 3 * D_MODEL)),
            _full((ntok, 3 * D_MODEL)),
            pl.BlockSpec((1, nb, POOL_BUF, D_MODEL), lambda i: (0, i, 0, 0)),
            _full((1, LANES)),
            _full((ng, POOL_GC, POOL_GC)),
            _full((1, D_MODEL)),
            _full((D_MODEL, D_MODEL)),
            _full((1, D_MODEL)),
            _full((1, D_MODEL)),
            _full((D_MODEL, LANES)),
            _full((D_MODEL, LANES)),
            _full((1, LANES)),
        ],
        out_specs=_token_out_specs(ntok, tok) + [
            _full((1, LANES)),
            pl.BlockSpec((1, nb, POOL_BUF, D_MODEL), lambda i: (0, i, 0, 0)),
        ],
        out_shape=_token_out_shapes(ntok) + [
            jax.ShapeDtypeStruct((1, LANES), F32),
            jax.ShapeDtypeStruct(state_pool.shape, F32),
        ],
        scratch_shapes=[pltpu.VMEM((ntok, D_MODEL), F32), pltpu.VMEM((ntok, D_MODEL), F32)],
        compiler_params=_params("arbitrary"),
        name="mixer1_sample",
    )(x, mod_a, mod_b, state_pool, cnt, p["w_grp"], p["pool_scale"], p["w_out1"],
      p["ln_g10"], p["ln_b10"], p["wr_hi1"], p["wr_lo1"], p["br1"])


def _dispatch_kernel(dest_ref, src_ref, init_any, dst_any, sem):
    del init_any
    n = dest_ref.shape[2]

    def row_copy(j):
        d = dest_ref[0, 0, j]
        return pltpu.make_async_copy(
            src_ref.at[pl.ds(pl.multiple_of(j * CHUNKS, CHUNKS), CHUNKS)],
            dst_any.at[pl.ds(pl.multiple_of(d * CHUNKS, CHUNKS), CHUNKS)], sem)

    def start(j, c):
        row_copy(j).start()
        return c
    lax.fori_loop(0, n, start, 0, unroll=8)

    def wait(j, c):
        row_copy(j).wait()
        return c
    lax.fori_loop(0, n, wait, 0, unroll=8)


def _dispatch_call(dest, h2c, sorted_buf, rows):
    ntok = dest.shape[0]
    return pl.pallas_call(
        _dispatch_kernel,
        grid=(ntok // rows,),
        in_specs=[
            pl.BlockSpec((1, 1, rows), lambda i: (i, 0, 0), memory_space=pltpu.SMEM),
            pl.BlockSpec((rows * CHUNKS, LANES), lambda i: (i, 0)),
            pl.BlockSpec(memory_space=pl.ANY),
        ],
        out_specs=pl.BlockSpec(memory_space=pl.ANY),
        out_shape=jax.ShapeDtypeStruct(sorted_buf.shape, F32),
        scratch_shapes=[pltpu.SemaphoreType.DMA],
        input_output_aliases={2: 0},
        compiler_params=_params("arbitrary"),
        name="dispatch",
    )(dest.reshape(ntok // rows, 1, rows), h2c, sorted_buf)


def _moe_kernel(sched_ref, x_ref, wga_ref, wua_ref, wda_ref, wgb_ref, wub_ref, wdb_ref, y_ref):
    i = pl.program_id(0)
    rows = x_ref.shape[0] // CHUNKS

    @pl.when(sched_ref[3, i] == 1)
    def _():
        x = _load_chunked(x_ref, rows, CHUNKS, 0).astype(BF16)
        for first, (wg, wu, wd) in ((0, (wga_ref, wua_ref, wda_ref)),
                                    (CHUNKS, (wgb_ref, wub_ref, wdb_ref))):
            hid = _silu(_dot(x, wg[0])) * _dot(x, wu[0])
            y = _dot(hid.astype(BF16), wd[0])
            for c in range(CHUNKS):
                y_ref[pl.ds(first + c, rows, stride=2 * CHUNKS), :] = y[:, c * LANES:(c + 1) * LANES]

    @pl.when(sched_ref[3, i] == 0)
    def _():
        y_ref[...] = jnp.zeros_like(y_ref)


def _moe_call(sched, sorted_x, wg, wu, wd):
    ntile = sched.shape[1]
    tm = MOE_TILE
    ea = lambda i, s: (s[1, i], 0, 0)
    eb = lambda i, s: (s[2, i], 0, 0)
    grid_spec = pltpu.PrefetchScalarGridSpec(
        num_scalar_prefetch=1,
        grid=(ntile,),
        in_specs=[
            pl.BlockSpec((tm * CHUNKS, LANES), lambda i, s: (s[0, i], 0)),
            pl.BlockSpec((1, D_MODEL, MOE_FF), ea),
            pl.BlockSpec((1, D_MODEL, MOE_FF), ea),
            pl.BlockSpec((1, MOE_FF, D_MODEL), ea),
            pl.BlockSpec((1, D_MODEL, MOE_FF), eb),
            pl.BlockSpec((1, D_MODEL, MOE_FF), eb),
            pl.BlockSpec((1, MOE_FF, D_MODEL), eb),
        ],
        out_specs=pl.BlockSpec((tm * 2 * CHUNKS, LANES), lambda i, s: (i, 0)),
    )
    return pl.pallas_call(
        _moe_kernel,
        grid_spec=grid_spec,
        out_shape=jax.ShapeDtypeStruct((ntile * tm * 2 * CHUNKS, LANES), F32),
        compiler_params=_params("arbitrary"),
        name="moe",
    )(sched, sorted_x, wg, wu, wd, wg, wu, wd)


def _combine_kernel(dest_ref, dnext_ref, x1_ref, mod_ref, route_ref, y_any, lng_ref, lnb_ref,
                    out_ref, buf_ref, sems):
    i = pl.program_id(0)
    nstep = pl.num_programs(0)
    n = x1_ref.shape[0]
    rec = 2 * CHUNKS
    slot = i % 2

    def row_copy(d_ref, j, s):
        d = d_ref[0, 0, j]
        return pltpu.make_async_copy(
            y_any.at[pl.ds(pl.multiple_of(d * rec, rec), rec)],
            buf_ref.at[s, pl.ds(pl.multiple_of(j * rec, rec), rec)], sems.at[s])

    def start_all(d_ref, s):
        def body(j, c):
            row_copy(d_ref, j, s).start()
            return c
        lax.fori_loop(0, n, body, 0, unroll=8)

    @pl.when(i == 0)
    def _():
        start_all(dest_ref, 0)

    @pl.when(i + 1 < nstep)
    def _():
        start_all(dnext_ref, 1 - slot)

    def wait_all(j, c):
        pltpu.make_async_copy(y_any.at[pl.ds(0, rec)], buf_ref.at[slot, pl.ds(0, rec)],
                              sems.at[slot]).wait()
        return c
    lax.fori_loop(0, n, wait_all, 0, unroll=8)

    ya = _load_chunked(buf_ref.at[slot], n, rec, 0)
    yb = _load_chunked(buf_ref.at[slot], n, rec, CHUNKS)
    lane = lax.broadcasted_iota(I32, route_ref.shape, 1)
    route = route_ref[...]
    w_a = jnp.sum(jnp.where(lane == 0, route, 0.0), axis=-1, keepdims=True)
    w_b = jnp.sum(jnp.where(lane == 1, route, 0.0), axis=-1, keepdims=True)
    _, _, gate = _mod3(mod_ref[0])
    out_ref[...] = _layer_norm(ALPHA * x1_ref[...] + (1.0 + gate) * (w_a * ya + w_b * yb),
                               lng_ref[...], lnb_ref[...])


def _combine_call(dest, x1, mod, route, y_sorted, lng, lnb, tiles_per_mod):
    n = ROW_TILE
    ntile = x1.shape[0] // n
    mrows = mod.shape[1]
    tok = lambda i: (i, 0)
    tile_off = 0
    dest3 = dest.reshape(ntile, 1, n)
    return pl.pallas_call(
        _combine_kernel,
        grid=(ntile,),
        in_specs=[
            pl.BlockSpec((1, 1, n), lambda i: (i + tile_off, 0, 0), memory_space=pltpu.SMEM),
            pl.BlockSpec((1, 1, n), lambda i: (jnp.minimum(i + 1, ntile - 1) + tile_off, 0, 0),
                         memory_space=pltpu.SMEM),
            pl.BlockSpec((n, D_MODEL), tok),
            pl.BlockSpec((1, mrows, 3 * D_MODEL), lambda i: (i // tiles_per_mod, 0, 0)),
            pl.BlockSpec((n, LANES), tok),
            pl.BlockSpec(memory_space=pl.ANY),
            _full((1, D_MODEL)),
            _full((1, D_MODEL)),
        ],
        out_specs=pl.BlockSpec((n, D_MODEL), lambda i: (i, 0)),
        out_shape=jax.ShapeDtypeStruct((ntile * n, D_MODEL), F32),
        scratch_shapes=[pltpu.VMEM((2, n * 2 * CHUNKS, LANES), F32), pltpu.SemaphoreType.DMA((2,))],
        compiler_params=_params("arbitrary"),
        name="combine",
    )(dest3, dest3, x1, mod, route, y_sorted, lng, lnb)


def _moe_plan(route_p, route_s, cnt):
    tm = MOE_TILE
    ntok_all = route_p.shape[0] + route_s.shape[0]
    ntile = -(-ntok_all // tm) + N_BUCKETS
    bucket = jnp.concatenate([route_p[:, 2], route_s[:, 2]]).astype(I32)
    rank = jnp.concatenate([route_p[:, 3], route_s[:, 3]]).astype(I32)
    counts = cnt[0, :N_BUCKETS].astype(I32)
    tiles_b = (counts + tm - 1) // tm
    tile_end = jnp.cumsum(tiles_b)
    row_start = (tile_end - tiles_b) * tm
    dest = row_start[bucket] + rank
    used = tile_end[-1]
    ti = jnp.minimum(jnp.arange(ntile, dtype=I32), used - 1)
    b_of = jnp.sum((ti[:, None] >= tile_end[None, :]).astype(I32), axis=1)
    grp, pair = b_of // N_PAIRS, b_of % N_PAIRS
    ea = grp * MOE_EXPERTS + jnp.asarray(PAIR_A, I32)[pair]
    eb = grp * MOE_EXPERTS + jnp.asarray(PAIR_B, I32)[pair]
    valid = (jnp.arange(ntile, dtype=I32) < used).astype(I32)
    sched = jnp.stack([ti, ea, eb, valid])
    return dest, sched


def _router_weights(w_coarse, b_coarse, w_fine, b_fine):
    wf = jnp.transpose(w_fine, (1, 0, 2)).reshape(D_MODEL, N_EXPERTS)
    w = jnp.concatenate([w_coarse, wf], axis=1)
    w = jnp.pad(w, ((0, 0), (0, LANES - w.shape[1])))
    b = jnp.concatenate([b_coarse, b_fine.reshape(N_EXPERTS)])
    b = jnp.pad(b, (0, LANES - b.shape[0])).reshape(1, LANES)
    hi_lo = _hi_lo(w)
    return hi_lo[0], hi_lo[1], b


def _prep_params(ln_g, ln_b, w_in_even, w_a2, b_a, gla_norm_g, conv_w, conv_b, conv_ln_g,
                 conv_ln_b, w_out_even, w_grp_pool, pool_scale, w_out_odd, w_coarse, b_coarse,
                 w_fine, b_fine):
    p = {}
    w_in = w_in_even[0]
    o_q, o_k, o_v, o_g = 0, GLA_KW, 2 * GLA_KW, 2 * GLA_KW + GLA_VW
    o_a = o_g + GLA_VW
    o_u = o_a + GLA_RANK
    w_in_r = jnp.concatenate(
        [w_in[:, o_q:o_a], w_in[:, o_u:o_u + 2 * CONV_CH], w_in[:, o_a:o_u],
         jnp.zeros((D_MODEL, A_PAD - GLA_RANK), F32)], axis=1)
    p["w_in"] = _hi_lo(w_in_r)
    p["w_a2"] = _hi_lo(jnp.pad(w_a2[0], ((0, A_PAD - GLA_RANK), (0, 0))))
    p["b_a"] = b_a[0].reshape(1, GLA_KW)
    p["gng"] = gla_norm_g[0].reshape(1, GLA_VW)
    p["conv_w"] = jnp.pad(conv_w[0], ((0, CONV_W_ROWS - CONV_WIDTH), (0, 0)))
    p["conv_b"] = conv_b[0].reshape(1, CONV_CH)
    p["conv_ln_g"] = conv_ln_g[0].reshape(1, CONV_CH)
    p["conv_ln_b"] = conv_ln_b[0].reshape(1, CONV_CH)
    p["w_out0"] = _hi_lo(w_out_even[0])
    p["w_grp"] = w_grp_pool[0].astype(BF16)
    p["pool_scale"] = pool_scale[0].reshape(1, D_MODEL)
    p["w_out1"] = w_out_odd[0].astype(BF16)
    for layer in range(DEPTH):
        for j in range(2):
            p[f"ln_g{layer}{j}"] = ln_g[layer, j].reshape(1, D_MODEL)
            p[f"ln_b{layer}{j}"] = ln_b[layer, j].reshape(1, D_MODEL)
        hi, lo, b = _router_weights(w_coarse[layer], b_coarse[layer], w_fine[layer], b_fine[layer])
        p[f"wr_hi{layer}"], p[f"wr_lo{layer}"], p[f"br{layer}"] = hi, lo, b
    return p


def kernel(x_prompt, x_sample, state_gla, state_conv, state_pool, c_prompt, c_sample, w_ada, b_ada,
           ln_g, ln_b, w_in_even, w_a2, b_a, gla_norm_g, conv_w, conv_b, conv_ln_g, conv_ln_b,
           w_out_even, w_grp_pool, pool_scale, w_out_odd, w_coarse, b_coarse, w_fine, b_fine,
           w_gate, w_up, w_down):
    bsz, seq, _ = x_prompt.shape
    nsmp = x_sample.shape[0]
    ntok_p = bsz * seq
    ntok_all = ntok_p + nsmp
    assert seq % TILE_L == 0 and seq % ROW_TILE == 0 and nsmp == ROW_TILE and ntok_p % nsmp == 0
    p = _prep_params(ln_g, ln_b, w_in_even, w_a2, b_a, gla_norm_g, conv_w, conv_b, conv_ln_g,
                     conv_ln_b, w_out_even, w_grp_pool, pool_scale, w_out_odd, w_coarse, b_coarse,
                     w_fine, b_fine)
    wg = w_gate.reshape(DEPTH, N_EXPERTS, D_MODEL, MOE_FF).astype(BF16)
    wu = w_up.reshape(DEPTH, N_EXPERTS, D_MODEL, MOE_FF).astype(BF16)
    wd = w_down.reshape(DEPTH, N_EXPERTS, MOE_FF, D_MODEL).astype(BF16)

    c_all = jnp.concatenate([c_prompt, c_sample], axis=0)
    mod = _ada_call(c_all, w_ada.reshape(2 * DEPTH, D_MODEL, 3 * D_MODEL),
                    b_ada.reshape(2 * DEPTH, 1, 3 * D_MODEL))
    mod_p = mod[:, :bsz].reshape(2 * DEPTH, bsz, 1, 3 * D_MODEL)
    mod_s = mod[:, bsz:]

    def moe(tok_p, tok_s, cnt, layer):
        x1_p, h2c_p, route_p = tok_p
        x1_s, h2c_s, route_s = tok_s
        dest, sched = _moe_plan(route_p, route_s, cnt)
        dest_p, dest_s = dest[:ntok_p], dest[ntok_p:]
        nrow = sched.shape[1] * MOE_TILE
        sorted_x = jnp.zeros((nrow * CHUNKS, LANES), F32)
        sorted_x = _dispatch_call(dest_p, h2c_p, sorted_x, DISPATCH_ROWS)
        sorted_x = _dispatch_call(dest_s, h2c_s, sorted_x, nsmp)
        y_sorted = _moe_call(sched, sorted_x, wg[layer], wu[layer], wd[layer])
        lng, lnb = p[f"ln_g{layer}1"], p[f"ln_b{layer}1"]
        yp = _combine_call(dest_p, x1_p, mod_p[2 * layer + 1], route_p, y_sorted, lng, lnb,
                           seq // ROW_TILE)
        ys = _combine_call(dest_s, x1_s, mod_s[2 * layer + 1][None], route_s, y_sorted, lng, lnb, 1)
        return yp.reshape(bsz, seq, D_MODEL), ys

    xs0 = x_sample.reshape(nsmp, D_MODEL)
    *tok0p, cnt0, conv_p, gla_p = _mixer0_prompt_call(x_prompt, mod_p[0], mod_p[1], p)
    *tok0s, cnt0, gla_s, conv_s = _mixer0_sample_call(xs0, mod_s[0], mod_s[1], state_gla,
                                                      state_conv, cnt0, p)
    x2p, x2s = moe(tok0p, tok0s, cnt0, 0)
    *tok1p, cnt1, pool_p = _mixer1_prompt_call(x2p, mod_p[2], mod_p[3], p)
    *tok1s, cnt1, pool_s = _mixer1_sample_call(x2s, mod_s[2], mod_s[3], state_pool, cnt1, p)
    x4p, x4s = moe(tok1p, tok1s, cnt1, 1)
    return (x4p, x4s.reshape(nsmp, 1, D_MODEL), gla_p, conv_p, pool_p, gla_s, conv_s, pool_s)
```

```python
import jax
import jax.numpy as jnp
from jax import lax
from jax.experimental import pallas as pl
from jax.experimental.pallas import tpu as pltpu

F32 = jnp.float32
BF16 = jnp.bfloat16
I32 = jnp.int32

D_MODEL = 1024
GLA_HEADS = 4
GLA_DK = 64
GLA_DV = 128
GLA_KW = GLA_HEADS * GLA_DK
GLA_VW = GLA_HEADS * GLA_DV
GLA_RANK = 16
GLA_TAU = 16.0
CONV_CH = 512
CONV_WIDTH = 31
CONV_BUF = CONV_WIDTH - 1
POOL_WINDOWS = (2, 4, 8, 16)
POOL_GC = D_MODEL // len(POOL_WINDOWS)
POOL_BUF = max(POOL_WINDOWS) - 1
MOE_GROUPS = 4
MOE_EXPERTS = 4
N_EXPERTS = MOE_GROUPS * MOE_EXPERTS
N_PAIRS = MOE_EXPERTS * (MOE_EXPERTS - 1) // 2
N_BUCKETS = MOE_GROUPS * N_PAIRS
PAIR_A = (0, 0, 0, 1, 1, 2)
PAIR_B = (1, 2, 3, 2, 3, 3)
MOE_FF = 512
DEPTH = 2
ALPHA = (2 * DEPTH) ** 0.25
LN_EPS = 1e-5

LANES = 128
SUBLANES = 8
CHUNKS = D_MODEL // LANES
A_PAD = LANES
Z_WIDTH = 2 * GLA_KW + 2 * GLA_VW + 2 * CONV_CH + A_PAD
NEG_BIG = -1e30
VMEM_LIMIT = 56 * 1024 * 1024

TILE_L = 256
CONV_HALO = 32
CONV_W_ROWS = 32
POOL_HALO = 16
SAMPLE_BLK = 16
MOE_TILE = 256
ROW_TILE = 128
DISPATCH_ROWS = 512


def _dot(a, b):
    return jnp.dot(a, b, preferred_element_type=F32)


def _dot_nt(a, b):
    return lax.dot_general(a, b, (((1,), (1,)), ((), ())), preferred_element_type=F32)


def _split3(x):
    hi = x.astype(BF16)
    r1 = x - hi.astype(F32)
    mid = r1.astype(BF16)
    lo = (r1 - mid.astype(F32)).astype(BF16)
    return hi, mid, lo


def _split2(x):
    hi = x.astype(BF16)
    lo = (x - hi.astype(F32)).astype(BF16)
    return hi, lo


def _dot_w3(a, w_hi, w_lo):
    a_hi, a_lo = _split2(a)
    return _dot(a_hi, w_hi) + (_dot(a_lo, w_hi) + _dot(a_hi, w_lo))


def _lhs3(parts):
    hi, lo = parts
    return jnp.concatenate([hi, lo, hi], axis=1)


def _rhs3_rows(x):
    hi, lo = _split2(x)
    return jnp.concatenate([hi, hi, lo], axis=0)


def _rhs3_lanes(parts):
    hi, lo = parts
    return jnp.concatenate([hi, hi, lo], axis=1)


def _layer_norm(x, g, b):
    mu = jnp.mean(x, axis=-1, keepdims=True)
    xc = x - mu
    var = jnp.mean(xc * xc, axis=-1, keepdims=True)
    return xc * lax.rsqrt(var + LN_EPS) * g + b


def _standardize(x):
    mu = jnp.mean(x, axis=-1, keepdims=True)
    xc = x - mu
    var = jnp.mean(xc * xc, axis=-1, keepdims=True)
    return xc * lax.rsqrt(var + LN_EPS)


def _sigmoid(x):
    return 1.0 / (1.0 + jnp.exp(-x))


def _silu(x):
    return x * _sigmoid(x)


def _log_sigmoid(x):
    return jnp.minimum(x, 0.0) - jnp.log(1.0 + jnp.exp(-jnp.abs(x)))


def _mod3(mod):
    return mod[:, 0:D_MODEL], mod[:, D_MODEL:2 * D_MODEL], mod[:, 2 * D_MODEL:3 * D_MODEL]


def _full(shape):
    nd = len(shape)
    return pl.BlockSpec(shape, lambda *_: (0,) * nd)


def _const(shape):
    nd = len(shape)
    return pl.BlockSpec(shape, lambda *_: (0,) * nd, pipeline_mode=pl.Buffered(1))


def _hi_lo(w):
    bits = lax.bitcast_convert_type(w, jnp.uint32) & jnp.uint32(0xFFFF0000)
    hi = lax.bitcast_convert_type(bits, F32)
    return jnp.stack([hi.astype(BF16), (w - hi).astype(BF16)])


def _params(*sem):
    return pltpu.CompilerParams(dimension_semantics=sem, vmem_limit_bytes=VMEM_LIMIT)


def _ada_kernel(c_ref, w_ref, b_ref, o_ref):
    w_hi, w_lo = _split2(w_ref[0])
    o_ref[0] = _dot_w3(_silu(c_ref[...]), w_hi, w_lo) + b_ref[0]


def _ada_call(c_all, w_ada, b_ada):
    n = c_all.shape[0]
    ncomb = w_ada.shape[0]
    tn = D_MODEL
    return pl.pallas_call(
        _ada_kernel,
        grid=(ncomb, 3 * D_MODEL // tn),
        in_specs=[
            pl.BlockSpec((n, D_MODEL), lambda i, j: (0, 0)),
            pl.BlockSpec((1, D_MODEL, tn), lambda i, j: (i, 0, j)),
            pl.BlockSpec((1, 1, tn), lambda i, j: (i, 0, j)),
        ],
        out_specs=pl.BlockSpec((1, n, tn), lambda i, j: (i, 0, j)),
        out_shape=jax.ShapeDtypeStruct((ncomb, n, 3 * D_MODEL), F32),
        compiler_params=_params("arbitrary", "arbitrary"),
        name="ada",
    )(c_all, w_ada, b_ada)


def _route(logits, carry):
    rows = logits.shape[0]
    lane = lax.broadcasted_iota(I32, (rows, LANES), 1)
    lanef = lane.astype(F32)
    big = float(LANES)
    lc = jnp.where(lane < MOE_GROUPS, logits, NEG_BIG)
    mc = jnp.max(lc, axis=-1, keepdims=True)
    pg = 1.0 / jnp.sum(jnp.exp(lc - mc), axis=-1, keepdims=True)
    gidx = jnp.min(jnp.where(lc == mc, lanef, big), axis=-1, keepdims=True)
    lo = float(MOE_GROUPS) + gidx * float(MOE_EXPERTS)
    in_grp = (lanef >= lo) & (lanef < lo + float(MOE_EXPERTS))
    lf = jnp.where(in_grp, logits, NEG_BIG)
    m1 = jnp.max(lf, axis=-1, keepdims=True)
    i1 = jnp.min(jnp.where(lf == m1, lanef, big), axis=-1, keepdims=True)
    lf2 = jnp.where(lanef == i1, NEG_BIG, lf)
    m2 = jnp.max(lf2, axis=-1, keepdims=True)
    i2 = jnp.min(jnp.where(lf2 == m2, lanef, big), axis=-1, keepdims=True)
    e2 = jnp.exp(m2 - m1)
    w1 = pg / (1.0 + e2)
    w2 = pg * e2 / (1.0 + e2)
    ea = jnp.minimum(i1, i2) - lo
    eb = jnp.maximum(i1, i2) - lo
    w_a = jnp.where(i1 < i2, w1, w2)
    w_b = jnp.where(i1 < i2, w2, w1)
    bucket = gidx * float(N_PAIRS) + ea * (7.0 - ea) * 0.5 + eb - ea - 1.0
    onehot = jnp.where(lanef == bucket, 1.0, 0.0)
    r = lax.broadcasted_iota(I32, (rows, rows), 0)
    c = lax.broadcasted_iota(I32, (rows, rows), 1)
    before = _dot(jnp.where(r > c, 1.0, 0.0).astype(BF16), onehot.astype(BF16))
    rank = jnp.sum(onehot * (before + carry), axis=-1, keepdims=True)
    new_carry = carry + jnp.sum(onehot, axis=0, keepdims=True)
    route = (jnp.where(lane == 0, w_a, 0.0) + jnp.where(lane == 1, w_b, 0.0)
             + jnp.where(lane == 2, bucket, 0.0) + jnp.where(lane == 3, rank, 0.0))
    return route, new_carry


def _store_chunked(ref, x):
    rows = x.shape[0]
    for c in range(CHUNKS):
        ref[pl.ds(c, rows, stride=CHUNKS), :] = x[:, c * LANES:(c + 1) * LANES]


def _load_chunked(ref, rows, group, first):
    return jnp.concatenate(
        [ref[pl.ds(first + c, rows, stride=group), :] for c in range(CHUNKS)], axis=1)


def _post_mixer(x, m, gate, lng_ref, lnb_ref, mod_b, wr_hi_ref, wr_lo_ref, br_ref, carry,
                x1_ref, h2c_ref, route_ref):
    x1 = _layer_norm(ALPHA * x + (1.0 + gate) * m, lng_ref[...], lnb_ref[...])
    shift2, scale2, _ = _mod3(mod_b)
    h2 = x1 * (1.0 + scale2) + shift2
    h_hi, h_lo = _split2(h2)
    wr_hi = wr_hi_ref[...]
    logits = _dot(h_hi, wr_hi) + _dot(h_lo, wr_hi) + _dot(h_hi, wr_lo_ref[...]) + br_ref[...]
    route, carry = _route(logits, carry)
    x1_ref[...] = x1
    _store_chunked(h2c_ref, h2)
    route_ref[...] = route
    return carry


def _token_out_specs(rows, index_map):
    return [pl.BlockSpec((rows, D_MODEL), index_map),
            pl.BlockSpec((rows * CHUNKS, LANES), index_map),
            pl.BlockSpec((rows, LANES), index_map)]


def _token_out_shapes(ntok):
    return [jax.ShapeDtypeStruct((ntok, D_MODEL), F32),
            jax.ShapeDtypeStruct((ntok * CHUNKS, LANES), F32),
            jax.ShapeDtypeStruct((ntok, LANES), F32)]


def _hold_rows(b, s):
    n, w = b.shape
    if s >= SUBLANES:
        pieces = []
        for p in range(n // (2 * s)):
            r = p * 2 * s + s - 1
            pieces.append(jnp.broadcast_to(b[r:r + 1, :], (2 * s, w)))
        return pieces[0] if len(pieces) == 1 else jnp.concatenate(pieces, axis=0)
    b3 = b.reshape(n // SUBLANES, SUBLANES, w)
    sub = lax.broadcasted_iota(I32, b3.shape, 1)

    def bc(r):
        return jnp.broadcast_to(b3[:, r:r + 1, :], b3.shape)

    out = bc(s - 1)
    for p in range(1, SUBLANES // (2 * s)):
        out = jnp.where(sub >= p * 2 * s, bc(p * 2 * s + s - 1), out)
    return out.reshape(n, w)


def _gla_tile(q, k, v, la, s_prev):
    n = q.shape[0]
    row = lax.broadcasted_iota(I32, (n, n), 0)
    col = lax.broadcasted_iota(I32, (n, n), 1)
    tri = jnp.where(row >= col, 1.0, 0.0).astype(BF16)
    hi, mid, lo = _split3(la)
    b = _dot(tri, hi) + _dot(tri, mid) + _dot(tri, lo)
    b_last = b[n - 1:n, :]
    kl = k * jnp.exp(b_last - b)

    rowi = lax.broadcasted_iota(I32, (n, GLA_KW), 0)
    levels = []
    s = n // 2
    while s >= 1:
        m = _hold_rows(b, s)
        second = (rowi & s) != 0
        qs = _split2(q * jnp.exp(jnp.where(second, b - m, NEG_BIG)))
        ks = _split2(k * jnp.exp(jnp.where(second, NEG_BIG, m - b)))
        levels.append((2 * s, qs, ks))
        s //= 2
    levels.append((1, _split2(q), _split2(k)))
    qe = _split2(q * jnp.exp(b))

    lane = lax.broadcasted_iota(I32, (n, LANES), 1)
    xor = row ^ col
    zero = jnp.zeros((), BF16)

    def head_part(parts, sl, hm):
        return tuple(jnp.where(hm, x[:, sl], zero) for x in parts)

    outs = []
    for h in range(GLA_HEADS):
        p = h // 2
        sl = slice(p * LANES, (p + 1) * LANES)
        hm = (lane < GLA_DK) if h % 2 == 0 else (lane >= GLA_DK)
        att = None
        for span, qs, ks in levels:
            a_s = _dot_nt(_lhs3(head_part(qs, sl, hm)), _rhs3_lanes((ks[0][:, sl], ks[1][:, sl])))
            att = a_s if att is None else jnp.where(xor < span, a_s, att)
        o_h = _dot(_lhs3(_split2(att)), _rhs3_rows(v[:, h * GLA_DV:(h + 1) * GLA_DV]))
        o_h = o_h + _dot(_lhs3(head_part(qe, sl, hm)), _rhs3_rows(s_prev[sl, :]))
        outs.append(o_h)

    decay = jnp.exp(b_last)
    upper = lax.broadcasted_iota(I32, (LANES, LANES), 0) < GLA_DK
    s_new = []
    for p in range(GLA_HEADS // 2):
        sl = slice(p * LANES, (p + 1) * LANES)
        kl_t = jnp.transpose(kl[:, sl])
        u = _dot(_lhs3(_split2(kl_t)), _rhs3_rows(v[:, p * 2 * GLA_DV:(p + 1) * 2 * GLA_DV]))
        upd = jnp.where(upper, u[:, 0:GLA_DV], u[:, GLA_DV:2 * GLA_DV])
        dcol = jnp.transpose(jnp.broadcast_to(decay[:, sl], (LANES, LANES)))
        s_new.append(dcol * s_prev[sl, :] + upd)
    return outs, jnp.concatenate(s_new, axis=0)


def _split_projection(z):
    c0 = 0
    q = z[:, c0:c0 + GLA_KW] * (GLA_DK ** -0.5); c0 += GLA_KW
    k = z[:, c0:c0 + GLA_KW]; c0 += GLA_KW
    v = z[:, c0:c0 + GLA_VW]; c0 += GLA_VW
    g = z[:, c0:c0 + GLA_VW]; c0 += GLA_VW
    ua = z[:, c0:c0 + CONV_CH]; c0 += CONV_CH
    ug = z[:, c0:c0 + CONV_CH]; c0 += CONV_CH
    a_lr = z[:, c0:c0 + A_PAD]
    return q, k, v, g, ua, ug, a_lr


def _mix0_project(o_heads, g, y, gng_ref, w_out_ref):
    sl = slice(GLA_VW, GLA_VW + CONV_CH)
    m = _dot_w3(y, w_out_ref[0, sl, :], w_out_ref[1, sl, :])
    for hd in range(GLA_HEADS):
        sl = slice(hd * GLA_DV, (hd + 1) * GLA_DV)
        o_h = _standardize(o_heads[hd]) * gng_ref[:, sl] * _silu(g[:, sl])
        m = m + _dot_w3(o_h, w_out_ref[0, sl, :], w_out_ref[1, sl, :])
    return m


def _mix0_inputs(h, w_in_ref, w_a2_ref, b_a_ref):
    z = _dot_w3(h, w_in_ref[0], w_in_ref[1])
    q, k, v, g, ua, ug, a_lr = _split_projection(z)
    la = _log_sigmoid(_dot_w3(a_lr, w_a2_ref[0], w_a2_ref[1]) + b_a_ref[...]) * (1.0 / GLA_TAU)
    return q, k, v, g, ua, ug, la


def _mixer0_prompt_kernel(x_ref, moda_ref, modb_ref, w_in_ref, w_a2_ref, b_a_ref, gng_ref,
                          cw_ref, cb_ref, clg_ref, clb_ref, w_out_ref, lng_ref, lnb_ref,
                          wr_hi_ref, wr_lo_ref, br_ref,
                          x1_ref, h2c_ref, route_ref, cnt_ref, nconv_ref, ngla_ref,
                          s_ref, cbuf_ref, shift_ref, carry_ref):
    t = pl.program_id(1)
    n = x_ref.shape[1]

    @pl.when((t == 0) & (pl.program_id(0) == 0))
    def _():
        carry_ref[...] = jnp.zeros_like(carry_ref)

    @pl.when(t == 0)
    def _():
        s_ref[...] = jnp.zeros_like(s_ref)
        cbuf_ref[0:CONV_HALO, :] = jnp.zeros((CONV_HALO, CONV_CH), F32)

    x = x_ref[0]
    shift, scale, gate = _mod3(moda_ref[0])
    h = x * (1.0 + scale) + shift
    q, k, v, g, ua, ug, la = _mix0_inputs(h, w_in_ref, w_a2_ref, b_a_ref)
    o_heads, s_new = _gla_tile(q, k, v, la, s_ref[...])
    s_ref[...] = s_new

    glu = ua * _sigmoid(ug)
    cbuf_ref[CONV_HALO:CONV_HALO + n, :] = glu
    acc = jnp.broadcast_to(cb_ref[...], (n, CONV_CH))
    base = CONV_HALO - CONV_BUF
    for r in range(SUBLANES):
        taps = range(r, CONV_WIDTH, SUBLANES)
        span = n + (len(taps) - 1) * SUBLANES
        if (base + r) % SUBLANES == 0:
            src, off = cbuf_ref, base + r
        else:
            shift_ref[r, 0:span, :] = cbuf_ref[base + r:base + r + span, :]
            src, off = shift_ref.at[r], 0
        for a, j in enumerate(taps):
            lo = off + a * SUBLANES
            acc = acc + cw_ref[j:j + 1, :] * src[lo:lo + n, :]
    y = _silu(_layer_norm(acc, clg_ref[...], clb_ref[...]))

    @pl.when(t == pl.num_programs(1) - 1)
    def _():
        nconv_ref[0, 0] = cbuf_ref[CONV_HALO + n - CONV_BUF:CONV_HALO + n, :]
        ngla_ref[0, 0] = s_new.reshape(GLA_HEADS, GLA_DK, GLA_DV)

    cbuf_ref[0:CONV_HALO, :] = cbuf_ref[n:n + CONV_HALO, :]

    m = _mix0_project(o_heads, g, y, gng_ref, w_out_ref)
    carry = _post_mixer(x, m, gate, lng_ref, lnb_ref, modb_ref[0], wr_hi_ref, wr_lo_ref, br_ref,
                        carry_ref[...], x1_ref, h2c_ref, route_ref)
    carry_ref[...] = carry
    cnt_ref[...] = carry


def _mixer0_prompt_call(x, mod_a, mod_b, p):
    bsz, seq, _ = x.shape
    tl = TILE_L
    nt = seq // tl
    row3 = lambda b, t: (b, 0, 0)
    tok = lambda b, t: (b * nt + t, 0)
    return pl.pallas_call(
        _mixer0_prompt_kernel,
        grid=(bsz, nt),
        in_specs=[
            pl.BlockSpec((1, tl, D_MODEL), lambda b, t: (b, t, 0)),
            pl.BlockSpec((1, 1, 3 * D_MODEL), row3),
            pl.BlockSpec((1, 1, 3 * D_MODEL), row3),
            _const((2, D_MODEL, Z_WIDTH)),
            _const((2, A_PAD, GLA_KW)),
            _full((1, GLA_KW)),
            _full((1, GLA_VW)),
            _full((CONV_W_ROWS, CONV_CH)),
            _full((1, CONV_CH)),
            _full((1, CONV_CH)),
            _full((1, CONV_CH)),
            _const((2, GLA_VW + CONV_CH, D_MODEL)),
            _full((1, D_MODEL)),
            _full((1, D_MODEL)),
            _full((D_MODEL, LANES)),
            _full((D_MODEL, LANES)),
            _full((1, LANES)),
        ],
        out_specs=_token_out_specs(tl, tok) + [
            _full((1, LANES)),
            pl.BlockSpec((1, 1, CONV_BUF, CONV_CH), lambda b, t: (0, b, 0, 0)),
            pl.BlockSpec((1, 1, GLA_HEADS, GLA_DK, GLA_DV), lambda b, t: (0, b, 0, 0, 0)),
        ],
        out_shape=_token_out_shapes(bsz * seq) + [
            jax.ShapeDtypeStruct((1, LANES), F32),
            jax.ShapeDtypeStruct((1, bsz, CONV_BUF, CONV_CH), F32),
            jax.ShapeDtypeStruct((1, bsz, GLA_HEADS, GLA_DK, GLA_DV), F32),
        ],
        scratch_shapes=[
            pltpu.VMEM((GLA_KW, GLA_DV), F32),
            pltpu.VMEM((CONV_HALO + tl, CONV_CH), F32),
            pltpu.VMEM((SUBLANES, CONV_HALO + tl, CONV_CH), F32),
            pltpu.VMEM((1, LANES), F32),
        ],
        compiler_params=_params("arbitrary", "arbitrary"),
        name="mixer0_prompt",
    )(x, mod_a, mod_b, p["w_in"], p["w_a2"], p["b_a"], p["gng"], p["conv_w"], p["conv_b"],
      p["conv_ln_g"], p["conv_ln_b"], p["w_out0"], p["ln_g00"], p["ln_b00"],
      p["wr_hi0"], p["wr_lo0"], p["br0"])


def _mixer0_sample_kernel(x_ref, moda_ref, modb_ref, sgla_ref, sconv_ref, cnt_in_ref,
                          w_in_ref, w_a2_ref,
                          b_a_ref, gng_ref, cw_ref, cb_ref, clg_ref, clb_ref, w_out_ref, lng_ref,
                          lnb_ref, wr_hi_ref, wr_lo_ref, br_ref,
                          x1_ref, h2c_ref, route_ref, cnt_ref, ngla_ref, nconv_ref,
                          zt_ref, v_ref, g_ref, glu_ref, o_ref, y_ref):
    i = pl.program_id(0)
    nb = sgla_ref.shape[1]
    ntok = x_ref.shape[0]

    @pl.when(i == 0)
    def _():
        shift, scale, _ = _mod3(moda_ref[...])
        h = x_ref[...] * (1.0 + scale) + shift
        q, k, v, g, ua, ug, la = _mix0_inputs(h, w_in_ref, w_a2_ref, b_a_ref)
        v_ref[...] = v
        g_ref[...] = g
        glu_ref[...] = ua * _sigmoid(ug)
        for j, val in enumerate((jnp.exp(la), k, q)):
            hi, lo = _split2(jnp.transpose(val))
            zt_ref[(2 * j) * GLA_KW:(2 * j + 1) * GLA_KW, :] = hi
            zt_ref[(2 * j + 1) * GLA_KW:(2 * j + 2) * GLA_KW, :] = lo

    tok_row = lax.broadcasted_iota(I32, (ntok, LANES), 0)
    blk = pl.ds(pl.multiple_of(i * nb, nb), nb)
    v_blk = v_ref[blk, :]
    glu_blk = glu_ref[blk, :]
    o_rows, y_rows = [], []
    for n in range(nb):
        onehot = jnp.where(tok_row == i * nb + n, 1.0, 0.0).astype(BF16)
        cols = _dot(zt_ref[...], onehot)
        a_col = cols[0:GLA_KW] + cols[GLA_KW:2 * GLA_KW]
        k_col = cols[2 * GLA_KW:3 * GLA_KW] + cols[3 * GLA_KW:4 * GLA_KW]
        q_col = cols[4 * GLA_KW:5 * GLA_KW] + cols[5 * GLA_KW:6 * GLA_KW]
        v_row = v_blk[n:n + 1, :]
        v_b = jnp.concatenate(
            [jnp.broadcast_to(v_row[:, h * GLA_DV:(h + 1) * GLA_DV], (GLA_DK, GLA_DV))
             for h in range(GLA_HEADS)], axis=0)
        s_old = sgla_ref[0, n].reshape(GLA_KW, GLA_DV)
        s_new = a_col * s_old + k_col * v_b
        ngla_ref[0, n] = s_new.reshape(GLA_HEADS, GLA_DK, GLA_DV)
        o4 = jnp.sum((q_col * s_new).reshape(GLA_HEADS, GLA_DK, GLA_DV), axis=1)
        o_rows.append(jnp.concatenate([o4[h:h + 1, :] for h in range(GLA_HEADS)], axis=1))
        glu_row = glu_blk[n:n + 1, :]
        past = sconv_ref[0, n]
        y_rows.append(jnp.sum(past * cw_ref[0:CONV_BUF, :], axis=0, keepdims=True)
                      + glu_row * cw_ref[CONV_BUF:CONV_WIDTH, :] + cb_ref[...])
        nconv_ref[0, n, 0:CONV_BUF - 1, :] = sconv_ref[0, n, 1:CONV_BUF, :]
        nconv_ref[0, n, CONV_BUF - 1:CONV_BUF, :] = glu_row
    o_ref[blk, :] = jnp.concatenate(o_rows, axis=0)
    y_ref[blk, :] = jnp.concatenate(y_rows, axis=0)

    @pl.when(i == pl.num_programs(0) - 1)
    def _():
        y = _silu(_layer_norm(y_ref[...], clg_ref[...], clb_ref[...]))
        o_heads = [o_ref[:, hd * GLA_DV:(hd + 1) * GLA_DV] for hd in range(GLA_HEADS)]
        m = _mix0_project(o_heads, g_ref[...], y, gng_ref, w_out_ref)
        _, _, gate = _mod3(moda_ref[...])
        cnt_ref[...] = _post_mixer(x_ref[...], m, gate, lng_ref, lnb_ref, modb_ref[...], wr_hi_ref,
                                   wr_lo_ref, br_ref, cnt_in_ref[...], x1_ref, h2c_ref, route_ref)


def _mixer0_sample_call(x, mod_a, mod_b, state_gla, state_conv, cnt, p):
    ntok = x.shape[0]
    nb = SAMPLE_BLK
    tok = lambda i: (0, 0)
    return pl.pallas_call(
        _mixer0_sample_kernel,
        grid=(ntok // nb,),
        in_specs=[
            _full((ntok, D_MODEL)),
            _full((ntok, 3 * D_MODEL)),
            _full((ntok, 3 * D_MODEL)),
            pl.BlockSpec((1, nb, GLA_HEADS, GLA_DK, GLA_DV), lambda i: (0, i, 0, 0, 0)),
            pl.BlockSpec((1, nb, CONV_BUF, CONV_CH), lambda i: (0, i, 0, 0)),
            _full((1, LANES)),
            _const((2, D_MODEL, Z_WIDTH)),
            _const((2, A_PAD, GLA_KW)),
            _full((1, GLA_KW)),
            _full((1, GLA_VW)),
            _full((CONV_W_ROWS, CONV_CH)),
            _full((1, CONV_CH)),
            _full((1, CONV_CH)),
            _full((1, CONV_CH)),
            _const((2, GLA_VW + CONV_CH, D_MODEL)),
            _full((1, D_MODEL)),
            _full((1, D_MODEL)),
            _full((D_MODEL, LANES)),
            _full((D_MODEL, LANES)),
            _full((1, LANES)),
        ],
        out_specs=_token_out_specs(ntok, tok) + [
            _full((1, LANES)),
            pl.BlockSpec((1, nb, GLA_HEADS, GLA_DK, GLA_DV), lambda i: (0, i, 0, 0, 0)),
            pl.BlockSpec((1, nb, CONV_BUF, CONV_CH), lambda i: (0, i, 0, 0)),
        ],
        out_shape=_token_out_shapes(ntok) + [
            jax.ShapeDtypeStruct((1, LANES), F32),
            jax.ShapeDtypeStruct(state_gla.shape, F32),
            jax.ShapeDtypeStruct(state_conv.shape, F32),
        ],
        scratch_shapes=[
            pltpu.VMEM((6 * GLA_KW, ntok), BF16),
            pltpu.VMEM((ntok, GLA_VW), F32),
            pltpu.VMEM((ntok, GLA_VW), F32),
            pltpu.VMEM((ntok, CONV_CH), F32),
            pltpu.VMEM((ntok, GLA_VW), F32),
            pltpu.VMEM((ntok, CONV_CH), F32),
        ],
        compiler_params=_params("arbitrary"),
        name="mixer0_sample",
    )(x, mod_a, mod_b, state_gla, state_conv, cnt, p["w_in"], p["w_a2"], p["b_a"],
      p["gng"], p["conv_w"], p["conv_b"], p["conv_ln_g"], p["conv_ln_b"], p["w_out0"],
      p["ln_g00"], p["ln_b00"], p["wr_hi0"], p["wr_lo0"], p["br0"])


def _pool_project(pooled, h, wg_ref, ps_ref, w_out_ref):
    m = None
    for gi in range(len(POOL_WINDOWS)):
        sl = slice(gi * POOL_GC, (gi + 1) * POOL_GC)
        mixed = _dot((pooled[:, sl] - h[:, sl]).astype(BF16), wg_ref[gi]) * ps_ref[:, sl]
        part = _dot(mixed.astype(BF16), w_out_ref[sl, :])
        m = part if m is None else m + part
    return m


def _mixer1_prompt_kernel(x_ref, moda_ref, modb_ref, wg_ref, ps_ref, w_out_ref, lng_ref, lnb_ref,
                          wr_hi_ref, wr_lo_ref, br_ref,
                          x1_ref, h2c_ref, route_ref, cnt_ref, npool_ref,
                          pbuf_ref, carry_ref):
    t = pl.program_id(1)
    n = x_ref.shape[1]

    @pl.when((t == 0) & (pl.program_id(0) == 0))
    def _():
        carry_ref[...] = jnp.zeros_like(carry_ref)

    @pl.when(t == 0)
    def _():
        pbuf_ref[0:POOL_HALO, :] = jnp.zeros((POOL_HALO, D_MODEL), F32)

    x = x_ref[0]
    shift, scale, gate = _mod3(moda_ref[0])
    h = x * (1.0 + scale) + shift
    pbuf_ref[POOL_HALO:POOL_HALO + n, :] = h

    cur = pbuf_ref[...]
    sums = []
    for gi, w in enumerate(POOL_WINDOWS):
        cur = cur + pltpu.roll(cur, w // 2, axis=0)
        sums.append(cur[POOL_HALO:, 0:POOL_GC])
        if gi + 1 < len(POOL_WINDOWS):
            cur = cur[:, POOL_GC:]
    pos = lax.broadcasted_iota(I32, (n, POOL_GC), 0) + t * n
    pooled = jnp.concatenate(
        [s / jnp.minimum(w, pos + 1).astype(F32) for s, w in zip(sums, POOL_WINDOWS)], axis=1)

    @pl.when(t == pl.num_programs(1) - 1)
    def _():
        npool_ref[0, 0] = pbuf_ref[POOL_HALO + n - POOL_BUF:POOL_HALO + n, :]

    pbuf_ref[0:POOL_HALO, :] = pbuf_ref[n:n + POOL_HALO, :]

    m = _pool_project(pooled, h, wg_ref, ps_ref, w_out_ref)
    carry = _post_mixer(x, m, gate, lng_ref, lnb_ref, modb_ref[0], wr_hi_ref, wr_lo_ref, br_ref,
                        carry_ref[...], x1_ref, h2c_ref, route_ref)
    carry_ref[...] = carry
    cnt_ref[...] = carry


def _mixer1_prompt_call(x, mod_a, mod_b, p):
    bsz, seq, _ = x.shape
    tl = TILE_L
    nt = seq // tl
    row3 = lambda b, t: (b, 0, 0)
    tok = lambda b, t: (b * nt + t, 0)
    ng = len(POOL_WINDOWS)
    return pl.pallas_call(
        _mixer1_prompt_kernel,
        grid=(bsz, nt),
        in_specs=[
            pl.BlockSpec((1, tl, D_MODEL), lambda b, t: (b, t, 0)),
            pl.BlockSpec((1, 1, 3 * D_MODEL), row3),
            pl.BlockSpec((1, 1, 3 * D_MODEL), row3),
            _full((ng, POOL_GC, POOL_GC)),
            _full((1, D_MODEL)),
            _full((D_MODEL, D_MODEL)),
            _full((1, D_MODEL)),
            _full((1, D_MODEL)),
            _full((D_MODEL, LANES)),
            _full((D_MODEL, LANES)),
            _full((1, LANES)),
        ],
        out_specs=_token_out_specs(tl, tok) + [
            _full((1, LANES)),
            pl.BlockSpec((1, 1, POOL_BUF, D_MODEL), lambda b, t: (0, b, 0, 0)),
        ],
        out_shape=_token_out_shapes(bsz * seq) + [
            jax.ShapeDtypeStruct((1, LANES), F32),
            jax.ShapeDtypeStruct((1, bsz, POOL_BUF, D_MODEL), F32),
        ],
        scratch_shapes=[pltpu.VMEM((POOL_HALO + tl, D_MODEL), F32), pltpu.VMEM((1, LANES), F32)],
        compiler_params=_params("arbitrary", "arbitrary"),
        name="mixer1_prompt",
    )(x, mod_a, mod_b, p["w_grp"], p["pool_scale"], p["w_out1"], p["ln_g10"], p["ln_b10"],
      p["wr_hi1"], p["wr_lo1"], p["br1"])


def _mixer1_sample_kernel(x_ref, moda_ref, modb_ref, spool_ref, cnt_in_ref,
                          wg_ref, ps_ref, w_out_ref,
                          lng_ref, lnb_ref, wr_hi_ref, wr_lo_ref, br_ref,
                          x1_ref, h2c_ref, route_ref, cnt_ref, npool_ref, h_ref, pooled_ref):
    i = pl.program_id(0)
    nb = spool_ref.shape[1]

    @pl.when(i == 0)
    def _():
        shift, scale, _ = _mod3(moda_ref[...])
        h_ref[...] = x_ref[...] * (1.0 + scale) + shift

    lane = lax.broadcasted_iota(I32, (POOL_BUF, D_MODEL), 1)
    rowi = lax.broadcasted_iota(I32, (POOL_BUF, D_MODEL), 0)
    first = jnp.zeros((POOL_BUF, D_MODEL), I32)
    lane1 = lax.broadcasted_iota(I32, (1, D_MODEL), 1)
    inv_w = jnp.zeros((1, D_MODEL), F32)
    for gi, w in enumerate(POOL_WINDOWS):
        in_g = (lane >= gi * POOL_GC) & (lane < (gi + 1) * POOL_GC)
        first = jnp.where(in_g, POOL_BUF - (w - 1), first)
        in_g1 = (lane1 >= gi * POOL_GC) & (lane1 < (gi + 1) * POOL_GC)
        inv_w = jnp.where(in_g1, 1.0 / w, inv_w)
    keep = rowi >= first

    blk = pl.ds(pl.multiple_of(i * nb, nb), nb)
    h_blk = h_ref[blk, :]
    rows = []
    for n in range(nb):
        h_row = h_blk[n:n + 1, :]
        past = spool_ref[0, n]
        total = jnp.sum(jnp.where(keep, past, 0.0), axis=0, keepdims=True) + h_row
        rows.append(total * inv_w)
        npool_ref[0, n, 0:POOL_BUF - 1, :] = spool_ref[0, n, 1:POOL_BUF, :]
        npool_ref[0, n, POOL_BUF - 1:POOL_BUF, :] = h_row
    pooled_ref[blk, :] = jnp.concatenate(rows, axis=0)

    @pl.when(i == pl.num_programs(0) - 1)
    def _():
        m = _pool_project(pooled_ref[...], h_ref[...], wg_ref, ps_ref, w_out_ref)
        _, _, gate = _mod3(moda_ref[...])
        cnt_ref[...] = _post_mixer(x_ref[...], m, gate, lng_ref, lnb_ref, modb_ref[...], wr_hi_ref,
                                   wr_lo_ref, br_ref, cnt_in_ref[...], x1_ref, h2c_ref, route_ref)


def _mixer1_sample_call(x, mod_a, mod_b, state_pool, cnt, p):
    ntok = x.shape[0]
    nb = SAMPLE_BLK
    ng = len(POOL_WINDOWS)
    tok = lambda i: (0, 0)
    return pl.pallas_call(
        _mixer1_sample_kernel,
        grid=(ntok // nb,),
        in_specs=[
            _full((ntok, D_MODEL)),
            _full((ntok, 3 * D_MODEL)),
            _full((ntok, 3 * D_MODEL)),
            pl.BlockSpec((1, nb, POOL_BUF, D_MODEL), lambda i: (0, i, 0, 0)),
            _full((1, LANES)),
            _full((ng, POOL_GC, POOL_GC)),
            _full((1, D_MODEL)),
            _full((D_MODEL, D_MODEL)),
            _full((1, D_MODEL)),
            _full((1, D_MODEL)),
            _full((D_MODEL, LANES)),
            _full((D_MODEL, LANES)),
            _full((1, LANES)),
        ],
        out_specs=_token_out_specs(ntok, tok) + [
            _full((1, LANES)),
            pl.BlockSpec((1, nb, POOL_BUF, D_MODEL), lambda i: (0, i, 0, 0)),
        ],
        out_shape=_token_out_shapes(ntok) + [
            jax.ShapeDtypeStruct((1, LANES), F32),
            jax.ShapeDtypeStruct(state_pool.shape, F32),
        ],
        scratch_shapes=[pltpu.VMEM((ntok, D_MODEL), F32), pltpu.VMEM((ntok, D_MODEL), F32)],
        compiler_params=_params("arbitrary"),
        name="mixer1_sample",
    )(x, mod_a, mod_b, state_pool, cnt, p["w_grp"], p["pool_scale"], p["w_out1"],
      p["ln_g10"], p["ln_b10"], p["wr_hi1"], p["wr_lo1"], p["br1"])


def _dispatch_kernel(dest_ref, src_ref, init_any, dst_any, sem):
    del init_any
    n = dest_ref.shape[2]
    for j in range(n):
        d = dest_ref[0, 0, j]
        pltpu.make_async_copy(
            src_ref.at[pl.ds(j * CHUNKS, CHUNKS)],
            dst_any.at[pl.ds(pl.multiple_of(d * CHUNKS, CHUNKS), CHUNKS)], sem).start()
    pltpu.make_async_copy(src_ref, dst_any.at[pl.ds(0, n * CHUNKS)], sem).wait()


def _dispatch_call(dest, h2c, sorted_buf, rows):
    ntok = dest.shape[0]
    return pl.pallas_call(
        _dispatch_kernel,
        grid=(ntok // rows,),
        in_specs=[
            pl.BlockSpec((1, 1, rows), lambda i: (i, 0, 0), memory_space=pltpu.SMEM),
            pl.BlockSpec((rows * CHUNKS, LANES), lambda i: (i, 0)),
            pl.BlockSpec(memory_space=pl.ANY),
        ],
        out_specs=pl.BlockSpec(memory_space=pl.ANY),
        out_shape=jax.ShapeDtypeStruct(sorted_buf.shape, F32),
        scratch_shapes=[pltpu.SemaphoreType.DMA],
        input_output_aliases={2: 0},
        compiler_params=_params("arbitrary"),
        name="dispatch",
    )(dest.reshape(ntok // rows, 1, rows), h2c, sorted_buf)


def _moe_kernel(sched_ref, x_ref, wga_ref, wua_ref, wda_ref, wgb_ref, wub_ref, wdb_ref, y_ref,
                wgu_ref, wd_ref):
    i = pl.program_id(0)
    rows = x_ref.shape[0] // CHUNKS

    @pl.when(sched_ref[4, i] == 1)
    def _():
        for e, (wg, wu, wd) in enumerate(((wga_ref, wua_ref, wda_ref), (wgb_ref, wub_ref, wdb_ref))):
            wgu_ref[2 * e] = wg[0].astype(BF16)
            wgu_ref[2 * e + 1] = wu[0].astype(BF16)
            wd_ref[e] = wd[0].astype(BF16)

    @pl.when(sched_ref[3, i] == 1)
    def _():
        x = _load_chunked(x_ref, rows, CHUNKS, 0).astype(BF16)
        for e in range(2):
            hid = _silu(_dot(x, wgu_ref[2 * e])) * _dot(x, wgu_ref[2 * e + 1])
            _store_chunked(y_ref.at[e], _dot(hid.astype(BF16), wd_ref[e]))

    @pl.when(sched_ref[3, i] == 0)
    def _():
        y_ref[...] = jnp.zeros_like(y_ref)


def _moe_call(sched, sorted_x, wg, wu, wd):
    ntile = sched.shape[1]
    tm = MOE_TILE
    ea = lambda i, s: (s[1, i], 0, 0)
    eb = lambda i, s: (s[2, i], 0, 0)
    grid_spec = pltpu.PrefetchScalarGridSpec(
        num_scalar_prefetch=1,
        grid=(ntile,),
        in_specs=[
            pl.BlockSpec((tm * CHUNKS, LANES), lambda i, s: (s[0, i], 0)),
            pl.BlockSpec((1, D_MODEL, MOE_FF), ea),
            pl.BlockSpec((1, D_MODEL, MOE_FF), ea),
            pl.BlockSpec((1, MOE_FF, D_MODEL), ea),
            pl.BlockSpec((1, D_MODEL, MOE_FF), eb),
            pl.BlockSpec((1, D_MODEL, MOE_FF), eb),
            pl.BlockSpec((1, MOE_FF, D_MODEL), eb),
        ],
        out_specs=pl.BlockSpec((2, tm * CHUNKS, LANES), lambda i, s: (0, i, 0)),
        scratch_shapes=[pltpu.VMEM((4, D_MODEL, MOE_FF), BF16), pltpu.VMEM((2, MOE_FF, D_MODEL), BF16)],
    )
    return pl.pallas_call(
        _moe_kernel,
        grid_spec=grid_spec,
        out_shape=jax.ShapeDtypeStruct((2, ntile * tm * CHUNKS, LANES), F32),
        compiler_params=_params("arbitrary"),
        name="moe",
    )(sched, sorted_x, wg, wu, wd, wg, wu, wd)


def _combine_kernel(dest_ref, dnext_ref, x1_ref, mod_ref, route_ref, y_any, lng_ref, lnb_ref,
                    out_ref, buf_ref, sems):
    i = pl.program_id(0)
    nstep = pl.num_programs(0)
    n = x1_ref.shape[0]
    slot = i % 2

    def start_all(d_ref, s):
        for j in range(n):
            d = d_ref[0, 0, j]
            pltpu.make_async_copy(
                y_any.at[:, pl.ds(pl.multiple_of(d * CHUNKS, CHUNKS), CHUNKS)],
                buf_ref.at[s, :, pl.ds(j * CHUNKS, CHUNKS)], sems.at[s]).start()

    @pl.when(i == 0)
    def _():
        start_all(dest_ref, 0)

    @pl.when(i + 1 < nstep)
    def _():
        start_all(dnext_ref, 1 - slot)

    pltpu.make_async_copy(y_any.at[:, pl.ds(0, n * CHUNKS)], buf_ref.at[slot], sems.at[slot]).wait()

    ya = _load_chunked(buf_ref.at[slot, 0], n, CHUNKS, 0)
    yb = _load_chunked(buf_ref.at[slot, 1], n, CHUNKS, 0)
    lane = lax.broadcasted_iota(I32, route_ref.shape, 1)
    route = route_ref[...]
    w_a = jnp.sum(jnp.where(lane == 0, route, 0.0), axis=-1, keepdims=True)
    w_b = jnp.sum(jnp.where(lane == 1, route, 0.0), axis=-1, keepdims=True)
    _, _, gate = _mod3(mod_ref[0])
    out_ref[...] = _layer_norm(ALPHA * x1_ref[...] + (1.0 + gate) * (w_a * ya + w_b * yb),
                               lng_ref[...], lnb_ref[...])


def _combine_call(dest, x1, mod, route, y_sorted, lng, lnb, tiles_per_mod):
    n = ROW_TILE
    ntile = x1.shape[0] // n
    mrows = mod.shape[1]
    tok = lambda i: (i, 0)
    tile_off = 0
    dest3 = dest.reshape(ntile, 1, n)
    return pl.pallas_call(
        _combine_kernel,
        grid=(ntile,),
        in_specs=[
            pl.BlockSpec((1, 1, n), lambda i: (i + tile_off, 0, 0), memory_space=pltpu.SMEM),
            pl.BlockSpec((1, 1, n), lambda i: (jnp.minimum(i + 1, ntile - 1) + tile_off, 0, 0),
                         memory_space=pltpu.SMEM),
            pl.BlockSpec((n, D_MODEL), tok),
            pl.BlockSpec((1, mrows, 3 * D_MODEL), lambda i: (i // tiles_per_mod, 0, 0)),
            pl.BlockSpec((n, LANES), tok),
            pl.BlockSpec(memory_space=pl.ANY),
            _full((1, D_MODEL)),
            _full((1, D_MODEL)),
        ],
        out_specs=pl.BlockSpec((n, D_MODEL), lambda i: (i, 0)),
        out_shape=jax.ShapeDtypeStruct((ntile * n, D_MODEL), F32),
        scratch_shapes=[pltpu.VMEM((2, 2, n * CHUNKS, LANES), F32), pltpu.SemaphoreType.DMA((2,))],
        compiler_params=_params("arbitrary"),
        name="combine",
    )(dest3, dest3, x1, mod, route, y_sorted, lng, lnb)


def _moe_plan(route_p, route_s, cnt, layer):
    tm = MOE_TILE
    ntok_all = route_p.shape[0] + route_s.shape[0]
    ntile = -(-ntok_all // tm) + N_BUCKETS
    bucket = jnp.concatenate([route_p[:, 2], route_s[:, 2]]).astype(I32)
    rank = jnp.concatenate([route_p[:, 3], route_s[:, 3]]).astype(I32)
    counts = cnt[0, :N_BUCKETS].astype(I32)
    tiles_b = (counts + tm - 1) // tm
    tile_end = jnp.cumsum(tiles_b)
    row_start = (tile_end - tiles_b) * tm
    dest = row_start[bucket] + rank
    used = tile_end[-1]
    ti = jnp.minimum(jnp.arange(ntile, dtype=I32), used - 1)
    b_of = jnp.sum((ti[:, None] >= tile_end[None, :]).astype(I32), axis=1)
    grp, pair = b_of // N_PAIRS, b_of % N_PAIRS
    first_expert = layer * N_EXPERTS + grp * MOE_EXPERTS
    ea = first_expert + jnp.asarray(PAIR_A, I32)[pair]
    eb = first_expert + jnp.asarray(PAIR_B, I32)[pair]
    valid = (jnp.arange(ntile, dtype=I32) < used).astype(I32)
    first = valid * (ti == (tile_end - tiles_b)[b_of]).astype(I32)
    sched = jnp.stack([ti, ea, eb, valid, first])
    return dest, sched


def _router_weights(w_coarse, b_coarse, w_fine, b_fine):
    wf = jnp.transpose(w_fine, (1, 0, 2)).reshape(D_MODEL, N_EXPERTS)
    w = jnp.concatenate([w_coarse, wf], axis=1)
    w = jnp.pad(w, ((0, 0), (0, LANES - w.shape[1])))
    b = jnp.concatenate([b_coarse, b_fine.reshape(N_EXPERTS)])
    b = jnp.pad(b, (0, LANES - b.shape[0])).reshape(1, LANES)
    hi_lo = _hi_lo(w)
    return hi_lo[0], hi_lo[1], b


def _prep_params(ln_g, ln_b, w_in_even, w_a2, b_a, gla_norm_g, conv_w, conv_b, conv_ln_g,
                 conv_ln_b, w_out_even, w_grp_pool, pool_scale, w_out_odd, w_coarse, b_coarse,
                 w_fine, b_fine):
    p = {}
    w_in = w_in_even[0]
    o_q, o_k, o_v, o_g = 0, GLA_KW, 2 * GLA_KW, 2 * GLA_KW + GLA_VW
    o_a = o_g + GLA_VW
    o_u = o_a + GLA_RANK
    w_in_r = jnp.concatenate(
        [w_in[:, o_q:o_a], w_in[:, o_u:o_u + 2 * CONV_CH], w_in[:, o_a:o_u],
         jnp.zeros((D_MODEL, A_PAD - GLA_RANK), F32)], axis=1)
    p["w_in"] = _hi_lo(w_in_r)
    p["w_a2"] = _hi_lo(jnp.pad(w_a2[0], ((0, A_PAD - GLA_RANK), (0, 0))))
    p["b_a"] = b_a[0].reshape(1, GLA_KW)
    p["gng"] = gla_norm_g[0].reshape(1, GLA_VW)
    p["conv_w"] = jnp.pad(conv_w[0], ((0, CONV_W_ROWS - CONV_WIDTH), (0, 0)))
    p["conv_b"] = conv_b[0].reshape(1, CONV_CH)
    p["conv_ln_g"] = conv_ln_g[0].reshape(1, CONV_CH)
    p["conv_ln_b"] = conv_ln_b[0].reshape(1, CONV_CH)
    p["w_out0"] = _hi_lo(w_out_even[0])
    p["w_grp"] = w_grp_pool[0].astype(BF16)
    p["pool_scale"] = pool_scale[0].reshape(1, D_MODEL)
    p["w_out1"] = w_out_odd[0].astype(BF16)
    for layer in range(DEPTH):
        for j in range(2):
            p[f"ln_g{layer}{j}"] = ln_g[layer, j].reshape(1, D_MODEL)
            p[f"ln_b{layer}{j}"] = ln_b[layer, j].reshape(1, D_MODEL)
        hi, lo, b = _router_weights(w_coarse[layer], b_coarse[layer], w_fine[layer], b_fine[layer])
        p[f"wr_hi{layer}"], p[f"wr_lo{layer}"], p[f"br{layer}"] = hi, lo, b
    return p


def kernel(x_prompt, x_sample, state_gla, state_conv, state_pool, c_prompt, c_sample, w_ada, b_ada,
           ln_g, ln_b, w_in_even, w_a2, b_a, gla_norm_g, conv_w, conv_b, conv_ln_g, conv_ln_b,
           w_out_even, w_grp_pool, pool_scale, w_out_odd, w_coarse, b_coarse, w_fine, b_fine,
           w_gate, w_up, w_down):
    bsz, seq, _ = x_prompt.shape
    nsmp = x_sample.shape[0]
    ntok_p = bsz * seq
    ntok_all = ntok_p + nsmp
    assert seq % TILE_L == 0 and seq % ROW_TILE == 0 and nsmp == ROW_TILE and ntok_p % nsmp == 0
    p = _prep_params(ln_g, ln_b, w_in_even, w_a2, b_a, gla_norm_g, conv_w, conv_b, conv_ln_g,
                     conv_ln_b, w_out_even, w_grp_pool, pool_scale, w_out_odd, w_coarse, b_coarse,
                     w_fine, b_fine)
    wg = w_gate.reshape(DEPTH * N_EXPERTS, D_MODEL, MOE_FF)
    wu = w_up.reshape(DEPTH * N_EXPERTS, D_MODEL, MOE_FF)
    wd = w_down.reshape(DEPTH * N_EXPERTS, MOE_FF, D_MODEL)

    c_all = jnp.concatenate([c_prompt, c_sample], axis=0)
    mod = _ada_call(c_all, w_ada.reshape(2 * DEPTH, D_MODEL, 3 * D_MODEL),
                    b_ada.reshape(2 * DEPTH, 1, 3 * D_MODEL))
    mod_p = mod[:, :bsz].reshape(2 * DEPTH, bsz, 1, 3 * D_MODEL)
    mod_s = mod[:, bsz:]

    def moe(tok_p, tok_s, cnt, layer):
        x1_p, h2c_p, route_p = tok_p
        x1_s, h2c_s, route_s = tok_s
        dest, sched = _moe_plan(route_p, route_s, cnt, layer)
        dest_p, dest_s = dest[:ntok_p], dest[ntok_p:]
        nrow = sched.shape[1] * MOE_TILE
        sorted_x = jnp.zeros((nrow * CHUNKS, LANES), F32)
        sorted_x = _dispatch_call(dest_p, h2c_p, sorted_x, DISPATCH_ROWS)
        sorted_x = _dispatch_call(dest_s, h2c_s, sorted_x, nsmp)
        y_sorted = _moe_call(sched, sorted_x, wg, wu, wd)
        lng, lnb = p[f"ln_g{layer}1"], p[f"ln_b{layer}1"]
        yp = _combine_call(dest_p, x1_p, mod_p[2 * layer + 1], route_p, y_sorted, lng, lnb,
                           seq // ROW_TILE)
        ys = _combine_call(dest_s, x1_s, mod_s[2 * layer + 1][None], route_s, y_sorted, lng, lnb, 1)
        return yp.reshape(bsz, seq, D_MODEL), ys

    xs0 = x_sample.reshape(nsmp, D_MODEL)
    *tok0p, cnt0, conv_p, gla_p = _mixer0_prompt_call(x_prompt, mod_p[0], mod_p[1], p)
    *tok0s, cnt0, gla_s, conv_s = _mixer0_sample_call(xs0, mod_s[0], mod_s[1], state_gla,
                                                      state_conv, cnt0, p)
    x2p, x2s = moe(tok0p, tok0s, cnt0, 0)
    *tok1p, cnt1, pool_p = _mixer1_prompt_call(x2p, mod_p[2], mod_p[3], p)
    *tok1s, cnt1, pool_s = _mixer1_sample_call(x2s, mod_s[2], mod_s[3], state_pool, cnt1, p)
    x4p, x4s = moe(tok1p, tok1s, cnt1, 1)
    return (x4p, x4s.reshape(nsmp, 1, D_MODEL), gla_p, conv_p, pool_p, gla_s, conv_s, pool_s)
```

```python
import jax
import jax.numpy as jnp
from jax import lax
from jax.experimental import pallas as pl
from jax.experimental.pallas import tpu as pltpu

F32 = jnp.float32
BF16 = jnp.bfloat16
I32 = jnp.int32

D_MODEL = 1024
GLA_HEADS = 4
GLA_DK = 64
GLA_DV = 128
GLA_KW = GLA_HEADS * GLA_DK
GLA_VW = GLA_HEADS * GLA_DV
GLA_RANK = 16
GLA_TAU = 16.0
CONV_CH = 512
CONV_WIDTH = 31
CONV_BUF = CONV_WIDTH - 1
POOL_WINDOWS = (2, 4, 8, 16)
POOL_GC = D_MODEL // len(POOL_WINDOWS)
POOL_BUF = max(POOL_WINDOWS) - 1
MOE_GROUPS = 4
MOE_EXPERTS = 4
N_EXPERTS = MOE_GROUPS * MOE_EXPERTS
N_PAIRS = MOE_EXPERTS * (MOE_EXPERTS - 1) // 2
N_BUCKETS = MOE_GROUPS * N_PAIRS
PAIR_A = (0, 0, 0, 1, 1, 2)
PAIR_B = (1, 2, 3, 2, 3, 3)
MOE_FF = 512
DEPTH = 2
ALPHA = (2 * DEPTH) ** 0.25
LN_EPS = 1e-5

LANES = 128
SUBLANES = 8
CHUNKS = D_MODEL // LANES
A_PAD = LANES
Z_WIDTH = 2 * GLA_KW + 2 * GLA_VW + 2 * CONV_CH + A_PAD
NEG_BIG = -1e30
VMEM_LIMIT = 56 * 1024 * 1024

TILE_L = 256
CONV_HALO = 32
CONV_W_ROWS = 32
POOL_HALO = 16
SAMPLE_BLK = 16
MOE_TILE = 256


def _dot(a, b):
    return jnp.dot(a, b, preferred_element_type=F32)


def _dot_nt(a, b):
    return lax.dot_general(a, b, (((1,), (1,)), ((), ())), preferred_element_type=F32)


def _split3(x):
    hi = x.astype(BF16)
    r1 = x - hi.astype(F32)
    mid = r1.astype(BF16)
    lo = (r1 - mid.astype(F32)).astype(BF16)
    return hi, mid, lo


def _split2(x):
    hi = x.astype(BF16)
    lo = (x - hi.astype(F32)).astype(BF16)
    return hi, lo


def _dot_w3(a, w_hi, w_lo):
    a_hi, a_lo = _split2(a)
    return _dot(a_hi, w_hi) + (_dot(a_lo, w_hi) + _dot(a_hi, w_lo))


def _lhs3(parts):
    hi, lo = parts
    return jnp.concatenate([hi, lo, hi], axis=1)


def _rhs3_rows(x):
    hi, lo = _split2(x)
    return jnp.concatenate([hi, hi, lo], axis=0)


def _rhs3_lanes(parts):
    hi, lo = parts
    return jnp.concatenate([hi, hi, lo], axis=1)


def _layer_norm(x, g, b):
    mu = jnp.mean(x, axis=-1, keepdims=True)
    xc = x - mu
    var = jnp.mean(xc * xc, axis=-1, keepdims=True)
    return xc * lax.rsqrt(var + LN_EPS) * g + b


def _standardize(x):
    mu = jnp.mean(x, axis=-1, keepdims=True)
    xc = x - mu
    var = jnp.mean(xc * xc, axis=-1, keepdims=True)
    return xc * lax.rsqrt(var + LN_EPS)


def _sigmoid(x):
    return 1.0 / (1.0 + jnp.exp(-x))


def _silu(x):
    return x * _sigmoid(x)


def _log_sigmoid(x):
    return jnp.minimum(x, 0.0) - jnp.log(1.0 + jnp.exp(-jnp.abs(x)))


def _mod3(mod):
    return mod[:, 0:D_MODEL], mod[:, D_MODEL:2 * D_MODEL], mod[:, 2 * D_MODEL:3 * D_MODEL]


def _full(shape):
    nd = len(shape)
    return pl.BlockSpec(shape, lambda *_: (0,) * nd)


def _const(shape):
    nd = len(shape)
    return pl.BlockSpec(shape, lambda *_: (0,) * nd, pipeline_mode=pl.Buffered(1))


def _hi_lo(w):
    bits = lax.bitcast_convert_type(w, jnp.uint32) & jnp.uint32(0xFFFF0000)
    hi = lax.bitcast_convert_type(bits, F32)
    return jnp.stack([hi.astype(BF16), (w - hi).astype(BF16)])


def _params(*sem):
    return pltpu.CompilerParams(dimension_semantics=sem, vmem_limit_bytes=VMEM_LIMIT)


def _ada_kernel(c_ref, w_ref, b_ref, o_ref):
    w_hi, w_lo = _split2(w_ref[0])
    o_ref[0] = _dot_w3(_silu(c_ref[...]), w_hi, w_lo) + b_ref[0]


def _ada_call(c_all, w_ada, b_ada):
    n = c_all.shape[0]
    ncomb = w_ada.shape[0]
    tn = D_MODEL
    return pl.pallas_call(
        _ada_kernel,
        grid=(ncomb, 3 * D_MODEL // tn),
        in_specs=[
            pl.BlockSpec((n, D_MODEL), lambda i, j: (0, 0)),
            pl.BlockSpec((1, D_MODEL, tn), lambda i, j: (i, 0, j)),
            pl.BlockSpec((1, 1, tn), lambda i, j: (i, 0, j)),
        ],
        out_specs=pl.BlockSpec((1, n, tn), lambda i, j: (i, 0, j)),
        out_shape=jax.ShapeDtypeStruct((ncomb, n, 3 * D_MODEL), F32),
        compiler_params=_params("arbitrary", "arbitrary"),
        name="ada",
    )(c_all, w_ada, b_ada)


def _route(logits, carry):
    rows = logits.shape[0]
    lane = lax.broadcasted_iota(I32, (rows, LANES), 1)
    lanef = lane.astype(F32)
    big = float(LANES)
    lc = jnp.where(lane < MOE_GROUPS, logits, NEG_BIG)
    mc = jnp.max(lc, axis=-1, keepdims=True)
    pg = 1.0 / jnp.sum(jnp.exp(lc - mc), axis=-1, keepdims=True)
    gidx = jnp.min(jnp.where(lc == mc, lanef, big), axis=-1, keepdims=True)
    lo = float(MOE_GROUPS) + gidx * float(MOE_EXPERTS)
    in_grp = (lanef >= lo) & (lanef < lo + float(MOE_EXPERTS))
    lf = jnp.where(in_grp, logits, NEG_BIG)
    m1 = jnp.max(lf, axis=-1, keepdims=True)
    i1 = jnp.min(jnp.where(lf == m1, lanef, big), axis=-1, keepdims=True)
    lf2 = jnp.where(lanef == i1, NEG_BIG, lf)
    m2 = jnp.max(lf2, axis=-1, keepdims=True)
    i2 = jnp.min(jnp.where(lf2 == m2, lanef, big), axis=-1, keepdims=True)
    e2 = jnp.exp(m2 - m1)
    w1 = pg / (1.0 + e2)
    w2 = pg * e2 / (1.0 + e2)
    ea = jnp.minimum(i1, i2) - lo
    eb = jnp.maximum(i1, i2) - lo
    w_a = jnp.where(i1 < i2, w1, w2)
    w_b = jnp.where(i1 < i2, w2, w1)
    bucket = gidx * float(N_PAIRS) + ea * (7.0 - ea) * 0.5 + eb - ea - 1.0
    onehot = jnp.where(lanef == bucket, 1.0, 0.0)
    r = lax.broadcasted_iota(I32, (rows, rows), 0)
    c = lax.broadcasted_iota(I32, (rows, rows), 1)
    before = _dot(jnp.where(r > c, 1.0, 0.0).astype(BF16), onehot.astype(BF16))
    rank = jnp.sum(onehot * (before + carry), axis=-1, keepdims=True)
    new_carry = carry + jnp.sum(onehot, axis=0, keepdims=True)
    route = (jnp.where(lane == 0, w_a, 0.0) + jnp.where(lane == 1, w_b, 0.0)
             + jnp.where(lane == 2, bucket, 0.0) + jnp.where(lane == 3, rank, 0.0))
    return route, new_carry


def _store_chunked(ref, x):
    rows = x.shape[0]
    for c in range(CHUNKS):
        ref[pl.ds(c, rows, stride=CHUNKS), :] = x[:, c * LANES:(c + 1) * LANES]


def _load_chunked(ref, rows, group, first):
    return jnp.concatenate(
        [ref[pl.ds(first + c, rows, stride=group), :] for c in range(CHUNKS)], axis=1)


def _router_logits(h2, wr_hi_ref, wr_lo_ref, br_ref):
    h_hi, h_lo = _split2(h2)
    wr_hi = wr_hi_ref[...]
    return _dot(h_hi, wr_hi) + _dot(h_lo, wr_hi) + _dot(h_hi, wr_lo_ref[...]) + br_ref[...]


def _post_mixer(x, m, gate, lng_ref, lnb_ref, mod_b, wr_hi_ref, wr_lo_ref, br_ref, carry,
                x1_ref, h2c_ref, route_ref):
    x1 = _layer_norm(ALPHA * x + (1.0 + gate) * m, lng_ref[...], lnb_ref[...])
    shift2, scale2, _ = _mod3(mod_b)
    h2 = x1 * (1.0 + scale2) + shift2
    pad = route_ref.shape[0] - x.shape[0]
    if pad:
        h2 = jnp.concatenate([h2, jnp.zeros((pad, D_MODEL), F32)], axis=0)
    route, carry = _route(_router_logits(h2, wr_hi_ref, wr_lo_ref, br_ref), carry)
    x1_ref[...] = x1
    _store_chunked(h2c_ref, h2)
    route_ref[...] = route
    return carry


def _token_out_specs(rows, slot_rows, x1_map, slot_map):
    return [pl.BlockSpec((rows, D_MODEL), x1_map),
            pl.BlockSpec((slot_rows * CHUNKS, LANES), slot_map),
            pl.BlockSpec((slot_rows, LANES), x1_map)]


def _token_out_shapes(ntok, nslot_h2, nslot_route):
    return [jax.ShapeDtypeStruct((ntok, D_MODEL), F32),
            jax.ShapeDtypeStruct((nslot_h2 * CHUNKS, LANES), F32),
            jax.ShapeDtypeStruct((nslot_route, LANES), F32)]


def _hold_rows(b, s):
    n, w = b.shape
    if s >= SUBLANES:
        pieces = []
        for p in range(n // (2 * s)):
            r = p * 2 * s + s - 1
            pieces.append(jnp.broadcast_to(b[r:r + 1, :], (2 * s, w)))
        return pieces[0] if len(pieces) == 1 else jnp.concatenate(pieces, axis=0)
    b3 = b.reshape(n // SUBLANES, SUBLANES, w)
    sub = lax.broadcasted_iota(I32, b3.shape, 1)

    def bc(r):
        return jnp.broadcast_to(b3[:, r:r + 1, :], b3.shape)

    out = bc(s - 1)
    for p in range(1, SUBLANES // (2 * s)):
        out = jnp.where(sub >= p * 2 * s, bc(p * 2 * s + s - 1), out)
    return out.reshape(n, w)


def _gla_tile(q, k, v, la, s_prev):
    n = q.shape[0]
    row = lax.broadcasted_iota(I32, (n, n), 0)
    col = lax.broadcasted_iota(I32, (n, n), 1)
    tri = jnp.where(row >= col, 1.0, 0.0).astype(BF16)
    hi, mid, lo = _split3(la)
    b = _dot(tri, hi) + _dot(tri, mid) + _dot(tri, lo)
    b_last = b[n - 1:n, :]
    kl = k * jnp.exp(b_last - b)

    rowi = lax.broadcasted_iota(I32, (n, GLA_KW), 0)
    levels = []
    s = n // 2
    while s >= 1:
        m = _hold_rows(b, s)
        second = (rowi & s) != 0
        qs = _split2(q * jnp.exp(jnp.where(second, b - m, NEG_BIG)))
        ks = _split2(k * jnp.exp(jnp.where(second, NEG_BIG, m - b)))
        levels.append((2 * s, qs, ks))
        s //= 2
    levels.append((1, _split2(q), _split2(k)))
    qe = _split2(q * jnp.exp(b))

    lane = lax.broadcasted_iota(I32, (n, LANES), 1)
    xor = row ^ col
    zero = jnp.zeros((), BF16)

    def head_part(parts, sl, hm):
        return tuple(jnp.where(hm, x[:, sl], zero) for x in parts)

    outs = []
    for h in range(GLA_HEADS):
        p = h // 2
        sl = slice(p * LANES, (p + 1) * LANES)
        hm = (lane < GLA_DK) if h % 2 == 0 else (lane >= GLA_DK)
        att = None
        for span, qs, ks in levels:
            a_s = _dot_nt(_lhs3(head_part(qs, sl, hm)), _rhs3_lanes((ks[0][:, sl], ks[1][:, sl])))
            att = a_s if att is None else jnp.where(xor < span, a_s, att)
        o_h = _dot(_lhs3(_split2(att)), _rhs3_rows(v[:, h * GLA_DV:(h + 1) * GLA_DV]))
        o_h = o_h + _dot(_lhs3(head_part(qe, sl, hm)), _rhs3_rows(s_prev[sl, :]))
        outs.append(o_h)

    decay = jnp.exp(b_last)
    upper = lax.broadcasted_iota(I32, (LANES, LANES), 0) < GLA_DK
    s_new = []
    for p in range(GLA_HEADS // 2):
        sl = slice(p * LANES, (p + 1) * LANES)
        kl_t = jnp.transpose(kl[:, sl])
        u = _dot(_lhs3(_split2(kl_t)), _rhs3_rows(v[:, p * 2 * GLA_DV:(p + 1) * 2 * GLA_DV]))
        upd = jnp.where(upper, u[:, 0:GLA_DV], u[:, GLA_DV:2 * GLA_DV])
        dcol = jnp.transpose(jnp.broadcast_to(decay[:, sl], (LANES, LANES)))
        s_new.append(dcol * s_prev[sl, :] + upd)
    return outs, jnp.concatenate(s_new, axis=0)


def _split_projection(z):
    c0 = 0
    q = z[:, c0:c0 + GLA_KW] * (GLA_DK ** -0.5); c0 += GLA_KW
    k = z[:, c0:c0 + GLA_KW]; c0 += GLA_KW
    v = z[:, c0:c0 + GLA_VW]; c0 += GLA_VW
    g = z[:, c0:c0 + GLA_VW]; c0 += GLA_VW
    ua = z[:, c0:c0 + CONV_CH]; c0 += CONV_CH
    ug = z[:, c0:c0 + CONV_CH]; c0 += CONV_CH
    a_lr = z[:, c0:c0 + A_PAD]
    return q, k, v, g, ua, ug, a_lr


def _mix0_project(o_heads, g, y, gng_ref, w_out_ref):
    sl = slice(GLA_VW, GLA_VW + CONV_CH)
    m = _dot_w3(y, w_out_ref[0, sl, :], w_out_ref[1, sl, :])
    for hd in range(GLA_HEADS):
        sl = slice(hd * GLA_DV, (hd + 1) * GLA_DV)
        o_h = _standardize(o_heads[hd]) * gng_ref[:, sl] * _silu(g[:, sl])
        m = m + _dot_w3(o_h, w_out_ref[0, sl, :], w_out_ref[1, sl, :])
    return m


def _mix0_inputs(h, w_in_ref, w_a2_ref, b_a_ref):
    z = _dot_w3(h, w_in_ref[0], w_in_ref[1])
    q, k, v, g, ua, ug, a_lr = _split_projection(z)
    la = _log_sigmoid(_dot_w3(a_lr, w_a2_ref[0], w_a2_ref[1]) + b_a_ref[...]) * (1.0 / GLA_TAU)
    return q, k, v, g, ua, ug, la


def _prompt_grid_specs(bsz, nt):
    ntile = bsz * nt
    seq_map = lambda b, t: (jnp.minimum(b, bsz - 1), 0, 0)
    x_map = lambda b, t: (jnp.minimum(b, bsz - 1), t, 0)
    tok_map = lambda b, t: (jnp.minimum(b * nt + t, ntile - 1), 0)
    slot_map = lambda b, t: (jnp.minimum(b * nt + t, ntile), 0)
    state_map = lambda b, t: (0, jnp.minimum(b, bsz - 1), 0, 0)
    return seq_map, x_map, tok_map, slot_map, state_map


def _mixer0_prompt_kernel(x_ref, moda_ref, modb_ref, cnt_in_ref, h2cs_ref, w_in_ref, w_a2_ref,
                          b_a_ref, gng_ref,
                          cw_ref, cb_ref, clg_ref, clb_ref, w_out_ref, lng_ref, lnb_ref,
                          wr_hi_ref, wr_lo_ref, br_ref,
                          x1_ref, h2c_ref, route_ref, cnt_ref, nconv_ref, ngla_ref,
                          s_ref, cbuf_ref, shift_ref, carry_ref):
    nseq = pl.num_programs(0) - 1

    @pl.when(pl.program_id(0) < nseq)
    def _():
        _mixer0_prompt_tile(x_ref, moda_ref, modb_ref, cnt_in_ref, w_in_ref, w_a2_ref, b_a_ref,
                            gng_ref, cw_ref, cb_ref, clg_ref, clb_ref, w_out_ref, lng_ref, lnb_ref,
                            wr_hi_ref, wr_lo_ref, br_ref, x1_ref, h2c_ref, route_ref, cnt_ref,
                            nconv_ref, ngla_ref, s_ref, cbuf_ref, shift_ref, carry_ref)

    @pl.when((pl.program_id(0) == nseq) & (pl.program_id(1) == 0))
    def _():
        h2c_ref[...] = h2cs_ref[...]


def _mixer0_prompt_tile(x_ref, moda_ref, modb_ref, cnt_in_ref, w_in_ref, w_a2_ref, b_a_ref,
                        gng_ref, cw_ref, cb_ref, clg_ref, clb_ref, w_out_ref, lng_ref, lnb_ref,
                        wr_hi_ref, wr_lo_ref, br_ref, x1_ref, h2c_ref, route_ref, cnt_ref,
                        nconv_ref, ngla_ref, s_ref, cbuf_ref, shift_ref, carry_ref):
    t = pl.program_id(1)
    n = x_ref.shape[1]

    @pl.when((t == 0) & (pl.program_id(0) == 0))
    def _():
        carry_ref[...] = cnt_in_ref[...]

    @pl.when(t == 0)
    def _():
        s_ref[...] = jnp.zeros_like(s_ref)
        cbuf_ref[0:CONV_HALO, :] = jnp.zeros((CONV_HALO, CONV_CH), F32)

    x = x_ref[0]
    shift, scale, gate = _mod3(moda_ref[0])
    h = x * (1.0 + scale) + shift
    q, k, v, g, ua, ug, la = _mix0_inputs(h, w_in_ref, w_a2_ref, b_a_ref)
    o_heads, s_new = _gla_tile(q, k, v, la, s_ref[...])
    s_ref[...] = s_new

    glu = ua * _sigmoid(ug)
    cbuf_ref[CONV_HALO:CONV_HALO + n, :] = glu
    acc = jnp.broadcast_to(cb_ref[...], (n, CONV_CH))
    base = CONV_HALO - CONV_BUF
    for r in range(SUBLANES):
        taps = range(r, CONV_WIDTH, SUBLANES)
        span = n + (len(taps) - 1) * SUBLANES
        if (base + r) % SUBLANES == 0:
            src, off = cbuf_ref, base + r
        else:
            shift_ref[r, 0:span, :] = cbuf_ref[base + r:base + r + span, :]
            src, off = shift_ref.at[r], 0
        for a, j in enumerate(taps):
            lo = off + a * SUBLANES
            acc = acc + cw_ref[j:j + 1, :] * src[lo:lo + n, :]
    y = _silu(_layer_norm(acc, clg_ref[...], clb_ref[...]))

    @pl.when(t == pl.num_programs(1) - 1)
    def _():
        nconv_ref[0, 0] = cbuf_ref[CONV_HALO + n - CONV_BUF:CONV_HALO + n, :]
        ngla_ref[0, 0] = s_new.reshape(GLA_HEADS, GLA_DK, GLA_DV)

    cbuf_ref[0:CONV_HALO, :] = cbuf_ref[n:n + CONV_HALO, :]

    m = _mix0_project(o_heads, g, y, gng_ref, w_out_ref)
    carry = _post_mixer(x, m, gate, lng_ref, lnb_ref, modb_ref[0], wr_hi_ref, wr_lo_ref, br_ref,
                        carry_ref[...], x1_ref, h2c_ref, route_ref)
    carry_ref[...] = carry
    cnt_ref[...] = carry


def _mixer0_prompt_call(x, mod_a, mod_b, cnt, h2c_s, p):
    bsz, seq, _ = x.shape
    tl = TILE_L
    nt = seq // tl
    seq_map, x_map, tok, slot, state_map = _prompt_grid_specs(bsz, nt)
    return pl.pallas_call(
        _mixer0_prompt_kernel,
        grid=(bsz + 1, nt),
        in_specs=[
            pl.BlockSpec((1, tl, D_MODEL), x_map),
            pl.BlockSpec((1, 1, 3 * D_MODEL), seq_map),
            pl.BlockSpec((1, 1, 3 * D_MODEL), seq_map),
            _full((1, LANES)),
            _full((tl * CHUNKS, LANES)),
            _const((2, D_MODEL, Z_WIDTH)),
            _const((2, A_PAD, GLA_KW)),
            _full((1, GLA_KW)),
            _full((1, GLA_VW)),
            _full((CONV_W_ROWS, CONV_CH)),
            _full((1, CONV_CH)),
            _full((1, CONV_CH)),
            _full((1, CONV_CH)),
            _const((2, GLA_VW + CONV_CH, D_MODEL)),
            _full((1, D_MODEL)),
            _full((1, D_MODEL)),
            _full((D_MODEL, LANES)),
            _full((D_MODEL, LANES)),
            _full((1, LANES)),
        ],
        out_specs=_token_out_specs(tl, tl, tok, slot) + [
            _full((1, LANES)),
            pl.BlockSpec((1, 1, CONV_BUF, CONV_CH), state_map),
            pl.BlockSpec((1, 1, GLA_HEADS, GLA_DK, GLA_DV), lambda b, t: state_map(b, t) + (0,)),
        ],
        out_shape=_token_out_shapes(bsz * seq, bsz * seq + tl, bsz * seq) + [
            jax.ShapeDtypeStruct((1, LANES), F32),
            jax.ShapeDtypeStruct((1, bsz, CONV_BUF, CONV_CH), F32),
            jax.ShapeDtypeStruct((1, bsz, GLA_HEADS, GLA_DK, GLA_DV), F32),
        ],
        scratch_shapes=[
            pltpu.VMEM((GLA_KW, GLA_DV), F32),
            pltpu.VMEM((CONV_HALO + tl, CONV_CH), F32),
            pltpu.VMEM((SUBLANES, CONV_HALO + tl, CONV_CH), F32),
            pltpu.VMEM((1, LANES), F32),
        ],
        compiler_params=_params("arbitrary", "arbitrary"),
        name="mixer0_prompt",
    )(x, mod_a, mod_b, cnt, h2c_s, p["w_in"], p["w_a2"], p["b_a"], p["gng"], p["conv_w"], p["conv_b"],
      p["conv_ln_g"], p["conv_ln_b"], p["w_out0"], p["ln_g00"], p["ln_b00"],
      p["wr_hi0"], p["wr_lo0"], p["br0"])


def _mixer0_sample_kernel(x_ref, moda_ref, modb_ref, sgla_ref, sconv_ref, cnt_in_ref,
                          w_in_ref, w_a2_ref,
                          b_a_ref, gng_ref, cw_ref, cb_ref, clg_ref, clb_ref, w_out_ref, lng_ref,
                          lnb_ref, wr_hi_ref, wr_lo_ref, br_ref,
                          x1_ref, h2c_ref, route_ref, cnt_ref, ngla_ref, nconv_ref,
                          zt_ref, v_ref, g_ref, glu_ref, o_ref, y_ref):
    i = pl.program_id(0)
    nb = sgla_ref.shape[1]
    ntok = x_ref.shape[0]

    @pl.when(i == 0)
    def _():
        shift, scale, _ = _mod3(moda_ref[...])
        h = x_ref[...] * (1.0 + scale) + shift
        q, k, v, g, ua, ug, la = _mix0_inputs(h, w_in_ref, w_a2_ref, b_a_ref)
        v_ref[...] = v
        g_ref[...] = g
        glu_ref[...] = ua * _sigmoid(ug)
        for j, val in enumerate((jnp.exp(la), k, q)):
            hi, lo = _split2(jnp.transpose(val))
            zt_ref[(2 * j) * GLA_KW:(2 * j + 1) * GLA_KW, :] = hi
            zt_ref[(2 * j + 1) * GLA_KW:(2 * j + 2) * GLA_KW, :] = lo

    tok_row = lax.broadcasted_iota(I32, (ntok, LANES), 0)
    blk = pl.ds(pl.multiple_of(i * nb, nb), nb)
    v_blk = v_ref[blk, :]
    glu_blk = glu_ref[blk, :]
    o_rows, y_rows = [], []
    for n in range(nb):
        onehot = jnp.where(tok_row == i * nb + n, 1.0, 0.0).astype(BF16)
        cols = _dot(zt_ref[...], onehot)
        a_col = cols[0:GLA_KW] + cols[GLA_KW:2 * GLA_KW]
        k_col = cols[2 * GLA_KW:3 * GLA_KW] + cols[3 * GLA_KW:4 * GLA_KW]
        q_col = cols[4 * GLA_KW:5 * GLA_KW] + cols[5 * GLA_KW:6 * GLA_KW]
        v_row = v_blk[n:n + 1, :]
        v_b = jnp.concatenate(
            [jnp.broadcast_to(v_row[:, h * GLA_DV:(h + 1) * GLA_DV], (GLA_DK, GLA_DV))
             for h in range(GLA_HEADS)], axis=0)
        s_old = sgla_ref[0, n].reshape(GLA_KW, GLA_DV)
        s_new = a_col * s_old + k_col * v_b
        ngla_ref[0, n] = s_new.reshape(GLA_HEADS, GLA_DK, GLA_DV)
        o4 = jnp.sum((q_col * s_new).reshape(GLA_HEADS, GLA_DK, GLA_DV), axis=1)
        o_rows.append(jnp.concatenate([o4[h:h + 1, :] for h in range(GLA_HEADS)], axis=1))
        glu_row = glu_blk[n:n + 1, :]
        past = sconv_ref[0, n]
        y_rows.append(jnp.sum(past * cw_ref[0:CONV_BUF, :], axis=0, keepdims=True)
                      + glu_row * cw_ref[CONV_BUF:CONV_WIDTH, :] + cb_ref[...])
        nconv_ref[0, n, 0:CONV_BUF - 1, :] = sconv_ref[0, n, 1:CONV_BUF, :]
        nconv_ref[0, n, CONV_BUF - 1:CONV_BUF, :] = glu_row
    o_ref[blk, :] = jnp.concatenate(o_rows, axis=0)
    y_ref[blk, :] = jnp.concatenate(y_rows, axis=0)

    @pl.when(i == pl.num_programs(0) - 1)
    def _():
        y = _silu(_layer_norm(y_ref[...], clg_ref[...], clb_ref[...]))
        o_heads = [o_ref[:, hd * GLA_DV:(hd + 1) * GLA_DV] for hd in range(GLA_HEADS)]
        m = _mix0_project(o_heads, g_ref[...], y, gng_ref, w_out_ref)
        _, _, gate = _mod3(moda_ref[...])
        cnt_ref[...] = _post_mixer(x_ref[...], m, gate, lng_ref, lnb_ref, modb_ref[...], wr_hi_ref,
                                   wr_lo_ref, br_ref, cnt_in_ref[...], x1_ref, h2c_ref, route_ref)


def _mixer0_sample_call(x, mod_a, mod_b, state_gla, state_conv, cnt, p):
    ntok = x.shape[0]
    nb = SAMPLE_BLK
    tok = lambda i: (0, 0)
    return pl.pallas_call(
        _mixer0_sample_kernel,
        grid=(ntok // nb,),
        in_specs=[
            _full((ntok, D_MODEL)),
            _full((ntok, 3 * D_MODEL)),
            _full((ntok, 3 * D_MODEL)),
            pl.BlockSpec((1, nb, GLA_HEADS, GLA_DK, GLA_DV), lambda i: (0, i, 0, 0, 0)),
            pl.BlockSpec((1, nb, CONV_BUF, CONV_CH), lambda i: (0, i, 0, 0)),
            _full((1, LANES)),
            _const((2, D_MODEL, Z_WIDTH)),
            _const((2, A_PAD, GLA_KW)),
            _full((1, GLA_KW)),
            _full((1, GLA_VW)),
            _full((CONV_W_ROWS, CONV_CH)),
            _full((1, CONV_CH)),
            _full((1, CONV_CH)),
            _full((1, CONV_CH)),
            _const((2, GLA_VW + CONV_CH, D_MODEL)),
            _full((1, D_MODEL)),
            _full((1, D_MODEL)),
            _full((D_MODEL, LANES)),
            _full((D_MODEL, LANES)),
            _full((1, LANES)),
        ],
        out_specs=_token_out_specs(ntok, TILE_L, tok, tok) + [
            _full((1, LANES)),
            pl.BlockSpec((1, nb, GLA_HEADS, GLA_DK, GLA_DV), lambda i: (0, i, 0, 0, 0)),
            pl.BlockSpec((1, nb, CONV_BUF, CONV_CH), lambda i: (0, i, 0, 0)),
        ],
        out_shape=_token_out_shapes(ntok, TILE_L, TILE_L) + [
            jax.ShapeDtypeStruct((1, LANES), F32),
            jax.ShapeDtypeStruct(state_gla.shape, F32),
            jax.ShapeDtypeStruct(state_conv.shape, F32),
        ],
        scratch_shapes=[
            pltpu.VMEM((6 * GLA_KW, ntok), BF16),
            pltpu.VMEM((ntok, GLA_VW), F32),
            pltpu.VMEM((ntok, GLA_VW), F32),
            pltpu.VMEM((ntok, CONV_CH), F32),
            pltpu.VMEM((ntok, GLA_VW), F32),
            pltpu.VMEM((ntok, CONV_CH), F32),
        ],
        compiler_params=_params("arbitrary"),
        name="mixer0_sample",
    )(x, mod_a, mod_b, state_gla, state_conv, cnt, p["w_in"], p["w_a2"], p["b_a"],
      p["gng"], p["conv_w"], p["conv_b"], p["conv_ln_g"], p["conv_ln_b"], p["w_out0"],
      p["ln_g00"], p["ln_b00"], p["wr_hi0"], p["wr_lo0"], p["br0"])


def _pool_project(pooled, h, wg_ref, ps_ref, w_out_ref):
    m = None
    for gi in range(len(POOL_WINDOWS)):
        sl = slice(gi * POOL_GC, (gi + 1) * POOL_GC)
        mixed = _dot((pooled[:, sl] - h[:, sl]).astype(BF16), wg_ref[gi]) * ps_ref[:, sl]
        part = _dot(mixed.astype(BF16), w_out_ref[sl, :])
        m = part if m is None else m + part
    return m


def _mixer1_prompt_kernel(x_ref, moda_ref, modb_ref, cnt_in_ref, h2cs_ref, wg_ref, ps_ref,
                          w_out_ref, lng_ref, lnb_ref, wr_hi_ref, wr_lo_ref, br_ref,
                          x1_ref, h2c_ref, route_ref, cnt_ref, npool_ref,
                          pbuf_ref, carry_ref):
    nseq = pl.num_programs(0) - 1

    @pl.when(pl.program_id(0) < nseq)
    def _():
        _mixer1_prompt_tile(x_ref, moda_ref, modb_ref, cnt_in_ref, wg_ref, ps_ref, w_out_ref,
                            lng_ref, lnb_ref, wr_hi_ref, wr_lo_ref, br_ref, x1_ref, h2c_ref,
                            route_ref, cnt_ref, npool_ref, pbuf_ref, carry_ref)

    @pl.when((pl.program_id(0) == nseq) & (pl.program_id(1) == 0))
    def _():
        h2c_ref[...] = h2cs_ref[...]


def _mixer1_prompt_tile(x_ref, moda_ref, modb_ref, cnt_in_ref, wg_ref, ps_ref, w_out_ref,
                        lng_ref, lnb_ref, wr_hi_ref, wr_lo_ref, br_ref, x1_ref, h2c_ref,
                        route_ref, cnt_ref, npool_ref, pbuf_ref, carry_ref):
    t = pl.program_id(1)
    n = x_ref.shape[1]

    @pl.when((t == 0) & (pl.program_id(0) == 0))
    def _():
        carry_ref[...] = cnt_in_ref[...]

    @pl.when(t == 0)
    def _():
        pbuf_ref[0:POOL_HALO, :] = jnp.zeros((POOL_HALO, D_MODEL), F32)

    x = x_ref[0]
    shift, scale, gate = _mod3(moda_ref[0])
    h = x * (1.0 + scale) + shift
    pbuf_ref[POOL_HALO:POOL_HALO + n, :] = h

    cur = pbuf_ref[...]
    sums = []
    for gi, w in enumerate(POOL_WINDOWS):
        cur = cur + pltpu.roll(cur, w // 2, axis=0)
        sums.append(cur[POOL_HALO:, 0:POOL_GC])
        if gi + 1 < len(POOL_WINDOWS):
            cur = cur[:, POOL_GC:]
    pos = lax.broadcasted_iota(I32, (n, POOL_GC), 0) + t * n
    pooled = jnp.concatenate(
        [s / jnp.minimum(w, pos + 1).astype(F32) for s, w in zip(sums, POOL_WINDOWS)], axis=1)

    @pl.when(t == pl.num_programs(1) - 1)
    def _():
        npool_ref[0, 0] = pbuf_ref[POOL_HALO + n - POOL_BUF:POOL_HALO + n, :]

    pbuf_ref[0:POOL_HALO, :] = pbuf_ref[n:n + POOL_HALO, :]

    m = _pool_project(pooled, h, wg_ref, ps_ref, w_out_ref)
    carry = _post_mixer(x, m, gate, lng_ref, lnb_ref, modb_ref[0], wr_hi_ref, wr_lo_ref, br_ref,
                        carry_ref[...], x1_ref, h2c_ref, route_ref)
    carry_ref[...] = carry
    cnt_ref[...] = carry


def _mixer1_prompt_call(x, mod_a, mod_b, cnt, h2c_s, p):
    bsz, seq, _ = x.shape
    tl = TILE_L
    nt = seq // tl
    seq_map, x_map, tok, slot, state_map = _prompt_grid_specs(bsz, nt)
    ng = len(POOL_WINDOWS)
    return pl.pallas_call(
        _mixer1_prompt_kernel,
        grid=(bsz + 1, nt),
        in_specs=[
            pl.BlockSpec((1, tl, D_MODEL), x_map),
            pl.BlockSpec((1, 1, 3 * D_MODEL), seq_map),
            pl.BlockSpec((1, 1, 3 * D_MODEL), seq_map),
            _full((1, LANES)),
            _full((tl * CHUNKS, LANES)),
            _full((ng, POOL_GC, POOL_GC)),
            _full((1, D_MODEL)),
            _full((D_MODEL, D_MODEL)),
            _full((1, D_MODEL)),
            _full((1, D_MODEL)),
            _full((D_MODEL, LANES)),
            _full((D_MODEL, LANES)),
            _full((1, LANES)),
        ],
        out_specs=_token_out_specs(tl, tl, tok, slot) + [
            _full((1, LANES)),
            pl.BlockSpec((1, 1, POOL_BUF, D_MODEL), state_map),
        ],
        out_shape=_token_out_shapes(bsz * seq, bsz * seq + tl, bsz * seq) + [
            jax.ShapeDtypeStruct((1, LANES), F32),
            jax.ShapeDtypeStruct((1, bsz, POOL_BUF, D_MODEL), F32),
        ],
        scratch_shapes=[pltpu.VMEM((POOL_HALO + tl, D_MODEL), F32), pltpu.VMEM((1, LANES), F32)],
        compiler_params=_params("arbitrary", "arbitrary"),
        name="mixer1_prompt",
    )(x, mod_a, mod_b, cnt, h2c_s, p["w_grp"], p["pool_scale"], p["w_out1"], p["ln_g10"], p["ln_b10"],
      p["wr_hi1"], p["wr_lo1"], p["br1"])


def _mixer1_sample_kernel(x_ref, moda_ref, modb_ref, spool_ref, cnt_in_ref,
                          wg_ref, ps_ref, w_out_ref,
                          lng_ref, lnb_ref, wr_hi_ref, wr_lo_ref, br_ref,
                          x1_ref, h2c_ref, route_ref, cnt_ref, npool_ref, h_ref, pooled_ref):
    i = pl.program_id(0)
    nb = spool_ref.shape[1]

    @pl.when(i == 0)
    def _():
        shift, scale, _ = _mod3(moda_ref[...])
        h_ref[...] = x_ref[...] * (1.0 + scale) + shift

    lane = lax.broadcasted_iota(I32, (POOL_BUF, D_MODEL), 1)
    rowi = lax.broadcasted_iota(I32, (POOL_BUF, D_MODEL), 0)
    first = jnp.zeros((POOL_BUF, D_MODEL), I32)
    lane1 = lax.broadcasted_iota(I32, (1, D_MODEL), 1)
    inv_w = jnp.zeros((1, D_MODEL), F32)
    for gi, w in enumerate(POOL_WINDOWS):
        in_g = (lane >= gi * POOL_GC) & (lane < (gi + 1) * POOL_GC)
        first = jnp.where(in_g, POOL_BUF - (w - 1), first)
        in_g1 = (lane1 >= gi * POOL_GC) & (lane1 < (gi + 1) * POOL_GC)
        inv_w = jnp.where(in_g1, 1.0 / w, inv_w)
    keep = rowi >= first

    blk = pl.ds(pl.multiple_of(i * nb, nb), nb)
    h_blk = h_ref[blk, :]
    rows = []
    for n in range(nb):
        h_row = h_blk[n:n + 1, :]
        past = spool_ref[0, n]
        total = jnp.sum(jnp.where(keep, past, 0.0), axis=0, keepdims=True) + h_row
        rows.append(total * inv_w)
        npool_ref[0, n, 0:POOL_BUF - 1, :] = spool_ref[0, n, 1:POOL_BUF, :]
        npool_ref[0, n, POOL_BUF - 1:POOL_BUF, :] = h_row
    pooled_ref[blk, :] = jnp.concatenate(rows, axis=0)

    @pl.when(i == pl.num_programs(0) - 1)
    def _():
        m = _pool_project(pooled_ref[...], h_ref[...], wg_ref, ps_ref, w_out_ref)
        _, _, gate = _mod3(moda_ref[...])
        cnt_ref[...] = _post_mixer(x_ref[...], m, gate, lng_ref, lnb_ref, modb_ref[...], wr_hi_ref,
                                   wr_lo_ref, br_ref, cnt_in_ref[...], x1_ref, h2c_ref, route_ref)


def _mixer1_sample_call(x, mod_a, mod_b, state_pool, cnt, p):
    ntok = x.shape[0]
    nb = SAMPLE_BLK
    ng = len(POOL_WINDOWS)
    tok = lambda i: (0, 0)
    return pl.pallas_call(
        _mixer1_sample_kernel,
        grid=(ntok // nb,),
        in_specs=[
            _full((ntok, D_MODEL)),
            _full((ntok, 3 * D_MODEL)),
            _full((ntok, 3 * D_MODEL)),
            pl.BlockSpec((1, nb, POOL_BUF, D_MODEL), lambda i: (0, i, 0, 0)),
            _full((1, LANES)),
            _full((ng, POOL_GC, POOL_GC)),
            _full((1, D_MODEL)),
            _full((D_MODEL, D_MODEL)),
            _full((1, D_MODEL)),
            _full((1, D_MODEL)),
            _full((D_MODEL, LANES)),
            _full((D_MODEL, LANES)),
            _full((1, LANES)),
        ],
        out_specs=_token_out_specs(ntok, TILE_L, tok, tok) + [
            _full((1, LANES)),
            pl.BlockSpec((1, nb, POOL_BUF, D_MODEL), lambda i: (0, i, 0, 0)),
        ],
        out_shape=_token_out_shapes(ntok, TILE_L, TILE_L) + [
            jax.ShapeDtypeStruct((1, LANES), F32),
            jax.ShapeDtypeStruct(state_pool.shape, F32),
        ],
        scratch_shapes=[pltpu.VMEM((ntok, D_MODEL), F32), pltpu.VMEM((ntok, D_MODEL), F32)],
        compiler_params=_params("arbitrary"),
        name="mixer1_sample",
    )(x, mod_a, mod_b, state_pool, cnt, p["w_grp"], p["pool_scale"], p["w_out1"],
      p["ln_g10"], p["ln_b10"], p["wr_hi1"], p["wr_lo1"], p["br1"])


S_EA, S_EB, S_VALID, S_FIRST, S_LANE_G, S_LANE_A, S_LANE_B, S_NEXT = range(8)


def _moe_kernel(sched_ref, src_ref, srcn_ref, h_any, wr_hi_ref, wr_lo_ref, br_ref,
                wga_ref, wua_ref, wda_ref, wgb_ref, wub_ref, wdb_ref,
                m_any,
                xbuf, ybuf, wgu_ref, wd_ref, gsem, ssem):
    i = pl.program_id(0)
    tm = src_ref.shape[2]
    nslot = h_any.shape[0] // CHUNKS
    slot = i % 2

    def gather(ids_ref, s):
        for j in range(tm):
            tok = ids_ref[0, 0, j]
            tok = jnp.where(tok >= nslot, 0, tok)
            pltpu.make_async_copy(
                h_any.at[pl.ds(pl.multiple_of(tok * CHUNKS, CHUNKS), CHUNKS)],
                xbuf.at[s, pl.ds(j * CHUNKS, CHUNKS)], gsem.at[s]).start()

    def scatter(s):
        for j in range(tm):
            tok = src_ref[0, 0, j]
            row = jnp.where(tok >= nslot, nslot + s * tm + j, tok)
            pltpu.make_async_copy(
                ybuf.at[s, pl.ds(j * CHUNKS, CHUNKS)],
                m_any.at[pl.ds(pl.multiple_of(row * CHUNKS, CHUNKS), CHUNKS)], ssem.at[s]).start()

    def wait_gather(s):
        pltpu.make_async_copy(h_any.at[pl.ds(0, tm * CHUNKS)], xbuf.at[s], gsem.at[s]).wait()

    def wait_scatter(s):
        pltpu.make_async_copy(ybuf.at[s], m_any.at[pl.ds(0, tm * CHUNKS)], ssem.at[s]).wait()

    @pl.when(i == 0)
    def _():
        ybuf[...] = jnp.zeros_like(ybuf)
        for s in range(2):
            spare = m_any.at[pl.ds((nslot + s * tm) * CHUNKS, tm * CHUNKS)]
            pltpu.make_async_copy(ybuf.at[s], spare, ssem.at[s]).start()
        for s in range(2):
            wait_scatter(s)
        gather(src_ref, 0)

    @pl.when(sched_ref[S_NEXT, i] == 1)
    def _():
        gather(srcn_ref, 1 - slot)

    @pl.when(sched_ref[S_FIRST, i] == 1)
    def _():
        for e, (wg, wu, wd) in enumerate(((wga_ref, wua_ref, wda_ref), (wgb_ref, wub_ref, wdb_ref))):
            wgu_ref[2 * e] = wg[0].astype(BF16)
            wgu_ref[2 * e + 1] = wu[0].astype(BF16)
            wd_ref[e * MOE_FF:(e + 1) * MOE_FF, :] = wd[0].astype(BF16)

    @pl.when(sched_ref[S_VALID, i] == 1)
    def _():
        wait_gather(slot)
        x = _load_chunked(xbuf.at[slot], tm, CHUNKS, 0)
        logits = _router_logits(x, wr_hi_ref, wr_lo_ref, br_ref)
        lane = lax.broadcasted_iota(I32, logits.shape, 1)

        def pick(row):
            return jnp.sum(jnp.where(lane == sched_ref[row, i], logits, 0.0), axis=-1, keepdims=True)

        l_g, l_a, l_b = pick(S_LANE_G), pick(S_LANE_A), pick(S_LANE_B)
        p_g = 1.0 / jnp.sum(jnp.where(lane < MOE_GROUPS, jnp.exp(logits - l_g), 0.0),
                            axis=-1, keepdims=True)
        w_ab = (p_g / (1.0 + jnp.exp(l_b - l_a)), p_g / (1.0 + jnp.exp(l_a - l_b)))
        xb = x.astype(BF16)
        hid = [(_silu(_dot(xb, wgu_ref[2 * e])) * _dot(xb, wgu_ref[2 * e + 1]) * w_ab[e]).astype(BF16)
               for e in range(2)]
        _store_chunked(ybuf.at[slot], _dot(jnp.concatenate(hid, axis=1), wd_ref[...]))
        scatter(slot)

        @pl.when(i > 0)
        def _():
            wait_scatter(1 - slot)

        @pl.when(sched_ref[S_NEXT, i] == 0)
        def _():
            wait_scatter(slot)


def _moe_call(sched, src3, h2c_all, wr_hi, wr_lo, br, wg, wu, wd):
    ntile = sched.shape[1]
    tm = MOE_TILE
    nslot = h2c_all.shape[0] // CHUNKS
    ea = lambda i, s: (s[S_EA, i], 0, 0)
    eb = lambda i, s: (s[S_EB, i], 0, 0)
    const2 = lambda i, s: (0, 0)
    smem = pltpu.SMEM
    grid_spec = pltpu.PrefetchScalarGridSpec(
        num_scalar_prefetch=1,
        grid=(ntile,),
        in_specs=[
            pl.BlockSpec((1, 1, tm), lambda i, s: (i, 0, 0), memory_space=smem),
            pl.BlockSpec((1, 1, tm), lambda i, s: (jnp.minimum(i + 1, ntile - 1), 0, 0),
                         memory_space=smem),
            pl.BlockSpec(memory_space=pl.ANY),
            pl.BlockSpec((D_MODEL, LANES), const2),
            pl.BlockSpec((D_MODEL, LANES), const2),
            pl.BlockSpec((1, LANES), const2),
            pl.BlockSpec((1, D_MODEL, MOE_FF), ea),
            pl.BlockSpec((1, D_MODEL, MOE_FF), ea),
            pl.BlockSpec((1, MOE_FF, D_MODEL), ea),
            pl.BlockSpec((1, D_MODEL, MOE_FF), eb),
            pl.BlockSpec((1, D_MODEL, MOE_FF), eb),
            pl.BlockSpec((1, MOE_FF, D_MODEL), eb),
        ],
        out_specs=pl.BlockSpec(memory_space=pl.ANY),
        scratch_shapes=[
            pltpu.VMEM((2, tm * CHUNKS, LANES), F32),
            pltpu.VMEM((2, tm * CHUNKS, LANES), F32),
            pltpu.VMEM((4, D_MODEL, MOE_FF), BF16),
            pltpu.VMEM((2 * MOE_FF, D_MODEL), BF16),
            pltpu.SemaphoreType.DMA((2,)),
            pltpu.SemaphoreType.DMA((2,)),
        ],
    )
    return pl.pallas_call(
        _moe_kernel,
        grid_spec=grid_spec,
        out_shape=jax.ShapeDtypeStruct(((nslot + 2 * tm) * CHUNKS, LANES), F32),
        compiler_params=_params("arbitrary"),
        name="moe",
    )(sched, src3, src3, h2c_all, wr_hi, wr_lo, br, wg, wu, wd, wg, wu, wd)


def _post_moe_kernel(x1_ref, m_ref, mod_ref, lng_ref, lnb_ref, out_ref):
    m = _load_chunked(m_ref, x1_ref.shape[0], CHUNKS, 0)
    _, _, gate = _mod3(mod_ref[0])
    out_ref[...] = _layer_norm(ALPHA * x1_ref[...] + (1.0 + gate) * m, lng_ref[...], lnb_ref[...])


def _post_moe_call(x1, m_all, mod, lng, lnb, rows, first_block, steps_per_mod):
    mrows = mod.shape[1]
    return pl.pallas_call(
        _post_moe_kernel,
        grid=(x1.shape[0] // rows,),
        in_specs=[
            pl.BlockSpec((rows, D_MODEL), lambda i: (i, 0)),
            pl.BlockSpec((rows * CHUNKS, LANES), lambda i: (i + first_block, 0)),
            pl.BlockSpec((1, mrows, 3 * D_MODEL), lambda i: (i // steps_per_mod, 0, 0)),
            _full((1, D_MODEL)),
            _full((1, D_MODEL)),
        ],
        out_specs=pl.BlockSpec((rows, D_MODEL), lambda i: (i, 0)),
        out_shape=jax.ShapeDtypeStruct(x1.shape, F32),
        compiler_params=_params("arbitrary"),
        name="post_moe",
    )(x1, m_all, mod, lng, lnb)


def _moe_plan(route_p, route_s, cnt, layer):
    tm = MOE_TILE
    bucket = jnp.concatenate([route_p[:, 2], route_s[:, 2]]).astype(I32)
    rank = jnp.concatenate([route_p[:, 3], route_s[:, 3]]).astype(I32)
    nslot = bucket.shape[0]
    ntile = -(-nslot // tm) + N_BUCKETS
    counts = cnt[0, :N_BUCKETS].astype(I32)
    tiles_b = (counts + tm - 1) // tm
    tile_end = jnp.cumsum(tiles_b)
    tile_start = tile_end - tiles_b
    dest = tile_start[bucket] * tm + rank
    src = jnp.full((ntile * tm,), nslot, I32).at[dest].set(
        jnp.arange(nslot, dtype=I32), unique_indices=True)
    used = tile_end[-1]
    tile = jnp.arange(ntile, dtype=I32)
    ti = jnp.minimum(tile, used - 1)
    b_of = jnp.sum((ti[:, None] >= tile_end[None, :]).astype(I32), axis=1)
    grp, pair = b_of // N_PAIRS, b_of % N_PAIRS
    ex_a, ex_b = jnp.asarray(PAIR_A, I32)[pair], jnp.asarray(PAIR_B, I32)[pair]
    first_expert = layer * N_EXPERTS + grp * MOE_EXPERTS
    first_lane = MOE_GROUPS + grp * MOE_EXPERTS
    valid = (tile < used).astype(I32)
    first = valid * (ti == tile_start[b_of]).astype(I32)
    nxt = (tile + 1 < used).astype(I32)
    sched = jnp.stack([first_expert + ex_a, first_expert + ex_b, valid, first,
                       grp, first_lane + ex_a, first_lane + ex_b, nxt])
    return src.reshape(ntile, 1, tm), sched


def _router_weights(w_coarse, b_coarse, w_fine, b_fine):
    wf = jnp.transpose(w_fine, (1, 0, 2)).reshape(D_MODEL, N_EXPERTS)
    w = jnp.concatenate([w_coarse, wf], axis=1)
    w = jnp.pad(w, ((0, 0), (0, LANES - w.shape[1])))
    b = jnp.concatenate([b_coarse, b_fine.reshape(N_EXPERTS)])
    b = jnp.pad(b, (0, LANES - b.shape[0])).reshape(1, LANES)
    hi_lo = _hi_lo(w)
    return hi_lo[0], hi_lo[1], b


def _prep_params(ln_g, ln_b, w_in_even, w_a2, b_a, gla_norm_g, conv_w, conv_b, conv_ln_g,
                 conv_ln_b, w_out_even, w_grp_pool, pool_scale, w_out_odd, w_coarse, b_coarse,
                 w_fine, b_fine):
    p = {}
    w_in = w_in_even[0]
    o_q, o_k, o_v, o_g = 0, GLA_KW, 2 * GLA_KW, 2 * GLA_KW + GLA_VW
    o_a = o_g + GLA_VW
    o_u = o_a + GLA_RANK
    w_in_r = jnp.concatenate(
        [w_in[:, o_q:o_a], w_in[:, o_u:o_u + 2 * CONV_CH], w_in[:, o_a:o_u],
         jnp.zeros((D_MODEL, A_PAD - GLA_RANK), F32)], axis=1)
    p["w_in"] = _hi_lo(w_in_r)
    p["w_a2"] = _hi_lo(jnp.pad(w_a2[0], ((0, A_PAD - GLA_RANK), (0, 0))))
    p["b_a"] = b_a[0].reshape(1, GLA_KW)
    p["gng"] = gla_norm_g[0].reshape(1, GLA_VW)
    p["conv_w"] = jnp.pad(conv_w[0], ((0, CONV_W_ROWS - CONV_WIDTH), (0, 0)))
    p["conv_b"] = conv_b[0].reshape(1, CONV_CH)
    p["conv_ln_g"] = conv_ln_g[0].reshape(1, CONV_CH)
    p["conv_ln_b"] = conv_ln_b[0].reshape(1, CONV_CH)
    p["w_out0"] = _hi_lo(w_out_even[0])
    p["w_grp"] = w_grp_pool[0].astype(BF16)
    p["pool_scale"] = pool_scale[0].reshape(1, D_MODEL)
    p["w_out1"] = w_out_odd[0].astype(BF16)
    for layer in range(DEPTH):
        for j in range(2):
            p[f"ln_g{layer}{j}"] = ln_g[layer, j].reshape(1, D_MODEL)
            p[f"ln_b{layer}{j}"] = ln_b[layer, j].reshape(1, D_MODEL)
        hi, lo, b = _router_weights(w_coarse[layer], b_coarse[layer], w_fine[layer], b_fine[layer])
        p[f"wr_hi{layer}"], p[f"wr_lo{layer}"], p[f"br{layer}"] = hi, lo, b
    return p


def kernel(x_prompt, x_sample, state_gla, state_conv, state_pool, c_prompt, c_sample, w_ada, b_ada,
           ln_g, ln_b, w_in_even, w_a2, b_a, gla_norm_g, conv_w, conv_b, conv_ln_g, conv_ln_b,
           w_out_even, w_grp_pool, pool_scale, w_out_odd, w_coarse, b_coarse, w_fine, b_fine,
           w_gate, w_up, w_down):
    bsz, seq, _ = x_prompt.shape
    nsmp = x_sample.shape[0]
    ntok_p = bsz * seq
    assert seq % TILE_L == 0 and nsmp <= TILE_L and nsmp % SAMPLE_BLK == 0 and ntok_p % nsmp == 0
    p = _prep_params(ln_g, ln_b, w_in_even, w_a2, b_a, gla_norm_g, conv_w, conv_b, conv_ln_g,
                     conv_ln_b, w_out_even, w_grp_pool, pool_scale, w_out_odd, w_coarse, b_coarse,
                     w_fine, b_fine)
    wg = w_gate.reshape(DEPTH * N_EXPERTS, D_MODEL, MOE_FF)
    wu = w_up.reshape(DEPTH * N_EXPERTS, D_MODEL, MOE_FF)
    wd = w_down.reshape(DEPTH * N_EXPERTS, MOE_FF, D_MODEL)

    c_all = jnp.concatenate([c_prompt, c_sample], axis=0)
    mod = _ada_call(c_all, w_ada.reshape(2 * DEPTH, D_MODEL, 3 * D_MODEL),
                    b_ada.reshape(2 * DEPTH, 1, 3 * D_MODEL))
    mod_p = mod[:, :bsz].reshape(2 * DEPTH, bsz, 1, 3 * D_MODEL)
    mod_s = mod[:, bsz:]

    def moe(x1_p, x1_s, h2c_all, route_p, route_s, cnt, layer):
        src3, sched = _moe_plan(route_p, route_s, cnt, layer)
        m_all = _moe_call(sched, src3, h2c_all, p[f"wr_hi{layer}"], p[f"wr_lo{layer}"],
                          p[f"br{layer}"], wg, wu, wd)
        lng, lnb = p[f"ln_g{layer}1"], p[f"ln_b{layer}1"]
        yp = _post_moe_call(x1_p, m_all, mod_p[2 * layer + 1], lng, lnb, TILE_L, 0, seq // TILE_L)
        ys = _post_moe_call(x1_s, m_all, mod_s[2 * layer + 1][None], lng, lnb, nsmp,
                            ntok_p // nsmp, 1)
        return yp.reshape(bsz, seq, D_MODEL), ys

    xs0 = x_sample.reshape(nsmp, D_MODEL)
    cnt0 = jnp.zeros((1, LANES), F32)
    x1s, h2c_s, route_s, cnt, gla_s, conv_s = _mixer0_sample_call(
        xs0, mod_s[0], mod_s[1], state_gla, state_conv, cnt0, p)
    x1p, h2c_all, route_p, cnt, conv_p, gla_p = _mixer0_prompt_call(
        x_prompt, mod_p[0], mod_p[1], cnt, h2c_s, p)
    x2p, x2s = moe(x1p, x1s, h2c_all, route_p, route_s, cnt, 0)
    x3s, h2c_s, route_s, cnt, pool_s = _mixer1_sample_call(
        x2s, mod_s[2], mod_s[3], state_pool, cnt0, p)
    x3p, h2c_all, route_p, cnt, pool_p = _mixer1_prompt_call(
        x2p, mod_p[2], mod_p[3], cnt, h2c_s, p)
    x4p, x4s = moe(x3p, x3s, h2c_all, route_p, route_s, cnt, 1)
    return (x4p, x4s.reshape(nsmp, 1, D_MODEL), gla_p, conv_p, pool_p, gla_s, conv_s, pool_s)
```

```python
import jax
import jax.numpy as jnp
from jax import lax
from jax.experimental import pallas as pl
from jax.experimental.pallas import tpu as pltpu

F32 = jnp.float32
BF16 = jnp.bfloat16
I32 = jnp.int32

D_MODEL = 1024
GLA_HEADS = 4
GLA_DK = 64
GLA_DV = 128
GLA_KW = GLA_HEADS * GLA_DK
GLA_VW = GLA_HEADS * GLA_DV
GLA_RANK = 16
GLA_TAU = 16.0
CONV_CH = 512
CONV_WIDTH = 31
CONV_BUF = CONV_WIDTH - 1
POOL_WINDOWS = (2, 4, 8, 16)
POOL_GC = D_MODEL // len(POOL_WINDOWS)
POOL_BUF = max(POOL_WINDOWS) - 1
MOE_GROUPS = 4
MOE_EXPERTS = 4
N_EXPERTS = MOE_GROUPS * MOE_EXPERTS
N_PAIRS = MOE_EXPERTS * (MOE_EXPERTS - 1) // 2
N_BUCKETS = MOE_GROUPS * N_PAIRS
PAIR_A = (0, 0, 0, 1, 1, 2)
PAIR_B = (1, 2, 3, 2, 3, 3)
MOE_FF = 512
DEPTH = 2
ALPHA = (2 * DEPTH) ** 0.25
LN_EPS = 1e-5

LANES = 128
SUBLANES = 8
CHUNKS = D_MODEL // LANES
A_PAD = LANES
Z_WIDTH = 2 * GLA_KW + 2 * GLA_VW + 2 * CONV_CH + A_PAD
NEG_BIG = -1e30
VMEM_LIMIT = 56 * 1024 * 1024

TILE_L = 256
CONV_HALO = 32
CONV_W_ROWS = 32
POOL_HALO = 16
SAMPLE_BLK = 16
MOE_TILE = 256


def _dot(a, b):
    return jnp.dot(a, b, preferred_element_type=F32)


def _dot_nt(a, b):
    return lax.dot_general(a, b, (((1,), (1,)), ((), ())), preferred_element_type=F32)


def _split3(x):
    hi = x.astype(BF16)
    r1 = x - hi.astype(F32)
    mid = r1.astype(BF16)
    lo = (r1 - mid.astype(F32)).astype(BF16)
    return hi, mid, lo


def _split2(x):
    hi = x.astype(BF16)
    lo = (x - hi.astype(F32)).astype(BF16)
    return hi, lo


def _dot_w3(a, w_hi, w_lo):
    a_hi, a_lo = _split2(a)
    return _dot(a_hi, w_hi) + (_dot(a_lo, w_hi) + _dot(a_hi, w_lo))


def _lhs3(parts):
    hi, lo = parts
    return jnp.concatenate([hi, lo, hi], axis=1)


def _rhs3_rows(x):
    hi, lo = _split2(x)
    return jnp.concatenate([hi, hi, lo], axis=0)


def _rhs3_lanes(parts):
    hi, lo = parts
    return jnp.concatenate([hi, hi, lo], axis=1)


def _layer_norm(x, g, b):
    mu = jnp.mean(x, axis=-1, keepdims=True)
    xc = x - mu
    var = jnp.mean(xc * xc, axis=-1, keepdims=True)
    return xc * lax.rsqrt(var + LN_EPS) * g + b


def _standardize(x):
    mu = jnp.mean(x, axis=-1, keepdims=True)
    xc = x - mu
    var = jnp.mean(xc * xc, axis=-1, keepdims=True)
    return xc * lax.rsqrt(var + LN_EPS)


def _sigmoid(x):
    return 1.0 / (1.0 + jnp.exp(-x))


def _silu(x):
    return x * _sigmoid(x)


def _log_sigmoid(x):
    return jnp.minimum(x, 0.0) - jnp.log(1.0 + jnp.exp(-jnp.abs(x)))


def _mod3(mod):
    return mod[:, 0:D_MODEL], mod[:, D_MODEL:2 * D_MODEL], mod[:, 2 * D_MODEL:3 * D_MODEL]


def _full(shape):
    nd = len(shape)
    return pl.BlockSpec(shape, lambda *_: (0,) * nd)


def _const(shape):
    nd = len(shape)
    return pl.BlockSpec(shape, lambda *_: (0,) * nd, pipeline_mode=pl.Buffered(1))


def _hi_lo(w):
    bits = lax.bitcast_convert_type(w, jnp.uint32) & jnp.uint32(0xFFFF0000)
    hi = lax.bitcast_convert_type(bits, F32)
    return jnp.stack([hi.astype(BF16), (w - hi).astype(BF16)])


def _params(*sem):
    return pltpu.CompilerParams(dimension_semantics=sem, vmem_limit_bytes=VMEM_LIMIT)


def _ada_kernel(c_ref, w_ref, b_ref, o_ref):
    w_hi, w_lo = _split2(w_ref[0])
    o_ref[0] = _dot_w3(_silu(c_ref[...]), w_hi, w_lo) + b_ref[0]


def _ada_call(c_all, w_ada, b_ada):
    n = c_all.shape[0]
    ncomb = w_ada.shape[0]
    tn = D_MODEL
    return pl.pallas_call(
        _ada_kernel,
        grid=(ncomb, 3 * D_MODEL // tn),
        in_specs=[
            pl.BlockSpec((n, D_MODEL), lambda i, j: (0, 0)),
            pl.BlockSpec((1, D_MODEL, tn), lambda i, j: (i, 0, j)),
            pl.BlockSpec((1, 1, tn), lambda i, j: (i, 0, j)),
        ],
        out_specs=pl.BlockSpec((1, n, tn), lambda i, j: (i, 0, j)),
        out_shape=jax.ShapeDtypeStruct((ncomb, n, 3 * D_MODEL), F32),
        compiler_params=_params("arbitrary", "arbitrary"),
        name="ada",
    )(c_all, w_ada, b_ada)


def _route(logits, carry):
    rows = logits.shape[0]
    lane = lax.broadcasted_iota(I32, (rows, LANES), 1)
    lanef = lane.astype(F32)
    big = float(LANES)
    lc = jnp.where(lane < MOE_GROUPS, logits, NEG_BIG)
    mc = jnp.max(lc, axis=-1, keepdims=True)
    pg = 1.0 / jnp.sum(jnp.exp(lc - mc), axis=-1, keepdims=True)
    gidx = jnp.min(jnp.where(lc == mc, lanef, big), axis=-1, keepdims=True)
    lo = float(MOE_GROUPS) + gidx * float(MOE_EXPERTS)
    in_grp = (lanef >= lo) & (lanef < lo + float(MOE_EXPERTS))
    lf = jnp.where(in_grp, logits, NEG_BIG)
    m1 = jnp.max(lf, axis=-1, keepdims=True)
    i1 = jnp.min(jnp.where(lf == m1, lanef, big), axis=-1, keepdims=True)
    lf2 = jnp.where(lanef == i1, NEG_BIG, lf)
    m2 = jnp.max(lf2, axis=-1, keepdims=True)
    i2 = jnp.min(jnp.where(lf2 == m2, lanef, big), axis=-1, keepdims=True)
    e2 = jnp.exp(m2 - m1)
    w1 = pg / (1.0 + e2)
    w2 = pg * e2 / (1.0 + e2)
    ea = jnp.minimum(i1, i2) - lo
    eb = jnp.maximum(i1, i2) - lo
    w_a = jnp.where(i1 < i2, w1, w2)
    w_b = jnp.where(i1 < i2, w2, w1)
    bucket = gidx * float(N_PAIRS) + ea * (7.0 - ea) * 0.5 + eb - ea - 1.0
    onehot = jnp.where(lanef == bucket, 1.0, 0.0)
    r = lax.broadcasted_iota(I32, (rows, rows), 0)
    c = lax.broadcasted_iota(I32, (rows, rows), 1)
    before = _dot(jnp.where(r > c, 1.0, 0.0).astype(BF16), onehot.astype(BF16))
    rank = jnp.sum(onehot * (before + carry), axis=-1, keepdims=True)
    new_carry = carry + jnp.sum(onehot, axis=0, keepdims=True)
    route = (jnp.where(lane == 0, w_a, 0.0) + jnp.where(lane == 1, w_b, 0.0)
             + jnp.where(lane == 2, bucket, 0.0) + jnp.where(lane == 3, rank, 0.0))
    return route, new_carry


def _store_chunked(ref, x):
    rows = x.shape[0]
    for c in range(CHUNKS):
        ref[pl.ds(c, rows, stride=CHUNKS), :] = x[:, c * LANES:(c + 1) * LANES]


def _load_chunked(ref, rows, group, first):
    return jnp.concatenate(
        [ref[pl.ds(first + c, rows, stride=group), :] for c in range(CHUNKS)], axis=1)


def _router_logits(h2, wr_hi_ref, wr_lo_ref, br_ref):
    h_hi, h_lo = _split2(h2)
    wr_hi = wr_hi_ref[...]
    return _dot(h_hi, wr_hi) + _dot(h_lo, wr_hi) + _dot(h_hi, wr_lo_ref[...]) + br_ref[...]


def _post_mixer(x, m, gate, lng_ref, lnb_ref, mod_b, wr_hi_ref, wr_lo_ref, br_ref, carry,
                x1_ref, h2c_ref, route_ref):
    x1 = _layer_norm(ALPHA * x + (1.0 + gate) * m, lng_ref[...], lnb_ref[...])
    shift2, scale2, _ = _mod3(mod_b)
    h2 = x1 * (1.0 + scale2) + shift2
    pad = route_ref.shape[0] - x.shape[0]
    if pad:
        h2 = jnp.concatenate([h2, jnp.zeros((pad, D_MODEL), F32)], axis=0)
    route, carry = _route(_router_logits(h2, wr_hi_ref, wr_lo_ref, br_ref), carry)
    x1_ref[...] = x1
    _store_chunked(h2c_ref, h2)
    route_ref[...] = route
    return carry


def _token_out_specs(rows, slot_rows, x1_map, slot_map):
    return [pl.BlockSpec((rows, D_MODEL), x1_map),
            pl.BlockSpec((slot_rows * CHUNKS, LANES), slot_map),
            pl.BlockSpec((slot_rows, LANES), x1_map)]


def _token_out_shapes(ntok, nslot_h2, nslot_route):
    return [jax.ShapeDtypeStruct((ntok, D_MODEL), F32),
            jax.ShapeDtypeStruct((nslot_h2 * CHUNKS, LANES), F32),
            jax.ShapeDtypeStruct((nslot_route, LANES), F32)]


def _hold_rows(b, s):
    n, w = b.shape
    if s >= SUBLANES:
        pieces = []
        for p in range(n // (2 * s)):
            r = p * 2 * s + s - 1
            pieces.append(jnp.broadcast_to(b[r:r + 1, :], (2 * s, w)))
        return pieces[0] if len(pieces) == 1 else jnp.concatenate(pieces, axis=0)
    b3 = b.reshape(n // SUBLANES, SUBLANES, w)
    sub = lax.broadcasted_iota(I32, b3.shape, 1)

    def bc(r):
        return jnp.broadcast_to(b3[:, r:r + 1, :], b3.shape)

    out = bc(s - 1)
    for p in range(1, SUBLANES // (2 * s)):
        out = jnp.where(sub >= p * 2 * s, bc(p * 2 * s + s - 1), out)
    return out.reshape(n, w)


def _gla_tile(q, k, v, la, s_prev):
    n = q.shape[0]
    row = lax.broadcasted_iota(I32, (n, n), 0)
    col = lax.broadcasted_iota(I32, (n, n), 1)
    tri = jnp.where(row >= col, 1.0, 0.0).astype(BF16)
    hi, mid, lo = _split3(la)
    b = _dot(tri, hi) + _dot(tri, mid) + _dot(tri, lo)
    b_last = b[n - 1:n, :]
    kl = k * jnp.exp(b_last - b)

    rowi = lax.broadcasted_iota(I32, (n, GLA_KW), 0)
    levels = []
    s = n // 2
    while s >= 1:
        m = _hold_rows(b, s)
        second = (rowi & s) != 0
        qs = _split2(q * jnp.exp(jnp.where(second, b - m, NEG_BIG)))
        ks = _split2(k * jnp.exp(jnp.where(second, NEG_BIG, m - b)))
        levels.append((2 * s, qs, ks))
        s //= 2
    levels.append((1, _split2(q), _split2(k)))
    qe = _split2(q * jnp.exp(b))

    lane = lax.broadcasted_iota(I32, (n, LANES), 1)
    xor = row ^ col
    zero = jnp.zeros((), BF16)

    def head_part(parts, sl, hm):
        return tuple(jnp.where(hm, x[:, sl], zero) for x in parts)

    outs = []
    for h in range(GLA_HEADS):
        p = h // 2
        sl = slice(p * LANES, (p + 1) * LANES)
        hm = (lane < GLA_DK) if h % 2 == 0 else (lane >= GLA_DK)
        att = None
        for span, qs, ks in levels:
            a_s = _dot_nt(_lhs3(head_part(qs, sl, hm)), _rhs3_lanes((ks[0][:, sl], ks[1][:, sl])))
            att = a_s if att is None else jnp.where(xor < span, a_s, att)
        o_h = _dot(_lhs3(_split2(att)), _rhs3_rows(v[:, h * GLA_DV:(h + 1) * GLA_DV]))
        o_h = o_h + _dot(_lhs3(head_part(qe, sl, hm)), _rhs3_rows(s_prev[sl, :]))
        outs.append(o_h)

    decay = jnp.exp(b_last)
    upper = lax.broadcasted_iota(I32, (LANES, LANES), 0) < GLA_DK
    s_new = []
    for p in range(GLA_HEADS // 2):
        sl = slice(p * LANES, (p + 1) * LANES)
        kl_t = jnp.transpose(kl[:, sl])
        u = _dot(_lhs3(_split2(kl_t)), _rhs3_rows(v[:, p * 2 * GLA_DV:(p + 1) * 2 * GLA_DV]))
        upd = jnp.where(upper, u[:, 0:GLA_DV], u[:, GLA_DV:2 * GLA_DV])
        dcol = jnp.transpose(jnp.broadcast_to(decay[:, sl], (LANES, LANES)))
        s_new.append(dcol * s_prev[sl, :] + upd)
    return outs, jnp.concatenate(s_new, axis=0)


def _split_projection(z):
    c0 = 0
    q = z[:, c0:c0 + GLA_KW] * (GLA_DK ** -0.5); c0 += GLA_KW
    k = z[:, c0:c0 + GLA_KW]; c0 += GLA_KW
    v = z[:, c0:c0 + GLA_VW]; c0 += GLA_VW
    g = z[:, c0:c0 + GLA_VW]; c0 += GLA_VW
    ua = z[:, c0:c0 + CONV_CH]; c0 += CONV_CH
    ug = z[:, c0:c0 + CONV_CH]; c0 += CONV_CH
    a_lr = z[:, c0:c0 + A_PAD]
    return q, k, v, g, ua, ug, a_lr


def _mix0_project(o_heads, g, y, gng_ref, w_out_ref):
    sl = slice(GLA_VW, GLA_VW + CONV_CH)
    m = _dot_w3(y, w_out_ref[0, sl, :], w_out_ref[1, sl, :])
    for hd in range(GLA_HEADS):
        sl = slice(hd * GLA_DV, (hd + 1) * GLA_DV)
        o_h = _standardize(o_heads[hd]) * gng_ref[:, sl] * _silu(g[:, sl])
        m = m + _dot_w3(o_h, w_out_ref[0, sl, :], w_out_ref[1, sl, :])
    return m


def _mix0_inputs(h, w_in_ref, w_a2_ref, b_a_ref):
    z = _dot_w3(h, w_in_ref[0], w_in_ref[1])
    q, k, v, g, ua, ug, a_lr = _split_projection(z)
    la = _log_sigmoid(_dot_w3(a_lr, w_a2_ref[0], w_a2_ref[1]) + b_a_ref[...]) * (1.0 / GLA_TAU)
    return q, k, v, g, ua, ug, la


def _prompt_grid_specs(bsz, nt):
    ntile = bsz * nt
    seq_map = lambda b, t: (jnp.minimum(b, bsz - 1), 0, 0)
    x_map = lambda b, t: (jnp.minimum(b, bsz - 1), t, 0)
    tok_map = lambda b, t: (jnp.minimum(b * nt + t, ntile - 1), 0)
    slot_map = lambda b, t: (jnp.minimum(b * nt + t, ntile), 0)
    state_map = lambda b, t: (0, jnp.minimum(b, bsz - 1), 0, 0)
    return seq_map, x_map, tok_map, slot_map, state_map


def _mixer0_prompt_kernel(x_ref, moda_ref, modb_ref, cnt_in_ref, h2cs_ref, w_in_ref, w_a2_ref,
                          b_a_ref, gng_ref,
                          cw_ref, cb_ref, clg_ref, clb_ref, w_out_ref, lng_ref, lnb_ref,
                          wr_hi_ref, wr_lo_ref, br_ref,
                          x1_ref, h2c_ref, route_ref, cnt_ref, nconv_ref, ngla_ref,
                          s_ref, cbuf_ref, shift_ref, carry_ref):
    nseq = pl.num_programs(0) - 1

    @pl.when(pl.program_id(0) < nseq)
    def _():
        _mixer0_prompt_tile(x_ref, moda_ref, modb_ref, cnt_in_ref, w_in_ref, w_a2_ref, b_a_ref,
                            gng_ref, cw_ref, cb_ref, clg_ref, clb_ref, w_out_ref, lng_ref, lnb_ref,
                            wr_hi_ref, wr_lo_ref, br_ref, x1_ref, h2c_ref, route_ref, cnt_ref,
                            nconv_ref, ngla_ref, s_ref, cbuf_ref, shift_ref, carry_ref)

    @pl.when((pl.program_id(0) == nseq) & (pl.program_id(1) == 0))
    def _():
        h2c_ref[...] = h2cs_ref[...]


def _mixer0_prompt_tile(x_ref, moda_ref, modb_ref, cnt_in_ref, w_in_ref, w_a2_ref, b_a_ref,
                        gng_ref, cw_ref, cb_ref, clg_ref, clb_ref, w_out_ref, lng_ref, lnb_ref,
                        wr_hi_ref, wr_lo_ref, br_ref, x1_ref, h2c_ref, route_ref, cnt_ref,
                        nconv_ref, ngla_ref, s_ref, cbuf_ref, shift_ref, carry_ref):
    t = pl.program_id(1)
    n = x_ref.shape[1]

    @pl.when((t == 0) & (pl.program_id(0) == 0))
    def _():
        carry_ref[...] = cnt_in_ref[...]

    @pl.when(t == 0)
    def _():
        s_ref[...] = jnp.zeros_like(s_ref)
        cbuf_ref[0:CONV_HALO, :] = jnp.zeros((CONV_HALO, CONV_CH), F32)

    x = x_ref[0]
    shift, scale, gate = _mod3(moda_ref[0])
    h = x * (1.0 + scale) + shift
    q, k, v, g, ua, ug, la = _mix0_inputs(h, w_in_ref, w_a2_ref, b_a_ref)
    o_heads, s_new = _gla_tile(q, k, v, la, s_ref[...])
    s_ref[...] = s_new

    glu = ua * _sigmoid(ug)
    cbuf_ref[CONV_HALO:CONV_HALO + n, :] = glu
    acc = jnp.broadcast_to(cb_ref[...], (n, CONV_CH))
    base = CONV_HALO - CONV_BUF
    for r in range(SUBLANES):
        taps = range(r, CONV_WIDTH, SUBLANES)
        span = n + (len(taps) - 1) * SUBLANES
        if (base + r) % SUBLANES == 0:
            src, off = cbuf_ref, base + r
        else:
            shift_ref[r, 0:span, :] = cbuf_ref[base + r:base + r + span, :]
            src, off = shift_ref.at[r], 0
        for a, j in enumerate(taps):
            lo = off + a * SUBLANES
            acc = acc + cw_ref[j:j + 1, :] * src[lo:lo + n, :]
    y = _silu(_layer_norm(acc, clg_ref[...], clb_ref[...]))

    @pl.when(t == pl.num_programs(1) - 1)
    def _():
        nconv_ref[0, 0] = cbuf_ref[CONV_HALO + n - CONV_BUF:CONV_HALO + n, :]
        ngla_ref[0, 0] = s_new.reshape(GLA_HEADS, GLA_DK, GLA_DV)

    cbuf_ref[0:CONV_HALO, :] = cbuf_ref[n:n + CONV_HALO, :]

    m = _mix0_project(o_heads, g, y, gng_ref, w_out_ref)
    carry = _post_mixer(x, m, gate, lng_ref, lnb_ref, modb_ref[0], wr_hi_ref, wr_lo_ref, br_ref,
                        carry_ref[...], x1_ref, h2c_ref, route_ref)
    carry_ref[...] = carry
    cnt_ref[...] = carry


def _mixer0_prompt_call(x, mod_a, mod_b, cnt, h2c_s, p):
    bsz, seq, _ = x.shape
    tl = TILE_L
    nt = seq // tl
    seq_map, x_map, tok, slot, state_map = _prompt_grid_specs(bsz, nt)
    return pl.pallas_call(
        _mixer0_prompt_kernel,
        grid=(bsz + 1, nt),
        in_specs=[
            pl.BlockSpec((1, tl, D_MODEL), x_map),
            pl.BlockSpec((1, 1, 3 * D_MODEL), seq_map),
            pl.BlockSpec((1, 1, 3 * D_MODEL), seq_map),
            _full((1, LANES)),
            _full((tl * CHUNKS, LANES)),
            _const((2, D_MODEL, Z_WIDTH)),
            _const((2, A_PAD, GLA_KW)),
            _full((1, GLA_KW)),
            _full((1, GLA_VW)),
            _full((CONV_W_ROWS, CONV_CH)),
            _full((1, CONV_CH)),
            _full((1, CONV_CH)),
            _full((1, CONV_CH)),
            _const((2, GLA_VW + CONV_CH, D_MODEL)),
            _full((1, D_MODEL)),
            _full((1, D_MODEL)),
            _full((D_MODEL, LANES)),
            _full((D_MODEL, LANES)),
            _full((1, LANES)),
        ],
        out_specs=_token_out_specs(tl, tl, tok, slot) + [
            _full((1, LANES)),
            pl.BlockSpec((1, 1, CONV_BUF, CONV_CH), state_map),
            pl.BlockSpec((1, 1, GLA_HEADS, GLA_DK, GLA_DV), lambda b, t: state_map(b, t) + (0,)),
        ],
        out_shape=_token_out_shapes(bsz * seq, bsz * seq + tl, bsz * seq) + [
            jax.ShapeDtypeStruct((1, LANES), F32),
            jax.ShapeDtypeStruct((1, bsz, CONV_BUF, CONV_CH), F32),
            jax.ShapeDtypeStruct((1, bsz, GLA_HEADS, GLA_DK, GLA_DV), F32),
        ],
        scratch_shapes=[
            pltpu.VMEM((GLA_KW, GLA_DV), F32),
            pltpu.VMEM((CONV_HALO + tl, CONV_CH), F32),
            pltpu.VMEM((SUBLANES, CONV_HALO + tl, CONV_CH), F32),
            pltpu.VMEM((1, LANES), F32),
        ],
        compiler_params=_params("arbitrary", "arbitrary"),
        name="mixer0_prompt",
    )(x, mod_a, mod_b, cnt, h2c_s, p["w_in"], p["w_a2"], p["b_a"], p["gng"], p["conv_w"], p["conv_b"],
      p["conv_ln_g"], p["conv_ln_b"], p["w_out0"], p["ln_g00"], p["ln_b00"],
      p["wr_hi0"], p["wr_lo0"], p["br0"])


def _mixer0_sample_kernel(x_ref, moda_ref, modb_ref, sgla_ref, sconv_ref, cnt_in_ref,
                          w_in_ref, w_a2_ref,
                          b_a_ref, gng_ref, cw_ref, cb_ref, clg_ref, clb_ref, w_out_ref, lng_ref,
                          lnb_ref, wr_hi_ref, wr_lo_ref, br_ref,
                          x1_ref, h2c_ref, route_ref, cnt_ref, ngla_ref, nconv_ref,
                          zt_ref, v_ref, g_ref, glu_ref, o_ref, y_ref):
    i = pl.program_id(0)
    nb = sgla_ref.shape[1]
    ntok = x_ref.shape[0]

    @pl.when(i == 0)
    def _():
        shift, scale, _ = _mod3(moda_ref[...])
        h = x_ref[...] * (1.0 + scale) + shift
        q, k, v, g, ua, ug, la = _mix0_inputs(h, w_in_ref, w_a2_ref, b_a_ref)
        v_ref[...] = v
        g_ref[...] = g
        glu_ref[...] = ua * _sigmoid(ug)
        for j, val in enumerate((jnp.exp(la), k, q)):
            hi, lo = _split2(jnp.transpose(val))
            zt_ref[(2 * j) * GLA_KW:(2 * j + 1) * GLA_KW, :] = hi
            zt_ref[(2 * j + 1) * GLA_KW:(2 * j + 2) * GLA_KW, :] = lo

    tok_row = lax.broadcasted_iota(I32, (ntok, LANES), 0)
    blk = pl.ds(pl.multiple_of(i * nb, nb), nb)
    v_blk = v_ref[blk, :]
    glu_blk = glu_ref[blk, :]
    o_rows, y_rows = [], []
    for n in range(nb):
        onehot = jnp.where(tok_row == i * nb + n, 1.0, 0.0).astype(BF16)
        cols = _dot(zt_ref[...], onehot)
        a_col = cols[0:GLA_KW] + cols[GLA_KW:2 * GLA_KW]
        k_col = cols[2 * GLA_KW:3 * GLA_KW] + cols[3 * GLA_KW:4 * GLA_KW]
        q_col = cols[4 * GLA_KW:5 * GLA_KW] + cols[5 * GLA_KW:6 * GLA_KW]
        v_row = v_blk[n:n + 1, :]
        v_b = jnp.concatenate(
            [jnp.broadcast_to(v_row[:, h * GLA_DV:(h + 1) * GLA_DV], (GLA_DK, GLA_DV))
             for h in range(GLA_HEADS)], axis=0)
        s_old = sgla_ref[0, n].reshape(GLA_KW, GLA_DV)
        s_new = a_col * s_old + k_col * v_b
        ngla_ref[0, n] = s_new.reshape(GLA_HEADS, GLA_DK, GLA_DV)
        o4 = jnp.sum((q_col * s_new).reshape(GLA_HEADS, GLA_DK, GLA_DV), axis=1)
        o_rows.append(jnp.concatenate([o4[h:h + 1, :] for h in range(GLA_HEADS)], axis=1))
        glu_row = glu_blk[n:n + 1, :]
        past = sconv_ref[0, n]
        y_rows.append(jnp.sum(past * cw_ref[0:CONV_BUF, :], axis=0, keepdims=True)
                      + glu_row * cw_ref[CONV_BUF:CONV_WIDTH, :] + cb_ref[...])
        nconv_ref[0, n, 0:CONV_BUF - 1, :] = sconv_ref[0, n, 1:CONV_BUF, :]
        nconv_ref[0, n, CONV_BUF - 1:CONV_BUF, :] = glu_row
    o_ref[blk, :] = jnp.concatenate(o_rows, axis=0)
    y_ref[blk, :] = jnp.concatenate(y_rows, axis=0)

    @pl.when(i == pl.num_programs(0) - 1)
    def _():
        y = _silu(_layer_norm(y_ref[...], clg_ref[...], clb_ref[...]))
        o_heads = [o_ref[:, hd * GLA_DV:(hd + 1) * GLA_DV] for hd in range(GLA_HEADS)]
        m = _mix0_project(o_heads, g_ref[...], y, gng_ref, w_out_ref)
        _, _, gate = _mod3(moda_ref[...])
        cnt_ref[...] = _post_mixer(x_ref[...], m, gate, lng_ref, lnb_ref, modb_ref[...], wr_hi_ref,
                                   wr_lo_ref, br_ref, cnt_in_ref[...], x1_ref, h2c_ref, route_ref)


def _mixer0_sample_call(x, mod_a, mod_b, state_gla, state_conv, cnt, p):
    ntok = x.shape[0]
    nb = SAMPLE_BLK
    tok = lambda i: (0, 0)
    return pl.pallas_call(
        _mixer0_sample_kernel,
        grid=(ntok // nb,),
        in_specs=[
            _full((ntok, D_MODEL)),
            _full((ntok, 3 * D_MODEL)),
            _full((ntok, 3 * D_MODEL)),
            pl.BlockSpec((1, nb, GLA_HEADS, GLA_DK, GLA_DV), lambda i: (0, i, 0, 0, 0)),
            pl.BlockSpec((1, nb, CONV_BUF, CONV_CH), lambda i: (0, i, 0, 0)),
            _full((1, LANES)),
            _const((2, D_MODEL, Z_WIDTH)),
            _const((2, A_PAD, GLA_KW)),
            _full((1, GLA_KW)),
            _full((1, GLA_VW)),
            _full((CONV_W_ROWS, CONV_CH)),
            _full((1, CONV_CH)),
            _full((1, CONV_CH)),
            _full((1, CONV_CH)),
            _const((2, GLA_VW + CONV_CH, D_MODEL)),
            _full((1, D_MODEL)),
            _full((1, D_MODEL)),
            _full((D_MODEL, LANES)),
            _full((D_MODEL, LANES)),
            _full((1, LANES)),
        ],
        out_specs=_token_out_specs(ntok, TILE_L, tok, tok) + [
            _full((1, LANES)),
            pl.BlockSpec((1, nb, GLA_HEADS, GLA_DK, GLA_DV), lambda i: (0, i, 0, 0, 0)),
            pl.BlockSpec((1, nb, CONV_BUF, CONV_CH), lambda i: (0, i, 0, 0)),
        ],
        out_shape=_token_out_shapes(ntok, TILE_L, TILE_L) + [
            jax.ShapeDtypeStruct((1, LANES), F32),
            jax.ShapeDtypeStruct(state_gla.shape, F32),
            jax.ShapeDtypeStruct(state_conv.shape, F32),
        ],
        scratch_shapes=[
            pltpu.VMEM((6 * GLA_KW, ntok), BF16),
            pltpu.VMEM((ntok, GLA_VW), F32),
            pltpu.VMEM((ntok, GLA_VW), F32),
            pltpu.VMEM((ntok, CONV_CH), F32),
            pltpu.VMEM((ntok, GLA_VW), F32),
            pltpu.VMEM((ntok, CONV_CH), F32),
        ],
        compiler_params=_params("arbitrary"),
        name="mixer0_sample",
    )(x, mod_a, mod_b, state_gla, state_conv, cnt, p["w_in"], p["w_a2"], p["b_a"],
      p["gng"], p["conv_w"], p["conv_b"], p["conv_ln_g"], p["conv_ln_b"], p["w_out0"],
      p["ln_g00"], p["ln_b00"], p["wr_hi0"], p["wr_lo0"], p["br0"])


def _pool_project(pooled, h, wg_ref, ps_ref, w_out_ref):
    m = None
    for gi in range(len(POOL_WINDOWS)):
        sl = slice(gi * POOL_GC, (gi + 1) * POOL_GC)
        mixed = _dot((pooled[:, sl] - h[:, sl]).astype(BF16), wg_ref[gi]) * ps_ref[:, sl]
        part = _dot(mixed.astype(BF16), w_out_ref[sl, :])
        m = part if m is None else m + part
    return m


def _block_input(xin_ref, m_ref, mod_m, lnpg_ref, lnpb_ref):
    m = _load_chunked(m_ref, xin_ref.shape[0], CHUNKS, 0)
    _, _, gate = _mod3(mod_m)
    return _layer_norm(ALPHA * xin_ref[...] + (1.0 + gate) * m, lnpg_ref[...], lnpb_ref[...])


def _mixer1_prompt_kernel(xin_ref, m_ref, modm_ref, lnpg_ref, lnpb_ref, moda_ref, modb_ref,
                          cnt_in_ref, h2cs_ref, wg_ref, ps_ref,
                          w_out_ref, lng_ref, lnb_ref, wr_hi_ref, wr_lo_ref, br_ref,
                          x1_ref, h2c_ref, route_ref, cnt_ref, npool_ref,
                          pbuf_ref, carry_ref):
    nseq = pl.num_programs(0) - 1

    @pl.when(pl.program_id(0) < nseq)
    def _():
        x = _block_input(xin_ref, m_ref, modm_ref[0], lnpg_ref, lnpb_ref)
        _mixer1_prompt_tile(x, moda_ref, modb_ref, cnt_in_ref, wg_ref, ps_ref, w_out_ref,
                            lng_ref, lnb_ref, wr_hi_ref, wr_lo_ref, br_ref, x1_ref, h2c_ref,
                            route_ref, cnt_ref, npool_ref, pbuf_ref, carry_ref)

    @pl.when((pl.program_id(0) == nseq) & (pl.program_id(1) == 0))
    def _():
        h2c_ref[...] = h2cs_ref[...]


def _mixer1_prompt_tile(x, moda_ref, modb_ref, cnt_in_ref, wg_ref, ps_ref, w_out_ref,
                        lng_ref, lnb_ref, wr_hi_ref, wr_lo_ref, br_ref, x1_ref, h2c_ref,
                        route_ref, cnt_ref, npool_ref, pbuf_ref, carry_ref):
    t = pl.program_id(1)
    n = x.shape[0]

    @pl.when((t == 0) & (pl.program_id(0) == 0))
    def _():
        carry_ref[...] = cnt_in_ref[...]

    @pl.when(t == 0)
    def _():
        pbuf_ref[0:POOL_HALO, :] = jnp.zeros((POOL_HALO, D_MODEL), F32)

    shift, scale, gate = _mod3(moda_ref[0])
    h = x * (1.0 + scale) + shift
    pbuf_ref[POOL_HALO:POOL_HALO + n, :] = h

    cur = pbuf_ref[...]
    sums = []
    for gi, w in enumerate(POOL_WINDOWS):
        cur = cur + pltpu.roll(cur, w // 2, axis=0)
        sums.append(cur[POOL_HALO:, 0:POOL_GC])
        if gi + 1 < len(POOL_WINDOWS):
            cur = cur[:, POOL_GC:]
    pos = lax.broadcasted_iota(I32, (n, POOL_GC), 0) + t * n
    pooled = jnp.concatenate(
        [s / jnp.minimum(w, pos + 1).astype(F32) for s, w in zip(sums, POOL_WINDOWS)], axis=1)

    @pl.when(t == pl.num_programs(1) - 1)
    def _():
        npool_ref[0, 0] = pbuf_ref[POOL_HALO + n - POOL_BUF:POOL_HALO + n, :]

    pbuf_ref[0:POOL_HALO, :] = pbuf_ref[n:n + POOL_HALO, :]

    m = _pool_project(pooled, h, wg_ref, ps_ref, w_out_ref)
    carry = _post_mixer(x, m, gate, lng_ref, lnb_ref, modb_ref[0], wr_hi_ref, wr_lo_ref, br_ref,
                        carry_ref[...], x1_ref, h2c_ref, route_ref)
    carry_ref[...] = carry
    cnt_ref[...] = carry


def _mixer1_prompt_call(x1_prev, m_all, mod_m, lnpg, lnpb, mod_a, mod_b, cnt, h2c_s, bsz, p):
    seq = x1_prev.shape[0] // bsz
    tl = TILE_L
    nt = seq // tl
    seq_map, _, tok, slot, state_map = _prompt_grid_specs(bsz, nt)
    ng = len(POOL_WINDOWS)
    return pl.pallas_call(
        _mixer1_prompt_kernel,
        grid=(bsz + 1, nt),
        in_specs=[
            pl.BlockSpec((tl, D_MODEL), tok),
            pl.BlockSpec((tl * CHUNKS, LANES), tok),
            pl.BlockSpec((1, 1, 3 * D_MODEL), seq_map),
            _full((1, D_MODEL)),
            _full((1, D_MODEL)),
            pl.BlockSpec((1, 1, 3 * D_MODEL), seq_map),
            pl.BlockSpec((1, 1, 3 * D_MODEL), seq_map),
            _full((1, LANES)),
            _full((tl * CHUNKS, LANES)),
            _full((ng, POOL_GC, POOL_GC)),
            _full((1, D_MODEL)),
            _full((D_MODEL, D_MODEL)),
            _full((1, D_MODEL)),
            _full((1, D_MODEL)),
            _full((D_MODEL, LANES)),
            _full((D_MODEL, LANES)),
            _full((1, LANES)),
        ],
        out_specs=_token_out_specs(tl, tl, tok, slot) + [
            _full((1, LANES)),
            pl.BlockSpec((1, 1, POOL_BUF, D_MODEL), state_map),
        ],
        out_shape=_token_out_shapes(bsz * seq, bsz * seq + tl, bsz * seq) + [
            jax.ShapeDtypeStruct((1, LANES), F32),
            jax.ShapeDtypeStruct((1, bsz, POOL_BUF, D_MODEL), F32),
        ],
        scratch_shapes=[pltpu.VMEM((POOL_HALO + tl, D_MODEL), F32), pltpu.VMEM((1, LANES), F32)],
        compiler_params=_params("arbitrary", "arbitrary"),
        name="mixer1_prompt",
    )(x1_prev, m_all, mod_m, lnpg, lnpb, mod_a, mod_b, cnt, h2c_s,
      p["w_grp"], p["pool_scale"], p["w_out1"], p["ln_g10"], p["ln_b10"],
      p["wr_hi1"], p["wr_lo1"], p["br1"])


def _mixer1_sample_kernel(xin_ref, m_ref, modm_ref, lnpg_ref, lnpb_ref, moda_ref, modb_ref,
                          spool_ref, cnt_in_ref, wg_ref, ps_ref, w_out_ref,
                          lng_ref, lnb_ref, wr_hi_ref, wr_lo_ref, br_ref,
                          x1_ref, h2c_ref, route_ref, cnt_ref, npool_ref,
                          x_ref, h_ref, pooled_ref):
    i = pl.program_id(0)
    nb = spool_ref.shape[1]

    @pl.when(i == 0)
    def _():
        x_ref[...] = _block_input(xin_ref, m_ref, modm_ref[...], lnpg_ref, lnpb_ref)
        shift, scale, _ = _mod3(moda_ref[...])
        h_ref[...] = x_ref[...] * (1.0 + scale) + shift

    lane = lax.broadcasted_iota(I32, (POOL_BUF, D_MODEL), 1)
    rowi = lax.broadcasted_iota(I32, (POOL_BUF, D_MODEL), 0)
    first = jnp.zeros((POOL_BUF, D_MODEL), I32)
    lane1 = lax.broadcasted_iota(I32, (1, D_MODEL), 1)
    inv_w = jnp.zeros((1, D_MODEL), F32)
    for gi, w in enumerate(POOL_WINDOWS):
        in_g = (lane >= gi * POOL_GC) & (lane < (gi + 1) * POOL_GC)
        first = jnp.where(in_g, POOL_BUF - (w - 1), first)
        in_g1 = (lane1 >= gi * POOL_GC) & (lane1 < (gi + 1) * POOL_GC)
        inv_w = jnp.where(in_g1, 1.0 / w, inv_w)
    keep = rowi >= first

    blk = pl.ds(pl.multiple_of(i * nb, nb), nb)
    h_blk = h_ref[blk, :]
    rows = []
    for n in range(nb):
        h_row = h_blk[n:n + 1, :]
        past = spool_ref[0, n]
        total = jnp.sum(jnp.where(keep, past, 0.0), axis=0, keepdims=True) + h_row
        rows.append(total * inv_w)
        npool_ref[0, n, 0:POOL_BUF - 1, :] = spool_ref[0, n, 1:POOL_BUF, :]
        npool_ref[0, n, POOL_BUF - 1:POOL_BUF, :] = h_row
    pooled_ref[blk, :] = jnp.concatenate(rows, axis=0)

    @pl.when(i == pl.num_programs(0) - 1)
    def _():
        m = _pool_project(pooled_ref[...], h_ref[...], wg_ref, ps_ref, w_out_ref)
        _, _, gate = _mod3(moda_ref[...])
        cnt_ref[...] = _post_mixer(x_ref[...], m, gate, lng_ref, lnb_ref, modb_ref[...], wr_hi_ref,
                                   wr_lo_ref, br_ref, cnt_in_ref[...], x1_ref, h2c_ref, route_ref)


def _mixer1_sample_call(x1_prev, m_all, first_block, mod_m, lnpg, lnpb, mod_a, mod_b, state_pool,
                        cnt, p):
    ntok = x1_prev.shape[0]
    nb = SAMPLE_BLK
    ng = len(POOL_WINDOWS)
    tok = lambda i: (0, 0)
    return pl.pallas_call(
        _mixer1_sample_kernel,
        grid=(ntok // nb,),
        in_specs=[
            _full((ntok, D_MODEL)),
            pl.BlockSpec((ntok * CHUNKS, LANES), lambda i: (first_block, 0)),
            _full((ntok, 3 * D_MODEL)),
            _full((1, D_MODEL)),
            _full((1, D_MODEL)),
            _full((ntok, 3 * D_MODEL)),
            _full((ntok, 3 * D_MODEL)),
            pl.BlockSpec((1, nb, POOL_BUF, D_MODEL), lambda i: (0, i, 0, 0)),
            _full((1, LANES)),
            _full((ng, POOL_GC, POOL_GC)),
            _full((1, D_MODEL)),
            _full((D_MODEL, D_MODEL)),
            _full((1, D_MODEL)),
            _full((1, D_MODEL)),
            _full((D_MODEL, LANES)),
            _full((D_MODEL, LANES)),
            _full((1, LANES)),
        ],
        out_specs=_token_out_specs(ntok, TILE_L, tok, tok) + [
            _full((1, LANES)),
            pl.BlockSpec((1, nb, POOL_BUF, D_MODEL), lambda i: (0, i, 0, 0)),
        ],
        out_shape=_token_out_shapes(ntok, TILE_L, TILE_L) + [
            jax.ShapeDtypeStruct((1, LANES), F32),
            jax.ShapeDtypeStruct(state_pool.shape, F32),
        ],
        scratch_shapes=[pltpu.VMEM((ntok, D_MODEL), F32)] * 3,
        compiler_params=_params("arbitrary"),
        name="mixer1_sample",
    )(x1_prev, m_all, mod_m, lnpg, lnpb, mod_a, mod_b, state_pool, cnt,
      p["w_grp"], p["pool_scale"], p["w_out1"],
      p["ln_g10"], p["ln_b10"], p["wr_hi1"], p["wr_lo1"], p["br1"])


S_EA, S_EB, S_VALID, S_FIRST, S_LANE_G, S_LANE_A, S_LANE_B, S_NEXT = range(8)


def _moe_kernel(sched_ref, srcp_ref, src_ref, srcn_ref, h_any, wr_hi_ref, wr_lo_ref, br_ref,
                wga_ref, wua_ref, wda_ref, wgb_ref, wub_ref, wdb_ref,
                m_any,
                xbuf, ybuf, wgu_ref, wd_ref, gsem, ssem):
    i = pl.program_id(0)
    tm = src_ref.shape[2]
    nslot = h_any.shape[0] // CHUNKS
    slot = i % 2

    def gather(ids_ref, s):
        for j in range(tm):
            tok = ids_ref[0, 0, j]
            tok = jnp.where(tok >= nslot, 0, tok)
            pltpu.make_async_copy(
                h_any.at[pl.ds(pl.multiple_of(tok * CHUNKS, CHUNKS), CHUNKS)],
                xbuf.at[s, pl.ds(j * CHUNKS, CHUNKS)], gsem.at[s]).start()

    def scatter(ids_ref, s):
        for j in range(tm):
            tok = ids_ref[0, 0, j]
            row = jnp.where(tok >= nslot, nslot + s * tm + j, tok)
            pltpu.make_async_copy(
                ybuf.at[s, pl.ds(j * CHUNKS, CHUNKS)],
                m_any.at[pl.ds(pl.multiple_of(row * CHUNKS, CHUNKS), CHUNKS)], ssem.at[s]).start()

    def wait_gather(s):
        pltpu.make_async_copy(h_any.at[pl.ds(0, tm * CHUNKS)], xbuf.at[s], gsem.at[s]).wait()

    def wait_scatter(s):
        pltpu.make_async_copy(ybuf.at[s], m_any.at[pl.ds(0, tm * CHUNKS)], ssem.at[s]).wait()

    @pl.when(i == 0)
    def _():
        ybuf[...] = jnp.zeros_like(ybuf)
        for s in range(2):
            spare = m_any.at[pl.ds((nslot + s * tm) * CHUNKS, tm * CHUNKS)]
            pltpu.make_async_copy(ybuf.at[s], spare, ssem.at[s]).start()
        for s in range(2):
            wait_scatter(s)
        gather(src_ref, 0)

    @pl.when(sched_ref[S_FIRST, i] == 1)
    def _():
        for e, (wg, wu, wd) in enumerate(((wga_ref, wua_ref, wda_ref), (wgb_ref, wub_ref, wdb_ref))):
            wgu_ref[2 * e] = wg[0].astype(BF16)
            wgu_ref[2 * e + 1] = wu[0].astype(BF16)
            wd_ref[e * MOE_FF:(e + 1) * MOE_FF, :] = wd[0].astype(BF16)

    @pl.when(sched_ref[S_VALID, i] == 1)
    def _():
        wait_gather(slot)
        gather(srcn_ref, 1 - slot)
        scatter(srcp_ref, 1 - slot)
        x = _load_chunked(xbuf.at[slot], tm, CHUNKS, 0)
        logits = _router_logits(x, wr_hi_ref, wr_lo_ref, br_ref)
        lane = lax.broadcasted_iota(I32, logits.shape, 1)

        def pick(row):
            return jnp.sum(jnp.where(lane == sched_ref[row, i], logits, 0.0), axis=-1, keepdims=True)

        l_g, l_a, l_b = pick(S_LANE_G), pick(S_LANE_A), pick(S_LANE_B)
        p_g = 1.0 / jnp.sum(jnp.where(lane < MOE_GROUPS, jnp.exp(logits - l_g), 0.0),
                            axis=-1, keepdims=True)
        w_ab = (p_g / (1.0 + jnp.exp(l_b - l_a)), p_g / (1.0 + jnp.exp(l_a - l_b)))
        xb = x.astype(BF16)
        hid = [(_silu(_dot(xb, wgu_ref[2 * e])) * _dot(xb, wgu_ref[2 * e + 1]) * w_ab[e]).astype(BF16)
               for e in range(2)]
        y = _dot(jnp.concatenate(hid, axis=1), wd_ref[...])

        @pl.when(i > 0)
        def _():
            wait_scatter(slot)

        _store_chunked(ybuf.at[slot], y)

        @pl.when(sched_ref[S_NEXT, i] == 0)
        def _():
            scatter(src_ref, slot)
            wait_scatter(1 - slot)
            wait_scatter(slot)
            wait_gather(1 - slot)


def _moe_call(sched, src3, h2c_all, wr_hi, wr_lo, br, wg, wu, wd):
    ntile = sched.shape[1]
    tm = MOE_TILE
    nslot = h2c_all.shape[0] // CHUNKS
    ea = lambda i, s: (s[S_EA, i], 0, 0)
    eb = lambda i, s: (s[S_EB, i], 0, 0)
    const2 = lambda i, s: (0, 0)
    smem = pltpu.SMEM
    grid_spec = pltpu.PrefetchScalarGridSpec(
        num_scalar_prefetch=1,
        grid=(ntile,),
        in_specs=[
            pl.BlockSpec((1, 1, tm), lambda i, s: (i, 0, 0), memory_space=smem),
            pl.BlockSpec((1, 1, tm), lambda i, s: (i + 1, 0, 0), memory_space=smem),
            pl.BlockSpec((1, 1, tm), lambda i, s: (jnp.minimum(i + 2, ntile), 0, 0),
                         memory_space=smem),
            pl.BlockSpec(memory_space=pl.ANY),
            pl.BlockSpec((D_MODEL, LANES), const2),
            pl.BlockSpec((D_MODEL, LANES), const2),
            pl.BlockSpec((1, LANES), const2),
            pl.BlockSpec((1, D_MODEL, MOE_FF), ea),
            pl.BlockSpec((1, D_MODEL, MOE_FF), ea),
            pl.BlockSpec((1, MOE_FF, D_MODEL), ea),
            pl.BlockSpec((1, D_MODEL, MOE_FF), eb),
            pl.BlockSpec((1, D_MODEL, MOE_FF), eb),
            pl.BlockSpec((1, MOE_FF, D_MODEL), eb),
        ],
        out_specs=pl.BlockSpec(memory_space=pl.ANY),
        scratch_shapes=[
            pltpu.VMEM((2, tm * CHUNKS, LANES), F32),
            pltpu.VMEM((2, tm * CHUNKS, LANES), F32),
            pltpu.VMEM((4, D_MODEL, MOE_FF), BF16),
            pltpu.VMEM((2 * MOE_FF, D_MODEL), BF16),
            pltpu.SemaphoreType.DMA((2,)),
            pltpu.SemaphoreType.DMA((2,)),
        ],
    )
    return pl.pallas_call(
        _moe_kernel,
        grid_spec=grid_spec,
        out_shape=jax.ShapeDtypeStruct(((nslot + 2 * tm) * CHUNKS, LANES), F32),
        compiler_params=_params("arbitrary"),
        name="moe",
    )(sched, src3, src3, src3, h2c_all, wr_hi, wr_lo, br, wg, wu, wd, wg, wu, wd)


def _post_moe_kernel(x1_ref, m_ref, mod_ref, lng_ref, lnb_ref, out_ref):
    m = _load_chunked(m_ref, x1_ref.shape[0], CHUNKS, 0)
    _, _, gate = _mod3(mod_ref[0])
    out_ref[...] = _layer_norm(ALPHA * x1_ref[...] + (1.0 + gate) * m, lng_ref[...], lnb_ref[...])


def _post_moe_call(x1, m_all, mod, lng, lnb, rows, first_block, steps_per_mod):
    mrows = mod.shape[1]
    return pl.pallas_call(
        _post_moe_kernel,
        grid=(x1.shape[0] // rows,),
        in_specs=[
            pl.BlockSpec((rows, D_MODEL), lambda i: (i, 0)),
            pl.BlockSpec((rows * CHUNKS, LANES), lambda i: (i + first_block, 0)),
            pl.BlockSpec((1, mrows, 3 * D_MODEL), lambda i: (i // steps_per_mod, 0, 0)),
            _full((1, D_MODEL)),
            _full((1, D_MODEL)),
        ],
        out_specs=pl.BlockSpec((rows, D_MODEL), lambda i: (i, 0)),
        out_shape=jax.ShapeDtypeStruct(x1.shape, F32),
        compiler_params=_params("arbitrary"),
        name="post_moe",
    )(x1, m_all, mod, lng, lnb)


def _moe_plan(route_p, route_s, cnt, layer):
    tm = MOE_TILE
    bucket = jnp.concatenate([route_p[:, 2], route_s[:, 2]]).astype(I32)
    rank = jnp.concatenate([route_p[:, 3], route_s[:, 3]]).astype(I32)
    nslot = bucket.shape[0]
    ntile = -(-nslot // tm) + N_BUCKETS
    counts = cnt[0, :N_BUCKETS].astype(I32)
    tiles_b = (counts + tm - 1) // tm
    tile_end = jnp.cumsum(tiles_b)
    tile_start = tile_end - tiles_b
    dest = tile_start[bucket] * tm + rank
    src = jnp.full(((ntile + 1) * tm,), nslot, I32).at[dest + tm].set(
        jnp.arange(nslot, dtype=I32), unique_indices=True)
    used = tile_end[-1]
    tile = jnp.arange(ntile, dtype=I32)
    ti = jnp.minimum(tile, used - 1)
    b_of = jnp.sum((ti[:, None] >= tile_end[None, :]).astype(I32), axis=1)
    grp, pair = b_of // N_PAIRS, b_of % N_PAIRS
    ex_a, ex_b = jnp.asarray(PAIR_A, I32)[pair], jnp.asarray(PAIR_B, I32)[pair]
    first_expert = layer * N_EXPERTS + grp * MOE_EXPERTS
    first_lane = MOE_GROUPS + grp * MOE_EXPERTS
    valid = (tile < used).astype(I32)
    first = valid * (ti == tile_start[b_of]).astype(I32)
    nxt = (tile + 1 < used).astype(I32)
    sched = jnp.stack([first_expert + ex_a, first_expert + ex_b, valid, first,
                       grp, first_lane + ex_a, first_lane + ex_b, nxt])
    return src.reshape(ntile + 1, 1, tm), sched


def _router_weights(w_coarse, b_coarse, w_fine, b_fine):
    wf = jnp.transpose(w_fine, (1, 0, 2)).reshape(D_MODEL, N_EXPERTS)
    w = jnp.concatenate([w_coarse, wf], axis=1)
    w = jnp.pad(w, ((0, 0), (0, LANES - w.shape[1])))
    b = jnp.concatenate([b_coarse, b_fine.reshape(N_EXPERTS)])
    b = jnp.pad(b, (0, LANES - b.shape[0])).reshape(1, LANES)
    hi_lo = _hi_lo(w)
    return hi_lo[0], hi_lo[1], b


def _prep_params(ln_g, ln_b, w_in_even, w_a2, b_a, gla_norm_g, conv_w, conv_b, conv_ln_g,
                 conv_ln_b, w_out_even, w_grp_pool, pool_scale, w_out_odd, w_coarse, b_coarse,
                 w_fine, b_fine):
    p = {}
    w_in = w_in_even[0]
    o_q, o_k, o_v, o_g = 0, GLA_KW, 2 * GLA_KW, 2 * GLA_KW + GLA_VW
    o_a = o_g + GLA_VW
    o_u = o_a + GLA_RANK
    w_in_r = jnp.concatenate(
        [w_in[:, o_q:o_a], w_in[:, o_u:o_u + 2 * CONV_CH], w_in[:, o_a:o_u],
         jnp.zeros((D_MODEL, A_PAD - GLA_RANK), F32)], axis=1)
    p["w_in"] = _hi_lo(w_in_r)
    p["w_a2"] = _hi_lo(jnp.pad(w_a2[0], ((0, A_PAD - GLA_RANK), (0, 0))))
    p["b_a"] = b_a[0].reshape(1, GLA_KW)
    p["gng"] = gla_norm_g[0].reshape(1, GLA_VW)
    p["conv_w"] = jnp.pad(conv_w[0], ((0, CONV_W_ROWS - CONV_WIDTH), (0, 0)))
    p["conv_b"] = conv_b[0].reshape(1, CONV_CH)
    p["conv_ln_g"] = conv_ln_g[0].reshape(1, CONV_CH)
    p["conv_ln_b"] = conv_ln_b[0].reshape(1, CONV_CH)
    p["w_out0"] = _hi_lo(w_out_even[0])
    p["w_grp"] = w_grp_pool[0].astype(BF16)
    p["pool_scale"] = pool_scale[0].reshape(1, D_MODEL)
    p["w_out1"] = w_out_odd[0].astype(BF16)
    for layer in range(DEPTH):
        for j in range(2):
            p[f"ln_g{layer}{j}"] = ln_g[layer, j].reshape(1, D_MODEL)
            p[f"ln_b{layer}{j}"] = ln_b[layer, j].reshape(1, D_MODEL)
        hi, lo, b = _router_weights(w_coarse[layer], b_coarse[layer], w_fine[layer], b_fine[layer])
        p[f"wr_hi{layer}"], p[f"wr_lo{layer}"], p[f"br{layer}"] = hi, lo, b
    return p


def kernel(x_prompt, x_sample, state_gla, state_conv, state_pool, c_prompt, c_sample, w_ada, b_ada,
           ln_g, ln_b, w_in_even, w_a2, b_a, gla_norm_g, conv_w, conv_b, conv_ln_g, conv_ln_b,
           w_out_even, w_grp_pool, pool_scale, w_out_odd, w_coarse, b_coarse, w_fine, b_fine,
           w_gate, w_up, w_down):
    bsz, seq, _ = x_prompt.shape
    nsmp = x_sample.shape[0]
    ntok_p = bsz * seq
    assert seq % TILE_L == 0 and nsmp <= TILE_L and nsmp % SAMPLE_BLK == 0 and ntok_p % nsmp == 0
    p = _prep_params(ln_g, ln_b, w_in_even, w_a2, b_a, gla_norm_g, conv_w, conv_b, conv_ln_g,
                     conv_ln_b, w_out_even, w_grp_pool, pool_scale, w_out_odd, w_coarse, b_coarse,
                     w_fine, b_fine)
    wg = w_gate.reshape(DEPTH * N_EXPERTS, D_MODEL, MOE_FF)
    wu = w_up.reshape(DEPTH * N_EXPERTS, D_MODEL, MOE_FF)
    wd = w_down.reshape(DEPTH * N_EXPERTS, MOE_FF, D_MODEL)

    c_all = jnp.concatenate([c_prompt, c_sample], axis=0)
    mod = _ada_call(c_all, w_ada.reshape(2 * DEPTH, D_MODEL, 3 * D_MODEL),
                    b_ada.reshape(2 * DEPTH, 1, 3 * D_MODEL))
    mod_p = mod[:, :bsz].reshape(2 * DEPTH, bsz, 1, 3 * D_MODEL)
    mod_s = mod[:, bsz:]

    def moe(h2c_all, route_p, route_s, cnt, layer):
        src3, sched = _moe_plan(route_p, route_s, cnt, layer)
        return _moe_call(sched, src3, h2c_all, p[f"wr_hi{layer}"], p[f"wr_lo{layer}"],
                         p[f"br{layer}"], wg, wu, wd)

    xs0 = x_sample.reshape(nsmp, D_MODEL)
    cnt0 = jnp.zeros((1, LANES), F32)
    x1s, h2c_s, route_s, cnt, gla_s, conv_s = _mixer0_sample_call(
        xs0, mod_s[0], mod_s[1], state_gla, state_conv, cnt0, p)
    x1p, h2c_all, route_p, cnt, conv_p, gla_p = _mixer0_prompt_call(
        x_prompt, mod_p[0], mod_p[1], cnt, h2c_s, p)
    m_all = moe(h2c_all, route_p, route_s, cnt, 0)
    sample_block = ntok_p // nsmp
    x3s, h2c_s, route_s, cnt, pool_s = _mixer1_sample_call(
        x1s, m_all, sample_block, mod_s[1], p["ln_g01"], p["ln_b01"], mod_s[2], mod_s[3],
        state_pool, cnt0, p)
    x3p, h2c_all, route_p, cnt, pool_p = _mixer1_prompt_call(
        x1p, m_all, mod_p[1], p["ln_g01"], p["ln_b01"], mod_p[2], mod_p[3], cnt, h2c_s, bsz, p)
    m_all = moe(h2c_all, route_p, route_s, cnt, 1)
    x4p = _post_moe_call(x3p, m_all, mod_p[3], p["ln_g11"], p["ln_b11"], TILE_L, 0,
                         seq // TILE_L).reshape(bsz, seq, D_MODEL)
    x4s = _post_moe_call(x3s, m_all, mod_s[3][None], p["ln_g11"], p["ln_b11"], nsmp,
                         sample_block, 1)
    return (x4p, x4s.reshape(nsmp, 1, D_MODEL), gla_p, conv_p, pool_p, gla_s, conv_s, pool_s)
```

```python
import jax
import jax.numpy as jnp
from jax import lax
from jax.experimental import pallas as pl
from jax.experimental.pallas import tpu as pltpu

F32 = jnp.float32
BF16 = jnp.bfloat16
I32 = jnp.int32

D_MODEL = 1024
GLA_HEADS = 4
GLA_DK = 64
GLA_DV = 128
GLA_KW = GLA_HEADS * GLA_DK
GLA_VW = GLA_HEADS * GLA_DV
GLA_RANK = 16
GLA_TAU = 16.0
CONV_CH = 512
CONV_WIDTH = 31
CONV_BUF = CONV_WIDTH - 1
POOL_WINDOWS = (2, 4, 8, 16)
POOL_GC = D_MODEL // len(POOL_WINDOWS)
POOL_BUF = max(POOL_WINDOWS) - 1
MOE_GROUPS = 4
MOE_EXPERTS = 4
N_EXPERTS = MOE_GROUPS * MOE_EXPERTS
N_PAIRS = MOE_EXPERTS * (MOE_EXPERTS - 1) // 2
N_BUCKETS = MOE_GROUPS * N_PAIRS
PAIR_A = (0, 0, 0, 1, 1, 2)
PAIR_B = (1, 2, 3, 2, 3, 3)
MOE_FF = 512
DEPTH = 2
ALPHA = (2 * DEPTH) ** 0.25
LN_EPS = 1e-5

LANES = 128
SUBLANES = 8
CHUNKS = D_MODEL // LANES
A_PAD = LANES
Z_WIDTH = 2 * GLA_KW + 2 * GLA_VW + 2 * CONV_CH + A_PAD
NEG_BIG = -1e30
VMEM_LIMIT = 56 * 1024 * 1024

TILE_L = 256
CONV_HALO = 32
CONV_W_ROWS = 32
POOL_HALO = 16
SAMPLE_BLK = 16
MOE_TILE = 256


def _dot(a, b):
    return jnp.dot(a, b, preferred_element_type=F32)


def _dot_nt(a, b):
    return lax.dot_general(a, b, (((1,), (1,)), ((), ())), preferred_element_type=F32)


def _split3(x):
    hi = x.astype(BF16)
    r1 = x - hi.astype(F32)
    mid = r1.astype(BF16)
    lo = (r1 - mid.astype(F32)).astype(BF16)
    return hi, mid, lo


def _split2(x):
    hi = x.astype(BF16)
    lo = (x - hi.astype(F32)).astype(BF16)
    return hi, lo


def _dot_w3(a, w_hi, w_lo):
    a_hi, a_lo = _split2(a)
    return _dot(a_hi, w_hi) + (_dot(a_lo, w_hi) + _dot(a_hi, w_lo))


def _lhs3(parts):
    hi, lo = parts
    return jnp.concatenate([hi, lo, hi], axis=1)


def _rhs3_rows(x):
    hi, lo = _split2(x)
    return jnp.concatenate([hi, hi, lo], axis=0)


def _rhs3_lanes(parts):
    hi, lo = parts
    return jnp.concatenate([hi, hi, lo], axis=1)


def _layer_norm(x, g, b):
    mu = jnp.mean(x, axis=-1, keepdims=True)
    xc = x - mu
    var = jnp.mean(xc * xc, axis=-1, keepdims=True)
    return xc * lax.rsqrt(var + LN_EPS) * g + b


def _standardize(x):
    mu = jnp.mean(x, axis=-1, keepdims=True)
    xc = x - mu
    var = jnp.mean(xc * xc, axis=-1, keepdims=True)
    return xc * lax.rsqrt(var + LN_EPS)


def _sigmoid(x):
    return 1.0 / (1.0 + jnp.exp(-x))


def _silu(x):
    return x * _sigmoid(x)


def _log_sigmoid(x):
    return jnp.minimum(x, 0.0) - jnp.log(1.0 + jnp.exp(-jnp.abs(x)))


def _mod3(mod):
    return mod[:, 0:D_MODEL], mod[:, D_MODEL:2 * D_MODEL], mod[:, 2 * D_MODEL:3 * D_MODEL]


def _full(shape):
    nd = len(shape)
    return pl.BlockSpec(shape, lambda *_: (0,) * nd)


def _const(shape):
    nd = len(shape)
    return pl.BlockSpec(shape, lambda *_: (0,) * nd, pipeline_mode=pl.Buffered(1))


def _hi_lo(w):
    bits = lax.bitcast_convert_type(w, jnp.uint32) & jnp.uint32(0xFFFF0000)
    hi = lax.bitcast_convert_type(bits, F32)
    return jnp.stack([hi.astype(BF16), (w - hi).astype(BF16)])


def _params(*sem):
    return pltpu.CompilerParams(dimension_semantics=sem, vmem_limit_bytes=VMEM_LIMIT)


def _ada_kernel(c_ref, w_ref, b_ref, o_ref):
    w_hi, w_lo = _split2(w_ref[0])
    o_ref[0] = _dot_w3(_silu(c_ref[...]), w_hi, w_lo) + b_ref[0]


def _ada_call(c_all, w_ada, b_ada):
    n = c_all.shape[0]
    ncomb = w_ada.shape[0]
    tn = D_MODEL
    return pl.pallas_call(
        _ada_kernel,
        grid=(ncomb, 3 * D_MODEL // tn),
        in_specs=[
            pl.BlockSpec((n, D_MODEL), lambda i, j: (0, 0)),
            pl.BlockSpec((1, D_MODEL, tn), lambda i, j: (i, 0, j)),
            pl.BlockSpec((1, 1, tn), lambda i, j: (i, 0, j)),
        ],
        out_specs=pl.BlockSpec((1, n, tn), lambda i, j: (i, 0, j)),
        out_shape=jax.ShapeDtypeStruct((ncomb, n, 3 * D_MODEL), F32),
        compiler_params=_params("arbitrary", "arbitrary"),
        name="ada",
    )(c_all, w_ada, b_ada)


def _route(logits, carry):
    rows = logits.shape[0]
    lane = lax.broadcasted_iota(I32, (rows, LANES), 1)
    lanef = lane.astype(F32)
    big = float(LANES)
    lc = jnp.where(lane < MOE_GROUPS, logits, NEG_BIG)
    mc = jnp.max(lc, axis=-1, keepdims=True)
    pg = 1.0 / jnp.sum(jnp.exp(lc - mc), axis=-1, keepdims=True)
    gidx = jnp.min(jnp.where(lc == mc, lanef, big), axis=-1, keepdims=True)
    lo = float(MOE_GROUPS) + gidx * float(MOE_EXPERTS)
    in_grp = (lanef >= lo) & (lanef < lo + float(MOE_EXPERTS))
    lf = jnp.where(in_grp, logits, NEG_BIG)
    m1 = jnp.max(lf, axis=-1, keepdims=True)
    i1 = jnp.min(jnp.where(lf == m1, lanef, big), axis=-1, keepdims=True)
    lf2 = jnp.where(lanef == i1, NEG_BIG, lf)
    m2 = jnp.max(lf2, axis=-1, keepdims=True)
    i2 = jnp.min(jnp.where(lf2 == m2, lanef, big), axis=-1, keepdims=True)
    e2 = jnp.exp(m2 - m1)
    w1 = pg / (1.0 + e2)
    w2 = pg * e2 / (1.0 + e2)
    ea = jnp.minimum(i1, i2) - lo
    eb = jnp.maximum(i1, i2) - lo
    w_a = jnp.where(i1 < i2, w1, w2)
    w_b = jnp.where(i1 < i2, w2, w1)
    bucket = gidx * float(N_PAIRS) + ea * (7.0 - ea) * 0.5 + eb - ea - 1.0
    onehot = jnp.where(lanef == bucket, 1.0, 0.0)
    r = lax.broadcasted_iota(I32, (rows, rows), 0)
    c = lax.broadcasted_iota(I32, (rows, rows), 1)
    before = _dot(jnp.where(r > c, 1.0, 0.0).astype(BF16), onehot.astype(BF16))
    rank = jnp.sum(onehot * (before + carry), axis=-1, keepdims=True)
    new_carry = carry + jnp.sum(onehot, axis=0, keepdims=True)
    route = (jnp.where(lane == 0, w_a, 0.0) + jnp.where(lane == 1, w_b, 0.0)
             + jnp.where(lane == 2, bucket, 0.0) + jnp.where(lane == 3, rank, 0.0))
    return route, new_carry


def _store_chunked(ref, x):
    rows = x.shape[0]
    for c in range(CHUNKS):
        ref[pl.ds(c, rows, stride=CHUNKS), :] = x[:, c * LANES:(c + 1) * LANES]


def _load_chunked(ref, rows, group, first):
    return jnp.concatenate(
        [ref[pl.ds(first + c, rows, stride=group), :] for c in range(CHUNKS)], axis=1)


def _router_logits(h2, wr_hi_ref, wr_lo_ref, br_ref):
    h_hi, h_lo = _split2(h2)
    wr_hi = wr_hi_ref[...]
    return _dot(h_hi, wr_hi) + _dot(h_lo, wr_hi) + _dot(h_hi, wr_lo_ref[...]) + br_ref[...]


def _post_mixer(x, m, gate, lng_ref, lnb_ref, mod_b, wr_hi_ref, wr_lo_ref, br_ref, carry,
                x1_ref, h2c_ref, route_ref):
    x1 = _layer_norm(ALPHA * x + (1.0 + gate) * m, lng_ref[...], lnb_ref[...])
    shift2, scale2, _ = _mod3(mod_b)
    h2 = x1 * (1.0 + scale2) + shift2
    pad = route_ref.shape[0] - x.shape[0]
    if pad:
        h2 = jnp.concatenate([h2, jnp.zeros((pad, D_MODEL), F32)], axis=0)
    route, carry = _route(_router_logits(h2, wr_hi_ref, wr_lo_ref, br_ref), carry)
    x1_ref[...] = x1
    _store_chunked(h2c_ref, h2)
    route_ref[...] = route
    return carry


def _token_out_specs(rows, slot_rows, x1_map, slot_map):
    return [pl.BlockSpec((rows, D_MODEL), x1_map),
            pl.BlockSpec((slot_rows * CHUNKS, LANES), slot_map),
            pl.BlockSpec((slot_rows, LANES), x1_map)]


def _token_out_shapes(ntok, nslot_h2, nslot_route):
    return [jax.ShapeDtypeStruct((ntok, D_MODEL), F32),
            jax.ShapeDtypeStruct((nslot_h2 * CHUNKS, LANES), F32),
            jax.ShapeDtypeStruct((nslot_route, LANES), F32)]


def _hold_rows(b, s):
    n, w = b.shape
    if s >= SUBLANES:
        pieces = []
        for p in range(n // (2 * s)):
            r = p * 2 * s + s - 1
            pieces.append(jnp.broadcast_to(b[r:r + 1, :], (2 * s, w)))
        return pieces[0] if len(pieces) == 1 else jnp.concatenate(pieces, axis=0)
    b3 = b.reshape(n // SUBLANES, SUBLANES, w)
    sub = lax.broadcasted_iota(I32, b3.shape, 1)

    def bc(r):
        return jnp.broadcast_to(b3[:, r:r + 1, :], b3.shape)

    out = bc(s - 1)
    for p in range(1, SUBLANES // (2 * s)):
        out = jnp.where(sub >= p * 2 * s, bc(p * 2 * s + s - 1), out)
    return out.reshape(n, w)


def _gla_tile(q, k, v, la, s_prev):
    n = q.shape[0]
    row = lax.broadcasted_iota(I32, (n, n), 0)
    col = lax.broadcasted_iota(I32, (n, n), 1)
    tri = jnp.where(row >= col, 1.0, 0.0).astype(BF16)
    hi, mid, lo = _split3(la)
    b = _dot(tri, hi) + _dot(tri, mid) + _dot(tri, lo)
    b_last = b[n - 1:n, :]
    kl = k * jnp.exp(b_last - b)

    rowi = lax.broadcasted_iota(I32, (n, GLA_KW), 0)
    levels = []
    s = n // 2
    while s >= 1:
        m = _hold_rows(b, s)
        second = (rowi & s) != 0
        qs = _split2(q * jnp.exp(jnp.where(second, b - m, NEG_BIG)))
        ks = _split2(k * jnp.exp(jnp.where(second, NEG_BIG, m - b)))
        levels.append((2 * s, qs, ks))
        s //= 2
    levels.append((1, _split2(q), _split2(k)))
    qe = _split2(q * jnp.exp(b))

    lane = lax.broadcasted_iota(I32, (n, LANES), 1)
    xor = row ^ col
    zero = jnp.zeros((), BF16)

    def head_part(parts, sl, hm):
        return tuple(jnp.where(hm, x[:, sl], zero) for x in parts)

    outs = []
    for h in range(GLA_HEADS):
        p = h // 2
        sl = slice(p * LANES, (p + 1) * LANES)
        hm = (lane < GLA_DK) if h % 2 == 0 else (lane >= GLA_DK)
        att = None
        for span, qs, ks in levels:
            a_s = _dot_nt(_lhs3(head_part(qs, sl, hm)), _rhs3_lanes((ks[0][:, sl], ks[1][:, sl])))
            att = a_s if att is None else jnp.where(xor < span, a_s, att)
        o_h = _dot(_lhs3(_split2(att)), _rhs3_rows(v[:, h * GLA_DV:(h + 1) * GLA_DV]))
        o_h = o_h + _dot(_lhs3(head_part(qe, sl, hm)), _rhs3_rows(s_prev[sl, :]))
        outs.append(o_h)

    decay = jnp.exp(b_last)
    upper = lax.broadcasted_iota(I32, (LANES, LANES), 0) < GLA_DK
    s_new = []
    for p in range(GLA_HEADS // 2):
        sl = slice(p * LANES, (p + 1) * LANES)
        kl_t = jnp.transpose(kl[:, sl])
        u = _dot(_lhs3(_split2(kl_t)), _rhs3_rows(v[:, p * 2 * GLA_DV:(p + 1) * 2 * GLA_DV]))
        upd = jnp.where(upper, u[:, 0:GLA_DV], u[:, GLA_DV:2 * GLA_DV])
        dcol = jnp.transpose(jnp.broadcast_to(decay[:, sl], (LANES, LANES)))
        s_new.append(dcol * s_prev[sl, :] + upd)
    return outs, jnp.concatenate(s_new, axis=0)


def _split_projection(z):
    c0 = 0
    q = z[:, c0:c0 + GLA_KW] * (GLA_DK ** -0.5); c0 += GLA_KW
    k = z[:, c0:c0 + GLA_KW]; c0 += GLA_KW
    v = z[:, c0:c0 + GLA_VW]; c0 += GLA_VW
    g = z[:, c0:c0 + GLA_VW]; c0 += GLA_VW
    ua = z[:, c0:c0 + CONV_CH]; c0 += CONV_CH
    ug = z[:, c0:c0 + CONV_CH]; c0 += CONV_CH
    a_lr = z[:, c0:c0 + A_PAD]
    return q, k, v, g, ua, ug, a_lr


def _mix0_project(o_heads, g, y, gng_ref, w_out_ref):
    sl = slice(GLA_VW, GLA_VW + CONV_CH)
    m = _dot_w3(y, w_out_ref[0, sl, :], w_out_ref[1, sl, :])
    for hd in range(GLA_HEADS):
        sl = slice(hd * GLA_DV, (hd + 1) * GLA_DV)
        o_h = _standardize(o_heads[hd]) * gng_ref[:, sl] * _silu(g[:, sl])
        m = m + _dot_w3(o_h, w_out_ref[0, sl, :], w_out_ref[1, sl, :])
    return m


def _mix0_inputs(h, w_in_ref, w_a2_ref, b_a_ref):
    z = _dot_w3(h, w_in_ref[0], w_in_ref[1])
    q, k, v, g, ua, ug, a_lr = _split_projection(z)
    la = _log_sigmoid(_dot_w3(a_lr, w_a2_ref[0], w_a2_ref[1]) + b_a_ref[...]) * (1.0 / GLA_TAU)
    return q, k, v, g, ua, ug, la


def _prompt_grid_specs(bsz, nt):
    ntile = bsz * nt
    seq_map = lambda b, t: (jnp.minimum(b, bsz - 1), 0, 0)
    x_map = lambda b, t: (jnp.minimum(b, bsz - 1), t, 0)
    tok_map = lambda b, t: (jnp.minimum(b * nt + t, ntile - 1), 0)
    slot_map = lambda b, t: (jnp.minimum(b * nt + t, ntile), 0)
    state_map = lambda b, t: (0, jnp.minimum(b, bsz - 1), 0, 0)
    return seq_map, x_map, tok_map, slot_map, state_map


def _mixer0_prompt_kernel(x_ref, moda_ref, modb_ref, cnt_in_ref, h2cs_ref, w_in_ref, w_a2_ref,
                          b_a_ref, gng_ref,
                          cw_ref, cb_ref, clg_ref, clb_ref, w_out_ref, lng_ref, lnb_ref,
                          wr_hi_ref, wr_lo_ref, br_ref,
                          x1_ref, h2c_ref, route_ref, cnt_ref, nconv_ref, ngla_ref,
                          s_ref, cbuf_ref, shift_ref, carry_ref):
    nseq = pl.num_programs(0) - 1

    @pl.when(pl.program_id(0) < nseq)
    def _():
        _mixer0_prompt_tile(x_ref, moda_ref, modb_ref, cnt_in_ref, w_in_ref, w_a2_ref, b_a_ref,
                            gng_ref, cw_ref, cb_ref, clg_ref, clb_ref, w_out_ref, lng_ref, lnb_ref,
                            wr_hi_ref, wr_lo_ref, br_ref, x1_ref, h2c_ref, route_ref, cnt_ref,
                            nconv_ref, ngla_ref, s_ref, cbuf_ref, shift_ref, carry_ref)

    @pl.when((pl.program_id(0) == nseq) & (pl.program_id(1) == 0))
    def _():
        h2c_ref[...] = h2cs_ref[...]


def _mixer0_prompt_tile(x_ref, moda_ref, modb_ref, cnt_in_ref, w_in_ref, w_a2_ref, b_a_ref,
                        gng_ref, cw_ref, cb_ref, clg_ref, clb_ref, w_out_ref, lng_ref, lnb_ref,
                        wr_hi_ref, wr_lo_ref, br_ref, x1_ref, h2c_ref, route_ref, cnt_ref,
                        nconv_ref, ngla_ref, s_ref, cbuf_ref, shift_ref, carry_ref):
    t = pl.program_id(1)
    n = x_ref.shape[1]

    @pl.when((t == 0) & (pl.program_id(0) == 0))
    def _():
        carry_ref[...] = cnt_in_ref[...]

    @pl.when(t == 0)
    def _():
        s_ref[...] = jnp.zeros_like(s_ref)
        cbuf_ref[0:CONV_HALO, :] = jnp.zeros((CONV_HALO, CONV_CH), F32)

    x = x_ref[0]
    shift, scale, gate = _mod3(moda_ref[0])
    h = x * (1.0 + scale) + shift
    q, k, v, g, ua, ug, la = _mix0_inputs(h, w_in_ref, w_a2_ref, b_a_ref)
    o_heads, s_new = _gla_tile(q, k, v, la, s_ref[...])
    s_ref[...] = s_new

    glu = ua * _sigmoid(ug)
    cbuf_ref[CONV_HALO:CONV_HALO + n, :] = glu
    acc = jnp.broadcast_to(cb_ref[...], (n, CONV_CH))
    base = CONV_HALO - CONV_BUF
    for r in range(SUBLANES):
        taps = range(r, CONV_WIDTH, SUBLANES)
        span = n + (len(taps) - 1) * SUBLANES
        if (base + r) % SUBLANES == 0:
            src, off = cbuf_ref, base + r
        else:
            shift_ref[r, 0:span, :] = cbuf_ref[base + r:base + r + span, :]
            src, off = shift_ref.at[r], 0
        for a, j in enumerate(taps):
            lo = off + a * SUBLANES
            acc = acc + cw_ref[j:j + 1, :] * src[lo:lo + n, :]
    y = _silu(_layer_norm(acc, clg_ref[...], clb_ref[...]))

    @pl.when(t == pl.num_programs(1) - 1)
    def _():
        nconv_ref[0, 0] = cbuf_ref[CONV_HALO + n - CONV_BUF:CONV_HALO + n, :]
        ngla_ref[0, 0] = s_new.reshape(GLA_HEADS, GLA_DK, GLA_DV)

    cbuf_ref[0:CONV_HALO, :] = cbuf_ref[n:n + CONV_HALO, :]

    m = _mix0_project(o_heads, g, y, gng_ref, w_out_ref)
    carry = _post_mixer(x, m, gate, lng_ref, lnb_ref, modb_ref[0], wr_hi_ref, wr_lo_ref, br_ref,
                        carry_ref[...], x1_ref, h2c_ref, route_ref)
    carry_ref[...] = carry
    cnt_ref[...] = carry


def _mixer0_prompt_call(x, mod_a, mod_b, cnt, h2c_s, p):
    bsz, seq, _ = x.shape
    tl = TILE_L
    nt = seq // tl
    seq_map, x_map, tok, slot, state_map = _prompt_grid_specs(bsz, nt)
    return pl.pallas_call(
        _mixer0_prompt_kernel,
        grid=(bsz + 1, nt),
        in_specs=[
            pl.BlockSpec((1, tl, D_MODEL), x_map),
            pl.BlockSpec((1, 1, 3 * D_MODEL), seq_map),
            pl.BlockSpec((1, 1, 3 * D_MODEL), seq_map),
            _full((1, LANES)),
            _full((tl * CHUNKS, LANES)),
            _const((2, D_MODEL, Z_WIDTH)),
            _const((2, A_PAD, GLA_KW)),
            _full((1, GLA_KW)),
            _full((1, GLA_VW)),
            _full((CONV_W_ROWS, CONV_CH)),
            _full((1, CONV_CH)),
            _full((1, CONV_CH)),
            _full((1, CONV_CH)),
            _const((2, GLA_VW + CONV_CH, D_MODEL)),
            _full((1, D_MODEL)),
            _full((1, D_MODEL)),
            _full((D_MODEL, LANES)),
            _full((D_MODEL, LANES)),
            _full((1, LANES)),
        ],
        out_specs=_token_out_specs(tl, tl, tok, slot) + [
            _full((1, LANES)),
            pl.BlockSpec((1, 1, CONV_BUF, CONV_CH), state_map),
            pl.BlockSpec((1, 1, GLA_HEADS, GLA_DK, GLA_DV), lambda b, t: state_map(b, t) + (0,)),
        ],
        out_shape=_token_out_shapes(bsz * seq, bsz * seq + tl, bsz * seq) + [
            jax.ShapeDtypeStruct((1, LANES), F32),
            jax.ShapeDtypeStruct((1, bsz, CONV_BUF, CONV_CH), F32),
            jax.ShapeDtypeStruct((1, bsz, GLA_HEADS, GLA_DK, GLA_DV), F32),
        ],
        scratch_shapes=[
            pltpu.VMEM((GLA_KW, GLA_DV), F32),
            pltpu.VMEM((CONV_HALO + tl, CONV_CH), F32),
            pltpu.VMEM((SUBLANES, CONV_HALO + tl, CONV_CH), F32),
            pltpu.VMEM((1, LANES), F32),
        ],
        compiler_params=_params("arbitrary", "arbitrary"),
        name="mixer0_prompt",
    )(x, mod_a, mod_b, cnt, h2c_s, p["w_in"], p["w_a2"], p["b_a"], p["gng"], p["conv_w"], p["conv_b"],
      p["conv_ln_g"], p["conv_ln_b"], p["w_out0"], p["ln_g00"], p["ln_b00"],
      p["wr_hi0"], p["wr_lo0"], p["br0"])


def _mixer0_sample_kernel(x_ref, moda_ref, modb_ref, sgla_ref, sconv_ref, cnt_in_ref,
                          w_in_ref, w_a2_ref,
                          b_a_ref, gng_ref, cw_ref, cb_ref, clg_ref, clb_ref, w_out_ref, lng_ref,
                          lnb_ref, wr_hi_ref, wr_lo_ref, br_ref,
                          x1_ref, h2c_ref, route_ref, cnt_ref, ngla_ref, nconv_ref,
                          zt_ref, v_ref, g_ref, glu_ref, o_ref, y_ref):
    i = pl.program_id(0)
    nb = sgla_ref.shape[1]
    ntok = x_ref.shape[0]

    @pl.when(i == 0)
    def _():
        shift, scale, _ = _mod3(moda_ref[...])
        h = x_ref[...] * (1.0 + scale) + shift
        q, k, v, g, ua, ug, la = _mix0_inputs(h, w_in_ref, w_a2_ref, b_a_ref)
        v_ref[...] = v
        g_ref[...] = g
        glu_ref[...] = ua * _sigmoid(ug)
        for j, val in enumerate((jnp.exp(la), k, q)):
            hi, lo = _split2(jnp.transpose(val))
            zt_ref[(2 * j) * GLA_KW:(2 * j + 1) * GLA_KW, :] = hi
            zt_ref[(2 * j + 1) * GLA_KW:(2 * j + 2) * GLA_KW, :] = lo

    tok_row = lax.broadcasted_iota(I32, (ntok, LANES), 0)
    blk = pl.ds(pl.multiple_of(i * nb, nb), nb)
    v_blk = v_ref[blk, :]
    glu_blk = glu_ref[blk, :]
    o_rows, y_rows = [], []
    for n in range(nb):
        onehot = jnp.where(tok_row == i * nb + n, 1.0, 0.0).astype(BF16)
        cols = _dot(zt_ref[...], onehot)
        a_col = cols[0:GLA_KW] + cols[GLA_KW:2 * GLA_KW]
        k_col = cols[2 * GLA_KW:3 * GLA_KW] + cols[3 * GLA_KW:4 * GLA_KW]
        q_col = cols[4 * GLA_KW:5 * GLA_KW] + cols[5 * GLA_KW:6 * GLA_KW]
        v_row = v_blk[n:n + 1, :]
        v_b = jnp.concatenate(
            [jnp.broadcast_to(v_row[:, h * GLA_DV:(h + 1) * GLA_DV], (GLA_DK, GLA_DV))
             for h in range(GLA_HEADS)], axis=0)
        s_old = sgla_ref[0, n].reshape(GLA_KW, GLA_DV)
        s_new = a_col * s_old + k_col * v_b
        ngla_ref[0, n] = s_new.reshape(GLA_HEADS, GLA_DK, GLA_DV)
        o4 = jnp.sum((q_col * s_new).reshape(GLA_HEADS, GLA_DK, GLA_DV), axis=1)
        o_rows.append(jnp.concatenate([o4[h:h + 1, :] for h in range(GLA_HEADS)], axis=1))
        glu_row = glu_blk[n:n + 1, :]
        past = sconv_ref[0, n]
        y_rows.append(jnp.sum(past * cw_ref[0:CONV_BUF, :], axis=0, keepdims=True)
                      + glu_row * cw_ref[CONV_BUF:CONV_WIDTH, :] + cb_ref[...])
        nconv_ref[0, n, 0:CONV_BUF - 1, :] = sconv_ref[0, n, 1:CONV_BUF, :]
        nconv_ref[0, n, CONV_BUF - 1:CONV_BUF, :] = glu_row
    o_ref[blk, :] = jnp.concatenate(o_rows, axis=0)
    y_ref[blk, :] = jnp.concatenate(y_rows, axis=0)

    @pl.when(i == pl.num_programs(0) - 1)
    def _():
        y = _silu(_layer_norm(y_ref[...], clg_ref[...], clb_ref[...]))
        o_heads = [o_ref[:, hd * GLA_DV:(hd + 1) * GLA_DV] for hd in range(GLA_HEADS)]
        m = _mix0_project(o_heads, g_ref[...], y, gng_ref, w_out_ref)
        _, _, gate = _mod3(moda_ref[...])
        cnt_ref[...] = _post_mixer(x_ref[...], m, gate, lng_ref, lnb_ref, modb_ref[...], wr_hi_ref,
                                   wr_lo_ref, br_ref, cnt_in_ref[...], x1_ref, h2c_ref, route_ref)


def _mixer0_sample_call(x, mod_a, mod_b, state_gla, state_conv, cnt, p):
    ntok = x.shape[0]
    nb = SAMPLE_BLK
    tok = lambda i: (0, 0)
    return pl.pallas_call(
        _mixer0_sample_kernel,
        grid=(ntok // nb,),
        in_specs=[
            _full((ntok, D_MODEL)),
            _full((ntok, 3 * D_MODEL)),
            _full((ntok, 3 * D_MODEL)),
            pl.BlockSpec((1, nb, GLA_HEADS, GLA_DK, GLA_DV), lambda i: (0, i, 0, 0, 0)),
            pl.BlockSpec((1, nb, CONV_BUF, CONV_CH), lambda i: (0, i, 0, 0)),
            _full((1, LANES)),
            _const((2, D_MODEL, Z_WIDTH)),
            _const((2, A_PAD, GLA_KW)),
            _full((1, GLA_KW)),
            _full((1, GLA_VW)),
            _full((CONV_W_ROWS, CONV_CH)),
            _full((1, CONV_CH)),
            _full((1, CONV_CH)),
            _full((1, CONV_CH)),
            _const((2, GLA_VW + CONV_CH, D_MODEL)),
            _full((1, D_MODEL)),
            _full((1, D_MODEL)),
            _full((D_MODEL, LANES)),
            _full((D_MODEL, LANES)),
            _full((1, LANES)),
        ],
        out_specs=_token_out_specs(ntok, TILE_L, tok, tok) + [
            _full((1, LANES)),
            pl.BlockSpec((1, nb, GLA_HEADS, GLA_DK, GLA_DV), lambda i: (0, i, 0, 0, 0)),
            pl.BlockSpec((1, nb, CONV_BUF, CONV_CH), lambda i: (0, i, 0, 0)),
        ],
        out_shape=_token_out_shapes(ntok, TILE_L, TILE_L) + [
            jax.ShapeDtypeStruct((1, LANES), F32),
            jax.ShapeDtypeStruct(state_gla.shape, F32),
            jax.ShapeDtypeStruct(state_conv.shape, F32),
        ],
        scratch_shapes=[
            pltpu.VMEM((6 * GLA_KW, ntok), BF16),
            pltpu.VMEM((ntok, GLA_VW), F32),
            pltpu.VMEM((ntok, GLA_VW), F32),
            pltpu.VMEM((ntok, CONV_CH), F32),
            pltpu.VMEM((ntok, GLA_VW), F32),
            pltpu.VMEM((ntok, CONV_CH), F32),
        ],
        compiler_params=_params("arbitrary"),
        name="mixer0_sample",
    )(x, mod_a, mod_b, state_gla, state_conv, cnt, p["w_in"], p["w_a2"], p["b_a"],
      p["gng"], p["conv_w"], p["conv_b"], p["conv_ln_g"], p["conv_ln_b"], p["w_out0"],
      p["ln_g00"], p["ln_b00"], p["wr_hi0"], p["wr_lo0"], p["br0"])


def _pool_project(pooled, h, wg_ref, ps_ref, w_out_ref):
    m = None
    for gi in range(len(POOL_WINDOWS)):
        sl = slice(gi * POOL_GC, (gi + 1) * POOL_GC)
        mixed = _dot((pooled[:, sl] - h[:, sl]).astype(BF16), wg_ref[gi]) * ps_ref[:, sl]
        part = _dot(mixed.astype(BF16), w_out_ref[sl, :])
        m = part if m is None else m + part
    return m


def _block_input(xin_ref, m_ref, mod_m, lnpg_ref, lnpb_ref):
    m = _load_chunked(m_ref, xin_ref.shape[0], CHUNKS, 0)
    _, _, gate = _mod3(mod_m)
    return _layer_norm(ALPHA * xin_ref[...] + (1.0 + gate) * m, lnpg_ref[...], lnpb_ref[...])


def _mixer1_prompt_kernel(xin_ref, m_ref, modm_ref, lnpg_ref, lnpb_ref, moda_ref, modb_ref,
                          cnt_in_ref, h2cs_ref, wg_ref, ps_ref,
                          w_out_ref, lng_ref, lnb_ref, wr_hi_ref, wr_lo_ref, br_ref,
                          x1_ref, h2c_ref, route_ref, cnt_ref, npool_ref,
                          pbuf_ref, carry_ref):
    nseq = pl.num_programs(0) - 1

    @pl.when(pl.program_id(0) < nseq)
    def _():
        x = _block_input(xin_ref, m_ref, modm_ref[0], lnpg_ref, lnpb_ref)
        _mixer1_prompt_tile(x, moda_ref, modb_ref, cnt_in_ref, wg_ref, ps_ref, w_out_ref,
                            lng_ref, lnb_ref, wr_hi_ref, wr_lo_ref, br_ref, x1_ref, h2c_ref,
                            route_ref, cnt_ref, npool_ref, pbuf_ref, carry_ref)

    @pl.when((pl.program_id(0) == nseq) & (pl.program_id(1) == 0))
    def _():
        h2c_ref[...] = h2cs_ref[...]


def _mixer1_prompt_tile(x, moda_ref, modb_ref, cnt_in_ref, wg_ref, ps_ref, w_out_ref,
                        lng_ref, lnb_ref, wr_hi_ref, wr_lo_ref, br_ref, x1_ref, h2c_ref,
                        route_ref, cnt_ref, npool_ref, pbuf_ref, carry_ref):
    t = pl.program_id(1)
    n = x.shape[0]

    @pl.when((t == 0) & (pl.program_id(0) == 0))
    def _():
        carry_ref[...] = cnt_in_ref[...]

    @pl.when(t == 0)
    def _():
        pbuf_ref[0:POOL_HALO, :] = jnp.zeros((POOL_HALO, D_MODEL), F32)

    shift, scale, gate = _mod3(moda_ref[0])
    h = x * (1.0 + scale) + shift
    pbuf_ref[POOL_HALO:POOL_HALO + n, :] = h

    cur = pbuf_ref[...]
    sums = []
    for gi, w in enumerate(POOL_WINDOWS):
        cur = cur + pltpu.roll(cur, w // 2, axis=0)
        sums.append(cur[POOL_HALO:, 0:POOL_GC])
        if gi + 1 < len(POOL_WINDOWS):
            cur = cur[:, POOL_GC:]
    pos = lax.broadcasted_iota(I32, (n, POOL_GC), 0) + t * n
    pooled = jnp.concatenate(
        [s / jnp.minimum(w, pos + 1).astype(F32) for s, w in zip(sums, POOL_WINDOWS)], axis=1)

    @pl.when(t == pl.num_programs(1) - 1)
    def _():
        npool_ref[0, 0] = pbuf_ref[POOL_HALO + n - POOL_BUF:POOL_HALO + n, :]

    pbuf_ref[0:POOL_HALO, :] = pbuf_ref[n:n + POOL_HALO, :]

    m = _pool_project(pooled, h, wg_ref, ps_ref, w_out_ref)
    carry = _post_mixer(x, m, gate, lng_ref, lnb_ref, modb_ref[0], wr_hi_ref, wr_lo_ref, br_ref,
                        carry_ref[...], x1_ref, h2c_ref, route_ref)
    carry_ref[...] = carry
    cnt_ref[...] = carry


def _mixer1_prompt_call(x1_prev, m_all, mod_m, lnpg, lnpb, mod_a, mod_b, cnt, h2c_s, bsz, p):
    seq = x1_prev.shape[0] // bsz
    tl = TILE_L
    nt = seq // tl
    seq_map, _, tok, slot, state_map = _prompt_grid_specs(bsz, nt)
    ng = len(POOL_WINDOWS)
    return pl.pallas_call(
        _mixer1_prompt_kernel,
        grid=(bsz + 1, nt),
        in_specs=[
            pl.BlockSpec((tl, D_MODEL), tok),
            pl.BlockSpec((tl * CHUNKS, LANES), tok),
            pl.BlockSpec((1, 1, 3 * D_MODEL), seq_map),
            _full((1, D_MODEL)),
            _full((1, D_MODEL)),
            pl.BlockSpec((1, 1, 3 * D_MODEL), seq_map),
            pl.BlockSpec((1, 1, 3 * D_MODEL), seq_map),
            _full((1, LANES)),
            _full((tl * CHUNKS, LANES)),
            _full((ng, POOL_GC, POOL_GC)),
            _full((1, D_MODEL)),
            _full((D_MODEL, D_MODEL)),
            _full((1, D_MODEL)),
            _full((1, D_MODEL)),
            _full((D_MODEL, LANES)),
            _full((D_MODEL, LANES)),
            _full((1, LANES)),
        ],
        out_specs=_token_out_specs(tl, tl, tok, slot) + [
            _full((1, LANES)),
            pl.BlockSpec((1, 1, POOL_BUF, D_MODEL), state_map),
        ],
        out_shape=_token_out_shapes(bsz * seq, bsz * seq + tl, bsz * seq) + [
            jax.ShapeDtypeStruct((1, LANES), F32),
            jax.ShapeDtypeStruct((1, bsz, POOL_BUF, D_MODEL), F32),
        ],
        scratch_shapes=[pltpu.VMEM((POOL_HALO + tl, D_MODEL), F32), pltpu.VMEM((1, LANES), F32)],
        compiler_params=_params("arbitrary", "arbitrary"),
        name="mixer1_prompt",
    )(x1_prev, m_all, mod_m, lnpg, lnpb, mod_a, mod_b, cnt, h2c_s,
      p["w_grp"], p["pool_scale"], p["w_out1"], p["ln_g10"], p["ln_b10"],
      p["wr_hi1"], p["wr_lo1"], p["br1"])


def _mixer1_sample_kernel(xin_ref, m_ref, modm_ref, lnpg_ref, lnpb_ref, moda_ref, modb_ref,
                          spool_ref, cnt_in_ref, wg_ref, ps_ref, w_out_ref,
                          lng_ref, lnb_ref, wr_hi_ref, wr_lo_ref, br_ref,
                          x1_ref, h2c_ref, route_ref, cnt_ref, npool_ref,
                          x_ref, h_ref, pooled_ref):
    i = pl.program_id(0)
    nb = spool_ref.shape[1]

    @pl.when(i == 0)
    def _():
        x_ref[...] = _block_input(xin_ref, m_ref, modm_ref[...], lnpg_ref, lnpb_ref)
        shift, scale, _ = _mod3(moda_ref[...])
        h_ref[...] = x_ref[...] * (1.0 + scale) + shift

    lane = lax.broadcasted_iota(I32, (POOL_BUF, D_MODEL), 1)
    rowi = lax.broadcasted_iota(I32, (POOL_BUF, D_MODEL), 0)
    first = jnp.zeros((POOL_BUF, D_MODEL), I32)
    lane1 = lax.broadcasted_iota(I32, (1, D_MODEL), 1)
    inv_w = jnp.zeros((1, D_MODEL), F32)
    for gi, w in enumerate(POOL_WINDOWS):
        in_g = (lane >= gi * POOL_GC) & (lane < (gi + 1) * POOL_GC)
        first = jnp.where(in_g, POOL_BUF - (w - 1), first)
        in_g1 = (lane1 >= gi * POOL_GC) & (lane1 < (gi + 1) * POOL_GC)
        inv_w = jnp.where(in_g1, 1.0 / w, inv_w)
    keep = rowi >= first

    blk = pl.ds(pl.multiple_of(i * nb, nb), nb)
    h_blk = h_ref[blk, :]
    rows = []
    for n in range(nb):
        h_row = h_blk[n:n + 1, :]
        past = spool_ref[0, n]
        total = jnp.sum(jnp.where(keep, past, 0.0), axis=0, keepdims=True) + h_row
        rows.append(total * inv_w)
        npool_ref[0, n, 0:POOL_BUF - 1, :] = spool_ref[0, n, 1:POOL_BUF, :]
        npool_ref[0, n, POOL_BUF - 1:POOL_BUF, :] = h_row
    pooled_ref[blk, :] = jnp.concatenate(rows, axis=0)

    @pl.when(i == pl.num_programs(0) - 1)
    def _():
        m = _pool_project(pooled_ref[...], h_ref[...], wg_ref, ps_ref, w_out_ref)
        _, _, gate = _mod3(moda_ref[...])
        cnt_ref[...] = _post_mixer(x_ref[...], m, gate, lng_ref, lnb_ref, modb_ref[...], wr_hi_ref,
                                   wr_lo_ref, br_ref, cnt_in_ref[...], x1_ref, h2c_ref, route_ref)


def _mixer1_sample_call(x1_prev, m_all, first_block, mod_m, lnpg, lnpb, mod_a, mod_b, state_pool,
                        cnt, p):
    ntok = x1_prev.shape[0]
    nb = SAMPLE_BLK
    ng = len(POOL_WINDOWS)
    tok = lambda i: (0, 0)
    return pl.pallas_call(
        _mixer1_sample_kernel,
        grid=(ntok // nb,),
        in_specs=[
            _full((ntok, D_MODEL)),
            pl.BlockSpec((ntok * CHUNKS, LANES), lambda i: (first_block, 0)),
            _full((ntok, 3 * D_MODEL)),
            _full((1, D_MODEL)),
            _full((1, D_MODEL)),
            _full((ntok, 3 * D_MODEL)),
            _full((ntok, 3 * D_MODEL)),
            pl.BlockSpec((1, nb, POOL_BUF, D_MODEL), lambda i: (0, i, 0, 0)),
            _full((1, LANES)),
            _full((ng, POOL_GC, POOL_GC)),
            _full((1, D_MODEL)),
            _full((D_MODEL, D_MODEL)),
            _full((1, D_MODEL)),
            _full((1, D_MODEL)),
            _full((D_MODEL, LANES)),
            _full((D_MODEL, LANES)),
            _full((1, LANES)),
        ],
        out_specs=_token_out_specs(ntok, TILE_L, tok, tok) + [
            _full((1, LANES)),
            pl.BlockSpec((1, nb, POOL_BUF, D_MODEL), lambda i: (0, i, 0, 0)),
        ],
        out_shape=_token_out_shapes(ntok, TILE_L, TILE_L) + [
            jax.ShapeDtypeStruct((1, LANES), F32),
            jax.ShapeDtypeStruct(state_pool.shape, F32),
        ],
        scratch_shapes=[pltpu.VMEM((ntok, D_MODEL), F32)] * 3,
        compiler_params=_params("arbitrary"),
        name="mixer1_sample",
    )(x1_prev, m_all, mod_m, lnpg, lnpb, mod_a, mod_b, state_pool, cnt,
      p["w_grp"], p["pool_scale"], p["w_out1"],
      p["ln_g10"], p["ln_b10"], p["wr_hi1"], p["wr_lo1"], p["br1"])


S_EA, S_EB, S_VALID, S_FIRST, S_LANE_G, S_LANE_A, S_LANE_B, S_NEXT = range(8)


def _moe_kernel(sched_ref, srcp_ref, src_ref, srcn_ref, h_any, wr_hi_ref, wr_lo_ref, br_ref,
                wga_ref, wua_ref, wda_ref, wgb_ref, wub_ref, wdb_ref,
                m_any,
                xbuf, ybuf, wgu_ref, wd_ref, gsem, ssem):
    i = pl.program_id(0)
    tm = src_ref.shape[2]
    nslot = h_any.shape[0] // CHUNKS
    slot = i % 2

    def gather(ids_ref, s):
        for j in range(tm):
            tok = ids_ref[0, 0, j]
            tok = jnp.where(tok >= nslot, 0, tok)
            pltpu.make_async_copy(
                h_any.at[pl.ds(pl.multiple_of(tok * CHUNKS, CHUNKS), CHUNKS)],
                xbuf.at[s, pl.ds(j * CHUNKS, CHUNKS)], gsem.at[s]).start(priority=j % 2)

    def scatter(ids_ref, s):
        for j in range(tm):
            tok = ids_ref[0, 0, j]
            row = jnp.where(tok >= nslot, nslot + s * tm + j, tok)
            pltpu.make_async_copy(
                ybuf.at[s, pl.ds(j * CHUNKS, CHUNKS)],
                m_any.at[pl.ds(pl.multiple_of(row * CHUNKS, CHUNKS), CHUNKS)],
                ssem.at[s]).start(priority=j % 2)

    def wait_gather(s):
        pltpu.make_async_copy(h_any.at[pl.ds(0, tm * CHUNKS)], xbuf.at[s], gsem.at[s]).wait()

    def wait_scatter(s):
        pltpu.make_async_copy(ybuf.at[s], m_any.at[pl.ds(0, tm * CHUNKS)], ssem.at[s]).wait()

    @pl.when(i == 0)
    def _():
        ybuf[...] = jnp.zeros_like(ybuf)
        for s in range(2):
            spare = m_any.at[pl.ds((nslot + s * tm) * CHUNKS, tm * CHUNKS)]
            pltpu.make_async_copy(ybuf.at[s], spare, ssem.at[s]).start()
        for s in range(2):
            wait_scatter(s)
        gather(src_ref, 0)

    @pl.when(sched_ref[S_FIRST, i] == 1)
    def _():
        for e, (wg, wu, wd) in enumerate(((wga_ref, wua_ref, wda_ref), (wgb_ref, wub_ref, wdb_ref))):
            wgu_ref[2 * e] = wg[0].astype(BF16)
            wgu_ref[2 * e + 1] = wu[0].astype(BF16)
            wd_ref[e * MOE_FF:(e + 1) * MOE_FF, :] = wd[0].astype(BF16)

    @pl.when(sched_ref[S_VALID, i] == 1)
    def _():
        wait_gather(slot)
        gather(srcn_ref, 1 - slot)
        scatter(srcp_ref, 1 - slot)
        x = _load_chunked(xbuf.at[slot], tm, CHUNKS, 0)
        logits = _router_logits(x, wr_hi_ref, wr_lo_ref, br_ref)
        lane = lax.broadcasted_iota(I32, logits.shape, 1)

        def pick(row):
            return jnp.sum(jnp.where(lane == sched_ref[row, i], logits, 0.0), axis=-1, keepdims=True)

        l_g, l_a, l_b = pick(S_LANE_G), pick(S_LANE_A), pick(S_LANE_B)
        p_g = 1.0 / jnp.sum(jnp.where(lane < MOE_GROUPS, jnp.exp(logits - l_g), 0.0),
                            axis=-1, keepdims=True)
        w_ab = (p_g / (1.0 + jnp.exp(l_b - l_a)), p_g / (1.0 + jnp.exp(l_a - l_b)))
        xb = x.astype(BF16)
        hid = [(_silu(_dot(xb, wgu_ref[2 * e])) * _dot(xb, wgu_ref[2 * e + 1]) * w_ab[e]).astype(BF16)
               for e in range(2)]
        y = _dot(jnp.concatenate(hid, axis=1), wd_ref[...])

        @pl.when(i > 0)
        def _():
            wait_scatter(slot)

        _store_chunked(ybuf.at[slot], y)

        @pl.when(sched_ref[S_NEXT, i] == 0)
        def _():
            scatter(src_ref, slot)
            wait_scatter(1 - slot)
            wait_scatter(slot)
            wait_gather(1 - slot)


def _moe_call(sched, src3, h2c_all, wr_hi, wr_lo, br, wg, wu, wd):
    ntile = sched.shape[1]
    tm = MOE_TILE
    nslot = h2c_all.shape[0] // CHUNKS
    ea = lambda i, s: (s[S_EA, i], 0, 0)
    eb = lambda i, s: (s[S_EB, i], 0, 0)
    const2 = lambda i, s: (0, 0)
    smem = pltpu.SMEM
    grid_spec = pltpu.PrefetchScalarGridSpec(
        num_scalar_prefetch=1,
        grid=(ntile,),
        in_specs=[
            pl.BlockSpec((1, 1, tm), lambda i, s: (i, 0, 0), memory_space=smem),
            pl.BlockSpec((1, 1, tm), lambda i, s: (i + 1, 0, 0), memory_space=smem),
            pl.BlockSpec((1, 1, tm), lambda i, s: (jnp.minimum(i + 2, ntile), 0, 0),
                         memory_space=smem),
            pl.BlockSpec(memory_space=pl.ANY),
            pl.BlockSpec((D_MODEL, LANES), const2),
            pl.BlockSpec((D_MODEL, LANES), const2),
            pl.BlockSpec((1, LANES), const2),
            pl.BlockSpec((1, D_MODEL, MOE_FF), ea),
            pl.BlockSpec((1, D_MODEL, MOE_FF), ea),
            pl.BlockSpec((1, MOE_FF, D_MODEL), ea),
            pl.BlockSpec((1, D_MODEL, MOE_FF), eb),
            pl.BlockSpec((1, D_MODEL, MOE_FF), eb),
            pl.BlockSpec((1, MOE_FF, D_MODEL), eb),
        ],
        out_specs=pl.BlockSpec(memory_space=pl.ANY),
        scratch_shapes=[
            pltpu.VMEM((2, tm * CHUNKS, LANES), F32),
            pltpu.VMEM((2, tm * CHUNKS, LANES), F32),
            pltpu.VMEM((4, D_MODEL, MOE_FF), BF16),
            pltpu.VMEM((2 * MOE_FF, D_MODEL), BF16),
            pltpu.SemaphoreType.DMA((2,)),
            pltpu.SemaphoreType.DMA((2,)),
        ],
    )
    return pl.pallas_call(
        _moe_kernel,
        grid_spec=grid_spec,
        out_shape=jax.ShapeDtypeStruct(((nslot + 2 * tm) * CHUNKS, LANES), F32),
        compiler_params=_params("arbitrary"),
        name="moe",
    )(sched, src3, src3, src3, h2c_all, wr_hi, wr_lo, br, wg, wu, wd, wg, wu, wd)


def _post_moe_kernel(x1_ref, m_ref, mod_ref, lng_ref, lnb_ref, out_ref):
    m = _load_chunked(m_ref, x1_ref.shape[0], CHUNKS, 0)
    _, _, gate = _mod3(mod_ref[0])
    out_ref[...] = _layer_norm(ALPHA * x1_ref[...] + (1.0 + gate) * m, lng_ref[...], lnb_ref[...])


def _post_moe_call(x1, m_all, mod, lng, lnb, rows, first_block, steps_per_mod):
    mrows = mod.shape[1]
    return pl.pallas_call(
        _post_moe_kernel,
        grid=(x1.shape[0] // rows,),
        in_specs=[
            pl.BlockSpec((rows, D_MODEL), lambda i: (i, 0)),
            pl.BlockSpec((rows * CHUNKS, LANES), lambda i: (i + first_block, 0)),
            pl.BlockSpec((1, mrows, 3 * D_MODEL), lambda i: (i // steps_per_mod, 0, 0)),
            _full((1, D_MODEL)),
            _full((1, D_MODEL)),
        ],
        out_specs=pl.BlockSpec((rows, D_MODEL), lambda i: (i, 0)),
        out_shape=jax.ShapeDtypeStruct(x1.shape, F32),
        compiler_params=_params("arbitrary"),
        name="post_moe",
    )(x1, m_all, mod, lng, lnb)


def _invert_kernel(dest_ref, src_ref):
    nslot = dest_ref.shape[0]
    lead = MOE_TILE

    def fill(r, c):
        src_ref[r] = nslot
        return c
    lax.fori_loop(0, src_ref.shape[0], fill, 0, unroll=8)

    def place(t, c):
        src_ref[dest_ref[t] + lead] = t
        return c
    lax.fori_loop(0, nslot, place, 0, unroll=8)


def _invert_call(dest, nsrc):
    smem = pl.BlockSpec(memory_space=pltpu.SMEM)
    return pl.pallas_call(
        _invert_kernel,
        in_specs=[smem],
        out_specs=smem,
        out_shape=jax.ShapeDtypeStruct((nsrc,), I32),
        name="invert",
    )(dest)


def _moe_plan(route_p, route_s, cnt, layer):
    tm = MOE_TILE
    bucket = jnp.concatenate([route_p[:, 2], route_s[:, 2]]).astype(I32)
    rank = jnp.concatenate([route_p[:, 3], route_s[:, 3]]).astype(I32)
    nslot = bucket.shape[0]
    ntile = -(-nslot // tm) + N_BUCKETS
    counts = cnt[0, :N_BUCKETS].astype(I32)
    tiles_b = (counts + tm - 1) // tm
    tile_end = jnp.cumsum(tiles_b)
    tile_start = tile_end - tiles_b
    dest = tile_start[bucket] * tm + rank
    src = _invert_call(dest, (ntile + 1) * tm)
    used = tile_end[-1]
    tile = jnp.arange(ntile, dtype=I32)
    ti = jnp.minimum(tile, used - 1)
    b_of = jnp.sum((ti[:, None] >= tile_end[None, :]).astype(I32), axis=1)
    grp, pair = b_of // N_PAIRS, b_of % N_PAIRS
    ex_a, ex_b = jnp.asarray(PAIR_A, I32)[pair], jnp.asarray(PAIR_B, I32)[pair]
    first_expert = layer * N_EXPERTS + grp * MOE_EXPERTS
    first_lane = MOE_GROUPS + grp * MOE_EXPERTS
    valid = (tile < used).astype(I32)
    first = valid * (ti == tile_start[b_of]).astype(I32)
    nxt = (tile + 1 < used).astype(I32)
    sched = jnp.stack([first_expert + ex_a, first_expert + ex_b, valid, first,
                       grp, first_lane + ex_a, first_lane + ex_b, nxt])
    return src.reshape(ntile + 1, 1, tm), sched


def _router_weights(w_coarse, b_coarse, w_fine, b_fine):
    wf = jnp.transpose(w_fine, (1, 0, 2)).reshape(D_MODEL, N_EXPERTS)
    w = jnp.concatenate([w_coarse, wf], axis=1)
    w = jnp.pad(w, ((0, 0), (0, LANES - w.shape[1])))
    b = jnp.concatenate([b_coarse, b_fine.reshape(N_EXPERTS)])
    b = jnp.pad(b, (0, LANES - b.shape[0])).reshape(1, LANES)
    hi_lo = _hi_lo(w)
    return hi_lo[0], hi_lo[1], b


def _prep_params(ln_g, ln_b, w_in_even, w_a2, b_a, gla_norm_g, conv_w, conv_b, conv_ln_g,
                 conv_ln_b, w_out_even, w_grp_pool, pool_scale, w_out_odd, w_coarse, b_coarse,
                 w_fine, b_fine):
    p = {}
    w_in = w_in_even[0]
    o_q, o_k, o_v, o_g = 0, GLA_KW, 2 * GLA_KW, 2 * GLA_KW + GLA_VW
    o_a = o_g + GLA_VW
    o_u = o_a + GLA_RANK
    w_in_r = jnp.concatenate(
        [w_in[:, o_q:o_a], w_in[:, o_u:o_u + 2 * CONV_CH], w_in[:, o_a:o_u],
         jnp.zeros((D_MODEL, A_PAD - GLA_RANK), F32)], axis=1)
    p["w_in"] = _hi_lo(w_in_r)
    p["w_a2"] = _hi_lo(jnp.pad(w_a2[0], ((0, A_PAD - GLA_RANK), (0, 0))))
    p["b_a"] = b_a[0].reshape(1, GLA_KW)
    p["gng"] = gla_norm_g[0].reshape(1, GLA_VW)
    p["conv_w"] = jnp.pad(conv_w[0], ((0, CONV_W_ROWS - CONV_WIDTH), (0, 0)))
    p["conv_b"] = conv_b[0].reshape(1, CONV_CH)
    p["conv_ln_g"] = conv_ln_g[0].reshape(1, CONV_CH)
    p["conv_ln_b"] = conv_ln_b[0].reshape(1, CONV_CH)
    p["w_out0"] = _hi_lo(w_out_even[0])
    p["w_grp"] = w_grp_pool[0].astype(BF16)
    p["pool_scale"] = pool_scale[0].reshape(1, D_MODEL)
    p["w_out1"] = w_out_odd[0].astype(BF16)
    for layer in range(DEPTH):
        for j in range(2):
            p[f"ln_g{layer}{j}"] = ln_g[layer, j].reshape(1, D_MODEL)
            p[f"ln_b{layer}{j}"] = ln_b[layer, j].reshape(1, D_MODEL)
        hi, lo, b = _router_weights(w_coarse[layer], b_coarse[layer], w_fine[layer], b_fine[layer])
        p[f"wr_hi{layer}"], p[f"wr_lo{layer}"], p[f"br{layer}"] = hi, lo, b
    return p


def kernel(x_prompt, x_sample, state_gla, state_conv, state_pool, c_prompt, c_sample, w_ada, b_ada,
           ln_g, ln_b, w_in_even, w_a2, b_a, gla_norm_g, conv_w, conv_b, conv_ln_g, conv_ln_b,
           w_out_even, w_grp_pool, pool_scale, w_out_odd, w_coarse, b_coarse, w_fine, b_fine,
           w_gate, w_up, w_down):
    bsz, seq, _ = x_prompt.shape
    nsmp = x_sample.shape[0]
    ntok_p = bsz * seq
    assert seq % TILE_L == 0 and nsmp <= TILE_L and nsmp % SAMPLE_BLK == 0 and ntok_p % nsmp == 0
    p = _prep_params(ln_g, ln_b, w_in_even, w_a2, b_a, gla_norm_g, conv_w, conv_b, conv_ln_g,
                     conv_ln_b, w_out_even, w_grp_pool, pool_scale, w_out_odd, w_coarse, b_coarse,
                     w_fine, b_fine)
    wg = w_gate.reshape(DEPTH * N_EXPERTS, D_MODEL, MOE_FF)
    wu = w_up.reshape(DEPTH * N_EXPERTS, D_MODEL, MOE_FF)
    wd = w_down.reshape(DEPTH * N_EXPERTS, MOE_FF, D_MODEL)

    c_all = jnp.concatenate([c_prompt, c_sample], axis=0)
    mod = _ada_call(c_all, w_ada.reshape(2 * DEPTH, D_MODEL, 3 * D_MODEL),
                    b_ada.reshape(2 * DEPTH, 1, 3 * D_MODEL))
    mod_p = mod[:, :bsz].reshape(2 * DEPTH, bsz, 1, 3 * D_MODEL)
    mod_s = mod[:, bsz:]

    def moe(h2c_all, route_p, route_s, cnt, layer):
        src3, sched = _moe_plan(route_p, route_s, cnt, layer)
        return _moe_call(sched, src3, h2c_all, p[f"wr_hi{layer}"], p[f"wr_lo{layer}"],
                         p[f"br{layer}"], wg, wu, wd)

    xs0 = x_sample.reshape(nsmp, D_MODEL)
    cnt0 = jnp.zeros((1, LANES), F32)
    x1s, h2c_s, route_s, cnt, gla_s, conv_s = _mixer0_sample_call(
        xs0, mod_s[0], mod_s[1], state_gla, state_conv, cnt0, p)
    x1p, h2c_all, route_p, cnt, conv_p, gla_p = _mixer0_prompt_call(
        x_prompt, mod_p[0], mod_p[1], cnt, h2c_s, p)
    m_all = moe(h2c_all, route_p, route_s, cnt, 0)
    sample_block = ntok_p // nsmp
    x3s, h2c_s, route_s, cnt, pool_s = _mixer1_sample_call(
        x1s, m_all, sample_block, mod_s[1], p["ln_g01"], p["ln_b01"], mod_s[2], mod_s[3],
        state_pool, cnt0, p)
    x3p, h2c_all, route_p, cnt, pool_p = _mixer1_prompt_call(
        x1p, m_all, mod_p[1], p["ln_g01"], p["ln_b01"], mod_p[2], mod_p[3], cnt, h2c_s, bsz, p)
    m_all = moe(h2c_all, route_p, route_s, cnt, 1)
    x4p = _post_moe_call(x3p, m_all, mod_p[3], p["ln_g11"], p["ln_b11"], TILE_L, 0,
                         seq // TILE_L).reshape(bsz, seq, D_MODEL)
    x4s = _post_moe_call(x3s, m_all, mod_s[3][None], p["ln_g11"], p["ln_b11"], nsmp,
                         sample_block, 1)
    return (x4p, x4s.reshape(nsmp, 1, D_MODEL), gla_p, conv_p, pool_p, gla_s, conv_s, pool_s)
```

```python
import jax
import jax.numpy as jnp
from jax import lax
from jax.experimental import pallas as pl
from jax.experimental.pallas import tpu as pltpu

F32 = jnp.float32
BF16 = jnp.bfloat16
I32 = jnp.int32

D_MODEL = 1024
GLA_HEADS = 4
GLA_DK = 64
GLA_DV = 128
GLA_KW = GLA_HEADS * GLA_DK
GLA_VW = GLA_HEADS * GLA_DV
GLA_RANK = 16
GLA_TAU = 16.0
CONV_CH = 512
CONV_WIDTH = 31
CONV_BUF = CONV_WIDTH - 1
POOL_WINDOWS = (2, 4, 8, 16)
POOL_GC = D_MODEL // len(POOL_WINDOWS)
POOL_BUF = max(POOL_WINDOWS) - 1
MOE_GROUPS = 4
MOE_EXPERTS = 4
N_EXPERTS = MOE_GROUPS * MOE_EXPERTS
N_PAIRS = MOE_EXPERTS * (MOE_EXPERTS - 1) // 2
N_BUCKETS = MOE_GROUPS * N_PAIRS
PAIR_A = (0, 0, 0, 1, 1, 2)
PAIR_B = (1, 2, 3, 2, 3, 3)
MOE_FF = 512
DEPTH = 2
ALPHA = (2 * DEPTH) ** 0.25
LN_EPS = 1e-5

LANES = 128
SUBLANES = 8
CHUNKS = D_MODEL // LANES
A_PAD = LANES
Z_WIDTH = 2 * GLA_KW + 2 * GLA_VW + 2 * CONV_CH + A_PAD
NEG_BIG = -1e30
VMEM_LIMIT = 56 * 1024 * 1024

TILE_L = 256
CONV_HALO = 32
CONV_W_ROWS = 32
POOL_HALO = 16
SAMPLE_BLK = 16
MOE_TILE = 256
DISPATCH_ROWS = 640


def _dot(a, b):
    return jnp.dot(a, b, preferred_element_type=F32)


def _dot_nt(a, b):
    return lax.dot_general(a, b, (((1,), (1,)), ((), ())), preferred_element_type=F32)


def _split3(x):
    hi = x.astype(BF16)
    r1 = x - hi.astype(F32)
    mid = r1.astype(BF16)
    lo = (r1 - mid.astype(F32)).astype(BF16)
    return hi, mid, lo


def _split2(x):
    hi = x.astype(BF16)
    lo = (x - hi.astype(F32)).astype(BF16)
    return hi, lo


def _dot_w3(a, w_hi, w_lo):
    a_hi, a_lo = _split2(a)
    return _dot(a_hi, w_hi) + (_dot(a_lo, w_hi) + _dot(a_hi, w_lo))


def _lhs3(parts):
    hi, lo = parts
    return jnp.concatenate([hi, lo, hi], axis=1)


def _rhs3_rows(x):
    hi, lo = _split2(x)
    return jnp.concatenate([hi, hi, lo], axis=0)


def _rhs3_lanes(parts):
    hi, lo = parts
    return jnp.concatenate([hi, hi, lo], axis=1)


def _layer_norm(x, g, b):
    mu = jnp.mean(x, axis=-1, keepdims=True)
    xc = x - mu
    var = jnp.mean(xc * xc, axis=-1, keepdims=True)
    return xc * lax.rsqrt(var + LN_EPS) * g + b


def _standardize(x):
    mu = jnp.mean(x, axis=-1, keepdims=True)
    xc = x - mu
    var = jnp.mean(xc * xc, axis=-1, keepdims=True)
    return xc * lax.rsqrt(var + LN_EPS)


def _sigmoid(x):
    return 1.0 / (1.0 + jnp.exp(-x))


def _silu(x):
    return x * _sigmoid(x)


def _log_sigmoid(x):
    return jnp.minimum(x, 0.0) - jnp.log(1.0 + jnp.exp(-jnp.abs(x)))


def _mod3(mod):
    return mod[:, 0:D_MODEL], mod[:, D_MODEL:2 * D_MODEL], mod[:, 2 * D_MODEL:3 * D_MODEL]


def _full(shape):
    nd = len(shape)
    return pl.BlockSpec(shape, lambda *_: (0,) * nd)


def _const(shape):
    nd = len(shape)
    return pl.BlockSpec(shape, lambda *_: (0,) * nd, pipeline_mode=pl.Buffered(1))


def _hi_lo(w):
    bits = lax.bitcast_convert_type(w, jnp.uint32) & jnp.uint32(0xFFFF0000)
    hi = lax.bitcast_convert_type(bits, F32)
    return jnp.stack([hi.astype(BF16), (w - hi).astype(BF16)])


def _params(*sem):
    return pltpu.CompilerParams(dimension_semantics=sem, vmem_limit_bytes=VMEM_LIMIT)


def _ada_kernel(c_ref, w_ref, b_ref, o_ref):
    w_hi, w_lo = _split2(w_ref[0])
    o_ref[0] = _dot_w3(_silu(c_ref[...]), w_hi, w_lo) + b_ref[0]


def _ada_call(c_all, w_ada, b_ada):
    n = c_all.shape[0]
    ncomb = w_ada.shape[0]
    tn = D_MODEL
    return pl.pallas_call(
        _ada_kernel,
        grid=(ncomb, 3 * D_MODEL // tn),
        in_specs=[
            pl.BlockSpec((n, D_MODEL), lambda i, j: (0, 0)),
            pl.BlockSpec((1, D_MODEL, tn), lambda i, j: (i, 0, j)),
            pl.BlockSpec((1, 1, tn), lambda i, j: (i, 0, j)),
        ],
        out_specs=pl.BlockSpec((1, n, tn), lambda i, j: (i, 0, j)),
        out_shape=jax.ShapeDtypeStruct((ncomb, n, 3 * D_MODEL), F32),
        compiler_params=_params("arbitrary", "arbitrary"),
        name="ada",
    )(c_all, w_ada, b_ada)


def _route(logits, carry):
    rows = logits.shape[0]
    lane = lax.broadcasted_iota(I32, (rows, LANES), 1)
    lanef = lane.astype(F32)
    big = float(LANES)
    lc = jnp.where(lane < MOE_GROUPS, logits, NEG_BIG)
    mc = jnp.max(lc, axis=-1, keepdims=True)
    pg = 1.0 / jnp.sum(jnp.exp(lc - mc), axis=-1, keepdims=True)
    gidx = jnp.min(jnp.where(lc == mc, lanef, big), axis=-1, keepdims=True)
    lo = float(MOE_GROUPS) + gidx * float(MOE_EXPERTS)
    in_grp = (lanef >= lo) & (lanef < lo + float(MOE_EXPERTS))
    lf = jnp.where(in_grp, logits, NEG_BIG)
    m1 = jnp.max(lf, axis=-1, keepdims=True)
    i1 = jnp.min(jnp.where(lf == m1, lanef, big), axis=-1, keepdims=True)
    lf2 = jnp.where(lanef == i1, NEG_BIG, lf)
    m2 = jnp.max(lf2, axis=-1, keepdims=True)
    i2 = jnp.min(jnp.where(lf2 == m2, lanef, big), axis=-1, keepdims=True)
    e2 = jnp.exp(m2 - m1)
    w1 = pg / (1.0 + e2)
    w2 = pg * e2 / (1.0 + e2)
    ea = jnp.minimum(i1, i2) - lo
    eb = jnp.maximum(i1, i2) - lo
    w_a = jnp.where(i1 < i2, w1, w2)
    w_b = jnp.where(i1 < i2, w2, w1)
    bucket = gidx * float(N_PAIRS) + ea * (7.0 - ea) * 0.5 + eb - ea - 1.0
    onehot = jnp.where(lanef == bucket, 1.0, 0.0)
    r = lax.broadcasted_iota(I32, (rows, rows), 0)
    c = lax.broadcasted_iota(I32, (rows, rows), 1)
    before = _dot(jnp.where(r > c, 1.0, 0.0).astype(BF16), onehot.astype(BF16))
    rank = jnp.sum(onehot * (before + carry), axis=-1, keepdims=True)
    new_carry = carry + jnp.sum(onehot, axis=0, keepdims=True)
    route = (jnp.where(lane == 0, w_a, 0.0) + jnp.where(lane == 1, w_b, 0.0)
             + jnp.where(lane == 2, bucket, 0.0) + jnp.where(lane == 3, rank, 0.0))
    return route, new_carry


def _store_chunked(ref, x):
    rows = x.shape[0]
    for c in range(CHUNKS):
        ref[pl.ds(c, rows, stride=CHUNKS), :] = x[:, c * LANES:(c + 1) * LANES]


def _load_chunked(ref, rows, group, first):
    return jnp.concatenate(
        [ref[pl.ds(first + c, rows, stride=group), :] for c in range(CHUNKS)], axis=1)


def _router_logits(h2, wr_hi_ref, wr_lo_ref, br_ref):
    h_hi, h_lo = _split2(h2)
    wr_hi = wr_hi_ref[...]
    return _dot(h_hi, wr_hi) + _dot(h_lo, wr_hi) + _dot(h_hi, wr_lo_ref[...]) + br_ref[...]


def _post_mixer(x, m, gate, lng_ref, lnb_ref, mod_b, wr_hi_ref, wr_lo_ref, br_ref, carry,
                x1_ref, h2c_ref, route_ref):
    x1 = _layer_norm(ALPHA * x + (1.0 + gate) * m, lng_ref[...], lnb_ref[...])
    shift2, scale2, _ = _mod3(mod_b)
    h2 = x1 * (1.0 + scale2) + shift2
    pad = route_ref.shape[0] - x.shape[0]
    if pad:
        h2 = jnp.concatenate([h2, jnp.zeros((pad, D_MODEL), F32)], axis=0)
    route, carry = _route(_router_logits(h2, wr_hi_ref, wr_lo_ref, br_ref), carry)
    x1_ref[...] = x1
    _store_chunked(h2c_ref, h2)
    route_ref[...] = route
    return carry


def _token_out_specs(rows, slot_rows, x1_map, slot_map):
    return [pl.BlockSpec((rows, D_MODEL), x1_map),
            pl.BlockSpec((slot_rows * CHUNKS, LANES), slot_map),
            pl.BlockSpec((slot_rows, LANES), x1_map)]


def _token_out_shapes(ntok, nslot_h2, nslot_route):
    return [jax.ShapeDtypeStruct((ntok, D_MODEL), F32),
            jax.ShapeDtypeStruct((nslot_h2 * CHUNKS, LANES), F32),
            jax.ShapeDtypeStruct((nslot_route, LANES), F32)]


def _hold_rows(b, s):
    n, w = b.shape
    if s >= SUBLANES:
        pieces = []
        for p in range(n // (2 * s)):
            r = p * 2 * s + s - 1
            pieces.append(jnp.broadcast_to(b[r:r + 1, :], (2 * s, w)))
        return pieces[0] if len(pieces) == 1 else jnp.concatenate(pieces, axis=0)
    b3 = b.reshape(n // SUBLANES, SUBLANES, w)
    sub = lax.broadcasted_iota(I32, b3.shape, 1)

    def bc(r):
        return jnp.broadcast_to(b3[:, r:r + 1, :], b3.shape)

    out = bc(s - 1)
    for p in range(1, SUBLANES // (2 * s)):
        out = jnp.where(sub >= p * 2 * s, bc(p * 2 * s + s - 1), out)
    return out.reshape(n, w)


def _gla_tile(q, k, v, la, s_prev):
    n = q.shape[0]
    row = lax.broadcasted_iota(I32, (n, n), 0)
    col = lax.broadcasted_iota(I32, (n, n), 1)
    tri = jnp.where(row >= col, 1.0, 0.0).astype(BF16)
    hi, mid, lo = _split3(la)
    b = _dot(tri, hi) + _dot(tri, mid) + _dot(tri, lo)
    b_last = b[n - 1:n, :]
    kl = k * jnp.exp(b_last - b)

    rowi = lax.broadcasted_iota(I32, (n, GLA_KW), 0)
    levels = []
    s = n // 2
    while s >= 1:
        m = _hold_rows(b, s)
        second = (rowi & s) != 0
        qs = _split2(q * jnp.exp(jnp.where(second, b - m, NEG_BIG)))
        ks = _split2(k * jnp.exp(jnp.where(second, NEG_BIG, m - b)))
        levels.append((2 * s, qs, ks))
        s //= 2
    levels.append((1, _split2(q), _split2(k)))
    qe = _split2(q * jnp.exp(b))

    lane = lax.broadcasted_iota(I32, (n, LANES), 1)
    xor = row ^ col
    zero = jnp.zeros((), BF16)

    def head_part(parts, sl, hm):
        return tuple(jnp.where(hm, x[:, sl], zero) for x in parts)

    outs = []
    for h in range(GLA_HEADS):
        p = h // 2
        sl = slice(p * LANES, (p + 1) * LANES)
        hm = (lane < GLA_DK) if h % 2 == 0 else (lane >= GLA_DK)
        att = None
        for span, qs, ks in levels:
            a_s = _dot_nt(_lhs3(head_part(qs, sl, hm)), _rhs3_lanes((ks[0][:, sl], ks[1][:, sl])))
            att = a_s if att is None else jnp.where(xor < span, a_s, att)
        o_h = _dot(_lhs3(_split2(att)), _rhs3_rows(v[:, h * GLA_DV:(h + 1) * GLA_DV]))
        o_h = o_h + _dot(_lhs3(head_part(qe, sl, hm)), _rhs3_rows(s_prev[sl, :]))
        outs.append(o_h)

    decay = jnp.exp(b_last)
    upper = lax.broadcasted_iota(I32, (LANES, LANES), 0) < GLA_DK
    s_new = []
    for p in range(GLA_HEADS // 2):
        sl = slice(p * LANES, (p + 1) * LANES)
        kl_t = jnp.transpose(kl[:, sl])
        u = _dot(_lhs3(_split2(kl_t)), _rhs3_rows(v[:, p * 2 * GLA_DV:(p + 1) * 2 * GLA_DV]))
        upd = jnp.where(upper, u[:, 0:GLA_DV], u[:, GLA_DV:2 * GLA_DV])
        dcol = jnp.transpose(jnp.broadcast_to(decay[:, sl], (LANES, LANES)))
        s_new.append(dcol * s_prev[sl, :] + upd)
    return outs, jnp.concatenate(s_new, axis=0)


def _split_projection(z):
    c0 = 0
    q = z[:, c0:c0 + GLA_KW] * (GLA_DK ** -0.5); c0 += GLA_KW
    k = z[:, c0:c0 + GLA_KW]; c0 += GLA_KW
    v = z[:, c0:c0 + GLA_VW]; c0 += GLA_VW
    g = z[:, c0:c0 + GLA_VW]; c0 += GLA_VW
    ua = z[:, c0:c0 + CONV_CH]; c0 += CONV_CH
    ug = z[:, c0:c0 + CONV_CH]; c0 += CONV_CH
    a_lr = z[:, c0:c0 + A_PAD]
    return q, k, v, g, ua, ug, a_lr


def _mix0_project(o_heads, g, y, gng_ref, w_out_ref):
    sl = slice(GLA_VW, GLA_VW + CONV_CH)
    m = _dot_w3(y, w_out_ref[0, sl, :], w_out_ref[1, sl, :])
    for hd in range(GLA_HEADS):
        sl = slice(hd * GLA_DV, (hd + 1) * GLA_DV)
        o_h = _standardize(o_heads[hd]) * gng_ref[:, sl] * _silu(g[:, sl])
        m = m + _dot_w3(o_h, w_out_ref[0, sl, :], w_out_ref[1, sl, :])
    return m


def _mix0_inputs(h, w_in_ref, w_a2_ref, b_a_ref):
    z = _dot_w3(h, w_in_ref[0], w_in_ref[1])
    q, k, v, g, ua, ug, a_lr = _split_projection(z)
    la = _log_sigmoid(_dot_w3(a_lr, w_a2_ref[0], w_a2_ref[1]) + b_a_ref[...]) * (1.0 / GLA_TAU)
    return q, k, v, g, ua, ug, la


def _prompt_grid_specs(bsz, nt):
    ntile = bsz * nt
    seq_map = lambda b, t: (jnp.minimum(b, bsz - 1), 0, 0)
    x_map = lambda b, t: (jnp.minimum(b, bsz - 1), t, 0)
    tok_map = lambda b, t: (jnp.minimum(b * nt + t, ntile - 1), 0)
    slot_map = lambda b, t: (jnp.minimum(b * nt + t, ntile), 0)
    state_map = lambda b, t: (0, jnp.minimum(b, bsz - 1), 0, 0)
    return seq_map, x_map, tok_map, slot_map, state_map


def _mixer0_prompt_kernel(x_ref, moda_ref, modb_ref, cnt_in_ref, h2cs_ref, w_in_ref, w_a2_ref,
                          b_a_ref, gng_ref,
                          cw_ref, cb_ref, clg_ref, clb_ref, w_out_ref, lng_ref, lnb_ref,
                          wr_hi_ref, wr_lo_ref, br_ref,
                          x1_ref, h2c_ref, route_ref, cnt_ref, nconv_ref, ngla_ref,
                          s_ref, cbuf_ref, shift_ref, carry_ref):
    nseq = pl.num_programs(0) - 1

    @pl.when(pl.program_id(0) < nseq)
    def _():
        _mixer0_prompt_tile(x_ref, moda_ref, modb_ref, cnt_in_ref, w_in_ref, w_a2_ref, b_a_ref,
                            gng_ref, cw_ref, cb_ref, clg_ref, clb_ref, w_out_ref, lng_ref, lnb_ref,
                            wr_hi_ref, wr_lo_ref, br_ref, x1_ref, h2c_ref, route_ref, cnt_ref,
                            nconv_ref, ngla_ref, s_ref, cbuf_ref, shift_ref, carry_ref)

    @pl.when((pl.program_id(0) == nseq) & (pl.program_id(1) == 0))
    def _():
        h2c_ref[...] = h2cs_ref[...]


def _mixer0_prompt_tile(x_ref, moda_ref, modb_ref, cnt_in_ref, w_in_ref, w_a2_ref, b_a_ref,
                        gng_ref, cw_ref, cb_ref, clg_ref, clb_ref, w_out_ref, lng_ref, lnb_ref,
                        wr_hi_ref, wr_lo_ref, br_ref, x1_ref, h2c_ref, route_ref, cnt_ref,
                        nconv_ref, ngla_ref, s_ref, cbuf_ref, shift_ref, carry_ref):
    t = pl.program_id(1)
    n = x_ref.shape[1]

    @pl.when((t == 0) & (pl.program_id(0) == 0))
    def _():
        carry_ref[...] = cnt_in_ref[...]

    @pl.when(t == 0)
    def _():
        s_ref[...] = jnp.zeros_like(s_ref)
        cbuf_ref[0:CONV_HALO, :] = jnp.zeros((CONV_HALO, CONV_CH), F32)

    x = x_ref[0]
    shift, scale, gate = _mod3(moda_ref[0])
    h = x * (1.0 + scale) + shift
    q, k, v, g, ua, ug, la = _mix0_inputs(h, w_in_ref, w_a2_ref, b_a_ref)
    o_heads, s_new = _gla_tile(q, k, v, la, s_ref[...])
    s_ref[...] = s_new

    glu = ua * _sigmoid(ug)
    cbuf_ref[CONV_HALO:CONV_HALO + n, :] = glu
    acc = jnp.broadcast_to(cb_ref[...], (n, CONV_CH))
    base = CONV_HALO - CONV_BUF
    for r in range(SUBLANES):
        taps = range(r, CONV_WIDTH, SUBLANES)
        span = n + (len(taps) - 1) * SUBLANES
        if (base + r) % SUBLANES == 0:
            src, off = cbuf_ref, base + r
        else:
            shift_ref[r, 0:span, :] = cbuf_ref[base + r:base + r + span, :]
            src, off = shift_ref.at[r], 0
        for a, j in enumerate(taps):
            lo = off + a * SUBLANES
            acc = acc + cw_ref[j:j + 1, :] * src[lo:lo + n, :]
    y = _silu(_layer_norm(acc, clg_ref[...], clb_ref[...]))

    @pl.when(t == pl.num_programs(1) - 1)
    def _():
        nconv_ref[0, 0] = cbuf_ref[CONV_HALO + n - CONV_BUF:CONV_HALO + n, :]
        ngla_ref[0, 0] = s_new.reshape(GLA_HEADS, GLA_DK, GLA_DV)

    cbuf_ref[0:CONV_HALO, :] = cbuf_ref[n:n + CONV_HALO, :]

    m = _mix0_project(o_heads, g, y, gng_ref, w_out_ref)
    carry = _post_mixer(x, m, gate, lng_ref, lnb_ref, modb_ref[0], wr_hi_ref, wr_lo_ref, br_ref,
                        carry_ref[...], x1_ref, h2c_ref, route_ref)
    carry_ref[...] = carry
    cnt_ref[...] = carry


def _mixer0_prompt_call(x, mod_a, mod_b, cnt, h2c_s, p):
    bsz, seq, _ = x.shape
    tl = TILE_L
    nt = seq // tl
    seq_map, x_map, tok, slot, state_map = _prompt_grid_specs(bsz, nt)
    return pl.pallas_call(
        _mixer0_prompt_kernel,
        grid=(bsz + 1, nt),
        in_specs=[
            pl.BlockSpec((1, tl, D_MODEL), x_map),
            pl.BlockSpec((1, 1, 3 * D_MODEL), seq_map),
            pl.BlockSpec((1, 1, 3 * D_MODEL), seq_map),
            _full((1, LANES)),
            _full((tl * CHUNKS, LANES)),
            _const((2, D_MODEL, Z_WIDTH)),
            _const((2, A_PAD, GLA_KW)),
            _full((1, GLA_KW)),
            _full((1, GLA_VW)),
            _full((CONV_W_ROWS, CONV_CH)),
            _full((1, CONV_CH)),
            _full((1, CONV_CH)),
            _full((1, CONV_CH)),
            _const((2, GLA_VW + CONV_CH, D_MODEL)),
            _full((1, D_MODEL)),
            _full((1, D_MODEL)),
            _full((D_MODEL, LANES)),
            _full((D_MODEL, LANES)),
            _full((1, LANES)),
        ],
        out_specs=_token_out_specs(tl, tl, tok, slot) + [
            _full((1, LANES)),
            pl.BlockSpec((1, 1, CONV_BUF, CONV_CH), state_map),
            pl.BlockSpec((1, 1, GLA_HEADS, GLA_DK, GLA_DV), lambda b, t: state_map(b, t) + (0,)),
        ],
        out_shape=_token_out_shapes(bsz * seq, bsz * seq + tl, bsz * seq) + [
            jax.ShapeDtypeStruct((1, LANES), F32),
            jax.ShapeDtypeStruct((1, bsz, CONV_BUF, CONV_CH), F32),
            jax.ShapeDtypeStruct((1, bsz, GLA_HEADS, GLA_DK, GLA_DV), F32),
        ],
        scratch_shapes=[
            pltpu.VMEM((GLA_KW, GLA_DV), F32),
            pltpu.VMEM((CONV_HALO + tl, CONV_CH), F32),
            pltpu.VMEM((SUBLANES, CONV_HALO + tl, CONV_CH), F32),
            pltpu.VMEM((1, LANES), F32),
        ],
        compiler_params=_params("arbitrary", "arbitrary"),
        name="mixer0_prompt",
    )(x, mod_a, mod_b, cnt, h2c_s, p["w_in"], p["w_a2"], p["b_a"], p["gng"], p["conv_w"], p["conv_b"],
      p["conv_ln_g"], p["conv_ln_b"], p["w_out0"], p["ln_g00"], p["ln_b00"],
      p["wr_hi0"], p["wr_lo0"], p["br0"])


def _mixer0_sample_kernel(x_ref, moda_ref, modb_ref, sgla_ref, sconv_ref, cnt_in_ref,
                          w_in_ref, w_a2_ref,
                          b_a_ref, gng_ref, cw_ref, cb_ref, clg_ref, clb_ref, w_out_ref, lng_ref,
                          lnb_ref, wr_hi_ref, wr_lo_ref, br_ref,
                          x1_ref, h2c_ref, route_ref, cnt_ref, ngla_ref, nconv_ref,
                          zt_ref, v_ref, g_ref, glu_ref, o_ref, y_ref):
    i = pl.program_id(0)
    nb = sgla_ref.shape[1]
    ntok = x_ref.shape[0]

    @pl.when(i == 0)
    def _():
        shift, scale, _ = _mod3(moda_ref[...])
        h = x_ref[...] * (1.0 + scale) + shift
        q, k, v, g, ua, ug, la = _mix0_inputs(h, w_in_ref, w_a2_ref, b_a_ref)
        v_ref[...] = v
        g_ref[...] = g
        glu_ref[...] = ua * _sigmoid(ug)
        for j, val in enumerate((jnp.exp(la), k, q)):
            hi, lo = _split2(jnp.transpose(val))
            zt_ref[(2 * j) * GLA_KW:(2 * j + 1) * GLA_KW, :] = hi
            zt_ref[(2 * j + 1) * GLA_KW:(2 * j + 2) * GLA_KW, :] = lo

    tok_row = lax.broadcasted_iota(I32, (ntok, LANES), 0)
    blk = pl.ds(pl.multiple_of(i * nb, nb), nb)
    v_blk = v_ref[blk, :]
    glu_blk = glu_ref[blk, :]
    o_rows, y_rows = [], []
    for n in range(nb):
        onehot = jnp.where(tok_row == i * nb + n, 1.0, 0.0).astype(BF16)
        cols = _dot(zt_ref[...], onehot)
        a_col = cols[0:GLA_KW] + cols[GLA_KW:2 * GLA_KW]
        k_col = cols[2 * GLA_KW:3 * GLA_KW] + cols[3 * GLA_KW:4 * GLA_KW]
        q_col = cols[4 * GLA_KW:5 * GLA_KW] + cols[5 * GLA_KW:6 * GLA_KW]
        v_row = v_blk[n:n + 1, :]
        v_b = jnp.concatenate(
            [jnp.broadcast_to(v_row[:, h * GLA_DV:(h + 1) * GLA_DV], (GLA_DK, GLA_DV))
             for h in range(GLA_HEADS)], axis=0)
        s_old = sgla_ref[0, n].reshape(GLA_KW, GLA_DV)
        s_new = a_col * s_old + k_col * v_b
        ngla_ref[0, n] = s_new.reshape(GLA_HEADS, GLA_DK, GLA_DV)
        o4 = jnp.sum((q_col * s_new).reshape(GLA_HEADS, GLA_DK, GLA_DV), axis=1)
        o_rows.append(jnp.concatenate([o4[h:h + 1, :] for h in range(GLA_HEADS)], axis=1))
        glu_row = glu_blk[n:n + 1, :]
        past = sconv_ref[0, n]
        y_rows.append(jnp.sum(past * cw_ref[0:CONV_BUF, :], axis=0, keepdims=True)
                      + glu_row * cw_ref[CONV_BUF:CONV_WIDTH, :] + cb_ref[...])
        nconv_ref[0, n, 0:CONV_BUF - 1, :] = sconv_ref[0, n, 1:CONV_BUF, :]
        nconv_ref[0, n, CONV_BUF - 1:CONV_BUF, :] = glu_row
    o_ref[blk, :] = jnp.concatenate(o_rows, axis=0)
    y_ref[blk, :] = jnp.concatenate(y_rows, axis=0)

    @pl.when(i == pl.num_programs(0) - 1)
    def _():
        y = _silu(_layer_norm(y_ref[...], clg_ref[...], clb_ref[...]))
        o_heads = [o_ref[:, hd * GLA_DV:(hd + 1) * GLA_DV] for hd in range(GLA_HEADS)]
        m = _mix0_project(o_heads, g_ref[...], y, gng_ref, w_out_ref)
        _, _, gate = _mod3(moda_ref[...])
        cnt_ref[...] = _post_mixer(x_ref[...], m, gate, lng_ref, lnb_ref, modb_ref[...], wr_hi_ref,
                                   wr_lo_ref, br_ref, cnt_in_ref[...], x1_ref, h2c_ref, route_ref)


def _mixer0_sample_call(x, mod_a, mod_b, state_gla, state_conv, cnt, p):
    ntok = x.shape[0]
    nb = SAMPLE_BLK
    tok = lambda i: (0, 0)
    return pl.pallas_call(
        _mixer0_sample_kernel,
        grid=(ntok // nb,),
        in_specs=[
            _full((ntok, D_MODEL)),
            _full((ntok, 3 * D_MODEL)),
            _full((ntok, 3 * D_MODEL)),
            pl.BlockSpec((1, nb, GLA_HEADS, GLA_DK, GLA_DV), lambda i: (0, i, 0, 0, 0)),
            pl.BlockSpec((1, nb, CONV_BUF, CONV_CH), lambda i: (0, i, 0, 0)),
            _full((1, LANES)),
            _const((2, D_MODEL, Z_WIDTH)),
            _const((2, A_PAD, GLA_KW)),
            _full((1, GLA_KW)),
            _full((1, GLA_VW)),
            _full((CONV_W_ROWS, CONV_CH)),
            _full((1, CONV_CH)),
            _full((1, CONV_CH)),
            _full((1, CONV_CH)),
            _const((2, GLA_VW + CONV_CH, D_MODEL)),
            _full((1, D_MODEL)),
            _full((1, D_MODEL)),
            _full((D_MODEL, LANES)),
            _full((D_MODEL, LANES)),
            _full((1, LANES)),
        ],
        out_specs=_token_out_specs(ntok, TILE_L, tok, tok) + [
            _full((1, LANES)),
            pl.BlockSpec((1, nb, GLA_HEADS, GLA_DK, GLA_DV), lambda i: (0, i, 0, 0, 0)),
            pl.BlockSpec((1, nb, CONV_BUF, CONV_CH), lambda i: (0, i, 0, 0)),
        ],
        out_shape=_token_out_shapes(ntok, TILE_L, TILE_L) + [
            jax.ShapeDtypeStruct((1, LANES), F32),
            jax.ShapeDtypeStruct(state_gla.shape, F32),
            jax.ShapeDtypeStruct(state_conv.shape, F32),
        ],
        scratch_shapes=[
            pltpu.VMEM((6 * GLA_KW, ntok), BF16),
            pltpu.VMEM((ntok, GLA_VW), F32),
            pltpu.VMEM((ntok, GLA_VW), F32),
            pltpu.VMEM((ntok, CONV_CH), F32),
            pltpu.VMEM((ntok, GLA_VW), F32),
            pltpu.VMEM((ntok, CONV_CH), F32),
        ],
        compiler_params=_params("arbitrary"),
        name="mixer0_sample",
    )(x, mod_a, mod_b, state_gla, state_conv, cnt, p["w_in"], p["w_a2"], p["b_a"],
      p["gng"], p["conv_w"], p["conv_b"], p["conv_ln_g"], p["conv_ln_b"], p["w_out0"],
      p["ln_g00"], p["ln_b00"], p["wr_hi0"], p["wr_lo0"], p["br0"])


def _pool_project(pooled, h, wg_ref, ps_ref, w_out_ref):
    m = None
    for gi in range(len(POOL_WINDOWS)):
        sl = slice(gi * POOL_GC, (gi + 1) * POOL_GC)
        mixed = _dot((pooled[:, sl] - h[:, sl]).astype(BF16), wg_ref[gi]) * ps_ref[:, sl]
        part = _dot(mixed.astype(BF16), w_out_ref[sl, :])
        m = part if m is None else m + part
    return m


def _block_input(xin_ref, m_ref, mod_m, lnpg_ref, lnpb_ref):
    m = _load_chunked(m_ref, xin_ref.shape[0], CHUNKS, 0)
    _, _, gate = _mod3(mod_m)
    return _layer_norm(ALPHA * xin_ref[...] + (1.0 + gate) * m, lnpg_ref[...], lnpb_ref[...])


def _mixer1_prompt_kernel(dest_ref, dnext_ref, xin_ref, y_any, modm_ref, lnpg_ref, lnpb_ref,
                          moda_ref, modb_ref, cnt_in_ref, h2cs_ref, wg_ref, ps_ref,
                          w_out_ref, lng_ref, lnb_ref, wr_hi_ref, wr_lo_ref, br_ref,
                          x1_ref, h2c_ref, route_ref, cnt_ref, npool_ref,
                          pbuf_ref, carry_ref, mbuf, sems):
    nseq = pl.num_programs(0) - 1
    nt = pl.num_programs(1)

    @pl.when(pl.program_id(0) < nseq)
    def _():
        m_ref = _gather_pipelined(pl.program_id(0) * nt + pl.program_id(1), nseq * nt,
                                  dest_ref, dnext_ref, y_any, mbuf, sems)
        x = _block_input(xin_ref, m_ref, modm_ref[0], lnpg_ref, lnpb_ref)
        _mixer1_prompt_tile(x, moda_ref, modb_ref, cnt_in_ref, wg_ref, ps_ref, w_out_ref,
                            lng_ref, lnb_ref, wr_hi_ref, wr_lo_ref, br_ref, x1_ref, h2c_ref,
                            route_ref, cnt_ref, npool_ref, pbuf_ref, carry_ref)

    @pl.when((pl.program_id(0) == nseq) & (pl.program_id(1) == 0))
    def _():
        h2c_ref[...] = h2cs_ref[...]


def _mixer1_prompt_tile(x, moda_ref, modb_ref, cnt_in_ref, wg_ref, ps_ref, w_out_ref,
                        lng_ref, lnb_ref, wr_hi_ref, wr_lo_ref, br_ref, x1_ref, h2c_ref,
                        route_ref, cnt_ref, npool_ref, pbuf_ref, carry_ref):
    t = pl.program_id(1)
    n = x.shape[0]

    @pl.when((t == 0) & (pl.program_id(0) == 0))
    def _():
        carry_ref[...] = cnt_in_ref[...]

    @pl.when(t == 0)
    def _():
        pbuf_ref[0:POOL_HALO, :] = jnp.zeros((POOL_HALO, D_MODEL), F32)

    shift, scale, gate = _mod3(moda_ref[0])
    h = x * (1.0 + scale) + shift
    pbuf_ref[POOL_HALO:POOL_HALO + n, :] = h

    cur = pbuf_ref[...]
    sums = []
    for gi, w in enumerate(POOL_WINDOWS):
        cur = cur + pltpu.roll(cur, w // 2, axis=0)
        sums.append(cur[POOL_HALO:, 0:POOL_GC])
        if gi + 1 < len(POOL_WINDOWS):
            cur = cur[:, POOL_GC:]
    pos = lax.broadcasted_iota(I32, (n, POOL_GC), 0) + t * n
    pooled = jnp.concatenate(
        [s / jnp.minimum(w, pos + 1).astype(F32) for s, w in zip(sums, POOL_WINDOWS)], axis=1)

    @pl.when(t == pl.num_programs(1) - 1)
    def _():
        npool_ref[0, 0] = pbuf_ref[POOL_HALO + n - POOL_BUF:POOL_HALO + n, :]

    pbuf_ref[0:POOL_HALO, :] = pbuf_ref[n:n + POOL_HALO, :]

    m = _pool_project(pooled, h, wg_ref, ps_ref, w_out_ref)
    carry = _post_mixer(x, m, gate, lng_ref, lnb_ref, modb_ref[0], wr_hi_ref, wr_lo_ref, br_ref,
                        carry_ref[...], x1_ref, h2c_ref, route_ref)
    carry_ref[...] = carry
    cnt_ref[...] = carry


def _mixer1_prompt_call(x1_prev, dest, y_sorted, mod_m, lnpg, lnpb, mod_a, mod_b, cnt, h2c_s, bsz, p):
    seq = x1_prev.shape[0] // bsz
    tl = TILE_L
    nt = seq // tl
    ntile = bsz * nt
    seq_map, _, tok, slot, state_map = _prompt_grid_specs(bsz, nt)
    ng = len(POOL_WINDOWS)
    return pl.pallas_call(
        _mixer1_prompt_kernel,
        grid=(bsz + 1, nt),
        in_specs=[
            pl.BlockSpec((1, 1, tl), lambda b, t: tok(b, t) + (0,), memory_space=pltpu.SMEM),
            pl.BlockSpec((1, 1, tl), lambda b, t: (jnp.minimum(b * nt + t + 1, ntile - 1), 0, 0),
                         memory_space=pltpu.SMEM),
            pl.BlockSpec((tl, D_MODEL), tok),
            pl.BlockSpec(memory_space=pl.ANY),
            pl.BlockSpec((1, 1, 3 * D_MODEL), seq_map),
            _full((1, D_MODEL)),
            _full((1, D_MODEL)),
            pl.BlockSpec((1, 1, 3 * D_MODEL), seq_map),
            pl.BlockSpec((1, 1, 3 * D_MODEL), seq_map),
            _full((1, LANES)),
            _full((tl * CHUNKS, LANES)),
            _full((ng, POOL_GC, POOL_GC)),
            _full((1, D_MODEL)),
            _full((D_MODEL, D_MODEL)),
            _full((1, D_MODEL)),
            _full((1, D_MODEL)),
            _full((D_MODEL, LANES)),
            _full((D_MODEL, LANES)),
            _full((1, LANES)),
        ],
        out_specs=_token_out_specs(tl, tl, tok, slot) + [
            _full((1, LANES)),
            pl.BlockSpec((1, 1, POOL_BUF, D_MODEL), state_map),
        ],
        out_shape=_token_out_shapes(bsz * seq, bsz * seq + tl, bsz * seq) + [
            jax.ShapeDtypeStruct((1, LANES), F32),
            jax.ShapeDtypeStruct((1, bsz, POOL_BUF, D_MODEL), F32),
        ],
        scratch_shapes=[pltpu.VMEM((POOL_HALO + tl, D_MODEL), F32), pltpu.VMEM((1, LANES), F32),
                        pltpu.VMEM((2, tl * CHUNKS, LANES), F32), pltpu.SemaphoreType.DMA((2,))],
        compiler_params=_params("arbitrary", "arbitrary"),
        name="mixer1_prompt",
    )(dest.reshape(ntile, 1, tl), dest.reshape(ntile, 1, tl), x1_prev, y_sorted, mod_m, lnpg, lnpb,
      mod_a, mod_b, cnt, h2c_s,
      p["w_grp"], p["pool_scale"], p["w_out1"], p["ln_g10"], p["ln_b10"],
      p["wr_hi1"], p["wr_lo1"], p["br1"])


def _mixer1_sample_kernel(dest_ref, xin_ref, y_any, modm_ref, lnpg_ref, lnpb_ref, moda_ref,
                          modb_ref, spool_ref, cnt_in_ref, wg_ref, ps_ref, w_out_ref,
                          lng_ref, lnb_ref, wr_hi_ref, wr_lo_ref, br_ref,
                          x1_ref, h2c_ref, route_ref, cnt_ref, npool_ref,
                          x_ref, h_ref, pooled_ref, mbuf, sem):
    i = pl.program_id(0)
    nb = spool_ref.shape[1]

    @pl.when(i == 0)
    def _():
        _start_row_gather(dest_ref, y_any, mbuf, sem)
        _wait_row_gather(y_any, mbuf, sem)
        x_ref[...] = _block_input(xin_ref, mbuf, modm_ref[...], lnpg_ref, lnpb_ref)
        shift, scale, _ = _mod3(moda_ref[...])
        h_ref[...] = x_ref[...] * (1.0 + scale) + shift

    lane = lax.broadcasted_iota(I32, (POOL_BUF, D_MODEL), 1)
    rowi = lax.broadcasted_iota(I32, (POOL_BUF, D_MODEL), 0)
    first = jnp.zeros((POOL_BUF, D_MODEL), I32)
    lane1 = lax.broadcasted_iota(I32, (1, D_MODEL), 1)
    inv_w = jnp.zeros((1, D_MODEL), F32)
    for gi, w in enumerate(POOL_WINDOWS):
        in_g = (lane >= gi * POOL_GC) & (lane < (gi + 1) * POOL_GC)
        first = jnp.where(in_g, POOL_BUF - (w - 1), first)
        in_g1 = (lane1 >= gi * POOL_GC) & (lane1 < (gi + 1) * POOL_GC)
        inv_w = jnp.where(in_g1, 1.0 / w, inv_w)
    keep = rowi >= first

    blk = pl.ds(pl.multiple_of(i * nb, nb), nb)
    h_blk = h_ref[blk, :]
    rows = []
    for n in range(nb):
        h_row = h_blk[n:n + 1, :]
        past = spool_ref[0, n]
        total = jnp.sum(jnp.where(keep, past, 0.0), axis=0, keepdims=True) + h_row
        rows.append(total * inv_w)
        npool_ref[0, n, 0:POOL_BUF - 1, :] = spool_ref[0, n, 1:POOL_BUF, :]
        npool_ref[0, n, POOL_BUF - 1:POOL_BUF, :] = h_row
    pooled_ref[blk, :] = jnp.concatenate(rows, axis=0)

    @pl.when(i == pl.num_programs(0) - 1)
    def _():
        m = _pool_project(pooled_ref[...], h_ref[...], wg_ref, ps_ref, w_out_ref)
        _, _, gate = _mod3(moda_ref[...])
        cnt_ref[...] = _post_mixer(x_ref[...], m, gate, lng_ref, lnb_ref, modb_ref[...], wr_hi_ref,
                                   wr_lo_ref, br_ref, cnt_in_ref[...], x1_ref, h2c_ref, route_ref)


def _mixer1_sample_call(x1_prev, dest, y_sorted, mod_m, lnpg, lnpb, mod_a, mod_b, state_pool,
                        cnt, p):
    ntok = x1_prev.shape[0]
    nb = SAMPLE_BLK
    ng = len(POOL_WINDOWS)
    tok = lambda i: (0, 0)
    return pl.pallas_call(
        _mixer1_sample_kernel,
        grid=(ntok // nb,),
        in_specs=[
            pl.BlockSpec((1, 1, ntok), lambda i: (0, 0, 0), memory_space=pltpu.SMEM),
            _full((ntok, D_MODEL)),
            pl.BlockSpec(memory_space=pl.ANY),
            _full((ntok, 3 * D_MODEL)),
            _full((1, D_MODEL)),
            _full((1, D_MODEL)),
            _full((ntok, 3 * D_MODEL)),
            _full((ntok, 3 * D_MODEL)),
            pl.BlockSpec((1, nb, POOL_BUF, D_MODEL), lambda i: (0, i, 0, 0)),
            _full((1, LANES)),
            _full((ng, POOL_GC, POOL_GC)),
            _full((1, D_MODEL)),
            _full((D_MODEL, D_MODEL)),
            _full((1, D_MODEL)),
            _full((1, D_MODEL)),
            _full((D_MODEL, LANES)),
            _full((D_MODEL, LANES)),
            _full((1, LANES)),
        ],
        out_specs=_token_out_specs(ntok, TILE_L, tok, tok) + [
            _full((1, LANES)),
            pl.BlockSpec((1, nb, POOL_BUF, D_MODEL), lambda i: (0, i, 0, 0)),
        ],
        out_shape=_token_out_shapes(ntok, TILE_L, TILE_L) + [
            jax.ShapeDtypeStruct((1, LANES), F32),
            jax.ShapeDtypeStruct(state_pool.shape, F32),
        ],
        scratch_shapes=[pltpu.VMEM((ntok, D_MODEL), F32)] * 3 + [
            pltpu.VMEM((ntok * CHUNKS, LANES), F32), pltpu.SemaphoreType.DMA],
        compiler_params=_params("arbitrary"),
        name="mixer1_sample",
    )(dest.reshape(1, 1, ntok), x1_prev, y_sorted, mod_m, lnpg, lnpb, mod_a, mod_b, state_pool, cnt,
      p["w_grp"], p["pool_scale"], p["w_out1"],
      p["ln_g10"], p["ln_b10"], p["wr_hi1"], p["wr_lo1"], p["br1"])


def _dispatch_kernel(dest_ref, src_ref, init_any, dst_any, sem):
    del init_any
    n = dest_ref.shape[2]
    for j in range(n):
        d = dest_ref[0, 0, j]
        pltpu.make_async_copy(
            src_ref.at[pl.ds(j * CHUNKS, CHUNKS)],
            dst_any.at[pl.ds(pl.multiple_of(d * CHUNKS, CHUNKS), CHUNKS)], sem).start()
    pltpu.make_async_copy(src_ref, dst_any.at[pl.ds(0, n * CHUNKS)], sem).wait()


def _dispatch_call(dest, h2c_all, sorted_buf):
    nslot = dest.shape[0]
    rows = DISPATCH_ROWS
    return pl.pallas_call(
        _dispatch_kernel,
        grid=(nslot // rows,),
        in_specs=[
            pl.BlockSpec((1, 1, rows), lambda i: (i, 0, 0), memory_space=pltpu.SMEM),
            pl.BlockSpec((rows * CHUNKS, LANES), lambda i: (i, 0)),
            pl.BlockSpec(memory_space=pl.ANY),
        ],
        out_specs=pl.BlockSpec(memory_space=pl.ANY),
        out_shape=jax.ShapeDtypeStruct(sorted_buf.shape, F32),
        scratch_shapes=[pltpu.SemaphoreType.DMA],
        input_output_aliases={2: 0},
        compiler_params=_params("arbitrary"),
        name="dispatch",
    )(dest.reshape(nslot // rows, 1, rows), h2c_all, sorted_buf)


S_BLK, S_EA, S_EB, S_VALID, S_FIRST, S_LANE_G, S_LANE_A, S_LANE_B = range(8)


def _moe_kernel(sched_ref, x_ref, wr_hi_ref, wr_lo_ref, br_ref,
                wga_ref, wua_ref, wda_ref, wgb_ref, wub_ref, wdb_ref,
                y_ref, wgu_ref, wd_ref):
    i = pl.program_id(0)
    tm = x_ref.shape[0] // CHUNKS

    @pl.when(sched_ref[S_FIRST, i] == 1)
    def _():
        for e, (wg, wu, wd) in enumerate(((wga_ref, wua_ref, wda_ref), (wgb_ref, wub_ref, wdb_ref))):
            wgu_ref[2 * e] = wg[0].astype(BF16)
            wgu_ref[2 * e + 1] = wu[0].astype(BF16)
            wd_ref[e * MOE_FF:(e + 1) * MOE_FF, :] = wd[0].astype(BF16)

    @pl.when(sched_ref[S_VALID, i] == 0)
    def _():
        y_ref[...] = jnp.zeros_like(y_ref)

    @pl.when(sched_ref[S_VALID, i] == 1)
    def _():
        x = _load_chunked(x_ref, tm, CHUNKS, 0)
        logits = _router_logits(x, wr_hi_ref, wr_lo_ref, br_ref)
        lane = lax.broadcasted_iota(I32, logits.shape, 1)

        def pick(row):
            return jnp.sum(jnp.where(lane == sched_ref[row, i], logits, 0.0), axis=-1, keepdims=True)

        l_g, l_a, l_b = pick(S_LANE_G), pick(S_LANE_A), pick(S_LANE_B)
        p_g = 1.0 / jnp.sum(jnp.where(lane < MOE_GROUPS, jnp.exp(logits - l_g), 0.0),
                            axis=-1, keepdims=True)
        w_ab = (p_g / (1.0 + jnp.exp(l_b - l_a)), p_g / (1.0 + jnp.exp(l_a - l_b)))
        xb = x.astype(BF16)
        hid = [(_silu(_dot(xb, wgu_ref[2 * e])) * _dot(xb, wgu_ref[2 * e + 1]) * w_ab[e]).astype(BF16)
               for e in range(2)]
        _store_chunked(y_ref, _dot(jnp.concatenate(hid, axis=1), wd_ref[...]))


def _moe_call(sched, sorted_x, wr_hi, wr_lo, br, wg, wu, wd):
    ntile = sched.shape[1]
    tm = MOE_TILE
    ea = lambda i, s: (s[S_EA, i], 0, 0)
    eb = lambda i, s: (s[S_EB, i], 0, 0)
    const2 = lambda i, s: (0, 0)
    grid_spec = pltpu.PrefetchScalarGridSpec(
        num_scalar_prefetch=1,
        grid=(ntile,),
        in_specs=[
            pl.BlockSpec((tm * CHUNKS, LANES), lambda i, s: (s[S_BLK, i], 0)),
            pl.BlockSpec((D_MODEL, LANES), const2),
            pl.BlockSpec((D_MODEL, LANES), const2),
            pl.BlockSpec((1, LANES), const2),
            pl.BlockSpec((1, D_MODEL, MOE_FF), ea),
            pl.BlockSpec((1, D_MODEL, MOE_FF), ea),
            pl.BlockSpec((1, MOE_FF, D_MODEL), ea),
            pl.BlockSpec((1, D_MODEL, MOE_FF), eb),
            pl.BlockSpec((1, D_MODEL, MOE_FF), eb),
            pl.BlockSpec((1, MOE_FF, D_MODEL), eb),
        ],
        out_specs=pl.BlockSpec((tm * CHUNKS, LANES), lambda i, s: (i, 0)),
        scratch_shapes=[
            pltpu.VMEM((4, D_MODEL, MOE_FF), BF16),
            pltpu.VMEM((2 * MOE_FF, D_MODEL), BF16),
        ],
    )
    return pl.pallas_call(
        _moe_kernel,
        grid_spec=grid_spec,
        out_shape=jax.ShapeDtypeStruct((ntile * tm * CHUNKS, LANES), F32),
        compiler_params=_params("arbitrary"),
        name="moe",
    )(sched, sorted_x, wr_hi, wr_lo, br, wg, wu, wd, wg, wu, wd)


def _start_row_gather(dest_ref, y_any, buf, sem):
    for j in range(dest_ref.shape[2]):
        d = dest_ref[0, 0, j]
        pltpu.make_async_copy(
            y_any.at[pl.ds(pl.multiple_of(d * CHUNKS, CHUNKS), CHUNKS)],
            buf.at[pl.ds(j * CHUNKS, CHUNKS)], sem).start()


def _wait_row_gather(y_any, buf, sem):
    pltpu.make_async_copy(y_any.at[pl.ds(0, buf.shape[0])], buf, sem).wait()


def _gather_pipelined(step, nstep, dest_ref, dnext_ref, y_any, mbuf, sems):
    slot = step % 2

    @pl.when(step == 0)
    def _():
        _start_row_gather(dest_ref, y_any, mbuf.at[0], sems.at[0])

    @pl.when(step + 1 < nstep)
    def _():
        _start_row_gather(dnext_ref, y_any, mbuf.at[1 - slot], sems.at[1 - slot])

    _wait_row_gather(y_any, mbuf.at[slot], sems.at[slot])
    return mbuf.at[slot]


def _post_moe_kernel(dest_ref, dnext_ref, x1_ref, y_any, mod_ref, lng_ref, lnb_ref, out_ref,
                     mbuf, sems):
    m_ref = _gather_pipelined(pl.program_id(0), pl.num_programs(0), dest_ref, dnext_ref, y_any,
                              mbuf, sems)
    out_ref[...] = _block_input(x1_ref, m_ref, mod_ref[0], lng_ref, lnb_ref)


def _post_moe_call(x1, dest, y_sorted, mod, lng, lnb, rows, steps_per_mod):
    mrows = mod.shape[1]
    nstep = x1.shape[0] // rows
    dest3 = dest.reshape(nstep, 1, rows)
    return pl.pallas_call(
        _post_moe_kernel,
        grid=(nstep,),
        in_specs=[
            pl.BlockSpec((1, 1, rows), lambda i: (i, 0, 0), memory_space=pltpu.SMEM),
            pl.BlockSpec((1, 1, rows), lambda i: (jnp.minimum(i + 1, nstep - 1), 0, 0),
                         memory_space=pltpu.SMEM),
            pl.BlockSpec((rows, D_MODEL), lambda i: (i, 0)),
            pl.BlockSpec(memory_space=pl.ANY),
            pl.BlockSpec((1, mrows, 3 * D_MODEL), lambda i: (i // steps_per_mod, 0, 0)),
            _full((1, D_MODEL)),
            _full((1, D_MODEL)),
        ],
        out_specs=pl.BlockSpec((rows, D_MODEL), lambda i: (i, 0)),
        out_shape=jax.ShapeDtypeStruct(x1.shape, F32),
        scratch_shapes=[pltpu.VMEM((2, rows * CHUNKS, LANES), F32), pltpu.SemaphoreType.DMA((2,))],
        compiler_params=_params("arbitrary"),
        name="post_moe",
    )(dest3, dest3, x1, y_sorted, mod, lng, lnb)


def _moe_plan(route_p, route_s, cnt, layer):
    tm = MOE_TILE
    bucket = jnp.concatenate([route_p[:, 2], route_s[:, 2]]).astype(I32)
    rank = jnp.concatenate([route_p[:, 3], route_s[:, 3]]).astype(I32)
    nslot = bucket.shape[0]
    ntile = -(-nslot // tm) + N_BUCKETS
    counts = cnt[0, :N_BUCKETS].astype(I32)
    tiles_b = (counts + tm - 1) // tm
    tile_end = jnp.cumsum(tiles_b)
    tile_start = tile_end - tiles_b
    dest = tile_start[bucket] * tm + rank
    used = tile_end[-1]
    tile = jnp.arange(ntile, dtype=I32)
    ti = jnp.minimum(tile, used - 1)
    b_of = jnp.sum((ti[:, None] >= tile_end[None, :]).astype(I32), axis=1)
    grp, pair = b_of // N_PAIRS, b_of % N_PAIRS
    ex_a, ex_b = jnp.asarray(PAIR_A, I32)[pair], jnp.asarray(PAIR_B, I32)[pair]
    first_expert = layer * N_EXPERTS + grp * MOE_EXPERTS
    first_lane = MOE_GROUPS + grp * MOE_EXPERTS
    valid = (tile < used).astype(I32)
    first = valid * (ti == tile_start[b_of]).astype(I32)
    sched = jnp.stack([ti, first_expert + ex_a, first_expert + ex_b, valid, first,
                       grp, first_lane + ex_a, first_lane + ex_b])
    return dest, sched


def _router_weights(w_coarse, b_coarse, w_fine, b_fine):
    wf = jnp.transpose(w_fine, (1, 0, 2)).reshape(D_MODEL, N_EXPERTS)
    w = jnp.concatenate([w_coarse, wf], axis=1)
    w = jnp.pad(w, ((0, 0), (0, LANES - w.shape[1])))
    b = jnp.concatenate([b_coarse, b_fine.reshape(N_EXPERTS)])
    b = jnp.pad(b, (0, LANES - b.shape[0])).reshape(1, LANES)
    hi_lo = _hi_lo(w)
    return hi_lo[0], hi_lo[1], b


def _prep_params(ln_g, ln_b, w_in_even, w_a2, b_a, gla_norm_g, conv_w, conv_b, conv_ln_g,
                 conv_ln_b, w_out_even, w_grp_pool, pool_scale, w_out_odd, w_coarse, b_coarse,
                 w_fine, b_fine):
    p = {}
    w_in = w_in_even[0]
    o_q, o_k, o_v, o_g = 0, GLA_KW, 2 * GLA_KW, 2 * GLA_KW + GLA_VW
    o_a = o_g + GLA_VW
    o_u = o_a + GLA_RANK
    w_in_r = jnp.concatenate(
        [w_in[:, o_q:o_a], w_in[:, o_u:o_u + 2 * CONV_CH], w_in[:, o_a:o_u],
         jnp.zeros((D_MODEL, A_PAD - GLA_RANK), F32)], axis=1)
    p["w_in"] = _hi_lo(w_in_r)
    p["w_a2"] = _hi_lo(jnp.pad(w_a2[0], ((0, A_PAD - GLA_RANK), (0, 0))))
    p["b_a"] = b_a[0].reshape(1, GLA_KW)
    p["gng"] = gla_norm_g[0].reshape(1, GLA_VW)
    p["conv_w"] = jnp.pad(conv_w[0], ((0, CONV_W_ROWS - CONV_WIDTH), (0, 0)))
    p["conv_b"] = conv_b[0].reshape(1, CONV_CH)
    p["conv_ln_g"] = conv_ln_g[0].reshape(1, CONV_CH)
    p["conv_ln_b"] = conv_ln_b[0].reshape(1, CONV_CH)
    p["w_out0"] = _hi_lo(w_out_even[0])
    p["w_grp"] = w_grp_pool[0].astype(BF16)
    p["pool_scale"] = pool_scale[0].reshape(1, D_MODEL)
    p["w_out1"] = w_out_odd[0].astype(BF16)
    for layer in range(DEPTH):
        for j in range(2):
            p[f"ln_g{layer}{j}"] = ln_g[layer, j].reshape(1, D_MODEL)
            p[f"ln_b{layer}{j}"] = ln_b[layer, j].reshape(1, D_MODEL)
        hi, lo, b = _router_weights(w_coarse[layer], b_coarse[layer], w_fine[layer], b_fine[layer])
        p[f"wr_hi{layer}"], p[f"wr_lo{layer}"], p[f"br{layer}"] = hi, lo, b
    return p


def kernel(x_prompt, x_sample, state_gla, state_conv, state_pool, c_prompt, c_sample, w_ada, b_ada,
           ln_g, ln_b, w_in_even, w_a2, b_a, gla_norm_g, conv_w, conv_b, conv_ln_g, conv_ln_b,
           w_out_even, w_grp_pool, pool_scale, w_out_odd, w_coarse, b_coarse, w_fine, b_fine,
           w_gate, w_up, w_down):
    bsz, seq, _ = x_prompt.shape
    nsmp = x_sample.shape[0]
    ntok_p = bsz * seq
    assert seq % TILE_L == 0 and nsmp <= TILE_L and nsmp % SAMPLE_BLK == 0 and ntok_p % nsmp == 0
    p = _prep_params(ln_g, ln_b, w_in_even, w_a2, b_a, gla_norm_g, conv_w, conv_b, conv_ln_g,
                     conv_ln_b, w_out_even, w_grp_pool, pool_scale, w_out_odd, w_coarse, b_coarse,
                     w_fine, b_fine)
    wg = w_gate.reshape(DEPTH * N_EXPERTS, D_MODEL, MOE_FF)
    wu = w_up.reshape(DEPTH * N_EXPERTS, D_MODEL, MOE_FF)
    wd = w_down.reshape(DEPTH * N_EXPERTS, MOE_FF, D_MODEL)

    c_all = jnp.concatenate([c_prompt, c_sample], axis=0)
    mod = _ada_call(c_all, w_ada.reshape(2 * DEPTH, D_MODEL, 3 * D_MODEL),
                    b_ada.reshape(2 * DEPTH, 1, 3 * D_MODEL))
    mod_p = mod[:, :bsz].reshape(2 * DEPTH, bsz, 1, 3 * D_MODEL)
    mod_s = mod[:, bsz:]

    def moe(h2c_all, route_p, route_s, cnt, layer):
        dest, sched = _moe_plan(route_p, route_s, cnt, layer)
        nrow = sched.shape[1] * MOE_TILE
        sorted_x = _dispatch_call(dest, h2c_all, jnp.zeros((nrow * CHUNKS, LANES), F32))
        y_sorted = _moe_call(sched, sorted_x, p[f"wr_hi{layer}"], p[f"wr_lo{layer}"],
                             p[f"br{layer}"], wg, wu, wd)
        return dest[:ntok_p], dest[ntok_p:ntok_p + nsmp], y_sorted

    xs0 = x_sample.reshape(nsmp, D_MODEL)
    cnt0 = jnp.zeros((1, LANES), F32)
    x1s, h2c_s, route_s, cnt, gla_s, conv_s = _mixer0_sample_call(
        xs0, mod_s[0], mod_s[1], state_gla, state_conv, cnt0, p)
    x1p, h2c_all, route_p, cnt, conv_p, gla_p = _mixer0_prompt_call(
        x_prompt, mod_p[0], mod_p[1], cnt, h2c_s, p)
    dest_p, dest_s, y_sorted = moe(h2c_all, route_p, route_s, cnt, 0)
    x3s, h2c_s, route_s, cnt, pool_s = _mixer1_sample_call(
        x1s, dest_s, y_sorted, mod_s[1], p["ln_g01"], p["ln_b01"], mod_s[2], mod_s[3],
        state_pool, cnt0, p)
    x3p, h2c_all, route_p, cnt, pool_p = _mixer1_prompt_call(
        x1p, dest_p, y_sorted, mod_p[1], p["ln_g01"], p["ln_b01"], mod_p[2], mod_p[3], cnt,
        h2c_s, bsz, p)
    dest_p, dest_s, y_sorted = moe(h2c_all, route_p, route_s, cnt, 1)
    x4p = _post_moe_call(x3p, dest_p, y_sorted, mod_p[3], p["ln_g11"], p["ln_b11"], TILE_L,
                         seq // TILE_L).reshape(bsz, seq, D_MODEL)
    x4s = _post_moe_call(x3s, dest_s, y_sorted, mod_s[3][None], p["ln_g11"], p["ln_b11"], nsmp, 1)
    return (x4p, x4s.reshape(nsmp, 1, D_MODEL), gla_p, conv_p, pool_p, gla_s, conv_s, pool_s)
```

```python
import functools
import math

import jax
import jax.numpy as jnp
from jax import lax
from jax.experimental import pallas as pl
from jax.experimental.pallas import tpu as pltpu

F32 = jnp.float32
BF16 = jnp.bfloat16
I32 = jnp.int32

D_MODEL = 1024
GLA_HEADS = 4
GLA_DK = 64
GLA_DV = 128
GLA_KW = GLA_HEADS * GLA_DK
GLA_VW = GLA_HEADS * GLA_DV
GLA_RANK = 16
GLA_TAU = 16.0
CONV_CH = 512
CONV_WIDTH = 31
CONV_BUF = CONV_WIDTH - 1
POOL_WINDOWS = (2, 4, 8, 16)
POOL_GC = D_MODEL // len(POOL_WINDOWS)
POOL_BUF = max(POOL_WINDOWS) - 1
MOE_GROUPS = 4
MOE_EXPERTS = 4
N_EXPERTS = MOE_GROUPS * MOE_EXPERTS
N_PAIRS = MOE_EXPERTS * (MOE_EXPERTS - 1) // 2
N_BUCKETS = MOE_GROUPS * N_PAIRS
PAIR_A = (0, 0, 0, 1, 1, 2)
PAIR_B = (1, 2, 3, 2, 3, 3)
MOE_FF = 512
R_BUCKET, R_RANK = 0, 1
DEPTH = 2
ALPHA = (2 * DEPTH) ** 0.25
LN_EPS = 1e-5

LANES = 128
SUBLANES = 8
CHUNKS = D_MODEL // LANES
A_PAD = LANES
Z_WIDTH = 2 * GLA_KW + 2 * GLA_VW + 2 * CONV_CH + A_PAD
NEG_BIG = -1e30
VMEM_LIMIT = 56 * 1024 * 1024

TILE_L = 256
CONV_HALO = 32
CONV_W_ROWS = 32
POOL_HALO = 16
SAMPLE_BLK = 16
MOE_TILE = 256
DISPATCH_ROWS = 640


def _dot(a, b):
    return jnp.dot(a, b, preferred_element_type=F32)


def _dot_nt(a, b):
    return lax.dot_general(a, b, (((1,), (1,)), ((), ())), preferred_element_type=F32)


def _split3(x):
    hi = x.astype(BF16)
    r1 = x - hi.astype(F32)
    mid = r1.astype(BF16)
    lo = (r1 - mid.astype(F32)).astype(BF16)
    return hi, mid, lo


def _split2(x):
    hi = x.astype(BF16)
    lo = (x - hi.astype(F32)).astype(BF16)
    return hi, lo


def _dot_w3(a, w_hi, w_lo):
    a_hi, a_lo = _split2(a)
    return _dot(a_hi, w_hi) + (_dot(a_lo, w_hi) + _dot(a_hi, w_lo))


def _lhs3(parts):
    hi, lo = parts
    return jnp.concatenate([hi, lo, hi], axis=1)


def _rhs3_rows(x):
    hi, lo = _split2(x)
    return jnp.concatenate([hi, hi, lo], axis=0)


def _rhs3_lanes(parts):
    hi, lo = parts
    return jnp.concatenate([hi, hi, lo], axis=1)


def _layer_norm(x, g, b):
    mu = jnp.mean(x, axis=-1, keepdims=True)
    xc = x - mu
    var = jnp.mean(xc * xc, axis=-1, keepdims=True)
    return xc * lax.rsqrt(var + LN_EPS) * g + b


def _standardize(x):
    mu = jnp.mean(x, axis=-1, keepdims=True)
    xc = x - mu
    var = jnp.mean(xc * xc, axis=-1, keepdims=True)
    return xc * lax.rsqrt(var + LN_EPS)


def _sigmoid(x):
    return 1.0 / (1.0 + jnp.exp(-x))


def _silu(x):
    return x * _sigmoid(x)


def _log_sigmoid(x):
    return jnp.minimum(x, 0.0) - jnp.log(1.0 + jnp.exp(-jnp.abs(x)))


def _mod3(mod):
    return mod[:, 0:D_MODEL], mod[:, D_MODEL:2 * D_MODEL], mod[:, 2 * D_MODEL:3 * D_MODEL]


def _full(shape):
    nd = len(shape)
    return pl.BlockSpec(shape, lambda *_: (0,) * nd)


def _const(shape):
    nd = len(shape)
    return pl.BlockSpec(shape, lambda *_: (0,) * nd, pipeline_mode=pl.Buffered(1))


def _hi_lo(w):
    bits = lax.bitcast_convert_type(w, jnp.uint32) & jnp.uint32(0xFFFF0000)
    hi = lax.bitcast_convert_type(bits, F32)
    return jnp.stack([hi.astype(BF16), (w - hi).astype(BF16)])


def _params(*sem):
    return pltpu.CompilerParams(dimension_semantics=sem, vmem_limit_bytes=VMEM_LIMIT)


def _ada_kernel(c_ref, w_ref, b_ref, o_ref):
    w_hi, w_lo = _split2(w_ref[0])
    o_ref[0] = _dot_w3(_silu(c_ref[...]), w_hi, w_lo) + b_ref[0]


def _ada_call(c_all, w_ada, b_ada):
    n = c_all.shape[0]
    ncomb = w_ada.shape[0]
    tn = D_MODEL
    return pl.pallas_call(
        _ada_kernel,
        grid=(ncomb, 3 * D_MODEL // tn),
        in_specs=[
            pl.BlockSpec((n, D_MODEL), lambda i, j: (0, 0)),
            pl.BlockSpec((1, D_MODEL, tn), lambda i, j: (i, 0, j)),
            pl.BlockSpec((1, 1, tn), lambda i, j: (i, 0, j)),
        ],
        out_specs=pl.BlockSpec((1, n, tn), lambda i, j: (i, 0, j)),
        out_shape=jax.ShapeDtypeStruct((ncomb, n, 3 * D_MODEL), F32),
        compiler_params=_params("arbitrary", "arbitrary"),
        name="ada",
    )(c_all, w_ada, b_ada)


def _route(logits, carry):
    rows = logits.shape[0]
    lane = lax.broadcasted_iota(I32, (rows, LANES), 1)
    lanef = lane.astype(F32)
    big = float(LANES)
    lc = jnp.where(lane < MOE_GROUPS, logits, NEG_BIG)
    mc = jnp.max(lc, axis=-1, keepdims=True)
    gidx = jnp.min(jnp.where(lc == mc, lanef, big), axis=-1, keepdims=True)
    lo = float(MOE_GROUPS) + gidx * float(MOE_EXPERTS)
    in_grp = (lanef >= lo) & (lanef < lo + float(MOE_EXPERTS))
    lf = jnp.where(in_grp, logits, NEG_BIG)
    m1 = jnp.max(lf, axis=-1, keepdims=True)
    i1 = jnp.min(jnp.where(lf == m1, lanef, big), axis=-1, keepdims=True)
    lf2 = jnp.where(lanef == i1, NEG_BIG, lf)
    m2 = jnp.max(lf2, axis=-1, keepdims=True)
    i2 = jnp.min(jnp.where(lf2 == m2, lanef, big), axis=-1, keepdims=True)
    ea = jnp.minimum(i1, i2) - lo
    eb = jnp.maximum(i1, i2) - lo
    bucket = gidx * float(N_PAIRS) + ea * (7.0 - ea) * 0.5 + eb - ea - 1.0
    onehot = jnp.where(lanef == bucket, 1.0, 0.0)
    r = lax.broadcasted_iota(I32, (rows, rows), 0)
    c = lax.broadcasted_iota(I32, (rows, rows), 1)
    before = _dot(jnp.where(r > c, 1.0, 0.0).astype(BF16), onehot.astype(BF16))
    rank = jnp.sum(onehot * (before + carry), axis=-1, keepdims=True)
    new_carry = carry + jnp.sum(onehot, axis=0, keepdims=True)
    cols = jnp.where(lane == R_BUCKET, bucket, 0.0) + jnp.where(lane == R_RANK, rank, 0.0)
    return jnp.transpose(cols)[0:SUBLANES, :], new_carry


def _store_chunked(ref, x):
    rows = x.shape[0]
    for c in range(CHUNKS):
        ref[pl.ds(c, rows, stride=CHUNKS), :] = x[:, c * LANES:(c + 1) * LANES]


def _load_chunked(ref, rows, group, first):
    return jnp.concatenate(
        [ref[pl.ds(first + c, rows, stride=group), :] for c in range(CHUNKS)], axis=1)


def _router_logits(h2, wr_hi_ref, wr_lo_ref, br_ref):
    h_hi, h_lo = _split2(h2)
    wr_hi = wr_hi_ref[...]
    return _dot(h_hi, wr_hi) + _dot(h_lo, wr_hi) + _dot(h_hi, wr_lo_ref[...]) + br_ref[...]


def _post_mixer(x, m, gate, lng_ref, lnb_ref, mod_b, wr_hi_ref, wr_lo_ref, br_ref, carry,
                x1_ref, h2c_ref, route_ref):
    x1 = _layer_norm(ALPHA * x + (1.0 + gate) * m, lng_ref[...], lnb_ref[...])
    shift2, scale2, _ = _mod3(mod_b)
    h2 = x1 * (1.0 + scale2) + shift2
    pad = route_ref.shape[1] - x.shape[0]
    if pad:
        h2 = jnp.concatenate([h2, jnp.zeros((pad, D_MODEL), F32)], axis=0)
    route, carry = _route(_router_logits(h2, wr_hi_ref, wr_lo_ref, br_ref), carry)
    x1_ref[...] = x1
    _store_chunked(h2c_ref, h2)
    route_ref[...] = route
    return carry


def _token_out_specs(rows, slot_rows, x1_map, slot_map):
    return [pl.BlockSpec((rows, D_MODEL), x1_map),
            pl.BlockSpec((slot_rows * CHUNKS, LANES), slot_map),
            pl.BlockSpec((SUBLANES, slot_rows), lambda *i: (0, x1_map(*i)[0]))]


def _token_out_shapes(ntok, nslot_h2, nslot_route):
    return [jax.ShapeDtypeStruct((ntok, D_MODEL), F32),
            jax.ShapeDtypeStruct((nslot_h2 * CHUNKS, LANES), F32),
            jax.ShapeDtypeStruct((SUBLANES, nslot_route), F32)]


def _hold_rows(b, s):
    n, w = b.shape
    if s >= SUBLANES:
        pieces = []
        for p in range(n // (2 * s)):
            r = p * 2 * s + s - 1
            pieces.append(jnp.broadcast_to(b[r:r + 1, :], (2 * s, w)))
        return pieces[0] if len(pieces) == 1 else jnp.concatenate(pieces, axis=0)
    b3 = b.reshape(n // SUBLANES, SUBLANES, w)
    sub = lax.broadcasted_iota(I32, b3.shape, 1)

    def bc(r):
        return jnp.broadcast_to(b3[:, r:r + 1, :], b3.shape)

    out = bc(s - 1)
    for p in range(1, SUBLANES // (2 * s)):
        out = jnp.where(sub >= p * 2 * s, bc(p * 2 * s + s - 1), out)
    return out.reshape(n, w)


def _gla_tile(q, k, v, la, s_prev):
    n = q.shape[0]
    row = lax.broadcasted_iota(I32, (n, n), 0)
    col = lax.broadcasted_iota(I32, (n, n), 1)
    tri = jnp.where(row >= col, 1.0, 0.0).astype(BF16)
    hi, mid, lo = _split3(la)
    b = _dot(tri, hi) + _dot(tri, mid) + _dot(tri, lo)
    b_last = b[n - 1:n, :]
    kl = k * jnp.exp(b_last - b)

    rowi = lax.broadcasted_iota(I32, (n, GLA_KW), 0)
    levels = []
    s = n // 2
    while s >= 1:
        m = _hold_rows(b, s)
        second = (rowi & s) != 0
        qs = _split2(q * jnp.exp(jnp.where(second, b - m, NEG_BIG)))
        ks = _split2(k * jnp.exp(jnp.where(second, NEG_BIG, m - b)))
        levels.append((2 * s, qs, ks))
        s //= 2
    levels.append((1, _split2(q), _split2(k)))
    qe = _split2(q * jnp.exp(b))

    lane = lax.broadcasted_iota(I32, (n, LANES), 1)
    xor = row ^ col
    zero = jnp.zeros((), BF16)

    def head_part(parts, sl, hm):
        return tuple(jnp.where(hm, x[:, sl], zero) for x in parts)

    outs = []
    for h in range(GLA_HEADS):
        p = h // 2
        sl = slice(p * LANES, (p + 1) * LANES)
        hm = (lane < GLA_DK) if h % 2 == 0 else (lane >= GLA_DK)
        att = None
        for span, qs, ks in levels:
            a_s = _dot_nt(_lhs3(head_part(qs, sl, hm)), _rhs3_lanes((ks[0][:, sl], ks[1][:, sl])))
            att = a_s if att is None else jnp.where(xor < span, a_s, att)
        o_h = _dot(_lhs3(_split2(att)), _rhs3_rows(v[:, h * GLA_DV:(h + 1) * GLA_DV]))
        o_h = o_h + _dot(_lhs3(head_part(qe, sl, hm)), _rhs3_rows(s_prev[sl, :]))
        outs.append(o_h)

    decay = jnp.exp(b_last)
    upper = lax.broadcasted_iota(I32, (LANES, LANES), 0) < GLA_DK
    s_new = []
    for p in range(GLA_HEADS // 2):
        sl = slice(p * LANES, (p + 1) * LANES)
        kl_t = jnp.transpose(kl[:, sl])
        u = _dot(_lhs3(_split2(kl_t)), _rhs3_rows(v[:, p * 2 * GLA_DV:(p + 1) * 2 * GLA_DV]))
        upd = jnp.where(upper, u[:, 0:GLA_DV], u[:, GLA_DV:2 * GLA_DV])
        dcol = jnp.transpose(jnp.broadcast_to(decay[:, sl], (LANES, LANES)))
        s_new.append(dcol * s_prev[sl, :] + upd)
    return outs, jnp.concatenate(s_new, axis=0)


def _split_projection(z):
    c0 = 0
    q = z[:, c0:c0 + GLA_KW] * (GLA_DK ** -0.5); c0 += GLA_KW
    k = z[:, c0:c0 + GLA_KW]; c0 += GLA_KW
    v = z[:, c0:c0 + GLA_VW]; c0 += GLA_VW
    g = z[:, c0:c0 + GLA_VW]; c0 += GLA_VW
    ua = z[:, c0:c0 + CONV_CH]; c0 += CONV_CH
    ug = z[:, c0:c0 + CONV_CH]; c0 += CONV_CH
    a_lr = z[:, c0:c0 + A_PAD]
    return q, k, v, g, ua, ug, a_lr


def _mix0_project(o_heads, g, y, gng_ref, w_out_ref):
    sl = slice(GLA_VW, GLA_VW + CONV_CH)
    m = _dot_w3(y, w_out_ref[0, sl, :], w_out_ref[1, sl, :])
    for hd in range(GLA_HEADS):
        sl = slice(hd * GLA_DV, (hd + 1) * GLA_DV)
        o_h = _standardize(o_heads[hd]) * gng_ref[:, sl] * _silu(g[:, sl])
        m = m + _dot_w3(o_h, w_out_ref[0, sl, :], w_out_ref[1, sl, :])
    return m


def _mix0_inputs(h, w_in_ref, w_a2_ref, b_a_ref):
    z = _dot_w3(h, w_in_ref[0], w_in_ref[1])
    q, k, v, g, ua, ug, a_lr = _split_projection(z)
    la = _log_sigmoid(_dot_w3(a_lr, w_a2_ref[0], w_a2_ref[1]) + b_a_ref[...]) * (1.0 / GLA_TAU)
    return q, k, v, g, ua, ug, la


def _prompt_grid_specs(bsz, nt):
    ntile = bsz * nt
    seq_map = lambda b, t: (jnp.minimum(b, bsz - 1), 0, 0)
    x_map = lambda b, t: (jnp.minimum(b, bsz - 1), t, 0)
    tok_map = lambda b, t: (jnp.minimum(b * nt + t, ntile - 1), 0)
    slot_map = lambda b, t: (jnp.minimum(b * nt + t, ntile), 0)
    state_map = lambda b, t: (0, jnp.minimum(b, bsz - 1), 0, 0)
    return seq_map, x_map, tok_map, slot_map, state_map


def _mixer0_prompt_kernel(x_ref, moda_ref, modb_ref, cnt_in_ref, h2cs_ref, w_in_ref, w_a2_ref,
                          b_a_ref, gng_ref,
                          cw_ref, cb_ref, clg_ref, clb_ref, w_out_ref, lng_ref, lnb_ref,
                          wr_hi_ref, wr_lo_ref, br_ref,
                          x1_ref, h2c_ref, route_ref, cnt_ref, nconv_ref, ngla_ref,
                          s_ref, cbuf_ref, shift_ref, carry_ref):
    nseq = pl.num_programs(0) - 1

    @pl.when(pl.program_id(0) < nseq)
    def _():
        _mixer0_prompt_tile(x_ref, moda_ref, modb_ref, cnt_in_ref, w_in_ref, w_a2_ref, b_a_ref,
                            gng_ref, cw_ref, cb_ref, clg_ref, clb_ref, w_out_ref, lng_ref, lnb_ref,
                            wr_hi_ref, wr_lo_ref, br_ref, x1_ref, h2c_ref, route_ref, cnt_ref,
                            nconv_ref, ngla_ref, s_ref, cbuf_ref, shift_ref, carry_ref)

    @pl.when((pl.program_id(0) == nseq) & (pl.program_id(1) == 0))
    def _():
        h2c_ref[...] = h2cs_ref[...]


def _mixer0_prompt_tile(x_ref, moda_ref, modb_ref, cnt_in_ref, w_in_ref, w_a2_ref, b_a_ref,
                        gng_ref, cw_ref, cb_ref, clg_ref, clb_ref, w_out_ref, lng_ref, lnb_ref,
                        wr_hi_ref, wr_lo_ref, br_ref, x1_ref, h2c_ref, route_ref, cnt_ref,
                        nconv_ref, ngla_ref, s_ref, cbuf_ref, shift_ref, carry_ref):
    t = pl.program_id(1)
    n = x_ref.shape[1]

    @pl.when((t == 0) & (pl.program_id(0) == 0))
    def _():
        carry_ref[...] = cnt_in_ref[...]

    @pl.when(t == 0)
    def _():
        s_ref[...] = jnp.zeros_like(s_ref)
        cbuf_ref[0:CONV_HALO, :] = jnp.zeros((CONV_HALO, CONV_CH), F32)

    x = x_ref[0]
    shift, scale, gate = _mod3(moda_ref[0])
    h = x * (1.0 + scale) + shift
    q, k, v, g, ua, ug, la = _mix0_inputs(h, w_in_ref, w_a2_ref, b_a_ref)
    o_heads, s_new = _gla_tile(q, k, v, la, s_ref[...])
    s_ref[...] = s_new

    glu = ua * _sigmoid(ug)
    cbuf_ref[CONV_HALO:CONV_HALO + n, :] = glu
    acc = jnp.broadcast_to(cb_ref[...], (n, CONV_CH))
    base = CONV_HALO - CONV_BUF
    for r in range(SUBLANES):
        taps = range(r, CONV_WIDTH, SUBLANES)
        span = n + (len(taps) - 1) * SUBLANES
        if (base + r) % SUBLANES == 0:
            src, off = cbuf_ref, base + r
        else:
            shift_ref[r, 0:span, :] = cbuf_ref[base + r:base + r + span, :]
            src, off = shift_ref.at[r], 0
        for a, j in enumerate(taps):
            lo = off + a * SUBLANES
            acc = acc + cw_ref[j:j + 1, :] * src[lo:lo + n, :]
    y = _silu(_layer_norm(acc, clg_ref[...], clb_ref[...]))

    @pl.when(t == pl.num_programs(1) - 1)
    def _():
        nconv_ref[0, 0] = cbuf_ref[CONV_HALO + n - CONV_BUF:CONV_HALO + n, :]
        ngla_ref[0, 0] = s_new.reshape(GLA_HEADS, GLA_DK, GLA_DV)

    cbuf_ref[0:CONV_HALO, :] = cbuf_ref[n:n + CONV_HALO, :]

    m = _mix0_project(o_heads, g, y, gng_ref, w_out_ref)
    carry = _post_mixer(x, m, gate, lng_ref, lnb_ref, modb_ref[0], wr_hi_ref, wr_lo_ref, br_ref,
                        carry_ref[...], x1_ref, h2c_ref, route_ref)
    carry_ref[...] = carry
    cnt_ref[...] = carry


def _mixer0_prompt_call(x, mod_a, mod_b, cnt, h2c_s, p):
    bsz, seq, _ = x.shape
    tl = TILE_L
    nt = seq // tl
    seq_map, x_map, tok, slot, state_map = _prompt_grid_specs(bsz, nt)
    return pl.pallas_call(
        _mixer0_prompt_kernel,
        grid=(bsz + 1, nt),
        in_specs=[
            pl.BlockSpec((1, tl, D_MODEL), x_map),
            pl.BlockSpec((1, 1, 3 * D_MODEL), seq_map),
            pl.BlockSpec((1, 1, 3 * D_MODEL), seq_map),
            _full((1, LANES)),
            _full((tl * CHUNKS, LANES)),
            _const((2, D_MODEL, Z_WIDTH)),
            _const((2, A_PAD, GLA_KW)),
            _full((1, GLA_KW)),
            _full((1, GLA_VW)),
            _full((CONV_W_ROWS, CONV_CH)),
            _full((1, CONV_CH)),
            _full((1, CONV_CH)),
            _full((1, CONV_CH)),
            _const((2, GLA_VW + CONV_CH, D_MODEL)),
            _full((1, D_MODEL)),
            _full((1, D_MODEL)),
            _full((D_MODEL, LANES)),
            _full((D_MODEL, LANES)),
            _full((1, LANES)),
        ],
        out_specs=_token_out_specs(tl, tl, tok, slot) + [
            _full((1, LANES)),
            pl.BlockSpec((1, 1, CONV_BUF, CONV_CH), state_map),
            pl.BlockSpec((1, 1, GLA_HEADS, GLA_DK, GLA_DV), lambda b, t: state_map(b, t) + (0,)),
        ],
        out_shape=_token_out_shapes(bsz * seq, bsz * seq + tl, bsz * seq) + [
            jax.ShapeDtypeStruct((1, LANES), F32),
            jax.ShapeDtypeStruct((1, bsz, CONV_BUF, CONV_CH), F32),
            jax.ShapeDtypeStruct((1, bsz, GLA_HEADS, GLA_DK, GLA_DV), F32),
        ],
        scratch_shapes=[
            pltpu.VMEM((GLA_KW, GLA_DV), F32),
            pltpu.VMEM((CONV_HALO + tl, CONV_CH), F32),
            pltpu.VMEM((SUBLANES, CONV_HALO + tl, CONV_CH), F32),
            pltpu.VMEM((1, LANES), F32),
        ],
        compiler_params=_params("arbitrary", "arbitrary"),
        name="mixer0_prompt",
    )(x, mod_a, mod_b, cnt, h2c_s, p["w_in"], p["w_a2"], p["b_a"], p["gng"], p["conv_w"], p["conv_b"],
      p["conv_ln_g"], p["conv_ln_b"], p["w_out0"], p["ln_g00"], p["ln_b00"],
      p["wr_hi0"], p["wr_lo0"], p["br0"])


def _mixer0_sample_kernel(x_ref, moda_ref, modb_ref, sgla_ref, sconv_ref, cnt_in_ref,
                          w_in_ref, w_a2_ref,
                          b_a_ref, gng_ref, cw_ref, cb_ref, clg_ref, clb_ref, w_out_ref, lng_ref,
                          lnb_ref, wr_hi_ref, wr_lo_ref, br_ref,
                          x1_ref, h2c_ref, route_ref, cnt_ref, ngla_ref, nconv_ref,
                          zt_ref, v_ref, g_ref, glu_ref, o_ref, y_ref):
    i = pl.program_id(0)
    nb = sgla_ref.shape[1]
    ntok = x_ref.shape[0]

    @pl.when(i == 0)
    def _():
        shift, scale, _ = _mod3(moda_ref[...])
        h = x_ref[...] * (1.0 + scale) + shift
        q, k, v, g, ua, ug, la = _mix0_inputs(h, w_in_ref, w_a2_ref, b_a_ref)
        v_ref[...] = v
        g_ref[...] = g
        glu_ref[...] = ua * _sigmoid(ug)
        for j, val in enumerate((jnp.exp(la), k, q)):
            hi, lo = _split2(jnp.transpose(val))
            zt_ref[(2 * j) * GLA_KW:(2 * j + 1) * GLA_KW, :] = hi
            zt_ref[(2 * j + 1) * GLA_KW:(2 * j + 2) * GLA_KW, :] = lo

    tok_row = lax.broadcasted_iota(I32, (ntok, LANES), 0)
    blk = pl.ds(pl.multiple_of(i * nb, nb), nb)
    v_blk = v_ref[blk, :]
    glu_blk = glu_ref[blk, :]
    o_rows, y_rows = [], []
    for n in range(nb):
        onehot = jnp.where(tok_row == i * nb + n, 1.0, 0.0).astype(BF16)
        cols = _dot(zt_ref[...], onehot)
        a_col = cols[0:GLA_KW] + cols[GLA_KW:2 * GLA_KW]
        k_col = cols[2 * GLA_KW:3 * GLA_KW] + cols[3 * GLA_KW:4 * GLA_KW]
        q_col = cols[4 * GLA_KW:5 * GLA_KW] + cols[5 * GLA_KW:6 * GLA_KW]
        v_row = v_blk[n:n + 1, :]
        v_b = jnp.concatenate(
            [jnp.broadcast_to(v_row[:, h * GLA_DV:(h + 1) * GLA_DV], (GLA_DK, GLA_DV))
             for h in range(GLA_HEADS)], axis=0)
        s_old = sgla_ref[0, n].reshape(GLA_KW, GLA_DV)
        s_new = a_col * s_old + k_col * v_b
        ngla_ref[0, n] = s_new.reshape(GLA_HEADS, GLA_DK, GLA_DV)
        o4 = jnp.sum((q_col * s_new).reshape(GLA_HEADS, GLA_DK, GLA_DV), axis=1)
        o_rows.append(jnp.concatenate([o4[h:h + 1, :] for h in range(GLA_HEADS)], axis=1))
        glu_row = glu_blk[n:n + 1, :]
        past = sconv_ref[0, n]
        y_rows.append(jnp.sum(past * cw_ref[0:CONV_BUF, :], axis=0, keepdims=True)
                      + glu_row * cw_ref[CONV_BUF:CONV_WIDTH, :] + cb_ref[...])
        nconv_ref[0, n, 0:CONV_BUF - 1, :] = sconv_ref[0, n, 1:CONV_BUF, :]
        nconv_ref[0, n, CONV_BUF - 1:CONV_BUF, :] = glu_row
    o_ref[blk, :] = jnp.concatenate(o_rows, axis=0)
    y_ref[blk, :] = jnp.concatenate(y_rows, axis=0)

    @pl.when(i == pl.num_programs(0) - 1)
    def _():
        y = _silu(_layer_norm(y_ref[...], clg_ref[...], clb_ref[...]))
        o_heads = [o_ref[:, hd * GLA_DV:(hd + 1) * GLA_DV] for hd in range(GLA_HEADS)]
        m = _mix0_project(o_heads, g_ref[...], y, gng_ref, w_out_ref)
        _, _, gate = _mod3(moda_ref[...])
        cnt_ref[...] = _post_mixer(x_ref[...], m, gate, lng_ref, lnb_ref, modb_ref[...], wr_hi_ref,
                                   wr_lo_ref, br_ref, cnt_in_ref[...], x1_ref, h2c_ref, route_ref)


def _mixer0_sample_call(x, mod_a, mod_b, state_gla, state_conv, cnt, p):
    ntok = x.shape[0]
    nb = SAMPLE_BLK
    tok = lambda i: (0, 0)
    return pl.pallas_call(
        _mixer0_sample_kernel,
        grid=(ntok // nb,),
        in_specs=[
            _full((ntok, D_MODEL)),
            _full((ntok, 3 * D_MODEL)),
            _full((ntok, 3 * D_MODEL)),
            pl.BlockSpec((1, nb, GLA_HEADS, GLA_DK, GLA_DV), lambda i: (0, i, 0, 0, 0)),
            pl.BlockSpec((1, nb, CONV_BUF, CONV_CH), lambda i: (0, i, 0, 0)),
            _full((1, LANES)),
            _const((2, D_MODEL, Z_WIDTH)),
            _const((2, A_PAD, GLA_KW)),
            _full((1, GLA_KW)),
            _full((1, GLA_VW)),
            _full((CONV_W_ROWS, CONV_CH)),
            _full((1, CONV_CH)),
            _full((1, CONV_CH)),
            _full((1, CONV_CH)),
            _const((2, GLA_VW + CONV_CH, D_MODEL)),
            _full((1, D_MODEL)),
            _full((1, D_MODEL)),
            _full((D_MODEL, LANES)),
            _full((D_MODEL, LANES)),
            _full((1, LANES)),
        ],
        out_specs=_token_out_specs(ntok, TILE_L, tok, tok) + [
            _full((1, LANES)),
            pl.BlockSpec((1, nb, GLA_HEADS, GLA_DK, GLA_DV), lambda i: (0, i, 0, 0, 0)),
            pl.BlockSpec((1, nb, CONV_BUF, CONV_CH), lambda i: (0, i, 0, 0)),
        ],
        out_shape=_token_out_shapes(ntok, TILE_L, TILE_L) + [
            jax.ShapeDtypeStruct((1, LANES), F32),
            jax.ShapeDtypeStruct(state_gla.shape, F32),
            jax.ShapeDtypeStruct(state_conv.shape, F32),
        ],
        scratch_shapes=[
            pltpu.VMEM((6 * GLA_KW, ntok), BF16),
            pltpu.VMEM((ntok, GLA_VW), F32),
            pltpu.VMEM((ntok, GLA_VW), F32),
            pltpu.VMEM((ntok, CONV_CH), F32),
            pltpu.VMEM((ntok, GLA_VW), F32),
            pltpu.VMEM((ntok, CONV_CH), F32),
        ],
        compiler_params=_params("arbitrary"),
        name="mixer0_sample",
    )(x, mod_a, mod_b, state_gla, state_conv, cnt, p["w_in"], p["w_a2"], p["b_a"],
      p["gng"], p["conv_w"], p["conv_b"], p["conv_ln_g"], p["conv_ln_b"], p["w_out0"],
      p["ln_g00"], p["ln_b00"], p["wr_hi0"], p["wr_lo0"], p["br0"])


def _pool_project(pooled, h, wg_ref, ps_ref, w_out_ref):
    m = None
    for gi in range(len(POOL_WINDOWS)):
        sl = slice(gi * POOL_GC, (gi + 1) * POOL_GC)
        mixed = _dot((pooled[:, sl] - h[:, sl]).astype(BF16), wg_ref[gi]) * ps_ref[:, sl]
        part = _dot(mixed.astype(BF16), w_out_ref[sl, :])
        m = part if m is None else m + part
    return m


def _block_input(xin_ref, m_ref, mod_m, lnpg_ref, lnpb_ref):
    m = _load_chunked(m_ref, xin_ref.shape[0], CHUNKS, 0)
    _, _, gate = _mod3(mod_m)
    return _layer_norm(ALPHA * xin_ref[...] + (1.0 + gate) * m, lnpg_ref[...], lnpb_ref[...])


def _mixer1_prompt_kernel(dest_ref, dnext_ref, xin_ref, y_any, modm_ref, lnpg_ref, lnpb_ref,
                          moda_ref, modb_ref, cnt_in_ref, h2cs_ref, wg_ref, ps_ref,
                          w_out_ref, lng_ref, lnb_ref, wr_hi_ref, wr_lo_ref, br_ref,
                          x1_ref, h2c_ref, route_ref, cnt_ref, npool_ref,
                          pbuf_ref, carry_ref, mbuf, sems):
    nseq = pl.num_programs(0) - 1
    nt = pl.num_programs(1)

    @pl.when(pl.program_id(0) < nseq)
    def _():
        m_ref = _gather_pipelined(pl.program_id(0) * nt + pl.program_id(1), nseq * nt,
                                  dest_ref, dnext_ref, y_any, mbuf, sems)
        x = _block_input(xin_ref, m_ref, modm_ref[0], lnpg_ref, lnpb_ref)
        _mixer1_prompt_tile(x, moda_ref, modb_ref, cnt_in_ref, wg_ref, ps_ref, w_out_ref,
                            lng_ref, lnb_ref, wr_hi_ref, wr_lo_ref, br_ref, x1_ref, h2c_ref,
                            route_ref, cnt_ref, npool_ref, pbuf_ref, carry_ref)

    @pl.when((pl.program_id(0) == nseq) & (pl.program_id(1) == 0))
    def _():
        h2c_ref[...] = h2cs_ref[...]


def _mixer1_prompt_tile(x, moda_ref, modb_ref, cnt_in_ref, wg_ref, ps_ref, w_out_ref,
                        lng_ref, lnb_ref, wr_hi_ref, wr_lo_ref, br_ref, x1_ref, h2c_ref,
                        route_ref, cnt_ref, npool_ref, pbuf_ref, carry_ref):
    t = pl.program_id(1)
    n = x.shape[0]

    @pl.when((t == 0) & (pl.program_id(0) == 0))
    def _():
        carry_ref[...] = cnt_in_ref[...]

    @pl.when(t == 0)
    def _():
        pbuf_ref[0:POOL_HALO, :] = jnp.zeros((POOL_HALO, D_MODEL), F32)

    shift, scale, gate = _mod3(moda_ref[0])
    h = x * (1.0 + scale) + shift
    pbuf_ref[POOL_HALO:POOL_HALO + n, :] = h

    cur = pbuf_ref[...]
    sums = []
    for gi, w in enumerate(POOL_WINDOWS):
        cur = cur + pltpu.roll(cur, w // 2, axis=0)
        sums.append(cur[POOL_HALO:, 0:POOL_GC])
        if gi + 1 < len(POOL_WINDOWS):
            cur = cur[:, POOL_GC:]
    pos = lax.broadcasted_iota(I32, (n, POOL_GC), 0) + t * n
    pooled = jnp.concatenate(
        [s / jnp.minimum(w, pos + 1).astype(F32) for s, w in zip(sums, POOL_WINDOWS)], axis=1)

    @pl.when(t == pl.num_programs(1) - 1)
    def _():
        npool_ref[0, 0] = pbuf_ref[POOL_HALO + n - POOL_BUF:POOL_HALO + n, :]

    pbuf_ref[0:POOL_HALO, :] = pbuf_ref[n:n + POOL_HALO, :]

    m = _pool_project(pooled, h, wg_ref, ps_ref, w_out_ref)
    carry = _post_mixer(x, m, gate, lng_ref, lnb_ref, modb_ref[0], wr_hi_ref, wr_lo_ref, br_ref,
                        carry_ref[...], x1_ref, h2c_ref, route_ref)
    carry_ref[...] = carry
    cnt_ref[...] = carry


def _mixer1_prompt_call(x1_prev, dest, y_sorted, mod_m, lnpg, lnpb, mod_a, mod_b, cnt, h2c_s, bsz, p):
    seq = x1_prev.shape[0] // bsz
    tl = TILE_L
    nt = seq // tl
    ntile = bsz * nt
    seq_map, _, tok, slot, state_map = _prompt_grid_specs(bsz, nt)
    ng = len(POOL_WINDOWS)
    return pl.pallas_call(
        _mixer1_prompt_kernel,
        grid=(bsz + 1, nt),
        in_specs=[
            pl.BlockSpec((1, 1, tl), lambda b, t: tok(b, t) + (0,), memory_space=pltpu.SMEM),
            pl.BlockSpec((1, 1, tl), lambda b, t: (jnp.minimum(b * nt + t + 1, ntile - 1), 0, 0),
                         memory_space=pltpu.SMEM),
            pl.BlockSpec((tl, D_MODEL), tok),
            pl.BlockSpec(memory_space=pl.ANY),
            pl.BlockSpec((1, 1, 3 * D_MODEL), seq_map),
            _full((1, D_MODEL)),
            _full((1, D_MODEL)),
            pl.BlockSpec((1, 1, 3 * D_MODEL), seq_map),
            pl.BlockSpec((1, 1, 3 * D_MODEL), seq_map),
            _full((1, LANES)),
            _full((tl * CHUNKS, LANES)),
            _full((ng, POOL_GC, POOL_GC)),
            _full((1, D_MODEL)),
            _full((D_MODEL, D_MODEL)),
            _full((1, D_MODEL)),
            _full((1, D_MODEL)),
            _full((D_MODEL, LANES)),
            _full((D_MODEL, LANES)),
            _full((1, LANES)),
        ],
        out_specs=_token_out_specs(tl, tl, tok, slot) + [
            _full((1, LANES)),
            pl.BlockSpec((1, 1, POOL_BUF, D_MODEL), state_map),
        ],
        out_shape=_token_out_shapes(bsz * seq, bsz * seq + tl, bsz * seq) + [
            jax.ShapeDtypeStruct((1, LANES), F32),
            jax.ShapeDtypeStruct((1, bsz, POOL_BUF, D_MODEL), F32),
        ],
        scratch_shapes=[pltpu.VMEM((POOL_HALO + tl, D_MODEL), F32), pltpu.VMEM((1, LANES), F32),
                        pltpu.VMEM((2, tl * CHUNKS, LANES), F32), pltpu.SemaphoreType.DMA((2,))],
        compiler_params=_params("arbitrary", "arbitrary"),
        name="mixer1_prompt",
    )(dest.reshape(ntile, 1, tl), dest.reshape(ntile, 1, tl), x1_prev, y_sorted, mod_m, lnpg, lnpb,
      mod_a, mod_b, cnt, h2c_s,
      p["w_grp"], p["pool_scale"], p["w_out1"], p["ln_g10"], p["ln_b10"],
      p["wr_hi1"], p["wr_lo1"], p["br1"])


def _mixer1_sample_kernel(dest_ref, xin_ref, y_any, modm_ref, lnpg_ref, lnpb_ref, moda_ref,
                          modb_ref, spool_ref, cnt_in_ref, wg_ref, ps_ref, w_out_ref,
                          lng_ref, lnb_ref, wr_hi_ref, wr_lo_ref, br_ref,
                          x1_ref, h2c_ref, route_ref, cnt_ref, npool_ref,
                          x_ref, h_ref, pooled_ref, mbuf, sem):
    i = pl.program_id(0)
    nb = spool_ref.shape[1]

    @pl.when(i == 0)
    def _():
        _start_row_gather(dest_ref, y_any, mbuf, sem)
        _wait_row_gather(y_any, mbuf, sem)
        x_ref[...] = _block_input(xin_ref, mbuf, modm_ref[...], lnpg_ref, lnpb_ref)
        shift, scale, _ = _mod3(moda_ref[...])
        h_ref[...] = x_ref[...] * (1.0 + scale) + shift

    lane = lax.broadcasted_iota(I32, (POOL_BUF, D_MODEL), 1)
    rowi = lax.broadcasted_iota(I32, (POOL_BUF, D_MODEL), 0)
    first = jnp.zeros((POOL_BUF, D_MODEL), I32)
    lane1 = lax.broadcasted_iota(I32, (1, D_MODEL), 1)
    inv_w = jnp.zeros((1, D_MODEL), F32)
    for gi, w in enumerate(POOL_WINDOWS):
        in_g = (lane >= gi * POOL_GC) & (lane < (gi + 1) * POOL_GC)
        first = jnp.where(in_g, POOL_BUF - (w - 1), first)
        in_g1 = (lane1 >= gi * POOL_GC) & (lane1 < (gi + 1) * POOL_GC)
        inv_w = jnp.where(in_g1, 1.0 / w, inv_w)
    keep = rowi >= first

    blk = pl.ds(pl.multiple_of(i * nb, nb), nb)
    h_blk = h_ref[blk, :]
    rows = []
    for n in range(nb):
        h_row = h_blk[n:n + 1, :]
        past = spool_ref[0, n]
        total = jnp.sum(jnp.where(keep, past, 0.0), axis=0, keepdims=True) + h_row
        rows.append(total * inv_w)
        npool_ref[0, n, 0:POOL_BUF - 1, :] = spool_ref[0, n, 1:POOL_BUF, :]
        npool_ref[0, n, POOL_BUF - 1:POOL_BUF, :] = h_row
    pooled_ref[blk, :] = jnp.concatenate(rows, axis=0)

    @pl.when(i == pl.num_programs(0) - 1)
    def _():
        m = _pool_project(pooled_ref[...], h_ref[...], wg_ref, ps_ref, w_out_ref)
        _, _, gate = _mod3(moda_ref[...])
        cnt_ref[...] = _post_mixer(x_ref[...], m, gate, lng_ref, lnb_ref, modb_ref[...], wr_hi_ref,
                                   wr_lo_ref, br_ref, cnt_in_ref[...], x1_ref, h2c_ref, route_ref)


def _mixer1_sample_call(x1_prev, dest, y_sorted, mod_m, lnpg, lnpb, mod_a, mod_b, state_pool,
                        cnt, p):
    ntok = x1_prev.shape[0]
    nb = SAMPLE_BLK
    ng = len(POOL_WINDOWS)
    tok = lambda i: (0, 0)
    return pl.pallas_call(
        _mixer1_sample_kernel,
        grid=(ntok // nb,),
        in_specs=[
            pl.BlockSpec((1, 1, ntok), lambda i: (0, 0, 0), memory_space=pltpu.SMEM),
            _full((ntok, D_MODEL)),
            pl.BlockSpec(memory_space=pl.ANY),
            _full((ntok, 3 * D_MODEL)),
            _full((1, D_MODEL)),
            _full((1, D_MODEL)),
            _full((ntok, 3 * D_MODEL)),
            _full((ntok, 3 * D_MODEL)),
            pl.BlockSpec((1, nb, POOL_BUF, D_MODEL), lambda i: (0, i, 0, 0)),
            _full((1, LANES)),
            _full((ng, POOL_GC, POOL_GC)),
            _full((1, D_MODEL)),
            _full((D_MODEL, D_MODEL)),
            _full((1, D_MODEL)),
            _full((1, D_MODEL)),
            _full((D_MODEL, LANES)),
            _full((D_MODEL, LANES)),
            _full((1, LANES)),
        ],
        out_specs=_token_out_specs(ntok, TILE_L, tok, tok) + [
            _full((1, LANES)),
            pl.BlockSpec((1, nb, POOL_BUF, D_MODEL), lambda i: (0, i, 0, 0)),
        ],
        out_shape=_token_out_shapes(ntok, TILE_L, TILE_L) + [
            jax.ShapeDtypeStruct((1, LANES), F32),
            jax.ShapeDtypeStruct(state_pool.shape, F32),
        ],
        scratch_shapes=[pltpu.VMEM((ntok, D_MODEL), F32)] * 3 + [
            pltpu.VMEM((ntok * CHUNKS, LANES), F32), pltpu.SemaphoreType.DMA],
        compiler_params=_params("arbitrary"),
        name="mixer1_sample",
    )(dest.reshape(1, 1, ntok), x1_prev, y_sorted, mod_m, lnpg, lnpb, mod_a, mod_b, state_pool, cnt,
      p["w_grp"], p["pool_scale"], p["w_out1"],
      p["ln_g10"], p["ln_b10"], p["wr_hi1"], p["wr_lo1"], p["br1"])


def _dispatch_kernel(dest_ref, src_ref, init_any, dst_any, sem):
    del init_any
    n = dest_ref.shape[2]
    for j in range(n):
        d = dest_ref[0, 0, j]
        pltpu.make_async_copy(
            src_ref.at[pl.ds(j * CHUNKS, CHUNKS)],
            dst_any.at[pl.ds(pl.multiple_of(d * CHUNKS, CHUNKS), CHUNKS)], sem).start()
    pltpu.make_async_copy(src_ref, dst_any.at[pl.ds(0, n * CHUNKS)], sem).wait()


def _dispatch_call(dest, h2c_all, sorted_buf):
    nslot = dest.shape[1]
    rows = DISPATCH_ROWS
    return pl.pallas_call(
        _dispatch_kernel,
        grid=(nslot // rows,),
        in_specs=[
            pl.BlockSpec((1, 1, rows), lambda i: (i, 0, 0), memory_space=pltpu.SMEM),
            pl.BlockSpec((rows * CHUNKS, LANES), lambda i: (i, 0)),
            pl.BlockSpec(memory_space=pl.ANY),
        ],
        out_specs=pl.BlockSpec(memory_space=pl.ANY),
        out_shape=jax.ShapeDtypeStruct(sorted_buf.shape, F32),
        scratch_shapes=[pltpu.SemaphoreType.DMA],
        input_output_aliases={2: 0},
        compiler_params=_params("arbitrary"),
        name="dispatch",
    )(dest.reshape(nslot // rows, 1, rows), h2c_all, sorted_buf)


S_BLK, S_EA, S_EB, S_VALID, S_FIRST, S_LANE_G, S_LANE_A, S_LANE_B = range(8)


def _moe_kernel(sched_ref, x_ref, wr_hi_ref, wr_lo_ref, br_ref,
                wga_ref, wua_ref, wda_ref, wgb_ref, wub_ref, wdb_ref,
                y_ref, wgu_ref, wd_ref):
    i = pl.program_id(0)
    tm = x_ref.shape[0] // CHUNKS

    @pl.when(sched_ref[S_FIRST, i] == 1)
    def _():
        for e, (wg, wu, wd) in enumerate(((wga_ref, wua_ref, wda_ref), (wgb_ref, wub_ref, wdb_ref))):
            wgu_ref[2 * e] = wg[0].astype(BF16)
            wgu_ref[2 * e + 1] = wu[0].astype(BF16)
            wd_ref[e * MOE_FF:(e + 1) * MOE_FF, :] = wd[0].astype(BF16)

    @pl.when(sched_ref[S_VALID, i] == 0)
    def _():
        y_ref[...] = jnp.zeros_like(y_ref)

    @pl.when(sched_ref[S_VALID, i] == 1)
    def _():
        x = _load_chunked(x_ref, tm, CHUNKS, 0)
        logits = _router_logits(x, wr_hi_ref, wr_lo_ref, br_ref)
        lane = lax.broadcasted_iota(I32, logits.shape, 1)

        def pick(row):
            return jnp.sum(jnp.where(lane == sched_ref[row, i], logits, 0.0), axis=-1, keepdims=True)

        l_g, l_a, l_b = pick(S_LANE_G), pick(S_LANE_A), pick(S_LANE_B)
        p_g = 1.0 / jnp.sum(jnp.where(lane < MOE_GROUPS, jnp.exp(logits - l_g), 0.0),
                            axis=-1, keepdims=True)
        w_ab = (p_g / (1.0 + jnp.exp(l_b - l_a)), p_g / (1.0 + jnp.exp(l_a - l_b)))
        xb = x.astype(BF16)
        hid = [(_silu(_dot(xb, wgu_ref[2 * e])) * _dot(xb, wgu_ref[2 * e + 1]) * w_ab[e]).astype(BF16)
               for e in range(2)]
        _store_chunked(y_ref, _dot(jnp.concatenate(hid, axis=1), wd_ref[...]))


def _moe_call(sched, sorted_x, wr_hi, wr_lo, br, wg, wu, wd):
    tm = MOE_TILE
    ntile = sorted_x.shape[0] // (tm * CHUNKS)
    ea = lambda i, s: (s[S_EA, i], 0, 0)
    eb = lambda i, s: (s[S_EB, i], 0, 0)
    const2 = lambda i, s: (0, 0)
    grid_spec = pltpu.PrefetchScalarGridSpec(
        num_scalar_prefetch=1,
        grid=(ntile,),
        in_specs=[
            pl.BlockSpec((tm * CHUNKS, LANES), lambda i, s: (s[S_BLK, i], 0)),
            pl.BlockSpec((D_MODEL, LANES), const2),
            pl.BlockSpec((D_MODEL, LANES), const2),
            pl.BlockSpec((1, LANES), const2),
            pl.BlockSpec((1, D_MODEL, MOE_FF), ea),
            pl.BlockSpec((1, D_MODEL, MOE_FF), ea),
            pl.BlockSpec((1, MOE_FF, D_MODEL), ea),
            pl.BlockSpec((1, D_MODEL, MOE_FF), eb),
            pl.BlockSpec((1, D_MODEL, MOE_FF), eb),
            pl.BlockSpec((1, MOE_FF, D_MODEL), eb),
        ],
        out_specs=pl.BlockSpec((tm * CHUNKS, LANES), lambda i, s: (i, 0)),
        scratch_shapes=[
            pltpu.VMEM((4, D_MODEL, MOE_FF), BF16),
            pltpu.VMEM((2 * MOE_FF, D_MODEL), BF16),
        ],
    )
    return pl.pallas_call(
        _moe_kernel,
        grid_spec=grid_spec,
        out_shape=jax.ShapeDtypeStruct((ntile * tm * CHUNKS, LANES), F32),
        compiler_params=_params("arbitrary"),
        name="moe",
    )(sched, sorted_x, wr_hi, wr_lo, br, wg, wu, wd, wg, wu, wd)


def _start_row_gather(dest_ref, y_any, buf, sem):
    for j in range(dest_ref.shape[2]):
        d = dest_ref[0, 0, j]
        pltpu.make_async_copy(
            y_any.at[pl.ds(pl.multiple_of(d * CHUNKS, CHUNKS), CHUNKS)],
            buf.at[pl.ds(j * CHUNKS, CHUNKS)], sem).start()


def _wait_row_gather(y_any, buf, sem):
    pltpu.make_async_copy(y_any.at[pl.ds(0, buf.shape[0])], buf, sem).wait()


def _gather_pipelined(step, nstep, dest_ref, dnext_ref, y_any, mbuf, sems):
    slot = step % 2

    @pl.when(step == 0)
    def _():
        _start_row_gather(dest_ref, y_any, mbuf.at[0], sems.at[0])

    @pl.when(step + 1 < nstep)
    def _():
        _start_row_gather(dnext_ref, y_any, mbuf.at[1 - slot], sems.at[1 - slot])

    _wait_row_gather(y_any, mbuf.at[slot], sems.at[slot])
    return mbuf.at[slot]


def _post_moe_kernel(dest_ref, dnext_ref, x1_ref, y_any, mod_ref, lng_ref, lnb_ref, out_ref,
                     mbuf, sems):
    m_ref = _gather_pipelined(pl.program_id(0), pl.num_programs(0), dest_ref, dnext_ref, y_any,
                              mbuf, sems)
    out_ref[...] = _block_input(x1_ref, m_ref, mod_ref[0], lng_ref, lnb_ref)


def _post_moe_call(x1, dest, y_sorted, mod, lng, lnb, rows, steps_per_mod):
    mrows = mod.shape[1]
    nstep = x1.shape[0] // rows
    dest3 = dest.reshape(nstep, 1, rows)
    return pl.pallas_call(
        _post_moe_kernel,
        grid=(nstep,),
        in_specs=[
            pl.BlockSpec((1, 1, rows), lambda i: (i, 0, 0), memory_space=pltpu.SMEM),
            pl.BlockSpec((1, 1, rows), lambda i: (jnp.minimum(i + 1, nstep - 1), 0, 0),
                         memory_space=pltpu.SMEM),
            pl.BlockSpec((rows, D_MODEL), lambda i: (i, 0)),
            pl.BlockSpec(memory_space=pl.ANY),
            pl.BlockSpec((1, mrows, 3 * D_MODEL), lambda i: (i // steps_per_mod, 0, 0)),
            _full((1, D_MODEL)),
            _full((1, D_MODEL)),
        ],
        out_specs=pl.BlockSpec((rows, D_MODEL), lambda i: (i, 0)),
        out_shape=jax.ShapeDtypeStruct(x1.shape, F32),
        scratch_shapes=[pltpu.VMEM((2, rows * CHUNKS, LANES), F32), pltpu.SemaphoreType.DMA((2,))],
        compiler_params=_params("arbitrary"),
        name="post_moe",
    )(dest3, dest3, x1, y_sorted, mod, lng, lnb)


PLAN_CHUNK = 2048


def _num_tiles(nslot):
    return -(-nslot // MOE_TILE) + N_BUCKETS


def _plan_kernel(rp_ref, rs_ref, cnt_ref, dest_ref, sched_ref, *, layer):
    tm = MOE_TILE
    lane = lax.broadcasted_iota(I32, (SUBLANES, LANES), 1)
    sub = lax.broadcasted_iota(I32, (SUBLANES, LANES), 0)
    row = lax.broadcasted_iota(I32, (LANES, LANES), 0)
    col = lax.broadcasted_iota(I32, (LANES, LANES), 1)
    counts = jnp.where(lane < N_BUCKETS, jnp.broadcast_to(cnt_ref[...], (SUBLANES, LANES)), 0.0)
    tiles_b = jnp.floor((counts + float(tm - 1)) * (1.0 / tm))
    tile_end = _dot(tiles_b.astype(BF16), jnp.where(row <= col, 1.0, 0.0).astype(BF16))
    tile_start = tile_end - tiles_b
    start_b = tile_start.astype(BF16)

    def dest_of(route):
        n = route.shape[1]
        bucket, rank = route[R_BUCKET:R_BUCKET + 1, :], route[R_RANK:R_RANK + 1, :]
        b_iota = lax.broadcasted_iota(I32, (LANES, n), 0).astype(F32)
        onehot = jnp.where(b_iota == bucket, 1.0, 0.0).astype(BF16)
        start = _dot(start_b, onehot)[0:1, :]
        return (start * float(tm) + rank).astype(I32)

    ntok_p = rp_ref.shape[1]
    chunk = math.gcd(ntok_p, PLAN_CHUNK)
    for c0 in range(0, ntok_p, chunk):
        dest_ref[:, c0:c0 + chunk] = dest_of(rp_ref[:, c0:c0 + chunk])
    dest_ref[:, ntok_p:] = dest_of(rs_ref[...])

    tile = lane.astype(F32)
    used = jnp.sum(jnp.where(lane == N_BUCKETS - 1, tile_end, 0.0), axis=-1, keepdims=True)
    ti = jnp.minimum(tile, used - 1.0)
    end_col = jnp.transpose(jnp.broadcast_to(tile_end[0:1], (LANES, LANES)))
    start_col = jnp.transpose(jnp.broadcast_to(tile_start[0:1], (LANES, LANES)))
    ti_rows = jnp.broadcast_to(ti[0:1], (LANES, LANES))
    b_of = jnp.sum(jnp.where((row < N_BUCKETS) & (ti_rows >= end_col), 1.0, 0.0),
                   axis=0, keepdims=True)
    start_of = jnp.sum(jnp.where(row.astype(F32) == b_of, start_col, 0.0), axis=0, keepdims=True)
    grp = sum(jnp.where(b_of >= float(g * N_PAIRS), 1.0, 0.0) for g in range(1, MOE_GROUPS))
    pair = b_of - float(N_PAIRS) * grp
    ex_a = sum(jnp.where(pair == float(j), float(PAIR_A[j]), 0.0) for j in range(N_PAIRS))
    ex_b = sum(jnp.where(pair == float(j), float(PAIR_B[j]), 0.0) for j in range(N_PAIRS))
    first_expert = float(layer * N_EXPERTS) + float(MOE_EXPERTS) * grp
    first_lane = float(MOE_GROUPS) + float(MOE_EXPERTS) * grp
    valid = jnp.where(tile[0:1] < used[0:1], 1.0, 0.0)
    first = valid * jnp.where(ti[0:1] == start_of, 1.0, 0.0)
    rows = {S_BLK: ti[0:1], S_EA: first_expert + ex_a, S_EB: first_expert + ex_b, S_VALID: valid,
            S_FIRST: first, S_LANE_G: grp, S_LANE_A: first_lane + ex_a, S_LANE_B: first_lane + ex_b}
    sched = sum(jnp.where(sub == r, jnp.broadcast_to(v, (SUBLANES, LANES)), 0.0)
                for r, v in rows.items())
    sched_ref[...] = sched.astype(I32)


def _moe_plan(route_p, route_s, cnt, layer):
    nslot = route_p.shape[1] + route_s.shape[1]
    assert _num_tiles(nslot) <= LANES
    return pl.pallas_call(
        functools.partial(_plan_kernel, layer=layer),
        out_shape=[jax.ShapeDtypeStruct((1, nslot), I32),
                   jax.ShapeDtypeStruct((SUBLANES, LANES), I32)],
        compiler_params=pltpu.CompilerParams(vmem_limit_bytes=VMEM_LIMIT),
        name="plan",
    )(route_p, route_s, cnt)


def _router_weights(w_coarse, b_coarse, w_fine, b_fine):
    wf = jnp.transpose(w_fine, (1, 0, 2)).reshape(D_MODEL, N_EXPERTS)
    w = jnp.concatenate([w_coarse, wf], axis=1)
    w = jnp.pad(w, ((0, 0), (0, LANES - w.shape[1])))
    b = jnp.concatenate([b_coarse, b_fine.reshape(N_EXPERTS)])
    b = jnp.pad(b, (0, LANES - b.shape[0])).reshape(1, LANES)
    hi_lo = _hi_lo(w)
    return hi_lo[0], hi_lo[1], b


def _prep_params(ln_g, ln_b, w_in_even, w_a2, b_a, gla_norm_g, conv_w, conv_b, conv_ln_g,
                 conv_ln_b, w_out_even, w_grp_pool, pool_scale, w_out_odd, w_coarse, b_coarse,
                 w_fine, b_fine):
    p = {}
    w_in = w_in_even[0]
    o_q, o_k, o_v, o_g = 0, GLA_KW, 2 * GLA_KW, 2 * GLA_KW + GLA_VW
    o_a = o_g + GLA_VW
    o_u = o_a + GLA_RANK
    w_in_r = jnp.concatenate(
        [w_in[:, o_q:o_a], w_in[:, o_u:o_u + 2 * CONV_CH], w_in[:, o_a:o_u],
         jnp.zeros((D_MODEL, A_PAD - GLA_RANK), F32)], axis=1)
    p["w_in"] = _hi_lo(w_in_r)
    p["w_a2"] = _hi_lo(jnp.pad(w_a2[0], ((0, A_PAD - GLA_RANK), (0, 0))))
    p["b_a"] = b_a[0].reshape(1, GLA_KW)
    p["gng"] = gla_norm_g[0].reshape(1, GLA_VW)
    p["conv_w"] = jnp.pad(conv_w[0], ((0, CONV_W_ROWS - CONV_WIDTH), (0, 0)))
    p["conv_b"] = conv_b[0].reshape(1, CONV_CH)
    p["conv_ln_g"] = conv_ln_g[0].reshape(1, CONV_CH)
    p["conv_ln_b"] = conv_ln_b[0].reshape(1, CONV_CH)
    p["w_out0"] = _hi_lo(w_out_even[0])
    p["w_grp"] = w_grp_pool[0].astype(BF16)
    p["pool_scale"] = pool_scale[0].reshape(1, D_MODEL)
    p["w_out1"] = w_out_odd[0].astype(BF16)
    for layer in range(DEPTH):
        for j in range(2):
            p[f"ln_g{layer}{j}"] = ln_g[layer, j].reshape(1, D_MODEL)
            p[f"ln_b{layer}{j}"] = ln_b[layer, j].reshape(1, D_MODEL)
        hi, lo, b = _router_weights(w_coarse[layer], b_coarse[layer], w_fine[layer], b_fine[layer])
        p[f"wr_hi{layer}"], p[f"wr_lo{layer}"], p[f"br{layer}"] = hi, lo, b
    return p


def kernel(x_prompt, x_sample, state_gla, state_conv, state_pool, c_prompt, c_sample, w_ada, b_ada,
           ln_g, ln_b, w_in_even, w_a2, b_a, gla_norm_g, conv_w, conv_b, conv_ln_g, conv_ln_b,
           w_out_even, w_grp_pool, pool_scale, w_out_odd, w_coarse, b_coarse, w_fine, b_fine,
           w_gate, w_up, w_down):
    bsz, seq, _ = x_prompt.shape
    nsmp = x_sample.shape[0]
    ntok_p = bsz * seq
    assert seq % TILE_L == 0 and nsmp <= TILE_L and nsmp % SAMPLE_BLK == 0 and ntok_p % nsmp == 0
    p = _prep_params(ln_g, ln_b, w_in_even, w_a2, b_a, gla_norm_g, conv_w, conv_b, conv_ln_g,
                     conv_ln_b, w_out_even, w_grp_pool, pool_scale, w_out_odd, w_coarse, b_coarse,
                     w_fine, b_fine)
    wg = w_gate.reshape(DEPTH * N_EXPERTS, D_MODEL, MOE_FF)
    wu = w_up.reshape(DEPTH * N_EXPERTS, D_MODEL, MOE_FF)
    wd = w_down.reshape(DEPTH * N_EXPERTS, MOE_FF, D_MODEL)

    c_all = jnp.concatenate([c_prompt, c_sample], axis=0)
    mod = _ada_call(c_all, w_ada.reshape(2 * DEPTH, D_MODEL, 3 * D_MODEL),
                    b_ada.reshape(2 * DEPTH, 1, 3 * D_MODEL))
    mod_p = mod[:, :bsz].reshape(2 * DEPTH, bsz, 1, 3 * D_MODEL)
    mod_s = mod[:, bsz:]

    def moe(h2c_all, route_p, route_s, cnt, layer):
        dest, sched = _moe_plan(route_p, route_s, cnt, layer)
        nrow = _num_tiles(dest.shape[1]) * MOE_TILE
        sorted_x = _dispatch_call(dest, h2c_all, jnp.zeros((nrow * CHUNKS, LANES), F32))
        y_sorted = _moe_call(sched, sorted_x, p[f"wr_hi{layer}"], p[f"wr_lo{layer}"],
                             p[f"br{layer}"], wg, wu, wd)
        return dest[0, :ntok_p], dest[0, ntok_p:ntok_p + nsmp], y_sorted

    xs0 = x_sample.reshape(nsmp, D_MODEL)
    cnt0 = jnp.zeros((1, LANES), F32)
    x1s, h2c_s, route_s, cnt, gla_s, conv_s = _mixer0_sample_call(
        xs0, mod_s[0], mod_s[1], state_gla, state_conv, cnt0, p)
    x1p, h2c_all, route_p, cnt, conv_p, gla_p = _mixer0_prompt_call(
        x_prompt, mod_p[0], mod_p[1], cnt, h2c_s, p)
    dest_p, dest_s, y_sorted = moe(h2c_all, route_p, route_s, cnt, 0)
    x3s, h2c_s, route_s, cnt, pool_s = _mixer1_sample_call(
        x1s, dest_s, y_sorted, mod_s[1], p["ln_g01"], p["ln_b01"], mod_s[2], mod_s[3],
        state_pool, cnt0, p)
    x3p, h2c_all, route_p, cnt, pool_p = _mixer1_prompt_call(
        x1p, dest_p, y_sorted, mod_p[1], p["ln_g01"], p["ln_b01"], mod_p[2], mod_p[3], cnt,
        h2c_s, bsz, p)
    dest_p, dest_s, y_sorted = moe(h2c_all, route_p, route_s, cnt, 1)
    x4p = _post_moe_call(x3p, dest_p, y_sorted, mod_p[3], p["ln_g11"], p["ln_b11"], TILE_L,
                         seq // TILE_L).reshape(bsz, seq, D_MODEL)
    x4s = _post_moe_call(x3s, dest_s, y_sorted, mod_s[3][None], p["ln_g11"], p["ln_b11"], nsmp, 1)
    return (x4p, x4s.reshape(nsmp, 1, D_MODEL), gla_p, conv_p, pool_p, gla_s, conv_s, pool_s)
```

```python
import functools
import math

import jax
import jax.numpy as jnp
from jax import lax
from jax.experimental import pallas as pl
from jax.experimental.pallas import tpu as pltpu

F32 = jnp.float32
BF16 = jnp.bfloat16
I32 = jnp.int32

D_MODEL = 1024
GLA_HEADS = 4
GLA_DK = 64
GLA_DV = 128
GLA_KW = GLA_HEADS * GLA_DK
GLA_VW = GLA_HEADS * GLA_DV
GLA_RANK = 16
GLA_TAU = 16.0
CONV_CH = 512
CONV_WIDTH = 31
CONV_BUF = CONV_WIDTH - 1
POOL_WINDOWS = (2, 4, 8, 16)
POOL_GC = D_MODEL // len(POOL_WINDOWS)
POOL_BUF = max(POOL_WINDOWS) - 1
MOE_GROUPS = 4
MOE_EXPERTS = 4
N_EXPERTS = MOE_GROUPS * MOE_EXPERTS
N_PAIRS = MOE_EXPERTS * (MOE_EXPERTS - 1) // 2
N_BUCKETS = MOE_GROUPS * N_PAIRS
PAIR_A = (0, 0, 0, 1, 1, 2)
PAIR_B = (1, 2, 3, 2, 3, 3)
MOE_FF = 512
R_BUCKET, R_RANK = 0, 1
DEPTH = 2
ALPHA = (2 * DEPTH) ** 0.25
LN_EPS = 1e-5

LANES = 128
SUBLANES = 8
CHUNKS = D_MODEL // LANES
A_PAD = LANES
Z_WIDTH = 2 * GLA_KW + 2 * GLA_VW + 2 * CONV_CH + A_PAD
NEG_BIG = -1e30
VMEM_LIMIT = 56 * 1024 * 1024

TILE_L = 256
CONV_HALO = 32
CONV_W_ROWS = 32
POOL_HALO = 16
SAMPLE_BLK = 16
MOE_TILE = 256
DISPATCH_ROWS = 640


def _dot(a, b):
    return jnp.dot(a, b, preferred_element_type=F32)


def _dot_nt(a, b):
    return lax.dot_general(a, b, (((1,), (1,)), ((), ())), preferred_element_type=F32)


def _split3(x):
    hi = x.astype(BF16)
    r1 = x - hi.astype(F32)
    mid = r1.astype(BF16)
    lo = (r1 - mid.astype(F32)).astype(BF16)
    return hi, mid, lo


def _split2(x):
    hi = x.astype(BF16)
    lo = (x - hi.astype(F32)).astype(BF16)
    return hi, lo


def _dot_w3(a, w_hi, w_lo):
    a_hi, a_lo = _split2(a)
    return _dot(a_hi, w_hi) + (_dot(a_lo, w_hi) + _dot(a_hi, w_lo))


def _lhs3(parts):
    hi, lo = parts
    return jnp.concatenate([hi, lo, hi], axis=1)


def _rhs3_rows(x):
    hi, lo = _split2(x)
    return jnp.concatenate([hi, hi, lo], axis=0)


def _rhs3_lanes(parts):
    hi, lo = parts
    return jnp.concatenate([hi, hi, lo], axis=1)


def _layer_norm(x, g, b):
    mu = jnp.mean(x, axis=-1, keepdims=True)
    xc = x - mu
    var = jnp.mean(xc * xc, axis=-1, keepdims=True)
    return xc * lax.rsqrt(var + LN_EPS) * g + b


def _standardize(x):
    mu = jnp.mean(x, axis=-1, keepdims=True)
    xc = x - mu
    var = jnp.mean(xc * xc, axis=-1, keepdims=True)
    return xc * lax.rsqrt(var + LN_EPS)


def _sigmoid(x):
    return 1.0 / (1.0 + jnp.exp(-x))


def _silu(x):
    return x * _sigmoid(x)


def _log_sigmoid(x):
    return jnp.minimum(x, 0.0) - jnp.log(1.0 + jnp.exp(-jnp.abs(x)))


def _mod3(mod):
    return mod[:, 0:D_MODEL], mod[:, D_MODEL:2 * D_MODEL], mod[:, 2 * D_MODEL:3 * D_MODEL]


def _full(shape):
    nd = len(shape)
    return pl.BlockSpec(shape, lambda *_: (0,) * nd)


def _const(shape):
    nd = len(shape)
    return pl.BlockSpec(shape, lambda *_: (0,) * nd, pipeline_mode=pl.Buffered(1))


def _hi_lo(w):
    bits = lax.bitcast_convert_type(w, jnp.uint32) & jnp.uint32(0xFFFF0000)
    hi = lax.bitcast_convert_type(bits, F32)
    return jnp.stack([hi.astype(BF16), (w - hi).astype(BF16)])


def _params(*sem):
    return pltpu.CompilerParams(dimension_semantics=sem, vmem_limit_bytes=VMEM_LIMIT)


def _ada_kernel(c_ref, w_ref, b_ref, o_ref):
    w_hi, w_lo = _split2(w_ref[0])
    o_ref[0] = _dot_w3(_silu(c_ref[...]), w_hi, w_lo) + b_ref[0]


def _ada_call(c_all, w_ada, b_ada):
    n = c_all.shape[0]
    ncomb = w_ada.shape[0]
    tn = D_MODEL
    return pl.pallas_call(
        _ada_kernel,
        grid=(ncomb, 3 * D_MODEL // tn),
        in_specs=[
            pl.BlockSpec((n, D_MODEL), lambda i, j: (0, 0)),
            pl.BlockSpec((1, D_MODEL, tn), lambda i, j: (i, 0, j)),
            pl.BlockSpec((1, 1, tn), lambda i, j: (i, 0, j)),
        ],
        out_specs=pl.BlockSpec((1, n, tn), lambda i, j: (i, 0, j)),
        out_shape=jax.ShapeDtypeStruct((ncomb, n, 3 * D_MODEL), F32),
        compiler_params=_params("arbitrary", "arbitrary"),
        name="ada",
    )(c_all, w_ada, b_ada)


def _route(logits, carry):
    rows = logits.shape[0]
    lane = lax.broadcasted_iota(I32, (rows, LANES), 1)
    lanef = lane.astype(F32)
    big = float(LANES)
    lc = jnp.where(lane < MOE_GROUPS, logits, NEG_BIG)
    mc = jnp.max(lc, axis=-1, keepdims=True)
    gidx = jnp.min(jnp.where(lc == mc, lanef, big), axis=-1, keepdims=True)
    lo = float(MOE_GROUPS) + gidx * float(MOE_EXPERTS)
    in_grp = (lanef >= lo) & (lanef < lo + float(MOE_EXPERTS))
    lf = jnp.where(in_grp, logits, NEG_BIG)
    m1 = jnp.max(lf, axis=-1, keepdims=True)
    i1 = jnp.min(jnp.where(lf == m1, lanef, big), axis=-1, keepdims=True)
    lf2 = jnp.where(lanef == i1, NEG_BIG, lf)
    m2 = jnp.max(lf2, axis=-1, keepdims=True)
    i2 = jnp.min(jnp.where(lf2 == m2, lanef, big), axis=-1, keepdims=True)
    ea = jnp.minimum(i1, i2) - lo
    eb = jnp.maximum(i1, i2) - lo
    bucket = gidx * float(N_PAIRS) + ea * (7.0 - ea) * 0.5 + eb - ea - 1.0
    onehot = jnp.where(lanef == bucket, 1.0, 0.0)
    r = lax.broadcasted_iota(I32, (rows, rows), 0)
    c = lax.broadcasted_iota(I32, (rows, rows), 1)
    before = _dot(jnp.where(r > c, 1.0, 0.0).astype(BF16), onehot.astype(BF16))
    rank = jnp.sum(onehot * (before + carry), axis=-1, keepdims=True)
    new_carry = carry + jnp.sum(onehot, axis=0, keepdims=True)
    cols = jnp.where(lane == R_BUCKET, bucket, 0.0) + jnp.where(lane == R_RANK, rank, 0.0)
    return jnp.transpose(cols)[0:SUBLANES, :], new_carry


def _store_chunked(ref, x):
    rows = x.shape[0]
    for c in range(CHUNKS):
        ref[pl.ds(c, rows, stride=CHUNKS), :] = x[:, c * LANES:(c + 1) * LANES]


def _load_chunked(ref, rows, group, first):
    return jnp.concatenate(
        [ref[pl.ds(first + c, rows, stride=group), :] for c in range(CHUNKS)], axis=1)


def _router_logits(h2, wr_hi_ref, wr_lo_ref, br_ref):
    h_hi, h_lo = _split2(h2)
    wr_hi = wr_hi_ref[...]
    return _dot(h_hi, wr_hi) + _dot(h_lo, wr_hi) + _dot(h_hi, wr_lo_ref[...]) + br_ref[...]


def _post_mixer(x, m, gate, lng_ref, lnb_ref, mod_b, wr_hi_ref, wr_lo_ref, br_ref, carry,
                x1_ref, h2c_ref, route_ref):
    x1 = _layer_norm(ALPHA * x + (1.0 + gate) * m, lng_ref[...], lnb_ref[...])
    shift2, scale2, _ = _mod3(mod_b)
    h2 = x1 * (1.0 + scale2) + shift2
    pad = route_ref.shape[1] - x.shape[0]
    if pad:
        h2 = jnp.concatenate([h2, jnp.zeros((pad, D_MODEL), F32)], axis=0)
    route, carry = _route(_router_logits(h2, wr_hi_ref, wr_lo_ref, br_ref), carry)
    x1_ref[...] = x1
    _store_chunked(h2c_ref, h2)
    route_ref[...] = route
    return carry


def _token_out_specs(rows, slot_rows, x1_map, slot_map):
    return [pl.BlockSpec((rows, D_MODEL), x1_map),
            pl.BlockSpec((slot_rows * CHUNKS, LANES), slot_map),
            pl.BlockSpec((SUBLANES, slot_rows), lambda *i: (0, x1_map(*i)[0]))]


def _token_out_shapes(ntok, nslot_h2, nslot_route):
    return [jax.ShapeDtypeStruct((ntok, D_MODEL), F32),
            jax.ShapeDtypeStruct((nslot_h2 * CHUNKS, LANES), F32),
            jax.ShapeDtypeStruct((SUBLANES, nslot_route), F32)]


def _hold_rows(b, s):
    n, w = b.shape
    if s >= SUBLANES:
        pieces = []
        for p in range(n // (2 * s)):
            r = p * 2 * s + s - 1
            pieces.append(jnp.broadcast_to(b[r:r + 1, :], (2 * s, w)))
        return pieces[0] if len(pieces) == 1 else jnp.concatenate(pieces, axis=0)
    b3 = b.reshape(n // SUBLANES, SUBLANES, w)
    sub = lax.broadcasted_iota(I32, b3.shape, 1)

    def bc(r):
        return jnp.broadcast_to(b3[:, r:r + 1, :], b3.shape)

    out = bc(s - 1)
    for p in range(1, SUBLANES // (2 * s)):
        out = jnp.where(sub >= p * 2 * s, bc(p * 2 * s + s - 1), out)
    return out.reshape(n, w)


def _gla_tile(q, k, v, la, s_prev):
    n = q.shape[0]
    row = lax.broadcasted_iota(I32, (n, n), 0)
    col = lax.broadcasted_iota(I32, (n, n), 1)
    tri = jnp.where(row >= col, 1.0, 0.0).astype(BF16)
    hi, mid, lo = _split3(la)
    b = _dot(tri, hi) + _dot(tri, mid) + _dot(tri, lo)
    b_last = b[n - 1:n, :]
    kl = k * jnp.exp(b_last - b)

    rowi = lax.broadcasted_iota(I32, (n, GLA_KW), 0)
    levels = []
    s = n // 2
    while s >= 1:
        m = _hold_rows(b, s)
        second = (rowi & s) != 0
        qs = _split2(q * jnp.exp(jnp.where(second, b - m, NEG_BIG)))
        ks = _split2(k * jnp.exp(jnp.where(second, NEG_BIG, m - b)))
        levels.append((2 * s, qs, ks))
        s //= 2
    levels.append((1, _split2(q), _split2(k)))
    qe = _split2(q * jnp.exp(b))

    lane = lax.broadcasted_iota(I32, (n, LANES), 1)
    xor = row ^ col
    zero = jnp.zeros((), BF16)

    def head_part(parts, sl, hm):
        return tuple(jnp.where(hm, x[:, sl], zero) for x in parts)

    outs = []
    for h in range(GLA_HEADS):
        p = h // 2
        sl = slice(p * LANES, (p + 1) * LANES)
        hm = (lane < GLA_DK) if h % 2 == 0 else (lane >= GLA_DK)
        att = None
        for span, qs, ks in levels:
            a_s = _dot_nt(_lhs3(head_part(qs, sl, hm)), _rhs3_lanes((ks[0][:, sl], ks[1][:, sl])))
            att = a_s if att is None else jnp.where(xor < span, a_s, att)
        o_h = _dot(_lhs3(_split2(att)), _rhs3_rows(v[:, h * GLA_DV:(h + 1) * GLA_DV]))
        o_h = o_h + _dot(_lhs3(head_part(qe, sl, hm)), _rhs3_rows(s_prev[sl, :]))
        outs.append(o_h)

    decay = jnp.exp(b_last)
    upper = lax.broadcasted_iota(I32, (LANES, LANES), 0) < GLA_DK
    s_new = []
    for p in range(GLA_HEADS // 2):
        sl = slice(p * LANES, (p + 1) * LANES)
        kl_t = jnp.transpose(kl[:, sl])
        u = _dot(_lhs3(_split2(kl_t)), _rhs3_rows(v[:, p * 2 * GLA_DV:(p + 1) * 2 * GLA_DV]))
        upd = jnp.where(upper, u[:, 0:GLA_DV], u[:, GLA_DV:2 * GLA_DV])
        dcol = jnp.transpose(jnp.broadcast_to(decay[:, sl], (LANES, LANES)))
        s_new.append(dcol * s_prev[sl, :] + upd)
    return outs, jnp.concatenate(s_new, axis=0)


def _split_projection(z):
    c0 = 0
    q = z[:, c0:c0 + GLA_KW] * (GLA_DK ** -0.5); c0 += GLA_KW
    k = z[:, c0:c0 + GLA_KW]; c0 += GLA_KW
    v = z[:, c0:c0 + GLA_VW]; c0 += GLA_VW
    g = z[:, c0:c0 + GLA_VW]; c0 += GLA_VW
    ua = z[:, c0:c0 + CONV_CH]; c0 += CONV_CH
    ug = z[:, c0:c0 + CONV_CH]; c0 += CONV_CH
    a_lr = z[:, c0:c0 + A_PAD]
    return q, k, v, g, ua, ug, a_lr


def _mix0_project(o_heads, g, y, gng_ref, w_out_ref):
    sl = slice(GLA_VW, GLA_VW + CONV_CH)
    m = _dot_w3(y, w_out_ref[0, sl, :], w_out_ref[1, sl, :])
    for hd in range(GLA_HEADS):
        sl = slice(hd * GLA_DV, (hd + 1) * GLA_DV)
        o_h = _standardize(o_heads[hd]) * gng_ref[:, sl] * _silu(g[:, sl])
        m = m + _dot_w3(o_h, w_out_ref[0, sl, :], w_out_ref[1, sl, :])
    return m


def _mix0_inputs(h, w_in_ref, w_a2_ref, b_a_ref):
    z = _dot_w3(h, w_in_ref[0], w_in_ref[1])
    q, k, v, g, ua, ug, a_lr = _split_projection(z)
    la = _log_sigmoid(_dot_w3(a_lr, w_a2_ref[0], w_a2_ref[1]) + b_a_ref[...]) * (1.0 / GLA_TAU)
    return q, k, v, g, ua, ug, la


def _prompt_grid_specs(bsz, nt):
    ntile = bsz * nt
    seq_map = lambda b, t: (jnp.minimum(b, bsz - 1), 0, 0)
    x_map = lambda b, t: (jnp.minimum(b, bsz - 1), t, 0)
    tok_map = lambda b, t: (jnp.minimum(b * nt + t, ntile - 1), 0)
    slot_map = lambda b, t: (jnp.minimum(b * nt + t, ntile), 0)
    state_map = lambda b, t: (0, jnp.minimum(b, bsz - 1), 0, 0)
    return seq_map, x_map, tok_map, slot_map, state_map


def _mixer0_prompt_kernel(x_ref, moda_ref, modb_ref, cnt_in_ref, h2cs_ref, w_in_ref, w_a2_ref,
                          b_a_ref, gng_ref,
                          cw_ref, cb_ref, clg_ref, clb_ref, w_out_ref, lng_ref, lnb_ref,
                          wr_hi_ref, wr_lo_ref, br_ref,
                          x1_ref, h2c_ref, route_ref, cnt_ref, nconv_ref, ngla_ref,
                          s_ref, cbuf_ref, shift_ref, carry_ref):
    nseq = pl.num_programs(0) - 1

    @pl.when(pl.program_id(0) < nseq)
    def _():
        _mixer0_prompt_tile(x_ref, moda_ref, modb_ref, cnt_in_ref, w_in_ref, w_a2_ref, b_a_ref,
                            gng_ref, cw_ref, cb_ref, clg_ref, clb_ref, w_out_ref, lng_ref, lnb_ref,
                            wr_hi_ref, wr_lo_ref, br_ref, x1_ref, h2c_ref, route_ref, cnt_ref,
                            nconv_ref, ngla_ref, s_ref, cbuf_ref, shift_ref, carry_ref)

    @pl.when((pl.program_id(0) == nseq) & (pl.program_id(1) == 0))
    def _():
        h2c_ref[...] = h2cs_ref[...]


def _mixer0_prompt_tile(x_ref, moda_ref, modb_ref, cnt_in_ref, w_in_ref, w_a2_ref, b_a_ref,
                        gng_ref, cw_ref, cb_ref, clg_ref, clb_ref, w_out_ref, lng_ref, lnb_ref,
                        wr_hi_ref, wr_lo_ref, br_ref, x1_ref, h2c_ref, route_ref, cnt_ref,
                        nconv_ref, ngla_ref, s_ref, cbuf_ref, shift_ref, carry_ref):
    t = pl.program_id(1)
    n = x_ref.shape[1]

    @pl.when((t == 0) & (pl.program_id(0) == 0))
    def _():
        carry_ref[...] = cnt_in_ref[...]

    @pl.when(t == 0)
    def _():
        s_ref[...] = jnp.zeros_like(s_ref)
        cbuf_ref[0:CONV_HALO, :] = jnp.zeros((CONV_HALO, CONV_CH), F32)

    x = x_ref[0]
    shift, scale, gate = _mod3(moda_ref[0])
    h = x * (1.0 + scale) + shift
    q, k, v, g, ua, ug, la = _mix0_inputs(h, w_in_ref, w_a2_ref, b_a_ref)
    o_heads, s_new = _gla_tile(q, k, v, la, s_ref[...])
    s_ref[...] = s_new

    glu = ua * _sigmoid(ug)
    cbuf_ref[CONV_HALO:CONV_HALO + n, :] = glu
    acc = jnp.broadcast_to(cb_ref[...], (n, CONV_CH))
    base = CONV_HALO - CONV_BUF
    for r in range(SUBLANES):
        taps = range(r, CONV_WIDTH, SUBLANES)
        span = n + (len(taps) - 1) * SUBLANES
        if (base + r) % SUBLANES == 0:
            src, off = cbuf_ref, base + r
        else:
            shift_ref[r, 0:span, :] = cbuf_ref[base + r:base + r + span, :]
            src, off = shift_ref.at[r], 0
        for a, j in enumerate(taps):
            lo = off + a * SUBLANES
            acc = acc + cw_ref[j:j + 1, :] * src[lo:lo + n, :]
    y = _silu(_layer_norm(acc, clg_ref[...], clb_ref[...]))

    @pl.when(t == pl.num_programs(1) - 1)
    def _():
        nconv_ref[0, 0] = cbuf_ref[CONV_HALO + n - CONV_BUF:CONV_HALO + n, :]
        ngla_ref[0, 0] = s_new.reshape(GLA_HEADS, GLA_DK, GLA_DV)

    cbuf_ref[0:CONV_HALO, :] = cbuf_ref[n:n + CONV_HALO, :]

    m = _mix0_project(o_heads, g, y, gng_ref, w_out_ref)
    carry = _post_mixer(x, m, gate, lng_ref, lnb_ref, modb_ref[0], wr_hi_ref, wr_lo_ref, br_ref,
                        carry_ref[...], x1_ref, h2c_ref, route_ref)
    carry_ref[...] = carry
    cnt_ref[...] = carry


def _mixer0_prompt_call(x, mod_a, mod_b, cnt, h2c_s, p):
    bsz, seq, _ = x.shape
    tl = TILE_L
    nt = seq // tl
    seq_map, x_map, tok, slot, state_map = _prompt_grid_specs(bsz, nt)
    return pl.pallas_call(
        _mixer0_prompt_kernel,
        grid=(bsz + 1, nt),
        in_specs=[
            pl.BlockSpec((1, tl, D_MODEL), x_map),
            pl.BlockSpec((1, 1, 3 * D_MODEL), seq_map),
            pl.BlockSpec((1, 1, 3 * D_MODEL), seq_map),
            _full((1, LANES)),
            _full((tl * CHUNKS, LANES)),
            _const((2, D_MODEL, Z_WIDTH)),
            _const((2, A_PAD, GLA_KW)),
            _full((1, GLA_KW)),
            _full((1, GLA_VW)),
            _full((CONV_W_ROWS, CONV_CH)),
            _full((1, CONV_CH)),
            _full((1, CONV_CH)),
            _full((1, CONV_CH)),
            _const((2, GLA_VW + CONV_CH, D_MODEL)),
            _full((1, D_MODEL)),
            _full((1, D_MODEL)),
            _full((D_MODEL, LANES)),
            _full((D_MODEL, LANES)),
            _full((1, LANES)),
        ],
        out_specs=_token_out_specs(tl, tl, tok, slot) + [
            _full((1, LANES)),
            pl.BlockSpec((1, 1, CONV_BUF, CONV_CH), state_map),
            pl.BlockSpec((1, 1, GLA_HEADS, GLA_DK, GLA_DV), lambda b, t: state_map(b, t) + (0,)),
        ],
        out_shape=_token_out_shapes(bsz * seq, bsz * seq + tl, bsz * seq) + [
            jax.ShapeDtypeStruct((1, LANES), F32),
            jax.ShapeDtypeStruct((1, bsz, CONV_BUF, CONV_CH), F32),
            jax.ShapeDtypeStruct((1, bsz, GLA_HEADS, GLA_DK, GLA_DV), F32),
        ],
        scratch_shapes=[
            pltpu.VMEM((GLA_KW, GLA_DV), F32),
            pltpu.VMEM((CONV_HALO + tl, CONV_CH), F32),
            pltpu.VMEM((SUBLANES, CONV_HALO + tl, CONV_CH), F32),
            pltpu.VMEM((1, LANES), F32),
        ],
        compiler_params=_params("arbitrary", "arbitrary"),
        name="mixer0_prompt",
    )(x, mod_a, mod_b, cnt, h2c_s, p["w_in"], p["w_a2"], p["b_a"], p["gng"], p["conv_w"], p["conv_b"],
      p["conv_ln_g"], p["conv_ln_b"], p["w_out0"], p["ln_g00"], p["ln_b00"],
      p["wr_hi0"], p["wr_lo0"], p["br0"])


def _mixer0_sample_kernel(x_ref, moda_ref, modb_ref, sgla_ref, sconv_ref, cnt_in_ref,
                          w_in_ref, w_a2_ref,
                          b_a_ref, gng_ref, cw_ref, cb_ref, clg_ref, clb_ref, w_out_ref, lng_ref,
                          lnb_ref, wr_hi_ref, wr_lo_ref, br_ref,
                          x1_ref, h2c_ref, route_ref, cnt_ref, ngla_ref, nconv_ref,
                          zt_ref, v_ref, g_ref, glu_ref, o_ref, y_ref):
    i = pl.program_id(0)
    nb = sgla_ref.shape[1]
    ntok = x_ref.shape[0]

    @pl.when(i == 0)
    def _():
        shift, scale, _ = _mod3(moda_ref[...])
        h = x_ref[...] * (1.0 + scale) + shift
        q, k, v, g, ua, ug, la = _mix0_inputs(h, w_in_ref, w_a2_ref, b_a_ref)
        v_ref[...] = v
        g_ref[...] = g
        glu_ref[...] = ua * _sigmoid(ug)
        for j, val in enumerate((jnp.exp(la), k, q)):
            hi, lo = _split2(jnp.transpose(val))
            zt_ref[(2 * j) * GLA_KW:(2 * j + 1) * GLA_KW, :] = hi
            zt_ref[(2 * j + 1) * GLA_KW:(2 * j + 2) * GLA_KW, :] = lo

    tok_row = lax.broadcasted_iota(I32, (ntok, LANES), 0)
    blk = pl.ds(pl.multiple_of(i * nb, nb), nb)
    v_blk = v_ref[blk, :]
    glu_blk = glu_ref[blk, :]
    o_rows, y_rows = [], []
    for n in range(nb):
        onehot = jnp.where(tok_row == i * nb + n, 1.0, 0.0).astype(BF16)
        cols = _dot(zt_ref[...], onehot)
        a_col = cols[0:GLA_KW] + cols[GLA_KW:2 * GLA_KW]
        k_col = cols[2 * GLA_KW:3 * GLA_KW] + cols[3 * GLA_KW:4 * GLA_KW]
        q_col = cols[4 * GLA_KW:5 * GLA_KW] + cols[5 * GLA_KW:6 * GLA_KW]
        v_row = v_blk[n:n + 1, :]
        v_b = jnp.concatenate(
            [jnp.broadcast_to(v_row[:, h * GLA_DV:(h + 1) * GLA_DV], (GLA_DK, GLA_DV))
             for h in range(GLA_HEADS)], axis=0)
        s_old = sgla_ref[0, n].reshape(GLA_KW, GLA_DV)
        s_new = a_col * s_old + k_col * v_b
        ngla_ref[0, n] = s_new.reshape(GLA_HEADS, GLA_DK, GLA_DV)
        o4 = jnp.sum((q_col * s_new).reshape(GLA_HEADS, GLA_DK, GLA_DV), axis=1)
        o_rows.append(jnp.concatenate([o4[h:h + 1, :] for h in range(GLA_HEADS)], axis=1))
        glu_row = glu_blk[n:n + 1, :]
        past = sconv_ref[0, n]
        y_rows.append(jnp.sum(past * cw_ref[0:CONV_BUF, :], axis=0, keepdims=True)
                      + glu_row * cw_ref[CONV_BUF:CONV_WIDTH, :] + cb_ref[...])
        nconv_ref[0, n, 0:CONV_BUF - 1, :] = sconv_ref[0, n, 1:CONV_BUF, :]
        nconv_ref[0, n, CONV_BUF - 1:CONV_BUF, :] = glu_row
    o_ref[blk, :] = jnp.concatenate(o_rows, axis=0)
    y_ref[blk, :] = jnp.concatenate(y_rows, axis=0)

    @pl.when(i == pl.num_programs(0) - 1)
    def _():
        y = _silu(_layer_norm(y_ref[...], clg_ref[...], clb_ref[...]))
        o_heads = [o_ref[:, hd * GLA_DV:(hd + 1) * GLA_DV] for hd in range(GLA_HEADS)]
        m = _mix0_project(o_heads, g_ref[...], y, gng_ref, w_out_ref)
        _, _, gate = _mod3(moda_ref[...])
        cnt_ref[...] = _post_mixer(x_ref[...], m, gate, lng_ref, lnb_ref, modb_ref[...], wr_hi_ref,
                                   wr_lo_ref, br_ref, cnt_in_ref[...], x1_ref, h2c_ref, route_ref)


def _mixer0_sample_call(x, mod_a, mod_b, state_gla, state_conv, cnt, p):
    ntok = x.shape[0]
    nb = SAMPLE_BLK
    tok = lambda i: (0, 0)
    return pl.pallas_call(
        _mixer0_sample_kernel,
        grid=(ntok // nb,),
        in_specs=[
            _full((ntok, D_MODEL)),
            _full((ntok, 3 * D_MODEL)),
            _full((ntok, 3 * D_MODEL)),
            pl.BlockSpec((1, nb, GLA_HEADS, GLA_DK, GLA_DV), lambda i: (0, i, 0, 0, 0)),
            pl.BlockSpec((1, nb, CONV_BUF, CONV_CH), lambda i: (0, i, 0, 0)),
            _full((1, LANES)),
            _const((2, D_MODEL, Z_WIDTH)),
            _const((2, A_PAD, GLA_KW)),
            _full((1, GLA_KW)),
            _full((1, GLA_VW)),
            _full((CONV_W_ROWS, CONV_CH)),
            _full((1, CONV_CH)),
            _full((1, CONV_CH)),
            _full((1, CONV_CH)),
            _const((2, GLA_VW + CONV_CH, D_MODEL)),
            _full((1, D_MODEL)),
            _full((1, D_MODEL)),
            _full((D_MODEL, LANES)),
            _full((D_MODEL, LANES)),
            _full((1, LANES)),
        ],
        out_specs=_token_out_specs(ntok, TILE_L, tok, tok) + [
            _full((1, LANES)),
            pl.BlockSpec((1, nb, GLA_HEADS, GLA_DK, GLA_DV), lambda i: (0, i, 0, 0, 0)),
            pl.BlockSpec((1, nb, CONV_BUF, CONV_CH), lambda i: (0, i, 0, 0)),
        ],
        out_shape=_token_out_shapes(ntok, TILE_L, TILE_L) + [
            jax.ShapeDtypeStruct((1, LANES), F32),
            jax.ShapeDtypeStruct(state_gla.shape, F32),
            jax.ShapeDtypeStruct(state_conv.shape, F32),
        ],
        scratch_shapes=[
            pltpu.VMEM((6 * GLA_KW, ntok), BF16),
            pltpu.VMEM((ntok, GLA_VW), F32),
            pltpu.VMEM((ntok, GLA_VW), F32),
            pltpu.VMEM((ntok, CONV_CH), F32),
            pltpu.VMEM((ntok, GLA_VW), F32),
            pltpu.VMEM((ntok, CONV_CH), F32),
        ],
        compiler_params=_params("arbitrary"),
        name="mixer0_sample",
    )(x, mod_a, mod_b, state_gla, state_conv, cnt, p["w_in"], p["w_a2"], p["b_a"],
      p["gng"], p["conv_w"], p["conv_b"], p["conv_ln_g"], p["conv_ln_b"], p["w_out0"],
      p["ln_g00"], p["ln_b00"], p["wr_hi0"], p["wr_lo0"], p["br0"])


def _pool_project(pooled, h, wg_ref, ps_ref, w_out_ref):
    m = None
    for gi in range(len(POOL_WINDOWS)):
        sl = slice(gi * POOL_GC, (gi + 1) * POOL_GC)
        mixed = _dot((pooled[:, sl] - h[:, sl]).astype(BF16), wg_ref[gi]) * ps_ref[:, sl]
        part = _dot(mixed.astype(BF16), w_out_ref[sl, :])
        m = part if m is None else m + part
    return m


def _block_input(xin_ref, m_ref, mod_m, lnpg_ref, lnpb_ref):
    m = _load_chunked(m_ref, xin_ref.shape[0], CHUNKS, 0)
    _, _, gate = _mod3(mod_m)
    return _layer_norm(ALPHA * xin_ref[...] + (1.0 + gate) * m, lnpg_ref[...], lnpb_ref[...])


def _mixer1_prompt_kernel(dest_ref, dnext_ref, xin_ref, y_any, modm_ref, lnpg_ref, lnpb_ref,
                          moda_ref, modb_ref, cnt_in_ref, h2cs_ref, wg_ref, ps_ref,
                          w_out_ref, lng_ref, lnb_ref, wr_hi_ref, wr_lo_ref, br_ref,
                          x1_ref, h2c_ref, route_ref, cnt_ref, npool_ref,
                          pbuf_ref, carry_ref, mbuf, sems):
    nseq = pl.num_programs(0) - 1
    nt = pl.num_programs(1)

    @pl.when(pl.program_id(0) < nseq)
    def _():
        m_ref, drain = _gather_pipelined(pl.program_id(0) * nt + pl.program_id(1), nseq * nt,
                                         dest_ref, dnext_ref, y_any, mbuf, sems)
        x = _block_input(xin_ref, m_ref, modm_ref[0], lnpg_ref, lnpb_ref)
        _mixer1_prompt_tile(x, moda_ref, modb_ref, cnt_in_ref, wg_ref, ps_ref, w_out_ref,
                            lng_ref, lnb_ref, wr_hi_ref, wr_lo_ref, br_ref, x1_ref, h2c_ref,
                            route_ref, cnt_ref, npool_ref, pbuf_ref, carry_ref)
        drain()

    @pl.when((pl.program_id(0) == nseq) & (pl.program_id(1) == 0))
    def _():
        h2c_ref[...] = h2cs_ref[...]


def _mixer1_prompt_tile(x, moda_ref, modb_ref, cnt_in_ref, wg_ref, ps_ref, w_out_ref,
                        lng_ref, lnb_ref, wr_hi_ref, wr_lo_ref, br_ref, x1_ref, h2c_ref,
                        route_ref, cnt_ref, npool_ref, pbuf_ref, carry_ref):
    t = pl.program_id(1)
    n = x.shape[0]

    @pl.when((t == 0) & (pl.program_id(0) == 0))
    def _():
        carry_ref[...] = cnt_in_ref[...]

    @pl.when(t == 0)
    def _():
        pbuf_ref[0:POOL_HALO, :] = jnp.zeros((POOL_HALO, D_MODEL), F32)

    shift, scale, gate = _mod3(moda_ref[0])
    h = x * (1.0 + scale) + shift
    pbuf_ref[POOL_HALO:POOL_HALO + n, :] = h

    cur = pbuf_ref[...]
    sums = []
    for gi, w in enumerate(POOL_WINDOWS):
        cur = cur + pltpu.roll(cur, w // 2, axis=0)
        sums.append(cur[POOL_HALO:, 0:POOL_GC])
        if gi + 1 < len(POOL_WINDOWS):
            cur = cur[:, POOL_GC:]
    pos = lax.broadcasted_iota(I32, (n, POOL_GC), 0) + t * n
    pooled = jnp.concatenate(
        [s / jnp.minimum(w, pos + 1).astype(F32) for s, w in zip(sums, POOL_WINDOWS)], axis=1)

    @pl.when(t == pl.num_programs(1) - 1)
    def _():
        npool_ref[0, 0] = pbuf_ref[POOL_HALO + n - POOL_BUF:POOL_HALO + n, :]

    pbuf_ref[0:POOL_HALO, :] = pbuf_ref[n:n + POOL_HALO, :]

    m = _pool_project(pooled, h, wg_ref, ps_ref, w_out_ref)
    carry = _post_mixer(x, m, gate, lng_ref, lnb_ref, modb_ref[0], wr_hi_ref, wr_lo_ref, br_ref,
                        carry_ref[...], x1_ref, h2c_ref, route_ref)
    carry_ref[...] = carry
    cnt_ref[...] = carry


def _mixer1_prompt_call(x1_prev, dest, y_sorted, mod_m, lnpg, lnpb, mod_a, mod_b, cnt, h2c_s, bsz, p):
    seq = x1_prev.shape[0] // bsz
    tl = TILE_L
    nt = seq // tl
    ntile = bsz * nt
    seq_map, _, tok, slot, state_map = _prompt_grid_specs(bsz, nt)
    ng = len(POOL_WINDOWS)
    return pl.pallas_call(
        _mixer1_prompt_kernel,
        grid=(bsz + 1, nt),
        in_specs=[
            pl.BlockSpec((1, 1, tl), lambda b, t: tok(b, t) + (0,), memory_space=pltpu.SMEM),
            pl.BlockSpec((1, 1, tl), lambda b, t: (jnp.minimum(b * nt + t + 1, ntile - 1), 0, 0),
                         memory_space=pltpu.SMEM),
            pl.BlockSpec((tl, D_MODEL), tok),
            pl.BlockSpec(memory_space=pl.ANY),
            pl.BlockSpec((1, 1, 3 * D_MODEL), seq_map),
            _full((1, D_MODEL)),
            _full((1, D_MODEL)),
            pl.BlockSpec((1, 1, 3 * D_MODEL), seq_map),
            pl.BlockSpec((1, 1, 3 * D_MODEL), seq_map),
            _full((1, LANES)),
            _full((tl * CHUNKS, LANES)),
            _full((ng, POOL_GC, POOL_GC)),
            _full((1, D_MODEL)),
            _full((D_MODEL, D_MODEL)),
            _full((1, D_MODEL)),
            _full((1, D_MODEL)),
            _full((D_MODEL, LANES)),
            _full((D_MODEL, LANES)),
            _full((1, LANES)),
        ],
        out_specs=_token_out_specs(tl, tl, tok, slot) + [
            _full((1, LANES)),
            pl.BlockSpec((1, 1, POOL_BUF, D_MODEL), state_map),
        ],
        out_shape=_token_out_shapes(bsz * seq, bsz * seq + tl, bsz * seq) + [
            jax.ShapeDtypeStruct((1, LANES), F32),
            jax.ShapeDtypeStruct((1, bsz, POOL_BUF, D_MODEL), F32),
        ],
        scratch_shapes=[pltpu.VMEM((POOL_HALO + tl, D_MODEL), F32), pltpu.VMEM((1, LANES), F32),
                        pltpu.VMEM((2, tl * CHUNKS, LANES), F32), pltpu.SemaphoreType.DMA((2,))],
        compiler_params=_params("arbitrary", "arbitrary"),
        name="mixer1_prompt",
    )(dest.reshape(ntile, 1, tl), dest.reshape(ntile, 1, tl), x1_prev, y_sorted, mod_m, lnpg, lnpb,
      mod_a, mod_b, cnt, h2c_s,
      p["w_grp"], p["pool_scale"], p["w_out1"], p["ln_g10"], p["ln_b10"],
      p["wr_hi1"], p["wr_lo1"], p["br1"])


def _mixer1_sample_kernel(dest_ref, xin_ref, y_any, modm_ref, lnpg_ref, lnpb_ref, moda_ref,
                          modb_ref, spool_ref, cnt_in_ref, wg_ref, ps_ref, w_out_ref,
                          lng_ref, lnb_ref, wr_hi_ref, wr_lo_ref, br_ref,
                          x1_ref, h2c_ref, route_ref, cnt_ref, npool_ref,
                          x_ref, h_ref, pooled_ref, mbuf, sem):
    i = pl.program_id(0)
    nb = spool_ref.shape[1]

    @pl.when(i == 0)
    def _():
        _start_row_gather(dest_ref, y_any, mbuf, sem)
        _wait_row_gather(y_any, mbuf, sem)
        x_ref[...] = _block_input(xin_ref, mbuf, modm_ref[...], lnpg_ref, lnpb_ref)
        shift, scale, _ = _mod3(moda_ref[...])
        h_ref[...] = x_ref[...] * (1.0 + scale) + shift

    lane = lax.broadcasted_iota(I32, (POOL_BUF, D_MODEL), 1)
    rowi = lax.broadcasted_iota(I32, (POOL_BUF, D_MODEL), 0)
    first = jnp.zeros((POOL_BUF, D_MODEL), I32)
    lane1 = lax.broadcasted_iota(I32, (1, D_MODEL), 1)
    inv_w = jnp.zeros((1, D_MODEL), F32)
    for gi, w in enumerate(POOL_WINDOWS):
        in_g = (lane >= gi * POOL_GC) & (lane < (gi + 1) * POOL_GC)
        first = jnp.where(in_g, POOL_BUF - (w - 1), first)
        in_g1 = (lane1 >= gi * POOL_GC) & (lane1 < (gi + 1) * POOL_GC)
        inv_w = jnp.where(in_g1, 1.0 / w, inv_w)
    keep = rowi >= first

    blk = pl.ds(pl.multiple_of(i * nb, nb), nb)
    h_blk = h_ref[blk, :]
    rows = []
    for n in range(nb):
        h_row = h_blk[n:n + 1, :]
        past = spool_ref[0, n]
        total = jnp.sum(jnp.where(keep, past, 0.0), axis=0, keepdims=True) + h_row
        rows.append(total * inv_w)
        npool_ref[0, n, 0:POOL_BUF - 1, :] = spool_ref[0, n, 1:POOL_BUF, :]
        npool_ref[0, n, POOL_BUF - 1:POOL_BUF, :] = h_row
    pooled_ref[blk, :] = jnp.concatenate(rows, axis=0)

    @pl.when(i == pl.num_programs(0) - 1)
    def _():
        m = _pool_project(pooled_ref[...], h_ref[...], wg_ref, ps_ref, w_out_ref)
        _, _, gate = _mod3(moda_ref[...])
        cnt_ref[...] = _post_mixer(x_ref[...], m, gate, lng_ref, lnb_ref, modb_ref[...], wr_hi_ref,
                                   wr_lo_ref, br_ref, cnt_in_ref[...], x1_ref, h2c_ref, route_ref)


def _mixer1_sample_call(x1_prev, dest, y_sorted, mod_m, lnpg, lnpb, mod_a, mod_b, state_pool,
                        cnt, p):
    ntok = x1_prev.shape[0]
    nb = SAMPLE_BLK
    ng = len(POOL_WINDOWS)
    tok = lambda i: (0, 0)
    return pl.pallas_call(
        _mixer1_sample_kernel,
        grid=(ntok // nb,),
        in_specs=[
            pl.BlockSpec((1, 1, ntok), lambda i: (0, 0, 0), memory_space=pltpu.SMEM),
            _full((ntok, D_MODEL)),
            pl.BlockSpec(memory_space=pl.ANY),
            _full((ntok, 3 * D_MODEL)),
            _full((1, D_MODEL)),
            _full((1, D_MODEL)),
            _full((ntok, 3 * D_MODEL)),
            _full((ntok, 3 * D_MODEL)),
            pl.BlockSpec((1, nb, POOL_BUF, D_MODEL), lambda i: (0, i, 0, 0)),
            _full((1, LANES)),
            _full((ng, POOL_GC, POOL_GC)),
            _full((1, D_MODEL)),
            _full((D_MODEL, D_MODEL)),
            _full((1, D_MODEL)),
            _full((1, D_MODEL)),
            _full((D_MODEL, LANES)),
            _full((D_MODEL, LANES)),
            _full((1, LANES)),
        ],
        out_specs=_token_out_specs(ntok, TILE_L, tok, tok) + [
            _full((1, LANES)),
            pl.BlockSpec((1, nb, POOL_BUF, D_MODEL), lambda i: (0, i, 0, 0)),
        ],
        out_shape=_token_out_shapes(ntok, TILE_L, TILE_L) + [
            jax.ShapeDtypeStruct((1, LANES), F32),
            jax.ShapeDtypeStruct(state_pool.shape, F32),
        ],
        scratch_shapes=[pltpu.VMEM((ntok, D_MODEL), F32)] * 3 + [
            pltpu.VMEM((ntok * CHUNKS, LANES), F32), pltpu.SemaphoreType.DMA],
        compiler_params=_params("arbitrary"),
        name="mixer1_sample",
    )(dest.reshape(1, 1, ntok), x1_prev, y_sorted, mod_m, lnpg, lnpb, mod_a, mod_b, state_pool, cnt,
      p["w_grp"], p["pool_scale"], p["w_out1"],
      p["ln_g10"], p["ln_b10"], p["wr_hi1"], p["wr_lo1"], p["br1"])


def _dispatch_kernel(sched_ref, dest_ref, src_ref, dst_any, zeros_ref, sem, zsem):
    n = dest_ref.shape[2]
    tile_rows = zeros_ref.shape[0]

    @pl.when(pl.program_id(0) == 0)
    def _():
        zeros_ref[...] = jnp.zeros_like(zeros_ref)

        def fill(i):
            return pltpu.make_async_copy(
                zeros_ref, dst_any.at[pl.ds(i * tile_rows, tile_rows)], zsem)

        for i in range(dst_any.shape[0] // tile_rows):
            pl.when(sched_ref[S_ZERO, i] == 1)(lambda i=i: fill(i).start())
        for i in range(dst_any.shape[0] // tile_rows):
            pl.when(sched_ref[S_ZERO, i] == 1)(lambda i=i: fill(i).wait())

    for j in range(n):
        d = dest_ref[0, 0, j]
        pltpu.make_async_copy(
            src_ref.at[pl.ds(j * CHUNKS, CHUNKS)],
            dst_any.at[pl.ds(pl.multiple_of(d * CHUNKS, CHUNKS), CHUNKS)], sem).start()
    pltpu.make_async_copy(src_ref, dst_any.at[pl.ds(0, n * CHUNKS)], sem).wait()


def _dispatch_call(sched, dest, h2c_all):
    nslot = dest.shape[1]
    rows = DISPATCH_ROWS
    tile_rows = MOE_TILE * CHUNKS
    grid_spec = pltpu.PrefetchScalarGridSpec(
        num_scalar_prefetch=1,
        grid=(nslot // rows,),
        in_specs=[
            pl.BlockSpec((1, 1, rows), lambda i, s: (i, 0, 0), memory_space=pltpu.SMEM),
            pl.BlockSpec((rows * CHUNKS, LANES), lambda i, s: (i, 0)),
        ],
        out_specs=pl.BlockSpec(memory_space=pl.ANY),
        scratch_shapes=[pltpu.VMEM((tile_rows, LANES), F32), pltpu.SemaphoreType.DMA,
                        pltpu.SemaphoreType.DMA],
    )
    return pl.pallas_call(
        _dispatch_kernel,
        grid_spec=grid_spec,
        out_shape=jax.ShapeDtypeStruct((_num_tiles(nslot) * tile_rows, LANES), F32),
        compiler_params=_params("arbitrary"),
        name="dispatch",
    )(sched, dest.reshape(nslot // rows, 1, rows), h2c_all)


(S_BLK, S_EA, S_EB, S_VALID, S_FIRST, S_LANE_G, S_LANE_A, S_LANE_B, S_ZERO) = range(9)
SCHED_ROWS = 2 * SUBLANES


def _moe_kernel(sched_ref, x_ref, wr_hi_ref, wr_lo_ref, br_ref,
                wga_ref, wua_ref, wda_ref, wgb_ref, wub_ref, wdb_ref,
                y_ref, wgu_ref, wd_ref):
    i = pl.program_id(0)
    tm = x_ref.shape[0] // CHUNKS

    @pl.when(sched_ref[S_FIRST, i] == 1)
    def _():
        for e, (wg, wu, wd) in enumerate(((wga_ref, wua_ref, wda_ref), (wgb_ref, wub_ref, wdb_ref))):
            wgu_ref[2 * e] = wg[0].astype(BF16)
            wgu_ref[2 * e + 1] = wu[0].astype(BF16)
            wd_ref[e * MOE_FF:(e + 1) * MOE_FF, :] = wd[0].astype(BF16)

    @pl.when(sched_ref[S_VALID, i] == 0)
    def _():
        y_ref[...] = jnp.zeros_like(y_ref)

    @pl.when(sched_ref[S_VALID, i] == 1)
    def _():
        x = _load_chunked(x_ref, tm, CHUNKS, 0)
        logits = _router_logits(x, wr_hi_ref, wr_lo_ref, br_ref)
        lane = lax.broadcasted_iota(I32, logits.shape, 1)

        def pick(row):
            return jnp.sum(jnp.where(lane == sched_ref[row, i], logits, 0.0), axis=-1, keepdims=True)

        l_g, l_a, l_b = pick(S_LANE_G), pick(S_LANE_A), pick(S_LANE_B)
        p_g = 1.0 / jnp.sum(jnp.where(lane < MOE_GROUPS, jnp.exp(logits - l_g), 0.0),
                            axis=-1, keepdims=True)
        w_ab = (p_g / (1.0 + jnp.exp(l_b - l_a)), p_g / (1.0 + jnp.exp(l_a - l_b)))
        xb = x.astype(BF16)
        hid = [(_silu(_dot(xb, wgu_ref[2 * e])) * _dot(xb, wgu_ref[2 * e + 1]) * w_ab[e]).astype(BF16)
               for e in range(2)]
        _store_chunked(y_ref, _dot(jnp.concatenate(hid, axis=1), wd_ref[...]))


def _moe_call(sched, sorted_x, wr_hi, wr_lo, br, wg, wu, wd):
    tm = MOE_TILE
    ntile = sorted_x.shape[0] // (tm * CHUNKS)
    ea = lambda i, s: (s[S_EA, i], 0, 0)
    eb = lambda i, s: (s[S_EB, i], 0, 0)
    const2 = lambda i, s: (0, 0)
    grid_spec = pltpu.PrefetchScalarGridSpec(
        num_scalar_prefetch=1,
        grid=(ntile,),
        in_specs=[
            pl.BlockSpec((tm * CHUNKS, LANES), lambda i, s: (s[S_BLK, i], 0)),
            pl.BlockSpec((D_MODEL, LANES), const2),
            pl.BlockSpec((D_MODEL, LANES), const2),
            pl.BlockSpec((1, LANES), const2),
            pl.BlockSpec((1, D_MODEL, MOE_FF), ea),
            pl.BlockSpec((1, D_MODEL, MOE_FF), ea),
            pl.BlockSpec((1, MOE_FF, D_MODEL), ea),
            pl.BlockSpec((1, D_MODEL, MOE_FF), eb),
            pl.BlockSpec((1, D_MODEL, MOE_FF), eb),
            pl.BlockSpec((1, MOE_FF, D_MODEL), eb),
        ],
        out_specs=pl.BlockSpec((tm * CHUNKS, LANES), lambda i, s: (i, 0)),
        scratch_shapes=[
            pltpu.VMEM((4, D_MODEL, MOE_FF), BF16),
            pltpu.VMEM((2 * MOE_FF, D_MODEL), BF16),
        ],
    )
    return pl.pallas_call(
        _moe_kernel,
        grid_spec=grid_spec,
        out_shape=jax.ShapeDtypeStruct((ntile * tm * CHUNKS, LANES), F32),
        compiler_params=_params("arbitrary"),
        name="moe",
    )(sched, sorted_x, wr_hi, wr_lo, br, wg, wu, wd, wg, wu, wd)


def _start_row_gather(dest_ref, y_any, buf, sem):
    for j in range(dest_ref.shape[2]):
        d = dest_ref[0, 0, j]
        pltpu.make_async_copy(
            y_any.at[pl.ds(pl.multiple_of(d * CHUNKS, CHUNKS), CHUNKS)],
            buf.at[pl.ds(j * CHUNKS, CHUNKS)], sem).start()


def _wait_row_gather(y_any, buf, sem):
    pltpu.make_async_copy(y_any.at[pl.ds(0, buf.shape[0])], buf, sem).wait()


def _gather_pipelined(step, nstep, dest_ref, dnext_ref, y_any, mbuf, sems):
    slot = step % 2

    @pl.when(step == 0)
    def _():
        _start_row_gather(dest_ref, y_any, mbuf.at[0], sems.at[0])

    _wait_row_gather(y_any, mbuf.at[slot], sems.at[slot])
    _start_row_gather(dnext_ref, y_any, mbuf.at[1 - slot], sems.at[1 - slot])

    def drain():
        @pl.when(step == nstep - 1)
        def _():
            _wait_row_gather(y_any, mbuf.at[1 - slot], sems.at[1 - slot])

    return mbuf.at[slot], drain


def _post_moe_kernel(dest_ref, dnext_ref, x1_ref, y_any, mod_ref, lng_ref, lnb_ref, out_ref,
                     mbuf, sems):
    m_ref, drain = _gather_pipelined(pl.program_id(0), pl.num_programs(0), dest_ref, dnext_ref,
                                     y_any, mbuf, sems)
    out_ref[...] = _block_input(x1_ref, m_ref, mod_ref[0], lng_ref, lnb_ref)
    drain()


def _post_moe_call(x1, dest, y_sorted, mod, lng, lnb, rows, steps_per_mod):
    mrows = mod.shape[1]
    nstep = x1.shape[0] // rows
    dest3 = dest.reshape(nstep, 1, rows)
    return pl.pallas_call(
        _post_moe_kernel,
        grid=(nstep,),
        in_specs=[
            pl.BlockSpec((1, 1, rows), lambda i: (i, 0, 0), memory_space=pltpu.SMEM),
            pl.BlockSpec((1, 1, rows), lambda i: (jnp.minimum(i + 1, nstep - 1), 0, 0),
                         memory_space=pltpu.SMEM),
            pl.BlockSpec((rows, D_MODEL), lambda i: (i, 0)),
            pl.BlockSpec(memory_space=pl.ANY),
            pl.BlockSpec((1, mrows, 3 * D_MODEL), lambda i: (i // steps_per_mod, 0, 0)),
            _full((1, D_MODEL)),
            _full((1, D_MODEL)),
        ],
        out_specs=pl.BlockSpec((rows, D_MODEL), lambda i: (i, 0)),
        out_shape=jax.ShapeDtypeStruct(x1.shape, F32),
        scratch_shapes=[pltpu.VMEM((2, rows * CHUNKS, LANES), F32), pltpu.SemaphoreType.DMA((2,))],
        compiler_params=_params("arbitrary"),
        name="post_moe",
    )(dest3, dest3, x1, y_sorted, mod, lng, lnb)


PLAN_CHUNK = 2048


def _num_tiles(nslot):
    return -(-nslot // MOE_TILE) + N_BUCKETS


def _plan_kernel(rp_ref, rs_ref, cnt_ref, dest_ref, sched_ref, *, layer, ntile):
    tm = MOE_TILE
    lane = lax.broadcasted_iota(I32, (SUBLANES, LANES), 1)
    row = lax.broadcasted_iota(I32, (LANES, LANES), 0)
    col = lax.broadcasted_iota(I32, (LANES, LANES), 1)
    counts = jnp.where(lane < N_BUCKETS, jnp.broadcast_to(cnt_ref[...], (SUBLANES, LANES)), 0.0)
    tiles_b = jnp.floor((counts + float(tm - 1)) * (1.0 / tm))
    tile_end = _dot(tiles_b.astype(BF16), jnp.where(row <= col, 1.0, 0.0).astype(BF16))
    tile_start = tile_end - tiles_b
    start_b = tile_start.astype(BF16)

    def dest_of(route):
        n = route.shape[1]
        bucket, rank = route[R_BUCKET:R_BUCKET + 1, :], route[R_RANK:R_RANK + 1, :]
        b_iota = lax.broadcasted_iota(I32, (LANES, n), 0).astype(F32)
        onehot = jnp.where(b_iota == bucket, 1.0, 0.0).astype(BF16)
        start = _dot(start_b, onehot)[0:1, :]
        return (start * float(tm) + rank).astype(I32)

    ntok_p = rp_ref.shape[1]
    chunk = math.gcd(ntok_p, PLAN_CHUNK)
    for c0 in range(0, ntok_p, chunk):
        dest_ref[:, c0:c0 + chunk] = dest_of(rp_ref[:, c0:c0 + chunk])
    dest_ref[:, ntok_p:] = dest_of(rs_ref[...])

    tile = lane.astype(F32)
    used = jnp.sum(jnp.where(lane == N_BUCKETS - 1, tile_end, 0.0), axis=-1, keepdims=True)
    ti = jnp.minimum(tile, used - 1.0)
    end_col = jnp.transpose(jnp.broadcast_to(tile_end[0:1], (LANES, LANES)))
    start_col = jnp.transpose(jnp.broadcast_to(tile_start[0:1], (LANES, LANES)))
    ti_rows = jnp.broadcast_to(ti[0:1], (LANES, LANES))
    b_of = jnp.sum(jnp.where((row < N_BUCKETS) & (ti_rows >= end_col), 1.0, 0.0),
                   axis=0, keepdims=True)
    start_of = jnp.sum(jnp.where(row.astype(F32) == b_of, start_col, 0.0), axis=0, keepdims=True)
    grp = sum(jnp.where(b_of >= float(g * N_PAIRS), 1.0, 0.0) for g in range(1, MOE_GROUPS))
    pair = b_of - float(N_PAIRS) * grp
    ex_a = sum(jnp.where(pair == float(j), float(PAIR_A[j]), 0.0) for j in range(N_PAIRS))
    ex_b = sum(jnp.where(pair == float(j), float(PAIR_B[j]), 0.0) for j in range(N_PAIRS))
    first_expert = float(layer * N_EXPERTS) + float(MOE_EXPERTS) * grp
    first_lane = float(MOE_GROUPS) + float(MOE_EXPERTS) * grp
    end_of = jnp.sum(jnp.where(row.astype(F32) == b_of, end_col, 0.0), axis=0, keepdims=True)
    valid = jnp.where(tile[0:1] < used[0:1], 1.0, 0.0)
    first = valid * jnp.where(ti[0:1] == start_of, 1.0, 0.0)
    partial = jnp.where((tile[0:1] == end_of - 1.0) | (tile[0:1] >= used[0:1]), 1.0, 0.0)
    zero = jnp.where(tile[0:1] < float(ntile), partial, 0.0)
    rows = {S_BLK: ti[0:1], S_EA: first_expert + ex_a, S_EB: first_expert + ex_b, S_VALID: valid,
            S_FIRST: first, S_LANE_G: grp, S_LANE_A: first_lane + ex_a, S_LANE_B: first_lane + ex_b,
            S_ZERO: zero}
    sub = lax.broadcasted_iota(I32, (SCHED_ROWS, LANES), 0)
    sched = sum(jnp.where(sub == r, jnp.broadcast_to(v, (SCHED_ROWS, LANES)), 0.0)
                for r, v in rows.items())
    sched_ref[...] = sched.astype(I32)


def _moe_plan(route_p, route_s, cnt, layer):
    nslot = route_p.shape[1] + route_s.shape[1]
    assert _num_tiles(nslot) <= LANES
    return pl.pallas_call(
        functools.partial(_plan_kernel, layer=layer, ntile=_num_tiles(nslot)),
        out_shape=[jax.ShapeDtypeStruct((1, nslot), I32),
                   jax.ShapeDtypeStruct((SCHED_ROWS, LANES), I32)],
        compiler_params=pltpu.CompilerParams(vmem_limit_bytes=VMEM_LIMIT),
        name="plan",
    )(route_p, route_s, cnt)


def _router_weights(w_coarse, b_coarse, w_fine, b_fine):
    wf = jnp.transpose(w_fine, (1, 0, 2)).reshape(D_MODEL, N_EXPERTS)
    w = jnp.concatenate([w_coarse, wf], axis=1)
    w = jnp.pad(w, ((0, 0), (0, LANES - w.shape[1])))
    b = jnp.concatenate([b_coarse, b_fine.reshape(N_EXPERTS)])
    b = jnp.pad(b, (0, LANES - b.shape[0])).reshape(1, LANES)
    hi_lo = _hi_lo(w)
    return hi_lo[0], hi_lo[1], b


def _prep_params(ln_g, ln_b, w_in_even, w_a2, b_a, gla_norm_g, conv_w, conv_b, conv_ln_g,
                 conv_ln_b, w_out_even, w_grp_pool, pool_scale, w_out_odd, w_coarse, b_coarse,
                 w_fine, b_fine):
    p = {}
    w_in = w_in_even[0]
    o_q, o_k, o_v, o_g = 0, GLA_KW, 2 * GLA_KW, 2 * GLA_KW + GLA_VW
    o_a = o_g + GLA_VW
    o_u = o_a + GLA_RANK
    w_in_r = jnp.concatenate(
        [w_in[:, o_q:o_a], w_in[:, o_u:o_u + 2 * CONV_CH], w_in[:, o_a:o_u],
         jnp.zeros((D_MODEL, A_PAD - GLA_RANK), F32)], axis=1)
    p["w_in"] = _hi_lo(w_in_r)
    p["w_a2"] = _hi_lo(jnp.pad(w_a2[0], ((0, A_PAD - GLA_RANK), (0, 0))))
    p["b_a"] = b_a[0].reshape(1, GLA_KW)
    p["gng"] = gla_norm_g[0].reshape(1, GLA_VW)
    p["conv_w"] = jnp.pad(conv_w[0], ((0, CONV_W_ROWS - CONV_WIDTH), (0, 0)))
    p["conv_b"] = conv_b[0].reshape(1, CONV_CH)
    p["conv_ln_g"] = conv_ln_g[0].reshape(1, CONV_CH)
    p["conv_ln_b"] = conv_ln_b[0].reshape(1, CONV_CH)
    p["w_out0"] = _hi_lo(w_out_even[0])
    p["w_grp"] = w_grp_pool[0].astype(BF16)
    p["pool_scale"] = pool_scale[0].reshape(1, D_MODEL)
    p["w_out1"] = w_out_odd[0].astype(BF16)
    for layer in range(DEPTH):
        for j in range(2):
            p[f"ln_g{layer}{j}"] = ln_g[layer, j].reshape(1, D_MODEL)
            p[f"ln_b{layer}{j}"] = ln_b[layer, j].reshape(1, D_MODEL)
        hi, lo, b = _router_weights(w_coarse[layer], b_coarse[layer], w_fine[layer], b_fine[layer])
        p[f"wr_hi{layer}"], p[f"wr_lo{layer}"], p[f"br{layer}"] = hi, lo, b
    return p


def kernel(x_prompt, x_sample, state_gla, state_conv, state_pool, c_prompt, c_sample, w_ada, b_ada,
           ln_g, ln_b, w_in_even, w_a2, b_a, gla_norm_g, conv_w, conv_b, conv_ln_g, conv_ln_b,
           w_out_even, w_grp_pool, pool_scale, w_out_odd, w_coarse, b_coarse, w_fine, b_fine,
           w_gate, w_up, w_down):
    bsz, seq, _ = x_prompt.shape
    nsmp = x_sample.shape[0]
    ntok_p = bsz * seq
    assert seq % TILE_L == 0 and nsmp <= TILE_L and nsmp % SAMPLE_BLK == 0 and ntok_p % nsmp == 0
    p = _prep_params(ln_g, ln_b, w_in_even, w_a2, b_a, gla_norm_g, conv_w, conv_b, conv_ln_g,
                     conv_ln_b, w_out_even, w_grp_pool, pool_scale, w_out_odd, w_coarse, b_coarse,
                     w_fine, b_fine)
    wg = w_gate.reshape(DEPTH * N_EXPERTS, D_MODEL, MOE_FF)
    wu = w_up.reshape(DEPTH * N_EXPERTS, D_MODEL, MOE_FF)
    wd = w_down.reshape(DEPTH * N_EXPERTS, MOE_FF, D_MODEL)

    c_all = jnp.concatenate([c_prompt, c_sample], axis=0)
    mod = _ada_call(c_all, w_ada.reshape(2 * DEPTH, D_MODEL, 3 * D_MODEL),
                    b_ada.reshape(2 * DEPTH, 1, 3 * D_MODEL))
    mod_p = mod[:, :bsz].reshape(2 * DEPTH, bsz, 1, 3 * D_MODEL)
    mod_s = mod[:, bsz:]

    def moe(h2c_all, route_p, route_s, cnt, layer):
        dest, sched = _moe_plan(route_p, route_s, cnt, layer)
        sorted_x = _dispatch_call(sched, dest, h2c_all)
        y_sorted = _moe_call(sched, sorted_x, p[f"wr_hi{layer}"], p[f"wr_lo{layer}"],
                             p[f"br{layer}"], wg, wu, wd)
        return dest[0, :ntok_p], dest[0, ntok_p:ntok_p + nsmp], y_sorted

    xs0 = x_sample.reshape(nsmp, D_MODEL)
    cnt0 = jnp.zeros((1, LANES), F32)
    x1s, h2c_s, route_s, cnt, gla_s, conv_s = _mixer0_sample_call(
        xs0, mod_s[0], mod_s[1], state_gla, state_conv, cnt0, p)
    x1p, h2c_all, route_p, cnt, conv_p, gla_p = _mixer0_prompt_call(
        x_prompt, mod_p[0], mod_p[1], cnt, h2c_s, p)
    dest_p, dest_s, y_sorted = moe(h2c_all, route_p, route_s, cnt, 0)
    x3s, h2c_s, route_s, cnt, pool_s = _mixer1_sample_call(
        x1s, dest_s, y_sorted, mod_s[1], p["ln_g01"], p["ln_b01"], mod_s[2], mod_s[3],
        state_pool, cnt0, p)
    x3p, h2c_all, route_p, cnt, pool_p = _mixer1_prompt_call(
        x1p, dest_p, y_sorted, mod_p[1], p["ln_g01"], p["ln_b01"], mod_p[2], mod_p[3], cnt,
        h2c_s, bsz, p)
    dest_p, dest_s, y_sorted = moe(h2c_all, route_p, route_s, cnt, 1)
    x4p = _post_moe_call(x3p, dest_p, y_sorted, mod_p[3], p["ln_g11"], p["ln_b11"], TILE_L,
                         seq // TILE_L).reshape(bsz, seq, D_MODEL)
    x4s = _post_moe_call(x3s, dest_s, y_sorted, mod_s[3][None], p["ln_g11"], p["ln_b11"], nsmp, 1)
    return (x4p, x4s.reshape(nsmp, 1, D_MODEL), gla_p, conv_p, pool_p, gla_s, conv_s, pool_s)
```

```python
import functools
import math

import jax
import jax.numpy as jnp
from jax import lax
from jax.experimental import pallas as pl
from jax.experimental.pallas import tpu as pltpu

F32 = jnp.float32
BF16 = jnp.bfloat16
I32 = jnp.int32

D_MODEL = 1024
GLA_HEADS = 4
GLA_DK = 64
GLA_DV = 128
GLA_KW = GLA_HEADS * GLA_DK
GLA_VW = GLA_HEADS * GLA_DV
GLA_RANK = 16
GLA_TAU = 16.0
CONV_CH = 512
CONV_WIDTH = 31
CONV_BUF = CONV_WIDTH - 1
POOL_WINDOWS = (2, 4, 8, 16)
POOL_GC = D_MODEL // len(POOL_WINDOWS)
POOL_BUF = max(POOL_WINDOWS) - 1
MOE_GROUPS = 4
MOE_EXPERTS = 4
N_EXPERTS = MOE_GROUPS * MOE_EXPERTS
N_PAIRS = MOE_EXPERTS * (MOE_EXPERTS - 1) // 2
N_BUCKETS = MOE_GROUPS * N_PAIRS
PAIR_A = (0, 0, 0, 1, 1, 2)
PAIR_B = (1, 2, 3, 2, 3, 3)
MOE_FF = 512
R_BUCKET, R_RANK = 0, 1
DEPTH = 2
ALPHA = (2 * DEPTH) ** 0.25
LN_EPS = 1e-5

LANES = 128
SUBLANES = 8
CHUNKS = D_MODEL // LANES
A_PAD = LANES
Z_WIDTH = 2 * GLA_KW + 2 * GLA_VW + 2 * CONV_CH + A_PAD
NEG_BIG = -1e30
VMEM_LIMIT = 56 * 1024 * 1024

TILE_L = 256
CONV_HALO = 32
CONV_W_ROWS = 32
POOL_HALO = 16
SAMPLE_BLK = 16
MOE_TILE = 256
DISPATCH_ROWS = 640


def _dot(a, b):
    return jnp.dot(a, b, preferred_element_type=F32)


def _dot_nt(a, b):
    return lax.dot_general(a, b, (((1,), (1,)), ((), ())), preferred_element_type=F32)


def _split3(x):
    hi = x.astype(BF16)
    r1 = x - hi.astype(F32)
    mid = r1.astype(BF16)
    lo = (r1 - mid.astype(F32)).astype(BF16)
    return hi, mid, lo


def _split2(x):
    hi = x.astype(BF16)
    lo = (x - hi.astype(F32)).astype(BF16)
    return hi, lo


def _dot_w3(a, w_hi, w_lo):
    a_hi, a_lo = _split2(a)
    return _dot(a_hi, w_hi) + (_dot(a_lo, w_hi) + _dot(a_hi, w_lo))


def _lhs3(parts):
    hi, lo = parts
    return jnp.concatenate([hi, lo, hi], axis=1)


def _rhs3_rows(x):
    hi, lo = _split2(x)
    return jnp.concatenate([hi, hi, lo], axis=0)


def _rhs3_lanes(parts):
    hi, lo = parts
    return jnp.concatenate([hi, hi, lo], axis=1)


def _layer_norm(x, g, b):
    mu = jnp.mean(x, axis=-1, keepdims=True)
    xc = x - mu
    var = jnp.mean(xc * xc, axis=-1, keepdims=True)
    return xc * lax.rsqrt(var + LN_EPS) * g + b


def _standardize(x):
    mu = jnp.mean(x, axis=-1, keepdims=True)
    xc = x - mu
    var = jnp.mean(xc * xc, axis=-1, keepdims=True)
    return xc * lax.rsqrt(var + LN_EPS)


def _sigmoid(x):
    return 1.0 / (1.0 + jnp.exp(-x))


def _silu(x):
    return x * _sigmoid(x)


def _log_sigmoid(x):
    return jnp.minimum(x, 0.0) - jnp.log(1.0 + jnp.exp(-jnp.abs(x)))


def _mod3(mod):
    return mod[:, 0:D_MODEL], mod[:, D_MODEL:2 * D_MODEL], mod[:, 2 * D_MODEL:3 * D_MODEL]


def _full(shape):
    nd = len(shape)
    return pl.BlockSpec(shape, lambda *_: (0,) * nd)


def _const(shape):
    nd = len(shape)
    return pl.BlockSpec(shape, lambda *_: (0,) * nd, pipeline_mode=pl.Buffered(1))


def _hi_lo(w):
    bits = lax.bitcast_convert_type(w, jnp.uint32) & jnp.uint32(0xFFFF0000)
    hi = lax.bitcast_convert_type(bits, F32)
    return jnp.stack([hi.astype(BF16), (w - hi).astype(BF16)])


def _params(*sem):
    return pltpu.CompilerParams(dimension_semantics=sem, vmem_limit_bytes=VMEM_LIMIT)


def _ada_kernel(c_ref, w_ref, b_ref, o_ref):
    w_hi, w_lo = _split2(w_ref[0])
    o_ref[0] = _dot_w3(_silu(c_ref[...]), w_hi, w_lo) + b_ref[0]


def _ada_call(c_all, w_ada, b_ada):
    n = c_all.shape[0]
    ncomb = w_ada.shape[0]
    tn = D_MODEL
    return pl.pallas_call(
        _ada_kernel,
        grid=(ncomb, 3 * D_MODEL // tn),
        in_specs=[
            pl.BlockSpec((n, D_MODEL), lambda i, j: (0, 0)),
            pl.BlockSpec((1, D_MODEL, tn), lambda i, j: (i, 0, j)),
            pl.BlockSpec((1, 1, tn), lambda i, j: (i, 0, j)),
        ],
        out_specs=pl.BlockSpec((1, n, tn), lambda i, j: (i, 0, j)),
        out_shape=jax.ShapeDtypeStruct((ncomb, n, 3 * D_MODEL), F32),
        compiler_params=_params("arbitrary", "arbitrary"),
        name="ada",
    )(c_all, w_ada, b_ada)


def _route(logits, carry):
    rows = logits.shape[0]
    lane = lax.broadcasted_iota(I32, (rows, LANES), 1)
    lanef = lane.astype(F32)
    big = float(LANES)
    lc = jnp.where(lane < MOE_GROUPS, logits, NEG_BIG)
    mc = jnp.max(lc, axis=-1, keepdims=True)
    gidx = jnp.min(jnp.where(lc == mc, lanef, big), axis=-1, keepdims=True)
    lo = float(MOE_GROUPS) + gidx * float(MOE_EXPERTS)
    in_grp = (lanef >= lo) & (lanef < lo + float(MOE_EXPERTS))
    lf = jnp.where(in_grp, logits, NEG_BIG)
    m1 = jnp.max(lf, axis=-1, keepdims=True)
    i1 = jnp.min(jnp.where(lf == m1, lanef, big), axis=-1, keepdims=True)
    lf2 = jnp.where(lanef == i1, NEG_BIG, lf)
    m2 = jnp.max(lf2, axis=-1, keepdims=True)
    i2 = jnp.min(jnp.where(lf2 == m2, lanef, big), axis=-1, keepdims=True)
    ea = jnp.minimum(i1, i2) - lo
    eb = jnp.maximum(i1, i2) - lo
    bucket = gidx * float(N_PAIRS) + ea * (7.0 - ea) * 0.5 + eb - ea - 1.0
    onehot = jnp.where(lanef == bucket, 1.0, 0.0)
    r = lax.broadcasted_iota(I32, (rows, rows), 0)
    c = lax.broadcasted_iota(I32, (rows, rows), 1)
    before = _dot(jnp.where(r > c, 1.0, 0.0).astype(BF16), onehot.astype(BF16))
    rank = jnp.sum(onehot * (before + carry), axis=-1, keepdims=True)
    new_carry = carry + jnp.sum(onehot, axis=0, keepdims=True)
    cols = jnp.where(lane == R_BUCKET, bucket, 0.0) + jnp.where(lane == R_RANK, rank, 0.0)
    return jnp.transpose(cols)[0:SUBLANES, :], new_carry


def _store_chunked(ref, x):
    rows = x.shape[0]
    for c in range(CHUNKS):
        ref[pl.ds(c, rows, stride=CHUNKS), :] = x[:, c * LANES:(c + 1) * LANES]


def _load_chunked(ref, rows, group, first):
    return jnp.concatenate(
        [ref[pl.ds(first + c, rows, stride=group), :] for c in range(CHUNKS)], axis=1)


def _router_logits(h2, wr_hi_ref, wr_lo_ref, br_ref):
    h_hi, h_lo = _split2(h2)
    wr_hi = wr_hi_ref[...]
    return _dot(h_hi, wr_hi) + _dot(h_lo, wr_hi) + _dot(h_hi, wr_lo_ref[...]) + br_ref[...]


def _post_mixer(x, m, gate, lng_ref, lnb_ref, mod_b, wr_hi_ref, wr_lo_ref, br_ref, carry,
                x1_ref, h2c_ref, route_ref):
    x1 = _layer_norm(ALPHA * x + (1.0 + gate) * m, lng_ref[...], lnb_ref[...])
    shift2, scale2, _ = _mod3(mod_b)
    h2 = x1 * (1.0 + scale2) + shift2
    pad = route_ref.shape[1] - x.shape[0]
    if pad:
        h2 = jnp.concatenate([h2, jnp.zeros((pad, D_MODEL), F32)], axis=0)
    route, carry = _route(_router_logits(h2, wr_hi_ref, wr_lo_ref, br_ref), carry)
    x1_ref[...] = x1
    _store_chunked(h2c_ref, h2)
    route_ref[...] = route
    return carry


def _token_out_specs(rows, slot_rows, x1_map, slot_map):
    return [pl.BlockSpec((rows, D_MODEL), x1_map),
            pl.BlockSpec((slot_rows * CHUNKS, LANES), slot_map),
            pl.BlockSpec((SUBLANES, slot_rows), lambda *i: (0, x1_map(*i)[0]))]


def _token_out_shapes(ntok, nslot_h2, nslot_route):
    return [jax.ShapeDtypeStruct((ntok, D_MODEL), F32),
            jax.ShapeDtypeStruct((nslot_h2 * CHUNKS, LANES), F32),
            jax.ShapeDtypeStruct((SUBLANES, nslot_route), F32)]


def _hold_rows(b, s):
    n, w = b.shape
    if s >= SUBLANES:
        pieces = []
        for p in range(n // (2 * s)):
            r = p * 2 * s + s - 1
            pieces.append(jnp.broadcast_to(b[r:r + 1, :], (2 * s, w)))
        return pieces[0] if len(pieces) == 1 else jnp.concatenate(pieces, axis=0)
    b3 = b.reshape(n // SUBLANES, SUBLANES, w)
    sub = lax.broadcasted_iota(I32, b3.shape, 1)

    def bc(r):
        return jnp.broadcast_to(b3[:, r:r + 1, :], b3.shape)

    out = bc(s - 1)
    for p in range(1, SUBLANES // (2 * s)):
        out = jnp.where(sub >= p * 2 * s, bc(p * 2 * s + s - 1), out)
    return out.reshape(n, w)


def _gla_tile(q, k, v, la, s_prev):
    n = q.shape[0]
    row = lax.broadcasted_iota(I32, (n, n), 0)
    col = lax.broadcasted_iota(I32, (n, n), 1)
    tri = jnp.where(row >= col, 1.0, 0.0).astype(BF16)
    hi, mid, lo = _split3(la)
    b = _dot(tri, hi) + _dot(tri, mid) + _dot(tri, lo)
    b_last = b[n - 1:n, :]
    kl = k * jnp.exp(b_last - b)

    rowi = lax.broadcasted_iota(I32, (n, GLA_KW), 0)
    pair_lanes = [slice(p * LANES, (p + 1) * LANES) for p in range(GLA_HEADS // 2)]

    def swapped(x):
        return [pltpu.roll(x[:, sl], GLA_DK, axis=1).astype(BF16) for sl in pair_lanes]

    def level_operands(qf, kf):
        q_hi = qf.astype(BF16)
        k_hi, k_lo = _split2(kf)
        return q_hi, swapped(qf - q_hi.astype(F32)), k_hi, swapped(k_hi.astype(F32)), k_lo

    levels = []
    s = n // 2
    while s >= 1:
        m = _hold_rows(b, s)
        second = (rowi & s) != 0
        levels.append((2 * s, level_operands(q * jnp.exp(jnp.where(second, b - m, NEG_BIG)),
                                             k * jnp.exp(jnp.where(second, NEG_BIG, m - b)))))
        s //= 2
    levels.append((1, level_operands(q, k)))
    qe = _split2(q * jnp.exp(b))

    lane = lax.broadcasted_iota(I32, (n, LANES), 1)
    xor = row ^ col
    zero = jnp.zeros((), BF16)

    def head_part(parts, sl, hm):
        return tuple(jnp.where(hm, x[:, sl], zero) for x in parts)

    outs = []
    for h in range(GLA_HEADS):
        p = h // 2
        sl = pair_lanes[p]
        hm = (lane < GLA_DK) if h % 2 == 0 else (lane >= GLA_DK)
        att = None
        for span, (q_hi, q_lo_sw, k_hi, k_hi_sw, k_lo) in levels:
            lhs = jnp.concatenate([jnp.where(hm, q_hi[:, sl], q_lo_sw[p]),
                                   jnp.where(hm, q_hi[:, sl], zero)], axis=1)
            rhs = jnp.concatenate([jnp.where(hm, k_hi[:, sl], k_hi_sw[p]), k_lo[:, sl]], axis=1)
            a_s = _dot_nt(lhs, rhs)
            att = a_s if att is None else jnp.where(xor < span, a_s, att)
        o_h = _dot(_lhs3(_split2(att)), _rhs3_rows(v[:, h * GLA_DV:(h + 1) * GLA_DV]))
        o_h = o_h + _dot(_lhs3(head_part(qe, sl, hm)), _rhs3_rows(s_prev[sl, :]))
        outs.append(o_h)

    decay = jnp.exp(b_last)
    upper = lax.broadcasted_iota(I32, (LANES, LANES), 0) < GLA_DK
    s_new = []
    for p in range(GLA_HEADS // 2):
        sl = slice(p * LANES, (p + 1) * LANES)
        kl_t = jnp.transpose(kl[:, sl])
        u = _dot(_lhs3(_split2(kl_t)), _rhs3_rows(v[:, p * 2 * GLA_DV:(p + 1) * 2 * GLA_DV]))
        upd = jnp.where(upper, u[:, 0:GLA_DV], u[:, GLA_DV:2 * GLA_DV])
        dcol = jnp.transpose(jnp.broadcast_to(decay[:, sl], (LANES, LANES)))
        s_new.append(dcol * s_prev[sl, :] + upd)
    return outs, jnp.concatenate(s_new, axis=0)


def _split_projection(z):
    c0 = 0
    q = z[:, c0:c0 + GLA_KW] * (GLA_DK ** -0.5); c0 += GLA_KW
    k = z[:, c0:c0 + GLA_KW]; c0 += GLA_KW
    v = z[:, c0:c0 + GLA_VW]; c0 += GLA_VW
    g = z[:, c0:c0 + GLA_VW]; c0 += GLA_VW
    ua = z[:, c0:c0 + CONV_CH]; c0 += CONV_CH
    ug = z[:, c0:c0 + CONV_CH]; c0 += CONV_CH
    a_lr = z[:, c0:c0 + A_PAD]
    return q, k, v, g, ua, ug, a_lr


def _mix0_project(o_heads, g, y, gng_ref, w_out_ref):
    sl = slice(GLA_VW, GLA_VW + CONV_CH)
    m = _dot_w3(y, w_out_ref[0, sl, :], w_out_ref[1, sl, :])
    for hd in range(GLA_HEADS):
        sl = slice(hd * GLA_DV, (hd + 1) * GLA_DV)
        o_h = _standardize(o_heads[hd]) * gng_ref[:, sl] * _silu(g[:, sl])
        m = m + _dot_w3(o_h, w_out_ref[0, sl, :], w_out_ref[1, sl, :])
    return m


def _mix0_inputs(h, w_in_ref, w_a2_ref, b_a_ref):
    z = _dot_w3(h, w_in_ref[0], w_in_ref[1])
    q, k, v, g, ua, ug, a_lr = _split_projection(z)
    la = _log_sigmoid(_dot_w3(a_lr, w_a2_ref[0], w_a2_ref[1]) + b_a_ref[...]) * (1.0 / GLA_TAU)
    return q, k, v, g, ua, ug, la


def _prompt_grid_specs(bsz, nt):
    ntile = bsz * nt
    seq_map = lambda b, t: (jnp.minimum(b, bsz - 1), 0, 0)
    x_map = lambda b, t: (jnp.minimum(b, bsz - 1), t, 0)
    tok_map = lambda b, t: (jnp.minimum(b * nt + t, ntile - 1), 0)
    slot_map = lambda b, t: (jnp.minimum(b * nt + t, ntile), 0)
    state_map = lambda b, t: (0, jnp.minimum(b, bsz - 1), 0, 0)
    return seq_map, x_map, tok_map, slot_map, state_map


def _mixer0_prompt_kernel(x_ref, moda_ref, modb_ref, cnt_in_ref, h2cs_ref, w_in_ref, w_a2_ref,
                          b_a_ref, gng_ref,
                          cw_ref, cb_ref, clg_ref, clb_ref, w_out_ref, lng_ref, lnb_ref,
                          wr_hi_ref, wr_lo_ref, br_ref,
                          x1_ref, h2c_ref, route_ref, cnt_ref, nconv_ref, ngla_ref,
                          s_ref, cbuf_ref, shift_ref, carry_ref):
    nseq = pl.num_programs(0) - 1

    @pl.when(pl.program_id(0) < nseq)
    def _():
        _mixer0_prompt_tile(x_ref, moda_ref, modb_ref, cnt_in_ref, w_in_ref, w_a2_ref, b_a_ref,
                            gng_ref, cw_ref, cb_ref, clg_ref, clb_ref, w_out_ref, lng_ref, lnb_ref,
                            wr_hi_ref, wr_lo_ref, br_ref, x1_ref, h2c_ref, route_ref, cnt_ref,
                            nconv_ref, ngla_ref, s_ref, cbuf_ref, shift_ref, carry_ref)

    @pl.when((pl.program_id(0) == nseq) & (pl.program_id(1) == 0))
    def _():
        h2c_ref[...] = h2cs_ref[...]


def _mixer0_prompt_tile(x_ref, moda_ref, modb_ref, cnt_in_ref, w_in_ref, w_a2_ref, b_a_ref,
                        gng_ref, cw_ref, cb_ref, clg_ref, clb_ref, w_out_ref, lng_ref, lnb_ref,
                        wr_hi_ref, wr_lo_ref, br_ref, x1_ref, h2c_ref, route_ref, cnt_ref,
                        nconv_ref, ngla_ref, s_ref, cbuf_ref, shift_ref, carry_ref):
    t = pl.program_id(1)
    n = x_ref.shape[1]

    @pl.when((t == 0) & (pl.program_id(0) == 0))
    def _():
        carry_ref[...] = cnt_in_ref[...]

    @pl.when(t == 0)
    def _():
        s_ref[...] = jnp.zeros_like(s_ref)
        cbuf_ref[0:CONV_HALO, :] = jnp.zeros((CONV_HALO, CONV_CH), F32)

    x = x_ref[0]
    shift, scale, gate = _mod3(moda_ref[0])
    h = x * (1.0 + scale) + shift
    q, k, v, g, ua, ug, la = _mix0_inputs(h, w_in_ref, w_a2_ref, b_a_ref)
    o_heads, s_new = _gla_tile(q, k, v, la, s_ref[...])
    s_ref[...] = s_new

    glu = ua * _sigmoid(ug)
    cbuf_ref[CONV_HALO:CONV_HALO + n, :] = glu
    acc = jnp.broadcast_to(cb_ref[...], (n, CONV_CH))
    base = CONV_HALO - CONV_BUF
    for r in range(SUBLANES):
        taps = range(r, CONV_WIDTH, SUBLANES)
        span = n + (len(taps) - 1) * SUBLANES
        if (base + r) % SUBLANES == 0:
            src, off = cbuf_ref, base + r
        else:
            shift_ref[r, 0:span, :] = cbuf_ref[base + r:base + r + span, :]
            src, off = shift_ref.at[r], 0
        for a, j in enumerate(taps):
            lo = off + a * SUBLANES
            acc = acc + cw_ref[j:j + 1, :] * src[lo:lo + n, :]
    y = _silu(_layer_norm(acc, clg_ref[...], clb_ref[...]))

    @pl.when(t == pl.num_programs(1) - 1)
    def _():
        nconv_ref[0, 0] = cbuf_ref[CONV_HALO + n - CONV_BUF:CONV_HALO + n, :]
        ngla_ref[0, 0] = s_new.reshape(GLA_HEADS, GLA_DK, GLA_DV)

    cbuf_ref[0:CONV_HALO, :] = cbuf_ref[n:n + CONV_HALO, :]

    m = _mix0_project(o_heads, g, y, gng_ref, w_out_ref)
    carry = _post_mixer(x, m, gate, lng_ref, lnb_ref, modb_ref[0], wr_hi_ref, wr_lo_ref, br_ref,
                        carry_ref[...], x1_ref, h2c_ref, route_ref)
    carry_ref[...] = carry
    cnt_ref[...] = carry


def _mixer0_prompt_call(x, mod_a, mod_b, cnt, h2c_s, p):
    bsz, seq, _ = x.shape
    tl = TILE_L
    nt = seq // tl
    seq_map, x_map, tok, slot, state_map = _prompt_grid_specs(bsz, nt)
    return pl.pallas_call(
        _mixer0_prompt_kernel,
        grid=(bsz + 1, nt),
        in_specs=[
            pl.BlockSpec((1, tl, D_MODEL), x_map),
            pl.BlockSpec((1, 1, 3 * D_MODEL), seq_map),
            pl.BlockSpec((1, 1, 3 * D_MODEL), seq_map),
            _full((1, LANES)),
            _full((tl * CHUNKS, LANES)),
            _const((2, D_MODEL, Z_WIDTH)),
            _const((2, A_PAD, GLA_KW)),
            _full((1, GLA_KW)),
            _full((1, GLA_VW)),
            _full((CONV_W_ROWS, CONV_CH)),
            _full((1, CONV_CH)),
            _full((1, CONV_CH)),
            _full((1, CONV_CH)),
            _const((2, GLA_VW + CONV_CH, D_MODEL)),
            _full((1, D_MODEL)),
            _full((1, D_MODEL)),
            _full((D_MODEL, LANES)),
            _full((D_MODEL, LANES)),
            _full((1, LANES)),
        ],
        out_specs=_token_out_specs(tl, tl, tok, slot) + [
            _full((1, LANES)),
            pl.BlockSpec((1, 1, CONV_BUF, CONV_CH), state_map),
            pl.BlockSpec((1, 1, GLA_HEADS, GLA_DK, GLA_DV), lambda b, t: state_map(b, t) + (0,)),
        ],
        out_shape=_token_out_shapes(bsz * seq, bsz * seq + tl, bsz * seq) + [
            jax.ShapeDtypeStruct((1, LANES), F32),
            jax.ShapeDtypeStruct((1, bsz, CONV_BUF, CONV_CH), F32),
            jax.ShapeDtypeStruct((1, bsz, GLA_HEADS, GLA_DK, GLA_DV), F32),
        ],
        scratch_shapes=[
            pltpu.VMEM((GLA_KW, GLA_DV), F32),
            pltpu.VMEM((CONV_HALO + tl, CONV_CH), F32),
            pltpu.VMEM((SUBLANES, CONV_HALO + tl, CONV_CH), F32),
            pltpu.VMEM((1, LANES), F32),
        ],
        compiler_params=_params("arbitrary", "arbitrary"),
        name="mixer0_prompt",
    )(x, mod_a, mod_b, cnt, h2c_s, p["w_in"], p["w_a2"], p["b_a"], p["gng"], p["conv_w"], p["conv_b"],
      p["conv_ln_g"], p["conv_ln_b"], p["w_out0"], p["ln_g00"], p["ln_b00"],
      p["wr_hi0"], p["wr_lo0"], p["br0"])


def _mixer0_sample_kernel(x_ref, moda_ref, modb_ref, sgla_ref, sconv_ref, cnt_in_ref,
                          w_in_ref, w_a2_ref,
                          b_a_ref, gng_ref, cw_ref, cb_ref, clg_ref, clb_ref, w_out_ref, lng_ref,
                          lnb_ref, wr_hi_ref, wr_lo_ref, br_ref,
                          x1_ref, h2c_ref, route_ref, cnt_ref, ngla_ref, nconv_ref,
                          zt_ref, v_ref, g_ref, glu_ref, o_ref, y_ref):
    i = pl.program_id(0)
    nb = sgla_ref.shape[1]
    ntok = x_ref.shape[0]

    @pl.when(i == 0)
    def _():
        shift, scale, _ = _mod3(moda_ref[...])
        h = x_ref[...] * (1.0 + scale) + shift
        q, k, v, g, ua, ug, la = _mix0_inputs(h, w_in_ref, w_a2_ref, b_a_ref)
        v_ref[...] = v
        g_ref[...] = g
        glu_ref[...] = ua * _sigmoid(ug)
        for j, val in enumerate((jnp.exp(la), k, q)):
            hi, lo = _split2(jnp.transpose(val))
            zt_ref[(2 * j) * GLA_KW:(2 * j + 1) * GLA_KW, :] = hi
            zt_ref[(2 * j + 1) * GLA_KW:(2 * j + 2) * GLA_KW, :] = lo

    tok_row = lax.broadcasted_iota(I32, (ntok, LANES), 0)
    blk = pl.ds(pl.multiple_of(i * nb, nb), nb)
    v_blk = v_ref[blk, :]
    glu_blk = glu_ref[blk, :]
    o_rows, y_rows = [], []
    for n in range(nb):
        onehot = jnp.where(tok_row == i * nb + n, 1.0, 0.0).astype(BF16)
        cols = _dot(zt_ref[...], onehot)
        a_col = cols[0:GLA_KW] + cols[GLA_KW:2 * GLA_KW]
        k_col = cols[2 * GLA_KW:3 * GLA_KW] + cols[3 * GLA_KW:4 * GLA_KW]
        q_col = cols[4 * GLA_KW:5 * GLA_KW] + cols[5 * GLA_KW:6 * GLA_KW]
        v_row = v_blk[n:n + 1, :]
        v_b = jnp.concatenate(
            [jnp.broadcast_to(v_row[:, h * GLA_DV:(h + 1) * GLA_DV], (GLA_DK, GLA_DV))
             for h in range(GLA_HEADS)], axis=0)
        s_old = sgla_ref[0, n].reshape(GLA_KW, GLA_DV)
        s_new = a_col * s_old + k_col * v_b
        ngla_ref[0, n] = s_new.reshape(GLA_HEADS, GLA_DK, GLA_DV)
        o4 = jnp.sum((q_col * s_new).reshape(GLA_HEADS, GLA_DK, GLA_DV), axis=1)
        o_rows.append(jnp.concatenate([o4[h:h + 1, :] for h in range(GLA_HEADS)], axis=1))
        glu_row = glu_blk[n:n + 1, :]
        past = sconv_ref[0, n]
        y_rows.append(jnp.sum(past * cw_ref[0:CONV_BUF, :], axis=0, keepdims=True)
                      + glu_row * cw_ref[CONV_BUF:CONV_WIDTH, :] + cb_ref[...])
        nconv_ref[0, n, 0:CONV_BUF - 1, :] = sconv_ref[0, n, 1:CONV_BUF, :]
        nconv_ref[0, n, CONV_BUF - 1:CONV_BUF, :] = glu_row
    o_ref[blk, :] = jnp.concatenate(o_rows, axis=0)
    y_ref[blk, :] = jnp.concatenate(y_rows, axis=0)

    @pl.when(i == pl.num_programs(0) - 1)
    def _():
        y = _silu(_layer_norm(y_ref[...], clg_ref[...], clb_ref[...]))
        o_heads = [o_ref[:, hd * GLA_DV:(hd + 1) * GLA_DV] for hd in range(GLA_HEADS)]
        m = _mix0_project(o_heads, g_ref[...], y, gng_ref, w_out_ref)
        _, _, gate = _mod3(moda_ref[...])
        cnt_ref[...] = _post_mixer(x_ref[...], m, gate, lng_ref, lnb_ref, modb_ref[...], wr_hi_ref,
                                   wr_lo_ref, br_ref, cnt_in_ref[...], x1_ref, h2c_ref, route_ref)


def _mixer0_sample_call(x, mod_a, mod_b, state_gla, state_conv, cnt, p):
    ntok = x.shape[0]
    nb = SAMPLE_BLK
    tok = lambda i: (0, 0)
    return pl.pallas_call(
        _mixer0_sample_kernel,
        grid=(ntok // nb,),
        in_specs=[
            _full((ntok, D_MODEL)),
            _full((ntok, 3 * D_MODEL)),
            _full((ntok, 3 * D_MODEL)),
            pl.BlockSpec((1, nb, GLA_HEADS, GLA_DK, GLA_DV), lambda i: (0, i, 0, 0, 0)),
            pl.BlockSpec((1, nb, CONV_BUF, CONV_CH), lambda i: (0, i, 0, 0)),
            _full((1, LANES)),
            _const((2, D_MODEL, Z_WIDTH)),
            _const((2, A_PAD, GLA_KW)),
            _full((1, GLA_KW)),
            _full((1, GLA_VW)),
            _full((CONV_W_ROWS, CONV_CH)),
            _full((1, CONV_CH)),
            _full((1, CONV_CH)),
            _full((1, CONV_CH)),
            _const((2, GLA_VW + CONV_CH, D_MODEL)),
            _full((1, D_MODEL)),
            _full((1, D_MODEL)),
            _full((D_MODEL, LANES)),
            _full((D_MODEL, LANES)),
            _full((1, LANES)),
        ],
        out_specs=_token_out_specs(ntok, TILE_L, tok, tok) + [
            _full((1, LANES)),
            pl.BlockSpec((1, nb, GLA_HEADS, GLA_DK, GLA_DV), lambda i: (0, i, 0, 0, 0)),
            pl.BlockSpec((1, nb, CONV_BUF, CONV_CH), lambda i: (0, i, 0, 0)),
        ],
        out_shape=_token_out_shapes(ntok, TILE_L, TILE_L) + [
            jax.ShapeDtypeStruct((1, LANES), F32),
            jax.ShapeDtypeStruct(state_gla.shape, F32),
            jax.ShapeDtypeStruct(state_conv.shape, F32),
        ],
        scratch_shapes=[
            pltpu.VMEM((6 * GLA_KW, ntok), BF16),
            pltpu.VMEM((ntok, GLA_VW), F32),
            pltpu.VMEM((ntok, GLA_VW), F32),
            pltpu.VMEM((ntok, CONV_CH), F32),
            pltpu.VMEM((ntok, GLA_VW), F32),
            pltpu.VMEM((ntok, CONV_CH), F32),
        ],
        compiler_params=_params("arbitrary"),
        name="mixer0_sample",
    )(x, mod_a, mod_b, state_gla, state_conv, cnt, p["w_in"], p["w_a2"], p["b_a"],
      p["gng"], p["conv_w"], p["conv_b"], p["conv_ln_g"], p["conv_ln_b"], p["w_out0"],
      p["ln_g00"], p["ln_b00"], p["wr_hi0"], p["wr_lo0"], p["br0"])


def _pool_project(pooled, h, wg_ref, ps_ref, w_out_ref):
    m = None
    for gi in range(len(POOL_WINDOWS)):
        sl = slice(gi * POOL_GC, (gi + 1) * POOL_GC)
        mixed = _dot((pooled[:, sl] - h[:, sl]).astype(BF16), wg_ref[gi]) * ps_ref[:, sl]
        part = _dot(mixed.astype(BF16), w_out_ref[sl, :])
        m = part if m is None else m + part
    return m


def _block_input(xin_ref, m_ref, mod_m, lnpg_ref, lnpb_ref):
    m = _load_chunked(m_ref, xin_ref.shape[0], CHUNKS, 0)
    _, _, gate = _mod3(mod_m)
    return _layer_norm(ALPHA * xin_ref[...] + (1.0 + gate) * m, lnpg_ref[...], lnpb_ref[...])


def _mixer1_prompt_kernel(dest_ref, dnext_ref, xin_ref, y_any, modm_ref, lnpg_ref, lnpb_ref,
                          moda_ref, modb_ref, cnt_in_ref, h2cs_ref, wg_ref, ps_ref,
                          w_out_ref, lng_ref, lnb_ref, wr_hi_ref, wr_lo_ref, br_ref,
                          x1_ref, h2c_ref, route_ref, cnt_ref, npool_ref,
                          pbuf_ref, carry_ref, mbuf, sems):
    nseq = pl.num_programs(0) - 1
    nt = pl.num_programs(1)

    @pl.when(pl.program_id(0) < nseq)
    def _():
        m_ref, drain = _gather_pipelined(pl.program_id(0) * nt + pl.program_id(1), nseq * nt,
                                         dest_ref, dnext_ref, y_any, mbuf, sems)
        x = _block_input(xin_ref, m_ref, modm_ref[0], lnpg_ref, lnpb_ref)
        _mixer1_prompt_tile(x, moda_ref, modb_ref, cnt_in_ref, wg_ref, ps_ref, w_out_ref,
                            lng_ref, lnb_ref, wr_hi_ref, wr_lo_ref, br_ref, x1_ref, h2c_ref,
                            route_ref, cnt_ref, npool_ref, pbuf_ref, carry_ref)
        drain()

    @pl.when((pl.program_id(0) == nseq) & (pl.program_id(1) == 0))
    def _():
        h2c_ref[...] = h2cs_ref[...]


def _mixer1_prompt_tile(x, moda_ref, modb_ref, cnt_in_ref, wg_ref, ps_ref, w_out_ref,
                        lng_ref, lnb_ref, wr_hi_ref, wr_lo_ref, br_ref, x1_ref, h2c_ref,
                        route_ref, cnt_ref, npool_ref, pbuf_ref, carry_ref):
    t = pl.program_id(1)
    n = x.shape[0]

    @pl.when((t == 0) & (pl.program_id(0) == 0))
    def _():
        carry_ref[...] = cnt_in_ref[...]

    @pl.when(t == 0)
    def _():
        pbuf_ref[0:POOL_HALO, :] = jnp.zeros((POOL_HALO, D_MODEL), F32)

    shift, scale, gate = _mod3(moda_ref[0])
    h = x * (1.0 + scale) + shift
    pbuf_ref[POOL_HALO:POOL_HALO + n, :] = h

    cur = pbuf_ref[...]
    sums = []
    for gi, w in enumerate(POOL_WINDOWS):
        cur = cur + pltpu.roll(cur, w // 2, axis=0)
        sums.append(cur[POOL_HALO:, 0:POOL_GC])
        if gi + 1 < len(POOL_WINDOWS):
            cur = cur[:, POOL_GC:]
    pos = lax.broadcasted_iota(I32, (n, POOL_GC), 0) + t * n
    pooled = jnp.concatenate(
        [s / jnp.minimum(w, pos + 1).astype(F32) for s, w in zip(sums, POOL_WINDOWS)], axis=1)

    @pl.when(t == pl.num_programs(1) - 1)
    def _():
        npool_ref[0, 0] = pbuf_ref[POOL_HALO + n - POOL_BUF:POOL_HALO + n, :]

    pbuf_ref[0:POOL_HALO, :] = pbuf_ref[n:n + POOL_HALO, :]

    m = _pool_project(pooled, h, wg_ref, ps_ref, w_out_ref)
    carry = _post_mixer(x, m, gate, lng_ref, lnb_ref, modb_ref[0], wr_hi_ref, wr_lo_ref, br_ref,
                        carry_ref[...], x1_ref, h2c_ref, route_ref)
    carry_ref[...] = carry
    cnt_ref[...] = carry


def _mixer1_prompt_call(x1_prev, dest, y_sorted, mod_m, lnpg, lnpb, mod_a, mod_b, cnt, h2c_s, bsz, p):
    seq = x1_prev.shape[0] // bsz
    tl = TILE_L
    nt = seq // tl
    ntile = bsz * nt
    seq_map, _, tok, slot, state_map = _prompt_grid_specs(bsz, nt)
    ng = len(POOL_WINDOWS)
    return pl.pallas_call(
        _mixer1_prompt_kernel,
        grid=(bsz + 1, nt),
        in_specs=[
            pl.BlockSpec((1, 1, tl), lambda b, t: tok(b, t) + (0,), memory_space=pltpu.SMEM),
            pl.BlockSpec((1, 1, tl), lambda b, t: (jnp.minimum(b * nt + t + 1, ntile - 1), 0, 0),
                         memory_space=pltpu.SMEM),
            pl.BlockSpec((tl, D_MODEL), tok),
            pl.BlockSpec(memory_space=pl.ANY),
            pl.BlockSpec((1, 1, 3 * D_MODEL), seq_map),
            _full((1, D_MODEL)),
            _full((1, D_MODEL)),
            pl.BlockSpec((1, 1, 3 * D_MODEL), seq_map),
            pl.BlockSpec((1, 1, 3 * D_MODEL), seq_map),
            _full((1, LANES)),
            _full((tl * CHUNKS, LANES)),
            _full((ng, POOL_GC, POOL_GC)),
            _full((1, D_MODEL)),
            _full((D_MODEL, D_MODEL)),
            _full((1, D_MODEL)),
            _full((1, D_MODEL)),
            _full((D_MODEL, LANES)),
            _full((D_MODEL, LANES)),
            _full((1, LANES)),
        ],
        out_specs=_token_out_specs(tl, tl, tok, slot) + [
            _full((1, LANES)),
            pl.BlockSpec((1, 1, POOL_BUF, D_MODEL), state_map),
        ],
        out_shape=_token_out_shapes(bsz * seq, bsz * seq + tl, bsz * seq) + [
            jax.ShapeDtypeStruct((1, LANES), F32),
            jax.ShapeDtypeStruct((1, bsz, POOL_BUF, D_MODEL), F32),
        ],
        scratch_shapes=[pltpu.VMEM((POOL_HALO + tl, D_MODEL), F32), pltpu.VMEM((1, LANES), F32),
                        pltpu.VMEM((2, tl * CHUNKS, LANES), F32), pltpu.SemaphoreType.DMA((2,))],
        compiler_params=_params("arbitrary", "arbitrary"),
        name="mixer1_prompt",
    )(dest.reshape(ntile, 1, tl), dest.reshape(ntile, 1, tl), x1_prev, y_sorted, mod_m, lnpg, lnpb,
      mod_a, mod_b, cnt, h2c_s,
      p["w_grp"], p["pool_scale"], p["w_out1"], p["ln_g10"], p["ln_b10"],
      p["wr_hi1"], p["wr_lo1"], p["br1"])


def _mixer1_sample_kernel(dest_ref, xin_ref, y_any, modm_ref, lnpg_ref, lnpb_ref, moda_ref,
                          modb_ref, spool_ref, cnt_in_ref, wg_ref, ps_ref, w_out_ref,
                          lng_ref, lnb_ref, wr_hi_ref, wr_lo_ref, br_ref,
                          x1_ref, h2c_ref, route_ref, cnt_ref, npool_ref,
                          x_ref, h_ref, pooled_ref, mbuf, sem):
    i = pl.program_id(0)
    nb = spool_ref.shape[1]

    @pl.when(i == 0)
    def _():
        _start_row_gather(dest_ref, y_any, mbuf, sem)
        _wait_row_gather(y_any, mbuf, sem)
        x_ref[...] = _block_input(xin_ref, mbuf, modm_ref[...], lnpg_ref, lnpb_ref)
        shift, scale, _ = _mod3(moda_ref[...])
        h_ref[...] = x_ref[...] * (1.0 + scale) + shift

    lane = lax.broadcasted_iota(I32, (POOL_BUF, D_MODEL), 1)
    rowi = lax.broadcasted_iota(I32, (POOL_BUF, D_MODEL), 0)
    first = jnp.zeros((POOL_BUF, D_MODEL), I32)
    lane1 = lax.broadcasted_iota(I32, (1, D_MODEL), 1)
    inv_w = jnp.zeros((1, D_MODEL), F32)
    for gi, w in enumerate(POOL_WINDOWS):
        in_g = (lane >= gi * POOL_GC) & (lane < (gi + 1) * POOL_GC)
        first = jnp.where(in_g, POOL_BUF - (w - 1), first)
        in_g1 = (lane1 >= gi * POOL_GC) & (lane1 < (gi + 1) * POOL_GC)
        inv_w = jnp.where(in_g1, 1.0 / w, inv_w)
    keep = rowi >= first

    blk = pl.ds(pl.multiple_of(i * nb, nb), nb)
    h_blk = h_ref[blk, :]
    rows = []
    for n in range(nb):
        h_row = h_blk[n:n + 1, :]
        past = spool_ref[0, n]
        total = jnp.sum(jnp.where(keep, past, 0.0), axis=0, keepdims=True) + h_row
        rows.append(total * inv_w)
        npool_ref[0, n, 0:POOL_BUF - 1, :] = spool_ref[0, n, 1:POOL_BUF, :]
        npool_ref[0, n, POOL_BUF - 1:POOL_BUF, :] = h_row
    pooled_ref[blk, :] = jnp.concatenate(rows, axis=0)

    @pl.when(i == pl.num_programs(0) - 1)
    def _():
        m = _pool_project(pooled_ref[...], h_ref[...], wg_ref, ps_ref, w_out_ref)
        _, _, gate = _mod3(moda_ref[...])
        cnt_ref[...] = _post_mixer(x_ref[...], m, gate, lng_ref, lnb_ref, modb_ref[...], wr_hi_ref,
                                   wr_lo_ref, br_ref, cnt_in_ref[...], x1_ref, h2c_ref, route_ref)


def _mixer1_sample_call(x1_prev, dest, y_sorted, mod_m, lnpg, lnpb, mod_a, mod_b, state_pool,
                        cnt, p):
    ntok = x1_prev.shape[0]
    nb = SAMPLE_BLK
    ng = len(POOL_WINDOWS)
    tok = lambda i: (0, 0)
    return pl.pallas_call(
        _mixer1_sample_kernel,
        grid=(ntok // nb,),
        in_specs=[
            pl.BlockSpec((1, 1, ntok), lambda i: (0, 0, 0), memory_space=pltpu.SMEM),
            _full((ntok, D_MODEL)),
            pl.BlockSpec(memory_space=pl.ANY),
            _full((ntok, 3 * D_MODEL)),
            _full((1, D_MODEL)),
            _full((1, D_MODEL)),
            _full((ntok, 3 * D_MODEL)),
            _full((ntok, 3 * D_MODEL)),
            pl.BlockSpec((1, nb, POOL_BUF, D_MODEL), lambda i: (0, i, 0, 0)),
            _full((1, LANES)),
            _full((ng, POOL_GC, POOL_GC)),
            _full((1, D_MODEL)),
            _full((D_MODEL, D_MODEL)),
            _full((1, D_MODEL)),
            _full((1, D_MODEL)),
            _full((D_MODEL, LANES)),
            _full((D_MODEL, LANES)),
            _full((1, LANES)),
        ],
        out_specs=_token_out_specs(ntok, TILE_L, tok, tok) + [
            _full((1, LANES)),
            pl.BlockSpec((1, nb, POOL_BUF, D_MODEL), lambda i: (0, i, 0, 0)),
        ],
        out_shape=_token_out_shapes(ntok, TILE_L, TILE_L) + [
            jax.ShapeDtypeStruct((1, LANES), F32),
            jax.ShapeDtypeStruct(state_pool.shape, F32),
        ],
        scratch_shapes=[pltpu.VMEM((ntok, D_MODEL), F32)] * 3 + [
            pltpu.VMEM((ntok * CHUNKS, LANES), F32), pltpu.SemaphoreType.DMA],
        compiler_params=_params("arbitrary"),
        name="mixer1_sample",
    )(dest.reshape(1, 1, ntok), x1_prev, y_sorted, mod_m, lnpg, lnpb, mod_a, mod_b, state_pool, cnt,
      p["w_grp"], p["pool_scale"], p["w_out1"],
      p["ln_g10"], p["ln_b10"], p["wr_hi1"], p["wr_lo1"], p["br1"])


def _dispatch_kernel(sched_ref, dest_ref, src_ref, dst_any, zeros_ref, sem, zsem):
    n = dest_ref.shape[2]
    tile_rows = zeros_ref.shape[0]

    @pl.when(pl.program_id(0) == 0)
    def _():
        zeros_ref[...] = jnp.zeros_like(zeros_ref)

        def fill(i):
            return pltpu.make_async_copy(
                zeros_ref, dst_any.at[pl.ds(i * tile_rows, tile_rows)], zsem)

        for i in range(dst_any.shape[0] // tile_rows):
            pl.when(sched_ref[S_ZERO, i] == 1)(lambda i=i: fill(i).start())
        for i in range(dst_any.shape[0] // tile_rows):
            pl.when(sched_ref[S_ZERO, i] == 1)(lambda i=i: fill(i).wait())

    for j in range(n):
        d = dest_ref[0, 0, j]
        pltpu.make_async_copy(
            src_ref.at[pl.ds(j * CHUNKS, CHUNKS)],
            dst_any.at[pl.ds(pl.multiple_of(d * CHUNKS, CHUNKS), CHUNKS)], sem).start()
    pltpu.make_async_copy(src_ref, dst_any.at[pl.ds(0, n * CHUNKS)], sem).wait()


def _dispatch_call(sched, dest, h2c_all):
    nslot = dest.shape[1]
    rows = DISPATCH_ROWS
    tile_rows = MOE_TILE * CHUNKS
    grid_spec = pltpu.PrefetchScalarGridSpec(
        num_scalar_prefetch=1,
        grid=(nslot // rows,),
        in_specs=[
            pl.BlockSpec((1, 1, rows), lambda i, s: (i, 0, 0), memory_space=pltpu.SMEM),
            pl.BlockSpec((rows * CHUNKS, LANES), lambda i, s: (i, 0)),
        ],
        out_specs=pl.BlockSpec(memory_space=pl.ANY),
        scratch_shapes=[pltpu.VMEM((tile_rows, LANES), F32), pltpu.SemaphoreType.DMA,
                        pltpu.SemaphoreType.DMA],
    )
    return pl.pallas_call(
        _dispatch_kernel,
        grid_spec=grid_spec,
        out_shape=jax.ShapeDtypeStruct((_num_tiles(nslot) * tile_rows, LANES), F32),
        compiler_params=_params("arbitrary"),
        name="dispatch",
    )(sched, dest.reshape(nslot // rows, 1, rows), h2c_all)


(S_BLK, S_EA, S_EB, S_VALID, S_FIRST, S_LANE_G, S_LANE_A, S_LANE_B, S_ZERO) = range(9)
SCHED_ROWS = 2 * SUBLANES


def _moe_kernel(sched_ref, x_ref, wr_hi_ref, wr_lo_ref, br_ref,
                wga_ref, wua_ref, wda_ref, wgb_ref, wub_ref, wdb_ref,
                y_ref, wgu_ref, wd_ref):
    i = pl.program_id(0)
    tm = x_ref.shape[0] // CHUNKS

    @pl.when(sched_ref[S_FIRST, i] == 1)
    def _():
        for e, (wg, wu, wd) in enumerate(((wga_ref, wua_ref, wda_ref), (wgb_ref, wub_ref, wdb_ref))):
            wgu_ref[2 * e] = wg[0].astype(BF16)
            wgu_ref[2 * e + 1] = wu[0].astype(BF16)
            wd_ref[e * MOE_FF:(e + 1) * MOE_FF, :] = wd[0].astype(BF16)

    @pl.when(sched_ref[S_VALID, i] == 0)
    def _():
        y_ref[...] = jnp.zeros_like(y_ref)

    @pl.when(sched_ref[S_VALID, i] == 1)
    def _():
        x = _load_chunked(x_ref, tm, CHUNKS, 0)
        logits = _router_logits(x, wr_hi_ref, wr_lo_ref, br_ref)
        lane = lax.broadcasted_iota(I32, logits.shape, 1)

        def pick(row):
            return jnp.sum(jnp.where(lane == sched_ref[row, i], logits, 0.0), axis=-1, keepdims=True)

        l_g, l_a, l_b = pick(S_LANE_G), pick(S_LANE_A), pick(S_LANE_B)
        p_g = 1.0 / jnp.sum(jnp.where(lane < MOE_GROUPS, jnp.exp(logits - l_g), 0.0),
                            axis=-1, keepdims=True)
        w_ab = (p_g / (1.0 + jnp.exp(l_b - l_a)), p_g / (1.0 + jnp.exp(l_a - l_b)))
        xb = x.astype(BF16)
        hid = [(_silu(_dot(xb, wgu_ref[2 * e])) * _dot(xb, wgu_ref[2 * e + 1]) * w_ab[e]).astype(BF16)
               for e in range(2)]
        _store_chunked(y_ref, _dot(jnp.concatenate(hid, axis=1), wd_ref[...]))


def _moe_call(sched, sorted_x, wr_hi, wr_lo, br, wg, wu, wd):
    tm = MOE_TILE
    ntile = sorted_x.shape[0] // (tm * CHUNKS)
    ea = lambda i, s: (s[S_EA, i], 0, 0)
    eb = lambda i, s: (s[S_EB, i], 0, 0)
    const2 = lambda i, s: (0, 0)
    grid_spec = pltpu.PrefetchScalarGridSpec(
        num_scalar_prefetch=1,
        grid=(ntile,),
        in_specs=[
            pl.BlockSpec((tm * CHUNKS, LANES), lambda i, s: (s[S_BLK, i], 0)),
            pl.BlockSpec((D_MODEL, LANES), const2),
            pl.BlockSpec((D_MODEL, LANES), const2),
            pl.BlockSpec((1, LANES), const2),
            pl.BlockSpec((1, D_MODEL, MOE_FF), ea),
            pl.BlockSpec((1, D_MODEL, MOE_FF), ea),
            pl.BlockSpec((1, MOE_FF, D_MODEL), ea),
            pl.BlockSpec((1, D_MODEL, MOE_FF), eb),
            pl.BlockSpec((1, D_MODEL, MOE_FF), eb),
            pl.BlockSpec((1, MOE_FF, D_MODEL), eb),
        ],
        out_specs=pl.BlockSpec((tm * CHUNKS, LANES), lambda i, s: (i, 0)),
        scratch_shapes=[
            pltpu.VMEM((4, D_MODEL, MOE_FF), BF16),
            pltpu.VMEM((2 * MOE_FF, D_MODEL), BF16),
        ],
    )
    return pl.pallas_call(
        _moe_kernel,
        grid_spec=grid_spec,
        out_shape=jax.ShapeDtypeStruct((ntile * tm * CHUNKS, LANES), F32),
        compiler_params=_params("arbitrary"),
        name="moe",
    )(sched, sorted_x, wr_hi, wr_lo, br, wg, wu, wd, wg, wu, wd)


def _start_row_gather(dest_ref, y_any, buf, sem):
    for j in range(dest_ref.shape[2]):
        d = dest_ref[0, 0, j]
        pltpu.make_async_copy(
            y_any.at[pl.ds(pl.multiple_of(d * CHUNKS, CHUNKS), CHUNKS)],
            buf.at[pl.ds(j * CHUNKS, CHUNKS)], sem).start()


def _wait_row_gather(y_any, buf, sem):
    pltpu.make_async_copy(y_any.at[pl.ds(0, buf.shape[0])], buf, sem).wait()


def _gather_pipelined(step, nstep, dest_ref, dnext_ref, y_any, mbuf, sems):
    slot = step % 2

    @pl.when(step == 0)
    def _():
        _start_row_gather(dest_ref, y_any, mbuf.at[0], sems.at[0])

    _start_row_gather(dnext_ref, y_any, mbuf.at[1 - slot], sems.at[1 - slot])
    _wait_row_gather(y_any, mbuf.at[slot], sems.at[slot])

    def drain():
        @pl.when(step == nstep - 1)
        def _():
            _wait_row_gather(y_any, mbuf.at[1 - slot], sems.at[1 - slot])

    return mbuf.at[slot], drain


def _post_moe_kernel(dest_ref, dnext_ref, x1_ref, y_any, mod_ref, lng_ref, lnb_ref, out_ref,
                     mbuf, sems):
    m_ref, drain = _gather_pipelined(pl.program_id(0), pl.num_programs(0), dest_ref, dnext_ref,
                                     y_any, mbuf, sems)
    out_ref[...] = _block_input(x1_ref, m_ref, mod_ref[0], lng_ref, lnb_ref)
    drain()


def _post_moe_call(x1, dest, y_sorted, mod, lng, lnb, rows, steps_per_mod):
    mrows = mod.shape[1]
    nstep = x1.shape[0] // rows
    dest3 = dest.reshape(nstep, 1, rows)
    return pl.pallas_call(
        _post_moe_kernel,
        grid=(nstep,),
        in_specs=[
            pl.BlockSpec((1, 1, rows), lambda i: (i, 0, 0), memory_space=pltpu.SMEM),
            pl.BlockSpec((1, 1, rows), lambda i: (jnp.minimum(i + 1, nstep - 1), 0, 0),
                         memory_space=pltpu.SMEM),
            pl.BlockSpec((rows, D_MODEL), lambda i: (i, 0)),
            pl.BlockSpec(memory_space=pl.ANY),
            pl.BlockSpec((1, mrows, 3 * D_MODEL), lambda i: (i // steps_per_mod, 0, 0)),
            _full((1, D_MODEL)),
            _full((1, D_MODEL)),
        ],
        out_specs=pl.BlockSpec((rows, D_MODEL), lambda i: (i, 0)),
        out_shape=jax.ShapeDtypeStruct(x1.shape, F32),
        scratch_shapes=[pltpu.VMEM((2, rows * CHUNKS, LANES), F32), pltpu.SemaphoreType.DMA((2,))],
        compiler_params=_params("arbitrary"),
        name="post_moe",
    )(dest3, dest3, x1, y_sorted, mod, lng, lnb)


PLAN_CHUNK = 2048


def _num_tiles(nslot):
    return -(-nslot // MOE_TILE) + N_BUCKETS


def _plan_kernel(rp_ref, rs_ref, cnt_ref, dest_ref, sched_ref, *, layer, ntile):
    tm = MOE_TILE
    lane = lax.broadcasted_iota(I32, (SUBLANES, LANES), 1)
    row = lax.broadcasted_iota(I32, (LANES, LANES), 0)
    col = lax.broadcasted_iota(I32, (LANES, LANES), 1)
    counts = jnp.where(lane < N_BUCKETS, jnp.broadcast_to(cnt_ref[...], (SUBLANES, LANES)), 0.0)
    tiles_b = jnp.floor((counts + float(tm - 1)) * (1.0 / tm))
    tile_end = _dot(tiles_b.astype(BF16), jnp.where(row <= col, 1.0, 0.0).astype(BF16))
    tile_start = tile_end - tiles_b
    start_b = tile_start.astype(BF16)

    def dest_of(route):
        n = route.shape[1]
        bucket, rank = route[R_BUCKET:R_BUCKET + 1, :], route[R_RANK:R_RANK + 1, :]
        b_iota = lax.broadcasted_iota(I32, (LANES, n), 0).astype(F32)
        onehot = jnp.where(b_iota == bucket, 1.0, 0.0).astype(BF16)
        start = _dot(start_b, onehot)[0:1, :]
        return (start * float(tm) + rank).astype(I32)

    ntok_p = rp_ref.shape[1]
    chunk = math.gcd(ntok_p, PLAN_CHUNK)
    for c0 in range(0, ntok_p, chunk):
        dest_ref[:, c0:c0 + chunk] = dest_of(rp_ref[:, c0:c0 + chunk])
    dest_ref[:, ntok_p:] = dest_of(rs_ref[...])

    tile = lane.astype(F32)
    used = jnp.sum(jnp.where(lane == N_BUCKETS - 1, tile_end, 0.0), axis=-1, keepdims=True)
    ti = jnp.minimum(tile, used - 1.0)
    end_col = jnp.transpose(jnp.broadcast_to(tile_end[0:1], (LANES, LANES)))
    start_col = jnp.transpose(jnp.broadcast_to(tile_start[0:1], (LANES, LANES)))
    ti_rows = jnp.broadcast_to(ti[0:1], (LANES, LANES))
    b_of = jnp.sum(jnp.where((row < N_BUCKETS) & (ti_rows >= end_col), 1.0, 0.0),
                   axis=0, keepdims=True)
    start_of = jnp.sum(jnp.where(row.astype(F32) == b_of, start_col, 0.0), axis=0, keepdims=True)
    grp = sum(jnp.where(b_of >= float(g * N_PAIRS), 1.0, 0.0) for g in range(1, MOE_GROUPS))
    pair = b_of - float(N_PAIRS) * grp
    ex_a = sum(jnp.where(pair == float(j), float(PAIR_A[j]), 0.0) for j in range(N_PAIRS))
    ex_b = sum(jnp.where(pair == float(j), float(PAIR_B[j]), 0.0) for j in range(N_PAIRS))
    first_expert = float(layer * N_EXPERTS) + float(MOE_EXPERTS) * grp
    first_lane = float(MOE_GROUPS) + float(MOE_EXPERTS) * grp
    end_of = jnp.sum(jnp.where(row.astype(F32) == b_of, end_col, 0.0), axis=0, keepdims=True)
    valid = jnp.where(tile[0:1] < used[0:1], 1.0, 0.0)
    first = valid * jnp.where(ti[0:1] == start_of, 1.0, 0.0)
    partial = jnp.where((tile[0:1] == end_of - 1.0) | (tile[0:1] >= used[0:1]), 1.0, 0.0)
    zero = jnp.where(tile[0:1] < float(ntile), partial, 0.0)
    rows = {S_BLK: ti[0:1], S_EA: first_expert + ex_a, S_EB: first_expert + ex_b, S_VALID: valid,
            S_FIRST: first, S_LANE_G: grp, S_LANE_A: first_lane + ex_a, S_LANE_B: first_lane + ex_b,
            S_ZERO: zero}
    sub = lax.broadcasted_iota(I32, (SCHED_ROWS, LANES), 0)
    sched = sum(jnp.where(sub == r, jnp.broadcast_to(v, (SCHED_ROWS, LANES)), 0.0)
                for r, v in rows.items())
    sched_ref[...] = sched.astype(I32)


def _moe_plan(route_p, route_s, cnt, layer):
    nslot = route_p.shape[1] + route_s.shape[1]
    assert _num_tiles(nslot) <= LANES
    return pl.pallas_call(
        functools.partial(_plan_kernel, layer=layer, ntile=_num_tiles(nslot)),
        out_shape=[jax.ShapeDtypeStruct((1, nslot), I32),
                   jax.ShapeDtypeStruct((SCHED_ROWS, LANES), I32)],
        compiler_params=pltpu.CompilerParams(vmem_limit_bytes=VMEM_LIMIT),
        name="plan",
    )(route_p, route_s, cnt)


def _router_weights(w_coarse, b_coarse, w_fine, b_fine):
    wf = jnp.transpose(w_fine, (1, 0, 2)).reshape(D_MODEL, N_EXPERTS)
    w = jnp.concatenate([w_coarse, wf], axis=1)
    w = jnp.pad(w, ((0, 0), (0, LANES - w.shape[1])))
    b = jnp.concatenate([b_coarse, b_fine.reshape(N_EXPERTS)])
    b = jnp.pad(b, (0, LANES - b.shape[0])).reshape(1, LANES)
    hi_lo = _hi_lo(w)
    return hi_lo[0], hi_lo[1], b


def _prep_params(ln_g, ln_b, w_in_even, w_a2, b_a, gla_norm_g, conv_w, conv_b, conv_ln_g,
                 conv_ln_b, w_out_even, w_grp_pool, pool_scale, w_out_odd, w_coarse, b_coarse,
                 w_fine, b_fine):
    p = {}
    w_in = w_in_even[0]
    o_q, o_k, o_v, o_g = 0, GLA_KW, 2 * GLA_KW, 2 * GLA_KW + GLA_VW
    o_a = o_g + GLA_VW
    o_u = o_a + GLA_RANK
    w_in_r = jnp.concatenate(
        [w_in[:, o_q:o_a], w_in[:, o_u:o_u + 2 * CONV_CH], w_in[:, o_a:o_u],
         jnp.zeros((D_MODEL, A_PAD - GLA_RANK), F32)], axis=1)
    p["w_in"] = _hi_lo(w_in_r)
    p["w_a2"] = _hi_lo(jnp.pad(w_a2[0], ((0, A_PAD - GLA_RANK), (0, 0))))
    p["b_a"] = b_a[0].reshape(1, GLA_KW)
    p["gng"] = gla_norm_g[0].reshape(1, GLA_VW)
    p["conv_w"] = jnp.pad(conv_w[0], ((0, CONV_W_ROWS - CONV_WIDTH), (0, 0)))
    p["conv_b"] = conv_b[0].reshape(1, CONV_CH)
    p["conv_ln_g"] = conv_ln_g[0].reshape(1, CONV_CH)
    p["conv_ln_b"] = conv_ln_b[0].reshape(1, CONV_CH)
    p["w_out0"] = _hi_lo(w_out_even[0])
    p["w_grp"] = w_grp_pool[0].astype(BF16)
    p["pool_scale"] = pool_scale[0].reshape(1, D_MODEL)
    p["w_out1"] = w_out_odd[0].astype(BF16)
    for layer in range(DEPTH):
        for j in range(2):
            p[f"ln_g{layer}{j}"] = ln_g[layer, j].reshape(1, D_MODEL)
            p[f"ln_b{layer}{j}"] = ln_b[layer, j].reshape(1, D_MODEL)
        hi, lo, b = _router_weights(w_coarse[layer], b_coarse[layer], w_fine[layer], b_fine[layer])
        p[f"wr_hi{layer}"], p[f"wr_lo{layer}"], p[f"br{layer}"] = hi, lo, b
    return p


def kernel(x_prompt, x_sample, state_gla, state_conv, state_pool, c_prompt, c_sample, w_ada, b_ada,
           ln_g, ln_b, w_in_even, w_a2, b_a, gla_norm_g, conv_w, conv_b, conv_ln_g, conv_ln_b,
           w_out_even, w_grp_pool, pool_scale, w_out_odd, w_coarse, b_coarse, w_fine, b_fine,
           w_gate, w_up, w_down):
    bsz, seq, _ = x_prompt.shape
    nsmp = x_sample.shape[0]
    ntok_p = bsz * seq
    assert seq % TILE_L == 0 and nsmp <= TILE_L and nsmp % SAMPLE_BLK == 0 and ntok_p % nsmp == 0
    p = _prep_params(ln_g, ln_b, w_in_even, w_a2, b_a, gla_norm_g, conv_w, conv_b, conv_ln_g,
                     conv_ln_b, w_out_even, w_grp_pool, pool_scale, w_out_odd, w_coarse, b_coarse,
                     w_fine, b_fine)
    wg = w_gate.reshape(DEPTH * N_EXPERTS, D_MODEL, MOE_FF)
    wu = w_up.reshape(DEPTH * N_EXPERTS, D_MODEL, MOE_FF)
    wd = w_down.reshape(DEPTH * N_EXPERTS, MOE_FF, D_MODEL)

    c_all = jnp.concatenate([c_prompt, c_sample], axis=0)
    mod = _ada_call(c_all, w_ada.reshape(2 * DEPTH, D_MODEL, 3 * D_MODEL),
                    b_ada.reshape(2 * DEPTH, 1, 3 * D_MODEL))
    mod_p = mod[:, :bsz].reshape(2 * DEPTH, bsz, 1, 3 * D_MODEL)
    mod_s = mod[:, bsz:]

    def moe(h2c_all, route_p, route_s, cnt, layer):
        dest, sched = _moe_plan(route_p, route_s, cnt, layer)
        sorted_x = _dispatch_call(sched, dest, h2c_all)
        y_sorted = _moe_call(sched, sorted_x, p[f"wr_hi{layer}"], p[f"wr_lo{layer}"],
                             p[f"br{layer}"], wg, wu, wd)
        return dest[0, :ntok_p], dest[0, ntok_p:ntok_p + nsmp], y_sorted

    xs0 = x_sample.reshape(nsmp, D_MODEL)
    cnt0 = jnp.zeros((1, LANES), F32)
    x1s, h2c_s, route_s, cnt, gla_s, conv_s = _mixer0_sample_call(
        xs0, mod_s[0], mod_s[1], state_gla, state_conv, cnt0, p)
    x1p, h2c_all, route_p, cnt, conv_p, gla_p = _mixer0_prompt_call(
        x_prompt, mod_p[0], mod_p[1], cnt, h2c_s, p)
    dest_p, dest_s, y_sorted = moe(h2c_all, route_p, route_s, cnt, 0)
    x3s, h2c_s, route_s, cnt, pool_s = _mixer1_sample_call(
        x1s, dest_s, y_sorted, mod_s[1], p["ln_g01"], p["ln_b01"], mod_s[2], mod_s[3],
        state_pool, cnt0, p)
    x3p, h2c_all, route_p, cnt, pool_p = _mixer1_prompt_call(
        x1p, dest_p, y_sorted, mod_p[1], p["ln_g01"], p["ln_b01"], mod_p[2], mod_p[3], cnt,
        h2c_s, bsz, p)
    dest_p, dest_s, y_sorted = moe(h2c_all, route_p, route_s, cnt, 1)
    x4p = _post_moe_call(x3p, dest_p, y_sorted, mod_p[3], p["ln_g11"], p["ln_b11"], TILE_L,
                         seq // TILE_L).reshape(bsz, seq, D_MODEL)
    x4s = _post_moe_call(x3s, dest_s, y_sorted, mod_s[3][None], p["ln_g11"], p["ln_b11"], nsmp, 1)
    return (x4p, x4s.reshape(nsmp, 1, D_MODEL), gla_p, conv_p, pool_p, gla_s, conv_s, pool_s)
```

```python
import functools
import math

import jax
import jax.numpy as jnp
from jax import lax
from jax.experimental import pallas as pl
from jax.experimental.pallas import tpu as pltpu

F32 = jnp.float32
BF16 = jnp.bfloat16
I32 = jnp.int32

D_MODEL = 1024
GLA_HEADS = 4
GLA_DK = 64
GLA_DV = 128
GLA_KW = GLA_HEADS * GLA_DK
GLA_VW = GLA_HEADS * GLA_DV
GLA_RANK = 16
GLA_TAU = 16.0
CONV_CH = 512
CONV_WIDTH = 31
CONV_BUF = CONV_WIDTH - 1
POOL_WINDOWS = (2, 4, 8, 16)
POOL_GC = D_MODEL // len(POOL_WINDOWS)
POOL_BUF = max(POOL_WINDOWS) - 1
MOE_GROUPS = 4
MOE_EXPERTS = 4
N_EXPERTS = MOE_GROUPS * MOE_EXPERTS
N_PAIRS = MOE_EXPERTS * (MOE_EXPERTS - 1) // 2
N_BUCKETS = MOE_GROUPS * N_PAIRS
PAIR_A = (0, 0, 0, 1, 1, 2)
PAIR_B = (1, 2, 3, 2, 3, 3)
MOE_FF = 512
R_BUCKET, R_RANK = 0, 1
DEPTH = 2
ALPHA = (2 * DEPTH) ** 0.25
LN_EPS = 1e-5

LANES = 128
SUBLANES = 8
CHUNKS = D_MODEL // LANES
A_PAD = LANES
Z_WIDTH = 2 * GLA_KW + 2 * GLA_VW + 2 * CONV_CH + A_PAD
NEG_BIG = -1e30
VMEM_LIMIT = 56 * 1024 * 1024

TILE_L = 256
CONV_HALO = 32
CONV_W_ROWS = 32
POOL_HALO = 16
SAMPLE_BLK = 16
MOE_TILE = 256
DISPATCH_ROWS = 640


def _dot(a, b):
    return jnp.dot(a, b, preferred_element_type=F32)


def _dot_nt(a, b):
    return lax.dot_general(a, b, (((1,), (1,)), ((), ())), preferred_element_type=F32)


def _split3(x):
    hi = x.astype(BF16)
    r1 = x - hi.astype(F32)
    mid = r1.astype(BF16)
    lo = (r1 - mid.astype(F32)).astype(BF16)
    return hi, mid, lo


def _split2(x):
    hi = x.astype(BF16)
    lo = (x - hi.astype(F32)).astype(BF16)
    return hi, lo


def _dot_w3(a, w_hi, w_lo):
    a_hi, a_lo = _split2(a)
    return _dot(a_hi, w_hi) + (_dot(a_lo, w_hi) + _dot(a_hi, w_lo))


def _lhs3(parts):
    hi, lo = parts
    return jnp.concatenate([hi, lo, hi], axis=1)


def _rhs3_rows(x):
    hi, lo = _split2(x)
    return jnp.concatenate([hi, hi, lo], axis=0)


def _rhs3_lanes(parts):
    hi, lo = parts
    return jnp.concatenate([hi, hi, lo], axis=1)


def _layer_norm(x, g, b):
    mu = jnp.mean(x, axis=-1, keepdims=True)
    xc = x - mu
    var = jnp.mean(xc * xc, axis=-1, keepdims=True)
    return xc * lax.rsqrt(var + LN_EPS) * g + b


def _standardize(x):
    mu = jnp.mean(x, axis=-1, keepdims=True)
    xc = x - mu
    var = jnp.mean(xc * xc, axis=-1, keepdims=True)
    return xc * lax.rsqrt(var + LN_EPS)


def _sigmoid(x):
    return 1.0 / (1.0 + jnp.exp(-x))


def _silu(x):
    return x * _sigmoid(x)


def _log_sigmoid(x):
    return jnp.minimum(x, 0.0) - jnp.log(1.0 + jnp.exp(-jnp.abs(x)))


def _mod3(mod):
    return mod[:, 0:D_MODEL], mod[:, D_MODEL:2 * D_MODEL], mod[:, 2 * D_MODEL:3 * D_MODEL]


def _full(shape):
    nd = len(shape)
    return pl.BlockSpec(shape, lambda *_: (0,) * nd)


def _const(shape):
    nd = len(shape)
    return pl.BlockSpec(shape, lambda *_: (0,) * nd, pipeline_mode=pl.Buffered(1))


def _hi_lo(w):
    bits = lax.bitcast_convert_type(w, jnp.uint32) & jnp.uint32(0xFFFF0000)
    hi = lax.bitcast_convert_type(bits, F32)
    return jnp.stack([hi.astype(BF16), (w - hi).astype(BF16)])


def _params(*sem):
    return pltpu.CompilerParams(dimension_semantics=sem, vmem_limit_bytes=VMEM_LIMIT)


def _ada_kernel(c_ref, w_ref, b_ref, o_ref):
    w_hi, w_lo = _split2(w_ref[0])
    o_ref[0] = _dot_w3(_silu(c_ref[...]), w_hi, w_lo) + b_ref[0]


def _ada_call(c_all, w_ada, b_ada):
    n = c_all.shape[0]
    ncomb = w_ada.shape[0]
    tn = D_MODEL
    return pl.pallas_call(
        _ada_kernel,
        grid=(ncomb, 3 * D_MODEL // tn),
        in_specs=[
            pl.BlockSpec((n, D_MODEL), lambda i, j: (0, 0)),
            pl.BlockSpec((1, D_MODEL, tn), lambda i, j: (i, 0, j)),
            pl.BlockSpec((1, 1, tn), lambda i, j: (i, 0, j)),
        ],
        out_specs=pl.BlockSpec((1, n, tn), lambda i, j: (i, 0, j)),
        out_shape=jax.ShapeDtypeStruct((ncomb, n, 3 * D_MODEL), F32),
        compiler_params=_params("arbitrary", "arbitrary"),
        name="ada",
    )(c_all, w_ada, b_ada)


def _route(logits, carry):
    rows = logits.shape[0]
    lane = lax.broadcasted_iota(I32, (rows, LANES), 1)
    lanef = lane.astype(F32)
    big = float(LANES)
    lc = jnp.where(lane < MOE_GROUPS, logits, NEG_BIG)
    mc = jnp.max(lc, axis=-1, keepdims=True)
    gidx = jnp.min(jnp.where(lc == mc, lanef, big), axis=-1, keepdims=True)
    lo = float(MOE_GROUPS) + gidx * float(MOE_EXPERTS)
    in_grp = (lanef >= lo) & (lanef < lo + float(MOE_EXPERTS))
    lf = jnp.where(in_grp, logits, NEG_BIG)
    m1 = jnp.max(lf, axis=-1, keepdims=True)
    i1 = jnp.min(jnp.where(lf == m1, lanef, big), axis=-1, keepdims=True)
    lf2 = jnp.where(lanef == i1, NEG_BIG, lf)
    m2 = jnp.max(lf2, axis=-1, keepdims=True)
    i2 = jnp.min(jnp.where(lf2 == m2, lanef, big), axis=-1, keepdims=True)
    ea = jnp.minimum(i1, i2) - lo
    eb = jnp.maximum(i1, i2) - lo
    bucket = gidx * float(N_PAIRS) + ea * (7.0 - ea) * 0.5 + eb - ea - 1.0
    onehot = jnp.where(lanef == bucket, 1.0, 0.0)
    r = lax.broadcasted_iota(I32, (rows, rows), 0)
    c = lax.broadcasted_iota(I32, (rows, rows), 1)
    before = _dot(jnp.where(r > c, 1.0, 0.0).astype(BF16), onehot.astype(BF16))
    rank = jnp.sum(onehot * (before + carry), axis=-1, keepdims=True)
    new_carry = carry + jnp.sum(onehot, axis=0, keepdims=True)
    cols = jnp.where(lane == R_BUCKET, bucket, 0.0) + jnp.where(lane == R_RANK, rank, 0.0)
    return jnp.transpose(cols)[0:SUBLANES, :], new_carry


def _store_chunked(ref, x):
    rows = x.shape[0]
    for c in range(CHUNKS):
        ref[pl.ds(c, rows, stride=CHUNKS), :] = x[:, c * LANES:(c + 1) * LANES]


def _load_chunked(ref, rows, group, first):
    return jnp.concatenate(
        [ref[pl.ds(first + c, rows, stride=group), :] for c in range(CHUNKS)], axis=1)


def _router_logits(h2, wr_hi_ref, wr_lo_ref, br_ref):
    h_hi, h_lo = _split2(h2)
    wr_hi = wr_hi_ref[...]
    return _dot(h_hi, wr_hi) + _dot(h_lo, wr_hi) + _dot(h_hi, wr_lo_ref[...]) + br_ref[...]


def _post_mixer(x, m, gate, lng_ref, lnb_ref, mod_b, wr_hi_ref, wr_lo_ref, br_ref, carry,
                x1_ref, h2c_ref, route_ref):
    x1 = _layer_norm(ALPHA * x + (1.0 + gate) * m, lng_ref[...], lnb_ref[...])
    shift2, scale2, _ = _mod3(mod_b)
    h2 = x1 * (1.0 + scale2) + shift2
    pad = route_ref.shape[1] - x.shape[0]
    if pad:
        h2 = jnp.concatenate([h2, jnp.zeros((pad, D_MODEL), F32)], axis=0)
    route, carry = _route(_router_logits(h2, wr_hi_ref, wr_lo_ref, br_ref), carry)
    x1_ref[...] = x1
    _store_chunked(h2c_ref, h2)
    route_ref[...] = route
    return carry


def _token_out_specs(rows, slot_rows, x1_map, slot_map):
    return [pl.BlockSpec((rows, D_MODEL), x1_map),
            pl.BlockSpec((slot_rows * CHUNKS, LANES), slot_map),
            pl.BlockSpec((SUBLANES, slot_rows), lambda *i: (0, x1_map(*i)[0]))]


def _token_out_shapes(ntok, nslot_h2, nslot_route):
    return [jax.ShapeDtypeStruct((ntok, D_MODEL), F32),
            jax.ShapeDtypeStruct((nslot_h2 * CHUNKS, LANES), F32),
            jax.ShapeDtypeStruct((SUBLANES, nslot_route), F32)]


def _hold_rows(b, s):
    n, w = b.shape
    if s >= SUBLANES:
        pieces = []
        for p in range(n // (2 * s)):
            r = p * 2 * s + s - 1
            pieces.append(jnp.broadcast_to(b[r:r + 1, :], (2 * s, w)))
        return pieces[0] if len(pieces) == 1 else jnp.concatenate(pieces, axis=0)
    b3 = b.reshape(n // SUBLANES, SUBLANES, w)
    sub = lax.broadcasted_iota(I32, b3.shape, 1)

    def bc(r):
        return jnp.broadcast_to(b3[:, r:r + 1, :], b3.shape)

    out = bc(s - 1)
    for p in range(1, SUBLANES // (2 * s)):
        out = jnp.where(sub >= p * 2 * s, bc(p * 2 * s + s - 1), out)
    return out.reshape(n, w)


def _gla_tile(q, k, v, la, s_prev):
    n = q.shape[0]
    row = lax.broadcasted_iota(I32, (n, n), 0)
    col = lax.broadcasted_iota(I32, (n, n), 1)
    tri = jnp.where(row >= col, 1.0, 0.0).astype(BF16)
    hi, mid, lo = _split3(la)
    b = _dot(tri, hi) + _dot(tri, mid) + _dot(tri, lo)
    b_last = b[n - 1:n, :]
    kl = k * jnp.exp(b_last - b)

    rowi = lax.broadcasted_iota(I32, (n, GLA_KW), 0)
    pair_lanes = [slice(p * LANES, (p + 1) * LANES) for p in range(GLA_HEADS // 2)]

    def swapped(x):
        return [pltpu.roll(x[:, sl], GLA_DK, axis=1).astype(BF16) for sl in pair_lanes]

    def level_operands(qf, kf):
        q_hi = qf.astype(BF16)
        k_hi, k_lo = _split2(kf)
        return q_hi, swapped(qf - q_hi.astype(F32)), k_hi, swapped(k_hi.astype(F32)), k_lo

    levels = []
    s = n // 2
    while s >= 1:
        m = _hold_rows(b, s)
        second = (rowi & s) != 0
        levels.append((2 * s, level_operands(q * jnp.exp(jnp.where(second, b - m, NEG_BIG)),
                                             k * jnp.exp(jnp.where(second, NEG_BIG, m - b)))))
        s //= 2
    levels.append((1, level_operands(q, k)))
    qe = _split2(q * jnp.exp(b))

    lane = lax.broadcasted_iota(I32, (n, LANES), 1)
    xor = row ^ col
    zero = jnp.zeros((), BF16)

    def head_part(parts, sl, hm):
        return tuple(jnp.where(hm, x[:, sl], zero) for x in parts)

    outs = []
    for h in range(GLA_HEADS):
        p = h // 2
        sl = pair_lanes[p]
        hm = (lane < GLA_DK) if h % 2 == 0 else (lane >= GLA_DK)
        att = None
        for span, (q_hi, q_lo_sw, k_hi, k_hi_sw, k_lo) in levels:
            lhs = jnp.concatenate([jnp.where(hm, q_hi[:, sl], q_lo_sw[p]),
                                   jnp.where(hm, q_hi[:, sl], zero)], axis=1)
            rhs = jnp.concatenate([jnp.where(hm, k_hi[:, sl], k_hi_sw[p]), k_lo[:, sl]], axis=1)
            a_s = _dot_nt(lhs, rhs)
            att = a_s if att is None else jnp.where(xor < span, a_s, att)
        o_h = _dot(_lhs3(_split2(att)), _rhs3_rows(v[:, h * GLA_DV:(h + 1) * GLA_DV]))
        o_h = o_h + _dot(_lhs3(head_part(qe, sl, hm)), _rhs3_rows(s_prev[sl, :]))
        outs.append(o_h)

    decay = jnp.exp(b_last)
    upper = lax.broadcasted_iota(I32, (LANES, LANES), 0) < GLA_DK
    s_new = []
    for p in range(GLA_HEADS // 2):
        sl = slice(p * LANES, (p + 1) * LANES)
        kl_t = jnp.transpose(kl[:, sl])
        u = _dot(_lhs3(_split2(kl_t)), _rhs3_rows(v[:, p * 2 * GLA_DV:(p + 1) * 2 * GLA_DV]))
        upd = jnp.where(upper, u[:, 0:GLA_DV], u[:, GLA_DV:2 * GLA_DV])
        dcol = jnp.transpose(jnp.broadcast_to(decay[:, sl], (LANES, LANES)))
        s_new.append(dcol * s_prev[sl, :] + upd)
    return outs, jnp.concatenate(s_new, axis=0)


def _split_projection(z):
    c0 = 0
    q = z[:, c0:c0 + GLA_KW] * (GLA_DK ** -0.5); c0 += GLA_KW
    k = z[:, c0:c0 + GLA_KW]; c0 += GLA_KW
    v = z[:, c0:c0 + GLA_VW]; c0 += GLA_VW
    g = z[:, c0:c0 + GLA_VW]; c0 += GLA_VW
    ua = z[:, c0:c0 + CONV_CH]; c0 += CONV_CH
    ug = z[:, c0:c0 + CONV_CH]; c0 += CONV_CH
    a_lr = z[:, c0:c0 + A_PAD]
    return q, k, v, g, ua, ug, a_lr


def _mix0_project(o_heads, g, y, gng_ref, w_out_ref):
    sl = slice(GLA_VW, GLA_VW + CONV_CH)
    m = _dot_w3(y, w_out_ref[0, sl, :], w_out_ref[1, sl, :])
    for hd in range(GLA_HEADS):
        sl = slice(hd * GLA_DV, (hd + 1) * GLA_DV)
        o_h = _standardize(o_heads[hd]) * gng_ref[:, sl] * _silu(g[:, sl])
        m = m + _dot_w3(o_h, w_out_ref[0, sl, :], w_out_ref[1, sl, :])
    return m


def _mix0_inputs(h, w_in_ref, w_a2_ref, b_a_ref):
    z = _dot_w3(h, w_in_ref[0], w_in_ref[1])
    q, k, v, g, ua, ug, a_lr = _split_projection(z)
    la = _log_sigmoid(_dot_w3(a_lr, w_a2_ref[0], w_a2_ref[1]) + b_a_ref[...]) * (1.0 / GLA_TAU)
    return q, k, v, g, ua, ug, la


def _prompt_grid_specs(bsz, nt):
    ntile = bsz * nt
    seq_map = lambda b, t: (jnp.minimum(b, bsz - 1), 0, 0)
    x_map = lambda b, t: (jnp.minimum(b, bsz - 1), t, 0)
    tok_map = lambda b, t: (jnp.minimum(b * nt + t, ntile - 1), 0)
    slot_map = lambda b, t: (jnp.minimum(b * nt + t, ntile), 0)
    state_map = lambda b, t: (0, jnp.minimum(b, bsz - 1), 0, 0)
    return seq_map, x_map, tok_map, slot_map, state_map


def _mixer0_prompt_kernel(x_ref, moda_ref, modb_ref, cnt_in_ref, h2cs_ref, w_in_ref, w_a2_ref,
                          b_a_ref, gng_ref,
                          cw_ref, cb_ref, clg_ref, clb_ref, w_out_ref, lng_ref, lnb_ref,
                          wr_hi_ref, wr_lo_ref, br_ref,
                          x1_ref, h2c_ref, route_ref, cnt_ref, nconv_ref, ngla_ref,
                          s_ref, cbuf_ref, shift_ref, carry_ref):
    nseq = pl.num_programs(0) - 1

    @pl.when(pl.program_id(0) < nseq)
    def _():
        _mixer0_prompt_tile(x_ref, moda_ref, modb_ref, cnt_in_ref, w_in_ref, w_a2_ref, b_a_ref,
                            gng_ref, cw_ref, cb_ref, clg_ref, clb_ref, w_out_ref, lng_ref, lnb_ref,
                            wr_hi_ref, wr_lo_ref, br_ref, x1_ref, h2c_ref, route_ref, cnt_ref,
                            nconv_ref, ngla_ref, s_ref, cbuf_ref, shift_ref, carry_ref)

    @pl.when((pl.program_id(0) == nseq) & (pl.program_id(1) == 0))
    def _():
        h2c_ref[...] = h2cs_ref[...]


def _mixer0_prompt_tile(x_ref, moda_ref, modb_ref, cnt_in_ref, w_in_ref, w_a2_ref, b_a_ref,
                        gng_ref, cw_ref, cb_ref, clg_ref, clb_ref, w_out_ref, lng_ref, lnb_ref,
                        wr_hi_ref, wr_lo_ref, br_ref, x1_ref, h2c_ref, route_ref, cnt_ref,
                        nconv_ref, ngla_ref, s_ref, cbuf_ref, shift_ref, carry_ref):
    t = pl.program_id(1)
    n = x_ref.shape[1]

    @pl.when((t == 0) & (pl.program_id(0) == 0))
    def _():
        carry_ref[...] = cnt_in_ref[...]

    @pl.when(t == 0)
    def _():
        s_ref[...] = jnp.zeros_like(s_ref)
        cbuf_ref[0:CONV_HALO, :] = jnp.zeros((CONV_HALO, CONV_CH), F32)

    x = x_ref[0]
    shift, scale, gate = _mod3(moda_ref[0])
    h = x * (1.0 + scale) + shift
    q, k, v, g, ua, ug, la = _mix0_inputs(h, w_in_ref, w_a2_ref, b_a_ref)
    o_heads, s_new = _gla_tile(q, k, v, la, s_ref[...])
    s_ref[...] = s_new

    glu = ua * _sigmoid(ug)
    cbuf_ref[CONV_HALO:CONV_HALO + n, :] = glu
    acc = jnp.broadcast_to(cb_ref[...], (n, CONV_CH))
    base = CONV_HALO - CONV_BUF
    for r in range(SUBLANES):
        taps = range(r, CONV_WIDTH, SUBLANES)
        span = n + (len(taps) - 1) * SUBLANES
        if (base + r) % SUBLANES == 0:
            src, off = cbuf_ref, base + r
        else:
            shift_ref[r, 0:span, :] = cbuf_ref[base + r:base + r + span, :]
            src, off = shift_ref.at[r], 0
        for a, j in enumerate(taps):
            lo = off + a * SUBLANES
            acc = acc + cw_ref[j:j + 1, :] * src[lo:lo + n, :]
    y = _silu(_layer_norm(acc, clg_ref[...], clb_ref[...]))

    @pl.when(t == pl.num_programs(1) - 1)
    def _():
        nconv_ref[0, 0] = cbuf_ref[CONV_HALO + n - CONV_BUF:CONV_HALO + n, :]
        ngla_ref[0, 0] = s_new.reshape(GLA_HEADS, GLA_DK, GLA_DV)

    cbuf_ref[0:CONV_HALO, :] = cbuf_ref[n:n + CONV_HALO, :]

    m = _mix0_project(o_heads, g, y, gng_ref, w_out_ref)
    carry = _post_mixer(x, m, gate, lng_ref, lnb_ref, modb_ref[0], wr_hi_ref, wr_lo_ref, br_ref,
                        carry_ref[...], x1_ref, h2c_ref, route_ref)
    carry_ref[...] = carry
    cnt_ref[...] = carry


def _mixer0_prompt_call(x, mod_a, mod_b, cnt, h2c_s, p):
    bsz, seq, _ = x.shape
    tl = TILE_L
    nt = seq // tl
    seq_map, x_map, tok, slot, state_map = _prompt_grid_specs(bsz, nt)
    return pl.pallas_call(
        _mixer0_prompt_kernel,
        grid=(bsz + 1, nt),
        in_specs=[
            pl.BlockSpec((1, tl, D_MODEL), x_map),
            pl.BlockSpec((1, 1, 3 * D_MODEL), seq_map),
            pl.BlockSpec((1, 1, 3 * D_MODEL), seq_map),
            _full((1, LANES)),
            _full((tl * CHUNKS, LANES)),
            _const((2, D_MODEL, Z_WIDTH)),
            _const((2, A_PAD, GLA_KW)),
            _full((1, GLA_KW)),
            _full((1, GLA_VW)),
            _full((CONV_W_ROWS, CONV_CH)),
            _full((1, CONV_CH)),
            _full((1, CONV_CH)),
            _full((1, CONV_CH)),
            _const((2, GLA_VW + CONV_CH, D_MODEL)),
            _full((1, D_MODEL)),
            _full((1, D_MODEL)),
            _full((D_MODEL, LANES)),
            _full((D_MODEL, LANES)),
            _full((1, LANES)),
        ],
        out_specs=_token_out_specs(tl, tl, tok, slot) + [
            _full((1, LANES)),
            pl.BlockSpec((1, 1, CONV_BUF, CONV_CH), state_map),
            pl.BlockSpec((1, 1, GLA_HEADS, GLA_DK, GLA_DV), lambda b, t: state_map(b, t) + (0,)),
        ],
        out_shape=_token_out_shapes(bsz * seq, bsz * seq + tl, bsz * seq) + [
            jax.ShapeDtypeStruct((1, LANES), F32),
            jax.ShapeDtypeStruct((1, bsz, CONV_BUF, CONV_CH), F32),
            jax.ShapeDtypeStruct((1, bsz, GLA_HEADS, GLA_DK, GLA_DV), F32),
        ],
        scratch_shapes=[
            pltpu.VMEM((GLA_KW, GLA_DV), F32),
            pltpu.VMEM((CONV_HALO + tl, CONV_CH), F32),
            pltpu.VMEM((SUBLANES, CONV_HALO + tl, CONV_CH), F32),
            pltpu.VMEM((1, LANES), F32),
        ],
        compiler_params=_params("arbitrary", "arbitrary"),
        name="mixer0_prompt",
    )(x, mod_a, mod_b, cnt, h2c_s, p["w_in"], p["w_a2"], p["b_a"], p["gng"], p["conv_w"], p["conv_b"],
      p["conv_ln_g"], p["conv_ln_b"], p["w_out0"], p["ln_g00"], p["ln_b00"],
      p["wr_hi0"], p["wr_lo0"], p["br0"])


def _mixer0_sample_kernel(x_ref, moda_ref, modb_ref, sgla_ref, sconv_ref, cnt_in_ref,
                          w_in_ref, w_a2_ref,
                          b_a_ref, gng_ref, cw_ref, cb_ref, clg_ref, clb_ref, w_out_ref, lng_ref,
                          lnb_ref, wr_hi_ref, wr_lo_ref, br_ref,
                          x1_ref, h2c_ref, route_ref, cnt_ref, ngla_ref, nconv_ref,
                          zt_ref, v_ref, g_ref, glu_ref, o_ref, y_ref):
    i = pl.program_id(0)
    nb = sgla_ref.shape[1]
    ntok = x_ref.shape[0]

    @pl.when(i == 0)
    def _():
        shift, scale, _ = _mod3(moda_ref[...])
        h = x_ref[...] * (1.0 + scale) + shift
        q, k, v, g, ua, ug, la = _mix0_inputs(h, w_in_ref, w_a2_ref, b_a_ref)
        v_ref[...] = v
        g_ref[...] = g
        glu_ref[...] = ua * _sigmoid(ug)
        for j, val in enumerate((jnp.exp(la), k, q)):
            hi, lo = _split2(jnp.transpose(val))
            zt_ref[(2 * j) * GLA_KW:(2 * j + 1) * GLA_KW, :] = hi
            zt_ref[(2 * j + 1) * GLA_KW:(2 * j + 2) * GLA_KW, :] = lo

    tok_row = lax.broadcasted_iota(I32, (ntok, LANES), 0)
    blk = pl.ds(pl.multiple_of(i * nb, nb), nb)
    v_blk = v_ref[blk, :]
    glu_blk = glu_ref[blk, :]
    o_rows, y_rows = [], []
    for n in range(nb):
        onehot = jnp.where(tok_row == i * nb + n, 1.0, 0.0).astype(BF16)
        cols = _dot(zt_ref[...], onehot)
        a_col = cols[0:GLA_KW] + cols[GLA_KW:2 * GLA_KW]
        k_col = cols[2 * GLA_KW:3 * GLA_KW] + cols[3 * GLA_KW:4 * GLA_KW]
        q_col = cols[4 * GLA_KW:5 * GLA_KW] + cols[5 * GLA_KW:6 * GLA_KW]
        v_row = v_blk[n:n + 1, :]
        v_b = jnp.concatenate(
            [jnp.broadcast_to(v_row[:, h * GLA_DV:(h + 1) * GLA_DV], (GLA_DK, GLA_DV))
             for h in range(GLA_HEADS)], axis=0)
        s_old = sgla_ref[0, n].reshape(GLA_KW, GLA_DV)
        s_new = a_col * s_old + k_col * v_b
        ngla_ref[0, n] = s_new.reshape(GLA_HEADS, GLA_DK, GLA_DV)
        o4 = jnp.sum((q_col * s_new).reshape(GLA_HEADS, GLA_DK, GLA_DV), axis=1)
        o_rows.append(jnp.concatenate([o4[h:h + 1, :] for h in range(GLA_HEADS)], axis=1))
        glu_row = glu_blk[n:n + 1, :]
        past = sconv_ref[0, n]
        y_rows.append(jnp.sum(past * cw_ref[0:CONV_BUF, :], axis=0, keepdims=True)
                      + glu_row * cw_ref[CONV_BUF:CONV_WIDTH, :] + cb_ref[...])
        nconv_ref[0, n, 0:CONV_BUF - 1, :] = sconv_ref[0, n, 1:CONV_BUF, :]
        nconv_ref[0, n, CONV_BUF - 1:CONV_BUF, :] = glu_row
    o_ref[blk, :] = jnp.concatenate(o_rows, axis=0)
    y_ref[blk, :] = jnp.concatenate(y_rows, axis=0)

    @pl.when(i == pl.num_programs(0) - 1)
    def _():
        y = _silu(_layer_norm(y_ref[...], clg_ref[...], clb_ref[...]))
        o_heads = [o_ref[:, hd * GLA_DV:(hd + 1) * GLA_DV] for hd in range(GLA_HEADS)]
        m = _mix0_project(o_heads, g_ref[...], y, gng_ref, w_out_ref)
        _, _, gate = _mod3(moda_ref[...])
        cnt_ref[...] = _post_mixer(x_ref[...], m, gate, lng_ref, lnb_ref, modb_ref[...], wr_hi_ref,
                                   wr_lo_ref, br_ref, cnt_in_ref[...], x1_ref, h2c_ref, route_ref)


def _mixer0_sample_call(x, mod_a, mod_b, state_gla, state_conv, cnt, p):
    ntok = x.shape[0]
    nb = SAMPLE_BLK
    tok = lambda i: (0, 0)
    return pl.pallas_call(
        _mixer0_sample_kernel,
        grid=(ntok // nb,),
        in_specs=[
            _full((ntok, D_MODEL)),
            _full((ntok, 3 * D_MODEL)),
            _full((ntok, 3 * D_MODEL)),
            pl.BlockSpec((1, nb, GLA_HEADS, GLA_DK, GLA_DV), lambda i: (0, i, 0, 0, 0)),
            pl.BlockSpec((1, nb, CONV_BUF, CONV_CH), lambda i: (0, i, 0, 0)),
            _full((1, LANES)),
            _const((2, D_MODEL, Z_WIDTH)),
            _const((2, A_PAD, GLA_KW)),
            _full((1, GLA_KW)),
            _full((1, GLA_VW)),
            _full((CONV_W_ROWS, CONV_CH)),
            _full((1, CONV_CH)),
            _full((1, CONV_CH)),
            _full((1, CONV_CH)),
            _const((2, GLA_VW + CONV_CH, D_MODEL)),
            _full((1, D_MODEL)),
            _full((1, D_MODEL)),
            _full((D_MODEL, LANES)),
            _full((D_MODEL, LANES)),
            _full((1, LANES)),
        ],
        out_specs=_token_out_specs(ntok, TILE_L, tok, tok) + [
            _full((1, LANES)),
            pl.BlockSpec((1, nb, GLA_HEADS, GLA_DK, GLA_DV), lambda i: (0, i, 0, 0, 0)),
            pl.BlockSpec((1, nb, CONV_BUF, CONV_CH), lambda i: (0, i, 0, 0)),
        ],
        out_shape=_token_out_shapes(ntok, TILE_L, TILE_L) + [
            jax.ShapeDtypeStruct((1, LANES), F32),
            jax.ShapeDtypeStruct(state_gla.shape, F32),
            jax.ShapeDtypeStruct(state_conv.shape, F32),
        ],
        scratch_shapes=[
            pltpu.VMEM((6 * GLA_KW, ntok), BF16),
            pltpu.VMEM((ntok, GLA_VW), F32),
            pltpu.VMEM((ntok, GLA_VW), F32),
            pltpu.VMEM((ntok, CONV_CH), F32),
            pltpu.VMEM((ntok, GLA_VW), F32),
            pltpu.VMEM((ntok, CONV_CH), F32),
        ],
        compiler_params=_params("arbitrary"),
        name="mixer0_sample",
    )(x, mod_a, mod_b, state_gla, state_conv, cnt, p["w_in"], p["w_a2"], p["b_a"],
      p["gng"], p["conv_w"], p["conv_b"], p["conv_ln_g"], p["conv_ln_b"], p["w_out0"],
      p["ln_g00"], p["ln_b00"], p["wr_hi0"], p["wr_lo0"], p["br0"])


def _pool_project(pooled, h, wg_ref, ps_ref, w_out_ref):
    m = None
    for gi in range(len(POOL_WINDOWS)):
        sl = slice(gi * POOL_GC, (gi + 1) * POOL_GC)
        mixed = _dot((pooled[:, sl] - h[:, sl]).astype(BF16), wg_ref[gi]) * ps_ref[:, sl]
        part = _dot(mixed.astype(BF16), w_out_ref[sl, :])
        m = part if m is None else m + part
    return m


def _block_input(xin_ref, m_ref, mod_m, lnpg_ref, lnpb_ref):
    m = _load_chunked(m_ref, xin_ref.shape[0], CHUNKS, 0)
    _, _, gate = _mod3(mod_m)
    return _layer_norm(ALPHA * xin_ref[...] + (1.0 + gate) * m, lnpg_ref[...], lnpb_ref[...])


def _mixer1_prompt_kernel(dest_ref, dnext_ref, xin_ref, y_any, modm_ref, lnpg_ref, lnpb_ref,
                          moda_ref, modb_ref, cnt_in_ref, h2cs_ref, wg_ref, ps_ref,
                          w_out_ref, lng_ref, lnb_ref, wr_hi_ref, wr_lo_ref, br_ref,
                          x1_ref, h2c_ref, route_ref, cnt_ref, npool_ref,
                          pbuf_ref, carry_ref, mbuf, sems):
    nseq = pl.num_programs(0) - 1
    nt = pl.num_programs(1)

    @pl.when(pl.program_id(0) < nseq)
    def _():
        m_ref, drain = _gather_pipelined(pl.program_id(0) * nt + pl.program_id(1), nseq * nt,
                                         dest_ref, dnext_ref, y_any, mbuf, sems)
        x = _block_input(xin_ref, m_ref, modm_ref[0], lnpg_ref, lnpb_ref)
        _mixer1_prompt_tile(x, moda_ref, modb_ref, cnt_in_ref, wg_ref, ps_ref, w_out_ref,
                            lng_ref, lnb_ref, wr_hi_ref, wr_lo_ref, br_ref, x1_ref, h2c_ref,
                            route_ref, cnt_ref, npool_ref, pbuf_ref, carry_ref)
        drain()

    @pl.when((pl.program_id(0) == nseq) & (pl.program_id(1) == 0))
    def _():
        h2c_ref[...] = h2cs_ref[...]


def _mixer1_prompt_tile(x, moda_ref, modb_ref, cnt_in_ref, wg_ref, ps_ref, w_out_ref,
                        lng_ref, lnb_ref, wr_hi_ref, wr_lo_ref, br_ref, x1_ref, h2c_ref,
                        route_ref, cnt_ref, npool_ref, pbuf_ref, carry_ref):
    t = pl.program_id(1)
    n = x.shape[0]

    @pl.when((t == 0) & (pl.program_id(0) == 0))
    def _():
        carry_ref[...] = cnt_in_ref[...]

    @pl.when(t == 0)
    def _():
        pbuf_ref[0:POOL_HALO, :] = jnp.zeros((POOL_HALO, D_MODEL), F32)

    shift, scale, gate = _mod3(moda_ref[0])
    h = x * (1.0 + scale) + shift
    pbuf_ref[POOL_HALO:POOL_HALO + n, :] = h

    cur = pbuf_ref[...]
    sums = []
    for gi, w in enumerate(POOL_WINDOWS):
        cur = cur + pltpu.roll(cur, w // 2, axis=0)
        sums.append(cur[POOL_HALO:, 0:POOL_GC])
        if gi + 1 < len(POOL_WINDOWS):
            cur = cur[:, POOL_GC:]
    pos = lax.broadcasted_iota(I32, (n, POOL_GC), 0) + t * n
    pooled = jnp.concatenate(
        [s / jnp.minimum(w, pos + 1).astype(F32) for s, w in zip(sums, POOL_WINDOWS)], axis=1)

    @pl.when(t == pl.num_programs(1) - 1)
    def _():
        npool_ref[0, 0] = pbuf_ref[POOL_HALO + n - POOL_BUF:POOL_HALO + n, :]

    pbuf_ref[0:POOL_HALO, :] = pbuf_ref[n:n + POOL_HALO, :]

    m = _pool_project(pooled, h, wg_ref, ps_ref, w_out_ref)
    carry = _post_mixer(x, m, gate, lng_ref, lnb_ref, modb_ref[0], wr_hi_ref, wr_lo_ref, br_ref,
                        carry_ref[...], x1_ref, h2c_ref, route_ref)
    carry_ref[...] = carry
    cnt_ref[...] = carry


def _mixer1_prompt_call(x1_prev, dest, y_sorted, mod_m, lnpg, lnpb, mod_a, mod_b, cnt, h2c_s, bsz, p):
    seq = x1_prev.shape[0] // bsz
    tl = TILE_L
    nt = seq // tl
    ntile = bsz * nt
    seq_map, _, tok, slot, state_map = _prompt_grid_specs(bsz, nt)
    ng = len(POOL_WINDOWS)
    return pl.pallas_call(
        _mixer1_prompt_kernel,
        grid=(bsz + 1, nt),
        in_specs=[
            pl.BlockSpec((1, 1, tl), lambda b, t: tok(b, t) + (0,), memory_space=pltpu.SMEM),
            pl.BlockSpec((1, 1, tl), lambda b, t: (jnp.minimum(b * nt + t + 1, ntile - 1), 0, 0),
                         memory_space=pltpu.SMEM),
            pl.BlockSpec((tl, D_MODEL), tok),
            pl.BlockSpec(memory_space=pl.ANY),
            pl.BlockSpec((1, 1, 3 * D_MODEL), seq_map),
            _full((1, D_MODEL)),
            _full((1, D_MODEL)),
            pl.BlockSpec((1, 1, 3 * D_MODEL), seq_map),
            pl.BlockSpec((1, 1, 3 * D_MODEL), seq_map),
            _full((1, LANES)),
            _full((tl * CHUNKS, LANES)),
            _full((ng, POOL_GC, POOL_GC)),
            _full((1, D_MODEL)),
            _full((D_MODEL, D_MODEL)),
            _full((1, D_MODEL)),
            _full((1, D_MODEL)),
            _full((D_MODEL, LANES)),
            _full((D_MODEL, LANES)),
            _full((1, LANES)),
        ],
        out_specs=_token_out_specs(tl, tl, tok, slot) + [
            _full((1, LANES)),
            pl.BlockSpec((1, 1, POOL_BUF, D_MODEL), state_map),
        ],
        out_shape=_token_out_shapes(bsz * seq, bsz * seq + tl, bsz * seq) + [
            jax.ShapeDtypeStruct((1, LANES), F32),
            jax.ShapeDtypeStruct((1, bsz, POOL_BUF, D_MODEL), F32),
        ],
        scratch_shapes=[pltpu.VMEM((POOL_HALO + tl, D_MODEL), F32), pltpu.VMEM((1, LANES), F32),
                        pltpu.VMEM((2, tl * CHUNKS, LANES), F32), pltpu.SemaphoreType.DMA((2,))],
        compiler_params=_params("arbitrary", "arbitrary"),
        name="mixer1_prompt",
    )(dest.reshape(ntile, 1, tl), dest.reshape(ntile, 1, tl), x1_prev, y_sorted, mod_m, lnpg, lnpb,
      mod_a, mod_b, cnt, h2c_s,
      p["w_grp"], p["pool_scale"], p["w_out1"], p["ln_g10"], p["ln_b10"],
      p["wr_hi1"], p["wr_lo1"], p["br1"])


def _mixer1_sample_kernel(dest_ref, xin_ref, y_any, modm_ref, lnpg_ref, lnpb_ref, moda_ref,
                          modb_ref, spool_ref, cnt_in_ref, wg_ref, ps_ref, w_out_ref,
                          lng_ref, lnb_ref, wr_hi_ref, wr_lo_ref, br_ref,
                          x1_ref, h2c_ref, route_ref, cnt_ref, npool_ref,
                          x_ref, h_ref, pooled_ref, mbuf, sem):
    i = pl.program_id(0)
    nb = spool_ref.shape[1]

    @pl.when(i == 0)
    def _():
        _start_row_gather(dest_ref, y_any, mbuf, sem)
        _wait_row_gather(y_any, mbuf, sem)
        x_ref[...] = _block_input(xin_ref, mbuf, modm_ref[...], lnpg_ref, lnpb_ref)
        shift, scale, _ = _mod3(moda_ref[...])
        h_ref[...] = x_ref[...] * (1.0 + scale) + shift

    lane = lax.broadcasted_iota(I32, (POOL_BUF, D_MODEL), 1)
    rowi = lax.broadcasted_iota(I32, (POOL_BUF, D_MODEL), 0)
    first = jnp.zeros((POOL_BUF, D_MODEL), I32)
    lane1 = lax.broadcasted_iota(I32, (1, D_MODEL), 1)
    inv_w = jnp.zeros((1, D_MODEL), F32)
    for gi, w in enumerate(POOL_WINDOWS):
        in_g = (lane >= gi * POOL_GC) & (lane < (gi + 1) * POOL_GC)
        first = jnp.where(in_g, POOL_BUF - (w - 1), first)
        in_g1 = (lane1 >= gi * POOL_GC) & (lane1 < (gi + 1) * POOL_GC)
        inv_w = jnp.where(in_g1, 1.0 / w, inv_w)
    keep = rowi >= first

    blk = pl.ds(pl.multiple_of(i * nb, nb), nb)
    h_blk = h_ref[blk, :]
    rows = []
    for n in range(nb):
        h_row = h_blk[n:n + 1, :]
        past = spool_ref[0, n]
        total = jnp.sum(jnp.where(keep, past, 0.0), axis=0, keepdims=True) + h_row
        rows.append(total * inv_w)
        npool_ref[0, n, 0:POOL_BUF - 1, :] = spool_ref[0, n, 1:POOL_BUF, :]
        npool_ref[0, n, POOL_BUF - 1:POOL_BUF, :] = h_row
    pooled_ref[blk, :] = jnp.concatenate(rows, axis=0)

    @pl.when(i == pl.num_programs(0) - 1)
    def _():
        m = _pool_project(pooled_ref[...], h_ref[...], wg_ref, ps_ref, w_out_ref)
        _, _, gate = _mod3(moda_ref[...])
        cnt_ref[...] = _post_mixer(x_ref[...], m, gate, lng_ref, lnb_ref, modb_ref[...], wr_hi_ref,
                                   wr_lo_ref, br_ref, cnt_in_ref[...], x1_ref, h2c_ref, route_ref)


def _mixer1_sample_call(x1_prev, dest, y_sorted, mod_m, lnpg, lnpb, mod_a, mod_b, state_pool,
                        cnt, p):
    ntok = x1_prev.shape[0]
    nb = SAMPLE_BLK
    ng = len(POOL_WINDOWS)
    tok = lambda i: (0, 0)
    return pl.pallas_call(
        _mixer1_sample_kernel,
        grid=(ntok // nb,),
        in_specs=[
            pl.BlockSpec((1, 1, ntok), lambda i: (0, 0, 0), memory_space=pltpu.SMEM),
            _full((ntok, D_MODEL)),
            pl.BlockSpec(memory_space=pl.ANY),
            _full((ntok, 3 * D_MODEL)),
            _full((1, D_MODEL)),
            _full((1, D_MODEL)),
            _full((ntok, 3 * D_MODEL)),
            _full((ntok, 3 * D_MODEL)),
            pl.BlockSpec((1, nb, POOL_BUF, D_MODEL), lambda i: (0, i, 0, 0)),
            _full((1, LANES)),
            _full((ng, POOL_GC, POOL_GC)),
            _full((1, D_MODEL)),
            _full((D_MODEL, D_MODEL)),
            _full((1, D_MODEL)),
            _full((1, D_MODEL)),
            _full((D_MODEL, LANES)),
            _full((D_MODEL, LANES)),
            _full((1, LANES)),
        ],
        out_specs=_token_out_specs(ntok, TILE_L, tok, tok) + [
            _full((1, LANES)),
            pl.BlockSpec((1, nb, POOL_BUF, D_MODEL), lambda i: (0, i, 0, 0)),
        ],
        out_shape=_token_out_shapes(ntok, TILE_L, TILE_L) + [
            jax.ShapeDtypeStruct((1, LANES), F32),
            jax.ShapeDtypeStruct(state_pool.shape, F32),
        ],
        scratch_shapes=[pltpu.VMEM((ntok, D_MODEL), F32)] * 3 + [
            pltpu.VMEM((ntok * CHUNKS, LANES), F32), pltpu.SemaphoreType.DMA],
        compiler_params=_params("arbitrary"),
        name="mixer1_sample",
    )(dest.reshape(1, 1, ntok), x1_prev, y_sorted, mod_m, lnpg, lnpb, mod_a, mod_b, state_pool, cnt,
      p["w_grp"], p["pool_scale"], p["w_out1"],
      p["ln_g10"], p["ln_b10"], p["wr_hi1"], p["wr_lo1"], p["br1"])


def _dispatch_kernel(sched_ref, dest_ref, src_ref, dst_any, zeros_ref, sem, zsem):
    n = dest_ref.shape[2]
    tile_rows = zeros_ref.shape[0]

    @pl.when(pl.program_id(0) == 0)
    def _():
        zeros_ref[...] = jnp.zeros_like(zeros_ref)

        def fill(i):
            return pltpu.make_async_copy(
                zeros_ref, dst_any.at[pl.ds(i * tile_rows, tile_rows)], zsem)

        for i in range(dst_any.shape[0] // tile_rows):
            pl.when(sched_ref[S_ZERO, i] == 1)(lambda i=i: fill(i).start())
        for i in range(dst_any.shape[0] // tile_rows):
            pl.when(sched_ref[S_ZERO, i] == 1)(lambda i=i: fill(i).wait())

    for j in range(n):
        d = dest_ref[0, 0, j]
        pltpu.make_async_copy(
            src_ref.at[pl.ds(j * CHUNKS, CHUNKS)],
            dst_any.at[pl.ds(pl.multiple_of(d * CHUNKS, CHUNKS), CHUNKS)], sem).start()
    pltpu.make_async_copy(src_ref, dst_any.at[pl.ds(0, n * CHUNKS)], sem).wait()


def _dispatch_call(sched, dest, h2c_all):
    nslot = dest.shape[1]
    rows = DISPATCH_ROWS
    tile_rows = MOE_TILE * CHUNKS
    grid_spec = pltpu.PrefetchScalarGridSpec(
        num_scalar_prefetch=1,
        grid=(nslot // rows,),
        in_specs=[
            pl.BlockSpec((1, 1, rows), lambda i, s: (i, 0, 0), memory_space=pltpu.SMEM),
            pl.BlockSpec((rows * CHUNKS, LANES), lambda i, s: (i, 0)),
        ],
        out_specs=pl.BlockSpec(memory_space=pl.ANY),
        scratch_shapes=[pltpu.VMEM((tile_rows, LANES), F32), pltpu.SemaphoreType.DMA,
                        pltpu.SemaphoreType.DMA],
    )
    return pl.pallas_call(
        _dispatch_kernel,
        grid_spec=grid_spec,
        out_shape=jax.ShapeDtypeStruct((_num_tiles(nslot) * tile_rows, LANES), F32),
        compiler_params=_params("arbitrary"),
        name="dispatch",
    )(sched, dest.reshape(nslot // rows, 1, rows), h2c_all)


(S_BLK, S_EA, S_EB, S_VALID, S_FIRST, S_LANE_G, S_LANE_A, S_LANE_B, S_ZERO,
 S_NEA, S_NEB, S_HASNEXT) = range(12)
SCHED_ROWS = 2 * SUBLANES


def _moe_kernel(sched_ref, x_ref, wr_hi_ref, wr_lo_ref, br_ref, wg_any, wu_any, wd_any,
                y_ref, wgu_ref, wd_ref, wbuf_gu, wbuf_d, wsem, wslot_ref):
    i = pl.program_id(0)
    tm = x_ref.shape[0] // CHUNKS

    def weight_copies(ea, eb, slot):
        srcs = ((wg_any, ea), (wu_any, ea), (wg_any, eb), (wu_any, eb))
        cps = [pltpu.make_async_copy(w.at[e], wbuf_gu.at[slot, j], wsem.at[slot])
               for j, (w, e) in enumerate(srcs)]
        return cps + [pltpu.make_async_copy(wd_any.at[e], wbuf_d.at[slot, j], wsem.at[slot])
                      for j, e in enumerate((ea, eb))]

    @pl.when(i == 0)
    def _():
        wslot_ref[0] = 0
        for cp in weight_copies(sched_ref[S_EA, 0], sched_ref[S_EB, 0], 0):
            cp.start()

    @pl.when(sched_ref[S_FIRST, i] == 1)
    def _():
        slot = wslot_ref[0]
        for cp in weight_copies(sched_ref[S_EA, i], sched_ref[S_EB, i], slot):
            cp.wait()

        @pl.when(sched_ref[S_HASNEXT, i] == 1)
        def _():
            for cp in weight_copies(sched_ref[S_NEA, i], sched_ref[S_NEB, i], 1 - slot):
                cp.start()

        for e in range(2):
            wgu_ref[2 * e] = wbuf_gu[slot, 2 * e].astype(BF16)
            wgu_ref[2 * e + 1] = wbuf_gu[slot, 2 * e + 1].astype(BF16)
            wd_ref[e * MOE_FF:(e + 1) * MOE_FF, :] = wbuf_d[slot, e].astype(BF16)
        wslot_ref[0] = 1 - slot

    @pl.when(sched_ref[S_VALID, i] == 0)
    def _():
        y_ref[...] = jnp.zeros_like(y_ref)

    @pl.when(sched_ref[S_VALID, i] == 1)
    def _():
        x = _load_chunked(x_ref, tm, CHUNKS, 0)
        logits = _router_logits(x, wr_hi_ref, wr_lo_ref, br_ref)
        lane = lax.broadcasted_iota(I32, logits.shape, 1)

        def pick(row):
            return jnp.sum(jnp.where(lane == sched_ref[row, i], logits, 0.0), axis=-1, keepdims=True)

        l_g, l_a, l_b = pick(S_LANE_G), pick(S_LANE_A), pick(S_LANE_B)
        p_g = 1.0 / jnp.sum(jnp.where(lane < MOE_GROUPS, jnp.exp(logits - l_g), 0.0),
                            axis=-1, keepdims=True)
        w_ab = (p_g / (1.0 + jnp.exp(l_b - l_a)), p_g / (1.0 + jnp.exp(l_a - l_b)))
        xb = x.astype(BF16)
        hid = [(_silu(_dot(xb, wgu_ref[2 * e])) * _dot(xb, wgu_ref[2 * e + 1]) * w_ab[e]).astype(BF16)
               for e in range(2)]
        _store_chunked(y_ref, _dot(jnp.concatenate(hid, axis=1), wd_ref[...]))


def _moe_call(sched, sorted_x, wr_hi, wr_lo, br, wg, wu, wd):
    tm = MOE_TILE
    ntile = sorted_x.shape[0] // (tm * CHUNKS)
    const2 = lambda i, s: (0, 0)
    anyspec = pl.BlockSpec(memory_space=pl.ANY)
    grid_spec = pltpu.PrefetchScalarGridSpec(
        num_scalar_prefetch=1,
        grid=(ntile,),
        in_specs=[
            pl.BlockSpec((tm * CHUNKS, LANES), lambda i, s: (s[S_BLK, i], 0)),
            pl.BlockSpec((D_MODEL, LANES), const2),
            pl.BlockSpec((D_MODEL, LANES), const2),
            pl.BlockSpec((1, LANES), const2),
            anyspec, anyspec, anyspec,
        ],
        out_specs=pl.BlockSpec((tm * CHUNKS, LANES), lambda i, s: (i, 0)),
        scratch_shapes=[
            pltpu.VMEM((4, D_MODEL, MOE_FF), BF16),
            pltpu.VMEM((2 * MOE_FF, D_MODEL), BF16),
            pltpu.VMEM((2, 4, D_MODEL, MOE_FF), F32),
            pltpu.VMEM((2, 2, MOE_FF, D_MODEL), F32),
            pltpu.SemaphoreType.DMA((2,)),
            pltpu.SMEM((1,), I32),
        ],
    )
    return pl.pallas_call(
        _moe_kernel,
        grid_spec=grid_spec,
        out_shape=jax.ShapeDtypeStruct((ntile * tm * CHUNKS, LANES), F32),
        compiler_params=_params("arbitrary"),
        name="moe",
    )(sched, sorted_x, wr_hi, wr_lo, br, wg, wu, wd)


def _start_row_gather(dest_ref, y_any, buf, sem):
    for j in range(dest_ref.shape[2]):
        d = dest_ref[0, 0, j]
        pltpu.make_async_copy(
            y_any.at[pl.ds(pl.multiple_of(d * CHUNKS, CHUNKS), CHUNKS)],
            buf.at[pl.ds(j * CHUNKS, CHUNKS)], sem).start()


def _wait_row_gather(y_any, buf, sem):
    pltpu.make_async_copy(y_any.at[pl.ds(0, buf.shape[0])], buf, sem).wait()


def _gather_pipelined(step, nstep, dest_ref, dnext_ref, y_any, mbuf, sems):
    slot = step % 2

    @pl.when(step == 0)
    def _():
        _start_row_gather(dest_ref, y_any, mbuf.at[0], sems.at[0])

    _start_row_gather(dnext_ref, y_any, mbuf.at[1 - slot], sems.at[1 - slot])
    _wait_row_gather(y_any, mbuf.at[slot], sems.at[slot])

    def drain():
        @pl.when(step == nstep - 1)
        def _():
            _wait_row_gather(y_any, mbuf.at[1 - slot], sems.at[1 - slot])

    return mbuf.at[slot], drain


def _post_moe_kernel(dest_ref, dnext_ref, x1_ref, y_any, mod_ref, lng_ref, lnb_ref, out_ref,
                     mbuf, sems):
    m_ref, drain = _gather_pipelined(pl.program_id(0), pl.num_programs(0), dest_ref, dnext_ref,
                                     y_any, mbuf, sems)
    out_ref[...] = _block_input(x1_ref, m_ref, mod_ref[0], lng_ref, lnb_ref)
    drain()


def _post_moe_call(x1, dest, y_sorted, mod, lng, lnb, rows, steps_per_mod):
    mrows = mod.shape[1]
    nstep = x1.shape[0] // rows
    dest3 = dest.reshape(nstep, 1, rows)
    return pl.pallas_call(
        _post_moe_kernel,
        grid=(nstep,),
        in_specs=[
            pl.BlockSpec((1, 1, rows), lambda i: (i, 0, 0), memory_space=pltpu.SMEM),
            pl.BlockSpec((1, 1, rows), lambda i: (jnp.minimum(i + 1, nstep - 1), 0, 0),
                         memory_space=pltpu.SMEM),
            pl.BlockSpec((rows, D_MODEL), lambda i: (i, 0)),
            pl.BlockSpec(memory_space=pl.ANY),
            pl.BlockSpec((1, mrows, 3 * D_MODEL), lambda i: (i // steps_per_mod, 0, 0)),
            _full((1, D_MODEL)),
            _full((1, D_MODEL)),
        ],
        out_specs=pl.BlockSpec((rows, D_MODEL), lambda i: (i, 0)),
        out_shape=jax.ShapeDtypeStruct(x1.shape, F32),
        scratch_shapes=[pltpu.VMEM((2, rows * CHUNKS, LANES), F32), pltpu.SemaphoreType.DMA((2,))],
        compiler_params=_params("arbitrary"),
        name="post_moe",
    )(dest3, dest3, x1, y_sorted, mod, lng, lnb)


PLAN_CHUNK = 2048


def _num_tiles(nslot):
    return -(-nslot // MOE_TILE) + N_BUCKETS


def _plan_kernel(rp_ref, rs_ref, cnt_ref, dest_ref, sched_ref, *, layer, ntile):
    tm = MOE_TILE
    lane = lax.broadcasted_iota(I32, (SUBLANES, LANES), 1)
    row = lax.broadcasted_iota(I32, (LANES, LANES), 0)
    col = lax.broadcasted_iota(I32, (LANES, LANES), 1)
    counts = jnp.where(lane < N_BUCKETS, jnp.broadcast_to(cnt_ref[...], (SUBLANES, LANES)), 0.0)
    tiles_b = jnp.floor((counts + float(tm - 1)) * (1.0 / tm))
    tile_end = _dot(tiles_b.astype(BF16), jnp.where(row <= col, 1.0, 0.0).astype(BF16))
    tile_start = tile_end - tiles_b
    start_b = tile_start.astype(BF16)

    def dest_of(route):
        n = route.shape[1]
        bucket, rank = route[R_BUCKET:R_BUCKET + 1, :], route[R_RANK:R_RANK + 1, :]
        b_iota = lax.broadcasted_iota(I32, (LANES, n), 0).astype(F32)
        onehot = jnp.where(b_iota == bucket, 1.0, 0.0).astype(BF16)
        start = _dot(start_b, onehot)[0:1, :]
        return (start * float(tm) + rank).astype(I32)

    ntok_p = rp_ref.shape[1]
    chunk = math.gcd(ntok_p, PLAN_CHUNK)
    for c0 in range(0, ntok_p, chunk):
        dest_ref[:, c0:c0 + chunk] = dest_of(rp_ref[:, c0:c0 + chunk])
    dest_ref[:, ntok_p:] = dest_of(rs_ref[...])

    tile = lane.astype(F32)
    used = jnp.sum(jnp.where(lane == N_BUCKETS - 1, tile_end, 0.0), axis=-1, keepdims=True)
    ti = jnp.minimum(tile, used - 1.0)
    end_col = jnp.transpose(jnp.broadcast_to(tile_end[0:1], (LANES, LANES)))
    start_col = jnp.transpose(jnp.broadcast_to(tile_start[0:1], (LANES, LANES)))
    ti_rows = jnp.broadcast_to(ti[0:1], (LANES, LANES))
    b_of = jnp.sum(jnp.where((row < N_BUCKETS) & (ti_rows >= end_col), 1.0, 0.0),
                   axis=0, keepdims=True)
    start_of = jnp.sum(jnp.where(row.astype(F32) == b_of, start_col, 0.0), axis=0, keepdims=True)
    grp = sum(jnp.where(b_of >= float(g * N_PAIRS), 1.0, 0.0) for g in range(1, MOE_GROUPS))
    pair = b_of - float(N_PAIRS) * grp
    ex_a = sum(jnp.where(pair == float(j), float(PAIR_A[j]), 0.0) for j in range(N_PAIRS))
    ex_b = sum(jnp.where(pair == float(j), float(PAIR_B[j]), 0.0) for j in range(N_PAIRS))
    first_expert = float(layer * N_EXPERTS) + float(MOE_EXPERTS) * grp
    first_lane = float(MOE_GROUPS) + float(MOE_EXPERTS) * grp
    end_of = jnp.sum(jnp.where(row.astype(F32) == b_of, end_col, 0.0), axis=0, keepdims=True)
    valid = jnp.where(tile[0:1] < used[0:1], 1.0, 0.0)
    first = valid * jnp.where(ti[0:1] == start_of, 1.0, 0.0)
    partial = jnp.where((tile[0:1] == end_of - 1.0) | (tile[0:1] >= used[0:1]), 1.0, 0.0)
    zero = jnp.where(tile[0:1] < float(ntile), partial, 0.0)
    e_a, e_b = first_expert + ex_a, first_expert + ex_b

    def at_next(v):
        v_col = jnp.transpose(jnp.broadcast_to(v, (LANES, LANES)))
        return jnp.sum(jnp.where(row.astype(F32) == end_of, v_col, 0.0), axis=0, keepdims=True)

    has_next = valid * jnp.where(end_of < used[0:1], 1.0, 0.0)
    rows = {S_BLK: ti[0:1], S_EA: e_a, S_EB: e_b, S_VALID: valid,
            S_FIRST: first, S_LANE_G: grp, S_LANE_A: first_lane + ex_a, S_LANE_B: first_lane + ex_b,
            S_ZERO: zero, S_NEA: at_next(e_a), S_NEB: at_next(e_b), S_HASNEXT: has_next}
    sub = lax.broadcasted_iota(I32, (SCHED_ROWS, LANES), 0)
    sched = sum(jnp.where(sub == r, jnp.broadcast_to(v, (SCHED_ROWS, LANES)), 0.0)
                for r, v in rows.items())
    sched_ref[...] = sched.astype(I32)


def _moe_plan(route_p, route_s, cnt, layer):
    nslot = route_p.shape[1] + route_s.shape[1]
    assert _num_tiles(nslot) <= LANES
    return pl.pallas_call(
        functools.partial(_plan_kernel, layer=layer, ntile=_num_tiles(nslot)),
        out_shape=[jax.ShapeDtypeStruct((1, nslot), I32),
                   jax.ShapeDtypeStruct((SCHED_ROWS, LANES), I32)],
        compiler_params=pltpu.CompilerParams(vmem_limit_bytes=VMEM_LIMIT),
        name="plan",
    )(route_p, route_s, cnt)


def _router_weights(w_coarse, b_coarse, w_fine, b_fine):
    wf = jnp.transpose(w_fine, (1, 0, 2)).reshape(D_MODEL, N_EXPERTS)
    w = jnp.concatenate([w_coarse, wf], axis=1)
    w = jnp.pad(w, ((0, 0), (0, LANES - w.shape[1])))
    b = jnp.concatenate([b_coarse, b_fine.reshape(N_EXPERTS)])
    b = jnp.pad(b, (0, LANES - b.shape[0])).reshape(1, LANES)
    hi_lo = _hi_lo(w)
    return hi_lo[0], hi_lo[1], b


def _prep_params(ln_g, ln_b, w_in_even, w_a2, b_a, gla_norm_g, conv_w, conv_b, conv_ln_g,
                 conv_ln_b, w_out_even, w_grp_pool, pool_scale, w_out_odd, w_coarse, b_coarse,
                 w_fine, b_fine):
    p = {}
    w_in = w_in_even[0]
    o_q, o_k, o_v, o_g = 0, GLA_KW, 2 * GLA_KW, 2 * GLA_KW + GLA_VW
    o_a = o_g + GLA_VW
    o_u = o_a + GLA_RANK
    w_in_r = jnp.concatenate(
        [w_in[:, o_q:o_a], w_in[:, o_u:o_u + 2 * CONV_CH], w_in[:, o_a:o_u],
         jnp.zeros((D_MODEL, A_PAD - GLA_RANK), F32)], axis=1)
    p["w_in"] = _hi_lo(w_in_r)
    p["w_a2"] = _hi_lo(jnp.pad(w_a2[0], ((0, A_PAD - GLA_RANK), (0, 0))))
    p["b_a"] = b_a[0].reshape(1, GLA_KW)
    p["gng"] = gla_norm_g[0].reshape(1, GLA_VW)
    p["conv_w"] = jnp.pad(conv_w[0], ((0, CONV_W_ROWS - CONV_WIDTH), (0, 0)))
    p["conv_b"] = conv_b[0].reshape(1, CONV_CH)
    p["conv_ln_g"] = conv_ln_g[0].reshape(1, CONV_CH)
    p["conv_ln_b"] = conv_ln_b[0].reshape(1, CONV_CH)
    p["w_out0"] = _hi_lo(w_out_even[0])
    p["w_grp"] = w_grp_pool[0].astype(BF16)
    p["pool_scale"] = pool_scale[0].reshape(1, D_MODEL)
    p["w_out1"] = w_out_odd[0].astype(BF16)
    for layer in range(DEPTH):
        for j in range(2):
            p[f"ln_g{layer}{j}"] = ln_g[layer, j].reshape(1, D_MODEL)
            p[f"ln_b{layer}{j}"] = ln_b[layer, j].reshape(1, D_MODEL)
        hi, lo, b = _router_weights(w_coarse[layer], b_coarse[layer], w_fine[layer], b_fine[layer])
        p[f"wr_hi{layer}"], p[f"wr_lo{layer}"], p[f"br{layer}"] = hi, lo, b
    return p


def kernel(x_prompt, x_sample, state_gla, state_conv, state_pool, c_prompt, c_sample, w_ada, b_ada,
           ln_g, ln_b, w_in_even, w_a2, b_a, gla_norm_g, conv_w, conv_b, conv_ln_g, conv_ln_b,
           w_out_even, w_grp_pool, pool_scale, w_out_odd, w_coarse, b_coarse, w_fine, b_fine,
           w_gate, w_up, w_down):
    bsz, seq, _ = x_prompt.shape
    nsmp = x_sample.shape[0]
    ntok_p = bsz * seq
    assert seq % TILE_L == 0 and nsmp <= TILE_L and nsmp % SAMPLE_BLK == 0 and ntok_p % nsmp == 0
    p = _prep_params(ln_g, ln_b, w_in_even, w_a2, b_a, gla_norm_g, conv_w, conv_b, conv_ln_g,
                     conv_ln_b, w_out_even, w_grp_pool, pool_scale, w_out_odd, w_coarse, b_coarse,
                     w_fine, b_fine)
    wg = w_gate.reshape(DEPTH * N_EXPERTS, D_MODEL, MOE_FF)
    wu = w_up.reshape(DEPTH * N_EXPERTS, D_MODEL, MOE_FF)
    wd = w_down.reshape(DEPTH * N_EXPERTS, MOE_FF, D_MODEL)

    c_all = jnp.concatenate([c_prompt, c_sample], axis=0)
    mod = _ada_call(c_all, w_ada.reshape(2 * DEPTH, D_MODEL, 3 * D_MODEL),
                    b_ada.reshape(2 * DEPTH, 1, 3 * D_MODEL))
    mod_p = mod[:, :bsz].reshape(2 * DEPTH, bsz, 1, 3 * D_MODEL)
    mod_s = mod[:, bsz:]

    def moe(h2c_all, route_p, route_s, cnt, layer):
        dest, sched = _moe_plan(route_p, route_s, cnt, layer)
        sorted_x = _dispatch_call(sched, dest, h2c_all)
        y_sorted = _moe_call(sched, sorted_x, p[f"wr_hi{layer}"], p[f"wr_lo{layer}"],
                             p[f"br{layer}"], wg, wu, wd)
        return dest[0, :ntok_p], dest[0, ntok_p:ntok_p + nsmp], y_sorted

    xs0 = x_sample.reshape(nsmp, D_MODEL)
    cnt0 = jnp.zeros((1, LANES), F32)
    x1s, h2c_s, route_s, cnt, gla_s, conv_s = _mixer0_sample_call(
        xs0, mod_s[0], mod_s[1], state_gla, state_conv, cnt0, p)
    x1p, h2c_all, route_p, cnt, conv_p, gla_p = _mixer0_prompt_call(
        x_prompt, mod_p[0], mod_p[1], cnt, h2c_s, p)
    dest_p, dest_s, y_sorted = moe(h2c_all, route_p, route_s, cnt, 0)
    x3s, h2c_s, route_s, cnt, pool_s = _mixer1_sample_call(
        x1s, dest_s, y_sorted, mod_s[1], p["ln_g01"], p["ln_b01"], mod_s[2], mod_s[3],
        state_pool, cnt0, p)
    x3p, h2c_all, route_p, cnt, pool_p = _mixer1_prompt_call(
        x1p, dest_p, y_sorted, mod_p[1], p["ln_g01"], p["ln_b01"], mod_p[2], mod_p[3], cnt,
        h2c_s, bsz, p)
    dest_p, dest_s, y_sorted = moe(h2c_all, route_p, route_s, cnt, 1)
    x4p = _post_moe_call(x3p, dest_p, y_sorted, mod_p[3], p["ln_g11"], p["ln_b11"], TILE_L,
                         seq // TILE_L).reshape(bsz, seq, D_MODEL)
    x4s = _post_moe_call(x3s, dest_s, y_sorted, mod_s[3][None], p["ln_g11"], p["ln_b11"], nsmp, 1)
    return (x4p, x4s.reshape(nsmp, 1, D_MODEL), gla_p, conv_p, pool_p, gla_s, conv_s, pool_s)
```

```python
import functools
import math

import jax
import jax.numpy as jnp
from jax import lax
from jax.experimental import pallas as pl
from jax.experimental.pallas import tpu as pltpu

F32 = jnp.float32
BF16 = jnp.bfloat16
I32 = jnp.int32

D_MODEL = 1024
GLA_HEADS = 4
GLA_DK = 64
GLA_DV = 128
GLA_KW = GLA_HEADS * GLA_DK
GLA_VW = GLA_HEADS * GLA_DV
GLA_RANK = 16
GLA_TAU = 16.0
CONV_CH = 512
CONV_WIDTH = 31
CONV_BUF = CONV_WIDTH - 1
POOL_WINDOWS = (2, 4, 8, 16)
POOL_GC = D_MODEL // len(POOL_WINDOWS)
POOL_BUF = max(POOL_WINDOWS) - 1
MOE_GROUPS = 4
MOE_EXPERTS = 4
N_EXPERTS = MOE_GROUPS * MOE_EXPERTS
N_PAIRS = MOE_EXPERTS * (MOE_EXPERTS - 1) // 2
N_BUCKETS = MOE_GROUPS * N_PAIRS
PAIR_A = (0, 0, 0, 1, 1, 2)
PAIR_B = (1, 2, 3, 2, 3, 3)
MOE_FF = 512
R_BUCKET, R_RANK = 0, 1
DEPTH = 2
ALPHA = (2 * DEPTH) ** 0.25
LN_EPS = 1e-5

LANES = 128
SUBLANES = 8
CHUNKS = D_MODEL // LANES
A_PAD = LANES
Z_WIDTH = 2 * GLA_KW + 2 * GLA_VW + 2 * CONV_CH + A_PAD
NEG_BIG = -1e30
VMEM_LIMIT = 56 * 1024 * 1024

TILE_L = 256
CONV_HALO = 32
CONV_W_ROWS = 32
POOL_HALO = 16
SAMPLE_BLK = 16
MOE_TILE = 256
DISPATCH_ROWS = 1280


def _dot(a, b):
    return jnp.dot(a, b, preferred_element_type=F32)


def _dot_nt(a, b):
    return lax.dot_general(a, b, (((1,), (1,)), ((), ())), preferred_element_type=F32)


def _split3(x):
    hi = x.astype(BF16)
    r1 = x - hi.astype(F32)
    mid = r1.astype(BF16)
    lo = (r1 - mid.astype(F32)).astype(BF16)
    return hi, mid, lo


def _split2(x):
    hi = x.astype(BF16)
    lo = (x - hi.astype(F32)).astype(BF16)
    return hi, lo


def _dot_w3(a, w_hi, w_lo):
    a_hi, a_lo = _split2(a)
    return _dot(a_hi, w_hi) + (_dot(a_lo, w_hi) + _dot(a_hi, w_lo))


def _lhs3(parts):
    hi, lo = parts
    return jnp.concatenate([hi, lo, hi], axis=1)


def _rhs3_rows(x):
    hi, lo = _split2(x)
    return jnp.concatenate([hi, hi, lo], axis=0)


def _rhs3_lanes(parts):
    hi, lo = parts
    return jnp.concatenate([hi, hi, lo], axis=1)


def _layer_norm(x, g, b):
    mu = jnp.mean(x, axis=-1, keepdims=True)
    xc = x - mu
    var = jnp.mean(xc * xc, axis=-1, keepdims=True)
    return xc * lax.rsqrt(var + LN_EPS) * g + b


def _standardize(x):
    mu = jnp.mean(x, axis=-1, keepdims=True)
    xc = x - mu
    var = jnp.mean(xc * xc, axis=-1, keepdims=True)
    return xc * lax.rsqrt(var + LN_EPS)


def _sigmoid(x):
    return 1.0 / (1.0 + jnp.exp(-x))


def _silu(x):
    return x * _sigmoid(x)


def _log_sigmoid(x):
    return jnp.minimum(x, 0.0) - jnp.log(1.0 + jnp.exp(-jnp.abs(x)))


def _mod3(mod):
    return mod[:, 0:D_MODEL], mod[:, D_MODEL:2 * D_MODEL], mod[:, 2 * D_MODEL:3 * D_MODEL]


def _full(shape):
    nd = len(shape)
    return pl.BlockSpec(shape, lambda *_: (0,) * nd)


def _const(shape):
    nd = len(shape)
    return pl.BlockSpec(shape, lambda *_: (0,) * nd, pipeline_mode=pl.Buffered(1))


def _hi_lo(w):
    bits = lax.bitcast_convert_type(w, jnp.uint32) & jnp.uint32(0xFFFF0000)
    hi = lax.bitcast_convert_type(bits, F32)
    part = lax.broadcasted_iota(I32, (2,) + w.shape, 0)
    return jnp.where(part == 0, hi[None], (w - hi)[None]).astype(BF16)


def _params(*sem):
    return pltpu.CompilerParams(dimension_semantics=sem, vmem_limit_bytes=VMEM_LIMIT)


def _ada_kernel(c_ref, w_ref, b_ref, o_ref):
    w_hi, w_lo = _split2(w_ref[0])
    o_ref[0] = _dot_w3(_silu(c_ref[...]), w_hi, w_lo) + b_ref[0]


def _ada_call(c_all, w_ada, b_ada):
    n = c_all.shape[0]
    ncomb = w_ada.shape[0]
    tn = D_MODEL
    return pl.pallas_call(
        _ada_kernel,
        grid=(ncomb, 3 * D_MODEL // tn),
        in_specs=[
            pl.BlockSpec((n, D_MODEL), lambda i, j: (0, 0)),
            pl.BlockSpec((1, D_MODEL, tn), lambda i, j: (i, 0, j)),
            pl.BlockSpec((1, 1, tn), lambda i, j: (i, 0, j)),
        ],
        out_specs=pl.BlockSpec((1, n, tn), lambda i, j: (i, 0, j)),
        out_shape=jax.ShapeDtypeStruct((ncomb, n, 3 * D_MODEL), F32),
        compiler_params=_params("arbitrary", "arbitrary"),
        name="ada",
    )(c_all, w_ada, b_ada)


def _route(logits, carry):
    rows = logits.shape[0]
    lane = lax.broadcasted_iota(I32, (rows, LANES), 1)
    lanef = lane.astype(F32)
    big = float(LANES)
    lc = jnp.where(lane < MOE_GROUPS, logits, NEG_BIG)
    mc = jnp.max(lc, axis=-1, keepdims=True)
    gidx = jnp.min(jnp.where(lc == mc, lanef, big), axis=-1, keepdims=True)
    lo = float(MOE_GROUPS) + gidx * float(MOE_EXPERTS)
    in_grp = (lanef >= lo) & (lanef < lo + float(MOE_EXPERTS))
    lf = jnp.where(in_grp, logits, NEG_BIG)
    m1 = jnp.max(lf, axis=-1, keepdims=True)
    i1 = jnp.min(jnp.where(lf == m1, lanef, big), axis=-1, keepdims=True)
    lf2 = jnp.where(lanef == i1, NEG_BIG, lf)
    m2 = jnp.max(lf2, axis=-1, keepdims=True)
    i2 = jnp.min(jnp.where(lf2 == m2, lanef, big), axis=-1, keepdims=True)
    ea = jnp.minimum(i1, i2) - lo
    eb = jnp.maximum(i1, i2) - lo
    bucket = gidx * float(N_PAIRS) + ea * (7.0 - ea) * 0.5 + eb - ea - 1.0
    onehot = jnp.where(lanef == bucket, 1.0, 0.0)
    r = lax.broadcasted_iota(I32, (rows, rows), 0)
    c = lax.broadcasted_iota(I32, (rows, rows), 1)
    before = _dot(jnp.where(r > c, 1.0, 0.0).astype(BF16), onehot.astype(BF16))
    rank = jnp.sum(onehot * (before + carry), axis=-1, keepdims=True)
    new_carry = carry + jnp.sum(onehot, axis=0, keepdims=True)
    cols = jnp.where(lane == R_BUCKET, bucket, 0.0) + jnp.where(lane == R_RANK, rank, 0.0)
    return jnp.transpose(cols)[0:SUBLANES, :], new_carry


def _store_chunked(ref, x):
    rows = x.shape[0]
    for c in range(CHUNKS):
        ref[pl.ds(c, rows, stride=CHUNKS), :] = x[:, c * LANES:(c + 1) * LANES]


def _load_chunked(ref, rows, group, first):
    return jnp.concatenate(
        [ref[pl.ds(first + c, rows, stride=group), :] for c in range(CHUNKS)], axis=1)


def _router_logits(h2, wr_hi_ref, wr_lo_ref, br_ref):
    h_hi, h_lo = _split2(h2)
    wr_hi = wr_hi_ref[...]
    return _dot(h_hi, wr_hi) + _dot(h_lo, wr_hi) + _dot(h_hi, wr_lo_ref[...]) + br_ref[...]


def _post_mixer(x, m, gate, lng_ref, lnb_ref, mod_b, wr_hi_ref, wr_lo_ref, br_ref, carry,
                x1_ref, h2c_ref, route_ref):
    x1 = _layer_norm(ALPHA * x + (1.0 + gate) * m, lng_ref[...], lnb_ref[...])
    shift2, scale2, _ = _mod3(mod_b)
    h2 = x1 * (1.0 + scale2) + shift2
    pad = route_ref.shape[1] - x.shape[0]
    if pad:
        h2 = jnp.concatenate([h2, jnp.zeros((pad, D_MODEL), F32)], axis=0)
    route, carry = _route(_router_logits(h2, wr_hi_ref, wr_lo_ref, br_ref), carry)
    x1_ref[...] = x1
    _store_chunked(h2c_ref, h2)
    route_ref[...] = route
    return carry


def _token_out_specs(rows, slot_rows, x1_map, slot_map):
    return [pl.BlockSpec((rows, D_MODEL), x1_map),
            pl.BlockSpec((slot_rows * CHUNKS, LANES), slot_map),
            pl.BlockSpec((SUBLANES, slot_rows), lambda *i: (0, x1_map(*i)[0]))]


def _token_out_shapes(ntok, nslot_h2, nslot_route):
    return [jax.ShapeDtypeStruct((ntok, D_MODEL), F32),
            jax.ShapeDtypeStruct((nslot_h2 * CHUNKS, LANES), F32),
            jax.ShapeDtypeStruct((SUBLANES, nslot_route), F32)]


def _hold_rows(b, s):
    n, w = b.shape
    if s >= SUBLANES:
        pieces = []
        for p in range(n // (2 * s)):
            r = p * 2 * s + s - 1
            pieces.append(jnp.broadcast_to(b[r:r + 1, :], (2 * s, w)))
        return pieces[0] if len(pieces) == 1 else jnp.concatenate(pieces, axis=0)
    b3 = b.reshape(n // SUBLANES, SUBLANES, w)
    sub = lax.broadcasted_iota(I32, b3.shape, 1)

    def bc(r):
        return jnp.broadcast_to(b3[:, r:r + 1, :], b3.shape)

    out = bc(s - 1)
    for p in range(1, SUBLANES // (2 * s)):
        out = jnp.where(sub >= p * 2 * s, bc(p * 2 * s + s - 1), out)
    return out.reshape(n, w)


def _gla_tile(q, k, v, la, s_prev):
    n = q.shape[0]
    row = lax.broadcasted_iota(I32, (n, n), 0)
    col = lax.broadcasted_iota(I32, (n, n), 1)
    tri = jnp.where(row >= col, 1.0, 0.0).astype(BF16)
    hi, mid, lo = _split3(la)
    b = _dot(tri, hi) + _dot(tri, mid) + _dot(tri, lo)
    b_last = b[n - 1:n, :]
    kl = k * jnp.exp(b_last - b)

    rowi = lax.broadcasted_iota(I32, (n, GLA_KW), 0)
    pair_lanes = [slice(p * LANES, (p + 1) * LANES) for p in range(GLA_HEADS // 2)]

    def swapped(x):
        return [pltpu.roll(x[:, sl], GLA_DK, axis=1).astype(BF16) for sl in pair_lanes]

    def level_operands(qf, kf):
        q_hi = qf.astype(BF16)
        k_hi, k_lo = _split2(kf)
        return q_hi, swapped(qf - q_hi.astype(F32)), k_hi, swapped(k_hi.astype(F32)), k_lo

    levels = []
    s = n // 2
    while s >= 1:
        m = _hold_rows(b, s)
        second = (rowi & s) != 0
        levels.append((2 * s, level_operands(q * jnp.exp(jnp.where(second, b - m, NEG_BIG)),
                                             k * jnp.exp(jnp.where(second, NEG_BIG, m - b)))))
        s //= 2
    levels.append((1, level_operands(q, k)))
    qe = _split2(q * jnp.exp(b))

    lane = lax.broadcasted_iota(I32, (n, LANES), 1)
    xor = row ^ col
    zero = jnp.zeros((), BF16)

    def head_part(parts, sl, hm):
        return tuple(jnp.where(hm, x[:, sl], zero) for x in parts)

    outs = []
    for h in range(GLA_HEADS):
        p = h // 2
        sl = pair_lanes[p]
        hm = (lane < GLA_DK) if h % 2 == 0 else (lane >= GLA_DK)
        att = None
        for span, (q_hi, q_lo_sw, k_hi, k_hi_sw, k_lo) in levels:
            lhs = jnp.concatenate([jnp.where(hm, q_hi[:, sl], q_lo_sw[p]),
                                   jnp.where(hm, q_hi[:, sl], zero)], axis=1)
            rhs = jnp.concatenate([jnp.where(hm, k_hi[:, sl], k_hi_sw[p]), k_lo[:, sl]], axis=1)
            a_s = _dot_nt(lhs, rhs)
            att = a_s if att is None else jnp.where(xor < span, a_s, att)
        o_h = _dot(_lhs3(_split2(att)), _rhs3_rows(v[:, h * GLA_DV:(h + 1) * GLA_DV]))
        o_h = o_h + _dot(_lhs3(head_part(qe, sl, hm)), _rhs3_rows(s_prev[sl, :]))
        outs.append(o_h)

    decay = jnp.exp(b_last)
    upper = lax.broadcasted_iota(I32, (LANES, LANES), 0) < GLA_DK
    s_new = []
    for p in range(GLA_HEADS // 2):
        sl = slice(p * LANES, (p + 1) * LANES)
        kl_t = jnp.transpose(kl[:, sl])
        u = _dot(_lhs3(_split2(kl_t)), _rhs3_rows(v[:, p * 2 * GLA_DV:(p + 1) * 2 * GLA_DV]))
        upd = jnp.where(upper, u[:, 0:GLA_DV], u[:, GLA_DV:2 * GLA_DV])
        dcol = jnp.transpose(jnp.broadcast_to(decay[:, sl], (LANES, LANES)))
        s_new.append(dcol * s_prev[sl, :] + upd)
    return outs, jnp.concatenate(s_new, axis=0)


def _split_projection(z):
    c0 = 0
    q = z[:, c0:c0 + GLA_KW] * (GLA_DK ** -0.5); c0 += GLA_KW
    k = z[:, c0:c0 + GLA_KW]; c0 += GLA_KW
    v = z[:, c0:c0 + GLA_VW]; c0 += GLA_VW
    g = z[:, c0:c0 + GLA_VW]; c0 += GLA_VW
    ua = z[:, c0:c0 + CONV_CH]; c0 += CONV_CH
    ug = z[:, c0:c0 + CONV_CH]; c0 += CONV_CH
    a_lr = z[:, c0:c0 + A_PAD]
    return q, k, v, g, ua, ug, a_lr


def _mix0_project(o_heads, g, y, gng_ref, w_out_ref):
    sl = slice(GLA_VW, GLA_VW + CONV_CH)
    m = _dot_w3(y, w_out_ref[0, sl, :], w_out_ref[1, sl, :])
    for hd in range(GLA_HEADS):
        sl = slice(hd * GLA_DV, (hd + 1) * GLA_DV)
        o_h = _standardize(o_heads[hd]) * gng_ref[:, sl] * _silu(g[:, sl])
        m = m + _dot_w3(o_h, w_out_ref[0, sl, :], w_out_ref[1, sl, :])
    return m


def _mix0_inputs(h, w_in_ref, w_a2_ref, b_a_ref):
    z = _dot_w3(h, w_in_ref[0], w_in_ref[1])
    q, k, v, g, ua, ug, a_lr = _split_projection(z)
    la = _log_sigmoid(_dot_w3(a_lr, w_a2_ref[0], w_a2_ref[1]) + b_a_ref[...]) * (1.0 / GLA_TAU)
    return q, k, v, g, ua, ug, la


def _prompt_grid_specs(bsz, nt):
    ntile = bsz * nt
    seq_map = lambda b, t: (jnp.minimum(b, bsz - 1), 0, 0)
    x_map = lambda b, t: (jnp.minimum(b, bsz - 1), t, 0)
    tok_map = lambda b, t: (jnp.minimum(b * nt + t, ntile - 1), 0)
    slot_map = lambda b, t: (jnp.minimum(b * nt + t, ntile), 0)
    state_map = lambda b, t: (0, jnp.minimum(b, bsz - 1), 0, 0)
    return seq_map, x_map, tok_map, slot_map, state_map


def _mixer0_prompt_kernel(x_ref, moda_ref, modb_ref, cnt_in_ref, h2cs_ref, w_in_ref, w_a2_ref,
                          b_a_ref, gng_ref,
                          cw_ref, cb_ref, clg_ref, clb_ref, w_out_ref, lng_ref, lnb_ref,
                          wr_hi_ref, wr_lo_ref, br_ref,
                          x1_ref, h2c_ref, route_ref, cnt_ref, nconv_ref, ngla_ref,
                          s_ref, cbuf_ref, shift_ref, carry_ref):
    nseq = pl.num_programs(0) - 1

    @pl.when(pl.program_id(0) < nseq)
    def _():
        _mixer0_prompt_tile(x_ref, moda_ref, modb_ref, cnt_in_ref, w_in_ref, w_a2_ref, b_a_ref,
                            gng_ref, cw_ref, cb_ref, clg_ref, clb_ref, w_out_ref, lng_ref, lnb_ref,
                            wr_hi_ref, wr_lo_ref, br_ref, x1_ref, h2c_ref, route_ref, cnt_ref,
                            nconv_ref, ngla_ref, s_ref, cbuf_ref, shift_ref, carry_ref)

    @pl.when((pl.program_id(0) == nseq) & (pl.program_id(1) == 0))
    def _():
        h2c_ref[...] = h2cs_ref[...]


def _mixer0_prompt_tile(x_ref, moda_ref, modb_ref, cnt_in_ref, w_in_ref, w_a2_ref, b_a_ref,
                        gng_ref, cw_ref, cb_ref, clg_ref, clb_ref, w_out_ref, lng_ref, lnb_ref,
                        wr_hi_ref, wr_lo_ref, br_ref, x1_ref, h2c_ref, route_ref, cnt_ref,
                        nconv_ref, ngla_ref, s_ref, cbuf_ref, shift_ref, carry_ref):
    t = pl.program_id(1)
    n = x_ref.shape[1]

    @pl.when((t == 0) & (pl.program_id(0) == 0))
    def _():
        carry_ref[...] = cnt_in_ref[...]

    @pl.when(t == 0)
    def _():
        s_ref[...] = jnp.zeros_like(s_ref)
        cbuf_ref[0:CONV_HALO, :] = jnp.zeros((CONV_HALO, CONV_CH), F32)

    x = x_ref[0]
    shift, scale, gate = _mod3(moda_ref[0])
    h = x * (1.0 + scale) + shift
    q, k, v, g, ua, ug, la = _mix0_inputs(h, w_in_ref, w_a2_ref, b_a_ref)
    o_heads, s_new = _gla_tile(q, k, v, la, s_ref[...])
    s_ref[...] = s_new

    glu = ua * _sigmoid(ug)
    cbuf_ref[CONV_HALO:CONV_HALO + n, :] = glu
    acc = jnp.broadcast_to(cb_ref[...], (n, CONV_CH))
    base = CONV_HALO - CONV_BUF
    for r in range(SUBLANES):
        taps = range(r, CONV_WIDTH, SUBLANES)
        span = n + (len(taps) - 1) * SUBLANES
        if (base + r) % SUBLANES == 0:
            src, off = cbuf_ref, base + r
        else:
            shift_ref[r, 0:span, :] = cbuf_ref[base + r:base + r + span, :]
            src, off = shift_ref.at[r], 0
        for a, j in enumerate(taps):
            lo = off + a * SUBLANES
            acc = acc + cw_ref[j:j + 1, :] * src[lo:lo + n, :]
    y = _silu(_layer_norm(acc, clg_ref[...], clb_ref[...]))

    @pl.when(t == pl.num_programs(1) - 1)
    def _():
        nconv_ref[0, 0] = cbuf_ref[CONV_HALO + n - CONV_BUF:CONV_HALO + n, :]
        ngla_ref[0, 0] = s_new.reshape(GLA_HEADS, GLA_DK, GLA_DV)

    cbuf_ref[0:CONV_HALO, :] = cbuf_ref[n:n + CONV_HALO, :]

    m = _mix0_project(o_heads, g, y, gng_ref, w_out_ref)
    carry = _post_mixer(x, m, gate, lng_ref, lnb_ref, modb_ref[0], wr_hi_ref, wr_lo_ref, br_ref,
                        carry_ref[...], x1_ref, h2c_ref, route_ref)
    carry_ref[...] = carry
    cnt_ref[...] = carry


def _mixer0_prompt_call(x, mod_a, mod_b, cnt, h2c_s, p):
    bsz, seq, _ = x.shape
    tl = TILE_L
    nt = seq // tl
    seq_map, x_map, tok, slot, state_map = _prompt_grid_specs(bsz, nt)
    return pl.pallas_call(
        _mixer0_prompt_kernel,
        grid=(bsz + 1, nt),
        in_specs=[
            pl.BlockSpec((1, tl, D_MODEL), x_map),
            pl.BlockSpec((1, 1, 3 * D_MODEL), seq_map),
            pl.BlockSpec((1, 1, 3 * D_MODEL), seq_map),
            _full((1, LANES)),
            _full((tl * CHUNKS, LANES)),
            _const((2, D_MODEL, Z_WIDTH)),
            _const((2, A_PAD, GLA_KW)),
            _full((1, GLA_KW)),
            _full((1, GLA_VW)),
            _full((CONV_W_ROWS, CONV_CH)),
            _full((1, CONV_CH)),
            _full((1, CONV_CH)),
            _full((1, CONV_CH)),
            _const((2, GLA_VW + CONV_CH, D_MODEL)),
            _full((1, D_MODEL)),
            _full((1, D_MODEL)),
            _full((D_MODEL, LANES)),
            _full((D_MODEL, LANES)),
            _full((1, LANES)),
        ],
        out_specs=_token_out_specs(tl, tl, tok, slot) + [
            _full((1, LANES)),
            pl.BlockSpec((1, 1, CONV_BUF, CONV_CH), state_map),
            pl.BlockSpec((1, 1, GLA_HEADS, GLA_DK, GLA_DV), lambda b, t: state_map(b, t) + (0,)),
        ],
        out_shape=_token_out_shapes(bsz * seq, bsz * seq + tl, bsz * seq) + [
            jax.ShapeDtypeStruct((1, LANES), F32),
            jax.ShapeDtypeStruct((1, bsz, CONV_BUF, CONV_CH), F32),
            jax.ShapeDtypeStruct((1, bsz, GLA_HEADS, GLA_DK, GLA_DV), F32),
        ],
        scratch_shapes=[
            pltpu.VMEM((GLA_KW, GLA_DV), F32),
            pltpu.VMEM((CONV_HALO + tl, CONV_CH), F32),
            pltpu.VMEM((SUBLANES, CONV_HALO + tl, CONV_CH), F32),
            pltpu.VMEM((1, LANES), F32),
        ],
        compiler_params=_params("arbitrary", "arbitrary"),
        name="mixer0_prompt",
    )(x, mod_a, mod_b, cnt, h2c_s, p["w_in"], p["w_a2"], p["b_a"], p["gng"], p["conv_w"], p["conv_b"],
      p["conv_ln_g"], p["conv_ln_b"], p["w_out0"], p["ln_g00"], p["ln_b00"],
      p["wr_hi0"], p["wr_lo0"], p["br0"])


def _mixer0_sample_kernel(x_ref, moda_ref, modb_ref, sgla_ref, sconv_ref, cnt_in_ref,
                          w_in_ref, w_a2_ref,
                          b_a_ref, gng_ref, cw_ref, cb_ref, clg_ref, clb_ref, w_out_ref, lng_ref,
                          lnb_ref, wr_hi_ref, wr_lo_ref, br_ref,
                          x1_ref, h2c_ref, route_ref, cnt_ref, ngla_ref, nconv_ref,
                          zt_ref, v_ref, g_ref, glu_ref, o_ref, y_ref):
    i = pl.program_id(0)
    nb = sgla_ref.shape[1]
    ntok = x_ref.shape[0]

    @pl.when(i == 0)
    def _():
        shift, scale, _ = _mod3(moda_ref[...])
        h = x_ref[...] * (1.0 + scale) + shift
        q, k, v, g, ua, ug, la = _mix0_inputs(h, w_in_ref, w_a2_ref, b_a_ref)
        v_ref[...] = v
        g_ref[...] = g
        glu_ref[...] = ua * _sigmoid(ug)
        for j, val in enumerate((jnp.exp(la), k, q)):
            hi, lo = _split2(jnp.transpose(val))
            zt_ref[(2 * j) * GLA_KW:(2 * j + 1) * GLA_KW, :] = hi
            zt_ref[(2 * j + 1) * GLA_KW:(2 * j + 2) * GLA_KW, :] = lo

    tok_row = lax.broadcasted_iota(I32, (ntok, LANES), 0)
    blk = pl.ds(pl.multiple_of(i * nb, nb), nb)
    v_blk = v_ref[blk, :]
    glu_blk = glu_ref[blk, :]
    o_rows, y_rows = [], []
    for n in range(nb):
        onehot = jnp.where(tok_row == i * nb + n, 1.0, 0.0).astype(BF16)
        cols = _dot(zt_ref[...], onehot)
        a_col = cols[0:GLA_KW] + cols[GLA_KW:2 * GLA_KW]
        k_col = cols[2 * GLA_KW:3 * GLA_KW] + cols[3 * GLA_KW:4 * GLA_KW]
        q_col = cols[4 * GLA_KW:5 * GLA_KW] + cols[5 * GLA_KW:6 * GLA_KW]
        v_row = v_blk[n:n + 1, :]
        v_b = jnp.concatenate(
            [jnp.broadcast_to(v_row[:, h * GLA_DV:(h + 1) * GLA_DV], (GLA_DK, GLA_DV))
             for h in range(GLA_HEADS)], axis=0)
        s_old = sgla_ref[0, n].reshape(GLA_KW, GLA_DV)
        s_new = a_col * s_old + k_col * v_b
        ngla_ref[0, n] = s_new.reshape(GLA_HEADS, GLA_DK, GLA_DV)
        o4 = jnp.sum((q_col * s_new).reshape(GLA_HEADS, GLA_DK, GLA_DV), axis=1)
        o_rows.append(jnp.concatenate([o4[h:h + 1, :] for h in range(GLA_HEADS)], axis=1))
        glu_row = glu_blk[n:n + 1, :]
        past = sconv_ref[0, n]
        y_rows.append(jnp.sum(past * cw_ref[0:CONV_BUF, :], axis=0, keepdims=True)
                      + glu_row * cw_ref[CONV_BUF:CONV_WIDTH, :] + cb_ref[...])
        nconv_ref[0, n, 0:CONV_BUF - 1, :] = sconv_ref[0, n, 1:CONV_BUF, :]
        nconv_ref[0, n, CONV_BUF - 1:CONV_BUF, :] = glu_row
    o_ref[blk, :] = jnp.concatenate(o_rows, axis=0)
    y_ref[blk, :] = jnp.concatenate(y_rows, axis=0)

    @pl.when(i == pl.num_programs(0) - 1)
    def _():
        y = _silu(_layer_norm(y_ref[...], clg_ref[...], clb_ref[...]))
        o_heads = [o_ref[:, hd * GLA_DV:(hd + 1) * GLA_DV] for hd in range(GLA_HEADS)]
        m = _mix0_project(o_heads, g_ref[...], y, gng_ref, w_out_ref)
        _, _, gate = _mod3(moda_ref[...])
        cnt_ref[...] = _post_mixer(x_ref[...], m, gate, lng_ref, lnb_ref, modb_ref[...], wr_hi_ref,
                                   wr_lo_ref, br_ref, cnt_in_ref[...], x1_ref, h2c_ref, route_ref)


def _mixer0_sample_call(x, mod_a, mod_b, state_gla, state_conv, cnt, p):
    ntok = x.shape[0]
    nb = SAMPLE_BLK
    tok = lambda i: (0, 0)
    return pl.pallas_call(
        _mixer0_sample_kernel,
        grid=(ntok // nb,),
        in_specs=[
            _full((ntok, D_MODEL)),
            _full((ntok, 3 * D_MODEL)),
            _full((ntok, 3 * D_MODEL)),
            pl.BlockSpec((1, nb, GLA_HEADS, GLA_DK, GLA_DV), lambda i: (0, i, 0, 0, 0)),
            pl.BlockSpec((1, nb, CONV_BUF, CONV_CH), lambda i: (0, i, 0, 0)),
            _full((1, LANES)),
            _const((2, D_MODEL, Z_WIDTH)),
            _const((2, A_PAD, GLA_KW)),
            _full((1, GLA_KW)),
            _full((1, GLA_VW)),
            _full((CONV_W_ROWS, CONV_CH)),
            _full((1, CONV_CH)),
            _full((1, CONV_CH)),
            _full((1, CONV_CH)),
            _const((2, GLA_VW + CONV_CH, D_MODEL)),
            _full((1, D_MODEL)),
            _full((1, D_MODEL)),
            _full((D_MODEL, LANES)),
            _full((D_MODEL, LANES)),
            _full((1, LANES)),
        ],
        out_specs=_token_out_specs(ntok, TILE_L, tok, tok) + [
            _full((1, LANES)),
            pl.BlockSpec((1, nb, GLA_HEADS, GLA_DK, GLA_DV), lambda i: (0, i, 0, 0, 0)),
            pl.BlockSpec((1, nb, CONV_BUF, CONV_CH), lambda i: (0, i, 0, 0)),
        ],
        out_shape=_token_out_shapes(ntok, TILE_L, TILE_L) + [
            jax.ShapeDtypeStruct((1, LANES), F32),
            jax.ShapeDtypeStruct(state_gla.shape, F32),
            jax.ShapeDtypeStruct(state_conv.shape, F32),
        ],
        scratch_shapes=[
            pltpu.VMEM((6 * GLA_KW, ntok), BF16),
            pltpu.VMEM((ntok, GLA_VW), F32),
            pltpu.VMEM((ntok, GLA_VW), F32),
            pltpu.VMEM((ntok, CONV_CH), F32),
            pltpu.VMEM((ntok, GLA_VW), F32),
            pltpu.VMEM((ntok, CONV_CH), F32),
        ],
        compiler_params=_params("arbitrary"),
        name="mixer0_sample",
    )(x, mod_a, mod_b, state_gla, state_conv, cnt, p["w_in"], p["w_a2"], p["b_a"],
      p["gng"], p["conv_w"], p["conv_b"], p["conv_ln_g"], p["conv_ln_b"], p["w_out0"],
      p["ln_g00"], p["ln_b00"], p["wr_hi0"], p["wr_lo0"], p["br0"])


def _pool_project(pooled, h, wg_ref, ps_ref, w_out_ref):
    m = None
    for gi in range(len(POOL_WINDOWS)):
        sl = slice(gi * POOL_GC, (gi + 1) * POOL_GC)
        mixed = _dot((pooled[:, sl] - h[:, sl]).astype(BF16), wg_ref[gi]) * ps_ref[:, sl]
        part = _dot(mixed.astype(BF16), w_out_ref[sl, :])
        m = part if m is None else m + part
    return m


def _block_input(xin_ref, m_ref, mod_m, lnpg_ref, lnpb_ref):
    m = _load_chunked(m_ref, xin_ref.shape[0], CHUNKS, 0)
    _, _, gate = _mod3(mod_m)
    return _layer_norm(ALPHA * xin_ref[...] + (1.0 + gate) * m, lnpg_ref[...], lnpb_ref[...])


def _mixer1_prompt_kernel(dest_ref, dnext_ref, xin_ref, y_any, modm_ref, lnpg_ref, lnpb_ref,
                          moda_ref, modb_ref, cnt_in_ref, h2cs_ref, wg_ref, ps_ref,
                          w_out_ref, lng_ref, lnb_ref, wr_hi_ref, wr_lo_ref, br_ref,
                          x1_ref, h2c_ref, route_ref, cnt_ref, npool_ref,
                          pbuf_ref, carry_ref, mbuf, sems):
    nseq = pl.num_programs(0) - 1
    nt = pl.num_programs(1)

    @pl.when(pl.program_id(0) < nseq)
    def _():
        m_ref, drain = _gather_pipelined(pl.program_id(0) * nt + pl.program_id(1), nseq * nt,
                                         dest_ref, dnext_ref, y_any, mbuf, sems)
        x = _block_input(xin_ref, m_ref, modm_ref[0], lnpg_ref, lnpb_ref)
        _mixer1_prompt_tile(x, moda_ref, modb_ref, cnt_in_ref, wg_ref, ps_ref, w_out_ref,
                            lng_ref, lnb_ref, wr_hi_ref, wr_lo_ref, br_ref, x1_ref, h2c_ref,
                            route_ref, cnt_ref, npool_ref, pbuf_ref, carry_ref)
        drain()

    @pl.when((pl.program_id(0) == nseq) & (pl.program_id(1) == 0))
    def _():
        h2c_ref[...] = h2cs_ref[...]


def _mixer1_prompt_tile(x, moda_ref, modb_ref, cnt_in_ref, wg_ref, ps_ref, w_out_ref,
                        lng_ref, lnb_ref, wr_hi_ref, wr_lo_ref, br_ref, x1_ref, h2c_ref,
                        route_ref, cnt_ref, npool_ref, pbuf_ref, carry_ref):
    t = pl.program_id(1)
    n = x.shape[0]

    @pl.when((t == 0) & (pl.program_id(0) == 0))
    def _():
        carry_ref[...] = cnt_in_ref[...]

    @pl.when(t == 0)
    def _():
        pbuf_ref[0:POOL_HALO, :] = jnp.zeros((POOL_HALO, D_MODEL), F32)

    shift, scale, gate = _mod3(moda_ref[0])
    h = x * (1.0 + scale) + shift
    pbuf_ref[POOL_HALO:POOL_HALO + n, :] = h

    cur = pbuf_ref[...]
    sums = []
    for gi, w in enumerate(POOL_WINDOWS):
        cur = cur + pltpu.roll(cur, w // 2, axis=0)
        sums.append(cur[POOL_HALO:, 0:POOL_GC])
        if gi + 1 < len(POOL_WINDOWS):
            cur = cur[:, POOL_GC:]
    pos = lax.broadcasted_iota(I32, (n, POOL_GC), 0) + t * n
    pooled = jnp.concatenate(
        [s / jnp.minimum(w, pos + 1).astype(F32) for s, w in zip(sums, POOL_WINDOWS)], axis=1)

    @pl.when(t == pl.num_programs(1) - 1)
    def _():
        npool_ref[0, 0] = pbuf_ref[POOL_HALO + n - POOL_BUF:POOL_HALO + n, :]

    pbuf_ref[0:POOL_HALO, :] = pbuf_ref[n:n + POOL_HALO, :]

    m = _pool_project(pooled, h, wg_ref, ps_ref, w_out_ref)
    carry = _post_mixer(x, m, gate, lng_ref, lnb_ref, modb_ref[0], wr_hi_ref, wr_lo_ref, br_ref,
                        carry_ref[...], x1_ref, h2c_ref, route_ref)
    carry_ref[...] = carry
    cnt_ref[...] = carry


def _mixer1_prompt_call(x1_prev, dest, y_sorted, mod_m, lnpg, lnpb, mod_a, mod_b, cnt, h2c_s, bsz, p):
    seq = x1_prev.shape[0] // bsz
    tl = TILE_L
    nt = seq // tl
    ntile = bsz * nt
    seq_map, _, tok, slot, state_map = _prompt_grid_specs(bsz, nt)
    ng = len(POOL_WINDOWS)
    return pl.pallas_call(
        _mixer1_prompt_kernel,
        grid=(bsz + 1, nt),
        in_specs=[
            pl.BlockSpec((1, 1, tl), lambda b, t: tok(b, t) + (0,), memory_space=pltpu.SMEM),
            pl.BlockSpec((1, 1, tl), lambda b, t: (jnp.minimum(b * nt + t + 1, ntile - 1), 0, 0),
                         memory_space=pltpu.SMEM),
            pl.BlockSpec((tl, D_MODEL), tok),
            pl.BlockSpec(memory_space=pl.ANY),
            pl.BlockSpec((1, 1, 3 * D_MODEL), seq_map),
            _full((1, D_MODEL)),
            _full((1, D_MODEL)),
            pl.BlockSpec((1, 1, 3 * D_MODEL), seq_map),
            pl.BlockSpec((1, 1, 3 * D_MODEL), seq_map),
            _full((1, LANES)),
            _full((tl * CHUNKS, LANES)),
            _full((ng, POOL_GC, POOL_GC)),
            _full((1, D_MODEL)),
            _full((D_MODEL, D_MODEL)),
            _full((1, D_MODEL)),
            _full((1, D_MODEL)),
            _full((D_MODEL, LANES)),
            _full((D_MODEL, LANES)),
            _full((1, LANES)),
        ],
        out_specs=_token_out_specs(tl, tl, tok, slot) + [
            _full((1, LANES)),
            pl.BlockSpec((1, 1, POOL_BUF, D_MODEL), state_map),
        ],
        out_shape=_token_out_shapes(bsz * seq, bsz * seq + tl, bsz * seq) + [
            jax.ShapeDtypeStruct((1, LANES), F32),
            jax.ShapeDtypeStruct((1, bsz, POOL_BUF, D_MODEL), F32),
        ],
        scratch_shapes=[pltpu.VMEM((POOL_HALO + tl, D_MODEL), F32), pltpu.VMEM((1, LANES), F32),
                        pltpu.VMEM((2, tl * CHUNKS, LANES), F32), pltpu.SemaphoreType.DMA((2,))],
        compiler_params=_params("arbitrary", "arbitrary"),
        name="mixer1_prompt",
    )(dest.reshape(ntile, 1, tl), dest.reshape(ntile, 1, tl), x1_prev, y_sorted, mod_m, lnpg, lnpb,
      mod_a, mod_b, cnt, h2c_s,
      p["w_grp"], p["pool_scale"], p["w_out1"], p["ln_g10"], p["ln_b10"],
      p["wr_hi1"], p["wr_lo1"], p["br1"])


def _mixer1_sample_kernel(dest_ref, xin_ref, y_any, modm_ref, lnpg_ref, lnpb_ref, moda_ref,
                          modb_ref, spool_ref, cnt_in_ref, wg_ref, ps_ref, w_out_ref,
                          lng_ref, lnb_ref, wr_hi_ref, wr_lo_ref, br_ref,
                          x1_ref, h2c_ref, route_ref, cnt_ref, npool_ref,
                          x_ref, h_ref, pooled_ref, mbuf, sem):
    i = pl.program_id(0)
    nb = spool_ref.shape[1]

    @pl.when(i == 0)
    def _():
        _start_row_gather(dest_ref, y_any, mbuf, sem)
        _wait_row_gather(y_any, mbuf, sem)
        x_ref[...] = _block_input(xin_ref, mbuf, modm_ref[...], lnpg_ref, lnpb_ref)
        shift, scale, _ = _mod3(moda_ref[...])
        h_ref[...] = x_ref[...] * (1.0 + scale) + shift

    lane = lax.broadcasted_iota(I32, (POOL_BUF, D_MODEL), 1)
    rowi = lax.broadcasted_iota(I32, (POOL_BUF, D_MODEL), 0)
    first = jnp.zeros((POOL_BUF, D_MODEL), I32)
    lane1 = lax.broadcasted_iota(I32, (1, D_MODEL), 1)
    inv_w = jnp.zeros((1, D_MODEL), F32)
    for gi, w in enumerate(POOL_WINDOWS):
        in_g = (lane >= gi * POOL_GC) & (lane < (gi + 1) * POOL_GC)
        first = jnp.where(in_g, POOL_BUF - (w - 1), first)
        in_g1 = (lane1 >= gi * POOL_GC) & (lane1 < (gi + 1) * POOL_GC)
        inv_w = jnp.where(in_g1, 1.0 / w, inv_w)
    keep = rowi >= first

    blk = pl.ds(pl.multiple_of(i * nb, nb), nb)
    h_blk = h_ref[blk, :]
    rows = []
    for n in range(nb):
        h_row = h_blk[n:n + 1, :]
        past = spool_ref[0, n]
        total = jnp.sum(jnp.where(keep, past, 0.0), axis=0, keepdims=True) + h_row
        rows.append(total * inv_w)
        npool_ref[0, n, 0:POOL_BUF - 1, :] = spool_ref[0, n, 1:POOL_BUF, :]
        npool_ref[0, n, POOL_BUF - 1:POOL_BUF, :] = h_row
    pooled_ref[blk, :] = jnp.concatenate(rows, axis=0)

    @pl.when(i == pl.num_programs(0) - 1)
    def _():
        m = _pool_project(pooled_ref[...], h_ref[...], wg_ref, ps_ref, w_out_ref)
        _, _, gate = _mod3(moda_ref[...])
        cnt_ref[...] = _post_mixer(x_ref[...], m, gate, lng_ref, lnb_ref, modb_ref[...], wr_hi_ref,
                                   wr_lo_ref, br_ref, cnt_in_ref[...], x1_ref, h2c_ref, route_ref)


def _mixer1_sample_call(x1_prev, dest, y_sorted, mod_m, lnpg, lnpb, mod_a, mod_b, state_pool,
                        cnt, p):
    ntok = x1_prev.shape[0]
    nb = SAMPLE_BLK
    ng = len(POOL_WINDOWS)
    tok = lambda i: (0, 0)
    return pl.pallas_call(
        _mixer1_sample_kernel,
        grid=(ntok // nb,),
        in_specs=[
            pl.BlockSpec((1, 1, ntok), lambda i: (0, 0, 0), memory_space=pltpu.SMEM),
            _full((ntok, D_MODEL)),
            pl.BlockSpec(memory_space=pl.ANY),
            _full((ntok, 3 * D_MODEL)),
            _full((1, D_MODEL)),
            _full((1, D_MODEL)),
            _full((ntok, 3 * D_MODEL)),
            _full((ntok, 3 * D_MODEL)),
            pl.BlockSpec((1, nb, POOL_BUF, D_MODEL), lambda i: (0, i, 0, 0)),
            _full((1, LANES)),
            _full((ng, POOL_GC, POOL_GC)),
            _full((1, D_MODEL)),
            _full((D_MODEL, D_MODEL)),
            _full((1, D_MODEL)),
            _full((1, D_MODEL)),
            _full((D_MODEL, LANES)),
            _full((D_MODEL, LANES)),
            _full((1, LANES)),
        ],
        out_specs=_token_out_specs(ntok, TILE_L, tok, tok) + [
            _full((1, LANES)),
            pl.BlockSpec((1, nb, POOL_BUF, D_MODEL), lambda i: (0, i, 0, 0)),
        ],
        out_shape=_token_out_shapes(ntok, TILE_L, TILE_L) + [
            jax.ShapeDtypeStruct((1, LANES), F32),
            jax.ShapeDtypeStruct(state_pool.shape, F32),
        ],
        scratch_shapes=[pltpu.VMEM((ntok, D_MODEL), F32)] * 3 + [
            pltpu.VMEM((ntok * CHUNKS, LANES), F32), pltpu.SemaphoreType.DMA],
        compiler_params=_params("arbitrary"),
        name="mixer1_sample",
    )(dest.reshape(1, 1, ntok), x1_prev, y_sorted, mod_m, lnpg, lnpb, mod_a, mod_b, state_pool, cnt,
      p["w_grp"], p["pool_scale"], p["w_out1"],
      p["ln_g10"], p["ln_b10"], p["wr_hi1"], p["wr_lo1"], p["br1"])


def _dispatch_kernel(sched_ref, dest_ref, src_ref, dst_any, zeros_ref, sem, zsem):
    n = dest_ref.shape[2]
    tile_rows = zeros_ref.shape[0]

    @pl.when(pl.program_id(0) == 0)
    def _():
        zeros_ref[...] = jnp.zeros_like(zeros_ref)

        def fill(i):
            return pltpu.make_async_copy(
                zeros_ref, dst_any.at[pl.ds(i * tile_rows, tile_rows)], zsem)

        for i in range(dst_any.shape[0] // tile_rows):
            pl.when(sched_ref[S_ZERO, i] == 1)(lambda i=i: fill(i).start())
        for i in range(dst_any.shape[0] // tile_rows):
            pl.when(sched_ref[S_ZERO, i] == 1)(lambda i=i: fill(i).wait())

    for j in range(n):
        d = dest_ref[0, 0, j]
        pltpu.make_async_copy(
            src_ref.at[pl.ds(j * CHUNKS, CHUNKS)],
            dst_any.at[pl.ds(pl.multiple_of(d * CHUNKS, CHUNKS), CHUNKS)], sem).start()
    pltpu.make_async_copy(src_ref, dst_any.at[pl.ds(0, n * CHUNKS)], sem).wait()


def _dispatch_call(sched, dest, h2c_all):
    nslot = dest.shape[1]
    rows = DISPATCH_ROWS
    tile_rows = MOE_TILE * CHUNKS
    grid_spec = pltpu.PrefetchScalarGridSpec(
        num_scalar_prefetch=1,
        grid=(nslot // rows,),
        in_specs=[
            pl.BlockSpec((1, 1, rows), lambda i, s: (i, 0, 0), memory_space=pltpu.SMEM),
            pl.BlockSpec((rows * CHUNKS, LANES), lambda i, s: (i, 0)),
        ],
        out_specs=pl.BlockSpec(memory_space=pl.ANY),
        scratch_shapes=[pltpu.VMEM((tile_rows, LANES), F32), pltpu.SemaphoreType.DMA,
                        pltpu.SemaphoreType.DMA],
    )
    return pl.pallas_call(
        _dispatch_kernel,
        grid_spec=grid_spec,
        out_shape=jax.ShapeDtypeStruct((_num_tiles(nslot) * tile_rows, LANES), F32),
        compiler_params=_params("arbitrary"),
        name="dispatch",
    )(sched, dest.reshape(nslot // rows, 1, rows), h2c_all)


(S_BLK, S_EA, S_EB, S_VALID, S_FIRST, S_LANE_G, S_LANE_A, S_LANE_B, S_ZERO,
 S_NEA, S_NEB, S_HASNEXT) = range(12)
SCHED_ROWS = 2 * SUBLANES


def _moe_kernel(sched_ref, x_ref, wr_hi_ref, wr_lo_ref, br_ref, wg_any, wu_any, wd_any,
                y_ref, wgu_ref, wd_ref, wbuf_gu, wbuf_d, wsem, wslot_ref):
    i = pl.program_id(0)
    tm = x_ref.shape[0] // CHUNKS

    def weight_copies(ea, eb, slot):
        srcs = ((wg_any, ea), (wu_any, ea), (wg_any, eb), (wu_any, eb))
        cps = [pltpu.make_async_copy(w.at[e], wbuf_gu.at[slot, j], wsem.at[slot])
               for j, (w, e) in enumerate(srcs)]
        return cps + [pltpu.make_async_copy(wd_any.at[e], wbuf_d.at[slot, j], wsem.at[slot])
                      for j, e in enumerate((ea, eb))]

    @pl.when(i == 0)
    def _():
        wslot_ref[0] = 0
        for cp in weight_copies(sched_ref[S_EA, 0], sched_ref[S_EB, 0], 0):
            cp.start()

    @pl.when(sched_ref[S_FIRST, i] == 1)
    def _():
        slot = wslot_ref[0]
        for cp in weight_copies(sched_ref[S_EA, i], sched_ref[S_EB, i], slot):
            cp.wait()

        @pl.when(sched_ref[S_HASNEXT, i] == 1)
        def _():
            for cp in weight_copies(sched_ref[S_NEA, i], sched_ref[S_NEB, i], 1 - slot):
                cp.start()

        for e in range(2):
            wgu_ref[2 * e] = wbuf_gu[slot, 2 * e].astype(BF16)
            wgu_ref[2 * e + 1] = wbuf_gu[slot, 2 * e + 1].astype(BF16)
            wd_ref[e * MOE_FF:(e + 1) * MOE_FF, :] = wbuf_d[slot, e].astype(BF16)
        wslot_ref[0] = 1 - slot

    @pl.when(sched_ref[S_VALID, i] == 0)
    def _():
        y_ref[...] = jnp.zeros_like(y_ref)

    @pl.when(sched_ref[S_VALID, i] == 1)
    def _():
        x = _load_chunked(x_ref, tm, CHUNKS, 0)
        logits = _router_logits(x, wr_hi_ref, wr_lo_ref, br_ref)
        lane = lax.broadcasted_iota(I32, logits.shape, 1)

        def pick(row):
            return jnp.sum(jnp.where(lane == sched_ref[row, i], logits, 0.0), axis=-1, keepdims=True)

        l_g, l_a, l_b = pick(S_LANE_G), pick(S_LANE_A), pick(S_LANE_B)
        p_g = 1.0 / jnp.sum(jnp.where(lane < MOE_GROUPS, jnp.exp(logits - l_g), 0.0),
                            axis=-1, keepdims=True)
        w_ab = (p_g / (1.0 + jnp.exp(l_b - l_a)), p_g / (1.0 + jnp.exp(l_a - l_b)))
        xb = x.astype(BF16)
        hid = [(_silu(_dot(xb, wgu_ref[2 * e])) * _dot(xb, wgu_ref[2 * e + 1]) * w_ab[e]).astype(BF16)
               for e in range(2)]
        _store_chunked(y_ref, _dot(jnp.concatenate(hid, axis=1), wd_ref[...]))


def _moe_call(sched, sorted_x, wr_hi, wr_lo, br, wg, wu, wd):
    tm = MOE_TILE
    ntile = sorted_x.shape[0] // (tm * CHUNKS)
    const2 = lambda i, s: (0, 0)
    anyspec = pl.BlockSpec(memory_space=pl.ANY)
    grid_spec = pltpu.PrefetchScalarGridSpec(
        num_scalar_prefetch=1,
        grid=(ntile,),
        in_specs=[
            pl.BlockSpec((tm * CHUNKS, LANES), lambda i, s: (s[S_BLK, i], 0)),
            pl.BlockSpec((D_MODEL, LANES), const2),
            pl.BlockSpec((D_MODEL, LANES), const2),
            pl.BlockSpec((1, LANES), const2),
            anyspec, anyspec, anyspec,
        ],
        out_specs=pl.BlockSpec((tm * CHUNKS, LANES), lambda i, s: (i, 0)),
        scratch_shapes=[
            pltpu.VMEM((4, D_MODEL, MOE_FF), BF16),
            pltpu.VMEM((2 * MOE_FF, D_MODEL), BF16),
            pltpu.VMEM((2, 4, D_MODEL, MOE_FF), F32),
            pltpu.VMEM((2, 2, MOE_FF, D_MODEL), F32),
            pltpu.SemaphoreType.DMA((2,)),
            pltpu.SMEM((1,), I32),
        ],
    )
    return pl.pallas_call(
        _moe_kernel,
        grid_spec=grid_spec,
        out_shape=jax.ShapeDtypeStruct((ntile * tm * CHUNKS, LANES), F32),
        compiler_params=_params("arbitrary"),
        name="moe",
    )(sched, sorted_x, wr_hi, wr_lo, br, wg, wu, wd)


def _start_row_gather(dest_ref, y_any, buf, sem):
    for j in range(dest_ref.shape[2]):
        d = dest_ref[0, 0, j]
        pltpu.make_async_copy(
            y_any.at[pl.ds(pl.multiple_of(d * CHUNKS, CHUNKS), CHUNKS)],
            buf.at[pl.ds(j * CHUNKS, CHUNKS)], sem).start()


def _wait_row_gather(y_any, buf, sem):
    pltpu.make_async_copy(y_any.at[pl.ds(0, buf.shape[0])], buf, sem).wait()


def _gather_pipelined(step, nstep, dest_ref, dnext_ref, y_any, mbuf, sems):
    slot = step % 2

    @pl.when(step == 0)
    def _():
        _start_row_gather(dest_ref, y_any, mbuf.at[0], sems.at[0])

    _start_row_gather(dnext_ref, y_any, mbuf.at[1 - slot], sems.at[1 - slot])
    _wait_row_gather(y_any, mbuf.at[slot], sems.at[slot])

    def drain():
        @pl.when(step == nstep - 1)
        def _():
            _wait_row_gather(y_any, mbuf.at[1 - slot], sems.at[1 - slot])

    return mbuf.at[slot], drain


def _post_moe_kernel(dest_ref, dnext_ref, x1_ref, y_any, mod_ref, lng_ref, lnb_ref, out_ref,
                     mbuf, sems):
    m_ref, drain = _gather_pipelined(pl.program_id(0), pl.num_programs(0), dest_ref, dnext_ref,
                                     y_any, mbuf, sems)
    out_ref[...] = _block_input(x1_ref, m_ref, mod_ref[0], lng_ref, lnb_ref)
    drain()


def _post_moe_call(x1, dest, y_sorted, mod, lng, lnb, rows, steps_per_mod):
    mrows = mod.shape[1]
    nstep = x1.shape[0] // rows
    dest3 = dest.reshape(nstep, 1, rows)
    return pl.pallas_call(
        _post_moe_kernel,
        grid=(nstep,),
        in_specs=[
            pl.BlockSpec((1, 1, rows), lambda i: (i, 0, 0), memory_space=pltpu.SMEM),
            pl.BlockSpec((1, 1, rows), lambda i: (jnp.minimum(i + 1, nstep - 1), 0, 0),
                         memory_space=pltpu.SMEM),
            pl.BlockSpec((rows, D_MODEL), lambda i: (i, 0)),
            pl.BlockSpec(memory_space=pl.ANY),
            pl.BlockSpec((1, mrows, 3 * D_MODEL), lambda i: (i // steps_per_mod, 0, 0)),
            _full((1, D_MODEL)),
            _full((1, D_MODEL)),
        ],
        out_specs=pl.BlockSpec((rows, D_MODEL), lambda i: (i, 0)),
        out_shape=jax.ShapeDtypeStruct(x1.shape, F32),
        scratch_shapes=[pltpu.VMEM((2, rows * CHUNKS, LANES), F32), pltpu.SemaphoreType.DMA((2,))],
        compiler_params=_params("arbitrary"),
        name="post_moe",
    )(dest3, dest3, x1, y_sorted, mod, lng, lnb)


PLAN_CHUNK = 2048


def _num_tiles(nslot):
    return -(-nslot // MOE_TILE) + N_BUCKETS


def _plan_kernel(rp_ref, rs_ref, cnt_ref, dest_ref, sched_ref, *, layer, ntile):
    tm = MOE_TILE
    lane = lax.broadcasted_iota(I32, (SUBLANES, LANES), 1)
    row = lax.broadcasted_iota(I32, (LANES, LANES), 0)
    col = lax.broadcasted_iota(I32, (LANES, LANES), 1)
    counts = jnp.where(lane < N_BUCKETS, jnp.broadcast_to(cnt_ref[...], (SUBLANES, LANES)), 0.0)
    tiles_b = jnp.floor((counts + float(tm - 1)) * (1.0 / tm))
    tile_end = _dot(tiles_b.astype(BF16), jnp.where(row <= col, 1.0, 0.0).astype(BF16))
    tile_start = tile_end - tiles_b
    start_b = tile_start.astype(BF16)

    def dest_of(route):
        n = route.shape[1]
        bucket, rank = route[R_BUCKET:R_BUCKET + 1, :], route[R_RANK:R_RANK + 1, :]
        b_iota = lax.broadcasted_iota(I32, (LANES, n), 0).astype(F32)
        onehot = jnp.where(b_iota == bucket, 1.0, 0.0).astype(BF16)
        start = _dot(start_b, onehot)[0:1, :]
        return (start * float(tm) + rank).astype(I32)

    ntok_p = rp_ref.shape[1]
    chunk = math.gcd(ntok_p, PLAN_CHUNK)
    for c0 in range(0, ntok_p, chunk):
        dest_ref[:, c0:c0 + chunk] = dest_of(rp_ref[:, c0:c0 + chunk])
    dest_ref[:, ntok_p:] = dest_of(rs_ref[...])

    tile = lane.astype(F32)
    used = jnp.sum(jnp.where(lane == N_BUCKETS - 1, tile_end, 0.0), axis=-1, keepdims=True)
    ti = jnp.minimum(tile, used - 1.0)
    end_col = jnp.transpose(jnp.broadcast_to(tile_end[0:1], (LANES, LANES)))
    start_col = jnp.transpose(jnp.broadcast_to(tile_start[0:1], (LANES, LANES)))
    ti_rows = jnp.broadcast_to(ti[0:1], (LANES, LANES))
    b_of = jnp.sum(jnp.where((row < N_BUCKETS) & (ti_rows >= end_col), 1.0, 0.0),
                   axis=0, keepdims=True)
    start_of = jnp.sum(jnp.where(row.astype(F32) == b_of, start_col, 0.0), axis=0, keepdims=True)
    grp = sum(jnp.where(b_of >= float(g * N_PAIRS), 1.0, 0.0) for g in range(1, MOE_GROUPS))
    pair = b_of - float(N_PAIRS) * grp
    ex_a = sum(jnp.where(pair == float(j), float(PAIR_A[j]), 0.0) for j in range(N_PAIRS))
    ex_b = sum(jnp.where(pair == float(j), float(PAIR_B[j]), 0.0) for j in range(N_PAIRS))
    first_expert = float(layer * N_EXPERTS) + float(MOE_EXPERTS) * grp
    first_lane = float(MOE_GROUPS) + float(MOE_EXPERTS) * grp
    end_of = jnp.sum(jnp.where(row.astype(F32) == b_of, end_col, 0.0), axis=0, keepdims=True)
    valid = jnp.where(tile[0:1] < used[0:1], 1.0, 0.0)
    first = valid * jnp.where(ti[0:1] == start_of, 1.0, 0.0)
    partial = jnp.where((tile[0:1] == end_of - 1.0) | (tile[0:1] >= used[0:1]), 1.0, 0.0)
    zero = jnp.where(tile[0:1] < float(ntile), partial, 0.0)
    e_a, e_b = first_expert + ex_a, first_expert + ex_b

    def at_next(v):
        v_col = jnp.transpose(jnp.broadcast_to(v, (LANES, LANES)))
        return jnp.sum(jnp.where(row.astype(F32) == end_of, v_col, 0.0), axis=0, keepdims=True)

    has_next = valid * jnp.where(end_of < used[0:1], 1.0, 0.0)
    rows = {S_BLK: ti[0:1], S_EA: e_a, S_EB: e_b, S_VALID: valid,
            S_FIRST: first, S_LANE_G: grp, S_LANE_A: first_lane + ex_a, S_LANE_B: first_lane + ex_b,
            S_ZERO: zero, S_NEA: at_next(e_a), S_NEB: at_next(e_b), S_HASNEXT: has_next}
    sub = lax.broadcasted_iota(I32, (SCHED_ROWS, LANES), 0)
    sched = sum(jnp.where(sub == r, jnp.broadcast_to(v, (SCHED_ROWS, LANES)), 0.0)
                for r, v in rows.items())
    sched_ref[...] = sched.astype(I32)


def _moe_plan(route_p, route_s, cnt, layer):
    nslot = route_p.shape[1] + route_s.shape[1]
    assert _num_tiles(nslot) <= LANES
    return pl.pallas_call(
        functools.partial(_plan_kernel, layer=layer, ntile=_num_tiles(nslot)),
        out_shape=[jax.ShapeDtypeStruct((1, nslot), I32),
                   jax.ShapeDtypeStruct((SCHED_ROWS, LANES), I32)],
        compiler_params=pltpu.CompilerParams(vmem_limit_bytes=VMEM_LIMIT),
        name="plan",
    )(route_p, route_s, cnt)


def _router_weights(w_coarse, b_coarse, w_fine, b_fine):
    wf = jnp.transpose(w_fine, (1, 0, 2)).reshape(D_MODEL, N_EXPERTS)
    w = jnp.concatenate([w_coarse, wf], axis=1)
    w = jnp.pad(w, ((0, 0), (0, LANES - w.shape[1])))
    b = jnp.concatenate([b_coarse, b_fine.reshape(N_EXPERTS)])
    b = jnp.pad(b, (0, LANES - b.shape[0])).reshape(1, LANES)
    hi_lo = _hi_lo(w)
    return hi_lo[0], hi_lo[1], b


def _prep_params(ln_g, ln_b, w_in_even, w_a2, b_a, gla_norm_g, conv_w, conv_b, conv_ln_g,
                 conv_ln_b, w_out_even, w_grp_pool, pool_scale, w_out_odd, w_coarse, b_coarse,
                 w_fine, b_fine):
    p = {}
    w_in = w_in_even[0]
    o_q, o_k, o_v, o_g = 0, GLA_KW, 2 * GLA_KW, 2 * GLA_KW + GLA_VW
    o_a = o_g + GLA_VW
    o_u = o_a + GLA_RANK
    w_in_r = jnp.concatenate(
        [w_in[:, o_q:o_a], w_in[:, o_u:o_u + 2 * CONV_CH], w_in[:, o_a:o_u],
         jnp.zeros((D_MODEL, A_PAD - GLA_RANK), F32)], axis=1)
    p["w_in"] = _hi_lo(w_in_r)
    p["w_a2"] = _hi_lo(jnp.pad(w_a2[0], ((0, A_PAD - GLA_RANK), (0, 0))))
    p["b_a"] = b_a[0].reshape(1, GLA_KW)
    p["gng"] = gla_norm_g[0].reshape(1, GLA_VW)
    p["conv_w"] = jnp.pad(conv_w[0], ((0, CONV_W_ROWS - CONV_WIDTH), (0, 0)))
    p["conv_b"] = conv_b[0].reshape(1, CONV_CH)
    p["conv_ln_g"] = conv_ln_g[0].reshape(1, CONV_CH)
    p["conv_ln_b"] = conv_ln_b[0].reshape(1, CONV_CH)
    p["w_out0"] = _hi_lo(w_out_even[0])
    p["w_grp"] = w_grp_pool[0].astype(BF16)
    p["pool_scale"] = pool_scale[0].reshape(1, D_MODEL)
    p["w_out1"] = w_out_odd[0].astype(BF16)
    for layer in range(DEPTH):
        for j in range(2):
            p[f"ln_g{layer}{j}"] = ln_g[layer, j].reshape(1, D_MODEL)
            p[f"ln_b{layer}{j}"] = ln_b[layer, j].reshape(1, D_MODEL)
        hi, lo, b = _router_weights(w_coarse[layer], b_coarse[layer], w_fine[layer], b_fine[layer])
        p[f"wr_hi{layer}"], p[f"wr_lo{layer}"], p[f"br{layer}"] = hi, lo, b
    return p


def kernel(x_prompt, x_sample, state_gla, state_conv, state_pool, c_prompt, c_sample, w_ada, b_ada,
           ln_g, ln_b, w_in_even, w_a2, b_a, gla_norm_g, conv_w, conv_b, conv_ln_g, conv_ln_b,
           w_out_even, w_grp_pool, pool_scale, w_out_odd, w_coarse, b_coarse, w_fine, b_fine,
           w_gate, w_up, w_down):
    bsz, seq, _ = x_prompt.shape
    nsmp = x_sample.shape[0]
    ntok_p = bsz * seq
    assert seq % TILE_L == 0 and nsmp <= TILE_L and nsmp % SAMPLE_BLK == 0 and ntok_p % nsmp == 0
    p = _prep_params(ln_g, ln_b, w_in_even, w_a2, b_a, gla_norm_g, conv_w, conv_b, conv_ln_g,
                     conv_ln_b, w_out_even, w_grp_pool, pool_scale, w_out_odd, w_coarse, b_coarse,
                     w_fine, b_fine)
    wg = w_gate.reshape(DEPTH * N_EXPERTS, D_MODEL, MOE_FF)
    wu = w_up.reshape(DEPTH * N_EXPERTS, D_MODEL, MOE_FF)
    wd = w_down.reshape(DEPTH * N_EXPERTS, MOE_FF, D_MODEL)

    c_all = jnp.concatenate([c_prompt, c_sample], axis=0)
    mod = _ada_call(c_all, w_ada.reshape(2 * DEPTH, D_MODEL, 3 * D_MODEL),
                    b_ada.reshape(2 * DEPTH, 1, 3 * D_MODEL))
    mod_p = mod[:, :bsz].reshape(2 * DEPTH, bsz, 1, 3 * D_MODEL)
    mod_s = mod[:, bsz:]

    def moe(h2c_all, route_p, route_s, cnt, layer):
        dest, sched = _moe_plan(route_p, route_s, cnt, layer)
        sorted_x = _dispatch_call(sched, dest, h2c_all)
        y_sorted = _moe_call(sched, sorted_x, p[f"wr_hi{layer}"], p[f"wr_lo{layer}"],
                             p[f"br{layer}"], wg, wu, wd)
        return dest[0, :ntok_p], dest[0, ntok_p:ntok_p + nsmp], y_sorted

    xs0 = x_sample.reshape(nsmp, D_MODEL)
    cnt0 = jnp.zeros((1, LANES), F32)
    x1s, h2c_s, route_s, cnt, gla_s, conv_s = _mixer0_sample_call(
        xs0, mod_s[0], mod_s[1], state_gla, state_conv, cnt0, p)
    x1p, h2c_all, route_p, cnt, conv_p, gla_p = _mixer0_prompt_call(
        x_prompt, mod_p[0], mod_p[1], cnt, h2c_s, p)
    dest_p, dest_s, y_sorted = moe(h2c_all, route_p, route_s, cnt, 0)
    x3s, h2c_s, route_s, cnt, pool_s = _mixer1_sample_call(
        x1s, dest_s, y_sorted, mod_s[1], p["ln_g01"], p["ln_b01"], mod_s[2], mod_s[3],
        state_pool, cnt0, p)
    x3p, h2c_all, route_p, cnt, pool_p = _mixer1_prompt_call(
        x1p, dest_p, y_sorted, mod_p[1], p["ln_g01"], p["ln_b01"], mod_p[2], mod_p[3], cnt,
        h2c_s, bsz, p)
    dest_p, dest_s, y_sorted = moe(h2c_all, route_p, route_s, cnt, 1)
    x4p = _post_moe_call(x3p, dest_p, y_sorted, mod_p[3], p["ln_g11"], p["ln_b11"], TILE_L,
                         seq // TILE_L).reshape(bsz, seq, D_MODEL)
    x4s = _post_moe_call(x3s, dest_s, y_sorted, mod_s[3][None], p["ln_g11"], p["ln_b11"], nsmp, 1)
    return (x4p, x4s.reshape(nsmp, 1, D_MODEL), gla_p, conv_p, pool_p, gla_s, conv_s, pool_s)
```

```python
import functools
import math

import jax
import jax.numpy as jnp
from jax import lax
from jax.experimental import pallas as pl
from jax.experimental.pallas import tpu as pltpu

F32 = jnp.float32
BF16 = jnp.bfloat16
I32 = jnp.int32

D_MODEL = 1024
GLA_HEADS = 4
GLA_DK = 64
GLA_DV = 128
GLA_KW = GLA_HEADS * GLA_DK
GLA_VW = GLA_HEADS * GLA_DV
GLA_RANK = 16
GLA_TAU = 16.0
CONV_CH = 512
CONV_WIDTH = 31
CONV_BUF = CONV_WIDTH - 1
POOL_WINDOWS = (2, 4, 8, 16)
POOL_GC = D_MODEL // len(POOL_WINDOWS)
POOL_BUF = max(POOL_WINDOWS) - 1
MOE_GROUPS = 4
MOE_EXPERTS = 4
N_EXPERTS = MOE_GROUPS * MOE_EXPERTS
N_PAIRS = MOE_EXPERTS * (MOE_EXPERTS - 1) // 2
N_BUCKETS = MOE_GROUPS * N_PAIRS
PAIR_A = (0, 0, 0, 1, 1, 2)
PAIR_B = (1, 2, 3, 2, 3, 3)
MOE_FF = 512
R_BUCKET, R_RANK = 0, 1
DEPTH = 2
ALPHA = (2 * DEPTH) ** 0.25
LN_EPS = 1e-5

LANES = 128
SUBLANES = 8
CHUNKS = D_MODEL // LANES
A_PAD = LANES
Z_WIDTH = 2 * GLA_KW + 2 * GLA_VW + 2 * CONV_CH + A_PAD
NEG_BIG = -1e30
VMEM_LIMIT = 56 * 1024 * 1024

TILE_L = 256
CONV_HALO = 32
CONV_W_ROWS = 32
POOL_HALO = 16
SAMPLE_BLK = 16
MOE_TILE = 256
DISPATCH_ROWS = 1280


def _dot(a, b):
    return jnp.dot(a, b, preferred_element_type=F32)


def _dot_nt(a, b):
    return lax.dot_general(a, b, (((1,), (1,)), ((), ())), preferred_element_type=F32)


def _split3(x):
    hi = x.astype(BF16)
    r1 = x - hi.astype(F32)
    mid = r1.astype(BF16)
    lo = (r1 - mid.astype(F32)).astype(BF16)
    return hi, mid, lo


def _split2(x):
    hi = x.astype(BF16)
    lo = (x - hi.astype(F32)).astype(BF16)
    return hi, lo


def _dot_w3(a, w_hi, w_lo):
    a_hi, a_lo = _split2(a)
    return _dot(a_hi, w_hi) + (_dot(a_lo, w_hi) + _dot(a_hi, w_lo))


def _lhs3(parts):
    hi, lo = parts
    return jnp.concatenate([hi, lo, hi], axis=1)


def _rhs3_rows(x):
    hi, lo = _split2(x)
    return jnp.concatenate([hi, hi, lo], axis=0)


def _rhs3_lanes(parts):
    hi, lo = parts
    return jnp.concatenate([hi, hi, lo], axis=1)


def _layer_norm(x, g, b):
    mu = jnp.mean(x, axis=-1, keepdims=True)
    xc = x - mu
    var = jnp.mean(xc * xc, axis=-1, keepdims=True)
    return xc * lax.rsqrt(var + LN_EPS) * g + b


def _standardize(x):
    mu = jnp.mean(x, axis=-1, keepdims=True)
    xc = x - mu
    var = jnp.mean(xc * xc, axis=-1, keepdims=True)
    return xc * lax.rsqrt(var + LN_EPS)


def _sigmoid(x):
    return 1.0 / (1.0 + jnp.exp(-x))


def _silu(x):
    return x * _sigmoid(x)


def _log_sigmoid(x):
    return jnp.minimum(x, 0.0) - jnp.log(1.0 + jnp.exp(-jnp.abs(x)))


def _mod3(mod):
    return mod[:, 0:D_MODEL], mod[:, D_MODEL:2 * D_MODEL], mod[:, 2 * D_MODEL:3 * D_MODEL]


def _full(shape):
    nd = len(shape)
    return pl.BlockSpec(shape, lambda *_: (0,) * nd)


def _const(shape):
    nd = len(shape)
    return pl.BlockSpec(shape, lambda *_: (0,) * nd, pipeline_mode=pl.Buffered(1))


def _hi_lo(w):
    bits = lax.bitcast_convert_type(w, jnp.uint32) & jnp.uint32(0xFFFF0000)
    hi = lax.bitcast_convert_type(bits, F32)
    part = lax.broadcasted_iota(I32, (2,) + w.shape, 0)
    return jnp.where(part == 0, hi[None], (w - hi)[None]).astype(BF16)


def _params(*sem):
    return pltpu.CompilerParams(dimension_semantics=sem, vmem_limit_bytes=VMEM_LIMIT)


def _ada_kernel(cp_ref, cs_ref, w_ref, b_ref, op_ref, os_ref):
    w_hi, w_lo = _split2(w_ref[0])
    op_ref[0] = _dot_w3(_silu(cp_ref[...]), w_hi, w_lo) + b_ref[0]
    os_ref[0] = _dot_w3(_silu(cs_ref[...]), w_hi, w_lo) + b_ref[0]


def _ada_call(c_prompt, c_sample, w_ada, b_ada):
    ncomb = w_ada.shape[0]
    tn = D_MODEL
    blk = lambda i, j: (i, 0, j)
    return pl.pallas_call(
        _ada_kernel,
        grid=(ncomb, 3 * D_MODEL // tn),
        in_specs=[
            _full(c_prompt.shape),
            _full(c_sample.shape),
            pl.BlockSpec((1, D_MODEL, tn), blk),
            pl.BlockSpec((1, 1, tn), blk),
        ],
        out_specs=[pl.BlockSpec((1, c_prompt.shape[0], tn), blk),
                   pl.BlockSpec((1, c_sample.shape[0], tn), blk)],
        out_shape=[jax.ShapeDtypeStruct((ncomb, c_prompt.shape[0], 3 * D_MODEL), F32),
                   jax.ShapeDtypeStruct((ncomb, c_sample.shape[0], 3 * D_MODEL), F32)],
        compiler_params=_params("arbitrary", "arbitrary"),
        name="ada",
    )(c_prompt, c_sample, w_ada, b_ada)


def _route(logits, carry):
    rows = logits.shape[0]
    lane = lax.broadcasted_iota(I32, (rows, LANES), 1)
    lanef = lane.astype(F32)
    big = float(LANES)
    lc = jnp.where(lane < MOE_GROUPS, logits, NEG_BIG)
    mc = jnp.max(lc, axis=-1, keepdims=True)
    gidx = jnp.min(jnp.where(lc == mc, lanef, big), axis=-1, keepdims=True)
    lo = float(MOE_GROUPS) + gidx * float(MOE_EXPERTS)
    in_grp = (lanef >= lo) & (lanef < lo + float(MOE_EXPERTS))
    lf = jnp.where(in_grp, logits, NEG_BIG)
    m1 = jnp.max(lf, axis=-1, keepdims=True)
    i1 = jnp.min(jnp.where(lf == m1, lanef, big), axis=-1, keepdims=True)
    lf2 = jnp.where(lanef == i1, NEG_BIG, lf)
    m2 = jnp.max(lf2, axis=-1, keepdims=True)
    i2 = jnp.min(jnp.where(lf2 == m2, lanef, big), axis=-1, keepdims=True)
    ea = jnp.minimum(i1, i2) - lo
    eb = jnp.maximum(i1, i2) - lo
    bucket = gidx * float(N_PAIRS) + ea * (7.0 - ea) * 0.5 + eb - ea - 1.0
    onehot = jnp.where(lanef == bucket, 1.0, 0.0)
    r = lax.broadcasted_iota(I32, (rows, rows), 0)
    c = lax.broadcasted_iota(I32, (rows, rows), 1)
    before = _dot(jnp.where(r > c, 1.0, 0.0).astype(BF16), onehot.astype(BF16))
    rank = jnp.sum(onehot * (before + carry), axis=-1, keepdims=True)
    new_carry = carry + jnp.sum(onehot, axis=0, keepdims=True)
    cols = jnp.where(lane == R_BUCKET, bucket, 0.0) + jnp.where(lane == R_RANK, rank, 0.0)
    return jnp.transpose(cols)[0:SUBLANES, :], new_carry


def _store_chunked(ref, x):
    rows = x.shape[0]
    for c in range(CHUNKS):
        ref[pl.ds(c, rows, stride=CHUNKS), :] = x[:, c * LANES:(c + 1) * LANES]


def _load_chunked(ref, rows, group, first):
    return jnp.concatenate(
        [ref[pl.ds(first + c, rows, stride=group), :] for c in range(CHUNKS)], axis=1)


def _router_logits(h2, wr_hi_ref, wr_lo_ref, br_ref):
    h_hi, h_lo = _split2(h2)
    wr_hi = wr_hi_ref[...]
    return _dot(h_hi, wr_hi) + _dot(h_lo, wr_hi) + _dot(h_hi, wr_lo_ref[...]) + br_ref[...]


def _post_mixer(x, m, gate, lng_ref, lnb_ref, mod_b, wr_hi_ref, wr_lo_ref, br_ref, carry,
                x1_ref, h2c_ref, route_ref):
    x1 = _layer_norm(ALPHA * x + (1.0 + gate) * m, lng_ref[...], lnb_ref[...])
    shift2, scale2, _ = _mod3(mod_b)
    h2 = x1 * (1.0 + scale2) + shift2
    pad = route_ref.shape[1] - x.shape[0]
    if pad:
        h2 = jnp.concatenate([h2, jnp.zeros((pad, D_MODEL), F32)], axis=0)
    route, carry = _route(_router_logits(h2, wr_hi_ref, wr_lo_ref, br_ref), carry)
    x1_ref[...] = x1
    _store_chunked(h2c_ref, h2)
    route_ref[...] = route
    return carry


def _token_out_specs(rows, slot_rows, x1_map, slot_map):
    return [pl.BlockSpec((rows, D_MODEL), x1_map),
            pl.BlockSpec((slot_rows * CHUNKS, LANES), slot_map),
            pl.BlockSpec((SUBLANES, slot_rows), lambda *i: (0, x1_map(*i)[0]))]


def _token_out_shapes(ntok, nslot_h2, nslot_route):
    return [jax.ShapeDtypeStruct((ntok, D_MODEL), F32),
            jax.ShapeDtypeStruct((nslot_h2 * CHUNKS, LANES), F32),
            jax.ShapeDtypeStruct((SUBLANES, nslot_route), F32)]


def _hold_rows(b, s):
    n, w = b.shape
    if s >= SUBLANES:
        pieces = []
        for p in range(n // (2 * s)):
            r = p * 2 * s + s - 1
            pieces.append(jnp.broadcast_to(b[r:r + 1, :], (2 * s, w)))
        return pieces[0] if len(pieces) == 1 else jnp.concatenate(pieces, axis=0)
    b3 = b.reshape(n // SUBLANES, SUBLANES, w)
    sub = lax.broadcasted_iota(I32, b3.shape, 1)

    def bc(r):
        return jnp.broadcast_to(b3[:, r:r + 1, :], b3.shape)

    out = bc(s - 1)
    for p in range(1, SUBLANES // (2 * s)):
        out = jnp.where(sub >= p * 2 * s, bc(p * 2 * s + s - 1), out)
    return out.reshape(n, w)


def _gla_tile(q, k, v, la, s_prev):
    n = q.shape[0]
    row = lax.broadcasted_iota(I32, (n, n), 0)
    col = lax.broadcasted_iota(I32, (n, n), 1)
    tri = jnp.where(row >= col, 1.0, 0.0).astype(BF16)
    hi, mid, lo = _split3(la)
    b = _dot(tri, hi) + _dot(tri, mid) + _dot(tri, lo)
    b_last = b[n - 1:n, :]
    kl = k * jnp.exp(b_last - b)

    rowi = lax.broadcasted_iota(I32, (n, GLA_KW), 0)
    pair_lanes = [slice(p * LANES, (p + 1) * LANES) for p in range(GLA_HEADS // 2)]

    def swapped(x):
        return [pltpu.roll(x[:, sl], GLA_DK, axis=1).astype(BF16) for sl in pair_lanes]

    def level_operands(qf, kf):
        q_hi = qf.astype(BF16)
        k_hi, k_lo = _split2(kf)
        return q_hi, swapped(qf - q_hi.astype(F32)), k_hi, swapped(k_hi.astype(F32)), k_lo

    levels = []
    s = n // 2
    while s >= 1:
        m = _hold_rows(b, s)
        second = (rowi & s) != 0
        levels.append((2 * s, level_operands(q * jnp.exp(jnp.where(second, b - m, NEG_BIG)),
                                             k * jnp.exp(jnp.where(second, NEG_BIG, m - b)))))
        s //= 2
    levels.append((1, level_operands(q, k)))
    qe = _split2(q * jnp.exp(b))

    lane = lax.broadcasted_iota(I32, (n, LANES), 1)
    xor = row ^ col
    zero = jnp.zeros((), BF16)

    def head_part(parts, sl, hm):
        return tuple(jnp.where(hm, x[:, sl], zero) for x in parts)

    outs = []
    for h in range(GLA_HEADS):
        p = h // 2
        sl = pair_lanes[p]
        hm = (lane < GLA_DK) if h % 2 == 0 else (lane >= GLA_DK)
        att = None
        for span, (q_hi, q_lo_sw, k_hi, k_hi_sw, k_lo) in levels:
            lhs = jnp.concatenate([jnp.where(hm, q_hi[:, sl], q_lo_sw[p]),
                                   jnp.where(hm, q_hi[:, sl], zero)], axis=1)
            rhs = jnp.concatenate([jnp.where(hm, k_hi[:, sl], k_hi_sw[p]), k_lo[:, sl]], axis=1)
            a_s = _dot_nt(lhs, rhs)
            att = a_s if att is None else jnp.where(xor < span, a_s, att)
        o_h = _dot(_lhs3(_split2(att)), _rhs3_rows(v[:, h * GLA_DV:(h + 1) * GLA_DV]))
        o_h = o_h + _dot(_lhs3(head_part(qe, sl, hm)), _rhs3_rows(s_prev[sl, :]))
        outs.append(o_h)

    decay = jnp.exp(b_last)
    upper = lax.broadcasted_iota(I32, (LANES, LANES), 0) < GLA_DK
    s_new = []
    for p in range(GLA_HEADS // 2):
        sl = slice(p * LANES, (p + 1) * LANES)
        kl_t = jnp.transpose(kl[:, sl])
        u = _dot(_lhs3(_split2(kl_t)), _rhs3_rows(v[:, p * 2 * GLA_DV:(p + 1) * 2 * GLA_DV]))
        upd = jnp.where(upper, u[:, 0:GLA_DV], u[:, GLA_DV:2 * GLA_DV])
        dcol = jnp.transpose(jnp.broadcast_to(decay[:, sl], (LANES, LANES)))
        s_new.append(dcol * s_prev[sl, :] + upd)
    return outs, jnp.concatenate(s_new, axis=0)


def _split_projection(z):
    c0 = 0
    q = z[:, c0:c0 + GLA_KW] * (GLA_DK ** -0.5); c0 += GLA_KW
    k = z[:, c0:c0 + GLA_KW]; c0 += GLA_KW
    v = z[:, c0:c0 + GLA_VW]; c0 += GLA_VW
    g = z[:, c0:c0 + GLA_VW]; c0 += GLA_VW
    ua = z[:, c0:c0 + CONV_CH]; c0 += CONV_CH
    ug = z[:, c0:c0 + CONV_CH]; c0 += CONV_CH
    a_lr = z[:, c0:c0 + A_PAD]
    return q, k, v, g, ua, ug, a_lr


def _mix0_project(o_heads, g, y, gng_ref, w_out_ref):
    sl = slice(GLA_VW, GLA_VW + CONV_CH)
    m = _dot_w3(y, w_out_ref[0, sl, :], w_out_ref[1, sl, :])
    for hd in range(GLA_HEADS):
        sl = slice(hd * GLA_DV, (hd + 1) * GLA_DV)
        o_h = _standardize(o_heads[hd]) * gng_ref[:, sl] * _silu(g[:, sl])
        m = m + _dot_w3(o_h, w_out_ref[0, sl, :], w_out_ref[1, sl, :])
    return m


def _mix0_inputs(h, w_in_ref, w_a2_ref, b_a_ref):
    z = _dot_w3(h, w_in_ref[0], w_in_ref[1])
    q, k, v, g, ua, ug, a_lr = _split_projection(z)
    la = _log_sigmoid(_dot_w3(a_lr, w_a2_ref[0], w_a2_ref[1]) + b_a_ref[...]) * (1.0 / GLA_TAU)
    return q, k, v, g, ua, ug, la


def _prompt_grid_specs(bsz, nt):
    ntile = bsz * nt
    seq_map = lambda b, t: (jnp.minimum(b, bsz - 1), 0, 0)
    x_map = lambda b, t: (jnp.minimum(b, bsz - 1), t, 0)
    tok_map = lambda b, t: (jnp.minimum(b * nt + t, ntile - 1), 0)
    slot_map = lambda b, t: (jnp.minimum(b * nt + t, ntile), 0)
    state_map = lambda b, t: (0, jnp.minimum(b, bsz - 1), 0, 0)
    return seq_map, x_map, tok_map, slot_map, state_map


def _mixer0_prompt_kernel(x_ref, moda_ref, modb_ref, cnt_in_ref, h2cs_ref, w_in_ref, w_a2_ref,
                          b_a_ref, gng_ref,
                          cw_ref, cb_ref, clg_ref, clb_ref, w_out_ref, lng_ref, lnb_ref,
                          wr_hi_ref, wr_lo_ref, br_ref,
                          x1_ref, h2c_ref, route_ref, cnt_ref, nconv_ref, ngla_ref,
                          s_ref, cbuf_ref, shift_ref, carry_ref):
    nseq = pl.num_programs(0) - 1

    @pl.when(pl.program_id(0) < nseq)
    def _():
        _mixer0_prompt_tile(x_ref, moda_ref, modb_ref, cnt_in_ref, w_in_ref, w_a2_ref, b_a_ref,
                            gng_ref, cw_ref, cb_ref, clg_ref, clb_ref, w_out_ref, lng_ref, lnb_ref,
                            wr_hi_ref, wr_lo_ref, br_ref, x1_ref, h2c_ref, route_ref, cnt_ref,
                            nconv_ref, ngla_ref, s_ref, cbuf_ref, shift_ref, carry_ref)

    @pl.when((pl.program_id(0) == nseq) & (pl.program_id(1) == 0))
    def _():
        h2c_ref[...] = h2cs_ref[...]


def _mixer0_prompt_tile(x_ref, moda_ref, modb_ref, cnt_in_ref, w_in_ref, w_a2_ref, b_a_ref,
                        gng_ref, cw_ref, cb_ref, clg_ref, clb_ref, w_out_ref, lng_ref, lnb_ref,
                        wr_hi_ref, wr_lo_ref, br_ref, x1_ref, h2c_ref, route_ref, cnt_ref,
                        nconv_ref, ngla_ref, s_ref, cbuf_ref, shift_ref, carry_ref):
    t = pl.program_id(1)
    n = x_ref.shape[1]

    @pl.when((t == 0) & (pl.program_id(0) == 0))
    def _():
        carry_ref[...] = cnt_in_ref[...]

    @pl.when(t == 0)
    def _():
        s_ref[...] = jnp.zeros_like(s_ref)
        cbuf_ref[0:CONV_HALO, :] = jnp.zeros((CONV_HALO, CONV_CH), F32)

    x = x_ref[0]
    shift, scale, gate = _mod3(moda_ref[0])
    h = x * (1.0 + scale) + shift
    q, k, v, g, ua, ug, la = _mix0_inputs(h, w_in_ref, w_a2_ref, b_a_ref)
    o_heads, s_new = _gla_tile(q, k, v, la, s_ref[...])
    s_ref[...] = s_new

    glu = ua * _sigmoid(ug)
    cbuf_ref[CONV_HALO:CONV_HALO + n, :] = glu
    acc = jnp.broadcast_to(cb_ref[...], (n, CONV_CH))
    base = CONV_HALO - CONV_BUF
    for r in range(SUBLANES):
        taps = range(r, CONV_WIDTH, SUBLANES)
        span = n + (len(taps) - 1) * SUBLANES
        if (base + r) % SUBLANES == 0:
            src, off = cbuf_ref, base + r
        else:
            shift_ref[r, 0:span, :] = cbuf_ref[base + r:base + r + span, :]
            src, off = shift_ref.at[r], 0
        for a, j in enumerate(taps):
            lo = off + a * SUBLANES
            acc = acc + cw_ref[j:j + 1, :] * src[lo:lo + n, :]
    y = _silu(_layer_norm(acc, clg_ref[...], clb_ref[...]))

    @pl.when(t == pl.num_programs(1) - 1)
    def _():
        nconv_ref[0, 0] = cbuf_ref[CONV_HALO + n - CONV_BUF:CONV_HALO + n, :]
        ngla_ref[0, 0] = s_new.reshape(GLA_HEADS, GLA_DK, GLA_DV)

    cbuf_ref[0:CONV_HALO, :] = cbuf_ref[n:n + CONV_HALO, :]

    m = _mix0_project(o_heads, g, y, gng_ref, w_out_ref)
    carry = _post_mixer(x, m, gate, lng_ref, lnb_ref, modb_ref[0], wr_hi_ref, wr_lo_ref, br_ref,
                        carry_ref[...], x1_ref, h2c_ref, route_ref)
    carry_ref[...] = carry
    cnt_ref[...] = carry


def _mixer0_prompt_call(x, mod_a, mod_b, cnt, h2c_s, p):
    bsz, seq, _ = x.shape
    tl = TILE_L
    nt = seq // tl
    seq_map, x_map, tok, slot, state_map = _prompt_grid_specs(bsz, nt)
    return pl.pallas_call(
        _mixer0_prompt_kernel,
        grid=(bsz + 1, nt),
        in_specs=[
            pl.BlockSpec((1, tl, D_MODEL), x_map),
            pl.BlockSpec((1, 1, 3 * D_MODEL), seq_map),
            pl.BlockSpec((1, 1, 3 * D_MODEL), seq_map),
            _full((1, LANES)),
            _full((tl * CHUNKS, LANES)),
            _const((2, D_MODEL, Z_WIDTH)),
            _const((2, A_PAD, GLA_KW)),
            _full((1, GLA_KW)),
            _full((1, GLA_VW)),
            _full((CONV_W_ROWS, CONV_CH)),
            _full((1, CONV_CH)),
            _full((1, CONV_CH)),
            _full((1, CONV_CH)),
            _const((2, GLA_VW + CONV_CH, D_MODEL)),
            _full((1, D_MODEL)),
            _full((1, D_MODEL)),
            _full((D_MODEL, LANES)),
            _full((D_MODEL, LANES)),
            _full((1, LANES)),
        ],
        out_specs=_token_out_specs(tl, tl, tok, slot) + [
            _full((1, LANES)),
            pl.BlockSpec((1, 1, CONV_BUF, CONV_CH), state_map),
            pl.BlockSpec((1, 1, GLA_HEADS, GLA_DK, GLA_DV), lambda b, t: state_map(b, t) + (0,)),
        ],
        out_shape=_token_out_shapes(bsz * seq, bsz * seq + tl, bsz * seq) + [
            jax.ShapeDtypeStruct((1, LANES), F32),
            jax.ShapeDtypeStruct((1, bsz, CONV_BUF, CONV_CH), F32),
            jax.ShapeDtypeStruct((1, bsz, GLA_HEADS, GLA_DK, GLA_DV), F32),
        ],
        scratch_shapes=[
            pltpu.VMEM((GLA_KW, GLA_DV), F32),
            pltpu.VMEM((CONV_HALO + tl, CONV_CH), F32),
            pltpu.VMEM((SUBLANES, CONV_HALO + tl, CONV_CH), F32),
            pltpu.VMEM((1, LANES), F32),
        ],
        compiler_params=_params("arbitrary", "arbitrary"),
        name="mixer0_prompt",
    )(x, mod_a, mod_b, cnt, h2c_s, p["w_in"], p["w_a2"], p["b_a"], p["gng"], p["conv_w"], p["conv_b"],
      p["conv_ln_g"], p["conv_ln_b"], p["w_out0"], p["ln_g00"], p["ln_b00"],
      p["wr_hi0"], p["wr_lo0"], p["br0"])


def _mixer0_sample_kernel(x_ref, moda_ref, modb_ref, sgla_ref, sconv_ref, cnt_in_ref,
                          w_in_ref, w_a2_ref,
                          b_a_ref, gng_ref, cw_ref, cb_ref, clg_ref, clb_ref, w_out_ref, lng_ref,
                          lnb_ref, wr_hi_ref, wr_lo_ref, br_ref,
                          x1_ref, h2c_ref, route_ref, cnt_ref, ngla_ref, nconv_ref,
                          zt_ref, v_ref, g_ref, glu_ref, o_ref, y_ref):
    i = pl.program_id(0)
    nb = sgla_ref.shape[1]
    ntok = x_ref.shape[0]

    @pl.when(i == 0)
    def _():
        shift, scale, _ = _mod3(moda_ref[...])
        h = x_ref[...] * (1.0 + scale) + shift
        q, k, v, g, ua, ug, la = _mix0_inputs(h, w_in_ref, w_a2_ref, b_a_ref)
        v_ref[...] = v
        g_ref[...] = g
        glu_ref[...] = ua * _sigmoid(ug)
        for j, val in enumerate((jnp.exp(la), k, q)):
            hi, lo = _split2(jnp.transpose(val))
            zt_ref[(2 * j) * GLA_KW:(2 * j + 1) * GLA_KW, :] = hi
            zt_ref[(2 * j + 1) * GLA_KW:(2 * j + 2) * GLA_KW, :] = lo

    tok_row = lax.broadcasted_iota(I32, (ntok, LANES), 0)
    blk = pl.ds(pl.multiple_of(i * nb, nb), nb)
    v_blk = v_ref[blk, :]
    glu_blk = glu_ref[blk, :]
    o_rows, y_rows = [], []
    for n in range(nb):
        onehot = jnp.where(tok_row == i * nb + n, 1.0, 0.0).astype(BF16)
        cols = _dot(zt_ref[...], onehot)
        a_col = cols[0:GLA_KW] + cols[GLA_KW:2 * GLA_KW]
        k_col = cols[2 * GLA_KW:3 * GLA_KW] + cols[3 * GLA_KW:4 * GLA_KW]
        q_col = cols[4 * GLA_KW:5 * GLA_KW] + cols[5 * GLA_KW:6 * GLA_KW]
        v_row = v_blk[n:n + 1, :]
        v_b = jnp.concatenate(
            [jnp.broadcast_to(v_row[:, h * GLA_DV:(h + 1) * GLA_DV], (GLA_DK, GLA_DV))
             for h in range(GLA_HEADS)], axis=0)
        s_old = sgla_ref[0, n].reshape(GLA_KW, GLA_DV)
        s_new = a_col * s_old + k_col * v_b
        ngla_ref[0, n] = s_new.reshape(GLA_HEADS, GLA_DK, GLA_DV)
        o4 = jnp.sum((q_col * s_new).reshape(GLA_HEADS, GLA_DK, GLA_DV), axis=1)
        o_rows.append(jnp.concatenate([o4[h:h + 1, :] for h in range(GLA_HEADS)], axis=1))
        glu_row = glu_blk[n:n + 1, :]
        past = sconv_ref[0, n]
        y_rows.append(jnp.sum(past * cw_ref[0:CONV_BUF, :], axis=0, keepdims=True)
                      + glu_row * cw_ref[CONV_BUF:CONV_WIDTH, :] + cb_ref[...])
        nconv_ref[0, n, 0:CONV_BUF - 1, :] = sconv_ref[0, n, 1:CONV_BUF, :]
        nconv_ref[0, n, CONV_BUF - 1:CONV_BUF, :] = glu_row
    o_ref[blk, :] = jnp.concatenate(o_rows, axis=0)
    y_ref[blk, :] = jnp.concatenate(y_rows, axis=0)

    @pl.when(i == pl.num_programs(0) - 1)
    def _():
        y = _silu(_layer_norm(y_ref[...], clg_ref[...], clb_ref[...]))
        o_heads = [o_ref[:, hd * GLA_DV:(hd + 1) * GLA_DV] for hd in range(GLA_HEADS)]
        m = _mix0_project(o_heads, g_ref[...], y, gng_ref, w_out_ref)
        _, _, gate = _mod3(moda_ref[...])
        cnt_ref[...] = _post_mixer(x_ref[...], m, gate, lng_ref, lnb_ref, modb_ref[...], wr_hi_ref,
                                   wr_lo_ref, br_ref, cnt_in_ref[...], x1_ref, h2c_ref, route_ref)


def _mixer0_sample_call(x, mod_a, mod_b, state_gla, state_conv, cnt, p):
    ntok = x.shape[0]
    nb = SAMPLE_BLK
    tok = lambda i: (0, 0)
    return pl.pallas_call(
        _mixer0_sample_kernel,
        grid=(ntok // nb,),
        in_specs=[
            _full((ntok, D_MODEL)),
            _full((ntok, 3 * D_MODEL)),
            _full((ntok, 3 * D_MODEL)),
            pl.BlockSpec((1, nb, GLA_HEADS, GLA_DK, GLA_DV), lambda i: (0, i, 0, 0, 0)),
            pl.BlockSpec((1, nb, CONV_BUF, CONV_CH), lambda i: (0, i, 0, 0)),
            _full((1, LANES)),
            _const((2, D_MODEL, Z_WIDTH)),
            _const((2, A_PAD, GLA_KW)),
            _full((1, GLA_KW)),
            _full((1, GLA_VW)),
            _full((CONV_W_ROWS, CONV_CH)),
            _full((1, CONV_CH)),
            _full((1, CONV_CH)),
            _full((1, CONV_CH)),
            _const((2, GLA_VW + CONV_CH, D_MODEL)),
            _full((1, D_MODEL)),
            _full((1, D_MODEL)),
            _full((D_MODEL, LANES)),
            _full((D_MODEL, LANES)),
            _full((1, LANES)),
        ],
        out_specs=_token_out_specs(ntok, TILE_L, tok, tok) + [
            _full((1, LANES)),
            pl.BlockSpec((1, nb, GLA_HEADS, GLA_DK, GLA_DV), lambda i: (0, i, 0, 0, 0)),
            pl.BlockSpec((1, nb, CONV_BUF, CONV_CH), lambda i: (0, i, 0, 0)),
        ],
        out_shape=_token_out_shapes(ntok, TILE_L, TILE_L) + [
            jax.ShapeDtypeStruct((1, LANES), F32),
            jax.ShapeDtypeStruct(state_gla.shape, F32),
            jax.ShapeDtypeStruct(state_conv.shape, F32),
        ],
        scratch_shapes=[
            pltpu.VMEM((6 * GLA_KW, ntok), BF16),
            pltpu.VMEM((ntok, GLA_VW), F32),
            pltpu.VMEM((ntok, GLA_VW), F32),
            pltpu.VMEM((ntok, CONV_CH), F32),
            pltpu.VMEM((ntok, GLA_VW), F32),
            pltpu.VMEM((ntok, CONV_CH), F32),
        ],
        compiler_params=_params("arbitrary"),
        name="mixer0_sample",
    )(x, mod_a, mod_b, state_gla, state_conv, cnt, p["w_in"], p["w_a2"], p["b_a"],
      p["gng"], p["conv_w"], p["conv_b"], p["conv_ln_g"], p["conv_ln_b"], p["w_out0"],
      p["ln_g00"], p["ln_b00"], p["wr_hi0"], p["wr_lo0"], p["br0"])


def _pool_project(pooled, h, wg_ref, ps_ref, w_out_ref):
    m = None
    for gi in range(len(POOL_WINDOWS)):
        sl = slice(gi * POOL_GC, (gi + 1) * POOL_GC)
        mixed = _dot((pooled[:, sl] - h[:, sl]).astype(BF16), wg_ref[gi]) * ps_ref[:, sl]
        part = _dot(mixed.astype(BF16), w_out_ref[sl, :])
        m = part if m is None else m + part
    return m


def _block_input(xin_ref, m_ref, mod_m, lnpg_ref, lnpb_ref):
    m = _load_chunked(m_ref, xin_ref.shape[0], CHUNKS, 0)
    _, _, gate = _mod3(mod_m)
    return _layer_norm(ALPHA * xin_ref[...] + (1.0 + gate) * m, lnpg_ref[...], lnpb_ref[...])


def _mixer1_prompt_kernel(dest_ref, dnext_ref, xin_ref, y_any, modm_ref, lnpg_ref, lnpb_ref,
                          moda_ref, modb_ref, cnt_in_ref, h2cs_ref, wg_ref, ps_ref,
                          w_out_ref, lng_ref, lnb_ref, wr_hi_ref, wr_lo_ref, br_ref,
                          x1_ref, h2c_ref, route_ref, cnt_ref, npool_ref,
                          pbuf_ref, carry_ref, mbuf, sems):
    nseq = pl.num_programs(0) - 1
    nt = pl.num_programs(1)

    @pl.when(pl.program_id(0) < nseq)
    def _():
        m_ref, drain = _gather_pipelined(pl.program_id(0) * nt + pl.program_id(1), nseq * nt,
                                         dest_ref, dnext_ref, y_any, mbuf, sems)
        x = _block_input(xin_ref, m_ref, modm_ref[0], lnpg_ref, lnpb_ref)
        _mixer1_prompt_tile(x, moda_ref, modb_ref, cnt_in_ref, wg_ref, ps_ref, w_out_ref,
                            lng_ref, lnb_ref, wr_hi_ref, wr_lo_ref, br_ref, x1_ref, h2c_ref,
                            route_ref, cnt_ref, npool_ref, pbuf_ref, carry_ref)
        drain()

    @pl.when((pl.program_id(0) == nseq) & (pl.program_id(1) == 0))
    def _():
        h2c_ref[...] = h2cs_ref[...]


def _mixer1_prompt_tile(x, moda_ref, modb_ref, cnt_in_ref, wg_ref, ps_ref, w_out_ref,
                        lng_ref, lnb_ref, wr_hi_ref, wr_lo_ref, br_ref, x1_ref, h2c_ref,
                        route_ref, cnt_ref, npool_ref, pbuf_ref, carry_ref):
    t = pl.program_id(1)
    n = x.shape[0]

    @pl.when((t == 0) & (pl.program_id(0) == 0))
    def _():
        carry_ref[...] = cnt_in_ref[...]

    @pl.when(t == 0)
    def _():
        pbuf_ref[0:POOL_HALO, :] = jnp.zeros((POOL_HALO, D_MODEL), F32)

    shift, scale, gate = _mod3(moda_ref[0])
    h = x * (1.0 + scale) + shift
    pbuf_ref[POOL_HALO:POOL_HALO + n, :] = h

    cur = pbuf_ref[...]
    sums = []
    for gi, w in enumerate(POOL_WINDOWS):
        cur = cur + pltpu.roll(cur, w // 2, axis=0)
        sums.append(cur[POOL_HALO:, 0:POOL_GC])
        if gi + 1 < len(POOL_WINDOWS):
            cur = cur[:, POOL_GC:]
    pos = lax.broadcasted_iota(I32, (n, POOL_GC), 0) + t * n
    pooled = jnp.concatenate(
        [s / jnp.minimum(w, pos + 1).astype(F32) for s, w in zip(sums, POOL_WINDOWS)], axis=1)

    @pl.when(t == pl.num_programs(1) - 1)
    def _():
        npool_ref[0, 0] = pbuf_ref[POOL_HALO + n - POOL_BUF:POOL_HALO + n, :]

    pbuf_ref[0:POOL_HALO, :] = pbuf_ref[n:n + POOL_HALO, :]

    m = _pool_project(pooled, h, wg_ref, ps_ref, w_out_ref)
    carry = _post_mixer(x, m, gate, lng_ref, lnb_ref, modb_ref[0], wr_hi_ref, wr_lo_ref, br_ref,
                        carry_ref[...], x1_ref, h2c_ref, route_ref)
    carry_ref[...] = carry
    cnt_ref[...] = carry


def _mixer1_prompt_call(x1_prev, dest, y_sorted, mod_m, lnpg, lnpb, mod_a, mod_b, cnt, h2c_s, bsz, p):
    seq = x1_prev.shape[0] // bsz
    tl = TILE_L
    nt = seq // tl
    ntile = bsz * nt
    seq_map, _, tok, slot, state_map = _prompt_grid_specs(bsz, nt)
    ng = len(POOL_WINDOWS)
    return pl.pallas_call(
        _mixer1_prompt_kernel,
        grid=(bsz + 1, nt),
        in_specs=[
            pl.BlockSpec((1, 1, tl), lambda b, t: tok(b, t) + (0,), memory_space=pltpu.SMEM),
            pl.BlockSpec((1, 1, tl), lambda b, t: (jnp.minimum(b * nt + t + 1, ntile - 1), 0, 0),
                         memory_space=pltpu.SMEM),
            pl.BlockSpec((tl, D_MODEL), tok),
            pl.BlockSpec(memory_space=pl.ANY),
            pl.BlockSpec((1, 1, 3 * D_MODEL), seq_map),
            _full((1, D_MODEL)),
            _full((1, D_MODEL)),
            pl.BlockSpec((1, 1, 3 * D_MODEL), seq_map),
            pl.BlockSpec((1, 1, 3 * D_MODEL), seq_map),
            _full((1, LANES)),
            _full((tl * CHUNKS, LANES)),
            _full((ng, POOL_GC, POOL_GC)),
            _full((1, D_MODEL)),
            _full((D_MODEL, D_MODEL)),
            _full((1, D_MODEL)),
            _full((1, D_MODEL)),
            _full((D_MODEL, LANES)),
            _full((D_MODEL, LANES)),
            _full((1, LANES)),
        ],
        out_specs=_token_out_specs(tl, tl, tok, slot) + [
            _full((1, LANES)),
            pl.BlockSpec((1, 1, POOL_BUF, D_MODEL), state_map),
        ],
        out_shape=_token_out_shapes(bsz * seq, bsz * seq + tl, bsz * seq) + [
            jax.ShapeDtypeStruct((1, LANES), F32),
            jax.ShapeDtypeStruct((1, bsz, POOL_BUF, D_MODEL), F32),
        ],
        scratch_shapes=[pltpu.VMEM((POOL_HALO + tl, D_MODEL), F32), pltpu.VMEM((1, LANES), F32),
                        pltpu.VMEM((2, tl * CHUNKS, LANES), F32), pltpu.SemaphoreType.DMA((2,))],
        compiler_params=_params("arbitrary", "arbitrary"),
        name="mixer1_prompt",
    )(dest.reshape(ntile, 1, tl), dest.reshape(ntile, 1, tl), x1_prev, y_sorted, mod_m, lnpg, lnpb,
      mod_a, mod_b, cnt, h2c_s,
      p["w_grp"], p["pool_scale"], p["w_out1"], p["ln_g10"], p["ln_b10"],
      p["wr_hi1"], p["wr_lo1"], p["br1"])


def _mixer1_sample_kernel(dest_ref, xin_ref, y_any, modm_ref, lnpg_ref, lnpb_ref, moda_ref,
                          modb_ref, spool_ref, cnt_in_ref, wg_ref, ps_ref, w_out_ref,
                          lng_ref, lnb_ref, wr_hi_ref, wr_lo_ref, br_ref,
                          x1_ref, h2c_ref, route_ref, cnt_ref, npool_ref,
                          x_ref, h_ref, pooled_ref, mbuf, sem):
    i = pl.program_id(0)
    nb = spool_ref.shape[1]

    @pl.when(i == 0)
    def _():
        _start_row_gather(dest_ref, y_any, mbuf, sem)
        _wait_row_gather(y_any, mbuf, sem)
        x_ref[...] = _block_input(xin_ref, mbuf, modm_ref[...], lnpg_ref, lnpb_ref)
        shift, scale, _ = _mod3(moda_ref[...])
        h_ref[...] = x_ref[...] * (1.0 + scale) + shift

    lane = lax.broadcasted_iota(I32, (POOL_BUF, D_MODEL), 1)
    rowi = lax.broadcasted_iota(I32, (POOL_BUF, D_MODEL), 0)
    first = jnp.zeros((POOL_BUF, D_MODEL), I32)
    lane1 = lax.broadcasted_iota(I32, (1, D_MODEL), 1)
    inv_w = jnp.zeros((1, D_MODEL), F32)
    for gi, w in enumerate(POOL_WINDOWS):
        in_g = (lane >= gi * POOL_GC) & (lane < (gi + 1) * POOL_GC)
        first = jnp.where(in_g, POOL_BUF - (w - 1), first)
        in_g1 = (lane1 >= gi * POOL_GC) & (lane1 < (gi + 1) * POOL_GC)
        inv_w = jnp.where(in_g1, 1.0 / w, inv_w)
    keep = rowi >= first

    blk = pl.ds(pl.multiple_of(i * nb, nb), nb)
    h_blk = h_ref[blk, :]
    rows = []
    for n in range(nb):
        h_row = h_blk[n:n + 1, :]
        past = spool_ref[0, n]
        total = jnp.sum(jnp.where(keep, past, 0.0), axis=0, keepdims=True) + h_row
        rows.append(total * inv_w)
        npool_ref[0, n, 0:POOL_BUF - 1, :] = spool_ref[0, n, 1:POOL_BUF, :]
        npool_ref[0, n, POOL_BUF - 1:POOL_BUF, :] = h_row
    pooled_ref[blk, :] = jnp.concatenate(rows, axis=0)

    @pl.when(i == pl.num_programs(0) - 1)
    def _():
        m = _pool_project(pooled_ref[...], h_ref[...], wg_ref, ps_ref, w_out_ref)
        _, _, gate = _mod3(moda_ref[...])
        cnt_ref[...] = _post_mixer(x_ref[...], m, gate, lng_ref, lnb_ref, modb_ref[...], wr_hi_ref,
                                   wr_lo_ref, br_ref, cnt_in_ref[...], x1_ref, h2c_ref, route_ref)


def _mixer1_sample_call(x1_prev, dest, y_sorted, mod_m, lnpg, lnpb, mod_a, mod_b, state_pool,
                        cnt, p):
    ntok = x1_prev.shape[0]
    nb = SAMPLE_BLK
    ng = len(POOL_WINDOWS)
    tok = lambda i: (0, 0)
    return pl.pallas_call(
        _mixer1_sample_kernel,
        grid=(ntok // nb,),
        in_specs=[
            pl.BlockSpec((1, 1, ntok), lambda i: (0, 0, 0), memory_space=pltpu.SMEM),
            _full((ntok, D_MODEL)),
            pl.BlockSpec(memory_space=pl.ANY),
            _full((ntok, 3 * D_MODEL)),
            _full((1, D_MODEL)),
            _full((1, D_MODEL)),
            _full((ntok, 3 * D_MODEL)),
            _full((ntok, 3 * D_MODEL)),
            pl.BlockSpec((1, nb, POOL_BUF, D_MODEL), lambda i: (0, i, 0, 0)),
            _full((1, LANES)),
            _full((ng, POOL_GC, POOL_GC)),
            _full((1, D_MODEL)),
            _full((D_MODEL, D_MODEL)),
            _full((1, D_MODEL)),
            _full((1, D_MODEL)),
            _full((D_MODEL, LANES)),
            _full((D_MODEL, LANES)),
            _full((1, LANES)),
        ],
        out_specs=_token_out_specs(ntok, TILE_L, tok, tok) + [
            _full((1, LANES)),
            pl.BlockSpec((1, nb, POOL_BUF, D_MODEL), lambda i: (0, i, 0, 0)),
        ],
        out_shape=_token_out_shapes(ntok, TILE_L, TILE_L) + [
            jax.ShapeDtypeStruct((1, LANES), F32),
            jax.ShapeDtypeStruct(state_pool.shape, F32),
        ],
        scratch_shapes=[pltpu.VMEM((ntok, D_MODEL), F32)] * 3 + [
            pltpu.VMEM((ntok * CHUNKS, LANES), F32), pltpu.SemaphoreType.DMA],
        compiler_params=_params("arbitrary"),
        name="mixer1_sample",
    )(dest.reshape(1, 1, ntok), x1_prev, y_sorted, mod_m, lnpg, lnpb, mod_a, mod_b, state_pool, cnt,
      p["w_grp"], p["pool_scale"], p["w_out1"],
      p["ln_g10"], p["ln_b10"], p["wr_hi1"], p["wr_lo1"], p["br1"])


def _dispatch_kernel(sched_ref, dest_ref, src_ref, dst_any, zeros_ref, sem, zsem):
    n = dest_ref.shape[2]
    tile_rows = zeros_ref.shape[0]

    @pl.when(pl.program_id(0) == 0)
    def _():
        zeros_ref[...] = jnp.zeros_like(zeros_ref)

        def fill(i):
            return pltpu.make_async_copy(
                zeros_ref, dst_any.at[pl.ds(i * tile_rows, tile_rows)], zsem)

        for i in range(dst_any.shape[0] // tile_rows):
            pl.when(sched_ref[S_ZERO, i] == 1)(lambda i=i: fill(i).start())
        for i in range(dst_any.shape[0] // tile_rows):
            pl.when(sched_ref[S_ZERO, i] == 1)(lambda i=i: fill(i).wait())

    for j in range(n):
        d = dest_ref[0, 0, j]
        pltpu.make_async_copy(
            src_ref.at[pl.ds(j * CHUNKS, CHUNKS)],
            dst_any.at[pl.ds(pl.multiple_of(d * CHUNKS, CHUNKS), CHUNKS)], sem).start()
    pltpu.make_async_copy(src_ref, dst_any.at[pl.ds(0, n * CHUNKS)], sem).wait()


def _dispatch_call(sched, dest, h2c_all):
    nslot = dest.shape[1]
    rows = DISPATCH_ROWS
    tile_rows = MOE_TILE * CHUNKS
    grid_spec = pltpu.PrefetchScalarGridSpec(
        num_scalar_prefetch=1,
        grid=(nslot // rows,),
        in_specs=[
            pl.BlockSpec((1, 1, rows), lambda i, s: (i, 0, 0), memory_space=pltpu.SMEM),
            pl.BlockSpec((rows * CHUNKS, LANES), lambda i, s: (i, 0)),
        ],
        out_specs=pl.BlockSpec(memory_space=pl.ANY),
        scratch_shapes=[pltpu.VMEM((tile_rows, LANES), F32), pltpu.SemaphoreType.DMA,
                        pltpu.SemaphoreType.DMA],
    )
    return pl.pallas_call(
        _dispatch_kernel,
        grid_spec=grid_spec,
        out_shape=jax.ShapeDtypeStruct((_num_tiles(nslot) * tile_rows, LANES), F32),
        compiler_params=_params("arbitrary"),
        name="dispatch",
    )(sched, dest.reshape(nslot // rows, 1, rows), h2c_all)


(S_BLK, S_EA, S_EB, S_VALID, S_FIRST, S_LANE_G, S_LANE_A, S_LANE_B, S_ZERO,
 S_NEA, S_NEB, S_HASNEXT) = range(12)
SCHED_ROWS = 2 * SUBLANES


def _moe_kernel(sched_ref, x_ref, wr_hi_ref, wr_lo_ref, br_ref, wg_any, wu_any, wd_any,
                y_ref, wgu_ref, wd_ref, wbuf_gu, wbuf_d, wsem, wslot_ref):
    i = pl.program_id(0)
    tm = x_ref.shape[0] // CHUNKS

    def weight_copies(ea, eb, slot):
        srcs = ((wg_any, ea), (wu_any, ea), (wg_any, eb), (wu_any, eb))
        cps = [pltpu.make_async_copy(w.at[e], wbuf_gu.at[slot, j], wsem.at[slot])
               for j, (w, e) in enumerate(srcs)]
        return cps + [pltpu.make_async_copy(wd_any.at[e], wbuf_d.at[slot, j], wsem.at[slot])
                      for j, e in enumerate((ea, eb))]

    @pl.when(i == 0)
    def _():
        wslot_ref[0] = 0
        for cp in weight_copies(sched_ref[S_EA, 0], sched_ref[S_EB, 0], 0):
            cp.start()

    @pl.when(sched_ref[S_FIRST, i] == 1)
    def _():
        slot = wslot_ref[0]
        for cp in weight_copies(sched_ref[S_EA, i], sched_ref[S_EB, i], slot):
            cp.wait()

        @pl.when(sched_ref[S_HASNEXT, i] == 1)
        def _():
            for cp in weight_copies(sched_ref[S_NEA, i], sched_ref[S_NEB, i], 1 - slot):
                cp.start()

        for e in range(2):
            wgu_ref[2 * e] = wbuf_gu[slot, 2 * e].astype(BF16)
            wgu_ref[2 * e + 1] = wbuf_gu[slot, 2 * e + 1].astype(BF16)
            wd_ref[e * MOE_FF:(e + 1) * MOE_FF, :] = wbuf_d[slot, e].astype(BF16)
        wslot_ref[0] = 1 - slot

    @pl.when(sched_ref[S_VALID, i] == 0)
    def _():
        y_ref[...] = jnp.zeros_like(y_ref)

    @pl.when(sched_ref[S_VALID, i] == 1)
    def _():
        x = _load_chunked(x_ref, tm, CHUNKS, 0)
        logits = _router_logits(x, wr_hi_ref, wr_lo_ref, br_ref)
        lane = lax.broadcasted_iota(I32, logits.shape, 1)

        def pick(row):
            return jnp.sum(jnp.where(lane == sched_ref[row, i], logits, 0.0), axis=-1, keepdims=True)

        l_g, l_a, l_b = pick(S_LANE_G), pick(S_LANE_A), pick(S_LANE_B)
        p_g = 1.0 / jnp.sum(jnp.where(lane < MOE_GROUPS, jnp.exp(logits - l_g), 0.0),
                            axis=-1, keepdims=True)
        w_ab = (p_g / (1.0 + jnp.exp(l_b - l_a)), p_g / (1.0 + jnp.exp(l_a - l_b)))
        xb = x.astype(BF16)
        hid = [(_silu(_dot(xb, wgu_ref[2 * e])) * _dot(xb, wgu_ref[2 * e + 1]) * w_ab[e]).astype(BF16)
               for e in range(2)]
        _store_chunked(y_ref, _dot(jnp.concatenate(hid, axis=1), wd_ref[...]))


def _moe_call(sched, sorted_x, wr_hi, wr_lo, br, wg, wu, wd):
    tm = MOE_TILE
    ntile = sorted_x.shape[0] // (tm * CHUNKS)
    const2 = lambda i, s: (0, 0)
    anyspec = pl.BlockSpec(memory_space=pl.ANY)
    grid_spec = pltpu.PrefetchScalarGridSpec(
        num_scalar_prefetch=1,
        grid=(ntile,),
        in_specs=[
            pl.BlockSpec((tm * CHUNKS, LANES), lambda i, s: (s[S_BLK, i], 0)),
            pl.BlockSpec((D_MODEL, LANES), const2),
            pl.BlockSpec((D_MODEL, LANES), const2),
            pl.BlockSpec((1, LANES), const2),
            anyspec, anyspec, anyspec,
        ],
        out_specs=pl.BlockSpec((tm * CHUNKS, LANES), lambda i, s: (i, 0)),
        scratch_shapes=[
            pltpu.VMEM((4, D_MODEL, MOE_FF), BF16),
            pltpu.VMEM((2 * MOE_FF, D_MODEL), BF16),
            pltpu.VMEM((2, 4, D_MODEL, MOE_FF), F32),
            pltpu.VMEM((2, 2, MOE_FF, D_MODEL), F32),
            pltpu.SemaphoreType.DMA((2,)),
            pltpu.SMEM((1,), I32),
        ],
    )
    return pl.pallas_call(
        _moe_kernel,
        grid_spec=grid_spec,
        out_shape=jax.ShapeDtypeStruct((ntile * tm * CHUNKS, LANES), F32),
        compiler_params=_params("arbitrary"),
        name="moe",
    )(sched, sorted_x, wr_hi, wr_lo, br, wg, wu, wd)


def _start_row_gather(dest_ref, y_any, buf, sem):
    for j in range(dest_ref.shape[2]):
        d = dest_ref[0, 0, j]
        pltpu.make_async_copy(
            y_any.at[pl.ds(pl.multiple_of(d * CHUNKS, CHUNKS), CHUNKS)],
            buf.at[pl.ds(j * CHUNKS, CHUNKS)], sem).start()


def _wait_row_gather(y_any, buf, sem):
    pltpu.make_async_copy(y_any.at[pl.ds(0, buf.shape[0])], buf, sem).wait()


def _gather_pipelined(step, nstep, dest_ref, dnext_ref, y_any, mbuf, sems):
    slot = step % 2

    @pl.when(step == 0)
    def _():
        _start_row_gather(dest_ref, y_any, mbuf.at[0], sems.at[0])

    _start_row_gather(dnext_ref, y_any, mbuf.at[1 - slot], sems.at[1 - slot])
    _wait_row_gather(y_any, mbuf.at[slot], sems.at[slot])

    def drain():
        @pl.when(step == nstep - 1)
        def _():
            _wait_row_gather(y_any, mbuf.at[1 - slot], sems.at[1 - slot])

    return mbuf.at[slot], drain


def _post_moe_kernel(dest_ref, dnext_ref, x1_ref, y_any, mod_ref, lng_ref, lnb_ref, out_ref,
                     mbuf, sems):
    m_ref, drain = _gather_pipelined(pl.program_id(0), pl.num_programs(0), dest_ref, dnext_ref,
                                     y_any, mbuf, sems)
    out_ref[...] = _block_input(x1_ref, m_ref, mod_ref[0], lng_ref, lnb_ref)
    drain()


def _post_moe_call(x1, dest, y_sorted, mod, lng, lnb, rows, steps_per_mod):
    mrows = mod.shape[1]
    nstep = x1.shape[0] // rows
    dest3 = dest.reshape(nstep, 1, rows)
    return pl.pallas_call(
        _post_moe_kernel,
        grid=(nstep,),
        in_specs=[
            pl.BlockSpec((1, 1, rows), lambda i: (i, 0, 0), memory_space=pltpu.SMEM),
            pl.BlockSpec((1, 1, rows), lambda i: (jnp.minimum(i + 1, nstep - 1), 0, 0),
                         memory_space=pltpu.SMEM),
            pl.BlockSpec((rows, D_MODEL), lambda i: (i, 0)),
            pl.BlockSpec(memory_space=pl.ANY),
            pl.BlockSpec((1, mrows, 3 * D_MODEL), lambda i: (i // steps_per_mod, 0, 0)),
            _full((1, D_MODEL)),
            _full((1, D_MODEL)),
        ],
        out_specs=pl.BlockSpec((rows, D_MODEL), lambda i: (i, 0)),
        out_shape=jax.ShapeDtypeStruct(x1.shape, F32),
        scratch_shapes=[pltpu.VMEM((2, rows * CHUNKS, LANES), F32), pltpu.SemaphoreType.DMA((2,))],
        compiler_params=_params("arbitrary"),
        name="post_moe",
    )(dest3, dest3, x1, y_sorted, mod, lng, lnb)


PLAN_CHUNK = 2048


def _num_tiles(nslot):
    return -(-nslot // MOE_TILE) + N_BUCKETS


def _plan_kernel(rp_ref, rs_ref, cnt_ref, dest_ref, sched_ref, *, layer, ntile):
    tm = MOE_TILE
    lane = lax.broadcasted_iota(I32, (SUBLANES, LANES), 1)
    row = lax.broadcasted_iota(I32, (LANES, LANES), 0)
    col = lax.broadcasted_iota(I32, (LANES, LANES), 1)
    counts = jnp.where(lane < N_BUCKETS, jnp.broadcast_to(cnt_ref[...], (SUBLANES, LANES)), 0.0)
    tiles_b = jnp.floor((counts + float(tm - 1)) * (1.0 / tm))
    tile_end = _dot(tiles_b.astype(BF16), jnp.where(row <= col, 1.0, 0.0).astype(BF16))
    tile_start = tile_end - tiles_b
    start_b = tile_start.astype(BF16)

    def dest_of(route):
        n = route.shape[1]
        bucket, rank = route[R_BUCKET:R_BUCKET + 1, :], route[R_RANK:R_RANK + 1, :]
        b_iota = lax.broadcasted_iota(I32, (LANES, n), 0).astype(F32)
        onehot = jnp.where(b_iota == bucket, 1.0, 0.0).astype(BF16)
        start = _dot(start_b, onehot)[0:1, :]
        return (start * float(tm) + rank).astype(I32)

    ntok_p = rp_ref.shape[1]
    chunk = math.gcd(ntok_p, PLAN_CHUNK)
    for c0 in range(0, ntok_p, chunk):
        dest_ref[:, c0:c0 + chunk] = dest_of(rp_ref[:, c0:c0 + chunk])
    dest_ref[:, ntok_p:] = dest_of(rs_ref[...])

    tile = lane.astype(F32)
    used = jnp.sum(jnp.where(lane == N_BUCKETS - 1, tile_end, 0.0), axis=-1, keepdims=True)
    ti = jnp.minimum(tile, used - 1.0)
    end_col = jnp.transpose(jnp.broadcast_to(tile_end[0:1], (LANES, LANES)))
    start_col = jnp.transpose(jnp.broadcast_to(tile_start[0:1], (LANES, LANES)))
    ti_rows = jnp.broadcast_to(ti[0:1], (LANES, LANES))
    b_of = jnp.sum(jnp.where((row < N_BUCKETS) & (ti_rows >= end_col), 1.0, 0.0),
                   axis=0, keepdims=True)
    start_of = jnp.sum(jnp.where(row.astype(F32) == b_of, start_col, 0.0), axis=0, keepdims=True)
    grp = sum(jnp.where(b_of >= float(g * N_PAIRS), 1.0, 0.0) for g in range(1, MOE_GROUPS))
    pair = b_of - float(N_PAIRS) * grp
    ex_a = sum(jnp.where(pair == float(j), float(PAIR_A[j]), 0.0) for j in range(N_PAIRS))
    ex_b = sum(jnp.where(pair == float(j), float(PAIR_B[j]), 0.0) for j in range(N_PAIRS))
    first_expert = float(layer * N_EXPERTS) + float(MOE_EXPERTS) * grp
    first_lane = float(MOE_GROUPS) + float(MOE_EXPERTS) * grp
    end_of = jnp.sum(jnp.where(row.astype(F32) == b_of, end_col, 0.0), axis=0, keepdims=True)
    valid = jnp.where(tile[0:1] < used[0:1], 1.0, 0.0)
    first = valid * jnp.where(ti[0:1] == start_of, 1.0, 0.0)
    partial = jnp.where((tile[0:1] == end_of - 1.0) | (tile[0:1] >= used[0:1]), 1.0, 0.0)
    zero = jnp.where(tile[0:1] < float(ntile), partial, 0.0)
    e_a, e_b = first_expert + ex_a, first_expert + ex_b

    def at_next(v):
        v_col = jnp.transpose(jnp.broadcast_to(v, (LANES, LANES)))
        return jnp.sum(jnp.where(row.astype(F32) == end_of, v_col, 0.0), axis=0, keepdims=True)

    has_next = valid * jnp.where(end_of < used[0:1], 1.0, 0.0)
    rows = {S_BLK: ti[0:1], S_EA: e_a, S_EB: e_b, S_VALID: valid,
            S_FIRST: first, S_LANE_G: grp, S_LANE_A: first_lane + ex_a, S_LANE_B: first_lane + ex_b,
            S_ZERO: zero, S_NEA: at_next(e_a), S_NEB: at_next(e_b), S_HASNEXT: has_next}
    sub = lax.broadcasted_iota(I32, (SCHED_ROWS, LANES), 0)
    sched = sum(jnp.where(sub == r, jnp.broadcast_to(v, (SCHED_ROWS, LANES)), 0.0)
                for r, v in rows.items())
    sched_ref[...] = sched.astype(I32)


def _moe_plan(route_p, route_s, cnt, layer):
    nslot = route_p.shape[1] + route_s.shape[1]
    assert _num_tiles(nslot) <= LANES
    return pl.pallas_call(
        functools.partial(_plan_kernel, layer=layer, ntile=_num_tiles(nslot)),
        out_shape=[jax.ShapeDtypeStruct((1, nslot), I32),
                   jax.ShapeDtypeStruct((SCHED_ROWS, LANES), I32)],
        compiler_params=pltpu.CompilerParams(vmem_limit_bytes=VMEM_LIMIT),
        name="plan",
    )(route_p, route_s, cnt)


def _router_weights(w_coarse, b_coarse, w_fine, b_fine):
    wf = jnp.transpose(w_fine, (1, 0, 2)).reshape(D_MODEL, N_EXPERTS)
    w = jnp.concatenate([w_coarse, wf], axis=1)
    w = jnp.pad(w, ((0, 0), (0, LANES - w.shape[1])))
    b = jnp.concatenate([b_coarse, b_fine.reshape(N_EXPERTS)])
    b = jnp.pad(b, (0, LANES - b.shape[0])).reshape(1, LANES)
    hi_lo = _hi_lo(w)
    return hi_lo[0], hi_lo[1], b


def _prep_params(ln_g, ln_b, w_in_even, w_a2, b_a, gla_norm_g, conv_w, conv_b, conv_ln_g,
                 conv_ln_b, w_out_even, w_grp_pool, pool_scale, w_out_odd, w_coarse, b_coarse,
                 w_fine, b_fine):
    p = {}
    w_in = w_in_even[0]
    o_q, o_k, o_v, o_g = 0, GLA_KW, 2 * GLA_KW, 2 * GLA_KW + GLA_VW
    o_a = o_g + GLA_VW
    o_u = o_a + GLA_RANK
    w_in_r = jnp.concatenate(
        [w_in[:, o_q:o_a], w_in[:, o_u:o_u + 2 * CONV_CH], w_in[:, o_a:o_u],
         jnp.zeros((D_MODEL, A_PAD - GLA_RANK), F32)], axis=1)
    p["w_in"] = _hi_lo(w_in_r)
    p["w_a2"] = _hi_lo(jnp.pad(w_a2[0], ((0, A_PAD - GLA_RANK), (0, 0))))
    p["b_a"] = b_a[0].reshape(1, GLA_KW)
    p["gng"] = gla_norm_g[0].reshape(1, GLA_VW)
    p["conv_w"] = jnp.pad(conv_w[0], ((0, CONV_W_ROWS - CONV_WIDTH), (0, 0)))
    p["conv_b"] = conv_b[0].reshape(1, CONV_CH)
    p["conv_ln_g"] = conv_ln_g[0].reshape(1, CONV_CH)
    p["conv_ln_b"] = conv_ln_b[0].reshape(1, CONV_CH)
    p["w_out0"] = _hi_lo(w_out_even[0])
    p["w_grp"] = w_grp_pool[0].astype(BF16)
    p["pool_scale"] = pool_scale[0].reshape(1, D_MODEL)
    p["w_out1"] = w_out_odd[0].astype(BF16)
    for layer in range(DEPTH):
        for j in range(2):
            p[f"ln_g{layer}{j}"] = ln_g[layer, j].reshape(1, D_MODEL)
            p[f"ln_b{layer}{j}"] = ln_b[layer, j].reshape(1, D_MODEL)
        hi, lo, b = _router_weights(w_coarse[layer], b_coarse[layer], w_fine[layer], b_fine[layer])
        p[f"wr_hi{layer}"], p[f"wr_lo{layer}"], p[f"br{layer}"] = hi, lo, b
    return p


def kernel(x_prompt, x_sample, state_gla, state_conv, state_pool, c_prompt, c_sample, w_ada, b_ada,
           ln_g, ln_b, w_in_even, w_a2, b_a, gla_norm_g, conv_w, conv_b, conv_ln_g, conv_ln_b,
           w_out_even, w_grp_pool, pool_scale, w_out_odd, w_coarse, b_coarse, w_fine, b_fine,
           w_gate, w_up, w_down):
    bsz, seq, _ = x_prompt.shape
    nsmp = x_sample.shape[0]
    ntok_p = bsz * seq
    assert seq % TILE_L == 0 and nsmp <= TILE_L and nsmp % SAMPLE_BLK == 0 and ntok_p % nsmp == 0
    p = _prep_params(ln_g, ln_b, w_in_even, w_a2, b_a, gla_norm_g, conv_w, conv_b, conv_ln_g,
                     conv_ln_b, w_out_even, w_grp_pool, pool_scale, w_out_odd, w_coarse, b_coarse,
                     w_fine, b_fine)
    wg = w_gate.reshape(DEPTH * N_EXPERTS, D_MODEL, MOE_FF)
    wu = w_up.reshape(DEPTH * N_EXPERTS, D_MODEL, MOE_FF)
    wd = w_down.reshape(DEPTH * N_EXPERTS, MOE_FF, D_MODEL)

    mod_p, mod_s = _ada_call(c_prompt, c_sample, w_ada.reshape(2 * DEPTH, D_MODEL, 3 * D_MODEL),
                             b_ada.reshape(2 * DEPTH, 1, 3 * D_MODEL))
    mod_p = mod_p.reshape(2 * DEPTH, bsz, 1, 3 * D_MODEL)

    def moe(h2c_all, route_p, route_s, cnt, layer):
        dest, sched = _moe_plan(route_p, route_s, cnt, layer)
        sorted_x = _dispatch_call(sched, dest, h2c_all)
        y_sorted = _moe_call(sched, sorted_x, p[f"wr_hi{layer}"], p[f"wr_lo{layer}"],
                             p[f"br{layer}"], wg, wu, wd)
        return dest[0, :ntok_p], dest[0, ntok_p:ntok_p + nsmp], y_sorted

    xs0 = x_sample.reshape(nsmp, D_MODEL)
    cnt0 = jnp.zeros((1, LANES), F32)
    x1s, h2c_s, route_s, cnt, gla_s, conv_s = _mixer0_sample_call(
        xs0, mod_s[0], mod_s[1], state_gla, state_conv, cnt0, p)
    x1p, h2c_all, route_p, cnt, conv_p, gla_p = _mixer0_prompt_call(
        x_prompt, mod_p[0], mod_p[1], cnt, h2c_s, p)
    dest_p, dest_s, y_sorted = moe(h2c_all, route_p, route_s, cnt, 0)
    x3s, h2c_s, route_s, cnt, pool_s = _mixer1_sample_call(
        x1s, dest_s, y_sorted, mod_s[1], p["ln_g01"], p["ln_b01"], mod_s[2], mod_s[3],
        state_pool, cnt0, p)
    x3p, h2c_all, route_p, cnt, pool_p = _mixer1_prompt_call(
        x1p, dest_p, y_sorted, mod_p[1], p["ln_g01"], p["ln_b01"], mod_p[2], mod_p[3], cnt,
        h2c_s, bsz, p)
    dest_p, dest_s, y_sorted = moe(h2c_all, route_p, route_s, cnt, 1)
    x4p = _post_moe_call(x3p, dest_p, y_sorted, mod_p[3], p["ln_g11"], p["ln_b11"], TILE_L,
                         seq // TILE_L).reshape(bsz, seq, D_MODEL)
    x4s = _post_moe_call(x3s, dest_s, y_sorted, mod_s[3][None], p["ln_g11"], p["ln_b11"], nsmp, 1)
    return (x4p, x4s.reshape(nsmp, 1, D_MODEL), gla_p, conv_p, pool_p, gla_s, conv_s, pool_s)
```

```python
import functools
import math

import jax
import jax.numpy as jnp
from jax import lax
from jax.experimental import pallas as pl
from jax.experimental.pallas import tpu as pltpu

F32 = jnp.float32
BF16 = jnp.bfloat16
I32 = jnp.int32

D_MODEL = 1024
GLA_HEADS = 4
GLA_DK = 64
GLA_DV = 128
GLA_KW = GLA_HEADS * GLA_DK
GLA_VW = GLA_HEADS * GLA_DV
GLA_RANK = 16
GLA_TAU = 16.0
CONV_CH = 512
CONV_WIDTH = 31
CONV_BUF = CONV_WIDTH - 1
POOL_WINDOWS = (2, 4, 8, 16)
POOL_GC = D_MODEL // len(POOL_WINDOWS)
POOL_BUF = max(POOL_WINDOWS) - 1
MOE_GROUPS = 4
MOE_EXPERTS = 4
N_EXPERTS = MOE_GROUPS * MOE_EXPERTS
N_PAIRS = MOE_EXPERTS * (MOE_EXPERTS - 1) // 2
N_BUCKETS = MOE_GROUPS * N_PAIRS
PAIR_A = (0, 0, 0, 1, 1, 2)
PAIR_B = (1, 2, 3, 2, 3, 3)
MOE_FF = 512
R_BUCKET, R_RANK = 0, 1
DEPTH = 2
ALPHA = (2 * DEPTH) ** 0.25
LN_EPS = 1e-5

LANES = 128
SUBLANES = 8
CHUNKS = D_MODEL // LANES
A_PAD = LANES
Z_WIDTH = 2 * GLA_KW + 2 * GLA_VW + 2 * CONV_CH + A_PAD
NEG_BIG = -1e30
VMEM_LIMIT = 56 * 1024 * 1024

TILE_L = 256
CONV_HALO = 32
CONV_W_ROWS = 32
POOL_HALO = 16
SAMPLE_BLK = 16
MOE_TILE = 256
DISPATCH_ROWS = 1280


def _dot(a, b):
    return jnp.dot(a, b, preferred_element_type=F32)


def _dot_nt(a, b):
    return lax.dot_general(a, b, (((1,), (1,)), ((), ())), preferred_element_type=F32)


def _split3(x):
    hi = x.astype(BF16)
    r1 = x - hi.astype(F32)
    mid = r1.astype(BF16)
    lo = (r1 - mid.astype(F32)).astype(BF16)
    return hi, mid, lo


def _split2(x):
    hi = x.astype(BF16)
    lo = (x - hi.astype(F32)).astype(BF16)
    return hi, lo


def _dot_w3(a, w_hi, w_lo):
    a_hi, a_lo = _split2(a)
    return _dot(a_hi, w_hi) + (_dot(a_lo, w_hi) + _dot(a_hi, w_lo))


def _lhs3(parts):
    hi, lo = parts
    return jnp.concatenate([hi, lo, hi], axis=1)


def _rhs3_rows(x):
    hi, lo = _split2(x)
    return jnp.concatenate([hi, hi, lo], axis=0)


def _rhs3_lanes(parts):
    hi, lo = parts
    return jnp.concatenate([hi, hi, lo], axis=1)


def _layer_norm(x, g, b):
    mu = jnp.mean(x, axis=-1, keepdims=True)
    xc = x - mu
    var = jnp.mean(xc * xc, axis=-1, keepdims=True)
    return xc * lax.rsqrt(var + LN_EPS) * g + b


def _standardize(x):
    mu = jnp.mean(x, axis=-1, keepdims=True)
    xc = x - mu
    var = jnp.mean(xc * xc, axis=-1, keepdims=True)
    return xc * lax.rsqrt(var + LN_EPS)


def _sigmoid(x):
    return 1.0 / (1.0 + jnp.exp(-x))


def _silu(x):
    return x * _sigmoid(x)


def _log_sigmoid(x):
    return jnp.minimum(x, 0.0) - jnp.log(1.0 + jnp.exp(-jnp.abs(x)))


def _mod3(mod):
    return mod[:, 0:D_MODEL], mod[:, D_MODEL:2 * D_MODEL], mod[:, 2 * D_MODEL:3 * D_MODEL]


def _full(shape):
    nd = len(shape)
    return pl.BlockSpec(shape, lambda *_: (0,) * nd)


def _const(shape):
    nd = len(shape)
    return pl.BlockSpec(shape, lambda *_: (0,) * nd, pipeline_mode=pl.Buffered(1))


def _hi_lo(w):
    bits = lax.bitcast_convert_type(w, jnp.uint32) & jnp.uint32(0xFFFF0000)
    hi = lax.bitcast_convert_type(bits, F32)
    part = lax.broadcasted_iota(I32, (2,) + w.shape, 0)
    return jnp.where(part == 0, hi[None], (w - hi)[None]).astype(BF16)


def _params(*sem):
    return pltpu.CompilerParams(dimension_semantics=sem, vmem_limit_bytes=VMEM_LIMIT)


def _ada_kernel(cp_ref, cs_ref, w_ref, b_ref, op_ref, os_ref):
    w_hi, w_lo = _split2(w_ref[0])
    op_ref[0] = _dot_w3(_silu(cp_ref[...]), w_hi, w_lo) + b_ref[0]
    os_ref[0] = _dot_w3(_silu(cs_ref[...]), w_hi, w_lo) + b_ref[0]


def _ada_call(c_prompt, c_sample, w_ada, b_ada):
    ncomb = w_ada.shape[0]
    tn = D_MODEL
    blk = lambda i, j: (i, 0, j)
    return pl.pallas_call(
        _ada_kernel,
        grid=(ncomb, 3 * D_MODEL // tn),
        in_specs=[
            _full(c_prompt.shape),
            _full(c_sample.shape),
            pl.BlockSpec((1, D_MODEL, tn), blk),
            pl.BlockSpec((1, 1, tn), blk),
        ],
        out_specs=[pl.BlockSpec((1, c_prompt.shape[0], tn), blk),
                   pl.BlockSpec((1, c_sample.shape[0], tn), blk)],
        out_shape=[jax.ShapeDtypeStruct((ncomb, c_prompt.shape[0], 3 * D_MODEL), F32),
                   jax.ShapeDtypeStruct((ncomb, c_sample.shape[0], 3 * D_MODEL), F32)],
        compiler_params=_params("arbitrary", "arbitrary"),
        name="ada",
    )(c_prompt, c_sample, w_ada, b_ada)


def _route(logits, carry):
    rows = logits.shape[0]
    lane = lax.broadcasted_iota(I32, (rows, LANES), 1)
    lanef = lane.astype(F32)
    big = float(LANES)
    lc = jnp.where(lane < MOE_GROUPS, logits, NEG_BIG)
    mc = jnp.max(lc, axis=-1, keepdims=True)
    gidx = jnp.min(jnp.where(lc == mc, lanef, big), axis=-1, keepdims=True)
    lo = float(MOE_GROUPS) + gidx * float(MOE_EXPERTS)
    in_grp = (lanef >= lo) & (lanef < lo + float(MOE_EXPERTS))
    lf = jnp.where(in_grp, logits, NEG_BIG)
    m1 = jnp.max(lf, axis=-1, keepdims=True)
    i1 = jnp.min(jnp.where(lf == m1, lanef, big), axis=-1, keepdims=True)
    lf2 = jnp.where(lanef == i1, NEG_BIG, lf)
    m2 = jnp.max(lf2, axis=-1, keepdims=True)
    i2 = jnp.min(jnp.where(lf2 == m2, lanef, big), axis=-1, keepdims=True)
    ea = jnp.minimum(i1, i2) - lo
    eb = jnp.maximum(i1, i2) - lo
    bucket = gidx * float(N_PAIRS) + ea * (7.0 - ea) * 0.5 + eb - ea - 1.0
    onehot = jnp.where(lanef == bucket, 1.0, 0.0)
    r = lax.broadcasted_iota(I32, (rows, rows), 0)
    c = lax.broadcasted_iota(I32, (rows, rows), 1)
    before = _dot(jnp.where(r > c, 1.0, 0.0).astype(BF16), onehot.astype(BF16))
    rank = jnp.sum(onehot * (before + carry), axis=-1, keepdims=True)
    new_carry = carry + jnp.sum(onehot, axis=0, keepdims=True)
    cols = jnp.where(lane == R_BUCKET, bucket, 0.0) + jnp.where(lane == R_RANK, rank, 0.0)
    return jnp.transpose(cols)[0:SUBLANES, :], new_carry


def _store_chunked(ref, x):
    rows = x.shape[0]
    for c in range(CHUNKS):
        ref[pl.ds(c, rows, stride=CHUNKS), :] = x[:, c * LANES:(c + 1) * LANES]


def _load_chunked(ref, rows, group, first):
    return jnp.concatenate(
        [ref[pl.ds(first + c, rows, stride=group), :] for c in range(CHUNKS)], axis=1)


def _router_logits(h2, wr_hi_ref, wr_lo_ref, br_ref):
    h_hi, h_lo = _split2(h2)
    wr_hi = wr_hi_ref[...]
    return _dot(h_hi, wr_hi) + _dot(h_lo, wr_hi) + _dot(h_hi, wr_lo_ref[...]) + br_ref[...]


def _post_mixer(x, m, gate, lng_ref, lnb_ref, mod_b, wr_hi_ref, wr_lo_ref, br_ref, carry,
                x1_ref, h2c_ref, route_ref):
    x1 = _layer_norm(ALPHA * x + (1.0 + gate) * m, lng_ref[...], lnb_ref[...])
    shift2, scale2, _ = _mod3(mod_b)
    h2 = x1 * (1.0 + scale2) + shift2
    pad = route_ref.shape[1] - x.shape[0]
    if pad:
        h2 = jnp.concatenate([h2, jnp.zeros((pad, D_MODEL), F32)], axis=0)
    route, carry = _route(_router_logits(h2, wr_hi_ref, wr_lo_ref, br_ref), carry)
    x1_ref[...] = x1
    _store_chunked(h2c_ref, h2)
    route_ref[...] = route
    return carry


def _token_out_specs(rows, slot_rows, x1_map, slot_map):
    return [pl.BlockSpec((rows, D_MODEL), x1_map),
            pl.BlockSpec((slot_rows * CHUNKS, LANES), slot_map),
            pl.BlockSpec((SUBLANES, slot_rows), lambda *i: (0, x1_map(*i)[0]))]


def _token_out_shapes(ntok, nslot_h2, nslot_route):
    return [jax.ShapeDtypeStruct((ntok, D_MODEL), F32),
            jax.ShapeDtypeStruct((nslot_h2 * CHUNKS, LANES), F32),
            jax.ShapeDtypeStruct((SUBLANES, nslot_route), F32)]


def _hold_rows(b, s):
    n, w = b.shape
    if s >= SUBLANES:
        pieces = []
        for p in range(n // (2 * s)):
            r = p * 2 * s + s - 1
            pieces.append(jnp.broadcast_to(b[r:r + 1, :], (2 * s, w)))
        return pieces[0] if len(pieces) == 1 else jnp.concatenate(pieces, axis=0)
    b3 = b.reshape(n // SUBLANES, SUBLANES, w)
    sub = lax.broadcasted_iota(I32, b3.shape, 1)

    def bc(r):
        return jnp.broadcast_to(b3[:, r:r + 1, :], b3.shape)

    out = bc(s - 1)
    for p in range(1, SUBLANES // (2 * s)):
        out = jnp.where(sub >= p * 2 * s, bc(p * 2 * s + s - 1), out)
    return out.reshape(n, w)


def _gla_tile(q, k, v, la, s_prev):
    n = q.shape[0]
    row = lax.broadcasted_iota(I32, (n, n), 0)
    col = lax.broadcasted_iota(I32, (n, n), 1)
    tri = jnp.where(row >= col, 1.0, 0.0).astype(BF16)
    hi, mid, lo = _split3(la)
    b = _dot(tri, hi) + _dot(tri, mid) + _dot(tri, lo)
    b_last = b[n - 1:n, :]
    kl = k * jnp.exp(b_last - b)

    rowi = lax.broadcasted_iota(I32, (n, GLA_KW), 0)
    pair_lanes = [slice(p * LANES, (p + 1) * LANES) for p in range(GLA_HEADS // 2)]

    def swapped(x):
        return [pltpu.roll(x[:, sl], GLA_DK, axis=1).astype(BF16) for sl in pair_lanes]

    def level_operands(qf, kf):
        q_hi = qf.astype(BF16)
        k_hi, k_lo = _split2(kf)
        return q_hi, swapped(qf - q_hi.astype(F32)), k_hi, swapped(k_hi.astype(F32)), k_lo

    levels = []
    s = n // 2
    while s >= 1:
        m = _hold_rows(b, s)
        second = (rowi & s) != 0
        levels.append((2 * s, level_operands(q * jnp.exp(jnp.where(second, b - m, NEG_BIG)),
                                             k * jnp.exp(jnp.where(second, NEG_BIG, m - b)))))
        s //= 2
    levels.append((1, level_operands(q, k)))
    qe = _split2(q * jnp.exp(b))

    lane = lax.broadcasted_iota(I32, (n, LANES), 1)
    xor = row ^ col
    zero = jnp.zeros((), BF16)

    def head_part(parts, sl, hm):
        return tuple(jnp.where(hm, x[:, sl], zero) for x in parts)

    outs = []
    for h in range(GLA_HEADS):
        p = h // 2
        sl = pair_lanes[p]
        hm = (lane < GLA_DK) if h % 2 == 0 else (lane >= GLA_DK)
        att = None
        for span, (q_hi, q_lo_sw, k_hi, k_hi_sw, k_lo) in levels:
            lhs = jnp.concatenate([jnp.where(hm, q_hi[:, sl], q_lo_sw[p]),
                                   jnp.where(hm, q_hi[:, sl], zero)], axis=1)
            rhs = jnp.concatenate([jnp.where(hm, k_hi[:, sl], k_hi_sw[p]), k_lo[:, sl]], axis=1)
            a_s = _dot_nt(lhs, rhs)
            att = a_s if att is None else jnp.where(xor < span, a_s, att)
        o_h = _dot(_lhs3(_split2(att)), _rhs3_rows(v[:, h * GLA_DV:(h + 1) * GLA_DV]))
        o_h = o_h + _dot(_lhs3(head_part(qe, sl, hm)), _rhs3_rows(s_prev[sl, :]))
        outs.append(o_h)

    decay = jnp.exp(b_last)
    upper = lax.broadcasted_iota(I32, (LANES, LANES), 0) < GLA_DK
    s_new = []
    for p in range(GLA_HEADS // 2):
        sl = slice(p * LANES, (p + 1) * LANES)
        kl_t = jnp.transpose(kl[:, sl])
        u = _dot(_lhs3(_split2(kl_t)), _rhs3_rows(v[:, p * 2 * GLA_DV:(p + 1) * 2 * GLA_DV]))
        upd = jnp.where(upper, u[:, 0:GLA_DV], u[:, GLA_DV:2 * GLA_DV])
        dcol = jnp.transpose(jnp.broadcast_to(decay[:, sl], (LANES, LANES)))
        s_new.append(dcol * s_prev[sl, :] + upd)
    return outs, jnp.concatenate(s_new, axis=0)


def _split_projection(z):
    c0 = 0
    q = z[:, c0:c0 + GLA_KW] * (GLA_DK ** -0.5); c0 += GLA_KW
    k = z[:, c0:c0 + GLA_KW]; c0 += GLA_KW
    v = z[:, c0:c0 + GLA_VW]; c0 += GLA_VW
    g = z[:, c0:c0 + GLA_VW]; c0 += GLA_VW
    ua = z[:, c0:c0 + CONV_CH]; c0 += CONV_CH
    ug = z[:, c0:c0 + CONV_CH]; c0 += CONV_CH
    a_lr = z[:, c0:c0 + A_PAD]
    return q, k, v, g, ua, ug, a_lr


def _mix0_project(o_heads, g, y, gng_ref, w_out_ref):
    sl = slice(GLA_VW, GLA_VW + CONV_CH)
    m = _dot_w3(y, w_out_ref[0, sl, :], w_out_ref[1, sl, :])
    for hd in range(GLA_HEADS):
        sl = slice(hd * GLA_DV, (hd + 1) * GLA_DV)
        o_h = _standardize(o_heads[hd]) * gng_ref[:, sl] * _silu(g[:, sl])
        m = m + _dot_w3(o_h, w_out_ref[0, sl, :], w_out_ref[1, sl, :])
    return m


def _mix0_inputs(h, w_in_ref, w_a2_ref, b_a_ref):
    z = _dot_w3(h, w_in_ref[0], w_in_ref[1])
    q, k, v, g, ua, ug, a_lr = _split_projection(z)
    la = _log_sigmoid(_dot_w3(a_lr, w_a2_ref[0], w_a2_ref[1]) + b_a_ref[...]) * (1.0 / GLA_TAU)
    return q, k, v, g, ua, ug, la


def _prompt_grid_specs(bsz, nt):
    ntile = bsz * nt
    seq_map = lambda b, t: (jnp.minimum(b, bsz - 1), 0, 0)
    x_map = lambda b, t: (jnp.minimum(b, bsz - 1), t, 0)
    tok_map = lambda b, t: (jnp.minimum(b * nt + t, ntile - 1), 0)
    slot_map = lambda b, t: (jnp.minimum(b * nt + t, ntile), 0)
    state_map = lambda b, t: (0, jnp.minimum(b, bsz - 1), 0, 0)
    return seq_map, x_map, tok_map, slot_map, state_map


def _mixer0_prompt_kernel(x_ref, moda_ref, modb_ref, cnt_in_ref, h2cs_ref, w_in_ref, w_a2_ref,
                          b_a_ref, gng_ref,
                          cw_ref, cb_ref, clg_ref, clb_ref, w_out_ref, lng_ref, lnb_ref,
                          wr_hi_ref, wr_lo_ref, br_ref,
                          x1_ref, h2c_ref, route_ref, cnt_ref, nconv_ref, ngla_ref,
                          s_ref, cbuf_ref, shift_ref, carry_ref):
    nseq = pl.num_programs(0) - 1

    @pl.when(pl.program_id(0) < nseq)
    def _():
        _mixer0_prompt_tile(x_ref, moda_ref, modb_ref, cnt_in_ref, w_in_ref, w_a2_ref, b_a_ref,
                            gng_ref, cw_ref, cb_ref, clg_ref, clb_ref, w_out_ref, lng_ref, lnb_ref,
                            wr_hi_ref, wr_lo_ref, br_ref, x1_ref, h2c_ref, route_ref, cnt_ref,
                            nconv_ref, ngla_ref, s_ref, cbuf_ref, shift_ref, carry_ref)

    @pl.when((pl.program_id(0) == nseq) & (pl.program_id(1) == 0))
    def _():
        h2c_ref[...] = h2cs_ref[...]


def _mixer0_prompt_tile(x_ref, moda_ref, modb_ref, cnt_in_ref, w_in_ref, w_a2_ref, b_a_ref,
                        gng_ref, cw_ref, cb_ref, clg_ref, clb_ref, w_out_ref, lng_ref, lnb_ref,
                        wr_hi_ref, wr_lo_ref, br_ref, x1_ref, h2c_ref, route_ref, cnt_ref,
                        nconv_ref, ngla_ref, s_ref, cbuf_ref, shift_ref, carry_ref):
    t = pl.program_id(1)
    n = x_ref.shape[1]

    @pl.when((t == 0) & (pl.program_id(0) == 0))
    def _():
        carry_ref[...] = cnt_in_ref[...]

    @pl.when(t == 0)
    def _():
        s_ref[...] = jnp.zeros_like(s_ref)
        cbuf_ref[0:CONV_HALO, :] = jnp.zeros((CONV_HALO, CONV_CH), F32)

    x = x_ref[0]
    shift, scale, gate = _mod3(moda_ref[0])
    h = x * (1.0 + scale) + shift
    q, k, v, g, ua, ug, la = _mix0_inputs(h, w_in_ref, w_a2_ref, b_a_ref)
    o_heads, s_new = _gla_tile(q, k, v, la, s_ref[...])
    s_ref[...] = s_new

    glu = ua * _sigmoid(ug)
    cbuf_ref[CONV_HALO:CONV_HALO + n, :] = glu
    acc = jnp.broadcast_to(cb_ref[...], (n, CONV_CH))
    base = CONV_HALO - CONV_BUF
    for r in range(SUBLANES):
        taps = range(r, CONV_WIDTH, SUBLANES)
        span = n + (len(taps) - 1) * SUBLANES
        if (base + r) % SUBLANES == 0:
            src, off = cbuf_ref, base + r
        else:
            shift_ref[r, 0:span, :] = cbuf_ref[base + r:base + r + span, :]
            src, off = shift_ref.at[r], 0
        for a, j in enumerate(taps):
            lo = off + a * SUBLANES
            acc = acc + cw_ref[j:j + 1, :] * src[lo:lo + n, :]
    y = _silu(_layer_norm(acc, clg_ref[...], clb_ref[...]))

    @pl.when(t == pl.num_programs(1) - 1)
    def _():
        nconv_ref[0, 0] = cbuf_ref[CONV_HALO + n - CONV_BUF:CONV_HALO + n, :]
        ngla_ref[0, 0] = s_new.reshape(GLA_HEADS, GLA_DK, GLA_DV)

    cbuf_ref[0:CONV_HALO, :] = cbuf_ref[n:n + CONV_HALO, :]

    m = _mix0_project(o_heads, g, y, gng_ref, w_out_ref)
    carry = _post_mixer(x, m, gate, lng_ref, lnb_ref, modb_ref[0], wr_hi_ref, wr_lo_ref, br_ref,
                        carry_ref[...], x1_ref, h2c_ref, route_ref)
    carry_ref[...] = carry
    cnt_ref[...] = carry


def _mixer0_prompt_call(x, mod_a, mod_b, cnt, h2c_s, p):
    bsz, seq, _ = x.shape
    tl = TILE_L
    nt = seq // tl
    seq_map, x_map, tok, slot, state_map = _prompt_grid_specs(bsz, nt)
    return pl.pallas_call(
        _mixer0_prompt_kernel,
        grid=(bsz + 1, nt),
        in_specs=[
            pl.BlockSpec((1, tl, D_MODEL), x_map),
            pl.BlockSpec((1, 1, 3 * D_MODEL), seq_map),
            pl.BlockSpec((1, 1, 3 * D_MODEL), seq_map),
            _full((1, LANES)),
            _full((tl * CHUNKS, LANES)),
            _const((2, D_MODEL, Z_WIDTH)),
            _const((2, A_PAD, GLA_KW)),
            _full((1, GLA_KW)),
            _full((1, GLA_VW)),
            _full((CONV_W_ROWS, CONV_CH)),
            _full((1, CONV_CH)),
            _full((1, CONV_CH)),
            _full((1, CONV_CH)),
            _const((2, GLA_VW + CONV_CH, D_MODEL)),
            _full((1, D_MODEL)),
            _full((1, D_MODEL)),
            _full((D_MODEL, LANES)),
            _full((D_MODEL, LANES)),
            _full((1, LANES)),
        ],
        out_specs=_token_out_specs(tl, tl, tok, slot) + [
            _full((1, LANES)),
            pl.BlockSpec((1, 1, CONV_BUF, CONV_CH), state_map),
            pl.BlockSpec((1, 1, GLA_HEADS, GLA_DK, GLA_DV), lambda b, t: state_map(b, t) + (0,)),
        ],
        out_shape=_token_out_shapes(bsz * seq, bsz * seq + tl, bsz * seq) + [
            jax.ShapeDtypeStruct((1, LANES), F32),
            jax.ShapeDtypeStruct((1, bsz, CONV_BUF, CONV_CH), F32),
            jax.ShapeDtypeStruct((1, bsz, GLA_HEADS, GLA_DK, GLA_DV), F32),
        ],
        scratch_shapes=[
            pltpu.VMEM((GLA_KW, GLA_DV), F32),
            pltpu.VMEM((CONV_HALO + tl, CONV_CH), F32),
            pltpu.VMEM((SUBLANES, CONV_HALO + tl, CONV_CH), F32),
            pltpu.VMEM((1, LANES), F32),
        ],
        compiler_params=_params("arbitrary", "arbitrary"),
        name="mixer0_prompt",
    )(x, mod_a, mod_b, cnt, h2c_s, p["w_in"], p["w_a2"], p["b_a"], p["gng"], p["conv_w"], p["conv_b"],
      p["conv_ln_g"], p["conv_ln_b"], p["w_out0"], p["ln_g00"], p["ln_b00"],
      p["wr_hi0"], p["wr_lo0"], p["br0"])


def _mixer0_sample_kernel(x_ref, moda_ref, modb_ref, sgla_ref, sconv_ref, cnt_in_ref,
                          w_in_ref, w_a2_ref,
                          b_a_ref, gng_ref, cw_ref, cb_ref, clg_ref, clb_ref, w_out_ref, lng_ref,
                          lnb_ref, wr_hi_ref, wr_lo_ref, br_ref,
                          x1_ref, h2c_ref, route_ref, cnt_ref, ngla_ref, nconv_ref,
                          zt_ref, v_ref, g_ref, glu_ref, o_ref, y_ref):
    i = pl.program_id(0)
    nb = sgla_ref.shape[1]
    ntok = x_ref.shape[0]

    @pl.when(i == 0)
    def _():
        shift, scale, _ = _mod3(moda_ref[...])
        h = x_ref[...] * (1.0 + scale) + shift
        q, k, v, g, ua, ug, la = _mix0_inputs(h, w_in_ref, w_a2_ref, b_a_ref)
        v_ref[...] = v
        g_ref[...] = g
        glu_ref[...] = ua * _sigmoid(ug)
        for j, val in enumerate((jnp.exp(la), k, q)):
            hi, lo = _split2(jnp.transpose(val))
            zt_ref[(2 * j) * GLA_KW:(2 * j + 1) * GLA_KW, :] = hi
            zt_ref[(2 * j + 1) * GLA_KW:(2 * j + 2) * GLA_KW, :] = lo

    tok_row = lax.broadcasted_iota(I32, (ntok, LANES), 0)
    blk = pl.ds(pl.multiple_of(i * nb, nb), nb)
    v_blk = v_ref[blk, :]
    glu_blk = glu_ref[blk, :]
    o_rows, y_rows = [], []
    for n in range(nb):
        onehot = jnp.where(tok_row == i * nb + n, 1.0, 0.0).astype(BF16)
        cols = _dot(zt_ref[...], onehot)
        a_col = cols[0:GLA_KW] + cols[GLA_KW:2 * GLA_KW]
        k_col = cols[2 * GLA_KW:3 * GLA_KW] + cols[3 * GLA_KW:4 * GLA_KW]
        q_col = cols[4 * GLA_KW:5 * GLA_KW] + cols[5 * GLA_KW:6 * GLA_KW]
        v_row = v_blk[n:n + 1, :]
        v_b = jnp.concatenate(
            [jnp.broadcast_to(v_row[:, h * GLA_DV:(h + 1) * GLA_DV], (GLA_DK, GLA_DV))
             for h in range(GLA_HEADS)], axis=0)
        s_old = sgla_ref[0, n].reshape(GLA_KW, GLA_DV)
        s_new = a_col * s_old + k_col * v_b
        ngla_ref[0, n] = s_new.reshape(GLA_HEADS, GLA_DK, GLA_DV)
        o4 = jnp.sum((q_col * s_new).reshape(GLA_HEADS, GLA_DK, GLA_DV), axis=1)
        o_rows.append(jnp.concatenate([o4[h:h + 1, :] for h in range(GLA_HEADS)], axis=1))
        glu_row = glu_blk[n:n + 1, :]
        past = sconv_ref[0, n]
        y_rows.append(jnp.sum(past * cw_ref[0:CONV_BUF, :], axis=0, keepdims=True)
                      + glu_row * cw_ref[CONV_BUF:CONV_WIDTH, :] + cb_ref[...])
        nconv_ref[0, n, 0:CONV_BUF - 1, :] = sconv_ref[0, n, 1:CONV_BUF, :]
        nconv_ref[0, n, CONV_BUF - 1:CONV_BUF, :] = glu_row
    o_ref[blk, :] = jnp.concatenate(o_rows, axis=0)
    y_ref[blk, :] = jnp.concatenate(y_rows, axis=0)

    @pl.when(i == pl.num_programs(0) - 1)
    def _():
        y = _silu(_layer_norm(y_ref[...], clg_ref[...], clb_ref[...]))
        o_heads = [o_ref[:, hd * GLA_DV:(hd + 1) * GLA_DV] for hd in range(GLA_HEADS)]
        m = _mix0_project(o_heads, g_ref[...], y, gng_ref, w_out_ref)
        _, _, gate = _mod3(moda_ref[...])
        cnt_ref[...] = _post_mixer(x_ref[...], m, gate, lng_ref, lnb_ref, modb_ref[...], wr_hi_ref,
                                   wr_lo_ref, br_ref, cnt_in_ref[...], x1_ref, h2c_ref, route_ref)


def _mixer0_sample_call(x, mod_a, mod_b, state_gla, state_conv, cnt, p):
    ntok = x.shape[0]
    nb = SAMPLE_BLK
    tok = lambda i: (0, 0)
    return pl.pallas_call(
        _mixer0_sample_kernel,
        grid=(ntok // nb,),
        in_specs=[
            _full((ntok, D_MODEL)),
            _full((ntok, 3 * D_MODEL)),
            _full((ntok, 3 * D_MODEL)),
            pl.BlockSpec((1, nb, GLA_HEADS, GLA_DK, GLA_DV), lambda i: (0, i, 0, 0, 0)),
            pl.BlockSpec((1, nb, CONV_BUF, CONV_CH), lambda i: (0, i, 0, 0)),
            _full((1, LANES)),
            _const((2, D_MODEL, Z_WIDTH)),
            _const((2, A_PAD, GLA_KW)),
            _full((1, GLA_KW)),
            _full((1, GLA_VW)),
            _full((CONV_W_ROWS, CONV_CH)),
            _full((1, CONV_CH)),
            _full((1, CONV_CH)),
            _full((1, CONV_CH)),
            _const((2, GLA_VW + CONV_CH, D_MODEL)),
            _full((1, D_MODEL)),
            _full((1, D_MODEL)),
            _full((D_MODEL, LANES)),
            _full((D_MODEL, LANES)),
            _full((1, LANES)),
        ],
        out_specs=_token_out_specs(ntok, TILE_L, tok, tok) + [
            _full((1, LANES)),
            pl.BlockSpec((1, nb, GLA_HEADS, GLA_DK, GLA_DV), lambda i: (0, i, 0, 0, 0)),
            pl.BlockSpec((1, nb, CONV_BUF, CONV_CH), lambda i: (0, i, 0, 0)),
        ],
        out_shape=_token_out_shapes(ntok, TILE_L, TILE_L) + [
            jax.ShapeDtypeStruct((1, LANES), F32),
            jax.ShapeDtypeStruct(state_gla.shape, F32),
            jax.ShapeDtypeStruct(state_conv.shape, F32),
        ],
        scratch_shapes=[
            pltpu.VMEM((6 * GLA_KW, ntok), BF16),
            pltpu.VMEM((ntok, GLA_VW), F32),
            pltpu.VMEM((ntok, GLA_VW), F32),
            pltpu.VMEM((ntok, CONV_CH), F32),
            pltpu.VMEM((ntok, GLA_VW), F32),
            pltpu.VMEM((ntok, CONV_CH), F32),
        ],
        compiler_params=_params("arbitrary"),
        name="mixer0_sample",
    )(x, mod_a, mod_b, state_gla, state_conv, cnt, p["w_in"], p["w_a2"], p["b_a"],
      p["gng"], p["conv_w"], p["conv_b"], p["conv_ln_g"], p["conv_ln_b"], p["w_out0"],
      p["ln_g00"], p["ln_b00"], p["wr_hi0"], p["wr_lo0"], p["br0"])


def _pool_project(pooled, h, wg_ref, ps_ref, w_out_ref):
    m = None
    for gi in range(len(POOL_WINDOWS)):
        sl = slice(gi * POOL_GC, (gi + 1) * POOL_GC)
        mixed = _dot((pooled[:, sl] - h[:, sl]).astype(BF16), wg_ref[gi]) * ps_ref[:, sl]
        part = _dot(mixed.astype(BF16), w_out_ref[sl, :])
        m = part if m is None else m + part
    return m


def _block_input(xin_ref, m_ref, mod_m, lnpg_ref, lnpb_ref):
    m = _load_chunked(m_ref, xin_ref.shape[0], CHUNKS, 0)
    _, _, gate = _mod3(mod_m)
    return _layer_norm(ALPHA * xin_ref[...] + (1.0 + gate) * m, lnpg_ref[...], lnpb_ref[...])


def _mixer1_prompt_kernel(dest_ref, dnext_ref, xin_ref, y_any, modm_ref, lnpg_ref, lnpb_ref,
                          moda_ref, modb_ref, cnt_in_ref, h2cs_ref, wg_ref, ps_ref,
                          w_out_ref, lng_ref, lnb_ref, wr_hi_ref, wr_lo_ref, br_ref,
                          x1_ref, h2c_ref, route_ref, cnt_ref, npool_ref,
                          pbuf_ref, carry_ref, mbuf, sems):
    nseq = pl.num_programs(0) - 1
    nt = pl.num_programs(1)

    @pl.when(pl.program_id(0) < nseq)
    def _():
        m_ref, drain = _gather_pipelined(pl.program_id(0) * nt + pl.program_id(1), nseq * nt,
                                         dest_ref, dnext_ref, y_any, mbuf, sems)
        x = _block_input(xin_ref, m_ref, modm_ref[0], lnpg_ref, lnpb_ref)
        _mixer1_prompt_tile(x, moda_ref, modb_ref, cnt_in_ref, wg_ref, ps_ref, w_out_ref,
                            lng_ref, lnb_ref, wr_hi_ref, wr_lo_ref, br_ref, x1_ref, h2c_ref,
                            route_ref, cnt_ref, npool_ref, pbuf_ref, carry_ref)
        drain()

    @pl.when((pl.program_id(0) == nseq) & (pl.program_id(1) == 0))
    def _():
        h2c_ref[...] = h2cs_ref[...]


def _mixer1_prompt_tile(x, moda_ref, modb_ref, cnt_in_ref, wg_ref, ps_ref, w_out_ref,
                        lng_ref, lnb_ref, wr_hi_ref, wr_lo_ref, br_ref, x1_ref, h2c_ref,
                        route_ref, cnt_ref, npool_ref, pbuf_ref, carry_ref):
    t = pl.program_id(1)
    n = x.shape[0]

    @pl.when((t == 0) & (pl.program_id(0) == 0))
    def _():
        carry_ref[...] = cnt_in_ref[...]

    @pl.when(t == 0)
    def _():
        pbuf_ref[0:POOL_HALO, :] = jnp.zeros((POOL_HALO, D_MODEL), F32)

    shift, scale, gate = _mod3(moda_ref[0])
    h = x * (1.0 + scale) + shift
    pbuf_ref[POOL_HALO:POOL_HALO + n, :] = h

    cur = pbuf_ref[...]
    sums = []
    for gi, w in enumerate(POOL_WINDOWS):
        cur = cur + pltpu.roll(cur, w // 2, axis=0)
        sums.append(cur[POOL_HALO:, 0:POOL_GC])
        if gi + 1 < len(POOL_WINDOWS):
            cur = cur[:, POOL_GC:]
    pos = lax.broadcasted_iota(I32, (n, POOL_GC), 0) + t * n
    pooled = jnp.concatenate(
        [s / jnp.minimum(w, pos + 1).astype(F32) for s, w in zip(sums, POOL_WINDOWS)], axis=1)

    @pl.when(t == pl.num_programs(1) - 1)
    def _():
        npool_ref[0, 0] = pbuf_ref[POOL_HALO + n - POOL_BUF:POOL_HALO + n, :]

    pbuf_ref[0:POOL_HALO, :] = pbuf_ref[n:n + POOL_HALO, :]

    m = _pool_project(pooled, h, wg_ref, ps_ref, w_out_ref)
    carry = _post_mixer(x, m, gate, lng_ref, lnb_ref, modb_ref[0], wr_hi_ref, wr_lo_ref, br_ref,
                        carry_ref[...], x1_ref, h2c_ref, route_ref)
    carry_ref[...] = carry
    cnt_ref[...] = carry


def _mixer1_prompt_call(x1_prev, dest, y_sorted, mod_m, lnpg, lnpb, mod_a, mod_b, cnt, h2c_s, bsz, p):
    seq = x1_prev.shape[0] // bsz
    tl = TILE_L
    nt = seq // tl
    ntile = bsz * nt
    seq_map, _, tok, slot, state_map = _prompt_grid_specs(bsz, nt)
    ng = len(POOL_WINDOWS)
    return pl.pallas_call(
        _mixer1_prompt_kernel,
        grid=(bsz + 1, nt),
        in_specs=[
            pl.BlockSpec((1, 1, tl), lambda b, t: tok(b, t) + (0,), memory_space=pltpu.SMEM),
            pl.BlockSpec((1, 1, tl), lambda b, t: (jnp.minimum(b * nt + t + 1, ntile - 1), 0, 0),
                         memory_space=pltpu.SMEM),
            pl.BlockSpec((tl, D_MODEL), tok),
            pl.BlockSpec(memory_space=pl.ANY),
            pl.BlockSpec((1, 1, 3 * D_MODEL), seq_map),
            _full((1, D_MODEL)),
            _full((1, D_MODEL)),
            pl.BlockSpec((1, 1, 3 * D_MODEL), seq_map),
            pl.BlockSpec((1, 1, 3 * D_MODEL), seq_map),
            _full((1, LANES)),
            _full((tl * CHUNKS, LANES)),
            _full((ng, POOL_GC, POOL_GC)),
            _full((1, D_MODEL)),
            _full((D_MODEL, D_MODEL)),
            _full((1, D_MODEL)),
            _full((1, D_MODEL)),
            _full((D_MODEL, LANES)),
            _full((D_MODEL, LANES)),
            _full((1, LANES)),
        ],
        out_specs=_token_out_specs(tl, tl, tok, slot) + [
            _full((1, LANES)),
            pl.BlockSpec((1, 1, POOL_BUF, D_MODEL), state_map),
        ],
        out_shape=_token_out_shapes(bsz * seq, bsz * seq + tl, bsz * seq) + [
            jax.ShapeDtypeStruct((1, LANES), F32),
            jax.ShapeDtypeStruct((1, bsz, POOL_BUF, D_MODEL), F32),
        ],
        scratch_shapes=[pltpu.VMEM((POOL_HALO + tl, D_MODEL), F32), pltpu.VMEM((1, LANES), F32),
                        pltpu.VMEM((2, tl * CHUNKS, LANES), F32), pltpu.SemaphoreType.DMA((2,))],
        compiler_params=_params("arbitrary", "arbitrary"),
        name="mixer1_prompt",
    )(dest.reshape(ntile, 1, tl), dest.reshape(ntile, 1, tl), x1_prev, y_sorted, mod_m, lnpg, lnpb,
      mod_a, mod_b, cnt, h2c_s,
      p["w_grp"], p["pool_scale"], p["w_out1"], p["ln_g10"], p["ln_b10"],
      p["wr_hi1"], p["wr_lo1"], p["br1"])


def _mixer1_sample_kernel(dest_ref, xin_ref, y_any, modm_ref, lnpg_ref, lnpb_ref, moda_ref,
                          modb_ref, spool_ref, cnt_in_ref, wg_ref, ps_ref, w_out_ref,
                          lng_ref, lnb_ref, wr_hi_ref, wr_lo_ref, br_ref,
                          x1_ref, h2c_ref, route_ref, cnt_ref, npool_ref,
                          x_ref, h_ref, pooled_ref, mbuf, sem):
    i = pl.program_id(0)
    nb = spool_ref.shape[1]

    @pl.when(i == 0)
    def _():
        _start_row_gather(dest_ref, y_any, mbuf, sem)
        _wait_row_gather(y_any, mbuf, sem)
        x_ref[...] = _block_input(xin_ref, mbuf, modm_ref[...], lnpg_ref, lnpb_ref)
        shift, scale, _ = _mod3(moda_ref[...])
        h_ref[...] = x_ref[...] * (1.0 + scale) + shift

    lane = lax.broadcasted_iota(I32, (POOL_BUF, D_MODEL), 1)
    rowi = lax.broadcasted_iota(I32, (POOL_BUF, D_MODEL), 0)
    first = jnp.zeros((POOL_BUF, D_MODEL), I32)
    lane1 = lax.broadcasted_iota(I32, (1, D_MODEL), 1)
    inv_w = jnp.zeros((1, D_MODEL), F32)
    for gi, w in enumerate(POOL_WINDOWS):
        in_g = (lane >= gi * POOL_GC) & (lane < (gi + 1) * POOL_GC)
        first = jnp.where(in_g, POOL_BUF - (w - 1), first)
        in_g1 = (lane1 >= gi * POOL_GC) & (lane1 < (gi + 1) * POOL_GC)
        inv_w = jnp.where(in_g1, 1.0 / w, inv_w)
    keep = rowi >= first

    blk = pl.ds(pl.multiple_of(i * nb, nb), nb)
    h_blk = h_ref[blk, :]
    rows = []
    for n in range(nb):
        h_row = h_blk[n:n + 1, :]
        past = spool_ref[0, n]
        total = jnp.sum(jnp.where(keep, past, 0.0), axis=0, keepdims=True) + h_row
        rows.append(total * inv_w)
        npool_ref[0, n, 0:POOL_BUF - 1, :] = spool_ref[0, n, 1:POOL_BUF, :]
        npool_ref[0, n, POOL_BUF - 1:POOL_BUF, :] = h_row
    pooled_ref[blk, :] = jnp.concatenate(rows, axis=0)

    @pl.when(i == pl.num_programs(0) - 1)
    def _():
        m = _pool_project(pooled_ref[...], h_ref[...], wg_ref, ps_ref, w_out_ref)
        _, _, gate = _mod3(moda_ref[...])
        cnt_ref[...] = _post_mixer(x_ref[...], m, gate, lng_ref, lnb_ref, modb_ref[...], wr_hi_ref,
                                   wr_lo_ref, br_ref, cnt_in_ref[...], x1_ref, h2c_ref, route_ref)


def _mixer1_sample_call(x1_prev, dest, y_sorted, mod_m, lnpg, lnpb, mod_a, mod_b, state_pool,
                        cnt, p):
    ntok = x1_prev.shape[0]
    nb = SAMPLE_BLK
    ng = len(POOL_WINDOWS)
    tok = lambda i: (0, 0)
    return pl.pallas_call(
        _mixer1_sample_kernel,
        grid=(ntok // nb,),
        in_specs=[
            pl.BlockSpec((1, 1, ntok), lambda i: (0, 0, 0), memory_space=pltpu.SMEM),
            _full((ntok, D_MODEL)),
            pl.BlockSpec(memory_space=pl.ANY),
            _full((ntok, 3 * D_MODEL)),
            _full((1, D_MODEL)),
            _full((1, D_MODEL)),
            _full((ntok, 3 * D_MODEL)),
            _full((ntok, 3 * D_MODEL)),
            pl.BlockSpec((1, nb, POOL_BUF, D_MODEL), lambda i: (0, i, 0, 0)),
            _full((1, LANES)),
            _full((ng, POOL_GC, POOL_GC)),
            _full((1, D_MODEL)),
            _full((D_MODEL, D_MODEL)),
            _full((1, D_MODEL)),
            _full((1, D_MODEL)),
            _full((D_MODEL, LANES)),
            _full((D_MODEL, LANES)),
            _full((1, LANES)),
        ],
        out_specs=_token_out_specs(ntok, TILE_L, tok, tok) + [
            _full((1, LANES)),
            pl.BlockSpec((1, nb, POOL_BUF, D_MODEL), lambda i: (0, i, 0, 0)),
        ],
        out_shape=_token_out_shapes(ntok, TILE_L, TILE_L) + [
            jax.ShapeDtypeStruct((1, LANES), F32),
            jax.ShapeDtypeStruct(state_pool.shape, F32),
        ],
        scratch_shapes=[pltpu.VMEM((ntok, D_MODEL), F32)] * 3 + [
            pltpu.VMEM((ntok * CHUNKS, LANES), F32), pltpu.SemaphoreType.DMA],
        compiler_params=_params("arbitrary"),
        name="mixer1_sample",
    )(dest.reshape(1, 1, ntok), x1_prev, y_sorted, mod_m, lnpg, lnpb, mod_a, mod_b, state_pool, cnt,
      p["w_grp"], p["pool_scale"], p["w_out1"],
      p["ln_g10"], p["ln_b10"], p["wr_hi1"], p["wr_lo1"], p["br1"])


def _dispatch_kernel(sched_ref, dest_ref, src_ref, dst_any, zeros_ref, sem, zsem):
    n = dest_ref.shape[2]
    tile_rows = zeros_ref.shape[0]

    @pl.when(pl.program_id(0) == 0)
    def _():
        zeros_ref[...] = jnp.zeros_like(zeros_ref)

        def fill(i):
            return pltpu.make_async_copy(
                zeros_ref, dst_any.at[pl.ds(i * tile_rows, tile_rows)], zsem)

        for i in range(dst_any.shape[0] // tile_rows):
            pl.when(sched_ref[S_ZERO, i] == 1)(lambda i=i: fill(i).start())
        for i in range(dst_any.shape[0] // tile_rows):
            pl.when(sched_ref[S_ZERO, i] == 1)(lambda i=i: fill(i).wait())

    for j in range(n):
        d = dest_ref[0, 0, j]
        pltpu.make_async_copy(
            src_ref.at[pl.ds(j * CHUNKS, CHUNKS)],
            dst_any.at[pl.ds(pl.multiple_of(d * CHUNKS, CHUNKS), CHUNKS)],
            sem).start(priority=j % 2)
    pltpu.make_async_copy(src_ref, dst_any.at[pl.ds(0, n * CHUNKS)], sem).wait()


def _dispatch_call(sched, dest, h2c_all):
    nslot = dest.shape[1]
    rows = DISPATCH_ROWS
    tile_rows = MOE_TILE * CHUNKS
    grid_spec = pltpu.PrefetchScalarGridSpec(
        num_scalar_prefetch=1,
        grid=(nslot // rows,),
        in_specs=[
            pl.BlockSpec((1, 1, rows), lambda i, s: (i, 0, 0), memory_space=pltpu.SMEM),
            pl.BlockSpec((rows * CHUNKS, LANES), lambda i, s: (i, 0)),
        ],
        out_specs=pl.BlockSpec(memory_space=pl.ANY),
        scratch_shapes=[pltpu.VMEM((tile_rows, LANES), F32), pltpu.SemaphoreType.DMA,
                        pltpu.SemaphoreType.DMA],
    )
    return pl.pallas_call(
        _dispatch_kernel,
        grid_spec=grid_spec,
        out_shape=jax.ShapeDtypeStruct((_num_tiles(nslot) * tile_rows, LANES), F32),
        compiler_params=_params("arbitrary"),
        name="dispatch",
    )(sched, dest.reshape(nslot // rows, 1, rows), h2c_all)


(S_BLK, S_EA, S_EB, S_VALID, S_FIRST, S_LANE_G, S_LANE_A, S_LANE_B, S_ZERO,
 S_NEA, S_NEB, S_HASNEXT) = range(12)
SCHED_ROWS = 2 * SUBLANES


def _moe_kernel(sched_ref, x_ref, wr_hi_ref, wr_lo_ref, br_ref, wg_any, wu_any, wd_any,
                y_ref, wgu_ref, wd_ref, wbuf_gu, wbuf_d, wsem, wslot_ref):
    i = pl.program_id(0)
    tm = x_ref.shape[0] // CHUNKS

    def weight_copies(ea, eb, slot):
        srcs = ((wg_any, ea), (wu_any, ea), (wg_any, eb), (wu_any, eb))
        cps = [pltpu.make_async_copy(w.at[e], wbuf_gu.at[slot, j], wsem.at[slot])
               for j, (w, e) in enumerate(srcs)]
        return cps + [pltpu.make_async_copy(wd_any.at[e], wbuf_d.at[slot, j], wsem.at[slot])
                      for j, e in enumerate((ea, eb))]

    @pl.when(i == 0)
    def _():
        wslot_ref[0] = 0
        for cp in weight_copies(sched_ref[S_EA, 0], sched_ref[S_EB, 0], 0):
            cp.start()

    @pl.when(sched_ref[S_FIRST, i] == 1)
    def _():
        slot = wslot_ref[0]
        for cp in weight_copies(sched_ref[S_EA, i], sched_ref[S_EB, i], slot):
            cp.wait()

        @pl.when(sched_ref[S_HASNEXT, i] == 1)
        def _():
            for cp in weight_copies(sched_ref[S_NEA, i], sched_ref[S_NEB, i], 1 - slot):
                cp.start()

        for e in range(2):
            wgu_ref[2 * e] = wbuf_gu[slot, 2 * e].astype(BF16)
            wgu_ref[2 * e + 1] = wbuf_gu[slot, 2 * e + 1].astype(BF16)
            wd_ref[e * MOE_FF:(e + 1) * MOE_FF, :] = wbuf_d[slot, e].astype(BF16)
        wslot_ref[0] = 1 - slot

    @pl.when(sched_ref[S_VALID, i] == 0)
    def _():
        y_ref[...] = jnp.zeros_like(y_ref)

    @pl.when(sched_ref[S_VALID, i] == 1)
    def _():
        x = _load_chunked(x_ref, tm, CHUNKS, 0)
        logits = _router_logits(x, wr_hi_ref, wr_lo_ref, br_ref)
        lane = lax.broadcasted_iota(I32, logits.shape, 1)

        def pick(row):
            return jnp.sum(jnp.where(lane == sched_ref[row, i], logits, 0.0), axis=-1, keepdims=True)

        l_g, l_a, l_b = pick(S_LANE_G), pick(S_LANE_A), pick(S_LANE_B)
        p_g = 1.0 / jnp.sum(jnp.where(lane < MOE_GROUPS, jnp.exp(logits - l_g), 0.0),
                            axis=-1, keepdims=True)
        w_ab = (p_g / (1.0 + jnp.exp(l_b - l_a)), p_g / (1.0 + jnp.exp(l_a - l_b)))
        xb = x.astype(BF16)
        hid = [(_silu(_dot(xb, wgu_ref[2 * e])) * _dot(xb, wgu_ref[2 * e + 1]) * w_ab[e]).astype(BF16)
               for e in range(2)]
        _store_chunked(y_ref, _dot(jnp.concatenate(hid, axis=1), wd_ref[...]))


def _moe_call(sched, sorted_x, wr_hi, wr_lo, br, wg, wu, wd):
    tm = MOE_TILE
    ntile = sorted_x.shape[0] // (tm * CHUNKS)
    const2 = lambda i, s: (0, 0)
    anyspec = pl.BlockSpec(memory_space=pl.ANY)
    grid_spec = pltpu.PrefetchScalarGridSpec(
        num_scalar_prefetch=1,
        grid=(ntile,),
        in_specs=[
            pl.BlockSpec((tm * CHUNKS, LANES), lambda i, s: (s[S_BLK, i], 0)),
            pl.BlockSpec((D_MODEL, LANES), const2),
            pl.BlockSpec((D_MODEL, LANES), const2),
            pl.BlockSpec((1, LANES), const2),
            anyspec, anyspec, anyspec,
        ],
        out_specs=pl.BlockSpec((tm * CHUNKS, LANES), lambda i, s: (i, 0)),
        scratch_shapes=[
            pltpu.VMEM((4, D_MODEL, MOE_FF), BF16),
            pltpu.VMEM((2 * MOE_FF, D_MODEL), BF16),
            pltpu.VMEM((2, 4, D_MODEL, MOE_FF), F32),
            pltpu.VMEM((2, 2, MOE_FF, D_MODEL), F32),
            pltpu.SemaphoreType.DMA((2,)),
            pltpu.SMEM((1,), I32),
        ],
    )
    return pl.pallas_call(
        _moe_kernel,
        grid_spec=grid_spec,
        out_shape=jax.ShapeDtypeStruct((ntile * tm * CHUNKS, LANES), F32),
        compiler_params=_params("arbitrary"),
        name="moe",
    )(sched, sorted_x, wr_hi, wr_lo, br, wg, wu, wd)


def _start_row_gather(dest_ref, y_any, buf, sem):
    for j in range(dest_ref.shape[2]):
        d = dest_ref[0, 0, j]
        pltpu.make_async_copy(
            y_any.at[pl.ds(pl.multiple_of(d * CHUNKS, CHUNKS), CHUNKS)],
            buf.at[pl.ds(j * CHUNKS, CHUNKS)], sem).start(priority=j % 2)


def _wait_row_gather(y_any, buf, sem):
    pltpu.make_async_copy(y_any.at[pl.ds(0, buf.shape[0])], buf, sem).wait()


def _gather_pipelined(step, nstep, dest_ref, dnext_ref, y_any, mbuf, sems):
    slot = step % 2

    @pl.when(step == 0)
    def _():
        _start_row_gather(dest_ref, y_any, mbuf.at[0], sems.at[0])

    _start_row_gather(dnext_ref, y_any, mbuf.at[1 - slot], sems.at[1 - slot])
    _wait_row_gather(y_any, mbuf.at[slot], sems.at[slot])

    def drain():
        @pl.when(step == nstep - 1)
        def _():
            _wait_row_gather(y_any, mbuf.at[1 - slot], sems.at[1 - slot])

    return mbuf.at[slot], drain


def _post_moe_kernel(dest_ref, dnext_ref, x1_ref, y_any, mod_ref, lng_ref, lnb_ref, out_ref,
                     mbuf, sems):
    m_ref, drain = _gather_pipelined(pl.program_id(0), pl.num_programs(0), dest_ref, dnext_ref,
                                     y_any, mbuf, sems)
    out_ref[...] = _block_input(x1_ref, m_ref, mod_ref[0], lng_ref, lnb_ref)
    drain()


def _post_moe_call(x1, dest, y_sorted, mod, lng, lnb, rows, steps_per_mod):
    mrows = mod.shape[1]
    nstep = x1.shape[0] // rows
    dest3 = dest.reshape(nstep, 1, rows)
    return pl.pallas_call(
        _post_moe_kernel,
        grid=(nstep,),
        in_specs=[
            pl.BlockSpec((1, 1, rows), lambda i: (i, 0, 0), memory_space=pltpu.SMEM),
            pl.BlockSpec((1, 1, rows), lambda i: (jnp.minimum(i + 1, nstep - 1), 0, 0),
                         memory_space=pltpu.SMEM),
            pl.BlockSpec((rows, D_MODEL), lambda i: (i, 0)),
            pl.BlockSpec(memory_space=pl.ANY),
            pl.BlockSpec((1, mrows, 3 * D_MODEL), lambda i: (i // steps_per_mod, 0, 0)),
            _full((1, D_MODEL)),
            _full((1, D_MODEL)),
        ],
        out_specs=pl.BlockSpec((rows, D_MODEL), lambda i: (i, 0)),
        out_shape=jax.ShapeDtypeStruct(x1.shape, F32),
        scratch_shapes=[pltpu.VMEM((2, rows * CHUNKS, LANES), F32), pltpu.SemaphoreType.DMA((2,))],
        compiler_params=_params("arbitrary"),
        name="post_moe",
    )(dest3, dest3, x1, y_sorted, mod, lng, lnb)


PLAN_CHUNK = 2048


def _num_tiles(nslot):
    return -(-nslot // MOE_TILE) + N_BUCKETS


def _plan_kernel(rp_ref, rs_ref, cnt_ref, dest_ref, sched_ref, *, layer, ntile):
    tm = MOE_TILE
    lane = lax.broadcasted_iota(I32, (SUBLANES, LANES), 1)
    row = lax.broadcasted_iota(I32, (LANES, LANES), 0)
    col = lax.broadcasted_iota(I32, (LANES, LANES), 1)
    counts = jnp.where(lane < N_BUCKETS, jnp.broadcast_to(cnt_ref[...], (SUBLANES, LANES)), 0.0)
    tiles_b = jnp.floor((counts + float(tm - 1)) * (1.0 / tm))
    tile_end = _dot(tiles_b.astype(BF16), jnp.where(row <= col, 1.0, 0.0).astype(BF16))
    tile_start = tile_end - tiles_b
    start_b = tile_start.astype(BF16)

    def dest_of(route):
        n = route.shape[1]
        bucket, rank = route[R_BUCKET:R_BUCKET + 1, :], route[R_RANK:R_RANK + 1, :]
        b_iota = lax.broadcasted_iota(I32, (LANES, n), 0).astype(F32)
        onehot = jnp.where(b_iota == bucket, 1.0, 0.0).astype(BF16)
        start = _dot(start_b, onehot)[0:1, :]
        return (start * float(tm) + rank).astype(I32)

    ntok_p = rp_ref.shape[1]
    chunk = math.gcd(ntok_p, PLAN_CHUNK)
    for c0 in range(0, ntok_p, chunk):
        dest_ref[:, c0:c0 + chunk] = dest_of(rp_ref[:, c0:c0 + chunk])
    dest_ref[:, ntok_p:] = dest_of(rs_ref[...])

    tile = lane.astype(F32)
    used = jnp.sum(jnp.where(lane == N_BUCKETS - 1, tile_end, 0.0), axis=-1, keepdims=True)
    ti = jnp.minimum(tile, used - 1.0)
    end_col = jnp.transpose(jnp.broadcast_to(tile_end[0:1], (LANES, LANES)))
    start_col = jnp.transpose(jnp.broadcast_to(tile_start[0:1], (LANES, LANES)))
    ti_rows = jnp.broadcast_to(ti[0:1], (LANES, LANES))
    b_of = jnp.sum(jnp.where((row < N_BUCKETS) & (ti_rows >= end_col), 1.0, 0.0),
                   axis=0, keepdims=True)
    start_of = jnp.sum(jnp.where(row.astype(F32) == b_of, start_col, 0.0), axis=0, keepdims=True)
    grp = sum(jnp.where(b_of >= float(g * N_PAIRS), 1.0, 0.0) for g in range(1, MOE_GROUPS))
    pair = b_of - float(N_PAIRS) * grp
    ex_a = sum(jnp.where(pair == float(j), float(PAIR_A[j]), 0.0) for j in range(N_PAIRS))
    ex_b = sum(jnp.where(pair == float(j), float(PAIR_B[j]), 0.0) for j in range(N_PAIRS))
    first_expert = float(layer * N_EXPERTS) + float(MOE_EXPERTS) * grp
    first_lane = float(MOE_GROUPS) + float(MOE_EXPERTS) * grp
    end_of = jnp.sum(jnp.where(row.astype(F32) == b_of, end_col, 0.0), axis=0, keepdims=True)
    valid = jnp.where(tile[0:1] < used[0:1], 1.0, 0.0)
    first = valid * jnp.where(ti[0:1] == start_of, 1.0, 0.0)
    partial = jnp.where((tile[0:1] == end_of - 1.0) | (tile[0:1] >= used[0:1]), 1.0, 0.0)
    zero = jnp.where(tile[0:1] < float(ntile), partial, 0.0)
    e_a, e_b = first_expert + ex_a, first_expert + ex_b

    def at_next(v):
        v_col = jnp.transpose(jnp.broadcast_to(v, (LANES, LANES)))
        return jnp.sum(jnp.where(row.astype(F32) == end_of, v_col, 0.0), axis=0, keepdims=True)

    has_next = valid * jnp.where(end_of < used[0:1], 1.0, 0.0)
    rows = {S_BLK: ti[0:1], S_EA: e_a, S_EB: e_b, S_VALID: valid,
            S_FIRST: first, S_LANE_G: grp, S_LANE_A: first_lane + ex_a, S_LANE_B: first_lane + ex_b,
            S_ZERO: zero, S_NEA: at_next(e_a), S_NEB: at_next(e_b), S_HASNEXT: has_next}
    sub = lax.broadcasted_iota(I32, (SCHED_ROWS, LANES), 0)
    sched = sum(jnp.where(sub == r, jnp.broadcast_to(v, (SCHED_ROWS, LANES)), 0.0)
                for r, v in rows.items())
    sched_ref[...] = sched.astype(I32)


def _moe_plan(route_p, route_s, cnt, layer):
    nslot = route_p.shape[1] + route_s.shape[1]
    assert _num_tiles(nslot) <= LANES
    return pl.pallas_call(
        functools.partial(_plan_kernel, layer=layer, ntile=_num_tiles(nslot)),
        out_shape=[jax.ShapeDtypeStruct((1, nslot), I32),
                   jax.ShapeDtypeStruct((SCHED_ROWS, LANES), I32)],
        compiler_params=pltpu.CompilerParams(vmem_limit_bytes=VMEM_LIMIT),
        name="plan",
    )(route_p, route_s, cnt)


def _router_weights(w_coarse, b_coarse, w_fine, b_fine):
    wf = jnp.transpose(w_fine, (1, 0, 2)).reshape(D_MODEL, N_EXPERTS)
    w = jnp.concatenate([w_coarse, wf], axis=1)
    w = jnp.pad(w, ((0, 0), (0, LANES - w.shape[1])))
    b = jnp.concatenate([b_coarse, b_fine.reshape(N_EXPERTS)])
    b = jnp.pad(b, (0, LANES - b.shape[0])).reshape(1, LANES)
    hi_lo = _hi_lo(w)
    return hi_lo[0], hi_lo[1], b


def _prep_params(ln_g, ln_b, w_in_even, w_a2, b_a, gla_norm_g, conv_w, conv_b, conv_ln_g,
                 conv_ln_b, w_out_even, w_grp_pool, pool_scale, w_out_odd, w_coarse, b_coarse,
                 w_fine, b_fine):
    p = {}
    w_in = w_in_even[0]
    o_q, o_k, o_v, o_g = 0, GLA_KW, 2 * GLA_KW, 2 * GLA_KW + GLA_VW
    o_a = o_g + GLA_VW
    o_u = o_a + GLA_RANK
    w_in_r = jnp.concatenate(
        [w_in[:, o_q:o_a], w_in[:, o_u:o_u + 2 * CONV_CH], w_in[:, o_a:o_u],
         jnp.zeros((D_MODEL, A_PAD - GLA_RANK), F32)], axis=1)
    p["w_in"] = _hi_lo(w_in_r)
    p["w_a2"] = _hi_lo(jnp.pad(w_a2[0], ((0, A_PAD - GLA_RANK), (0, 0))))
    p["b_a"] = b_a[0].reshape(1, GLA_KW)
    p["gng"] = gla_norm_g[0].reshape(1, GLA_VW)
    p["conv_w"] = jnp.pad(conv_w[0], ((0, CONV_W_ROWS - CONV_WIDTH), (0, 0)))
    p["conv_b"] = conv_b[0].reshape(1, CONV_CH)
    p["conv_ln_g"] = conv_ln_g[0].reshape(1, CONV_CH)
    p["conv_ln_b"] = conv_ln_b[0].reshape(1, CONV_CH)
    p["w_out0"] = _hi_lo(w_out_even[0])
    p["w_grp"] = w_grp_pool[0].astype(BF16)
    p["pool_scale"] = pool_scale[0].reshape(1, D_MODEL)
    p["w_out1"] = w_out_odd[0].astype(BF16)
    for layer in range(DEPTH):
        for j in range(2):
            p[f"ln_g{layer}{j}"] = ln_g[layer, j].reshape(1, D_MODEL)
            p[f"ln_b{layer}{j}"] = ln_b[layer, j].reshape(1, D_MODEL)
        hi, lo, b = _router_weights(w_coarse[layer], b_coarse[layer], w_fine[layer], b_fine[layer])
        p[f"wr_hi{layer}"], p[f"wr_lo{layer}"], p[f"br{layer}"] = hi, lo, b
    return p


def kernel(x_prompt, x_sample, state_gla, state_conv, state_pool, c_prompt, c_sample, w_ada, b_ada,
           ln_g, ln_b, w_in_even, w_a2, b_a, gla_norm_g, conv_w, conv_b, conv_ln_g, conv_ln_b,
           w_out_even, w_grp_pool, pool_scale, w_out_odd, w_coarse, b_coarse, w_fine, b_fine,
           w_gate, w_up, w_down):
    bsz, seq, _ = x_prompt.shape
    nsmp = x_sample.shape[0]
    ntok_p = bsz * seq
    assert seq % TILE_L == 0 and nsmp <= TILE_L and nsmp % SAMPLE_BLK == 0 and ntok_p % nsmp == 0
    p = _prep_params(ln_g, ln_b, w_in_even, w_a2, b_a, gla_norm_g, conv_w, conv_b, conv_ln_g,
                     conv_ln_b, w_out_even, w_grp_pool, pool_scale, w_out_odd, w_coarse, b_coarse,
                     w_fine, b_fine)
    wg = w_gate.reshape(DEPTH * N_EXPERTS, D_MODEL, MOE_FF)
    wu = w_up.reshape(DEPTH * N_EXPERTS, D_MODEL, MOE_FF)
    wd = w_down.reshape(DEPTH * N_EXPERTS, MOE_FF, D_MODEL)

    mod_p, mod_s = _ada_call(c_prompt, c_sample, w_ada.reshape(2 * DEPTH, D_MODEL, 3 * D_MODEL),
                             b_ada.reshape(2 * DEPTH, 1, 3 * D_MODEL))
    mod_p = mod_p.reshape(2 * DEPTH, bsz, 1, 3 * D_MODEL)

    def moe(h2c_all, route_p, route_s, cnt, layer):
        dest, sched = _moe_plan(route_p, route_s, cnt, layer)
        sorted_x = _dispatch_call(sched, dest, h2c_all)
        y_sorted = _moe_call(sched, sorted_x, p[f"wr_hi{layer}"], p[f"wr_lo{layer}"],
                             p[f"br{layer}"], wg, wu, wd)
        return dest[0, :ntok_p], dest[0, ntok_p:ntok_p + nsmp], y_sorted

    xs0 = x_sample.reshape(nsmp, D_MODEL)
    cnt0 = jnp.zeros((1, LANES), F32)
    x1s, h2c_s, route_s, cnt, gla_s, conv_s = _mixer0_sample_call(
        xs0, mod_s[0], mod_s[1], state_gla, state_conv, cnt0, p)
    x1p, h2c_all, route_p, cnt, conv_p, gla_p = _mixer0_prompt_call(
        x_prompt, mod_p[0], mod_p[1], cnt, h2c_s, p)
    dest_p, dest_s, y_sorted = moe(h2c_all, route_p, route_s, cnt, 0)
    x3s, h2c_s, route_s, cnt, pool_s = _mixer1_sample_call(
        x1s, dest_s, y_sorted, mod_s[1], p["ln_g01"], p["ln_b01"], mod_s[2], mod_s[3],
        state_pool, cnt0, p)
    x3p, h2c_all, route_p, cnt, pool_p = _mixer1_prompt_call(
        x1p, dest_p, y_sorted, mod_p[1], p["ln_g01"], p["ln_b01"], mod_p[2], mod_p[3], cnt,
        h2c_s, bsz, p)
    dest_p, dest_s, y_sorted = moe(h2c_all, route_p, route_s, cnt, 1)
    x4p = _post_moe_call(x3p, dest_p, y_sorted, mod_p[3], p["ln_g11"], p["ln_b11"], TILE_L,
                         seq // TILE_L).reshape(bsz, seq, D_MODEL)
    x4s = _post_moe_call(x3s, dest_s, y_sorted, mod_s[3][None], p["ln_g11"], p["ln_b11"], nsmp, 1)
    return (x4p, x4s.reshape(nsmp, 1, D_MODEL), gla_p, conv_p, pool_p, gla_s, conv_s, pool_s)
```
